```python
import jax, jax.numpy as jnp
from jax import lax
import numpy as np

D_MODEL = 1024
BATCH = 16
SEQ = 2048
DEPTH = 1

CHUNK = 64
Q_BLOCK = 128
EPS = 1e-6
GLA_HEADS = 4
GLA_DK = 128
GLA_DV = 256
GLA_LOWRANK = 16
GLA_TAU = 16.0
MLA_HEADS = 16
MLA_Q_RANK = 256
MLA_KV_RANK = 128
MLA_NOPE = 64
MLA_ROPE = 32
MLA_V = 64
ROPE_THETA = 10000.0
D_FF = 4 * D_MODEL
N_BRANCH = 2
IN_SPLITS = (GLA_HEADS * GLA_DK, GLA_HEADS * GLA_DK, GLA_HEADS * GLA_DV, GLA_HEADS * GLA_DV,
             GLA_LOWRANK, MLA_Q_RANK, MLA_KV_RANK, MLA_ROPE, N_BRANCH * D_MODEL)
IN_WIDTH = sum(IN_SPLITS)

kernel_name = "hybrid_gla_mla_sqrelu_adaln_block"


def rms_norm(x, g):
    xf = x.astype(jnp.float32)
    y = xf * lax.rsqrt(jnp.mean(xf * xf, axis=-1, keepdims=True) + EPS)
    return (y * g.astype(jnp.float32)).astype(x.dtype)


def modulate(h, shift, scale):
    return h * (1.0 + scale[:, None, :]) + shift[:, None, :]


def rope(x, positions):
    r = x.shape[-1]
    freqs = ROPE_THETA ** (-jnp.arange(0, r, 2, dtype=jnp.float32) / r)
    ang = positions.astype(jnp.float32)[..., None] * freqs
    cos = jnp.cos(ang)[:, :, None, :]
    sin = jnp.sin(ang)[:, :, None, :]
    xf = x.astype(jnp.float32)
    x1, x2 = xf[..., : r // 2], xf[..., r // 2:]
    return jnp.concatenate([x1 * cos - x2 * sin, x2 * cos + x1 * sin], axis=-1).astype(x.dtype)


def gla_branch(q, k, v, g, a_lr, w_alpha, b_alpha, out_norm_g, w_o):
    b, s, _ = q.shape
    nc = s // CHUNK
    qc = q.reshape(b, nc, CHUNK, GLA_HEADS, GLA_DK) * (GLA_DK ** -0.5)
    kc = k.reshape(b, nc, CHUNK, GLA_HEADS, GLA_DK)
    vc = v.reshape(b, nc, CHUNK, GLA_HEADS, GLA_DV)
    log_a = jax.nn.log_sigmoid((a_lr @ w_alpha + b_alpha).astype(jnp.float32)) / GLA_TAU
    log_a = log_a.reshape(b, nc, CHUNK, GLA_HEADS, GLA_DK)
    cum = jnp.cumsum(log_a, axis=2)
    cum_end = cum[:, :, -1]
    k_dec = kc.astype(jnp.float32) * jnp.exp(cum_end[:, :, None] - cum)
    u = jnp.einsum('bnchk,bnchv->nbhkv', k_dec, vc.astype(jnp.float32))
    decay = jnp.transpose(jnp.exp(cum_end), (1, 0, 2, 3))

    def step(state, inp):
        d, uc = inp
        state = d[..., None] * state + uc
        return state, state

    s0 = jnp.zeros((b, GLA_HEADS, GLA_DK, GLA_DV), jnp.float32)
    _, states = lax.scan(step, s0, (decay, u))
    o = jnp.einsum('bnchk,nbhkv->bnchv', qc.astype(jnp.float32), states).astype(q.dtype)
    o = o.reshape(b, s, GLA_HEADS, GLA_DV)
    o = rms_norm(o, out_norm_g) * jax.nn.silu(g.reshape(b, s, GLA_HEADS, GLA_DV))
    return o.reshape(b, s, GLA_HEADS * GLA_DV) @ w_o


def chunk_causal_attention(q, k, v):
    b, s, h, dqk = q.shape
    dv = v.shape[-1]
    nb = s // Q_BLOCK
    scale = dqk ** -0.5
    qb = jnp.transpose(q.reshape(b, nb, Q_BLOCK, h, dqk), (1, 0, 3, 2, 4))
    key_chunk = jnp.arange(s) // CHUNK

    def one_block(args):
        qi, bi = args
        sc = jnp.einsum('bhqd,bkhd->bhqk', qi, k).astype(jnp.float32) * scale
        q_chunk = (bi * Q_BLOCK + jnp.arange(Q_BLOCK)) // CHUNK
        mask = key_chunk[None, :] <= q_chunk[:, None]
        sc = jnp.where(mask[None, None], sc, -jnp.inf)
        p = jax.nn.softmax(sc, axis=-1).astype(v.dtype)
        return jnp.einsum('bhqk,bkhd->bqhd', p, v)

    out = lax.map(one_block, (qb, jnp.arange(nb)))
    return jnp.transpose(out, (1, 0, 2, 3, 4)).reshape(b, s, h, dv)


def mla_branch(cq, ckv, kpe, positions, q_lat_g, w_uq, kv_lat_g, w_ukv, qn_g, kn_g, w_o):
    b, s, _ = cq.shape
    q = (rms_norm(cq, q_lat_g) @ w_uq).reshape(b, s, MLA_HEADS, MLA_NOPE + MLA_ROPE)
    kv = (rms_norm(ckv, kv_lat_g) @ w_ukv).reshape(b, s, MLA_HEADS, MLA_NOPE + MLA_V)
    k_nope, v = kv[..., :MLA_NOPE], kv[..., MLA_NOPE:]
    k_rope = jnp.broadcast_to(kpe[:, :, None, :], (b, s, MLA_HEADS, MLA_ROPE))
    k = jnp.concatenate([k_nope, k_rope], axis=-1)
    q = rms_norm(q, qn_g)
    k = rms_norm(k, kn_g)
    q = jnp.concatenate([q[..., :MLA_NOPE], rope(q[..., MLA_NOPE:], positions)], axis=-1)
    k = jnp.concatenate([k[..., :MLA_NOPE], rope(k[..., MLA_NOPE:], positions)], axis=-1)
    o = chunk_causal_attention(q, k, v)
    return o.reshape(b, s, MLA_HEADS * MLA_V) @ w_o


def _fwd_setup_inputs(seed: int = 0) -> dict:
    key = jax.random.key(seed)
    ks = jax.random.split(key, 24)
    f32 = jnp.float32

    def nrm(k, shape, scale):
        return jax.random.normal(k, shape, f32) * scale

    def gain(k, dim):
        return 1.0 + 0.02 * jax.random.normal(k, (DEPTH, dim), f32)

    L = DEPTH
    offsets = jax.random.randint(ks[2], (BATCH, 1), 0, 4096, dtype=jnp.int32)
    positions = offsets + jnp.arange(SEQ, dtype=jnp.int32)[None, :]
    return {
        "x": nrm(ks[0], (BATCH, SEQ, D_MODEL), 1.0),
        "c": nrm(ks[1], (BATCH, D_MODEL), 1.0),
        "positions": positions,
        "w_ada": nrm(ks[3], (L, D_MODEL, 6 * D_MODEL), 0.5 * D_MODEL ** -0.5),
        "b_ada": nrm(ks[4], (L, 6 * D_MODEL), 0.02),
        "norm1_g": gain(ks[5], D_MODEL),
        "w_in": nrm(ks[6], (L, D_MODEL, IN_WIDTH), D_MODEL ** -0.5),
        "b_merge": nrm(ks[7], (L, N_BRANCH * D_MODEL), 0.02),
        "gla_w_alpha": nrm(ks[8], (L, GLA_LOWRANK, GLA_HEADS * GLA_DK), GLA_LOWRANK ** -0.5),
        "gla_b_alpha": nrm(ks[9], (L, GLA_HEADS * GLA_DK), 0.1),
        "gla_out_norm_g": gain(ks[10], GLA_DV),
        "gla_w_o": nrm(ks[11], (L, GLA_HEADS * GLA_DV, D_MODEL), (GLA_HEADS * GLA_DV) ** -0.5),
        "mla_q_lat_g": gain(ks[12], MLA_Q_RANK),
        "mla_w_uq": nrm(ks[13], (L, MLA_Q_RANK, MLA_HEADS * (MLA_NOPE + MLA_ROPE)), MLA_Q_RANK ** -0.5),
        "mla_kv_lat_g": gain(ks[14], MLA_KV_RANK),
        "mla_w_ukv": nrm(ks[15], (L, MLA_KV_RANK, MLA_HEADS * (MLA_NOPE + MLA_V)), MLA_KV_RANK ** -0.5),
        "mla_qn_g": gain(ks[16], MLA_NOPE + MLA_ROPE),
        "mla_kn_g": gain(ks[17], MLA_NOPE + MLA_ROPE),
        "mla_w_o": nrm(ks[18], (L, MLA_HEADS * MLA_V, D_MODEL), (MLA_HEADS * MLA_V) ** -0.5),
        "w_out": nrm(ks[19], (L, D_MODEL, D_MODEL), D_MODEL ** -0.5),
        "norm2_g": gain(ks[20], D_MODEL),
        "mlp_w1": nrm(ks[21], (L, D_MODEL, D_FF), D_MODEL ** -0.5),
        "mlp_w2": nrm(ks[22], (L, D_FF, D_MODEL), D_FF ** -0.5),
    }


def _fwd_reference(x, c, positions, w_ada, b_ada, norm1_g, w_in, b_merge, gla_w_alpha, gla_b_alpha,
              gla_out_norm_g, gla_w_o, mla_q_lat_g, mla_w_uq, mla_kv_lat_g, mla_w_ukv,
              mla_qn_g, mla_kn_g, mla_w_o, w_out, norm2_g, mlp_w1, mlp_w2):
    split_at = np.cumsum(IN_SPLITS)[:-1].tolist()
    c_act = jax.nn.silu(c)
    for l in range(DEPTH):
        mod = c_act @ w_ada[l] + b_ada[l]
        shift1, scale1, gate1, shift2, scale2, gate2 = jnp.split(mod, 6, axis=-1)

        h = modulate(rms_norm(x, norm1_g[l]), shift1, scale1)
        proj = h @ w_in[l]
        g_q, g_k, g_v, g_g, g_a, m_cq, m_ckv, m_kpe, merge_logits = jnp.split(proj, split_at, axis=-1)
        y_a = gla_branch(g_q, g_k, g_v, g_g, g_a, gla_w_alpha[l], gla_b_alpha[l],
                         gla_out_norm_g[l], gla_w_o[l])
        y_b = mla_branch(m_cq, m_ckv, m_kpe, positions, mla_q_lat_g[l], mla_w_uq[l],
                         mla_kv_lat_g[l], mla_w_ukv[l], mla_qn_g[l], mla_kn_g[l], mla_w_o[l])
        gates = jax.nn.sigmoid(merge_logits + b_merge[l])
        gate_a, gate_b = gates[..., :D_MODEL], gates[..., D_MODEL:]
        mixed = (gate_a * y_a + gate_b * y_b) @ w_out[l]
        x = x + gate1[:, None, :] * mixed

        h2 = modulate(rms_norm(x, norm2_g[l]), shift2, scale2)
        ff = jnp.square(jax.nn.relu(h2 @ mlp_w1[l])) @ mlp_w2[l]
        x = x + gate2[:, None, :] * ff
    return x


import jax as _jax
import jax.numpy as _jnp

TWIN_FORMAT = 'train_step'
FWD_PARAMS = ['x', 'c', 'positions', 'w_ada', 'b_ada', 'norm1_g', 'w_in', 'b_merge', 'gla_w_alpha', 'gla_b_alpha', 'gla_out_norm_g', 'gla_w_o', 'mla_q_lat_g', 'mla_w_uq', 'mla_kv_lat_g', 'mla_w_ukv', 'mla_qn_g', 'mla_kn_g', 'mla_w_o', 'w_out', 'norm2_g', 'mlp_w1', 'mlp_w2']
TWIN_WEIGHTS = ['w_ada', 'b_ada', 'norm1_g', 'w_in', 'b_merge', 'gla_w_alpha', 'gla_b_alpha', 'gla_out_norm_g', 'gla_w_o', 'mla_q_lat_g', 'mla_w_uq', 'mla_kv_lat_g', 'mla_w_ukv', 'mla_qn_g', 'mla_kn_g', 'mla_w_o', 'w_out', 'norm2_g', 'mlp_w1', 'mlp_w2']
TWIN_DIFF_INPUT = 'x'
TWIN_INPUTS = ['x', 'c', 'positions', 'w_ada', 'b_ada', 'norm1_g', 'w_in', 'b_merge', 'gla_w_alpha', 'gla_b_alpha', 'gla_out_norm_g', 'gla_w_o', 'mla_q_lat_g', 'mla_w_uq', 'mla_kv_lat_g', 'mla_w_ukv', 'mla_qn_g', 'mla_kn_g', 'mla_w_o', 'w_out', 'norm2_g', 'mlp_w1', 'mlp_w2', 'loss_target', 'm_w_ada', 'm_b_ada', 'm_norm1_g', 'm_w_in', 'm_b_merge', 'm_gla_w_alpha', 'm_gla_b_alpha', 'm_gla_out_norm_g', 'm_gla_w_o', 'm_mla_q_lat_g', 'm_mla_w_uq', 'm_mla_kv_lat_g', 'm_mla_w_ukv', 'm_mla_qn_g', 'm_mla_kn_g', 'm_mla_w_o', 'm_w_out', 'm_norm2_g', 'm_mlp_w1', 'm_mlp_w2', 'v_w_ada', 'v_b_ada', 'v_norm1_g', 'v_w_in', 'v_b_merge', 'v_gla_w_alpha', 'v_gla_b_alpha', 'v_gla_out_norm_g', 'v_gla_w_o', 'v_mla_q_lat_g', 'v_mla_w_uq', 'v_mla_kv_lat_g', 'v_mla_w_ukv', 'v_mla_qn_g', 'v_mla_kn_g', 'v_mla_w_o', 'v_w_out', 'v_norm2_g', 'v_mlp_w1', 'v_mlp_w2']
TWIN_OUTPUTS = ['loss', 'grad_x', 'grad_w_ada', 'grad_b_ada', 'grad_norm1_g', 'grad_w_in', 'grad_b_merge', 'grad_gla_w_alpha', 'grad_gla_b_alpha', 'grad_gla_out_norm_g', 'grad_gla_w_o', 'grad_mla_q_lat_g', 'grad_mla_w_uq', 'grad_mla_kv_lat_g', 'grad_mla_w_ukv', 'grad_mla_qn_g', 'grad_mla_kn_g', 'grad_mla_w_o', 'grad_w_out', 'grad_norm2_g', 'grad_mlp_w1', 'grad_mlp_w2', 'delta_w_ada', 'delta_b_ada', 'delta_norm1_g', 'delta_w_in', 'delta_b_merge', 'delta_gla_w_alpha', 'delta_gla_b_alpha', 'delta_gla_out_norm_g', 'delta_gla_w_o', 'delta_mla_q_lat_g', 'delta_mla_w_uq', 'delta_mla_kv_lat_g', 'delta_mla_w_ukv', 'delta_mla_qn_g', 'delta_mla_kn_g', 'delta_mla_w_o', 'delta_w_out', 'delta_norm2_g', 'delta_mlp_w1', 'delta_mlp_w2', 'new_m_w_ada', 'new_m_b_ada', 'new_m_norm1_g', 'new_m_w_in', 'new_m_b_merge', 'new_m_gla_w_alpha', 'new_m_gla_b_alpha', 'new_m_gla_out_norm_g', 'new_m_gla_w_o', 'new_m_mla_q_lat_g', 'new_m_mla_w_uq', 'new_m_mla_kv_lat_g', 'new_m_mla_w_ukv', 'new_m_mla_qn_g', 'new_m_mla_kn_g', 'new_m_mla_w_o', 'new_m_w_out', 'new_m_norm2_g', 'new_m_mlp_w1', 'new_m_mlp_w2', 'new_v_w_ada', 'new_v_b_ada', 'new_v_norm1_g', 'new_v_w_in', 'new_v_b_merge', 'new_v_gla_w_alpha', 'new_v_gla_b_alpha', 'new_v_gla_out_norm_g', 'new_v_gla_w_o', 'new_v_mla_q_lat_g', 'new_v_mla_w_uq', 'new_v_mla_kv_lat_g', 'new_v_mla_w_ukv', 'new_v_mla_qn_g', 'new_v_mla_kn_g', 'new_v_mla_w_o', 'new_v_w_out', 'new_v_norm2_g', 'new_v_mlp_w1', 'new_v_mlp_w2']
TWIN_LEAF_KINDS = {'loss': 'loss', 'grad_x': 'grad_x', 'grad_w_ada': 'grad_w', 'grad_b_ada': 'grad_w', 'grad_norm1_g': 'grad_w', 'grad_w_in': 'grad_w', 'grad_b_merge': 'grad_w', 'grad_gla_w_alpha': 'grad_w', 'grad_gla_b_alpha': 'grad_w', 'grad_gla_out_norm_g': 'grad_w', 'grad_gla_w_o': 'grad_w', 'grad_mla_q_lat_g': 'grad_w', 'grad_mla_w_uq': 'grad_w', 'grad_mla_kv_lat_g': 'grad_w', 'grad_mla_w_ukv': 'grad_w', 'grad_mla_qn_g': 'grad_w', 'grad_mla_kn_g': 'grad_w', 'grad_mla_w_o': 'grad_w', 'grad_w_out': 'grad_w', 'grad_norm2_g': 'grad_w', 'grad_mlp_w1': 'grad_w', 'grad_mlp_w2': 'grad_w', 'delta_w_ada': 'delta_w', 'delta_b_ada': 'delta_w', 'delta_norm1_g': 'delta_w', 'delta_w_in': 'delta_w', 'delta_b_merge': 'delta_w', 'delta_gla_w_alpha': 'delta_w', 'delta_gla_b_alpha': 'delta_w', 'delta_gla_out_norm_g': 'delta_w', 'delta_gla_w_o': 'delta_w', 'delta_mla_q_lat_g': 'delta_w', 'delta_mla_w_uq': 'delta_w', 'delta_mla_kv_lat_g': 'delta_w', 'delta_mla_w_ukv': 'delta_w', 'delta_mla_qn_g': 'delta_w', 'delta_mla_kn_g': 'delta_w', 'delta_mla_w_o': 'delta_w', 'delta_w_out': 'delta_w', 'delta_norm2_g': 'delta_w', 'delta_mlp_w1': 'delta_w', 'delta_mlp_w2': 'delta_w', 'new_m_w_ada': 'new_m', 'new_m_b_ada': 'new_m', 'new_m_norm1_g': 'new_m', 'new_m_w_in': 'new_m', 'new_m_b_merge': 'new_m', 'new_m_gla_w_alpha': 'new_m', 'new_m_gla_b_alpha': 'new_m', 'new_m_gla_out_norm_g': 'new_m', 'new_m_gla_w_o': 'new_m', 'new_m_mla_q_lat_g': 'new_m', 'new_m_mla_w_uq': 'new_m', 'new_m_mla_kv_lat_g': 'new_m', 'new_m_mla_w_ukv': 'new_m', 'new_m_mla_qn_g': 'new_m', 'new_m_mla_kn_g': 'new_m', 'new_m_mla_w_o': 'new_m', 'new_m_w_out': 'new_m', 'new_m_norm2_g': 'new_m', 'new_m_mlp_w1': 'new_m', 'new_m_mlp_w2': 'new_m', 'new_v_w_ada': 'new_v', 'new_v_b_ada': 'new_v', 'new_v_norm1_g': 'new_v', 'new_v_w_in': 'new_v', 'new_v_b_merge': 'new_v', 'new_v_gla_w_alpha': 'new_v', 'new_v_gla_b_alpha': 'new_v', 'new_v_gla_out_norm_g': 'new_v', 'new_v_gla_w_o': 'new_v', 'new_v_mla_q_lat_g': 'new_v', 'new_v_mla_w_uq': 'new_v', 'new_v_mla_kv_lat_g': 'new_v', 'new_v_mla_w_ukv': 'new_v', 'new_v_mla_qn_g': 'new_v', 'new_v_mla_kn_g': 'new_v', 'new_v_mla_w_o': 'new_v', 'new_v_w_out': 'new_v', 'new_v_norm2_g': 'new_v', 'new_v_mlp_w1': 'new_v', 'new_v_mlp_w2': 'new_v'}


def _forward(args):
    return _fwd_reference(*[args[k] for k in FWD_PARAMS])


def _output_shape():
    out = _jax.eval_shape(lambda: _forward(_fwd_setup_inputs(0)))
    return out.shape, out.dtype

N_MICROBATCH = 1
ADAM_LR = 0.001
ADAM_B1 = 0.9
ADAM_B2 = 0.999
ADAM_EPS = 1e-08
ADAM_WD = 0.01
ADAM_STEP = 10
PER_EXAMPLE_BATCH_AXIS = {'x': 0, 'c': 0, 'positions': 0, 'loss_target': 0}
SHARED_INPUTS = []
_WEIGHT_DTYPES = {'w_ada': _jnp.float32, 'b_ada': _jnp.float32, 'norm1_g': _jnp.float32, 'w_in': _jnp.float32, 'b_merge': _jnp.float32, 'gla_w_alpha': _jnp.float32, 'gla_b_alpha': _jnp.float32, 'gla_out_norm_g': _jnp.float32, 'gla_w_o': _jnp.float32, 'mla_q_lat_g': _jnp.float32, 'mla_w_uq': _jnp.float32, 'mla_kv_lat_g': _jnp.float32, 'mla_w_ukv': _jnp.float32, 'mla_qn_g': _jnp.float32, 'mla_kn_g': _jnp.float32, 'mla_w_o': _jnp.float32, 'w_out': _jnp.float32, 'norm2_g': _jnp.float32, 'mlp_w1': _jnp.float32, 'mlp_w2': _jnp.float32}
MOMENT_SCALE = {'w_ada': 3.431970e+00, 'b_ada': 7.334199e+00, 'norm1_g': 3.604590e-01, 'w_in': 8.212571e-02, 'b_merge': 1.103429e-01, 'gla_w_alpha': 1.051063e-02, 'gla_b_alpha': 2.830717e-02, 'gla_out_norm_g': 1.698089e+00, 'gla_w_o': 4.086497e-02, 'mla_q_lat_g': 1.852204e-02, 'mla_w_uq': 7.008715e-03, 'mla_kv_lat_g': 1.194329e+00, 'mla_w_ukv': 1.468251e-01, 'mla_qn_g': 4.674819e-02, 'mla_kn_g': 4.762609e-02, 'mla_w_o': 1.989667e-01, 'w_out': 1.645254e-01, 'norm2_g': 1.284853e+01, 'mlp_w1': 3.563250e-01, 'mlp_w2': 1.415754e+00}


def _to_microbatches(a, axis):
    t = _jnp.moveaxis(a, axis, 0)
    t = t.reshape((N_MICROBATCH, t.shape[0] // N_MICROBATCH) + t.shape[1:])
    return _jnp.moveaxis(t, 1, axis + 1)


def setup_inputs(seed: int = 0) -> dict:
    inp = _fwd_setup_inputs(seed)
    key = _jax.random.fold_in(_jax.random.key(seed), 7919)
    shape, _ = _output_shape()
    out = dict(inp)
    out["loss_target"] = _jax.random.normal(_jax.random.fold_in(key, 0), shape, _jnp.float32)
    for i, name in enumerate(TWIN_WEIGHTS):
        w = inp[name].astype(_jnp.float32)
        if MOMENT_SCALE is None:
            s = _jnp.sqrt(_jnp.mean(_jnp.square(w)) + 1e-30)
        else:
            s = MOMENT_SCALE[name]
        km, kv = _jax.random.split(_jax.random.fold_in(key, i + 1))
        out[name] = w
        out["m_" + name] = s * _jax.random.normal(km, w.shape, _jnp.float32)
        out["v_" + name] = (s * s) * _jax.random.uniform(kv, w.shape, _jnp.float32, 0.5, 1.5)
    if N_MICROBATCH > 1:
        for name, axis in PER_EXAMPLE_BATCH_AXIS.items():
            out[name] = _to_microbatches(out[name], axis)
    return {'x': out['x'], 'c': out['c'], 'positions': out['positions'], 'w_ada': out['w_ada'], 'b_ada': out['b_ada'], 'norm1_g': out['norm1_g'], 'w_in': out['w_in'], 'b_merge': out['b_merge'], 'gla_w_alpha': out['gla_w_alpha'], 'gla_b_alpha': out['gla_b_alpha'], 'gla_out_norm_g': out['gla_out_norm_g'], 'gla_w_o': out['gla_w_o'], 'mla_q_lat_g': out['mla_q_lat_g'], 'mla_w_uq': out['mla_w_uq'], 'mla_kv_lat_g': out['mla_kv_lat_g'], 'mla_w_ukv': out['mla_w_ukv'], 'mla_qn_g': out['mla_qn_g'], 'mla_kn_g': out['mla_kn_g'], 'mla_w_o': out['mla_w_o'], 'w_out': out['w_out'], 'norm2_g': out['norm2_g'], 'mlp_w1': out['mlp_w1'], 'mlp_w2': out['mlp_w2'], 'loss_target': out['loss_target'], 'm_w_ada': out['m_w_ada'], 'm_b_ada': out['m_b_ada'], 'm_norm1_g': out['m_norm1_g'], 'm_w_in': out['m_w_in'], 'm_b_merge': out['m_b_merge'], 'm_gla_w_alpha': out['m_gla_w_alpha'], 'm_gla_b_alpha': out['m_gla_b_alpha'], 'm_gla_out_norm_g': out['m_gla_out_norm_g'], 'm_gla_w_o': out['m_gla_w_o'], 'm_mla_q_lat_g': out['m_mla_q_lat_g'], 'm_mla_w_uq': out['m_mla_w_uq'], 'm_mla_kv_lat_g': out['m_mla_kv_lat_g'], 'm_mla_w_ukv': out['m_mla_w_ukv'], 'm_mla_qn_g': out['m_mla_qn_g'], 'm_mla_kn_g': out['m_mla_kn_g'], 'm_mla_w_o': out['m_mla_w_o'], 'm_w_out': out['m_w_out'], 'm_norm2_g': out['m_norm2_g'], 'm_mlp_w1': out['m_mlp_w1'], 'm_mlp_w2': out['m_mlp_w2'], 'v_w_ada': out['v_w_ada'], 'v_b_ada': out['v_b_ada'], 'v_norm1_g': out['v_norm1_g'], 'v_w_in': out['v_w_in'], 'v_b_merge': out['v_b_merge'], 'v_gla_w_alpha': out['v_gla_w_alpha'], 'v_gla_b_alpha': out['v_gla_b_alpha'], 'v_gla_out_norm_g': out['v_gla_out_norm_g'], 'v_gla_w_o': out['v_gla_w_o'], 'v_mla_q_lat_g': out['v_mla_q_lat_g'], 'v_mla_w_uq': out['v_mla_w_uq'], 'v_mla_kv_lat_g': out['v_mla_kv_lat_g'], 'v_mla_w_ukv': out['v_mla_w_ukv'], 'v_mla_qn_g': out['v_mla_qn_g'], 'v_mla_kn_g': out['v_mla_kn_g'], 'v_mla_w_o': out['v_mla_w_o'], 'v_w_out': out['v_w_out'], 'v_norm2_g': out['v_norm2_g'], 'v_mlp_w1': out['v_mlp_w1'], 'v_mlp_w2': out['v_mlp_w2']}


def _loss(weights, diff, rest, loss_target):
    with _jax.named_scope("forward"):
        args = {**rest, TWIN_DIFF_INPUT: diff, **{k: w.astype(_WEIGHT_DTYPES[k]) for k, w in weights.items()}}
        y = _forward(args)
    with _jax.named_scope("loss_head"):
        err = _jnp.square(y.astype(_jnp.float32) - loss_target)
        return 0.5 * _jnp.sum(_jnp.mean(err, axis=-1)) if err.ndim else 0.5 * err


def _adamw(w, g, m, v):
    m = ADAM_B1 * m + (1.0 - ADAM_B1) * g
    v = ADAM_B2 * v + (1.0 - ADAM_B2) * _jnp.square(g)
    m_hat = m / (1.0 - ADAM_B1 ** ADAM_STEP)
    v_hat = v / (1.0 - ADAM_B2 ** ADAM_STEP)
    delta = -ADAM_LR * (m_hat / (_jnp.sqrt(v_hat) + ADAM_EPS) + ADAM_WD * w)
    return delta, m, v


def reference(x, c, positions, w_ada, b_ada, norm1_g, w_in, b_merge, gla_w_alpha, gla_b_alpha, gla_out_norm_g, gla_w_o, mla_q_lat_g, mla_w_uq, mla_kv_lat_g, mla_w_ukv, mla_qn_g, mla_kn_g, mla_w_o, w_out, norm2_g, mlp_w1, mlp_w2, loss_target, m_w_ada, m_b_ada, m_norm1_g, m_w_in, m_b_merge, m_gla_w_alpha, m_gla_b_alpha, m_gla_out_norm_g, m_gla_w_o, m_mla_q_lat_g, m_mla_w_uq, m_mla_kv_lat_g, m_mla_w_ukv, m_mla_qn_g, m_mla_kn_g, m_mla_w_o, m_w_out, m_norm2_g, m_mlp_w1, m_mlp_w2, v_w_ada, v_b_ada, v_norm1_g, v_w_in, v_b_merge, v_gla_w_alpha, v_gla_b_alpha, v_gla_out_norm_g, v_gla_w_o, v_mla_q_lat_g, v_mla_w_uq, v_mla_kv_lat_g, v_mla_w_ukv, v_mla_qn_g, v_mla_kn_g, v_mla_w_o, v_w_out, v_norm2_g, v_mlp_w1, v_mlp_w2):
    given = dict(x=x, c=c, positions=positions, w_ada=w_ada, b_ada=b_ada, norm1_g=norm1_g, w_in=w_in, b_merge=b_merge, gla_w_alpha=gla_w_alpha, gla_b_alpha=gla_b_alpha, gla_out_norm_g=gla_out_norm_g, gla_w_o=gla_w_o, mla_q_lat_g=mla_q_lat_g, mla_w_uq=mla_w_uq, mla_kv_lat_g=mla_kv_lat_g, mla_w_ukv=mla_w_ukv, mla_qn_g=mla_qn_g, mla_kn_g=mla_kn_g, mla_w_o=mla_w_o, w_out=w_out, norm2_g=norm2_g, mlp_w1=mlp_w1, mlp_w2=mlp_w2, loss_target=loss_target, m_w_ada=m_w_ada, m_b_ada=m_b_ada, m_norm1_g=m_norm1_g, m_w_in=m_w_in, m_b_merge=m_b_merge, m_gla_w_alpha=m_gla_w_alpha, m_gla_b_alpha=m_gla_b_alpha, m_gla_out_norm_g=m_gla_out_norm_g, m_gla_w_o=m_gla_w_o, m_mla_q_lat_g=m_mla_q_lat_g, m_mla_w_uq=m_mla_w_uq, m_mla_kv_lat_g=m_mla_kv_lat_g, m_mla_w_ukv=m_mla_w_ukv, m_mla_qn_g=m_mla_qn_g, m_mla_kn_g=m_mla_kn_g, m_mla_w_o=m_mla_w_o, m_w_out=m_w_out, m_norm2_g=m_norm2_g, m_mlp_w1=m_mlp_w1, m_mlp_w2=m_mlp_w2, v_w_ada=v_w_ada, v_b_ada=v_b_ada, v_norm1_g=v_norm1_g, v_w_in=v_w_in, v_b_merge=v_b_merge, v_gla_w_alpha=v_gla_w_alpha, v_gla_b_alpha=v_gla_b_alpha, v_gla_out_norm_g=v_gla_out_norm_g, v_gla_w_o=v_gla_w_o, v_mla_q_lat_g=v_mla_q_lat_g, v_mla_w_uq=v_mla_w_uq, v_mla_kv_lat_g=v_mla_kv_lat_g, v_mla_w_ukv=v_mla_w_ukv, v_mla_qn_g=v_mla_qn_g, v_mla_kn_g=v_mla_kn_g, v_mla_w_o=v_mla_w_o, v_w_out=v_w_out, v_norm2_g=v_norm2_g, v_mlp_w1=v_mlp_w1, v_mlp_w2=v_mlp_w2)
    weights = {n: given[n] for n in TWIN_WEIGHTS}
    shared = {n: given[n] for n in SHARED_INPUTS}
    per_example = {n: given[n] for n in ['x', 'c', 'positions']}
    grad_fn = _jax.value_and_grad(_loss, argnums=(0, 1))

    def one_microbatch(ex, loss_target):
        ex = dict(ex)
        diff = ex.pop(TWIN_DIFF_INPUT)
        return grad_fn(weights, diff, {**shared, **ex}, loss_target)

    if N_MICROBATCH == 1:
        loss, (grad_w, grad_x) = one_microbatch(per_example, given["loss_target"])
    else:
        def body(carry, xs):
            loss_sum, grad_sum = carry
            l_k, (gw_k, gx_k) = one_microbatch(xs[0], xs[1])
            with _jax.named_scope("update"):
                return (loss_sum + l_k, _jax.tree.map(_jnp.add, grad_sum, gw_k)), gx_k

        init = (_jnp.zeros((), _jnp.float32), _jax.tree.map(_jnp.zeros_like, weights))
        (loss, grad_w), grad_x = _jax.lax.scan(body, init, (per_example, given["loss_target"]))
    with _jax.named_scope("update"):
        delta_w, new_m, new_v = {}, {}, {}
        for n in TWIN_WEIGHTS:
            delta_w[n], new_m[n], new_v[n] = _adamw(weights[n], grad_w[n], given["m_" + n], given["v_" + n])
    return (loss, grad_x, *[grad_w[n] for n in TWIN_WEIGHTS], *[delta_w[n] for n in TWIN_WEIGHTS],
            *[new_m[n] for n in TWIN_WEIGHTS], *[new_v[n] for n in TWIN_WEIGHTS])
```

```python
import functools

import jax
import jax.numpy as jnp
from jax import lax
from jax.experimental import pallas as pl
from jax.experimental.pallas import tpu as pltpu

F32 = jnp.float32
BF16 = jnp.bfloat16
MESH = pl.DeviceIdType.MESH

D = 1024
EPS = 1e-6
CHUNK = 64
GH, GDK, GDV, GLR, GTAU = 4, 128, 256, 16, 16.0
MH, MQR, MKVR, NOPE, ROPE, MV = 16, 256, 128, 64, 32, 64
MQK = NOPE + ROPE
HP = 128
FF = 4 * D
ROPE_THETA = 10000.0
IN_WIDTH = 5552
PW = 5632
N_DEV = 8
LANES = 128
PACK_ROWS = 2960
PACK_BLOCK = 592
ADAM_BLOCK = 80
SMALL_ROWS = 32
SMALL_SOURCES = tuple([(r, 6 + r) for r in range(6)] + [(12,), (13,), (14,), (15,), (16,), (17, 18, 19, 20),
                                                         (21,), (22,), (23,), (24,)])
LOSS_SOURCE = (25,)
VMEM_LIMIT = 56 * 1024 * 1024

ADAM_LR, ADAM_B1, ADAM_B2, ADAM_EPS, ADAM_WD, ADAM_STEP = 0.001, 0.9, 0.999, 1e-08, 0.01, 10

BIG = ("w_ada", "w_in", "gla_w_alpha", "gla_w_o", "mla_w_uq", "mla_w_ukv", "mla_w_o", "w_out", "mlp_w1", "mlp_w2")
BIG_SHARD_AXIS = {"w_ada": 1, "w_in": 1, "gla_w_alpha": 1, "gla_w_o": 0, "mla_w_uq": 1, "mla_w_ukv": 1,
                  "mla_w_o": 0, "w_out": 0, "mlp_w1": 1, "mlp_w2": 0}
SMALL = ("b_ada", "norm1_g", "norm2_g", "b_merge", "gla_b_alpha", "gla_out_norm_g", "mla_q_lat_g", "mla_kv_lat_g",
         "mla_qn_g", "mla_kn_g")
SMALL_ROW = {"b_ada": 0, "norm1_g": 6, "norm2_g": 7, "b_merge": 8, "gla_b_alpha": 10, "gla_out_norm_g": 11,
             "mla_q_lat_g": 12, "mla_kv_lat_g": 13, "mla_qn_g": 14, "mla_kn_g": 15}
WEIGHTS = ("w_ada", "b_ada", "norm1_g", "w_in", "b_merge", "gla_w_alpha", "gla_b_alpha", "gla_out_norm_g", "gla_w_o",
           "mla_q_lat_g", "mla_w_uq", "mla_kv_lat_g", "mla_w_ukv", "mla_qn_g", "mla_kn_g", "mla_w_o", "w_out",
           "norm2_g", "mlp_w1", "mlp_w2")


def _cparams(sem=None):
    return pltpu.CompilerParams(dimension_semantics=sem, vmem_limit_bytes=VMEM_LIMIT)


def _tile(n, pref):
    for t in (2048, 1024, 512, 256, 128):
        if t <= pref and n % t == 0:
            return t
    return n


def _dot(a, b, dims, precision=None):
    return lax.dot_general(a, b, (dims, ((), ())), preferred_element_type=F32, precision=precision)


NN = ((1,), (0,))
NT = ((1,), (1,))
TN = ((0,), (0,))


def _sigmoid(x):
    return 1.0 / (1.0 + jnp.exp(-x))


def _silu(x):
    return x * _sigmoid(x)


def _mm(a, b, mode, out_dtypes, name, *, pro=None, epi=None, extras=(), a_off=0, m=None, tm=1024, tn=512, tk=1024):
    if mode == "tn":
        kc, n = b.shape
        m = a.shape[1] if m is None else m
    elif mode == "nn":
        m, kc = a.shape
        n = b.shape[1]
    else:
        m, kc = a.shape
        n = b.shape[0]
    tm, tn, tk = _tile(m, tm), _tile(n, tn), _tile(kc, tk)
    nk = kc // tk
    dims = {"nn": NN, "nt": NT, "tn": TN}[mode]
    if mode == "tn":
        a_spec = pl.BlockSpec((tk, tm), lambda i, j, k: (k, i + a_off))
    else:
        a_spec = pl.BlockSpec((tm, tk), lambda i, j, k: (i + a_off, k))
    if mode == "nt":
        b_spec = pl.BlockSpec((tn, tk), lambda i, j, k: (j, k))
    else:
        b_spec = pl.BlockSpec((tk, tn), lambda i, j, k: (k, j))
    o_spec = pl.BlockSpec((tm, tn), lambda i, j, k: (i, j))
    n_ex, n_out = len(extras), len(out_dtypes)

    def body(a_ref, b_ref, *rest):
        ex, outs, acc = rest[:n_ex], rest[n_ex:n_ex + n_out], rest[-1]
        k = pl.program_id(2)

        @pl.when(k == 0)
        def _():
            acc[...] = jnp.zeros_like(acc)

        av = a_ref[...]
        if pro is not None:
            av = pro(av)
        acc[...] += _dot(av.astype(BF16), b_ref[...].astype(BF16), dims)

        @pl.when(k == nk - 1)
        def _():
            res = (acc[...],) if epi is None else epi(acc[...], *[e[...] for e in ex])
            for o_ref, r in zip(outs, res):
                o_ref[...] = r.astype(o_ref.dtype)

    out = pl.pallas_call(
        body, name=name, grid=(m // tm, n // tn, nk),
        in_specs=[a_spec, b_spec] + [o_spec] * n_ex,
        out_specs=[o_spec] * n_out,
        out_shape=[jax.ShapeDtypeStruct((m, n), dt) for dt in out_dtypes],
        scratch_shapes=[pltpu.VMEM((tm, tn), F32)],
        compiler_params=_cparams(("parallel", "parallel", "arbitrary")),
    )(a, b, *extras)
    return out[0] if n_out == 1 else out


def _rows(s):
    return _tile(s, 512)


def _mod_spec():
    return pl.BlockSpec((1, 6, D), lambda b, i: (b, 0, 0))


def _tok_spec(tr, nb, width=D, col=0):
    return pl.BlockSpec((tr, width), lambda b, i: (b * nb + i, col))


def _norm_mod_fwd(x, g, mod3, i_shift, i_scale, name, mixed=None, i_gate=None):
    bsz, _, _ = mod3.shape
    t = x.shape[0]
    s = t // bsz
    tr = _rows(s)
    nb = s // tr
    has_res = mixed is not None

    def body(*refs):
        if has_res:
            x_ref, mx_ref, g_ref, mod_ref, x1_ref, h_ref = refs
            xv = x_ref[...] + mod_ref[0, i_gate:i_gate + 1, :] * mx_ref[...]
            x1_ref[...] = xv
        else:
            x_ref, g_ref, mod_ref, h_ref = refs
            xv = x_ref[...]
        r = lax.rsqrt(jnp.mean(xv * xv, axis=1, keepdims=True) + EPS)
        hn = (xv * r) * g_ref[...]
        h = hn * (1.0 + mod_ref[0, i_scale:i_scale + 1, :]) + mod_ref[0, i_shift:i_shift + 1, :]
        h_ref[...] = h.astype(BF16)

    tok = _tok_spec(tr, nb)
    gspec = pl.BlockSpec((1, D), lambda b, i: (0, 0))
    ins = [x] + ([mixed] if has_res else []) + [g, mod3]
    in_specs = [tok] + ([tok] if has_res else []) + [gspec, _mod_spec()]
    out_shape = ([jax.ShapeDtypeStruct((t, D), F32)] if has_res else []) + [jax.ShapeDtypeStruct((t, D), BF16)]
    out = pl.pallas_call(
        body, name=name, grid=(bsz, nb), in_specs=in_specs, out_specs=[tok] * len(out_shape), out_shape=out_shape,
        compiler_params=_cparams(("arbitrary", "arbitrary")),
    )(*ins)
    return (out[0], out[1]) if has_res else (None, out[0])


def _norm_mod_bwd(x, dh, dres, g, mod3, i_shift, i_scale, name, mixed=None, i_gate=None):
    bsz = mod3.shape[0]
    t = x.shape[0]
    s = t // bsz
    tr = _rows(s)
    nb = s // tr
    has_res = mixed is not None

    def body(*refs):
        if has_res:
            x_ref, dh_ref, dres_ref, mx_ref, g_ref, mod_ref, dx_ref, dmx_ref, accb, accg = refs
        else:
            x_ref, dh_ref, dres_ref, g_ref, mod_ref, dx_ref, accb, accg = refs
        b, i = pl.program_id(0), pl.program_id(1)

        @pl.when(i == 0)
        def _():
            accb[...] = jnp.zeros_like(accb)

        @pl.when((i == 0) & (b == 0))
        def _():
            accg[...] = jnp.zeros_like(accg)

        xv, dhv, gv = x_ref[...], dh_ref[...], g_ref[...]
        r = lax.rsqrt(jnp.mean(xv * xv, axis=1, keepdims=True) + EPS)
        xn = xv * r
        accb[0, 0:1, :] += jnp.sum(dhv, axis=0, keepdims=True)
        accb[0, 1:2, :] += jnp.sum(dhv * (xn * gv), axis=0, keepdims=True)
        tt = dhv * (1.0 + mod_ref[0, i_scale:i_scale + 1, :])
        accg[0:1, :] += jnp.sum(tt * xn, axis=0, keepdims=True)
        dxn = tt * gv
        dx = dres_ref[...] + r * (dxn - xn * jnp.mean(dxn * xn, axis=1, keepdims=True))
        dx_ref[...] = dx
        if has_res:
            accb[0, 2:3, :] += jnp.sum(dx * mx_ref[...], axis=0, keepdims=True)
            dmx_ref[...] = (dx * mod_ref[0, i_gate:i_gate + 1, :]).astype(BF16)

    tok = _tok_spec(tr, nb)
    gspec = pl.BlockSpec((1, D), lambda b, i: (0, 0))
    ins = [x, dh, dres] + ([mixed] if has_res else []) + [g, mod3]
    in_specs = [tok] * (4 if has_res else 3) + [gspec, _mod_spec()]
    out_shape = [jax.ShapeDtypeStruct((t, D), F32)] + ([jax.ShapeDtypeStruct((t, D), BF16)] if has_res else [])
    out_specs = [tok] * len(out_shape)
    out_shape += [jax.ShapeDtypeStruct((bsz, 8, D), F32), jax.ShapeDtypeStruct((8, D), F32)]
    out_specs += [pl.BlockSpec((1, 8, D), lambda b, i: (b, 0, 0)), pl.BlockSpec((8, D), lambda b, i: (0, 0))]
    return pl.pallas_call(
        body, name=name, grid=(bsz, nb), in_specs=in_specs, out_specs=out_specs, out_shape=out_shape,
        compiler_params=_cparams(("arbitrary", "arbitrary")),
    )(*ins)


def _loss_head(x1, ff, tgt, mod3):
    bsz = mod3.shape[0]
    t = x1.shape[0]
    s = t // bsz
    tr = _rows(s)
    nb = s // tr

    def body(x1_ref, ff_ref, tg_ref, mod_ref, dy_ref, dff_ref, accb, accl):
        b, i = pl.program_id(0), pl.program_id(1)

        @pl.when(i == 0)
        def _():
            accb[...] = jnp.zeros_like(accb)

        @pl.when((i == 0) & (b == 0))
        def _():
            accl[...] = jnp.zeros_like(accl)

        gate = mod_ref[0, 5:6, :]
        ffv = ff_ref[...]
        err = x1_ref[...] + gate * ffv - tg_ref[...]
        accl[0:1, :] += jnp.sum(err * err, axis=0, keepdims=True) * (0.5 / D)
        dy = err * (1.0 / D)
        dy_ref[...] = dy
        dff_ref[...] = (dy * gate).astype(BF16)
        accb[0, 0:1, :] += jnp.sum(dy * ffv, axis=0, keepdims=True)

    tok = _tok_spec(tr, nb)
    return pl.pallas_call(
        body, name="loss_head", grid=(bsz, nb), in_specs=[tok, tok, tok, _mod_spec()],
        out_specs=[tok, tok, pl.BlockSpec((1, 8, D), lambda b, i: (b, 0, 0)), pl.BlockSpec((8, D), lambda b, i: (0, 0))],
        out_shape=[jax.ShapeDtypeStruct((t, D), F32), jax.ShapeDtypeStruct((t, D), BF16),
                   jax.ShapeDtypeStruct((bsz, 8, D), F32), jax.ShapeDtypeStruct((8, D), F32)],
        compiler_params=_cparams(("arbitrary", "arbitrary")),
    )(x1, ff, tgt, mod3)


def _merge_fwd(proj, b_merge, y_a, y_b):
    t = proj.shape[0]
    tr = _tile(t, 512)

    def body(la_ref, lb_ref, bm_ref, ya_ref, yb_ref, mix_ref):
        ga = _sigmoid(la_ref[...] + bm_ref[:, 0:D])
        gb = _sigmoid(lb_ref[...] + bm_ref[:, D:2 * D])
        mix_ref[...] = (ga * ya_ref[...] + gb * yb_ref[...]).astype(BF16)

    tok = pl.BlockSpec((tr, D), lambda i: (i, 0))
    return pl.pallas_call(
        body, name="merge_fwd", grid=(t // tr,),
        in_specs=[pl.BlockSpec((tr, D), lambda i: (i, 3)), pl.BlockSpec((tr, D), lambda i: (i, 4)),
                  pl.BlockSpec((1, 2 * D), lambda i: (0, 0)), tok, tok],
        out_specs=tok, out_shape=jax.ShapeDtypeStruct((t, D), BF16),
        compiler_params=_cparams(("arbitrary",)),
    )(proj, proj, b_merge, y_a, y_b)


def _merge_bwd(dmix, proj, b_merge, y_a, y_b):
    t = proj.shape[0]
    tr = _tile(t, 512)

    def body(dm_ref, la_ref, lb_ref, bm_ref, ya_ref, yb_ref, dya_ref, dyb_ref, dl_ref, acc):
        @pl.when(pl.program_id(0) == 0)
        def _():
            acc[...] = jnp.zeros_like(acc)

        dm = dm_ref[...]
        ga = _sigmoid(la_ref[...] + bm_ref[:, 0:D])
        gb = _sigmoid(lb_ref[...] + bm_ref[:, D:2 * D])
        dya_ref[...] = (dm * ga).astype(BF16)
        dyb_ref[...] = (dm * gb).astype(BF16)
        dla = dm * ya_ref[...] * ga * (1.0 - ga)
        dlb = dm * yb_ref[...] * gb * (1.0 - gb)
        dl_ref[:, 0:D] = dla.astype(BF16)
        dl_ref[:, D:2 * D] = dlb.astype(BF16)
        acc[0:1, 0:D] += jnp.sum(dla, axis=0, keepdims=True)
        acc[0:1, D:2 * D] += jnp.sum(dlb, axis=0, keepdims=True)

    tok = pl.BlockSpec((tr, D), lambda i: (i, 0))
    return pl.pallas_call(
        body, name="merge_bwd", grid=(t // tr,),
        in_specs=[tok, pl.BlockSpec((tr, D), lambda i: (i, 3)), pl.BlockSpec((tr, D), lambda i: (i, 4)),
                  pl.BlockSpec((1, 2 * D), lambda i: (0, 0)), tok, tok],
        out_specs=[tok, tok, pl.BlockSpec((tr, 2 * D), lambda i: (i, 0)), pl.BlockSpec((8, 2 * D), lambda i: (0, 0))],
        out_shape=[jax.ShapeDtypeStruct((t, D), BF16), jax.ShapeDtypeStruct((t, D), BF16),
                   jax.ShapeDtypeStruct((t, 2 * D), BF16), jax.ShapeDtypeStruct((8, 2 * D), F32)],
        compiler_params=_cparams(("arbitrary",)),
    )(dmix, proj, proj, b_merge, y_a, y_b)


def _log_sigmoid(z):
    return jnp.minimum(z, 0.0) - jnp.log(1.0 + jnp.exp(-jnp.abs(z)))


def _tri(lower):
    r = lax.broadcasted_iota(jnp.int32, (CHUNK, CHUNK), 0)
    c = lax.broadcasted_iota(jnp.int32, (CHUNK, CHUNK), 1)
    return jnp.where(r >= c if lower else r <= c, 1.0, 0.0).astype(F32)


def _gla_fwd(proj, wa_pad, b_alpha, g_out, bsz):
    t = proj.shape[0]
    s = t // bsz
    nc = s // CHUNK

    def body(q_ref, k_ref, v_ref, gg_ref, ms_ref, wa_ref, ba_ref, go_ref, o_ref, og_ref, st_ref, la, state):
        z = _dot(ms_ref[...].astype(BF16), wa_ref[...], NN) + ba_ref[...]
        la[...] = _log_sigmoid(z) * (1.0 / GTAU)
        state[...] = jnp.zeros_like(state)
        low = _tri(True)
        gout = go_ref[...]

        def chunk(n, carry):
            rows = pl.ds(pl.multiple_of(n * CHUNK, CHUNK), CHUNK)
            lac = la[rows, :]
            cum = _dot(low, lac, NN, lax.Precision.HIGHEST)
            ce = jnp.sum(lac, axis=0, keepdims=True)
            kd = (k_ref[rows, :] * jnp.exp(ce - cum)).astype(BF16)
            new = state[...] * jnp.exp(ce) + _dot(v_ref[rows, :].astype(BF16), kd, TN)
            state[...] = new
            st_ref[pl.ds(pl.multiple_of(n * GDV, GDV), GDV), :] = new
            qs = (q_ref[rows, :] * (GDK ** -0.5)).astype(BF16)
            o = _dot(qs, new.astype(BF16), NT)
            o_ref[rows, :] = o
            ro = lax.rsqrt(jnp.mean(o * o, axis=1, keepdims=True) + EPS)
            og_ref[rows, :] = (((o * ro) * gout) * _silu(gg_ref[rows, :])).astype(BF16)
            return carry

        lax.fori_loop(0, nc, chunk, 0)

    hk = pl.BlockSpec((s, GDK), lambda b, h: (b, h))
    return pl.pallas_call(
        body, name="gla_fwd", grid=(bsz, GH),
        in_specs=[hk, pl.BlockSpec((s, GDK), lambda b, h: (b, GH + h)), pl.BlockSpec((s, GDV), lambda b, h: (b, 4 + h)),
                  pl.BlockSpec((s, GDV), lambda b, h: (b, 8 + h)), pl.BlockSpec((s, LANES), lambda b, h: (b, 43)),
                  pl.BlockSpec((LANES, GDK), lambda b, h: (0, h)), pl.BlockSpec((1, GDK), lambda b, h: (0, h)),
                  pl.BlockSpec((1, GDV), lambda b, h: (0, 0))],
        out_specs=[pl.BlockSpec((s, GDV), lambda b, h: (b, h)), pl.BlockSpec((s, GDV), lambda b, h: (b, h)),
                   pl.BlockSpec((nc * GDV, GDK), lambda b, h: (b * GH + h, 0))],
        out_shape=[jax.ShapeDtypeStruct((t, GH * GDV), F32), jax.ShapeDtypeStruct((t, GH * GDV), BF16),
                   jax.ShapeDtypeStruct((bsz * GH * nc * GDV, GDK), F32)],
        scratch_shapes=[pltpu.VMEM((s, GDK), F32), pltpu.VMEM((GDV, GDK), F32)],
        compiler_params=_cparams(("arbitrary", "arbitrary")),
    )(proj, proj, proj, proj, proj, wa_pad, b_alpha, g_out)


def _gla_bwd(dog, o, proj, wa_pad, b_alpha, g_out, states, bsz):
    t = proj.shape[0]
    s = t // bsz
    nc = s // CHUNK

    def body(dog_ref, o_ref, q_ref, k_ref, v_ref, gg_ref, ms_ref, wa_ref, ba_ref, go_ref, st_ref,
             dq_ref, dk_ref, dv_ref, dgg_ref, dz_ref, dba, dgo, zs, la, carry_g):
        @pl.when(pl.program_id(1) == 0)
        def _():
            dba[...] = jnp.zeros_like(dba)
            dgo[...] = jnp.zeros_like(dgo)

        z = _dot(ms_ref[...].astype(BF16), wa_ref[...], NN) + ba_ref[...]
        zs[...] = z
        la[...] = _log_sigmoid(z) * (1.0 / GTAU)
        carry_g[...] = jnp.zeros_like(carry_g)
        low, upp = _tri(True), _tri(False)
        gout = go_ref[...]
        last_row = lax.broadcasted_iota(jnp.int32, (CHUNK, GDK), 0) == CHUNK - 1

        def chunk(step, carry):
            n = nc - 1 - step
            rows = pl.ds(pl.multiple_of(n * CHUNK, CHUNK), CHUNK)
            lac = la[rows, :]
            cum = _dot(low, lac, NN, lax.Precision.HIGHEST)
            ce = jnp.sum(lac, axis=0, keepdims=True)
            e = jnp.exp(ce - cum)
            dec = jnp.exp(ce)
            kf = k_ref[rows, :]
            kd = (kf * e).astype(BF16)
            vv = v_ref[rows, :].astype(BF16)
            qs = (q_ref[rows, :] * (GDK ** -0.5)).astype(BF16)
            ov = o_ref[rows, :]
            ro = lax.rsqrt(jnp.mean(ov * ov, axis=1, keepdims=True) + EPS)
            on = ov * ro
            gg = gg_ref[rows, :]
            sg = _sigmoid(gg)
            dogv = dog_ref[rows, :]
            dgg_ref[rows, :] = (dogv * (on * gout) * (sg * (1.0 + gg * (1.0 - sg)))).astype(BF16)
            t1 = dogv * (gg * sg)
            dgo[0:1, :] += jnp.sum(t1 * on, axis=0, keepdims=True)
            don = t1 * gout
            do = ro * (don - on * jnp.mean(don * on, axis=1, keepdims=True))
            dob = do.astype(BF16)
            st_n = st_ref[pl.ds(pl.multiple_of(n * GDV, GDV), GDV), :]
            dq_ref[rows, :] = (_dot(dob, st_n.astype(BF16), NN) * (GDK ** -0.5)).astype(BF16)
            dn = carry_g[...] + _dot(dob, qs, TN)
            prev = jnp.maximum(n - 1, 0)
            st_p = st_ref[pl.ds(pl.multiple_of(prev * GDV, GDV), GDV), :] * jnp.where(n > 0, 1.0, 0.0)
            ddec = jnp.sum(dn * st_p, axis=0, keepdims=True)
            dnb = dn.astype(BF16)
            dkd = _dot(vv, dnb, NN)
            dv_ref[rows, :] = _dot(kd, dnb, NT).astype(BF16)
            dk_ref[rows, :] = (dkd * e).astype(BF16)
            w = dkd * kf * e
            dce = jnp.sum(w, axis=0, keepdims=True) + ddec * dec
            dcum = jnp.where(last_row, dce - w, -w)
            dla = _dot(upp, dcum, NN, lax.Precision.HIGHEST)
            dz = dla * (1.0 / GTAU) * _sigmoid(-zs[rows, :])
            dba[0:1, :] += jnp.sum(dz, axis=0, keepdims=True)
            dz_ref[rows, :] = dz.astype(BF16)
            carry_g[...] = dn * dec
            return carry

        lax.fori_loop(0, nc, chunk, 0)

    hv = pl.BlockSpec((s, GDV), lambda h, b: (b, h))
    hk = pl.BlockSpec((s, GDK), lambda h, b: (b, h))
    return pl.pallas_call(
        body, name="gla_bwd", grid=(GH, bsz),
        in_specs=[hv, hv, hk, pl.BlockSpec((s, GDK), lambda h, b: (b, GH + h)),
                  pl.BlockSpec((s, GDV), lambda h, b: (b, 4 + h)), pl.BlockSpec((s, GDV), lambda h, b: (b, 8 + h)),
                  pl.BlockSpec((s, LANES), lambda h, b: (b, 43)), pl.BlockSpec((LANES, GDK), lambda h, b: (0, h)),
                  pl.BlockSpec((1, GDK), lambda h, b: (0, h)), pl.BlockSpec((1, GDV), lambda h, b: (0, 0)),
                  pl.BlockSpec((nc * GDV, GDK), lambda h, b: (b * GH + h, 0))],
        out_specs=[hk, hk, hv, hv, hk, pl.BlockSpec((8, GDK), lambda h, b: (0, h)),
                   pl.BlockSpec((8, GDV), lambda h, b: (h, 0))],
        out_shape=[jax.ShapeDtypeStruct((t, GH * GDK), BF16), jax.ShapeDtypeStruct((t, GH * GDK), BF16),
                   jax.ShapeDtypeStruct((t, GH * GDV), BF16), jax.ShapeDtypeStruct((t, GH * GDV), BF16),
                   jax.ShapeDtypeStruct((t, GH * GDK), BF16), jax.ShapeDtypeStruct((8, GH * GDK), F32),
                   jax.ShapeDtypeStruct((8 * GH, GDV), F32)],
        scratch_shapes=[pltpu.VMEM((s, GDK), F32), pltpu.VMEM((s, GDK), F32), pltpu.VMEM((GDV, GDK), F32)],
        compiler_params=_cparams(("arbitrary", "arbitrary")),
    )(dog, o, proj, proj, proj, proj, proj, wa_pad, b_alpha, g_out, states)


def _rope_tables(pos_ref, fr_ref, sg_ref):
    ang = pos_ref[...].astype(F32) * fr_ref[...]
    return jnp.cos(ang), jnp.sin(ang) * sg_ref[...]


def _partner(x):
    lane = lax.broadcasted_iota(jnp.int32, x.shape, 1)
    return jnp.where(lane < NOPE + ROPE // 2, pltpu.roll(x, LANES - ROPE // 2, 1), pltpu.roll(x, ROPE // 2, 1))


def _mla_rows(t):
    return _tile(t, 256)


def _mla_pre_fwd(proj, pos, fr, sg, q_lat_g, kv_lat_g, qn_g, kn_g, wuq, wukv):
    t = proj.shape[0]
    tr = _mla_rows(t)

    def body(cq_ref, ckv_ref, ms_ref, pos_ref, fr_ref, sg_ref, qlg, kvlg, qng, kng, wuq_ref, wukv_ref, q_out, k_out, v_out):
        cos, sin = _rope_tables(pos_ref, fr_ref, sg_ref)
        cq = cq_ref[...]
        cqn = (cq * lax.rsqrt(jnp.mean(cq * cq, axis=1, keepdims=True) + EPS) * qlg[...]).astype(BF16)
        ckv = ckv_ref[...]
        ckvn = (ckv * lax.rsqrt(jnp.mean(ckv * ckv, axis=1, keepdims=True) + EPS) * kvlg[...]).astype(BF16)
        lane = lax.broadcasted_iota(jnp.int32, (tr, HP), 1)
        kpe = jnp.where((lane >= NOPE) & (lane < MQK), ms_ref[...], 0.0)
        v_out[...] = _dot(ckvn, wukv_ref[:, MH * HP:], NN).astype(BF16)
        for h in range(MH):
            cols = slice(h * HP, (h + 1) * HP)
            qh = _dot(cqn, wuq_ref[:, cols], NN)
            qn = qh * lax.rsqrt(jnp.sum(qh * qh, axis=1, keepdims=True) * (1.0 / MQK) + EPS) * qng[...]
            q_out[:, cols] = (qn * cos + _partner(qn) * sin).astype(BF16)
            kh = _dot(ckvn, wukv_ref[:, cols], NN) + kpe
            kn = kh * lax.rsqrt(jnp.sum(kh * kh, axis=1, keepdims=True) * (1.0 / MQK) + EPS) * kng[...]
            k_out[:, cols] = (kn * cos + _partner(kn) * sin).astype(BF16)

    def full(a):
        return pl.BlockSpec(a.shape, lambda i: (0, 0))

    wide = pl.BlockSpec((tr, MH * HP), lambda i: (i, 0))
    return pl.pallas_call(
        body, name="mla_pre_fwd", grid=(t // tr,),
        in_specs=[pl.BlockSpec((tr, MQR), lambda i: (i, 20)), pl.BlockSpec((tr, MKVR), lambda i: (i, 42)),
                  pl.BlockSpec((tr, LANES), lambda i: (i, 43)), pl.BlockSpec((tr, 1), lambda i: (i, 0)),
                  full(fr), full(sg), full(q_lat_g), full(kv_lat_g), full(qn_g), full(kn_g), full(wuq), full(wukv)],
        out_specs=[wide, wide, wide],
        out_shape=[jax.ShapeDtypeStruct((t, MH * HP), BF16)] * 3,
        compiler_params=_cparams(("arbitrary",)),
    )(proj, proj, proj, pos, fr, sg, q_lat_g, kv_lat_g, qn_g, kn_g, wuq, wukv)


def _mla_pre_bwd(dq2, dk2, dv2, dmisc_gla, proj, pos, fr, sg, q_lat_g, kv_lat_g, qn_g, kn_g, wuq, wukv):
    t = proj.shape[0]
    tr = _mla_rows(t)

    def body(dq_ref, dk_ref, dv_ref, dmg_ref, cq_ref, ckv_ref, ms_ref, pos_ref, fr_ref, sg_ref, qlg, kvlg, qng, kng,
             wuq_ref, wukv_ref, dcq_ref, dckv_ref, dms_ref, dwuq, dwukv, acc, dqf, dkvf):
        @pl.when(pl.program_id(0) == 0)
        def _():
            dwuq[...] = jnp.zeros_like(dwuq)
            dwukv[...] = jnp.zeros_like(dwukv)
            acc[...] = jnp.zeros_like(acc)

        cos, sin = _rope_tables(pos_ref, fr_ref, sg_ref)
        cq = cq_ref[...]
        rc = lax.rsqrt(jnp.mean(cq * cq, axis=1, keepdims=True) + EPS)
        xc = cq * rc
        cqn = (xc * qlg[...]).astype(BF16)
        ckv = ckv_ref[...]
        rkv = lax.rsqrt(jnp.mean(ckv * ckv, axis=1, keepdims=True) + EPS)
        xkv = ckv * rkv
        ckvn = (xkv * kvlg[...]).astype(BF16)
        lane = lax.broadcasted_iota(jnp.int32, (tr, HP), 1)
        is_rope = (lane >= NOPE) & (lane < MQK)
        kpe = jnp.where(is_rope, ms_ref[...], 0.0)
        dkpe = jnp.zeros((tr, HP), F32)
        dqng = jnp.zeros((1, HP), F32)
        dkng = jnp.zeros((1, HP), F32)
        for h in range(MH):
            cols = slice(h * HP, (h + 1) * HP)
            qh = _dot(cqn, wuq_ref[:, cols], NN)
            rq = lax.rsqrt(jnp.sum(qh * qh, axis=1, keepdims=True) * (1.0 / MQK) + EPS)
            xq = qh * rq
            dy = dq_ref[:, cols].astype(F32)
            dqn = dy * cos - _partner(dy) * sin
            dqng += jnp.sum(dqn * xq, axis=0, keepdims=True)
            tq = dqn * qng[...]
            dqf[:, cols] = (rq * (tq - xq * (jnp.sum(tq * xq, axis=1, keepdims=True) * (1.0 / MQK)))).astype(BF16)
            kh = _dot(ckvn, wukv_ref[:, cols], NN) + kpe
            rk = lax.rsqrt(jnp.sum(kh * kh, axis=1, keepdims=True) * (1.0 / MQK) + EPS)
            xk = kh * rk
            dy = dk_ref[:, cols].astype(F32)
            dkn = dy * cos - _partner(dy) * sin
            dkng += jnp.sum(dkn * xk, axis=0, keepdims=True)
            tk = dkn * kng[...]
            dkh = rk * (tk - xk * (jnp.sum(tk * xk, axis=1, keepdims=True) * (1.0 / MQK)))
            dkvf[:, cols] = jnp.where(lane < NOPE, dkh, 0.0).astype(BF16)
            dkpe += jnp.where(is_rope, dkh, 0.0)
        dkvf[:, MH * HP:] = dv_ref[...]
        acc[2:3, 0:HP] += dqng
        acc[3:4, 0:HP] += dkng
        dms_ref[...] = (dmg_ref[...] + dkpe).astype(BF16)

        dqfv = dqf[...]
        dwuq[...] += _dot(cqn, dqfv, TN)
        dcqn = _dot(dqfv, wuq_ref[...], NT)
        acc[0:1, :] += jnp.sum(dcqn * xc, axis=0, keepdims=True)
        tc = dcqn * qlg[...]
        dcq_ref[...] = (rc * (tc - xc * jnp.mean(tc * xc, axis=1, keepdims=True))).astype(BF16)

        dkvfv = dkvf[...]
        dwukv[...] += _dot(ckvn, dkvfv, TN)
        dckvn = _dot(dkvfv, wukv_ref[...], NT)
        acc[1:2, 0:MKVR] += jnp.sum(dckvn * xkv, axis=0, keepdims=True)
        tkv = dckvn * kvlg[...]
        dckv_ref[...] = (rkv * (tkv - xkv * jnp.mean(tkv * xkv, axis=1, keepdims=True))).astype(BF16)

    def full(a):
        return pl.BlockSpec(a.shape, lambda i: (0, 0))

    wide = pl.BlockSpec((tr, MH * HP), lambda i: (i, 0))
    narrow = pl.BlockSpec((tr, LANES), lambda i: (i, 0))
    return pl.pallas_call(
        body, name="mla_pre_bwd", grid=(t // tr,),
        in_specs=[wide, wide, wide, narrow,
                  pl.BlockSpec((tr, MQR), lambda i: (i, 20)), pl.BlockSpec((tr, MKVR), lambda i: (i, 42)),
                  pl.BlockSpec((tr, LANES), lambda i: (i, 43)), pl.BlockSpec((tr, 1), lambda i: (i, 0)),
                  full(fr), full(sg), full(q_lat_g), full(kv_lat_g), full(qn_g), full(kn_g), full(wuq), full(wukv)],
        out_specs=[pl.BlockSpec((tr, MQR), lambda i: (i, 0)), narrow, narrow,
                   pl.BlockSpec((MQR, MH * HP), lambda i: (0, 0)), pl.BlockSpec((MKVR, 2 * MH * HP), lambda i: (0, 0)),
                   pl.BlockSpec((8, MQR), lambda i: (0, 0))],
        out_shape=[jax.ShapeDtypeStruct((t, MQR), BF16), jax.ShapeDtypeStruct((t, MKVR), BF16),
                   jax.ShapeDtypeStruct((t, LANES), BF16), jax.ShapeDtypeStruct((MQR, MH * HP), F32),
                   jax.ShapeDtypeStruct((MKVR, 2 * MH * HP), F32), jax.ShapeDtypeStruct((8, MQR), F32)],
        scratch_shapes=[pltpu.VMEM((tr, MH * HP), BF16), pltpu.VMEM((tr, 2 * MH * HP), BF16)],
        compiler_params=_cparams(("arbitrary",)),
    )(dq2, dk2, dv2, dmisc_gla, proj, proj, proj, pos, fr, sg, q_lat_g, kv_lat_g, qn_g, kn_g, wuq, wukv)


ATT_TILE = 256
NEG = -1e30


def _att_mask(q0, k0, tq, tk):
    qc = (q0 + lax.broadcasted_iota(jnp.int32, (tq, tk), 0)) // CHUNK
    kc = (k0 + lax.broadcasted_iota(jnp.int32, (tq, tk), 1)) // CHUNK
    return kc <= qc


def _attn_fwd(q2, k2, v2, bsz):
    t = q2.shape[0]
    s = t // bsz
    tile = _tile(s, ATT_TILE)
    nq = s // tile
    scale = MQK ** -0.5

    def body(q_ref, k_ref, v_ref, o_ref, lse_ref):
        def q_loop(qi, carry):
            q0 = pl.multiple_of(qi * tile, tile)
            q = q_ref[pl.ds(q0, tile), :]

            def k_loop(kj, c):
                m, l, acc = c
                k0 = pl.multiple_of(kj * tile, tile)
                sc = _dot(q, k_ref[pl.ds(k0, tile), :], NT) * scale
                sc = jnp.where(_att_mask(q0, k0, tile, tile), sc, NEG)
                m_new = jnp.maximum(m, jnp.max(sc, axis=1, keepdims=True))
                alpha = jnp.exp(m - m_new)
                p = jnp.exp(sc - m_new)
                l = alpha * l + jnp.sum(p, axis=1, keepdims=True)
                acc = alpha * acc + _dot(p.astype(BF16), v_ref[pl.ds(k0, tile), :], NN)
                return m_new, l, acc

            init = (jnp.full((tile, 1), NEG, F32), jnp.zeros((tile, 1), F32), jnp.zeros((tile, HP), F32))
            m, l, acc = lax.fori_loop(0, qi + 1, k_loop, init)
            o_ref[pl.ds(q0, tile), :] = (acc / l).astype(BF16)
            lse_ref[pl.ds(q0, tile), :] = jnp.broadcast_to(m + jnp.log(l), (tile, HP))
            return carry

        lax.fori_loop(0, nq, q_loop, 0)

    spec = pl.BlockSpec((s, HP), lambda b, h: (b, h))
    return pl.pallas_call(
        body, name="attn_fwd", grid=(bsz, MH), in_specs=[spec] * 3, out_specs=[spec, spec],
        out_shape=[jax.ShapeDtypeStruct((t, MH * HP), BF16), jax.ShapeDtypeStruct((t, MH * HP), F32)],
        compiler_params=_cparams(("arbitrary", "arbitrary")),
    )(q2, k2, v2)


def _attn_bwd(q2, k2, v2, do2, o2, lse2, bsz):
    t = q2.shape[0]
    s = t // bsz
    tile = _tile(s, ATT_TILE)
    nq = s // tile
    scale = MQK ** -0.5

    def body(q_ref, k_ref, v_ref, do_ref, o_ref, lse_ref, dq_ref, dk_ref, dv_ref, dq_acc, delta):
        dq_acc[...] = jnp.zeros_like(dq_acc)

        def d_loop(i, carry):
            rows = pl.ds(pl.multiple_of(i * tile, tile), tile)
            dl = jnp.sum(do_ref[rows, :].astype(F32) * o_ref[rows, :].astype(F32), axis=1, keepdims=True)
            delta[rows, :] = jnp.broadcast_to(dl, (tile, HP))
            return carry

        lax.fori_loop(0, nq, d_loop, 0)

        def k_loop(kj, carry):
            k0 = pl.multiple_of(kj * tile, tile)
            kk = k_ref[pl.ds(k0, tile), :]
            vv = v_ref[pl.ds(k0, tile), :]

            def q_loop(qi, c):
                dk, dv = c
                q0 = pl.multiple_of(qi * tile, tile)
                rows = pl.ds(q0, tile)
                q = q_ref[rows, :]
                do = do_ref[rows, :]
                sc = _dot(q, kk, NT) * scale
                p = jnp.where(_att_mask(q0, k0, tile, tile), jnp.exp(sc - lse_ref[rows, 0:1]), 0.0)
                dv = dv + _dot(p.astype(BF16), do, TN)
                dp = _dot(do, vv, NT)
                ds = (p * (dp - delta[rows, 0:1]) * scale).astype(BF16)
                dq_acc[rows, :] += _dot(ds, kk, NN)
                dk = dk + _dot(ds, q, TN)
                return dk, dv

            zero = jnp.zeros((tile, HP), F32)
            dk, dv = lax.fori_loop(kj, nq, q_loop, (zero, zero))
            dk_ref[pl.ds(k0, tile), :] = dk.astype(BF16)
            dv_ref[pl.ds(k0, tile), :] = dv.astype(BF16)
            return carry

        lax.fori_loop(0, nq, k_loop, 0)
        dq_ref[...] = dq_acc[...].astype(BF16)

    spec = pl.BlockSpec((s, HP), lambda b, h: (b, h))
    return pl.pallas_call(
        body, name="attn_bwd", grid=(bsz, MH), in_specs=[spec] * 6, out_specs=[spec] * 3,
        out_shape=[jax.ShapeDtypeStruct((t, MH * HP), BF16)] * 3,
        scratch_shapes=[pltpu.VMEM((s, HP), F32), pltpu.VMEM((s, HP), F32)],
        compiler_params=_cparams(("arbitrary", "arbitrary")),
    )(q2, k2, v2, do2, o2, lse2)


def _perm_w_in(w):
    z = lambda n: jnp.zeros((w.shape[0], n), w.dtype)
    return jnp.concatenate([w[:, :3072], w[:, 3504:5552], w[:, 3088:3344], w[:, 3344:3472], w[:, 3072:3088], z(48),
                            w[:, 3472:3504], z(32)], axis=1)


def _unperm_w_in(g):
    return jnp.concatenate([g[:, :3072], g[:, 5504:5520], g[:, 5120:5376], g[:, 5376:5504], g[:, 5568:5600],
                            g[:, 3072:5120]], axis=1)


def _pad_wa(w):
    return jnp.pad(w, ((0, LANES - GLR), (0, 0)))


def _pad_wuq(w):
    return jnp.pad(w.reshape(MQR, MH, MQK), ((0, 0), (0, 0), (0, HP - MQK))).reshape(MQR, MH * HP)


def _unpad_wuq(g):
    return g.reshape(MQR, MH, HP)[:, :, :MQK].reshape(MQR, MH * MQK)


def _pad_wukv(w):
    w3 = w.reshape(MKVR, MH, NOPE + MV)
    kp = jnp.pad(w3[:, :, :NOPE], ((0, 0), (0, 0), (0, HP - NOPE))).reshape(MKVR, MH * HP)
    vp = jnp.pad(w3[:, :, NOPE:], ((0, 0), (0, 0), (0, HP - MV))).reshape(MKVR, MH * HP)
    return jnp.concatenate([kp, vp], axis=1)


def _unpad_wukv(g):
    kp = g[:, :MH * HP].reshape(MKVR, MH, HP)[:, :, :NOPE]
    vp = g[:, MH * HP:].reshape(MKVR, MH, HP)[:, :, :MV]
    return jnp.concatenate([kp, vp], axis=2).reshape(MKVR, MH * (NOPE + MV))


def _pad_wo(w):
    return jnp.pad(w.reshape(MH, MV, D), ((0, 0), (0, HP - MV), (0, 0))).reshape(MH * HP, D)


def _unpad_wo(g):
    return g.reshape(MH, HP, D)[:, :MV, :].reshape(MH * MV, D)


def _pad_lanes(v, n=HP):
    return jnp.pad(v, ((0, 0), (0, n - v.shape[1])))


def _local_step(x, c, positions, tgt, wb, sp):
    bsz, s, _ = x.shape
    t = bsz * s
    x2 = x.reshape(t, D)
    tgt2 = tgt.reshape(t, D)
    pos = positions.reshape(t, 1)
    fr16 = ROPE_THETA ** (-jnp.arange(0, ROPE, 2, dtype=F32) / ROPE)
    zero = lambda n: jnp.zeros((n,), F32)
    fr = jnp.concatenate([zero(NOPE), fr16, fr16, zero(HP - MQK)]).reshape(1, HP)
    sg = jnp.concatenate([zero(NOPE), -jnp.ones((ROPE // 2,), F32), jnp.ones((ROPE // 2,), F32), zero(HP - MQK)]).reshape(1, HP)

    w_in = _perm_w_in(wb["w_in"])
    wa_pad = _pad_wa(wb["gla_w_alpha"])
    wuq = _pad_wuq(wb["mla_w_uq"])
    wukv = _pad_wukv(wb["mla_w_ukv"])
    wo_pad = _pad_wo(wb["mla_w_o"])
    qn_g, kn_g = _pad_lanes(sp["mla_qn_g"]), _pad_lanes(sp["mla_kn_g"])

    c8 = jnp.pad(c, ((0, 8 - bsz), (0, 0)))
    b_ada = sp["b_ada"]
    mod8 = _mm(c8, wb["w_ada"], "nn", (F32,), "ada_fwd", pro=_silu, extras=(jnp.broadcast_to(b_ada, (8, 6 * D)),),
               epi=lambda acc, bias: (acc + bias,))
    mod3 = mod8[:bsz].reshape(bsz, 6, D)

    _, h = _norm_mod_fwd(x2, sp["norm1_g"], mod3, 0, 1, "norm1_fwd")
    proj = _mm(h, w_in, "nn", (F32,), "proj_fwd")
    o_gla, og, states = _gla_fwd(proj, wa_pad, sp["gla_b_alpha"], sp["gla_out_norm_g"], bsz)
    y_a = _mm(og, wb["gla_w_o"], "nn", (F32,), "gla_out_fwd")
    q2, k2, v2 = _mla_pre_fwd(proj, pos, fr, sg, sp["mla_q_lat_g"], sp["mla_kv_lat_g"], qn_g, kn_g, wuq, wukv)
    o2, lse2 = _attn_fwd(q2, k2, v2, bsz)
    y_b = _mm(o2, wo_pad, "nn", (F32,), "mla_out_fwd")
    mix = _merge_fwd(proj, sp["b_merge"], y_a, y_b)
    mixed = _mm(mix, wb["w_out"], "nn", (F32,), "w_out_fwd")

    x1, h2 = _norm_mod_fwd(x2, sp["norm2_g"], mod3, 3, 4, "norm2_fwd", mixed=mixed, i_gate=2)
    a, f = _mm(h2, wb["mlp_w1"], "nn", (F32, BF16), "mlp1_fwd",
               epi=lambda acc: (acc, jnp.square(jnp.maximum(acc, 0.0))))
    ff = _mm(f, wb["mlp_w2"], "nn", (F32,), "mlp2_fwd")
    dy, dff, acc_g2, acc_loss = _loss_head(x1, ff, tgt2, mod3)

    gw = {}
    gw["mlp_w2"] = _mm(f, dff, "tn", (F32,), "mlp2_dw")
    da = _mm(dff, wb["mlp_w2"], "nt", (BF16,), "mlp2_dx", extras=(a,),
             epi=lambda acc, av: (acc * (2.0 * jnp.maximum(av, 0.0)),))
    gw["mlp_w1"] = _mm(h2, da, "tn", (F32,), "mlp1_dw")
    dh2 = _mm(da, wb["mlp_w1"], "nt", (F32,), "mlp1_dx")
    dx1, dmixed, accb2, accg2 = _norm_mod_bwd(x1, dh2, dy, sp["norm2_g"], mod3, 3, 4, "norm2_bwd", mixed=mixed, i_gate=2)

    gw["w_out"] = _mm(mix, dmixed, "tn", (F32,), "w_out_dw")
    dmix = _mm(dmixed, wb["w_out"], "nt", (F32,), "w_out_dx")
    dy_a, dy_b, dlogits, acc_bm = _merge_bwd(dmix, proj, sp["b_merge"], y_a, y_b)
    gw["gla_w_o"] = _mm(og, dy_a, "tn", (F32,), "gla_out_dw")
    dog = _mm(dy_a, wb["gla_w_o"], "nt", (F32,), "gla_out_dx")
    dq_g, dk_g, dv_g, dgg, dz, acc_ba, acc_go = _gla_bwd(dog, o_gla, proj, wa_pad, sp["gla_b_alpha"],
                                                         sp["gla_out_norm_g"], states, bsz)
    gw["gla_w_alpha"] = _mm(proj, dz, "tn", (F32,), "gla_alpha_dw", a_off=43, m=LANES)[:GLR]
    dmisc_gla = _mm(dz, wa_pad, "nt", (F32,), "gla_alpha_dx")
    gwo_pad = _mm(o2, dy_b, "tn", (F32,), "mla_out_dw")
    gw["mla_w_o"] = _unpad_wo(gwo_pad)
    do2 = _mm(dy_b, wo_pad, "nt", (BF16,), "mla_out_dx")
    dq2, dk2, dv2 = _attn_bwd(q2, k2, v2, do2, o2, lse2, bsz)
    dcq, dckv, dmisc, gwuq, gwukv, acc_mla = _mla_pre_bwd(dq2, dk2, dv2, dmisc_gla, proj, pos, fr, sg, sp["mla_q_lat_g"],
                                                         sp["mla_kv_lat_g"], qn_g, kn_g, wuq, wukv)
    gw["mla_w_uq"] = _unpad_wuq(gwuq)
    gw["mla_w_ukv"] = _unpad_wukv(gwukv)
    dproj = jnp.concatenate([dq_g, dk_g, dv_g, dgg, dlogits, dcq, dckv, dmisc], axis=1)
    gw["w_in"] = _unperm_w_in(_mm(h, dproj, "tn", (F32,), "proj_dw"))
    dh = _mm(dproj, w_in, "nt", (F32,), "proj_dx")
    grad_x, accb1, accg1 = _norm_mod_bwd(x2, dh, dx1, sp["norm1_g"], mod3, 0, 1, "norm1_bwd")

    dmod = jnp.stack([accb1[:, 0], accb1[:, 1], accb2[:, 2], accb2[:, 0], accb2[:, 1], acc_g2[:, 0]], axis=1)
    dmod8 = jnp.pad(dmod.reshape(bsz, 6 * D), ((0, 8 - bsz), (0, 0)))
    gw["w_ada"] = _mm(c8, dmod8, "tn", (F32,), "ada_dw", pro=_silu)

    rows = {
        "dmod": dmod.reshape(bsz * 6, D),
        "norm1_g": accg1[0:1], "norm2_g": accg2[0:1],
        "b_merge": acc_bm[0:1].reshape(2, D),
        "gla_b_alpha": _pad_lanes(acc_ba[0:1], D),
        "gla_out_norm_g": _pad_lanes(acc_go.reshape(GH, 8, GDV)[:, 0, :], D),
        "mla_q_lat_g": _pad_lanes(acc_mla[0:1], D), "mla_kv_lat_g": _pad_lanes(acc_mla[1:2], D),
        "mla_qn_g": _pad_lanes(acc_mla[2:3], D), "mla_kn_g": _pad_lanes(acc_mla[3:4], D),
        "loss": acc_loss[0:1],
    }
    return grad_x.reshape(bsz, s, D), gw, rows


HBM_SPEC = pl.BlockSpec(memory_space=pltpu.HBM)


def _all_gather(p, name):
    r, cdim = p.shape

    def body(p_ref, out_ref, send_sems, recv_sems, local_sem):
        x, y, c = lax.axis_index("x"), lax.axis_index("y"), lax.axis_index("c")
        me, sibling = (x, y, c), (x, y, 1 - c)
        chips = [(1 - x, y), (x, 1 - y), (1 - x, 1 - y)]

        def slot(px, py, pc):
            return out_ref.at[4 * px + 2 * py + pc]

        def copy(k, block, to, src=None):
            return pltpu.make_async_remote_copy(
                src_ref=slot(*block) if src is None else src, dst_ref=slot(*block),
                send_sem=send_sems.at[k], recv_sem=recv_sems.at[k], device_id=to, device_id_type=MESH)

        mine = pltpu.make_async_copy(p_ref, slot(*me), local_sem)
        mine.start()
        first = [copy(0, me, sibling, src=p_ref)] + [copy(1 + j, me, (*chip, c), src=p_ref) for j, chip in enumerate(chips)]
        for cp in first:
            cp.start()
        passed = [copy(4 + j, (*chip, c), sibling) for j, chip in enumerate(chips)]
        for j, chip in enumerate(chips):
            copy(1 + j, (*chip, c), me).wait_recv()
            passed[j].start()
        copy(0, sibling, me).wait_recv()
        for j, chip in enumerate(chips):
            copy(4 + j, (*chip, 1 - c), me).wait_recv()
        for cp in first + passed:
            cp.wait_send()
        mine.wait()

    return pl.pallas_call(
        body, name=name, out_shape=jax.ShapeDtypeStruct((N_DEV, r, cdim), p.dtype),
        in_specs=[HBM_SPEC], out_specs=HBM_SPEC,
        scratch_shapes=[pltpu.SemaphoreType.DMA((7,)), pltpu.SemaphoreType.DMA((7,)), pltpu.SemaphoreType.DMA(())],
    )(p)


def _sibling_exchange(g):
    def body(g_ref, out_ref, send_sem, recv_sem):
        x, y, c = lax.axis_index("x"), lax.axis_index("y"), lax.axis_index("c")
        cp = pltpu.make_async_remote_copy(src_ref=g_ref, dst_ref=out_ref, send_sem=send_sem, recv_sem=recv_sem,
                                          device_id=(x, y, 1 - c), device_id_type=MESH)
        cp.start()
        cp.wait()

    return pl.pallas_call(
        body, name="rs_sibling_exchange", out_shape=jax.ShapeDtypeStruct(g.shape, g.dtype),
        in_specs=[HBM_SPEC], out_specs=HBM_SPEC,
        scratch_shapes=[pltpu.SemaphoreType.DMA(()), pltpu.SemaphoreType.DMA(())],
    )(g)


def _chip_exchange(tsum):
    def body(t_ref, out_ref, send_sems, recv_sems, local_sem):
        x, y, c = lax.axis_index("x"), lax.axis_index("y"), lax.axis_index("c")
        my_chip = 2 * x + y
        chips = [(1 - x, y), (x, 1 - y), (1 - x, 1 - y)]
        mine = pltpu.make_async_copy(t_ref.at[my_chip], out_ref.at[my_chip], local_sem)
        mine.start()
        sends = [pltpu.make_async_remote_copy(
            src_ref=t_ref.at[2 * px + py], dst_ref=out_ref.at[my_chip], send_sem=send_sems.at[j], recv_sem=recv_sems.at[j],
            device_id=(px, py, c), device_id_type=MESH) for j, (px, py) in enumerate(chips)]
        for cp in sends:
            cp.start()
        for j, (px, py) in enumerate(chips):
            pltpu.make_async_remote_copy(
                src_ref=t_ref.at[my_chip], dst_ref=out_ref.at[2 * px + py], send_sem=send_sems.at[j],
                recv_sem=recv_sems.at[j], device_id=(px, py, c), device_id_type=MESH).wait_recv()
        for cp in sends:
            cp.wait_send()
        mine.wait()

    return pl.pallas_call(
        body, name="rs_chip_exchange", out_shape=jax.ShapeDtypeStruct(tsum.shape, tsum.dtype),
        in_specs=[HBM_SPEC], out_specs=HBM_SPEC,
        scratch_shapes=[pltpu.SemaphoreType.DMA((3,)), pltpu.SemaphoreType.DMA((3,)), pltpu.SemaphoreType.DMA(())],
    )(tsum)


def _pair_sum(a, b):
    n, r, cdim = a.shape
    blk = pl.BlockSpec((1, PACK_BLOCK, cdim), lambda j, i: (j, i, 0))

    def body(a_ref, b_ref, o_ref):
        o_ref[...] = (a_ref[...].astype(F32) + b_ref[...].astype(F32)).astype(BF16)

    return pl.pallas_call(
        body, name="rs_pair_sum", grid=(n, r // PACK_BLOCK), in_specs=[blk, blk], out_specs=blk,
        out_shape=jax.ShapeDtypeStruct(a.shape, BF16), compiler_params=_cparams(("arbitrary", "arbitrary")),
    )(a, b)


def _adamw_math(w, g, m, v):
    m = ADAM_B1 * m + (1.0 - ADAM_B1) * g
    v = ADAM_B2 * v + (1.0 - ADAM_B2) * jnp.square(g)
    m_hat = m / (1.0 - ADAM_B1 ** ADAM_STEP)
    v_hat = v / (1.0 - ADAM_B2 ** ADAM_STEP)
    delta = -ADAM_LR * (m_hat / (jnp.sqrt(v_hat) + ADAM_EPS) + ADAM_WD * w)
    return delta, m, v


def _adamw_packed(parts, w, m, v):
    _, r, cdim = parts.shape
    blk = pl.BlockSpec((ADAM_BLOCK, cdim), lambda i: (i, 0))

    def body(p_ref, w_ref, m_ref, v_ref, g_out, d_out, m_out, v_out):
        g = p_ref[0].astype(F32)
        for j in range(1, 4):
            g = g + p_ref[j].astype(F32)
        g_out[...] = g
        d_out[...], m_out[...], v_out[...] = _adamw_math(w_ref[...], g, m_ref[...], v_ref[...])

    return pl.pallas_call(
        body, name="adamw_packed", grid=(r // ADAM_BLOCK,),
        in_specs=[pl.BlockSpec((4, ADAM_BLOCK, cdim), lambda i: (0, i, 0)), blk, blk, blk], out_specs=[blk] * 4,
        out_shape=[jax.ShapeDtypeStruct((r, cdim), F32)] * 4, compiler_params=_cparams(("arbitrary",)),
    )(parts, w, m, v)


def _adamw_small(parts, w, m, v):
    def body(p_ref, w_ref, m_ref, v_ref, g_out, d_out, m_out, v_out, loss_out):
        def total(srcs):
            acc = None
            for r in srcs:
                for j in range(N_DEV):
                    term = p_ref[j, r:r + 1, :]
                    acc = term if acc is None else acc + term
            return acc

        for prow, srcs in enumerate(SMALL_SOURCES):
            one = slice(prow, prow + 1)
            g = total(srcs)
            g_out[one, :] = g
            d_out[one, :], m_out[one, :], v_out[one, :] = _adamw_math(w_ref[one, :], g, m_ref[one, :], v_ref[one, :])
        loss_out[...] = jnp.broadcast_to(jnp.sum(total(LOSS_SOURCE), axis=1, keepdims=True), (8, LANES))

    full = lambda shp: pl.BlockSpec(shp, lambda i: (0,) * len(shp))
    return pl.pallas_call(
        body, name="adamw_small", grid=(1,),
        in_specs=[full((N_DEV, SMALL_ROWS, D)), full((16, D)), full((16, D)), full((16, D))],
        out_specs=[full((16, D))] * 4 + [full((8, LANES))],
        out_shape=[jax.ShapeDtypeStruct((16, D), F32)] * 4 + [jax.ShapeDtypeStruct((8, LANES), F32)],
        compiler_params=_cparams(("arbitrary",)),
    )(parts, w, m, v)


def _pack_local(shards, dtype):
    flat = jnp.concatenate([shards[n].astype(dtype).reshape(-1) for n in BIG])
    return jnp.pad(flat, (0, PACK_ROWS * D - flat.shape[0])).reshape(PACK_ROWS, D)


def _unpack(packed, shard_shapes, lead=()):
    flat = packed.reshape(*lead, PACK_ROWS * D)
    out, off = {}, 0
    for n in BIG:
        r, cdim = shard_shapes[n]
        out[n] = flat[..., off:off + r * cdim].reshape(*lead, r, cdim)
        off += r * cdim
    return out


def _full_from_gathered(g, axis):
    if axis == 0:
        return g.reshape(g.shape[0] * g.shape[1], g.shape[2])
    return jnp.transpose(g, (1, 0, 2)).reshape(g.shape[1], g.shape[0] * g.shape[2])


def _split_for_devices(full, axis):
    if axis == 0:
        return full.reshape(N_DEV, -1)
    r, c8 = full.shape
    return jnp.transpose(full.reshape(r, N_DEV, c8 // N_DEV), (1, 0, 2)).reshape(N_DEV, -1)


def _small_pack(vals):
    rows = []
    for n in SMALL:
        v = vals[n].reshape(-1)
        k = -(-v.shape[0] // D)
        rows.append(jnp.pad(v, (0, k * D - v.shape[0])).reshape(k, D))
    return jnp.concatenate(rows, axis=0)


def _small_unpack(packed, shapes):
    out = {}
    for n in SMALL:
        k = shapes[n][-1]
        r0 = SMALL_ROW[n]
        out[n] = packed[r0:r0 + -(-k // D)].reshape(-1)[:k].reshape(shapes[n])
    return out


def kernel(x, c, positions, w_ada, b_ada, norm1_g, w_in, b_merge, gla_w_alpha, gla_b_alpha, gla_out_norm_g, gla_w_o, mla_q_lat_g, mla_w_uq, mla_kv_lat_g, mla_w_ukv, mla_qn_g, mla_kn_g, mla_w_o, w_out, norm2_g, mlp_w1, mlp_w2, loss_target, m_w_ada, m_b_ada, m_norm1_g, m_w_in, m_b_merge, m_gla_w_alpha, m_gla_b_alpha, m_gla_out_norm_g, m_gla_w_o, m_mla_q_lat_g, m_mla_w_uq, m_mla_kv_lat_g, m_mla_w_ukv, m_mla_qn_g, m_mla_kn_g, m_mla_w_o, m_w_out, m_norm2_g, m_mlp_w1, m_mlp_w2, v_w_ada, v_b_ada, v_norm1_g, v_w_in, v_b_merge, v_gla_w_alpha, v_gla_b_alpha, v_gla_out_norm_g, v_gla_w_o, v_mla_q_lat_g, v_mla_w_uq, v_mla_kv_lat_g, v_mla_w_ukv, v_mla_qn_g, v_mla_kn_g, v_mla_w_o, v_w_out, v_norm2_g, v_mlp_w1, v_mlp_w2):
    args = dict(locals())
    wts = {n: args[n][0] for n in WEIGHTS}
    mom = {n: args["m_" + n][0] for n in WEIGHTS}
    var = {n: args["v_" + n][0] for n in WEIGHTS}
    shard_shapes = {n: wts[n].shape for n in BIG}
    my_c = lax.axis_index("c")

    gathered = _all_gather(_pack_local(wts, BF16), "weights_all_gather")
    gsh = _unpack(gathered, shard_shapes, lead=(N_DEV,))
    wb = {n: _full_from_gathered(gsh[n], BIG_SHARD_AXIS[n]) for n in BIG}
    sp = {n: wts[n].reshape(1, -1) for n in SMALL}

    grad_x, gw, rows = _local_step(x, c, positions, loss_target, wb, sp)

    per_dev = jnp.concatenate([_split_for_devices(gw[n], BIG_SHARD_AXIS[n]).astype(BF16) for n in BIG], axis=1)
    per_dev = jnp.pad(per_dev, ((0, 0), (0, PACK_ROWS * D - per_dev.shape[1]))).reshape(4, 2, PACK_ROWS, D)
    keep = lax.dynamic_index_in_dim(per_dev, my_c, axis=1, keepdims=False)
    give = lax.dynamic_index_in_dim(per_dev, 1 - my_c, axis=1, keepdims=False)
    chip_sums = _pair_sum(keep, _sibling_exchange(give))
    parts = _chip_exchange(chip_sums)
    big = _adamw_packed(parts, _pack_local(wts, F32), _pack_local(mom, F32), _pack_local(var, F32))
    big = [_unpack(o, shard_shapes) for o in big]

    order = ["dmod", "norm1_g", "norm2_g", "b_merge", "gla_b_alpha", "gla_out_norm_g", "mla_q_lat_g", "mla_kv_lat_g",
             "mla_qn_g", "mla_kn_g", "loss"]
    part_rows = jnp.concatenate([rows[n] for n in order], axis=0)
    part_rows = jnp.pad(part_rows, ((0, SMALL_ROWS - part_rows.shape[0]), (0, 0)))
    all_rows = _all_gather(part_rows, "partials_all_gather")
    small = _adamw_small(all_rows, _small_pack({n: wts[n] for n in SMALL}), _small_pack({n: mom[n] for n in SMALL}),
                         _small_pack({n: var[n] for n in SMALL}))
    loss = small[4][0, 0]
    small_shapes = {n: wts[n].shape for n in SMALL}
    small = [_small_unpack(o, small_shapes) for o in small[:4]]

    outs = [loss, grad_x]
    for k in range(4):
        for n in WEIGHTS:
            val = big[k][n] if n in BIG else small[k][n]
            outs.append(val.reshape((1,) + tuple(wts[n].shape)))
    return tuple(outs)
```

```python
import functools

import jax
import jax.numpy as jnp
from jax import lax
from jax.experimental import pallas as pl
from jax.experimental.pallas import tpu as pltpu

F32 = jnp.float32
BF16 = jnp.bfloat16
MESH = pl.DeviceIdType.MESH

D = 1024
EPS = 1e-6
CHUNK = 64
GH, GDK, GDV, GLR, GTAU = 4, 128, 256, 16, 16.0
MH, MQR, MKVR, NOPE, ROPE, MV = 16, 256, 128, 64, 32, 64
MQK = NOPE + ROPE
HP = 128
FF = 4 * D
ROPE_THETA = 10000.0
IN_WIDTH = 5552
PW = 5632
N_DEV = 8
LANES = 128
PACK_ROWS = 2960
PACK_BLOCK = 592
SMALL_ROWS = 32
SMALL_SOURCES = tuple([(r, 6 + r) for r in range(6)] + [(12,), (13,), (14,), (15,), (16,), (17, 18, 19, 20),
                                                         (21,), (22,), (23,), (24,)])
LOSS_SOURCE = (25,)
VMEM_LIMIT = 56 * 1024 * 1024

ADAM_LR, ADAM_B1, ADAM_B2, ADAM_EPS, ADAM_WD, ADAM_STEP = 0.001, 0.9, 0.999, 1e-08, 0.01, 10

SLAB = (("w_ada", 768, "T"), ("mlp_w1", 512, "T"), ("gla_w_o", 128, "N"), ("mla_w_o", 128, "N"), ("w_out", 128, "N"),
        ("mlp_w2", 512, "N"), ("mla_w_uq", 48, "TR"), ("mla_w_ukv", 32, "TR"), ("w_in", 694, "T"), ("gla_w_alpha", 1, "TR"))
BIG = tuple(n for n, _, _ in SLAB)
SMALL = ("b_ada", "norm1_g", "norm2_g", "b_merge", "gla_b_alpha", "gla_out_norm_g", "mla_q_lat_g", "mla_kv_lat_g",
         "mla_qn_g", "mla_kn_g")
SMALL_ROW = {"b_ada": 0, "norm1_g": 6, "norm2_g": 7, "b_merge": 8, "gla_b_alpha": 10, "gla_out_norm_g": 11,
             "mla_q_lat_g": 12, "mla_kv_lat_g": 13, "mla_qn_g": 14, "mla_kn_g": 15}
WEIGHTS = ("w_ada", "b_ada", "norm1_g", "w_in", "b_merge", "gla_w_alpha", "gla_b_alpha", "gla_out_norm_g", "gla_w_o",
           "mla_q_lat_g", "mla_w_uq", "mla_kv_lat_g", "mla_w_ukv", "mla_qn_g", "mla_kn_g", "mla_w_o", "w_out",
           "norm2_g", "mlp_w1", "mlp_w2")


def _cparams(sem=None):
    return pltpu.CompilerParams(dimension_semantics=sem, vmem_limit_bytes=VMEM_LIMIT)


def _tile(n, pref):
    for t in (2048, 1024, 512, 256, 128):
        if t <= pref and n % t == 0:
            return t
    return n


def _dot(a, b, dims, precision=None):
    return lax.dot_general(a, b, (dims, ((), ())), preferred_element_type=F32, precision=precision)


NN = ((1,), (0,))
NT = ((1,), (1,))
TN = ((0,), (0,))


def _sigmoid(x):
    return 1.0 / (1.0 + jnp.exp(-x))


def _silu(x):
    return x * _sigmoid(x)


def _mm(a, b, mode, out_dtypes, name, *, pro=None, pro_b=None, epi=None, extras=(), a_off=0, m=None, tm=1024, tn=512,
        tk=1024):
    if mode == "tn":
        kc, n = b.shape
        m = a.shape[1] if m is None else m
    elif mode == "nn":
        m, kc = a.shape
        n = b.shape[1]
    else:
        m, kc = a.shape
        n = b.shape[0]
    tm, tn, tk = _tile(m, tm), _tile(n, tn), _tile(kc, tk)
    nk = kc // tk
    dims = {"nn": NN, "nt": NT, "tn": TN}[mode]
    if mode == "tn":
        a_spec = pl.BlockSpec((tk, tm), lambda i, j, k: (k, i + a_off))
    else:
        a_spec = pl.BlockSpec((tm, tk), lambda i, j, k: (i + a_off, k))
    if mode == "nt":
        b_spec = pl.BlockSpec((tn, tk), lambda i, j, k: (j, k))
    else:
        b_spec = pl.BlockSpec((tk, tn), lambda i, j, k: (k, j))
    o_spec = pl.BlockSpec((tm, tn), lambda i, j, k: (i, j))
    n_ex, n_out = len(extras), len(out_dtypes)

    def body(a_ref, b_ref, *rest):
        ex, outs, acc = rest[:n_ex], rest[n_ex:n_ex + n_out], rest[-1]
        k = pl.program_id(2)

        @pl.when(k == 0)
        def _():
            acc[...] = jnp.zeros_like(acc)

        av = a_ref[...]
        if pro is not None:
            av = pro(av)
        bv = b_ref[...]
        if pro_b is not None:
            bv = pro_b(bv)
        acc[...] += _dot(av.astype(BF16), bv.astype(BF16), dims)

        @pl.when(k == nk - 1)
        def _():
            res = (acc[...],) if epi is None else epi(acc[...], *[e[...] for e in ex])
            for o_ref, r in zip(outs, res):
                o_ref[...] = r.astype(o_ref.dtype)

    out = pl.pallas_call(
        body, name=name, grid=(m // tm, n // tn, nk),
        in_specs=[a_spec, b_spec] + [o_spec] * n_ex,
        out_specs=[o_spec] * n_out,
        out_shape=[jax.ShapeDtypeStruct((m, n), dt) for dt in out_dtypes],
        scratch_shapes=[pltpu.VMEM((tm, tn), F32)],
        compiler_params=_cparams(("parallel", "parallel", "arbitrary")),
    )(a, b, *extras)
    return out[0] if n_out == 1 else out


def _rows(s):
    return _tile(s, 512)


def _mod_spec():
    return pl.BlockSpec((1, 6, D), lambda b, i: (b, 0, 0))


def _tok_spec(tr, nb, width=D, col=0):
    return pl.BlockSpec((tr, width), lambda b, i: (b * nb + i, col))


def _norm_mod_fwd(x, g, mod3, i_shift, i_scale, name, mixed=None, i_gate=None):
    bsz, _, _ = mod3.shape
    t = x.shape[0]
    s = t // bsz
    tr = _rows(s)
    nb = s // tr
    has_res = mixed is not None

    def body(*refs):
        if has_res:
            x_ref, mx_ref, g_ref, mod_ref, x1_ref, h_ref = refs
            xv = x_ref[...] + mod_ref[0, i_gate:i_gate + 1, :] * mx_ref[...]
            x1_ref[...] = xv
        else:
            x_ref, g_ref, mod_ref, h_ref = refs
            xv = x_ref[...]
        r = lax.rsqrt(jnp.mean(xv * xv, axis=1, keepdims=True) + EPS)
        hn = (xv * r) * g_ref[...]
        h = hn * (1.0 + mod_ref[0, i_scale:i_scale + 1, :]) + mod_ref[0, i_shift:i_shift + 1, :]
        h_ref[...] = h.astype(BF16)

    tok = _tok_spec(tr, nb)
    gspec = pl.BlockSpec((1, D), lambda b, i: (0, 0))
    ins = [x] + ([mixed] if has_res else []) + [g, mod3]
    in_specs = [tok] + ([tok] if has_res else []) + [gspec, _mod_spec()]
    out_shape = ([jax.ShapeDtypeStruct((t, D), F32)] if has_res else []) + [jax.ShapeDtypeStruct((t, D), BF16)]
    out = pl.pallas_call(
        body, name=name, grid=(bsz, nb), in_specs=in_specs, out_specs=[tok] * len(out_shape), out_shape=out_shape,
        compiler_params=_cparams(("arbitrary", "arbitrary")),
    )(*ins)
    return (out[0], out[1]) if has_res else (None, out[0])


def _norm_mod_bwd(x, dh, dres, g, mod3, i_shift, i_scale, name, mixed=None, i_gate=None):
    bsz = mod3.shape[0]
    t = x.shape[0]
    s = t // bsz
    tr = _rows(s)
    nb = s // tr
    has_res = mixed is not None

    def body(*refs):
        if has_res:
            x_ref, dh_ref, dres_ref, mx_ref, g_ref, mod_ref, dx_ref, dmx_ref, accb, accg = refs
        else:
            x_ref, dh_ref, dres_ref, g_ref, mod_ref, dx_ref, accb, accg = refs
        b, i = pl.program_id(0), pl.program_id(1)

        @pl.when(i == 0)
        def _():
            accb[...] = jnp.zeros_like(accb)

        @pl.when((i == 0) & (b == 0))
        def _():
            accg[...] = jnp.zeros_like(accg)

        xv, dhv, gv = x_ref[...], dh_ref[...], g_ref[...]
        r = lax.rsqrt(jnp.mean(xv * xv, axis=1, keepdims=True) + EPS)
        xn = xv * r
        accb[0, 0:1, :] += jnp.sum(dhv, axis=0, keepdims=True)
        accb[0, 1:2, :] += jnp.sum(dhv * (xn * gv), axis=0, keepdims=True)
        tt = dhv * (1.0 + mod_ref[0, i_scale:i_scale + 1, :])
        accg[0:1, :] += jnp.sum(tt * xn, axis=0, keepdims=True)
        dxn = tt * gv
        dx = dres_ref[...] + r * (dxn - xn * jnp.mean(dxn * xn, axis=1, keepdims=True))
        dx_ref[...] = dx
        if has_res:
            accb[0, 2:3, :] += jnp.sum(dx * mx_ref[...], axis=0, keepdims=True)
            dmx_ref[...] = (dx * mod_ref[0, i_gate:i_gate + 1, :]).astype(BF16)

    tok = _tok_spec(tr, nb)
    gspec = pl.BlockSpec((1, D), lambda b, i: (0, 0))
    ins = [x, dh, dres] + ([mixed] if has_res else []) + [g, mod3]
    in_specs = [tok] * (4 if has_res else 3) + [gspec, _mod_spec()]
    out_shape = [jax.ShapeDtypeStruct((t, D), F32)] + ([jax.ShapeDtypeStruct((t, D), BF16)] if has_res else [])
    out_specs = [tok] * len(out_shape)
    out_shape += [jax.ShapeDtypeStruct((bsz, 8, D), F32), jax.ShapeDtypeStruct((8, D), F32)]
    out_specs += [pl.BlockSpec((1, 8, D), lambda b, i: (b, 0, 0)), pl.BlockSpec((8, D), lambda b, i: (0, 0))]
    return pl.pallas_call(
        body, name=name, grid=(bsz, nb), in_specs=in_specs, out_specs=out_specs, out_shape=out_shape,
        compiler_params=_cparams(("arbitrary", "arbitrary")),
    )(*ins)


def _loss_head(x1, ff, tgt, mod3):
    bsz = mod3.shape[0]
    t = x1.shape[0]
    s = t // bsz
    tr = _rows(s)
    nb = s // tr

    def body(x1_ref, ff_ref, tg_ref, mod_ref, dy_ref, dff_ref, accb, accl):
        b, i = pl.program_id(0), pl.program_id(1)

        @pl.when(i == 0)
        def _():
            accb[...] = jnp.zeros_like(accb)

        @pl.when((i == 0) & (b == 0))
        def _():
            accl[...] = jnp.zeros_like(accl)

        gate = mod_ref[0, 5:6, :]
        ffv = ff_ref[...]
        err = x1_ref[...] + gate * ffv - tg_ref[...]
        accl[0:1, :] += jnp.sum(err * err, axis=0, keepdims=True) * (0.5 / D)
        dy = err * (1.0 / D)
        dy_ref[...] = dy
        dff_ref[...] = (dy * gate).astype(BF16)
        accb[0, 0:1, :] += jnp.sum(dy * ffv, axis=0, keepdims=True)

    tok = _tok_spec(tr, nb)
    return pl.pallas_call(
        body, name="loss_head", grid=(bsz, nb), in_specs=[tok, tok, tok, _mod_spec()],
        out_specs=[tok, tok, pl.BlockSpec((1, 8, D), lambda b, i: (b, 0, 0)), pl.BlockSpec((8, D), lambda b, i: (0, 0))],
        out_shape=[jax.ShapeDtypeStruct((t, D), F32), jax.ShapeDtypeStruct((t, D), BF16),
                   jax.ShapeDtypeStruct((bsz, 8, D), F32), jax.ShapeDtypeStruct((8, D), F32)],
        compiler_params=_cparams(("arbitrary", "arbitrary")),
    )(x1, ff, tgt, mod3)


def _merge_fwd(proj, b_merge, y_a, y_b):
    t = proj.shape[0]
    tr = _tile(t, 512)

    def body(la_ref, lb_ref, bm_ref, ya_ref, yb_ref, mix_ref):
        ga = _sigmoid(la_ref[...] + bm_ref[:, 0:D])
        gb = _sigmoid(lb_ref[...] + bm_ref[:, D:2 * D])
        mix_ref[...] = (ga * ya_ref[...] + gb * yb_ref[...]).astype(BF16)

    tok = pl.BlockSpec((tr, D), lambda i: (i, 0))
    return pl.pallas_call(
        body, name="merge_fwd", grid=(t // tr,),
        in_specs=[pl.BlockSpec((tr, D), lambda i: (i, 3)), pl.BlockSpec((tr, D), lambda i: (i, 4)),
                  pl.BlockSpec((1, 2 * D), lambda i: (0, 0)), tok, tok],
        out_specs=tok, out_shape=jax.ShapeDtypeStruct((t, D), BF16),
        compiler_params=_cparams(("arbitrary",)),
    )(proj, proj, b_merge, y_a, y_b)


def _merge_bwd(dmix, proj, b_merge, y_a, y_b):
    t = proj.shape[0]
    tr = _tile(t, 512)

    def body(dm_ref, la_ref, lb_ref, bm_ref, ya_ref, yb_ref, dya_ref, dyb_ref, dl_ref, acc):
        @pl.when(pl.program_id(0) == 0)
        def _():
            acc[...] = jnp.zeros_like(acc)

        dm = dm_ref[...]
        ga = _sigmoid(la_ref[...] + bm_ref[:, 0:D])
        gb = _sigmoid(lb_ref[...] + bm_ref[:, D:2 * D])
        dya_ref[...] = (dm * ga).astype(BF16)
        dyb_ref[...] = (dm * gb).astype(BF16)
        dla = dm * ya_ref[...] * ga * (1.0 - ga)
        dlb = dm * yb_ref[...] * gb * (1.0 - gb)
        dl_ref[:, 0:D] = dla.astype(BF16)
        dl_ref[:, D:2 * D] = dlb.astype(BF16)
        acc[0:1, 0:D] += jnp.sum(dla, axis=0, keepdims=True)
        acc[0:1, D:2 * D] += jnp.sum(dlb, axis=0, keepdims=True)

    tok = pl.BlockSpec((tr, D), lambda i: (i, 0))
    return pl.pallas_call(
        body, name="merge_bwd", grid=(t // tr,),
        in_specs=[tok, pl.BlockSpec((tr, D), lambda i: (i, 3)), pl.BlockSpec((tr, D), lambda i: (i, 4)),
                  pl.BlockSpec((1, 2 * D), lambda i: (0, 0)), tok, tok],
        out_specs=[tok, tok, pl.BlockSpec((tr, 2 * D), lambda i: (i, 0)), pl.BlockSpec((8, 2 * D), lambda i: (0, 0))],
        out_shape=[jax.ShapeDtypeStruct((t, D), BF16), jax.ShapeDtypeStruct((t, D), BF16),
                   jax.ShapeDtypeStruct((t, 2 * D), BF16), jax.ShapeDtypeStruct((8, 2 * D), F32)],
        compiler_params=_cparams(("arbitrary",)),
    )(dmix, proj, proj, b_merge, y_a, y_b)


def _log_sigmoid(z):
    return jnp.minimum(z, 0.0) - jnp.log(1.0 + jnp.exp(-jnp.abs(z)))


def _tri(lower):
    r = lax.broadcasted_iota(jnp.int32, (CHUNK, CHUNK), 0)
    c = lax.broadcasted_iota(jnp.int32, (CHUNK, CHUNK), 1)
    return jnp.where(r >= c if lower else r <= c, 1.0, 0.0).astype(F32)


def _gla_fwd(proj, wa_pad, b_alpha, g_out, bsz):
    t = proj.shape[0]
    s = t // bsz
    nc = s // CHUNK

    def body(q_ref, k_ref, v_ref, gg_ref, ms_ref, wa_ref, ba_ref, go_ref, o_ref, og_ref, st_ref, la, state):
        z = _dot(ms_ref[...].astype(BF16), wa_ref[...], NN) + ba_ref[...]
        la[...] = _log_sigmoid(z) * (1.0 / GTAU)
        state[...] = jnp.zeros_like(state)
        low = _tri(True)
        gout = go_ref[...]

        def chunk(n, carry):
            rows = pl.ds(pl.multiple_of(n * CHUNK, CHUNK), CHUNK)
            lac = la[rows, :]
            cum = _dot(low, lac, NN, lax.Precision.HIGHEST)
            ce = jnp.sum(lac, axis=0, keepdims=True)
            kd = (k_ref[rows, :] * jnp.exp(ce - cum)).astype(BF16)
            new = state[...] * jnp.exp(ce) + _dot(v_ref[rows, :].astype(BF16), kd, TN)
            state[...] = new
            st_ref[pl.ds(pl.multiple_of(n * GDV, GDV), GDV), :] = new
            qs = (q_ref[rows, :] * (GDK ** -0.5)).astype(BF16)
            o = _dot(qs, new.astype(BF16), NT)
            o_ref[rows, :] = o
            ro = lax.rsqrt(jnp.mean(o * o, axis=1, keepdims=True) + EPS)
            og_ref[rows, :] = (((o * ro) * gout) * _silu(gg_ref[rows, :])).astype(BF16)
            return carry

        lax.fori_loop(0, nc, chunk, 0)

    hk = pl.BlockSpec((s, GDK), lambda b, h: (b, h))
    return pl.pallas_call(
        body, name="gla_fwd", grid=(bsz, GH),
        in_specs=[hk, pl.BlockSpec((s, GDK), lambda b, h: (b, GH + h)), pl.BlockSpec((s, GDV), lambda b, h: (b, 4 + h)),
                  pl.BlockSpec((s, GDV), lambda b, h: (b, 8 + h)), pl.BlockSpec((s, LANES), lambda b, h: (b, 43)),
                  pl.BlockSpec((LANES, GDK), lambda b, h: (0, h)), pl.BlockSpec((1, GDK), lambda b, h: (0, h)),
                  pl.BlockSpec((1, GDV), lambda b, h: (0, 0))],
        out_specs=[pl.BlockSpec((s, GDV), lambda b, h: (b, h)), pl.BlockSpec((s, GDV), lambda b, h: (b, h)),
                   pl.BlockSpec((nc * GDV, GDK), lambda b, h: (b * GH + h, 0))],
        out_shape=[jax.ShapeDtypeStruct((t, GH * GDV), F32), jax.ShapeDtypeStruct((t, GH * GDV), BF16),
                   jax.ShapeDtypeStruct((bsz * GH * nc * GDV, GDK), F32)],
        scratch_shapes=[pltpu.VMEM((s, GDK), F32), pltpu.VMEM((GDV, GDK), F32)],
        compiler_params=_cparams(("arbitrary", "arbitrary")),
    )(proj, proj, proj, proj, proj, wa_pad, b_alpha, g_out)


def _gla_bwd(dog, o, proj, wa_pad, b_alpha, g_out, states, bsz):
    t = proj.shape[0]
    s = t // bsz
    nc = s // CHUNK

    def body(dog_ref, o_ref, q_ref, k_ref, v_ref, gg_ref, ms_ref, wa_ref, ba_ref, go_ref, st_ref,
             dq_ref, dk_ref, dv_ref, dgg_ref, dz_ref, dba, dgo, zs, la, carry_g):
        @pl.when(pl.program_id(1) == 0)
        def _():
            dba[...] = jnp.zeros_like(dba)
            dgo[...] = jnp.zeros_like(dgo)

        z = _dot(ms_ref[...].astype(BF16), wa_ref[...], NN) + ba_ref[...]
        zs[...] = z
        la[...] = _log_sigmoid(z) * (1.0 / GTAU)
        carry_g[...] = jnp.zeros_like(carry_g)
        low, upp = _tri(True), _tri(False)
        gout = go_ref[...]
        last_row = lax.broadcasted_iota(jnp.int32, (CHUNK, GDK), 0) == CHUNK - 1

        def chunk(step, carry):
            n = nc - 1 - step
            rows = pl.ds(pl.multiple_of(n * CHUNK, CHUNK), CHUNK)
            lac = la[rows, :]
            cum = _dot(low, lac, NN, lax.Precision.HIGHEST)
            ce = jnp.sum(lac, axis=0, keepdims=True)
            e = jnp.exp(ce - cum)
            dec = jnp.exp(ce)
            kf = k_ref[rows, :]
            kd = (kf * e).astype(BF16)
            vv = v_ref[rows, :].astype(BF16)
            qs = (q_ref[rows, :] * (GDK ** -0.5)).astype(BF16)
            ov = o_ref[rows, :]
            ro = lax.rsqrt(jnp.mean(ov * ov, axis=1, keepdims=True) + EPS)
            on = ov * ro
            gg = gg_ref[rows, :]
            sg = _sigmoid(gg)
            dogv = dog_ref[rows, :]
            dgg_ref[rows, :] = (dogv * (on * gout) * (sg * (1.0 + gg * (1.0 - sg)))).astype(BF16)
            t1 = dogv * (gg * sg)
            dgo[0:1, :] += jnp.sum(t1 * on, axis=0, keepdims=True)
            don = t1 * gout
            do = ro * (don - on * jnp.mean(don * on, axis=1, keepdims=True))
            dob = do.astype(BF16)
            st_n = st_ref[pl.ds(pl.multiple_of(n * GDV, GDV), GDV), :]
            dq_ref[rows, :] = (_dot(dob, st_n.astype(BF16), NN) * (GDK ** -0.5)).astype(BF16)
            dn = carry_g[...] + _dot(dob, qs, TN)
            prev = jnp.maximum(n - 1, 0)
            st_p = st_ref[pl.ds(pl.multiple_of(prev * GDV, GDV), GDV), :] * jnp.where(n > 0, 1.0, 0.0)
            ddec = jnp.sum(dn * st_p, axis=0, keepdims=True)
            dnb = dn.astype(BF16)
            dkd = _dot(vv, dnb, NN)
            dv_ref[rows, :] = _dot(kd, dnb, NT).astype(BF16)
            dk_ref[rows, :] = (dkd * e).astype(BF16)
            w = dkd * kf * e
            dce = jnp.sum(w, axis=0, keepdims=True) + ddec * dec
            dcum = jnp.where(last_row, dce - w, -w)
            dla = _dot(upp, dcum, NN, lax.Precision.HIGHEST)
            dz = dla * (1.0 / GTAU) * _sigmoid(-zs[rows, :])
            dba[0:1, :] += jnp.sum(dz, axis=0, keepdims=True)
            dz_ref[rows, :] = dz.astype(BF16)
            carry_g[...] = dn * dec
            return carry

        lax.fori_loop(0, nc, chunk, 0)

    hv = pl.BlockSpec((s, GDV), lambda h, b: (b, h))
    hk = pl.BlockSpec((s, GDK), lambda h, b: (b, h))
    return pl.pallas_call(
        body, name="gla_bwd", grid=(GH, bsz),
        in_specs=[hv, hv, hk, pl.BlockSpec((s, GDK), lambda h, b: (b, GH + h)),
                  pl.BlockSpec((s, GDV), lambda h, b: (b, 4 + h)), pl.BlockSpec((s, GDV), lambda h, b: (b, 8 + h)),
                  pl.BlockSpec((s, LANES), lambda h, b: (b, 43)), pl.BlockSpec((LANES, GDK), lambda h, b: (0, h)),
                  pl.BlockSpec((1, GDK), lambda h, b: (0, h)), pl.BlockSpec((1, GDV), lambda h, b: (0, 0)),
                  pl.BlockSpec((nc * GDV, GDK), lambda h, b: (b * GH + h, 0))],
        out_specs=[hk, hk, hv, hv, hk, pl.BlockSpec((8, GDK), lambda h, b: (0, h)),
                   pl.BlockSpec((8, GDV), lambda h, b: (h, 0))],
        out_shape=[jax.ShapeDtypeStruct((t, GH * GDK), BF16), jax.ShapeDtypeStruct((t, GH * GDK), BF16),
                   jax.ShapeDtypeStruct((t, GH * GDV), BF16), jax.ShapeDtypeStruct((t, GH * GDV), BF16),
                   jax.ShapeDtypeStruct((t, GH * GDK), BF16), jax.ShapeDtypeStruct((8, GH * GDK), F32),
                   jax.ShapeDtypeStruct((8 * GH, GDV), F32)],
        scratch_shapes=[pltpu.VMEM((s, GDK), F32), pltpu.VMEM((s, GDK), F32), pltpu.VMEM((GDV, GDK), F32)],
        compiler_params=_cparams(("arbitrary", "arbitrary")),
    )(dog, o, proj, proj, proj, proj, proj, wa_pad, b_alpha, g_out, states)


def _rope_tables(pos_ref, fr_ref, sg_ref):
    ang = pos_ref[...].astype(F32) * fr_ref[...]
    return jnp.cos(ang), jnp.sin(ang) * sg_ref[...]


def _partner(x):
    lane = lax.broadcasted_iota(jnp.int32, x.shape, 1)
    return jnp.where(lane < NOPE + ROPE // 2, pltpu.roll(x, LANES - ROPE // 2, 1), pltpu.roll(x, ROPE // 2, 1))


def _mla_rows(t):
    return _tile(t, 256)


def _mla_pre_fwd(proj, pos, fr, sg, q_lat_g, kv_lat_g, qn_g, kn_g, wuq, wukv):
    t = proj.shape[0]
    tr = _mla_rows(t)

    def body(cq_ref, ckv_ref, ms_ref, pos_ref, fr_ref, sg_ref, qlg, kvlg, qng, kng, wuq_ref, wukv_ref, q_out, k_out, v_out):
        cos, sin = _rope_tables(pos_ref, fr_ref, sg_ref)
        cq = cq_ref[...]
        cqn = (cq * lax.rsqrt(jnp.mean(cq * cq, axis=1, keepdims=True) + EPS) * qlg[...]).astype(BF16)
        ckv = ckv_ref[...]
        ckvn = (ckv * lax.rsqrt(jnp.mean(ckv * ckv, axis=1, keepdims=True) + EPS) * kvlg[...]).astype(BF16)
        lane = lax.broadcasted_iota(jnp.int32, (tr, HP), 1)
        kpe = jnp.where((lane >= NOPE) & (lane < MQK), ms_ref[...], 0.0)
        v_out[...] = _dot(ckvn, wukv_ref[:, MH * HP:], NN).astype(BF16)
        for h in range(MH):
            cols = slice(h * HP, (h + 1) * HP)
            qh = _dot(cqn, wuq_ref[:, cols], NN)
            qn = qh * lax.rsqrt(jnp.sum(qh * qh, axis=1, keepdims=True) * (1.0 / MQK) + EPS) * qng[...]
            q_out[:, cols] = (qn * cos + _partner(qn) * sin).astype(BF16)
            kh = _dot(ckvn, wukv_ref[:, cols], NN) + kpe
            kn = kh * lax.rsqrt(jnp.sum(kh * kh, axis=1, keepdims=True) * (1.0 / MQK) + EPS) * kng[...]
            k_out[:, cols] = (kn * cos + _partner(kn) * sin).astype(BF16)

    def full(a):
        return pl.BlockSpec(a.shape, lambda i: (0, 0))

    wide = pl.BlockSpec((tr, MH * HP), lambda i: (i, 0))
    return pl.pallas_call(
        body, name="mla_pre_fwd", grid=(t // tr,),
        in_specs=[pl.BlockSpec((tr, MQR), lambda i: (i, 20)), pl.BlockSpec((tr, MKVR), lambda i: (i, 42)),
                  pl.BlockSpec((tr, LANES), lambda i: (i, 43)), pl.BlockSpec((tr, 1), lambda i: (i, 0)),
                  full(fr), full(sg), full(q_lat_g), full(kv_lat_g), full(qn_g), full(kn_g), full(wuq), full(wukv)],
        out_specs=[wide, wide, wide],
        out_shape=[jax.ShapeDtypeStruct((t, MH * HP), BF16)] * 3,
        compiler_params=_cparams(("arbitrary",)),
    )(proj, proj, proj, pos, fr, sg, q_lat_g, kv_lat_g, qn_g, kn_g, wuq, wukv)


def _mla_pre_bwd(dq2, dk2, dv2, dmisc_gla, proj, pos, fr, sg, q_lat_g, kv_lat_g, qn_g, kn_g, wuq, wukv):
    t = proj.shape[0]
    tr = _mla_rows(t)

    def body(dq_ref, dk_ref, dv_ref, dmg_ref, cq_ref, ckv_ref, ms_ref, pos_ref, fr_ref, sg_ref, qlg, kvlg, qng, kng,
             wuq_ref, wukv_ref, dcq_ref, dckv_ref, dms_ref, dwuq, dwukv, acc, dqf, dkvf):
        @pl.when(pl.program_id(0) == 0)
        def _():
            dwuq[...] = jnp.zeros_like(dwuq)
            dwukv[...] = jnp.zeros_like(dwukv)
            acc[...] = jnp.zeros_like(acc)

        cos, sin = _rope_tables(pos_ref, fr_ref, sg_ref)
        cq = cq_ref[...]
        rc = lax.rsqrt(jnp.mean(cq * cq, axis=1, keepdims=True) + EPS)
        xc = cq * rc
        cqn = (xc * qlg[...]).astype(BF16)
        ckv = ckv_ref[...]
        rkv = lax.rsqrt(jnp.mean(ckv * ckv, axis=1, keepdims=True) + EPS)
        xkv = ckv * rkv
        ckvn = (xkv * kvlg[...]).astype(BF16)
        lane = lax.broadcasted_iota(jnp.int32, (tr, HP), 1)
        is_rope = (lane >= NOPE) & (lane < MQK)
        kpe = jnp.where(is_rope, ms_ref[...], 0.0)
        dkpe = jnp.zeros((tr, HP), F32)
        dqng = jnp.zeros((1, HP), F32)
        dkng = jnp.zeros((1, HP), F32)
        for h in range(MH):
            cols = slice(h * HP, (h + 1) * HP)
            qh = _dot(cqn, wuq_ref[:, cols], NN)
            rq = lax.rsqrt(jnp.sum(qh * qh, axis=1, keepdims=True) * (1.0 / MQK) + EPS)
            xq = qh * rq
            dy = dq_ref[:, cols].astype(F32)
            dqn = dy * cos - _partner(dy) * sin
            dqng += jnp.sum(dqn * xq, axis=0, keepdims=True)
            tq = dqn * qng[...]
            dqf[:, cols] = (rq * (tq - xq * (jnp.sum(tq * xq, axis=1, keepdims=True) * (1.0 / MQK)))).astype(BF16)
            kh = _dot(ckvn, wukv_ref[:, cols], NN) + kpe
            rk = lax.rsqrt(jnp.sum(kh * kh, axis=1, keepdims=True) * (1.0 / MQK) + EPS)
            xk = kh * rk
            dy = dk_ref[:, cols].astype(F32)
            dkn = dy * cos - _partner(dy) * sin
            dkng += jnp.sum(dkn * xk, axis=0, keepdims=True)
            tk = dkn * kng[...]
            dkh = rk * (tk - xk * (jnp.sum(tk * xk, axis=1, keepdims=True) * (1.0 / MQK)))
            dkvf[:, cols] = jnp.where(lane < NOPE, dkh, 0.0).astype(BF16)
            dkpe += jnp.where(is_rope, dkh, 0.0)
        dkvf[:, MH * HP:] = dv_ref[...]
        acc[2:3, 0:HP] += dqng
        acc[3:4, 0:HP] += dkng
        dms_ref[...] = (dmg_ref[...] + dkpe).astype(BF16)

        dqfv = dqf[...]
        dwuq[...] += _dot(cqn, dqfv, TN)
        dcqn = _dot(dqfv, wuq_ref[...], NT)
        acc[0:1, :] += jnp.sum(dcqn * xc, axis=0, keepdims=True)
        tc = dcqn * qlg[...]
        dcq_ref[...] = (rc * (tc - xc * jnp.mean(tc * xc, axis=1, keepdims=True))).astype(BF16)

        dkvfv = dkvf[...]
        dwukv[...] += _dot(ckvn, dkvfv, TN)
        dckvn = _dot(dkvfv, wukv_ref[...], NT)
        acc[1:2, 0:MKVR] += jnp.sum(dckvn * xkv, axis=0, keepdims=True)
        tkv = dckvn * kvlg[...]
        dckv_ref[...] = (rkv * (tkv - xkv * jnp.mean(tkv * xkv, axis=1, keepdims=True))).astype(BF16)

    def full(a):
        return pl.BlockSpec(a.shape, lambda i: (0, 0))

    wide = pl.BlockSpec((tr, MH * HP), lambda i: (i, 0))
    narrow = pl.BlockSpec((tr, LANES), lambda i: (i, 0))
    return pl.pallas_call(
        body, name="mla_pre_bwd", grid=(t // tr,),
        in_specs=[wide, wide, wide, narrow,
                  pl.BlockSpec((tr, MQR), lambda i: (i, 20)), pl.BlockSpec((tr, MKVR), lambda i: (i, 42)),
                  pl.BlockSpec((tr, LANES), lambda i: (i, 43)), pl.BlockSpec((tr, 1), lambda i: (i, 0)),
                  full(fr), full(sg), full(q_lat_g), full(kv_lat_g), full(qn_g), full(kn_g), full(wuq), full(wukv)],
        out_specs=[pl.BlockSpec((tr, MQR), lambda i: (i, 0)), narrow, narrow,
                   pl.BlockSpec((MQR, MH * HP), lambda i: (0, 0)), pl.BlockSpec((MKVR, 2 * MH * HP), lambda i: (0, 0)),
                   pl.BlockSpec((8, MQR), lambda i: (0, 0))],
        out_shape=[jax.ShapeDtypeStruct((t, MQR), BF16), jax.ShapeDtypeStruct((t, MKVR), BF16),
                   jax.ShapeDtypeStruct((t, LANES), BF16), jax.ShapeDtypeStruct((MQR, MH * HP), F32),
                   jax.ShapeDtypeStruct((MKVR, 2 * MH * HP), F32), jax.ShapeDtypeStruct((8, MQR), F32)],
        scratch_shapes=[pltpu.VMEM((tr, MH * HP), BF16), pltpu.VMEM((tr, 2 * MH * HP), BF16)],
        compiler_params=_cparams(("arbitrary",)),
    )(dq2, dk2, dv2, dmisc_gla, proj, proj, proj, pos, fr, sg, q_lat_g, kv_lat_g, qn_g, kn_g, wuq, wukv)


ATT_TK = 256
NEG = -1e30
LOG2E = 1.4426950408889634


def _att_mask(q0, k0, tq, tk):
    qc = (q0 + lax.broadcasted_iota(jnp.int32, (tq, tk), 0)) // CHUNK
    kc = (k0 + lax.broadcasted_iota(jnp.int32, (tq, tk), 1)) // CHUNK
    return kc <= qc


def _att_tiles(s):
    tk = _tile(s, ATT_TK)
    return tk // 2, tk


def _attn_fwd(q2, k2, v2, bsz):
    t = q2.shape[0]
    s = t // bsz
    tq, tk = _att_tiles(s)
    nq = s // tq
    scale = MQK ** -0.5
    c2 = scale * LOG2E

    def body(q_ref, k_ref, v_ref, o_ref, lse_ref):
        def q_loop(qi, carry):
            q0 = pl.multiple_of(qi * tq, tq)
            q = q_ref[pl.ds(q0, tq), :]

            def tile(kj, c, masked):
                m, l, acc = c
                k0 = pl.multiple_of(kj * tk, tk)
                sc = _dot(q, k_ref[pl.ds(k0, tk), :], NT)
                if masked:
                    sc = jnp.where(_att_mask(q0, k0, tq, tk), sc, NEG)
                m_new = jnp.maximum(m, jnp.max(sc, axis=1, keepdims=True))
                alpha = jnp.exp2((m - m_new) * c2)
                p = jnp.exp2((sc - m_new) * c2)
                l = alpha * l + jnp.sum(p, axis=1, keepdims=True)
                acc = alpha * acc + _dot(p.astype(BF16), v_ref[pl.ds(k0, tk), :], NN)
                return m_new, l, acc

            init = (jnp.full((tq, 1), NEG, F32), jnp.zeros((tq, 1), F32), jnp.zeros((tq, HP), F32))
            n_full = (qi * tq) // tk
            c = lax.fori_loop(0, n_full, lambda kj, c: tile(kj, c, False), init)
            m, l, acc = tile(n_full, c, True)
            o_ref[pl.ds(q0, tq), :] = (acc / l).astype(BF16)
            lse_ref[pl.ds(q0, tq), :] = jnp.broadcast_to(m * scale + jnp.log(l), (tq, HP))
            return carry

        lax.fori_loop(0, nq, q_loop, 0)

    spec = pl.BlockSpec((s, HP), lambda b, h: (b, h))
    return pl.pallas_call(
        body, name="attn_fwd", grid=(bsz, MH), in_specs=[spec] * 3, out_specs=[spec, spec],
        out_shape=[jax.ShapeDtypeStruct((t, MH * HP), BF16), jax.ShapeDtypeStruct((t, MH * HP), F32)],
        compiler_params=_cparams(("arbitrary", "arbitrary")),
    )(q2, k2, v2)


def _attn_bwd(q2, k2, v2, do2, o2, lse2, bsz):
    t = q2.shape[0]
    s = t // bsz
    tq, tk = _att_tiles(s)
    nq, nk, per = s // tq, s // tk, tk // tq
    scale = MQK ** -0.5
    c2 = scale * LOG2E

    def body(q_ref, k_ref, v_ref, do_ref, o_ref, lse_ref, dq_ref, dk_ref, dv_ref, dq_acc, delta, lse_b2):
        dq_acc[...] = jnp.zeros_like(dq_acc)

        def d_loop(i, carry):
            rows = pl.ds(pl.multiple_of(i * tq, tq), tq)
            dl = jnp.sum(do_ref[rows, :].astype(F32) * o_ref[rows, :].astype(F32), axis=1, keepdims=True)
            delta[rows, :] = jnp.broadcast_to(dl, (tq, HP))
            lse_b2[rows, :] = lse_ref[rows, :] * LOG2E
            return carry

        lax.fori_loop(0, nq, d_loop, 0)

        def k_loop(kj, carry):
            k0 = pl.multiple_of(kj * tk, tk)
            kk = k_ref[pl.ds(k0, tk), :]
            vv = v_ref[pl.ds(k0, tk), :]

            def tile(qi, c, masked):
                dk, dv = c
                q0 = pl.multiple_of(qi * tq, tq)
                rows = pl.ds(q0, tq)
                q = q_ref[rows, :]
                do = do_ref[rows, :]
                p = jnp.exp2(_dot(q, kk, NT) * c2 - lse_b2[rows, 0:1])
                if masked:
                    p = jnp.where(_att_mask(q0, k0, tq, tk), p, 0.0)
                dv = dv + _dot(p.astype(BF16), do, TN)
                ds = (p * (_dot(do, vv, NT) - delta[rows, 0:1])).astype(BF16)
                dq_acc[rows, :] += _dot(ds, kk, NN)
                dk = dk + _dot(ds, q, TN)
                return dk, dv

            zero = jnp.zeros((tk, HP), F32)
            c = (zero, zero)
            for u in range(per):
                c = tile(kj * per + u, c, True)
            dk, dv = lax.fori_loop((kj + 1) * per, nq, lambda qi, c: tile(qi, c, False), c)
            dk_ref[pl.ds(k0, tk), :] = (dk * scale).astype(BF16)
            dv_ref[pl.ds(k0, tk), :] = dv.astype(BF16)
            return carry

        lax.fori_loop(0, nk, k_loop, 0)
        dq_ref[...] = (dq_acc[...] * scale).astype(BF16)

    spec = pl.BlockSpec((s, HP), lambda b, h: (b, h))
    return pl.pallas_call(
        body, name="attn_bwd", grid=(bsz, MH), in_specs=[spec] * 6, out_specs=[spec] * 3,
        out_shape=[jax.ShapeDtypeStruct((t, MH * HP), BF16)] * 3,
        scratch_shapes=[pltpu.VMEM((s, HP), F32)] * 3,
        compiler_params=_cparams(("arbitrary", "arbitrary")),
    )(q2, k2, v2, do2, o2, lse2)


def _perm_w_in_t(w):
    z = lambda n: jnp.zeros((n, w.shape[1]), w.dtype)
    return jnp.concatenate([w[:3072], w[3504:5552], w[3088:3344], w[3344:3472], w[3072:3088], z(48), w[3472:3504], z(32)],
                           axis=0)


def _unperm_w_in_t(g):
    return jnp.concatenate([g[:3072], g[5504:5520], g[5120:5376], g[5376:5504], g[5568:5600], g[3072:5120]], axis=0)


def _pad_wa(w):
    return jnp.pad(w, ((0, LANES - GLR), (0, 0)))


def _pad_wuq(w):
    return jnp.pad(w.reshape(MQR, MH, MQK), ((0, 0), (0, 0), (0, HP - MQK))).reshape(MQR, MH * HP)


def _unpad_wuq(g):
    return g.reshape(MQR, MH, HP)[:, :, :MQK].reshape(MQR, MH * MQK)


def _pad_wukv(w):
    w3 = w.reshape(MKVR, MH, NOPE + MV)
    kp = jnp.pad(w3[:, :, :NOPE], ((0, 0), (0, 0), (0, HP - NOPE))).reshape(MKVR, MH * HP)
    vp = jnp.pad(w3[:, :, NOPE:], ((0, 0), (0, 0), (0, HP - MV))).reshape(MKVR, MH * HP)
    return jnp.concatenate([kp, vp], axis=1)


def _unpad_wukv(g):
    kp = g[:, :MH * HP].reshape(MKVR, MH, HP)[:, :, :NOPE]
    vp = g[:, MH * HP:].reshape(MKVR, MH, HP)[:, :, :MV]
    return jnp.concatenate([kp, vp], axis=2).reshape(MKVR, MH * (NOPE + MV))


def _pad_wo(w):
    return jnp.pad(w.reshape(MH, MV, D), ((0, 0), (0, HP - MV), (0, 0))).reshape(MH * HP, D)


def _unpad_wo(g):
    return g.reshape(MH, HP, D)[:, :MV, :].reshape(MH * MV, D)


def _pad_lanes(v, n=HP):
    return jnp.pad(v, ((0, 0), (0, n - v.shape[1])))


def _local_step(x, c, positions, tgt, wt, sp):
    bsz, s, _ = x.shape
    t = bsz * s
    x2 = x.reshape(t, D)
    tgt2 = tgt.reshape(t, D)
    pos = positions.reshape(t, 1)
    fr16 = ROPE_THETA ** (-jnp.arange(0, ROPE, 2, dtype=F32) / ROPE)
    zero = lambda n: jnp.zeros((n,), F32)
    fr = jnp.concatenate([zero(NOPE), fr16, fr16, zero(HP - MQK)]).reshape(1, HP)
    sg = jnp.concatenate([zero(NOPE), -jnp.ones((ROPE // 2,), F32), jnp.ones((ROPE // 2,), F32), zero(HP - MQK)]).reshape(1, HP)

    w_in_t = _perm_w_in_t(wt["w_in"])
    wa_pad = _pad_wa(wt["gla_w_alpha"].T)
    wuq = _pad_wuq(wt["mla_w_uq"].T)
    wukv = _pad_wukv(wt["mla_w_ukv"].T)
    wo_pad = _pad_wo(wt["mla_w_o"])
    qn_g, kn_g = _pad_lanes(sp["mla_qn_g"]), _pad_lanes(sp["mla_kn_g"])

    c8 = jnp.pad(c, ((0, 8 - bsz), (0, 0)))
    mod8 = _mm(c8, wt["w_ada"], "nt", (F32,), "ada_fwd", pro=_silu, extras=(jnp.broadcast_to(sp["b_ada"], (8, 6 * D)),),
               epi=lambda acc, bias: (acc + bias,))
    mod3 = mod8[:bsz].reshape(bsz, 6, D)

    _, h = _norm_mod_fwd(x2, sp["norm1_g"], mod3, 0, 1, "norm1_fwd")
    proj = _mm(h, w_in_t, "nt", (F32,), "proj_fwd")
    o_gla, og, states = _gla_fwd(proj, wa_pad, sp["gla_b_alpha"], sp["gla_out_norm_g"], bsz)
    y_a = _mm(og, wt["gla_w_o"], "nn", (F32,), "gla_out_fwd")
    q2, k2, v2 = _mla_pre_fwd(proj, pos, fr, sg, sp["mla_q_lat_g"], sp["mla_kv_lat_g"], qn_g, kn_g, wuq, wukv)
    o2, lse2 = _attn_fwd(q2, k2, v2, bsz)
    y_b = _mm(o2, wo_pad, "nn", (F32,), "mla_out_fwd")
    mix = _merge_fwd(proj, sp["b_merge"], y_a, y_b)
    mixed = _mm(mix, wt["w_out"], "nn", (F32,), "w_out_fwd")

    x1, h2 = _norm_mod_fwd(x2, sp["norm2_g"], mod3, 3, 4, "norm2_fwd", mixed=mixed, i_gate=2)
    a, f = _mm(h2, wt["mlp_w1"], "nt", (F32, BF16), "mlp1_fwd",
               epi=lambda acc: (acc, jnp.square(jnp.maximum(acc, 0.0))))
    ff = _mm(f, wt["mlp_w2"], "nn", (F32,), "mlp2_fwd")
    dy, dff, acc_g2, acc_loss = _loss_head(x1, ff, tgt2, mod3)

    gw = {}
    gw["mlp_w2"] = _mm(f, dff, "tn", (BF16,), "mlp2_dw")
    da = _mm(dff, wt["mlp_w2"], "nt", (BF16,), "mlp2_dx", extras=(a,),
             epi=lambda acc, av: (acc * (2.0 * jnp.maximum(av, 0.0)),))
    gw["mlp_w1"] = _mm(da, h2, "tn", (BF16,), "mlp1_dw")
    dh2 = _mm(da, wt["mlp_w1"], "nn", (F32,), "mlp1_dx")
    dx1, dmixed, accb2, accg2 = _norm_mod_bwd(x1, dh2, dy, sp["norm2_g"], mod3, 3, 4, "norm2_bwd", mixed=mixed, i_gate=2)

    gw["w_out"] = _mm(mix, dmixed, "tn", (BF16,), "w_out_dw")
    dmix = _mm(dmixed, wt["w_out"], "nt", (F32,), "w_out_dx")
    dy_a, dy_b, dlogits, acc_bm = _merge_bwd(dmix, proj, sp["b_merge"], y_a, y_b)
    gw["gla_w_o"] = _mm(og, dy_a, "tn", (BF16,), "gla_out_dw")
    dog = _mm(dy_a, wt["gla_w_o"], "nt", (F32,), "gla_out_dx")
    dq_g, dk_g, dv_g, dgg, dz, acc_ba, acc_go = _gla_bwd(dog, o_gla, proj, wa_pad, sp["gla_b_alpha"],
                                                         sp["gla_out_norm_g"], states, bsz)
    gw["gla_w_alpha"] = _mm(proj, dz, "tn", (F32,), "gla_alpha_dw", a_off=43, m=LANES)[:GLR].T.astype(BF16)
    dmisc_gla = _mm(dz, wa_pad, "nt", (F32,), "gla_alpha_dx")
    gw["mla_w_o"] = _unpad_wo(_mm(o2, dy_b, "tn", (BF16,), "mla_out_dw"))
    do2 = _mm(dy_b, wo_pad, "nt", (BF16,), "mla_out_dx")
    dq2, dk2, dv2 = _attn_bwd(q2, k2, v2, do2, o2, lse2, bsz)
    dcq, dckv, dmisc, gwuq, gwukv, acc_mla = _mla_pre_bwd(dq2, dk2, dv2, dmisc_gla, proj, pos, fr, sg, sp["mla_q_lat_g"],
                                                         sp["mla_kv_lat_g"], qn_g, kn_g, wuq, wukv)
    gw["mla_w_uq"] = _unpad_wuq(gwuq).T.astype(BF16)
    gw["mla_w_ukv"] = _unpad_wukv(gwukv).T.astype(BF16)
    dproj = jnp.concatenate([dq_g, dk_g, dv_g, dgg, dlogits, dcq, dckv, dmisc], axis=1)
    gw["w_in"] = _unperm_w_in_t(_mm(dproj, h, "tn", (BF16,), "proj_dw"))
    dh = _mm(dproj, w_in_t, "nn", (F32,), "proj_dx")
    grad_x, accb1, accg1 = _norm_mod_bwd(x2, dh, dx1, sp["norm1_g"], mod3, 0, 1, "norm1_bwd")

    dmod = jnp.stack([accb1[:, 0], accb1[:, 1], accb2[:, 2], accb2[:, 0], accb2[:, 1], acc_g2[:, 0]], axis=1)
    dmod8 = jnp.pad(dmod.reshape(bsz, 6 * D), ((0, 8 - bsz), (0, 0)))
    gw["w_ada"] = _mm(dmod8, c8, "tn", (BF16,), "ada_dw", pro_b=_silu)

    rows = {
        "dmod": dmod.reshape(bsz * 6, D),
        "norm1_g": accg1[0:1], "norm2_g": accg2[0:1],
        "b_merge": acc_bm[0:1].reshape(2, D),
        "gla_b_alpha": _pad_lanes(acc_ba[0:1], D),
        "gla_out_norm_g": _pad_lanes(acc_go.reshape(GH, 8, GDV)[:, 0, :], D),
        "mla_q_lat_g": _pad_lanes(acc_mla[0:1], D), "mla_kv_lat_g": _pad_lanes(acc_mla[1:2], D),
        "mla_qn_g": _pad_lanes(acc_mla[2:3], D), "mla_kn_g": _pad_lanes(acc_mla[3:4], D),
        "loss": acc_loss[0:1],
    }
    return grad_x.reshape(bsz, s, D), gw, rows


HBM_SPEC = pl.BlockSpec(memory_space=pltpu.HBM)


def _all_gather(p, name):
    r, cdim = p.shape

    def body(p_ref, out_ref, send_sems, recv_sems, local_sem):
        x, y, c = lax.axis_index("x"), lax.axis_index("y"), lax.axis_index("c")
        me, sibling = (x, y, c), (x, y, 1 - c)
        chips = [(1 - x, y), (x, 1 - y), (1 - x, 1 - y)]

        def slot(px, py, pc):
            return out_ref.at[4 * px + 2 * py + pc]

        def copy(k, block, to, src=None):
            return pltpu.make_async_remote_copy(
                src_ref=slot(*block) if src is None else src, dst_ref=slot(*block),
                send_sem=send_sems.at[k], recv_sem=recv_sems.at[k], device_id=to, device_id_type=MESH)

        mine = pltpu.make_async_copy(p_ref, slot(*me), local_sem)
        mine.start()
        first = [copy(0, me, sibling, src=p_ref)] + [copy(1 + j, me, (*chip, c), src=p_ref) for j, chip in enumerate(chips)]
        for cp in first:
            cp.start()
        passed = [copy(4 + j, (*chip, c), sibling) for j, chip in enumerate(chips)]
        for j, chip in enumerate(chips):
            copy(1 + j, (*chip, c), me).wait_recv()
            passed[j].start()
        copy(0, sibling, me).wait_recv()
        for j, chip in enumerate(chips):
            copy(4 + j, (*chip, 1 - c), me).wait_recv()
        for cp in first + passed:
            cp.wait_send()
        mine.wait()

    return pl.pallas_call(
        body, name=name, out_shape=jax.ShapeDtypeStruct((N_DEV, r, cdim), p.dtype),
        in_specs=[HBM_SPEC], out_specs=HBM_SPEC,
        scratch_shapes=[pltpu.SemaphoreType.DMA((7,)), pltpu.SemaphoreType.DMA((7,)), pltpu.SemaphoreType.DMA(())],
    )(p)


def _sibling_exchange(g):
    def body(g_ref, out_ref, send_sem, recv_sem):
        x, y, c = lax.axis_index("x"), lax.axis_index("y"), lax.axis_index("c")
        cp = pltpu.make_async_remote_copy(src_ref=g_ref, dst_ref=out_ref, send_sem=send_sem, recv_sem=recv_sem,
                                          device_id=(x, y, 1 - c), device_id_type=MESH)
        cp.start()
        cp.wait()

    return pl.pallas_call(
        body, name="rs_sibling_exchange", out_shape=jax.ShapeDtypeStruct(g.shape, g.dtype),
        in_specs=[HBM_SPEC], out_specs=HBM_SPEC,
        scratch_shapes=[pltpu.SemaphoreType.DMA(()), pltpu.SemaphoreType.DMA(())],
    )(g)


def _chip_exchange(tsum):
    def body(t_ref, out_ref, send_sems, recv_sems, local_sem):
        x, y, c = lax.axis_index("x"), lax.axis_index("y"), lax.axis_index("c")
        my_chip = 2 * x + y
        chips = [(1 - x, y), (x, 1 - y), (1 - x, 1 - y)]
        mine = pltpu.make_async_copy(t_ref.at[my_chip], out_ref.at[my_chip], local_sem)
        mine.start()
        sends = [pltpu.make_async_remote_copy(
            src_ref=t_ref.at[2 * px + py], dst_ref=out_ref.at[my_chip], send_sem=send_sems.at[j], recv_sem=recv_sems.at[j],
            device_id=(px, py, c), device_id_type=MESH) for j, (px, py) in enumerate(chips)]
        for cp in sends:
            cp.start()
        for j, (px, py) in enumerate(chips):
            pltpu.make_async_remote_copy(
                src_ref=t_ref.at[my_chip], dst_ref=out_ref.at[2 * px + py], send_sem=send_sems.at[j],
                recv_sem=recv_sems.at[j], device_id=(px, py, c), device_id_type=MESH).wait_recv()
        for cp in sends:
            cp.wait_send()
        mine.wait()

    return pl.pallas_call(
        body, name="rs_chip_exchange", out_shape=jax.ShapeDtypeStruct(tsum.shape, tsum.dtype),
        in_specs=[HBM_SPEC], out_specs=HBM_SPEC,
        scratch_shapes=[pltpu.SemaphoreType.DMA((3,)), pltpu.SemaphoreType.DMA((3,)), pltpu.SemaphoreType.DMA(())],
    )(tsum)


def _pair_sum(a, b):
    n, r, cdim = a.shape
    blk = pl.BlockSpec((1, PACK_BLOCK, cdim), lambda j, i: (j, i, 0))

    def body(a_ref, b_ref, o_ref):
        o_ref[...] = (a_ref[...].astype(F32) + b_ref[...].astype(F32)).astype(BF16)

    return pl.pallas_call(
        body, name="rs_pair_sum", grid=(n, r // PACK_BLOCK), in_specs=[blk, blk], out_specs=blk,
        out_shape=jax.ShapeDtypeStruct(a.shape, BF16), compiler_params=_cparams(("arbitrary", "arbitrary")),
    )(a, b)


def _adamw_math(w, g, m, v):
    m = ADAM_B1 * m + (1.0 - ADAM_B1) * g
    v = ADAM_B2 * v + (1.0 - ADAM_B2) * jnp.square(g)
    m_hat = m / (1.0 - ADAM_B1 ** ADAM_STEP)
    v_hat = v / (1.0 - ADAM_B2 ** ADAM_STEP)
    delta = -ADAM_LR * (m_hat / (jnp.sqrt(v_hat) + ADAM_EPS) + ADAM_WD * w)
    return delta, m, v


def _slab_sum(parts):
    _, r, cdim = parts.shape
    blk = pl.BlockSpec((PACK_BLOCK, cdim), lambda i: (i, 0))

    def body(p_ref, g_out):
        g = p_ref[0].astype(F32)
        for j in range(1, 4):
            g = g + p_ref[j].astype(F32)
        g_out[...] = g

    return pl.pallas_call(
        body, name="rs_slab_sum", grid=(r // PACK_BLOCK,),
        in_specs=[pl.BlockSpec((4, PACK_BLOCK, cdim), lambda i: (0, i, 0))], out_specs=blk,
        out_shape=jax.ShapeDtypeStruct((r, cdim), F32), compiler_params=_cparams(("arbitrary",)),
    )(parts)


def _adamw(g, w, m, v, name):
    r, cdim = w.shape
    rb = _tile(r, 256)
    blk = pl.BlockSpec((rb, cdim), lambda i: (i, 0))

    def body(g_ref, w_ref, m_ref, v_ref, d_out, m_out, v_out):
        d_out[...], m_out[...], v_out[...] = _adamw_math(w_ref[...], g_ref[...], m_ref[...], v_ref[...])

    return pl.pallas_call(
        body, name=name, grid=(r // rb,), in_specs=[blk] * 4, out_specs=[blk] * 3,
        out_shape=[jax.ShapeDtypeStruct((r, cdim), F32)] * 3, compiler_params=_cparams(("arbitrary",)),
    )(g, w, m, v)


def _adamw_small(parts, w, m, v):
    def body(p_ref, w_ref, m_ref, v_ref, g_out, d_out, m_out, v_out, loss_out):
        def total(srcs):
            acc = None
            for r in srcs:
                for j in range(N_DEV):
                    term = p_ref[j, r:r + 1, :]
                    acc = term if acc is None else acc + term
            return acc

        for prow, srcs in enumerate(SMALL_SOURCES):
            one = slice(prow, prow + 1)
            g = total(srcs)
            g_out[one, :] = g
            d_out[one, :], m_out[one, :], v_out[one, :] = _adamw_math(w_ref[one, :], g, m_ref[one, :], v_ref[one, :])
        loss_out[...] = jnp.broadcast_to(jnp.sum(total(LOSS_SOURCE), axis=1, keepdims=True), (8, LANES))

    full = lambda shp: pl.BlockSpec(shp, lambda i: (0,) * len(shp))
    return pl.pallas_call(
        body, name="adamw_small", grid=(1,),
        in_specs=[full((N_DEV, SMALL_ROWS, D)), full((16, D)), full((16, D)), full((16, D))],
        out_specs=[full((16, D))] * 4 + [full((8, LANES))],
        out_shape=[jax.ShapeDtypeStruct((16, D), F32)] * 4 + [jax.ShapeDtypeStruct((8, LANES), F32)],
        compiler_params=_cparams(("arbitrary",)),
    )(parts, w, m, v)


def _to_slab(shard, form):
    if form == "N":
        return shard
    return shard.T if form == "T" else shard.T.reshape(-1, D)


def _from_slab(block, form, shard_shape):
    if form == "N":
        return block
    return block.T if form == "T" else block.reshape(shard_shape[1], shard_shape[0]).T


def _gathered_full(g, form, shard_shape):
    if form == "TR":
        return g.reshape(N_DEV * shard_shape[1], shard_shape[0])
    return g.reshape(N_DEV * g.shape[1], D)


def _small_pack(vals):
    rows = []
    for n in SMALL:
        v = vals[n].reshape(-1)
        k = -(-v.shape[0] // D)
        rows.append(jnp.pad(v, (0, k * D - v.shape[0])).reshape(k, D))
    return jnp.concatenate(rows, axis=0)


def _small_unpack(packed, shapes):
    out = {}
    for n in SMALL:
        k = shapes[n][-1]
        r0 = SMALL_ROW[n]
        out[n] = packed[r0:r0 + -(-k // D)].reshape(-1)[:k].reshape(shapes[n])
    return out


def kernel(x, c, positions, w_ada, b_ada, norm1_g, w_in, b_merge, gla_w_alpha, gla_b_alpha, gla_out_norm_g, gla_w_o, mla_q_lat_g, mla_w_uq, mla_kv_lat_g, mla_w_ukv, mla_qn_g, mla_kn_g, mla_w_o, w_out, norm2_g, mlp_w1, mlp_w2, loss_target, m_w_ada, m_b_ada, m_norm1_g, m_w_in, m_b_merge, m_gla_w_alpha, m_gla_b_alpha, m_gla_out_norm_g, m_gla_w_o, m_mla_q_lat_g, m_mla_w_uq, m_mla_kv_lat_g, m_mla_w_ukv, m_mla_qn_g, m_mla_kn_g, m_mla_w_o, m_w_out, m_norm2_g, m_mlp_w1, m_mlp_w2, v_w_ada, v_b_ada, v_norm1_g, v_w_in, v_b_merge, v_gla_w_alpha, v_gla_b_alpha, v_gla_out_norm_g, v_gla_w_o, v_mla_q_lat_g, v_mla_w_uq, v_mla_kv_lat_g, v_mla_w_ukv, v_mla_qn_g, v_mla_kn_g, v_mla_w_o, v_w_out, v_norm2_g, v_mlp_w1, v_mlp_w2):
    args = dict(locals())
    wts = {n: args[n][0] for n in WEIGHTS}
    mom = {n: args["m_" + n][0] for n in WEIGHTS}
    var = {n: args["v_" + n][0] for n in WEIGHTS}
    my_c = lax.axis_index("c")
    pad_rows = PACK_ROWS - sum(r for _, r, _ in SLAB)

    slab = jnp.concatenate([_to_slab(wts[n], form).astype(BF16) for n, _, form in SLAB] + [jnp.zeros((pad_rows, D), BF16)],
                           axis=0)
    gathered = _all_gather(slab, "weights_all_gather")
    wt, off = {}, 0
    for n, r, form in SLAB:
        wt[n] = _gathered_full(gathered[:, off:off + r], form, wts[n].shape)
        off += r
    sp = {n: wts[n].reshape(1, -1) for n in SMALL}

    grad_x, gw, rows = _local_step(x, c, positions, loss_target, wt, sp)

    per_dev = jnp.concatenate([gw[n].reshape(N_DEV, r, D) for n, r, _ in SLAB] + [jnp.zeros((N_DEV, pad_rows, D), BF16)],
                              axis=1).reshape(4, 2, PACK_ROWS, D)
    keep = lax.dynamic_index_in_dim(per_dev, my_c, axis=1, keepdims=False)
    give = lax.dynamic_index_in_dim(per_dev, 1 - my_c, axis=1, keepdims=False)
    chip_sums = _pair_sum(keep, _sibling_exchange(give))
    gslab = _slab_sum(_chip_exchange(chip_sums))
    big, off = {}, 0
    for n, r, form in SLAB:
        g = _from_slab(gslab[off:off + r], form, wts[n].shape)
        big[n] = (g,) + tuple(_adamw(g, wts[n], mom[n], var[n], "adamw_" + n))
        off += r

    order = ["dmod", "norm1_g", "norm2_g", "b_merge", "gla_b_alpha", "gla_out_norm_g", "mla_q_lat_g", "mla_kv_lat_g",
             "mla_qn_g", "mla_kn_g", "loss"]
    part_rows = jnp.concatenate([rows[n] for n in order], axis=0)
    part_rows = jnp.pad(part_rows, ((0, SMALL_ROWS - part_rows.shape[0]), (0, 0)))
    all_rows = _all_gather(part_rows, "partials_all_gather")
    small = _adamw_small(all_rows, _small_pack({n: wts[n] for n in SMALL}), _small_pack({n: mom[n] for n in SMALL}),
                         _small_pack({n: var[n] for n in SMALL}))
    loss = small[4][0, 0]
    small_shapes = {n: wts[n].shape for n in SMALL}
    small = [_small_unpack(o, small_shapes) for o in small[:4]]

    outs = [loss, grad_x]
    for k in range(4):
        for n in WEIGHTS:
            val = big[n][k] if n in BIG else small[k][n]
            outs.append(val.reshape((1,) + tuple(wts[n].shape)))
    return tuple(outs)
```

```python
import functools

import jax
import jax.numpy as jnp
from jax import lax
from jax.experimental import pallas as pl
from jax.experimental.pallas import tpu as pltpu

F32 = jnp.float32
BF16 = jnp.bfloat16
MESH = pl.DeviceIdType.MESH

D = 1024
EPS = 1e-6
CHUNK = 64
GH, GDK, GDV, GLR, GTAU = 4, 128, 256, 16, 16.0
MH, MQR, MKVR, NOPE, ROPE, MV = 16, 256, 128, 64, 32, 64
MQK = NOPE + ROPE
HP = 128
FF = 4 * D
ROPE_THETA = 10000.0
IN_WIDTH = 5552
PW = 5632
N_DEV = 8
LANES = 128
PACK_ROWS = 2960
PACK_BLOCK = 592
SMALL_ROWS = 32
SMALL_SOURCES = tuple([(r, 6 + r) for r in range(6)] + [(12,), (13,), (14,), (15,), (16,), (17, 18, 19, 20),
                                                         (21,), (22,), (23,), (24,)])
LOSS_SOURCE = (25,)
VMEM_LIMIT = 56 * 1024 * 1024

ADAM_LR, ADAM_B1, ADAM_B2, ADAM_EPS, ADAM_WD, ADAM_STEP = 0.001, 0.9, 0.999, 1e-08, 0.01, 10

SLAB = (("w_ada", 768, "T"), ("mlp_w1", 512, "T"), ("gla_w_o", 128, "N"), ("mla_w_o", 128, "N"), ("w_out", 128, "N"),
        ("mlp_w2", 512, "N"), ("mla_w_uq", 48, "TR"), ("mla_w_ukv", 32, "TR"), ("w_in", 694, "T"), ("gla_w_alpha", 1, "TR"))
BIG = tuple(n for n, _, _ in SLAB)
SMALL = ("b_ada", "norm1_g", "norm2_g", "b_merge", "gla_b_alpha", "gla_out_norm_g", "mla_q_lat_g", "mla_kv_lat_g",
         "mla_qn_g", "mla_kn_g")
SMALL_ROW = {"b_ada": 0, "norm1_g": 6, "norm2_g": 7, "b_merge": 8, "gla_b_alpha": 10, "gla_out_norm_g": 11,
             "mla_q_lat_g": 12, "mla_kv_lat_g": 13, "mla_qn_g": 14, "mla_kn_g": 15}
WEIGHTS = ("w_ada", "b_ada", "norm1_g", "w_in", "b_merge", "gla_w_alpha", "gla_b_alpha", "gla_out_norm_g", "gla_w_o",
           "mla_q_lat_g", "mla_w_uq", "mla_kv_lat_g", "mla_w_ukv", "mla_qn_g", "mla_kn_g", "mla_w_o", "w_out",
           "norm2_g", "mlp_w1", "mlp_w2")


def _cparams(sem=None):
    return pltpu.CompilerParams(dimension_semantics=sem, vmem_limit_bytes=VMEM_LIMIT)


def _tile(n, pref):
    for t in (2048, 1024, 512, 256, 128):
        if t <= pref and n % t == 0:
            return t
    return n


def _dot(a, b, dims, precision=None):
    return lax.dot_general(a, b, (dims, ((), ())), preferred_element_type=F32, precision=precision)


NN = ((1,), (0,))
NT = ((1,), (1,))
TN = ((0,), (0,))


def _sigmoid(x):
    return 1.0 / (1.0 + jnp.exp(-x))


def _silu(x):
    return x * _sigmoid(x)


def _mm(a, b, mode, out_dtypes, name, *, pro=None, pro_b=None, epi=None, extras=(), a_off=0, m=None, tm=1024, tn=512,
        tk=1024):
    if mode == "tn":
        kc, n = b.shape
        m = a.shape[1] if m is None else m
    elif mode == "nn":
        m, kc = a.shape
        n = b.shape[1]
    else:
        m, kc = a.shape
        n = b.shape[0]
    tm, tn, tk = _tile(m, tm), _tile(n, tn), _tile(kc, tk)
    nk = kc // tk
    dims = {"nn": NN, "nt": NT, "tn": TN}[mode]
    if mode == "tn":
        a_spec = pl.BlockSpec((tk, tm), lambda i, j, k: (k, i + a_off))
    else:
        a_spec = pl.BlockSpec((tm, tk), lambda i, j, k: (i + a_off, k))
    if mode == "nt":
        b_spec = pl.BlockSpec((tn, tk), lambda i, j, k: (j, k))
    else:
        b_spec = pl.BlockSpec((tk, tn), lambda i, j, k: (k, j))
    o_spec = pl.BlockSpec((tm, tn), lambda i, j, k: (i, j))
    n_ex, n_out = len(extras), len(out_dtypes)

    def body(a_ref, b_ref, *rest):
        ex, outs, acc = rest[:n_ex], rest[n_ex:n_ex + n_out], rest[-1]
        k = pl.program_id(2)

        @pl.when(k == 0)
        def _():
            acc[...] = jnp.zeros_like(acc)

        av = a_ref[...]
        if pro is not None:
            av = pro(av)
        bv = b_ref[...]
        if pro_b is not None:
            bv = pro_b(bv)
        acc[...] += _dot(av.astype(BF16), bv.astype(BF16), dims)

        @pl.when(k == nk - 1)
        def _():
            res = (acc[...],) if epi is None else epi(acc[...], *[e[...] for e in ex])
            for o_ref, r in zip(outs, res):
                o_ref[...] = r.astype(o_ref.dtype)

    out = pl.pallas_call(
        body, name=name, grid=(m // tm, n // tn, nk),
        in_specs=[a_spec, b_spec] + [o_spec] * n_ex,
        out_specs=[o_spec] * n_out,
        out_shape=[jax.ShapeDtypeStruct((m, n), dt) for dt in out_dtypes],
        scratch_shapes=[pltpu.VMEM((tm, tn), F32)],
        compiler_params=_cparams(("parallel", "parallel", "arbitrary")),
    )(a, b, *extras)
    return out[0] if n_out == 1 else out


def _rows(s):
    return _tile(s, 512)


def _mod_spec():
    return pl.BlockSpec((1, 6, D), lambda b, i: (b, 0, 0))


def _tok_spec(tr, nb, width=D, col=0):
    return pl.BlockSpec((tr, width), lambda b, i: (b * nb + i, col))


def _norm_mod_fwd(x, g, mod3, i_shift, i_scale, name, mixed=None, i_gate=None):
    bsz, _, _ = mod3.shape
    t = x.shape[0]
    s = t // bsz
    tr = _rows(s)
    nb = s // tr
    has_res = mixed is not None

    def body(*refs):
        if has_res:
            x_ref, mx_ref, g_ref, mod_ref, x1_ref, h_ref = refs
            xv = x_ref[...] + mod_ref[0, i_gate:i_gate + 1, :] * mx_ref[...]
            x1_ref[...] = xv
        else:
            x_ref, g_ref, mod_ref, h_ref = refs
            xv = x_ref[...]
        r = lax.rsqrt(jnp.mean(xv * xv, axis=1, keepdims=True) + EPS)
        hn = (xv * r) * g_ref[...]
        h = hn * (1.0 + mod_ref[0, i_scale:i_scale + 1, :]) + mod_ref[0, i_shift:i_shift + 1, :]
        h_ref[...] = h.astype(BF16)

    tok = _tok_spec(tr, nb)
    gspec = pl.BlockSpec((1, D), lambda b, i: (0, 0))
    ins = [x] + ([mixed] if has_res else []) + [g, mod3]
    in_specs = [tok] + ([tok] if has_res else []) + [gspec, _mod_spec()]
    out_shape = ([jax.ShapeDtypeStruct((t, D), F32)] if has_res else []) + [jax.ShapeDtypeStruct((t, D), BF16)]
    out = pl.pallas_call(
        body, name=name, grid=(bsz, nb), in_specs=in_specs, out_specs=[tok] * len(out_shape), out_shape=out_shape,
        compiler_params=_cparams(("arbitrary", "arbitrary")),
    )(*ins)
    return (out[0], out[1]) if has_res else (None, out[0])


def _norm_mod_bwd(x, dh, dres, g, mod3, i_shift, i_scale, name, mixed=None, i_gate=None):
    bsz = mod3.shape[0]
    t = x.shape[0]
    s = t // bsz
    tr = _rows(s)
    nb = s // tr
    has_res = mixed is not None

    def body(*refs):
        if has_res:
            x_ref, dh_ref, dres_ref, mx_ref, g_ref, mod_ref, dx_ref, dmx_ref, accb, accg = refs
        else:
            x_ref, dh_ref, dres_ref, g_ref, mod_ref, dx_ref, accb, accg = refs
        b, i = pl.program_id(0), pl.program_id(1)

        @pl.when(i == 0)
        def _():
            accb[...] = jnp.zeros_like(accb)

        @pl.when((i == 0) & (b == 0))
        def _():
            accg[...] = jnp.zeros_like(accg)

        xv, dhv, gv = x_ref[...], dh_ref[...], g_ref[...]
        r = lax.rsqrt(jnp.mean(xv * xv, axis=1, keepdims=True) + EPS)
        xn = xv * r
        accb[0, 0:1, :] += jnp.sum(dhv, axis=0, keepdims=True)
        accb[0, 1:2, :] += jnp.sum(dhv * (xn * gv), axis=0, keepdims=True)
        tt = dhv * (1.0 + mod_ref[0, i_scale:i_scale + 1, :])
        accg[0:1, :] += jnp.sum(tt * xn, axis=0, keepdims=True)
        dxn = tt * gv
        dx = dres_ref[...] + r * (dxn - xn * jnp.mean(dxn * xn, axis=1, keepdims=True))
        dx_ref[...] = dx
        if has_res:
            accb[0, 2:3, :] += jnp.sum(dx * mx_ref[...], axis=0, keepdims=True)
            dmx_ref[...] = (dx * mod_ref[0, i_gate:i_gate + 1, :]).astype(BF16)

    tok = _tok_spec(tr, nb)
    gspec = pl.BlockSpec((1, D), lambda b, i: (0, 0))
    ins = [x, dh, dres] + ([mixed] if has_res else []) + [g, mod3]
    in_specs = [tok] * (4 if has_res else 3) + [gspec, _mod_spec()]
    out_shape = [jax.ShapeDtypeStruct((t, D), F32)] + ([jax.ShapeDtypeStruct((t, D), BF16)] if has_res else [])
    out_specs = [tok] * len(out_shape)
    out_shape += [jax.ShapeDtypeStruct((bsz, 8, D), F32), jax.ShapeDtypeStruct((8, D), F32)]
    out_specs += [pl.BlockSpec((1, 8, D), lambda b, i: (b, 0, 0)), pl.BlockSpec((8, D), lambda b, i: (0, 0))]
    return pl.pallas_call(
        body, name=name, grid=(bsz, nb), in_specs=in_specs, out_specs=out_specs, out_shape=out_shape,
        compiler_params=_cparams(("arbitrary", "arbitrary")),
    )(*ins)


def _loss_head(x1, ff, tgt, mod3):
    bsz = mod3.shape[0]
    t = x1.shape[0]
    s = t // bsz
    tr = _rows(s)
    nb = s // tr

    def body(x1_ref, ff_ref, tg_ref, mod_ref, dy_ref, dff_ref, accb, accl):
        b, i = pl.program_id(0), pl.program_id(1)

        @pl.when(i == 0)
        def _():
            accb[...] = jnp.zeros_like(accb)

        @pl.when((i == 0) & (b == 0))
        def _():
            accl[...] = jnp.zeros_like(accl)

        gate = mod_ref[0, 5:6, :]
        ffv = ff_ref[...]
        err = x1_ref[...] + gate * ffv - tg_ref[...]
        accl[0:1, :] += jnp.sum(err * err, axis=0, keepdims=True) * (0.5 / D)
        dy = err * (1.0 / D)
        dy_ref[...] = dy
        dff_ref[...] = (dy * gate).astype(BF16)
        accb[0, 0:1, :] += jnp.sum(dy * ffv, axis=0, keepdims=True)

    tok = _tok_spec(tr, nb)
    return pl.pallas_call(
        body, name="loss_head", grid=(bsz, nb), in_specs=[tok, tok, tok, _mod_spec()],
        out_specs=[tok, tok, pl.BlockSpec((1, 8, D), lambda b, i: (b, 0, 0)), pl.BlockSpec((8, D), lambda b, i: (0, 0))],
        out_shape=[jax.ShapeDtypeStruct((t, D), F32), jax.ShapeDtypeStruct((t, D), BF16),
                   jax.ShapeDtypeStruct((bsz, 8, D), F32), jax.ShapeDtypeStruct((8, D), F32)],
        compiler_params=_cparams(("arbitrary", "arbitrary")),
    )(x1, ff, tgt, mod3)


def _merge_fwd(proj, b_merge, y_a, y_b):
    t = proj.shape[0]
    tr = _tile(t, 512)

    def body(la_ref, lb_ref, bm_ref, ya_ref, yb_ref, mix_ref):
        ga = _sigmoid(la_ref[...] + bm_ref[:, 0:D])
        gb = _sigmoid(lb_ref[...] + bm_ref[:, D:2 * D])
        mix_ref[...] = (ga * ya_ref[...] + gb * yb_ref[...]).astype(BF16)

    tok = pl.BlockSpec((tr, D), lambda i: (i, 0))
    return pl.pallas_call(
        body, name="merge_fwd", grid=(t // tr,),
        in_specs=[pl.BlockSpec((tr, D), lambda i: (i, 3)), pl.BlockSpec((tr, D), lambda i: (i, 4)),
                  pl.BlockSpec((1, 2 * D), lambda i: (0, 0)), tok, tok],
        out_specs=tok, out_shape=jax.ShapeDtypeStruct((t, D), BF16),
        compiler_params=_cparams(("arbitrary",)),
    )(proj, proj, b_merge, y_a, y_b)


def _merge_bwd(dmix, proj, b_merge, y_a, y_b):
    t = proj.shape[0]
    tr = _tile(t, 512)

    def body(dm_ref, la_ref, lb_ref, bm_ref, ya_ref, yb_ref, dya_ref, dyb_ref, dl_ref, acc):
        @pl.when(pl.program_id(0) == 0)
        def _():
            acc[...] = jnp.zeros_like(acc)

        dm = dm_ref[...]
        ga = _sigmoid(la_ref[...] + bm_ref[:, 0:D])
        gb = _sigmoid(lb_ref[...] + bm_ref[:, D:2 * D])
        dya_ref[...] = (dm * ga).astype(BF16)
        dyb_ref[...] = (dm * gb).astype(BF16)
        dla = dm * ya_ref[...] * ga * (1.0 - ga)
        dlb = dm * yb_ref[...] * gb * (1.0 - gb)
        dl_ref[:, 0:D] = dla.astype(BF16)
        dl_ref[:, D:2 * D] = dlb.astype(BF16)
        acc[0:1, 0:D] += jnp.sum(dla, axis=0, keepdims=True)
        acc[0:1, D:2 * D] += jnp.sum(dlb, axis=0, keepdims=True)

    tok = pl.BlockSpec((tr, D), lambda i: (i, 0))
    return pl.pallas_call(
        body, name="merge_bwd", grid=(t // tr,),
        in_specs=[tok, pl.BlockSpec((tr, D), lambda i: (i, 3)), pl.BlockSpec((tr, D), lambda i: (i, 4)),
                  pl.BlockSpec((1, 2 * D), lambda i: (0, 0)), tok, tok],
        out_specs=[tok, tok, pl.BlockSpec((tr, 2 * D), lambda i: (i, 0)), pl.BlockSpec((8, 2 * D), lambda i: (0, 0))],
        out_shape=[jax.ShapeDtypeStruct((t, D), BF16), jax.ShapeDtypeStruct((t, D), BF16),
                   jax.ShapeDtypeStruct((t, 2 * D), BF16), jax.ShapeDtypeStruct((8, 2 * D), F32)],
        compiler_params=_cparams(("arbitrary",)),
    )(dmix, proj, proj, b_merge, y_a, y_b)


def _log_sigmoid(z):
    return jnp.minimum(z, 0.0) - jnp.log(1.0 + jnp.exp(-jnp.abs(z)))


def _tri(lower):
    r = lax.broadcasted_iota(jnp.int32, (CHUNK, CHUNK), 0)
    c = lax.broadcasted_iota(jnp.int32, (CHUNK, CHUNK), 1)
    return jnp.where(r >= c if lower else r <= c, 1.0, 0.0).astype(F32)


def _gla_fwd(proj, wa_pad, b_alpha, g_out, bsz):
    t = proj.shape[0]
    s = t // bsz
    nc = s // CHUNK

    def body(q_ref, k_ref, v_ref, gg_ref, ms_ref, wa_ref, ba_ref, go_ref, o_ref, og_ref, st_ref, la, state):
        z = _dot(ms_ref[...].astype(BF16), wa_ref[...], NN) + ba_ref[...]
        la[...] = _log_sigmoid(z) * (1.0 / GTAU)
        state[...] = jnp.zeros_like(state)
        low = _tri(True)
        gout = go_ref[...]

        def chunk(n, carry):
            rows = pl.ds(pl.multiple_of(n * CHUNK, CHUNK), CHUNK)
            lac = la[rows, :]
            cum = _dot(low, lac, NN, lax.Precision.HIGHEST)
            ce = jnp.sum(lac, axis=0, keepdims=True)
            kd = (k_ref[rows, :] * jnp.exp(ce - cum)).astype(BF16)
            new = state[...] * jnp.exp(ce) + _dot(v_ref[rows, :].astype(BF16), kd, TN)
            state[...] = new
            st_ref[pl.ds(pl.multiple_of(n * GDV, GDV), GDV), :] = new
            qs = (q_ref[rows, :] * (GDK ** -0.5)).astype(BF16)
            o = _dot(qs, new.astype(BF16), NT)
            o_ref[rows, :] = o
            ro = lax.rsqrt(jnp.mean(o * o, axis=1, keepdims=True) + EPS)
            og_ref[rows, :] = (((o * ro) * gout) * _silu(gg_ref[rows, :])).astype(BF16)
            return carry

        lax.fori_loop(0, nc, chunk, 0)

    hk = pl.BlockSpec((s, GDK), lambda b, h: (b, h))
    return pl.pallas_call(
        body, name="gla_fwd", grid=(bsz, GH),
        in_specs=[hk, pl.BlockSpec((s, GDK), lambda b, h: (b, GH + h)), pl.BlockSpec((s, GDV), lambda b, h: (b, 4 + h)),
                  pl.BlockSpec((s, GDV), lambda b, h: (b, 8 + h)), pl.BlockSpec((s, LANES), lambda b, h: (b, 43)),
                  pl.BlockSpec((LANES, GDK), lambda b, h: (0, h)), pl.BlockSpec((1, GDK), lambda b, h: (0, h)),
                  pl.BlockSpec((1, GDV), lambda b, h: (0, 0))],
        out_specs=[pl.BlockSpec((s, GDV), lambda b, h: (b, h)), pl.BlockSpec((s, GDV), lambda b, h: (b, h)),
                   pl.BlockSpec((nc * GDV, GDK), lambda b, h: (b * GH + h, 0))],
        out_shape=[jax.ShapeDtypeStruct((t, GH * GDV), F32), jax.ShapeDtypeStruct((t, GH * GDV), BF16),
                   jax.ShapeDtypeStruct((bsz * GH * nc * GDV, GDK), F32)],
        scratch_shapes=[pltpu.VMEM((s, GDK), F32), pltpu.VMEM((GDV, GDK), F32)],
        compiler_params=_cparams(("arbitrary", "arbitrary")),
    )(proj, proj, proj, proj, proj, wa_pad, b_alpha, g_out)


def _gla_bwd(dog, o, proj, wa_pad, b_alpha, g_out, states, bsz):
    t = proj.shape[0]
    s = t // bsz
    nc = s // CHUNK

    def body(dog_ref, o_ref, q_ref, k_ref, v_ref, gg_ref, ms_ref, wa_ref, ba_ref, go_ref, st_ref,
             dq_ref, dk_ref, dv_ref, dgg_ref, dz_ref, dba, dgo, zs, la, carry_g):
        @pl.when(pl.program_id(1) == 0)
        def _():
            dba[...] = jnp.zeros_like(dba)
            dgo[...] = jnp.zeros_like(dgo)

        z = _dot(ms_ref[...].astype(BF16), wa_ref[...], NN) + ba_ref[...]
        zs[...] = z
        la[...] = _log_sigmoid(z) * (1.0 / GTAU)
        carry_g[...] = jnp.zeros_like(carry_g)
        low, upp = _tri(True), _tri(False)
        gout = go_ref[...]
        last_row = lax.broadcasted_iota(jnp.int32, (CHUNK, GDK), 0) == CHUNK - 1

        def chunk(step, carry):
            n = nc - 1 - step
            rows = pl.ds(pl.multiple_of(n * CHUNK, CHUNK), CHUNK)
            lac = la[rows, :]
            cum = _dot(low, lac, NN, lax.Precision.HIGHEST)
            ce = jnp.sum(lac, axis=0, keepdims=True)
            e = jnp.exp(ce - cum)
            dec = jnp.exp(ce)
            kf = k_ref[rows, :]
            kd = (kf * e).astype(BF16)
            vv = v_ref[rows, :].astype(BF16)
            qs = (q_ref[rows, :] * (GDK ** -0.5)).astype(BF16)
            ov = o_ref[rows, :]
            ro = lax.rsqrt(jnp.mean(ov * ov, axis=1, keepdims=True) + EPS)
            on = ov * ro
            gg = gg_ref[rows, :]
            sg = _sigmoid(gg)
            dogv = dog_ref[rows, :]
            dgg_ref[rows, :] = (dogv * (on * gout) * (sg * (1.0 + gg * (1.0 - sg)))).astype(BF16)
            t1 = dogv * (gg * sg)
            dgo[0:1, :] += jnp.sum(t1 * on, axis=0, keepdims=True)
            don = t1 * gout
            do = ro * (don - on * jnp.mean(don * on, axis=1, keepdims=True))
            dob = do.astype(BF16)
            st_n = st_ref[pl.ds(pl.multiple_of(n * GDV, GDV), GDV), :]
            dq_ref[rows, :] = (_dot(dob, st_n.astype(BF16), NN) * (GDK ** -0.5)).astype(BF16)
            dn = carry_g[...] + _dot(dob, qs, TN)
            prev = jnp.maximum(n - 1, 0)
            st_p = st_ref[pl.ds(pl.multiple_of(prev * GDV, GDV), GDV), :] * jnp.where(n > 0, 1.0, 0.0)
            ddec = jnp.sum(dn * st_p, axis=0, keepdims=True)
            dnb = dn.astype(BF16)
            dkd = _dot(vv, dnb, NN)
            dv_ref[rows, :] = _dot(kd, dnb, NT).astype(BF16)
            dk_ref[rows, :] = (dkd * e).astype(BF16)
            w = dkd * kf * e
            dce = jnp.sum(w, axis=0, keepdims=True) + ddec * dec
            dcum = jnp.where(last_row, dce - w, -w)
            dla = _dot(upp, dcum, NN, lax.Precision.HIGHEST)
            dz = dla * (1.0 / GTAU) * _sigmoid(-zs[rows, :])
            dba[0:1, :] += jnp.sum(dz, axis=0, keepdims=True)
            dz_ref[rows, :] = dz.astype(BF16)
            carry_g[...] = dn * dec
            return carry

        lax.fori_loop(0, nc, chunk, 0)

    hv = pl.BlockSpec((s, GDV), lambda h, b: (b, h))
    hk = pl.BlockSpec((s, GDK), lambda h, b: (b, h))
    return pl.pallas_call(
        body, name="gla_bwd", grid=(GH, bsz),
        in_specs=[hv, hv, hk, pl.BlockSpec((s, GDK), lambda h, b: (b, GH + h)),
                  pl.BlockSpec((s, GDV), lambda h, b: (b, 4 + h)), pl.BlockSpec((s, GDV), lambda h, b: (b, 8 + h)),
                  pl.BlockSpec((s, LANES), lambda h, b: (b, 43)), pl.BlockSpec((LANES, GDK), lambda h, b: (0, h)),
                  pl.BlockSpec((1, GDK), lambda h, b: (0, h)), pl.BlockSpec((1, GDV), lambda h, b: (0, 0)),
                  pl.BlockSpec((nc * GDV, GDK), lambda h, b: (b * GH + h, 0))],
        out_specs=[hk, hk, hv, hv, hk, pl.BlockSpec((8, GDK), lambda h, b: (0, h)),
                   pl.BlockSpec((8, GDV), lambda h, b: (h, 0))],
        out_shape=[jax.ShapeDtypeStruct((t, GH * GDK), BF16), jax.ShapeDtypeStruct((t, GH * GDK), BF16),
                   jax.ShapeDtypeStruct((t, GH * GDV), BF16), jax.ShapeDtypeStruct((t, GH * GDV), BF16),
                   jax.ShapeDtypeStruct((t, GH * GDK), BF16), jax.ShapeDtypeStruct((8, GH * GDK), F32),
                   jax.ShapeDtypeStruct((8 * GH, GDV), F32)],
        scratch_shapes=[pltpu.VMEM((s, GDK), F32), pltpu.VMEM((s, GDK), F32), pltpu.VMEM((GDV, GDK), F32)],
        compiler_params=_cparams(("arbitrary", "arbitrary")),
    )(dog, o, proj, proj, proj, proj, proj, wa_pad, b_alpha, g_out, states)


def _rope_tables(pos_ref, fr_ref, sg_ref):
    ang = pos_ref[...].astype(F32) * fr_ref[...]
    return jnp.cos(ang), jnp.sin(ang) * sg_ref[...]


def _partner(x):
    lane = lax.broadcasted_iota(jnp.int32, x.shape, 1)
    return jnp.where(lane < NOPE + ROPE // 2, pltpu.roll(x, LANES - ROPE // 2, 1), pltpu.roll(x, ROPE // 2, 1))


def _mla_rows(t):
    return _tile(t, 256)


def _mla_pre_fwd(proj, pos, fr, sg, q_lat_g, kv_lat_g, qn_g, kn_g, wuq, wukv):
    t = proj.shape[0]
    tr = _mla_rows(t)

    def body(cq_ref, ckv_ref, ms_ref, pos_ref, fr_ref, sg_ref, qlg, kvlg, qng, kng, wuq_ref, wukv_ref, q_out, k_out, v_out):
        cos, sin = _rope_tables(pos_ref, fr_ref, sg_ref)
        cq = cq_ref[...]
        cqn = (cq * lax.rsqrt(jnp.mean(cq * cq, axis=1, keepdims=True) + EPS) * qlg[...]).astype(BF16)
        ckv = ckv_ref[...]
        ckvn = (ckv * lax.rsqrt(jnp.mean(ckv * ckv, axis=1, keepdims=True) + EPS) * kvlg[...]).astype(BF16)
        lane = lax.broadcasted_iota(jnp.int32, (tr, HP), 1)
        kpe = jnp.where((lane >= NOPE) & (lane < MQK), ms_ref[...], 0.0)
        lane_all = lax.broadcasted_iota(jnp.int32, (tr, MH * HP), 1)
        v_out[...] = jnp.where(lane_all % HP == MV, 1.0, _dot(ckvn, wukv_ref[:, MH * HP:], NN)).astype(BF16)
        for h in range(MH):
            cols = slice(h * HP, (h + 1) * HP)
            qh = _dot(cqn, wuq_ref[:, cols], NN)
            qn = qh * lax.rsqrt(jnp.sum(qh * qh, axis=1, keepdims=True) * (1.0 / MQK) + EPS) * qng[...]
            q_out[:, cols] = (qn * cos + _partner(qn) * sin).astype(BF16)
            kh = _dot(ckvn, wukv_ref[:, cols], NN) + kpe
            kn = kh * lax.rsqrt(jnp.sum(kh * kh, axis=1, keepdims=True) * (1.0 / MQK) + EPS) * kng[...]
            k_out[:, cols] = (kn * cos + _partner(kn) * sin).astype(BF16)

    def full(a):
        return pl.BlockSpec(a.shape, lambda i: (0, 0))

    wide = pl.BlockSpec((tr, MH * HP), lambda i: (i, 0))
    return pl.pallas_call(
        body, name="mla_pre_fwd", grid=(t // tr,),
        in_specs=[pl.BlockSpec((tr, MQR), lambda i: (i, 20)), pl.BlockSpec((tr, MKVR), lambda i: (i, 42)),
                  pl.BlockSpec((tr, LANES), lambda i: (i, 43)), pl.BlockSpec((tr, 1), lambda i: (i, 0)),
                  full(fr), full(sg), full(q_lat_g), full(kv_lat_g), full(qn_g), full(kn_g), full(wuq), full(wukv)],
        out_specs=[wide, wide, wide],
        out_shape=[jax.ShapeDtypeStruct((t, MH * HP), BF16)] * 3,
        compiler_params=_cparams(("arbitrary",)),
    )(proj, proj, proj, pos, fr, sg, q_lat_g, kv_lat_g, qn_g, kn_g, wuq, wukv)


def _mla_pre_bwd(dq2, dk2, dv2, dmisc_gla, proj, pos, fr, sg, q_lat_g, kv_lat_g, qn_g, kn_g, wuq, wukv):
    t = proj.shape[0]
    tr = _mla_rows(t)

    def body(dq_ref, dk_ref, dv_ref, dmg_ref, cq_ref, ckv_ref, ms_ref, pos_ref, fr_ref, sg_ref, qlg, kvlg, qng, kng,
             wuq_ref, wukv_ref, dcq_ref, dckv_ref, dms_ref, dwuq, dwukv, acc, dqf, dkvf):
        @pl.when(pl.program_id(0) == 0)
        def _():
            dwuq[...] = jnp.zeros_like(dwuq)
            dwukv[...] = jnp.zeros_like(dwukv)
            acc[...] = jnp.zeros_like(acc)

        cos, sin = _rope_tables(pos_ref, fr_ref, sg_ref)
        cq = cq_ref[...]
        rc = lax.rsqrt(jnp.mean(cq * cq, axis=1, keepdims=True) + EPS)
        xc = cq * rc
        cqn = (xc * qlg[...]).astype(BF16)
        ckv = ckv_ref[...]
        rkv = lax.rsqrt(jnp.mean(ckv * ckv, axis=1, keepdims=True) + EPS)
        xkv = ckv * rkv
        ckvn = (xkv * kvlg[...]).astype(BF16)
        lane = lax.broadcasted_iota(jnp.int32, (tr, HP), 1)
        is_rope = (lane >= NOPE) & (lane < MQK)
        kpe = jnp.where(is_rope, ms_ref[...], 0.0)
        dkpe = jnp.zeros((tr, HP), F32)
        dqng = jnp.zeros((1, HP), F32)
        dkng = jnp.zeros((1, HP), F32)
        for h in range(MH):
            cols = slice(h * HP, (h + 1) * HP)
            qh = _dot(cqn, wuq_ref[:, cols], NN)
            rq = lax.rsqrt(jnp.sum(qh * qh, axis=1, keepdims=True) * (1.0 / MQK) + EPS)
            xq = qh * rq
            dy = dq_ref[:, cols].astype(F32)
            dqn = dy * cos - _partner(dy) * sin
            dqng += jnp.sum(dqn * xq, axis=0, keepdims=True)
            tq = dqn * qng[...]
            dqf[:, cols] = (rq * (tq - xq * (jnp.sum(tq * xq, axis=1, keepdims=True) * (1.0 / MQK)))).astype(BF16)
            kh = _dot(ckvn, wukv_ref[:, cols], NN) + kpe
            rk = lax.rsqrt(jnp.sum(kh * kh, axis=1, keepdims=True) * (1.0 / MQK) + EPS)
            xk = kh * rk
            dy = dk_ref[:, cols].astype(F32)
            dkn = dy * cos - _partner(dy) * sin
            dkng += jnp.sum(dkn * xk, axis=0, keepdims=True)
            tk = dkn * kng[...]
            dkh = rk * (tk - xk * (jnp.sum(tk * xk, axis=1, keepdims=True) * (1.0 / MQK)))
            dkvf[:, cols] = jnp.where(lane < NOPE, dkh, 0.0).astype(BF16)
            dkpe += jnp.where(is_rope, dkh, 0.0)
        dkvf[:, MH * HP:] = dv_ref[...]
        acc[2:3, 0:HP] += dqng
        acc[3:4, 0:HP] += dkng
        dms_ref[...] = (dmg_ref[...] + dkpe).astype(BF16)

        dqfv = dqf[...]
        dwuq[...] += _dot(cqn, dqfv, TN)
        dcqn = _dot(dqfv, wuq_ref[...], NT)
        acc[0:1, :] += jnp.sum(dcqn * xc, axis=0, keepdims=True)
        tc = dcqn * qlg[...]
        dcq_ref[...] = (rc * (tc - xc * jnp.mean(tc * xc, axis=1, keepdims=True))).astype(BF16)

        dkvfv = dkvf[...]
        dwukv[...] += _dot(ckvn, dkvfv, TN)
        dckvn = _dot(dkvfv, wukv_ref[...], NT)
        acc[1:2, 0:MKVR] += jnp.sum(dckvn * xkv, axis=0, keepdims=True)
        tkv = dckvn * kvlg[...]
        dckv_ref[...] = (rkv * (tkv - xkv * jnp.mean(tkv * xkv, axis=1, keepdims=True))).astype(BF16)

    def full(a):
        return pl.BlockSpec(a.shape, lambda i: (0, 0))

    wide = pl.BlockSpec((tr, MH * HP), lambda i: (i, 0))
    narrow = pl.BlockSpec((tr, LANES), lambda i: (i, 0))
    return pl.pallas_call(
        body, name="mla_pre_bwd", grid=(t // tr,),
        in_specs=[wide, wide, wide, narrow,
                  pl.BlockSpec((tr, MQR), lambda i: (i, 20)), pl.BlockSpec((tr, MKVR), lambda i: (i, 42)),
                  pl.BlockSpec((tr, LANES), lambda i: (i, 43)), pl.BlockSpec((tr, 1), lambda i: (i, 0)),
                  full(fr), full(sg), full(q_lat_g), full(kv_lat_g), full(qn_g), full(kn_g), full(wuq), full(wukv)],
        out_specs=[pl.BlockSpec((tr, MQR), lambda i: (i, 0)), narrow, narrow,
                   pl.BlockSpec((MQR, MH * HP), lambda i: (0, 0)), pl.BlockSpec((MKVR, 2 * MH * HP), lambda i: (0, 0)),
                   pl.BlockSpec((8, MQR), lambda i: (0, 0))],
        out_shape=[jax.ShapeDtypeStruct((t, MQR), BF16), jax.ShapeDtypeStruct((t, MKVR), BF16),
                   jax.ShapeDtypeStruct((t, LANES), BF16), jax.ShapeDtypeStruct((MQR, MH * HP), F32),
                   jax.ShapeDtypeStruct((MKVR, 2 * MH * HP), F32), jax.ShapeDtypeStruct((8, MQR), F32)],
        scratch_shapes=[pltpu.VMEM((tr, MH * HP), BF16), pltpu.VMEM((tr, 2 * MH * HP), BF16)],
        compiler_params=_cparams(("arbitrary",)),
    )(dq2, dk2, dv2, dmisc_gla, proj, proj, proj, pos, fr, sg, q_lat_g, kv_lat_g, qn_g, kn_g, wuq, wukv)


ATT_FWD_TILES = (1024, 512)
ATT_BWD_TILES = (512, 512)
ATT_HEADS = 2
NEG = -1e30
LOG2E = 1.4426950408889634


def _att_mask(q0, k0, tq, tk):
    qc = (q0 + lax.broadcasted_iota(jnp.int32, (tq, tk), 0)) // CHUNK
    kc = (k0 + lax.broadcasted_iota(jnp.int32, (tq, tk), 1)) // CHUNK
    return kc <= qc


def _att_tiles(s, tiles):
    return _tile(s, tiles[0]), _tile(s, tiles[1])


def _lanes(x, n):
    return x if n == 1 else jnp.concatenate([x] * n, axis=1)


def _attn_fwd(q2, k2, v2, bsz):
    t = q2.shape[0]
    s = t // bsz
    tq, tk = _att_tiles(s, ATT_FWD_TILES)
    nq, groups, n_diag = s // tq, tk // HP, max(tq // tk, 1)
    scale = MQK ** -0.5
    c2 = scale * LOG2E
    heads = range(ATT_HEADS)

    def body(q_ref, k_ref, v_ref, o_ref, lse_ref):
        def q_loop(qi, carry):
            q0 = pl.multiple_of(qi * tq, tq)
            rows = pl.ds(q0, tq)
            n_full = q0 // tk
            qs = [q_ref[rows, h * HP:(h + 1) * HP] for h in heads]

            def scores(h, kj, masked):
                k0 = pl.multiple_of(kj * tk, tk)
                sc = _dot(qs[h], k_ref[pl.ds(k0, tk), h * HP:(h + 1) * HP], NT)
                return jnp.where(_att_mask(q0, k0, tq, tk), sc, NEG) if masked else sc

            def fold(mx, sc):
                for j in range(groups):
                    mx = jnp.maximum(mx, sc[:, j * HP:(j + 1) * HP])
                return mx

            mx = lax.fori_loop(0, n_full, lambda kj, mx: tuple(fold(mx[h], scores(h, kj, False)) for h in heads),
                               tuple(jnp.full((tq, HP), NEG, F32) for _ in heads))
            for u in range(n_diag):
                mx = tuple(fold(mx[h], scores(h, n_full + u, True)) for h in heads)
            mb = [jnp.broadcast_to(jnp.max(mx[h], axis=1, keepdims=True), (tq, HP)) for h in heads]

            def weighted(h, kj, masked):
                p = jnp.exp2((scores(h, kj, masked) - _lanes(mb[h], groups)) * c2)
                k0 = pl.multiple_of(kj * tk, tk)
                return _dot(p.astype(BF16), v_ref[pl.ds(k0, tk), h * HP:(h + 1) * HP], NN)

            acc = lax.fori_loop(0, n_full, lambda kj, acc: tuple(acc[h] + weighted(h, kj, False) for h in heads),
                                tuple(jnp.zeros((tq, HP), F32) for _ in heads))
            for u in range(n_diag):
                acc = tuple(acc[h] + weighted(h, n_full + u, True) for h in heads)
            lane = lax.broadcasted_iota(jnp.int32, (tq, HP), 1)
            for h in heads:
                a = acc[h]
                l = jnp.sum(jnp.where(lane == MV, a, 0.0), axis=1, keepdims=True)
                o_ref[rows, h * HP:(h + 1) * HP] = (a / l).astype(BF16)
                lse_ref[rows, h * HP:(h + 1) * HP] = mb[h] * scale + jnp.log(l)
            return carry

        lax.fori_loop(0, nq, q_loop, 0)

    spec = pl.BlockSpec((s, ATT_HEADS * HP), lambda b, h: (b, h))
    return pl.pallas_call(
        body, name="attn_fwd", grid=(bsz, MH // ATT_HEADS), in_specs=[spec] * 3, out_specs=[spec, spec],
        out_shape=[jax.ShapeDtypeStruct((t, MH * HP), BF16), jax.ShapeDtypeStruct((t, MH * HP), F32)],
        compiler_params=_cparams(("arbitrary", "arbitrary")),
    )(q2, k2, v2)


def _attn_bwd(q2, k2, v2, do2, o2, lse2, bsz):
    t = q2.shape[0]
    s = t // bsz
    tq, tk = _att_tiles(s, ATT_BWD_TILES)
    nq, nk, per, groups = s // tq, s // tk, max(tk // tq, 1), tk // HP
    scale = MQK ** -0.5
    c2 = scale * LOG2E
    heads = range(ATT_HEADS)

    def body(q_ref, k_ref, v_ref, do_ref, o_ref, lse_ref, dq_ref, dk_ref, dv_ref, dq_acc, delta, lse_b2):
        dq_acc[...] = jnp.zeros_like(dq_acc)

        def d_loop(i, carry):
            rows = pl.ds(pl.multiple_of(i * tq, tq), tq)
            for h in heads:
                hs = slice(h * HP, (h + 1) * HP)
                dl = jnp.sum(do_ref[rows, hs].astype(F32) * o_ref[rows, hs].astype(F32), axis=1, keepdims=True)
                delta[rows, hs] = jnp.broadcast_to(dl, (tq, HP))
            lse_b2[rows, :] = lse_ref[rows, :] * LOG2E
            return carry

        lax.fori_loop(0, nq, d_loop, 0)

        def k_loop(kj, carry):
            k0 = pl.multiple_of(kj * tk, tk)
            kk = [k_ref[pl.ds(k0, tk), h * HP:(h + 1) * HP] for h in heads]
            vv = [v_ref[pl.ds(k0, tk), h * HP:(h + 1) * HP] for h in heads]

            def tile(qi, c, masked):
                q0 = pl.multiple_of(qi * tq, tq)
                rows = pl.ds(q0, tq)
                out = []
                for h in heads:
                    hs = slice(h * HP, (h + 1) * HP)
                    dk, dv = c[h]
                    q = q_ref[rows, hs]
                    do = do_ref[rows, hs]
                    e = _dot(q, kk[h], NT) * c2 - _lanes(lse_b2[rows, hs], groups)
                    if masked:
                        e = jnp.where(_att_mask(q0, k0, tq, tk), e, NEG)
                    p = jnp.exp2(e)
                    dv = dv + _dot(p.astype(BF16), do, TN)
                    ds = (p * (_dot(do, vv[h], NT) - _lanes(delta[rows, hs], groups))).astype(BF16)
                    dq_acc[rows, hs] += _dot(ds, kk[h], NN)
                    dk = dk + _dot(ds, q, TN)
                    out.append((dk, dv))
                return tuple(out)

            zero = jnp.zeros((tk, HP), F32)
            c = tuple((zero, zero) for _ in heads)
            first = k0 // tq
            for u in range(per):
                c = tile(first + u, c, True)
            c = lax.fori_loop(first + per, nq, lambda qi, c: tile(qi, c, False), c)
            for h in heads:
                dk_ref[pl.ds(k0, tk), h * HP:(h + 1) * HP] = (c[h][0] * scale).astype(BF16)
                dv_ref[pl.ds(k0, tk), h * HP:(h + 1) * HP] = c[h][1].astype(BF16)
            return carry

        lax.fori_loop(0, nk, k_loop, 0)
        dq_ref[...] = (dq_acc[...] * scale).astype(BF16)

    spec = pl.BlockSpec((s, ATT_HEADS * HP), lambda b, h: (b, h))
    return pl.pallas_call(
        body, name="attn_bwd", grid=(bsz, MH // ATT_HEADS), in_specs=[spec] * 6, out_specs=[spec] * 3,
        out_shape=[jax.ShapeDtypeStruct((t, MH * HP), BF16)] * 3,
        scratch_shapes=[pltpu.VMEM((s, ATT_HEADS * HP), F32)] * 3,
        compiler_params=_cparams(("arbitrary", "arbitrary")),
    )(q2, k2, v2, do2, o2, lse2)


def _perm_w_in_t(w):
    z = lambda n: jnp.zeros((n, w.shape[1]), w.dtype)
    return jnp.concatenate([w[:3072], w[3504:5552], w[3088:3344], w[3344:3472], w[3072:3088], z(48), w[3472:3504], z(32)],
                           axis=0)


def _unperm_w_in_t(g):
    return jnp.concatenate([g[:3072], g[5504:5520], g[5120:5376], g[5376:5504], g[5568:5600], g[3072:5120]], axis=0)


def _pad_wa(w):
    return jnp.pad(w, ((0, LANES - GLR), (0, 0)))


def _pad_wuq(w):
    return jnp.pad(w.reshape(MQR, MH, MQK), ((0, 0), (0, 0), (0, HP - MQK))).reshape(MQR, MH * HP)


def _unpad_wuq(g):
    return g.reshape(MQR, MH, HP)[:, :, :MQK].reshape(MQR, MH * MQK)


def _pad_wukv(w):
    w3 = w.reshape(MKVR, MH, NOPE + MV)
    kp = jnp.pad(w3[:, :, :NOPE], ((0, 0), (0, 0), (0, HP - NOPE))).reshape(MKVR, MH * HP)
    vp = jnp.pad(w3[:, :, NOPE:], ((0, 0), (0, 0), (0, HP - MV))).reshape(MKVR, MH * HP)
    return jnp.concatenate([kp, vp], axis=1)


def _unpad_wukv(g):
    kp = g[:, :MH * HP].reshape(MKVR, MH, HP)[:, :, :NOPE]
    vp = g[:, MH * HP:].reshape(MKVR, MH, HP)[:, :, :MV]
    return jnp.concatenate([kp, vp], axis=2).reshape(MKVR, MH * (NOPE + MV))


def _pad_wo(w):
    return jnp.pad(w.reshape(MH, MV, D), ((0, 0), (0, HP - MV), (0, 0))).reshape(MH * HP, D)


def _unpad_wo(g):
    return g.reshape(MH, HP, D)[:, :MV, :].reshape(MH * MV, D)


def _pad_lanes(v, n=HP):
    return jnp.pad(v, ((0, 0), (0, n - v.shape[1])))


def _local_step(x, c, positions, tgt, wt, sp):
    bsz, s, _ = x.shape
    t = bsz * s
    x2 = x.reshape(t, D)
    tgt2 = tgt.reshape(t, D)
    pos = positions.reshape(t, 1)
    fr16 = ROPE_THETA ** (-jnp.arange(0, ROPE, 2, dtype=F32) / ROPE)
    zero = lambda n: jnp.zeros((n,), F32)
    fr = jnp.concatenate([zero(NOPE), fr16, fr16, zero(HP - MQK)]).reshape(1, HP)
    sg = jnp.concatenate([zero(NOPE), -jnp.ones((ROPE // 2,), F32), jnp.ones((ROPE // 2,), F32), zero(HP - MQK)]).reshape(1, HP)

    w_in_t = _perm_w_in_t(wt["w_in"])
    wa_pad = _pad_wa(wt["gla_w_alpha"].T)
    wuq = _pad_wuq(wt["mla_w_uq"].T)
    wukv = _pad_wukv(wt["mla_w_ukv"].T)
    wo_pad = _pad_wo(wt["mla_w_o"])
    qn_g, kn_g = _pad_lanes(sp["mla_qn_g"]), _pad_lanes(sp["mla_kn_g"])

    c8 = jnp.pad(c, ((0, 8 - bsz), (0, 0)))
    mod8 = _mm(c8, wt["w_ada"], "nt", (F32,), "ada_fwd", pro=_silu, extras=(jnp.broadcast_to(sp["b_ada"], (8, 6 * D)),),
               epi=lambda acc, bias: (acc + bias,))
    mod3 = mod8[:bsz].reshape(bsz, 6, D)

    _, h = _norm_mod_fwd(x2, sp["norm1_g"], mod3, 0, 1, "norm1_fwd")
    proj = _mm(h, w_in_t, "nt", (F32,), "proj_fwd")
    o_gla, og, states = _gla_fwd(proj, wa_pad, sp["gla_b_alpha"], sp["gla_out_norm_g"], bsz)
    y_a = _mm(og, wt["gla_w_o"], "nn", (F32,), "gla_out_fwd")
    q2, k2, v2 = _mla_pre_fwd(proj, pos, fr, sg, sp["mla_q_lat_g"], sp["mla_kv_lat_g"], qn_g, kn_g, wuq, wukv)
    o2, lse2 = _attn_fwd(q2, k2, v2, bsz)
    y_b = _mm(o2, wo_pad, "nn", (F32,), "mla_out_fwd")
    mix = _merge_fwd(proj, sp["b_merge"], y_a, y_b)
    mixed = _mm(mix, wt["w_out"], "nn", (F32,), "w_out_fwd")

    x1, h2 = _norm_mod_fwd(x2, sp["norm2_g"], mod3, 3, 4, "norm2_fwd", mixed=mixed, i_gate=2)
    a, f = _mm(h2, wt["mlp_w1"], "nt", (F32, BF16), "mlp1_fwd",
               epi=lambda acc: (acc, jnp.square(jnp.maximum(acc, 0.0))))
    ff = _mm(f, wt["mlp_w2"], "nn", (F32,), "mlp2_fwd")
    dy, dff, acc_g2, acc_loss = _loss_head(x1, ff, tgt2, mod3)

    gw = {}
    gw["mlp_w2"] = _mm(f, dff, "tn", (BF16,), "mlp2_dw")
    da = _mm(dff, wt["mlp_w2"], "nt", (BF16,), "mlp2_dx", extras=(a,),
             epi=lambda acc, av: (acc * (2.0 * jnp.maximum(av, 0.0)),))
    gw["mlp_w1"] = _mm(da, h2, "tn", (BF16,), "mlp1_dw")
    dh2 = _mm(da, wt["mlp_w1"], "nn", (F32,), "mlp1_dx")
    dx1, dmixed, accb2, accg2 = _norm_mod_bwd(x1, dh2, dy, sp["norm2_g"], mod3, 3, 4, "norm2_bwd", mixed=mixed, i_gate=2)

    gw["w_out"] = _mm(mix, dmixed, "tn", (BF16,), "w_out_dw")
    dmix = _mm(dmixed, wt["w_out"], "nt", (F32,), "w_out_dx")
    dy_a, dy_b, dlogits, acc_bm = _merge_bwd(dmix, proj, sp["b_merge"], y_a, y_b)
    gw["gla_w_o"] = _mm(og, dy_a, "tn", (BF16,), "gla_out_dw")
    dog = _mm(dy_a, wt["gla_w_o"], "nt", (F32,), "gla_out_dx")
    dq_g, dk_g, dv_g, dgg, dz, acc_ba, acc_go = _gla_bwd(dog, o_gla, proj, wa_pad, sp["gla_b_alpha"],
                                                         sp["gla_out_norm_g"], states, bsz)
    gw["gla_w_alpha"] = _mm(proj, dz, "tn", (F32,), "gla_alpha_dw", a_off=43, m=LANES)[:GLR].T.astype(BF16)
    dmisc_gla = _mm(dz, wa_pad, "nt", (F32,), "gla_alpha_dx")
    gw["mla_w_o"] = _unpad_wo(_mm(o2, dy_b, "tn", (BF16,), "mla_out_dw"))
    do2 = _mm(dy_b, wo_pad, "nt", (BF16,), "mla_out_dx")
    dq2, dk2, dv2 = _attn_bwd(q2, k2, v2, do2, o2, lse2, bsz)
    dcq, dckv, dmisc, gwuq, gwukv, acc_mla = _mla_pre_bwd(dq2, dk2, dv2, dmisc_gla, proj, pos, fr, sg, sp["mla_q_lat_g"],
                                                         sp["mla_kv_lat_g"], qn_g, kn_g, wuq, wukv)
    gw["mla_w_uq"] = _unpad_wuq(gwuq).T.astype(BF16)
    gw["mla_w_ukv"] = _unpad_wukv(gwukv).T.astype(BF16)
    dproj = jnp.concatenate([dq_g, dk_g, dv_g, dgg, dlogits, dcq, dckv, dmisc], axis=1)
    gw["w_in"] = _unperm_w_in_t(_mm(dproj, h, "tn", (BF16,), "proj_dw"))
    dh = _mm(dproj, w_in_t, "nn", (F32,), "proj_dx")
    grad_x, accb1, accg1 = _norm_mod_bwd(x2, dh, dx1, sp["norm1_g"], mod3, 0, 1, "norm1_bwd")

    dmod = jnp.stack([accb1[:, 0], accb1[:, 1], accb2[:, 2], accb2[:, 0], accb2[:, 1], acc_g2[:, 0]], axis=1)
    dmod8 = jnp.pad(dmod.reshape(bsz, 6 * D), ((0, 8 - bsz), (0, 0)))
    gw["w_ada"] = _mm(dmod8, c8, "tn", (BF16,), "ada_dw", pro_b=_silu)

    rows = {
        "dmod": dmod.reshape(bsz * 6, D),
        "norm1_g": accg1[0:1], "norm2_g": accg2[0:1],
        "b_merge": acc_bm[0:1].reshape(2, D),
        "gla_b_alpha": _pad_lanes(acc_ba[0:1], D),
        "gla_out_norm_g": _pad_lanes(acc_go.reshape(GH, 8, GDV)[:, 0, :], D),
        "mla_q_lat_g": _pad_lanes(acc_mla[0:1], D), "mla_kv_lat_g": _pad_lanes(acc_mla[1:2], D),
        "mla_qn_g": _pad_lanes(acc_mla[2:3], D), "mla_kn_g": _pad_lanes(acc_mla[3:4], D),
        "loss": acc_loss[0:1],
    }
    return grad_x.reshape(bsz, s, D), gw, rows


HBM_SPEC = pl.BlockSpec(memory_space=pltpu.HBM)


def _all_gather(p, name):
    r, cdim = p.shape

    def body(p_ref, out_ref, send_sems, recv_sems, local_sem):
        x, y, c = lax.axis_index("x"), lax.axis_index("y"), lax.axis_index("c")
        me, sibling = (x, y, c), (x, y, 1 - c)
        chips = [(1 - x, y), (x, 1 - y), (1 - x, 1 - y)]

        def slot(px, py, pc):
            return out_ref.at[4 * px + 2 * py + pc]

        def copy(k, block, to, src=None):
            return pltpu.make_async_remote_copy(
                src_ref=slot(*block) if src is None else src, dst_ref=slot(*block),
                send_sem=send_sems.at[k], recv_sem=recv_sems.at[k], device_id=to, device_id_type=MESH)

        mine = pltpu.make_async_copy(p_ref, slot(*me), local_sem)
        mine.start()
        first = [copy(0, me, sibling, src=p_ref)] + [copy(1 + j, me, (*chip, c), src=p_ref) for j, chip in enumerate(chips)]
        for cp in first:
            cp.start()
        passed = [copy(4 + j, (*chip, c), sibling) for j, chip in enumerate(chips)]
        for j, chip in enumerate(chips):
            copy(1 + j, (*chip, c), me).wait_recv()
            passed[j].start()
        copy(0, sibling, me).wait_recv()
        for j, chip in enumerate(chips):
            copy(4 + j, (*chip, 1 - c), me).wait_recv()
        for cp in first + passed:
            cp.wait_send()
        mine.wait()

    return pl.pallas_call(
        body, name=name, out_shape=jax.ShapeDtypeStruct((N_DEV, r, cdim), p.dtype),
        in_specs=[HBM_SPEC], out_specs=HBM_SPEC,
        scratch_shapes=[pltpu.SemaphoreType.DMA((7,)), pltpu.SemaphoreType.DMA((7,)), pltpu.SemaphoreType.DMA(())],
    )(p)


def _sibling_exchange(g):
    def body(g_ref, out_ref, send_sem, recv_sem):
        x, y, c = lax.axis_index("x"), lax.axis_index("y"), lax.axis_index("c")
        cp = pltpu.make_async_remote_copy(src_ref=g_ref, dst_ref=out_ref, send_sem=send_sem, recv_sem=recv_sem,
                                          device_id=(x, y, 1 - c), device_id_type=MESH)
        cp.start()
        cp.wait()

    return pl.pallas_call(
        body, name="rs_sibling_exchange", out_shape=jax.ShapeDtypeStruct(g.shape, g.dtype),
        in_specs=[HBM_SPEC], out_specs=HBM_SPEC,
        scratch_shapes=[pltpu.SemaphoreType.DMA(()), pltpu.SemaphoreType.DMA(())],
    )(g)


def _chip_exchange(tsum):
    def body(t_ref, out_ref, send_sems, recv_sems, local_sem):
        x, y, c = lax.axis_index("x"), lax.axis_index("y"), lax.axis_index("c")
        my_chip = 2 * x + y
        chips = [(1 - x, y), (x, 1 - y), (1 - x, 1 - y)]
        mine = pltpu.make_async_copy(t_ref.at[my_chip], out_ref.at[my_chip], local_sem)
        mine.start()
        sends = [pltpu.make_async_remote_copy(
            src_ref=t_ref.at[2 * px + py], dst_ref=out_ref.at[my_chip], send_sem=send_sems.at[j], recv_sem=recv_sems.at[j],
            device_id=(px, py, c), device_id_type=MESH) for j, (px, py) in enumerate(chips)]
        for cp in sends:
            cp.start()
        for j, (px, py) in enumerate(chips):
            pltpu.make_async_remote_copy(
                src_ref=t_ref.at[my_chip], dst_ref=out_ref.at[2 * px + py], send_sem=send_sems.at[j],
                recv_sem=recv_sems.at[j], device_id=(px, py, c), device_id_type=MESH).wait_recv()
        for cp in sends:
            cp.wait_send()
        mine.wait()

    return pl.pallas_call(
        body, name="rs_chip_exchange", out_shape=jax.ShapeDtypeStruct(tsum.shape, tsum.dtype),
        in_specs=[HBM_SPEC], out_specs=HBM_SPEC,
        scratch_shapes=[pltpu.SemaphoreType.DMA((3,)), pltpu.SemaphoreType.DMA((3,)), pltpu.SemaphoreType.DMA(())],
    )(tsum)


def _pair_sum(a, b):
    n, r, cdim = a.shape
    blk = pl.BlockSpec((1, PACK_BLOCK, cdim), lambda j, i: (j, i, 0))

    def body(a_ref, b_ref, o_ref):
        o_ref[...] = (a_ref[...].astype(F32) + b_ref[...].astype(F32)).astype(BF16)

    return pl.pallas_call(
        body, name="rs_pair_sum", grid=(n, r // PACK_BLOCK), in_specs=[blk, blk], out_specs=blk,
        out_shape=jax.ShapeDtypeStruct(a.shape, BF16), compiler_params=_cparams(("arbitrary", "arbitrary")),
    )(a, b)


def _adamw_math(w, g, m, v):
    m = ADAM_B1 * m + (1.0 - ADAM_B1) * g
    v = ADAM_B2 * v + (1.0 - ADAM_B2) * jnp.square(g)
    m_hat = m / (1.0 - ADAM_B1 ** ADAM_STEP)
    v_hat = v / (1.0 - ADAM_B2 ** ADAM_STEP)
    delta = -ADAM_LR * (m_hat / (jnp.sqrt(v_hat) + ADAM_EPS) + ADAM_WD * w)
    return delta, m, v


def _slab_sum(parts):
    _, r, cdim = parts.shape
    blk = pl.BlockSpec((PACK_BLOCK, cdim), lambda i: (i, 0))

    def body(p_ref, g_out):
        g = p_ref[0].astype(F32)
        for j in range(1, 4):
            g = g + p_ref[j].astype(F32)
        g_out[...] = g

    return pl.pallas_call(
        body, name="rs_slab_sum", grid=(r // PACK_BLOCK,),
        in_specs=[pl.BlockSpec((4, PACK_BLOCK, cdim), lambda i: (0, i, 0))], out_specs=blk,
        out_shape=jax.ShapeDtypeStruct((r, cdim), F32), compiler_params=_cparams(("arbitrary",)),
    )(parts)


def _adamw(g, w, m, v, name):
    r, cdim = w.shape
    rb = _tile(r, 256)
    blk = pl.BlockSpec((rb, cdim), lambda i: (i, 0))

    def body(g_ref, w_ref, m_ref, v_ref, d_out, m_out, v_out):
        d_out[...], m_out[...], v_out[...] = _adamw_math(w_ref[...], g_ref[...], m_ref[...], v_ref[...])

    return pl.pallas_call(
        body, name=name, grid=(r // rb,), in_specs=[blk] * 4, out_specs=[blk] * 3,
        out_shape=[jax.ShapeDtypeStruct((r, cdim), F32)] * 3, compiler_params=_cparams(("arbitrary",)),
    )(g, w, m, v)


def _adamw_small(parts, w, m, v):
    def body(p_ref, w_ref, m_ref, v_ref, g_out, d_out, m_out, v_out, loss_out):
        def total(srcs):
            acc = None
            for r in srcs:
                for j in range(N_DEV):
                    term = p_ref[j, r:r + 1, :]
                    acc = term if acc is None else acc + term
            return acc

        for prow, srcs in enumerate(SMALL_SOURCES):
            one = slice(prow, prow + 1)
            g = total(srcs)
            g_out[one, :] = g
            d_out[one, :], m_out[one, :], v_out[one, :] = _adamw_math(w_ref[one, :], g, m_ref[one, :], v_ref[one, :])
        loss_out[...] = jnp.broadcast_to(jnp.sum(total(LOSS_SOURCE), axis=1, keepdims=True), (8, LANES))

    full = lambda shp: pl.BlockSpec(shp, lambda i: (0,) * len(shp))
    return pl.pallas_call(
        body, name="adamw_small", grid=(1,),
        in_specs=[full((N_DEV, SMALL_ROWS, D)), full((16, D)), full((16, D)), full((16, D))],
        out_specs=[full((16, D))] * 4 + [full((8, LANES))],
        out_shape=[jax.ShapeDtypeStruct((16, D), F32)] * 4 + [jax.ShapeDtypeStruct((8, LANES), F32)],
        compiler_params=_cparams(("arbitrary",)),
    )(parts, w, m, v)


def _to_slab(shard, form):
    if form == "N":
        return shard
    return shard.T if form == "T" else shard.T.reshape(-1, D)


def _from_slab(block, form, shard_shape):
    if form == "N":
        return block
    return block.T if form == "T" else block.reshape(shard_shape[1], shard_shape[0]).T


def _gathered_full(g, form, shard_shape):
    if form == "TR":
        return g.reshape(N_DEV * shard_shape[1], shard_shape[0])
    return g.reshape(N_DEV * g.shape[1], D)


def _small_pack(vals):
    rows = []
    for n in SMALL:
        v = vals[n].reshape(-1)
        k = -(-v.shape[0] // D)
        rows.append(jnp.pad(v, (0, k * D - v.shape[0])).reshape(k, D))
    return jnp.concatenate(rows, axis=0)


def _small_unpack(packed, shapes):
    out = {}
    for n in SMALL:
        k = shapes[n][-1]
        r0 = SMALL_ROW[n]
        out[n] = packed[r0:r0 + -(-k // D)].reshape(-1)[:k].reshape(shapes[n])
    return out


def kernel(x, c, positions, w_ada, b_ada, norm1_g, w_in, b_merge, gla_w_alpha, gla_b_alpha, gla_out_norm_g, gla_w_o, mla_q_lat_g, mla_w_uq, mla_kv_lat_g, mla_w_ukv, mla_qn_g, mla_kn_g, mla_w_o, w_out, norm2_g, mlp_w1, mlp_w2, loss_target, m_w_ada, m_b_ada, m_norm1_g, m_w_in, m_b_merge, m_gla_w_alpha, m_gla_b_alpha, m_gla_out_norm_g, m_gla_w_o, m_mla_q_lat_g, m_mla_w_uq, m_mla_kv_lat_g, m_mla_w_ukv, m_mla_qn_g, m_mla_kn_g, m_mla_w_o, m_w_out, m_norm2_g, m_mlp_w1, m_mlp_w2, v_w_ada, v_b_ada, v_norm1_g, v_w_in, v_b_merge, v_gla_w_alpha, v_gla_b_alpha, v_gla_out_norm_g, v_gla_w_o, v_mla_q_lat_g, v_mla_w_uq, v_mla_kv_lat_g, v_mla_w_ukv, v_mla_qn_g, v_mla_kn_g, v_mla_w_o, v_w_out, v_norm2_g, v_mlp_w1, v_mlp_w2):
    args = dict(locals())
    wts = {n: args[n][0] for n in WEIGHTS}
    mom = {n: args["m_" + n][0] for n in WEIGHTS}
    var = {n: args["v_" + n][0] for n in WEIGHTS}
    my_c = lax.axis_index("c")
    pad_rows = PACK_ROWS - sum(r for _, r, _ in SLAB)

    slab = jnp.concatenate([_to_slab(wts[n], form).astype(BF16) for n, _, form in SLAB] + [jnp.zeros((pad_rows, D), BF16)],
                           axis=0)
    gathered = _all_gather(slab, "weights_all_gather")
    wt, off = {}, 0
    for n, r, form in SLAB:
        wt[n] = _gathered_full(gathered[:, off:off + r], form, wts[n].shape)
        off += r
    sp = {n: wts[n].reshape(1, -1) for n in SMALL}

    grad_x, gw, rows = _local_step(x, c, positions, loss_target, wt, sp)

    per_dev = jnp.concatenate([gw[n].reshape(N_DEV, r, D) for n, r, _ in SLAB] + [jnp.zeros((N_DEV, pad_rows, D), BF16)],
                              axis=1).reshape(4, 2, PACK_ROWS, D)
    keep = lax.dynamic_index_in_dim(per_dev, my_c, axis=1, keepdims=False)
    give = lax.dynamic_index_in_dim(per_dev, 1 - my_c, axis=1, keepdims=False)
    chip_sums = _pair_sum(keep, _sibling_exchange(give))
    gslab = _slab_sum(_chip_exchange(chip_sums))
    big, off = {}, 0
    for n, r, form in SLAB:
        g = _from_slab(gslab[off:off + r], form, wts[n].shape)
        big[n] = (g,) + tuple(_adamw(g, wts[n], mom[n], var[n], "adamw_" + n))
        off += r

    order = ["dmod", "norm1_g", "norm2_g", "b_merge", "gla_b_alpha", "gla_out_norm_g", "mla_q_lat_g", "mla_kv_lat_g",
             "mla_qn_g", "mla_kn_g", "loss"]
    part_rows = jnp.concatenate([rows[n] for n in order], axis=0)
    part_rows = jnp.pad(part_rows, ((0, SMALL_ROWS - part_rows.shape[0]), (0, 0)))
    all_rows = _all_gather(part_rows, "partials_all_gather")
    small = _adamw_small(all_rows, _small_pack({n: wts[n] for n in SMALL}), _small_pack({n: mom[n] for n in SMALL}),
                         _small_pack({n: var[n] for n in SMALL}))
    loss = small[4][0, 0]
    small_shapes = {n: wts[n].shape for n in SMALL}
    small = [_small_unpack(o, small_shapes) for o in small[:4]]

    outs = [loss, grad_x]
    for k in range(4):
        for n in WEIGHTS:
            val = big[n][k] if n in BIG else small[k][n]
            outs.append(val.reshape((1,) + tuple(wts[n].shape)))
    return tuple(outs)
```

```python
import functools

import jax
import jax.numpy as jnp
from jax import lax
from jax.experimental import pallas as pl
from jax.experimental.pallas import tpu as pltpu

F32 = jnp.float32
BF16 = jnp.bfloat16
MESH = pl.DeviceIdType.MESH

D = 1024
EPS = 1e-6
CHUNK = 64
GH, GDK, GDV, GLR, GTAU = 4, 128, 256, 16, 16.0
MH, MQR, MKVR, NOPE, ROPE, MV = 16, 256, 128, 64, 32, 64
MQK = NOPE + ROPE
HP = 128
FF = 4 * D
ROPE_THETA = 10000.0
IN_WIDTH = 5552
PW = 5632
N_DEV = 8
LANES = 128
PACK_ROWS = 2240
PACK_BLOCK = 320
ADA_COLS = 6 * D // N_DEV
SMALL_ROWS = 32
SMALL_SOURCES = tuple([(r, 6 + r) for r in range(6)] + [(12,), (13,), (14,), (15,), (16,), (17, 18, 19, 20),
                                                         (21,), (22,), (23,), (24,)])
LOSS_SOURCE = (25,)
VMEM_LIMIT = 56 * 1024 * 1024

ADAM_LR, ADAM_B1, ADAM_B2, ADAM_EPS, ADAM_WD, ADAM_STEP = 0.001, 0.9, 0.999, 1e-08, 0.01, 10

SLAB = (("w_in", 694, "T"), ("gla_w_alpha", 1, "TR"), (None, 9, None), ("mlp_w1", 512, "T"), ("gla_w_o", 128, "N"),
        ("mla_w_o", 128, "N"), ("w_out", 128, "N"), ("mlp_w2", 512, "N"), ("mla_w_uq", 48, "TR"), ("mla_w_ukv", 32, "TR"),
        (None, 48, None))
assert sum(r for _, r, _ in SLAB) == PACK_ROWS
BIG = ("w_ada",) + tuple(n for n, _, _ in SLAB if n is not None)
SMALL = ("b_ada", "norm1_g", "norm2_g", "b_merge", "gla_b_alpha", "gla_out_norm_g", "mla_q_lat_g", "mla_kv_lat_g",
         "mla_qn_g", "mla_kn_g")
SMALL_ROW = {"b_ada": 0, "norm1_g": 6, "norm2_g": 7, "b_merge": 8, "gla_b_alpha": 10, "gla_out_norm_g": 11,
             "mla_q_lat_g": 12, "mla_kv_lat_g": 13, "mla_qn_g": 14, "mla_kn_g": 15}
WEIGHTS = ("w_ada", "b_ada", "norm1_g", "w_in", "b_merge", "gla_w_alpha", "gla_b_alpha", "gla_out_norm_g", "gla_w_o",
           "mla_q_lat_g", "mla_w_uq", "mla_kv_lat_g", "mla_w_ukv", "mla_qn_g", "mla_kn_g", "mla_w_o", "w_out",
           "norm2_g", "mlp_w1", "mlp_w2")


def _cparams(sem=None):
    return pltpu.CompilerParams(dimension_semantics=sem, vmem_limit_bytes=VMEM_LIMIT)


def _tile(n, pref):
    for t in (2048, 1024, 512, 256, 128):
        if t <= pref and n % t == 0:
            return t
    return n


def _dot(a, b, dims, precision=None):
    return lax.dot_general(a, b, (dims, ((), ())), preferred_element_type=F32, precision=precision)


NN = ((1,), (0,))
NT = ((1,), (1,))
TN = ((0,), (0,))


def _sigmoid(x):
    return 1.0 / (1.0 + jnp.exp(-x))


def _silu(x):
    return x * _sigmoid(x)


def _mm(a, b, mode, out_dtypes, name, *, pro=None, pro_b=None, epi=None, extras=(), a_off=0, m=None, tm=1024, tn=512,
        tk=1024):
    if mode == "tn":
        kc, n = b.shape
        m = a.shape[1] if m is None else m
    elif mode == "nn":
        m, kc = a.shape
        n = b.shape[1]
    else:
        m, kc = a.shape
        n = b.shape[0]
    tm, tn, tk = _tile(m, tm), _tile(n, tn), _tile(kc, tk)
    nk = kc // tk
    dims = {"nn": NN, "nt": NT, "tn": TN}[mode]
    if mode == "tn":
        a_spec = pl.BlockSpec((tk, tm), lambda i, j, k: (k, i + a_off))
    else:
        a_spec = pl.BlockSpec((tm, tk), lambda i, j, k: (i + a_off, k))
    if mode == "nt":
        b_spec = pl.BlockSpec((tn, tk), lambda i, j, k: (j, k))
    else:
        b_spec = pl.BlockSpec((tk, tn), lambda i, j, k: (k, j))
    o_spec = pl.BlockSpec((tm, tn), lambda i, j, k: (i, j))
    n_ex, n_out = len(extras), len(out_dtypes)

    def body(a_ref, b_ref, *rest):
        ex, outs, acc = rest[:n_ex], rest[n_ex:n_ex + n_out], rest[-1]
        k = pl.program_id(2)

        @pl.when(k == 0)
        def _():
            acc[...] = jnp.zeros_like(acc)

        av = a_ref[...]
        if pro is not None:
            av = pro(av)
        bv = b_ref[...]
        if pro_b is not None:
            bv = pro_b(bv)
        acc[...] += _dot(av.astype(BF16), bv.astype(BF16), dims)

        @pl.when(k == nk - 1)
        def _():
            res = (acc[...],) if epi is None else epi(acc[...], *[e[...] for e in ex])
            for o_ref, r in zip(outs, res):
                o_ref[...] = r.astype(o_ref.dtype)

    out = pl.pallas_call(
        body, name=name, grid=(m // tm, n // tn, nk),
        in_specs=[a_spec, b_spec] + [o_spec] * n_ex,
        out_specs=[o_spec] * n_out,
        out_shape=[jax.ShapeDtypeStruct((m, n), dt) for dt in out_dtypes],
        scratch_shapes=[pltpu.VMEM((tm, tn), F32)],
        compiler_params=_cparams(("parallel", "parallel", "arbitrary")),
    )(a, b, *extras)
    return out[0] if n_out == 1 else out


def _rows(s):
    return _tile(s, 512)


def _mod_spec():
    return pl.BlockSpec((1, 6, D), lambda b, i: (b, 0, 0))


def _tok_spec(tr, nb, width=D, col=0):
    return pl.BlockSpec((tr, width), lambda b, i: (b * nb + i, col))


def _norm_mod_fwd(x, g, mod3, i_shift, i_scale, name, mixed=None, i_gate=None):
    bsz, _, _ = mod3.shape
    t = x.shape[0]
    s = t // bsz
    tr = _rows(s)
    nb = s // tr
    has_res = mixed is not None

    def body(*refs):
        if has_res:
            x_ref, mx_ref, g_ref, mod_ref, x1_ref, h_ref = refs
            xv = x_ref[...] + mod_ref[0, i_gate:i_gate + 1, :] * mx_ref[...]
            x1_ref[...] = xv
        else:
            x_ref, g_ref, mod_ref, h_ref = refs
            xv = x_ref[...]
        r = lax.rsqrt(jnp.mean(xv * xv, axis=1, keepdims=True) + EPS)
        hn = (xv * r) * g_ref[...]
        h = hn * (1.0 + mod_ref[0, i_scale:i_scale + 1, :]) + mod_ref[0, i_shift:i_shift + 1, :]
        h_ref[...] = h.astype(BF16)

    tok = _tok_spec(tr, nb)
    gspec = pl.BlockSpec((1, D), lambda b, i: (0, 0))
    ins = [x] + ([mixed] if has_res else []) + [g, mod3]
    in_specs = [tok] + ([tok] if has_res else []) + [gspec, _mod_spec()]
    out_shape = ([jax.ShapeDtypeStruct((t, D), F32)] if has_res else []) + [jax.ShapeDtypeStruct((t, D), BF16)]
    out = pl.pallas_call(
        body, name=name, grid=(bsz, nb), in_specs=in_specs, out_specs=[tok] * len(out_shape), out_shape=out_shape,
        compiler_params=_cparams(("arbitrary", "arbitrary")),
    )(*ins)
    return (out[0], out[1]) if has_res else (None, out[0])


def _norm_mod_bwd(x, dh, dres, g, mod3, i_shift, i_scale, name, mixed=None, i_gate=None):
    bsz = mod3.shape[0]
    t = x.shape[0]
    s = t // bsz
    tr = _rows(s)
    nb = s // tr
    has_res = mixed is not None

    def body(*refs):
        if has_res:
            x_ref, dh_ref, dres_ref, mx_ref, g_ref, mod_ref, dx_ref, dmx_ref, accb, accg = refs
        else:
            x_ref, dh_ref, dres_ref, g_ref, mod_ref, dx_ref, accb, accg = refs
        b, i = pl.program_id(0), pl.program_id(1)

        @pl.when(i == 0)
        def _():
            accb[...] = jnp.zeros_like(accb)

        @pl.when((i == 0) & (b == 0))
        def _():
            accg[...] = jnp.zeros_like(accg)

        xv, dhv, gv = x_ref[...], dh_ref[...], g_ref[...]
        r = lax.rsqrt(jnp.mean(xv * xv, axis=1, keepdims=True) + EPS)
        xn = xv * r
        accb[0, 0:1, :] += jnp.sum(dhv, axis=0, keepdims=True)
        accb[0, 1:2, :] += jnp.sum(dhv * (xn * gv), axis=0, keepdims=True)
        tt = dhv * (1.0 + mod_ref[0, i_scale:i_scale + 1, :])
        accg[0:1, :] += jnp.sum(tt * xn, axis=0, keepdims=True)
        dxn = tt * gv
        dx = dres_ref[...] + r * (dxn - xn * jnp.mean(dxn * xn, axis=1, keepdims=True))
        dx_ref[...] = dx
        if has_res:
            accb[0, 2:3, :] += jnp.sum(dx * mx_ref[...], axis=0, keepdims=True)
            dmx_ref[...] = (dx * mod_ref[0, i_gate:i_gate + 1, :]).astype(BF16)

    tok = _tok_spec(tr, nb)
    gspec = pl.BlockSpec((1, D), lambda b, i: (0, 0))
    ins = [x, dh, dres] + ([mixed] if has_res else []) + [g, mod3]
    in_specs = [tok] * (4 if has_res else 3) + [gspec, _mod_spec()]
    out_shape = [jax.ShapeDtypeStruct((t, D), F32)] + ([jax.ShapeDtypeStruct((t, D), BF16)] if has_res else [])
    out_specs = [tok] * len(out_shape)
    out_shape += [jax.ShapeDtypeStruct((bsz, 8, D), F32), jax.ShapeDtypeStruct((8, D), F32)]
    out_specs += [pl.BlockSpec((1, 8, D), lambda b, i: (b, 0, 0)), pl.BlockSpec((8, D), lambda b, i: (0, 0))]
    return pl.pallas_call(
        body, name=name, grid=(bsz, nb), in_specs=in_specs, out_specs=out_specs, out_shape=out_shape,
        compiler_params=_cparams(("arbitrary", "arbitrary")),
    )(*ins)


def _loss_head(x1, ff, tgt, mod3):
    bsz = mod3.shape[0]
    t = x1.shape[0]
    s = t // bsz
    tr = _rows(s)
    nb = s // tr

    def body(x1_ref, ff_ref, tg_ref, mod_ref, dy_ref, dff_ref, accb, accl):
        b, i = pl.program_id(0), pl.program_id(1)

        @pl.when(i == 0)
        def _():
            accb[...] = jnp.zeros_like(accb)

        @pl.when((i == 0) & (b == 0))
        def _():
            accl[...] = jnp.zeros_like(accl)

        gate = mod_ref[0, 5:6, :]
        ffv = ff_ref[...]
        err = x1_ref[...] + gate * ffv - tg_ref[...]
        accl[0:1, :] += jnp.sum(err * err, axis=0, keepdims=True) * (0.5 / D)
        dy = err * (1.0 / D)
        dy_ref[...] = dy
        dff_ref[...] = (dy * gate).astype(BF16)
        accb[0, 0:1, :] += jnp.sum(dy * ffv, axis=0, keepdims=True)

    tok = _tok_spec(tr, nb)
    return pl.pallas_call(
        body, name="loss_head", grid=(bsz, nb), in_specs=[tok, tok, tok, _mod_spec()],
        out_specs=[tok, tok, pl.BlockSpec((1, 8, D), lambda b, i: (b, 0, 0)), pl.BlockSpec((8, D), lambda b, i: (0, 0))],
        out_shape=[jax.ShapeDtypeStruct((t, D), F32), jax.ShapeDtypeStruct((t, D), BF16),
                   jax.ShapeDtypeStruct((bsz, 8, D), F32), jax.ShapeDtypeStruct((8, D), F32)],
        compiler_params=_cparams(("arbitrary", "arbitrary")),
    )(x1, ff, tgt, mod3)


def _merge_fwd(proj, b_merge, y_a, y_b):
    t = proj.shape[0]
    tr = _tile(t, 512)

    def body(la_ref, lb_ref, bm_ref, ya_ref, yb_ref, mix_ref):
        ga = _sigmoid(la_ref[...] + bm_ref[:, 0:D])
        gb = _sigmoid(lb_ref[...] + bm_ref[:, D:2 * D])
        mix_ref[...] = (ga * ya_ref[...] + gb * yb_ref[...]).astype(BF16)

    tok = pl.BlockSpec((tr, D), lambda i: (i, 0))
    return pl.pallas_call(
        body, name="merge_fwd", grid=(t // tr,),
        in_specs=[pl.BlockSpec((tr, D), lambda i: (i, 3)), pl.BlockSpec((tr, D), lambda i: (i, 4)),
                  pl.BlockSpec((1, 2 * D), lambda i: (0, 0)), tok, tok],
        out_specs=tok, out_shape=jax.ShapeDtypeStruct((t, D), BF16),
        compiler_params=_cparams(("arbitrary",)),
    )(proj, proj, b_merge, y_a, y_b)


def _merge_bwd(dmix, proj, b_merge, y_a, y_b):
    t = proj.shape[0]
    tr = _tile(t, 512)

    def body(dm_ref, la_ref, lb_ref, bm_ref, ya_ref, yb_ref, dya_ref, dyb_ref, dl_ref, acc):
        @pl.when(pl.program_id(0) == 0)
        def _():
            acc[...] = jnp.zeros_like(acc)

        dm = dm_ref[...]
        ga = _sigmoid(la_ref[...] + bm_ref[:, 0:D])
        gb = _sigmoid(lb_ref[...] + bm_ref[:, D:2 * D])
        dya_ref[...] = (dm * ga).astype(BF16)
        dyb_ref[...] = (dm * gb).astype(BF16)
        dla = dm * ya_ref[...] * ga * (1.0 - ga)
        dlb = dm * yb_ref[...] * gb * (1.0 - gb)
        dl_ref[:, 0:D] = dla.astype(BF16)
        dl_ref[:, D:2 * D] = dlb.astype(BF16)
        acc[0:1, 0:D] += jnp.sum(dla, axis=0, keepdims=True)
        acc[0:1, D:2 * D] += jnp.sum(dlb, axis=0, keepdims=True)

    tok = pl.BlockSpec((tr, D), lambda i: (i, 0))
    return pl.pallas_call(
        body, name="merge_bwd", grid=(t // tr,),
        in_specs=[tok, pl.BlockSpec((tr, D), lambda i: (i, 3)), pl.BlockSpec((tr, D), lambda i: (i, 4)),
                  pl.BlockSpec((1, 2 * D), lambda i: (0, 0)), tok, tok],
        out_specs=[tok, tok, pl.BlockSpec((tr, 2 * D), lambda i: (i, 0)), pl.BlockSpec((8, 2 * D), lambda i: (0, 0))],
        out_shape=[jax.ShapeDtypeStruct((t, D), BF16), jax.ShapeDtypeStruct((t, D), BF16),
                   jax.ShapeDtypeStruct((t, 2 * D), BF16), jax.ShapeDtypeStruct((8, 2 * D), F32)],
        compiler_params=_cparams(("arbitrary",)),
    )(dmix, proj, proj, b_merge, y_a, y_b)


def _log_sigmoid(z):
    return jnp.minimum(z, 0.0) - jnp.log(1.0 + jnp.exp(-jnp.abs(z)))


def _tri(lower):
    r = lax.broadcasted_iota(jnp.int32, (CHUNK, CHUNK), 0)
    c = lax.broadcasted_iota(jnp.int32, (CHUNK, CHUNK), 1)
    return jnp.where(r >= c if lower else r <= c, 1.0, 0.0).astype(F32)


def _gla_fwd(proj, wa_pad, b_alpha, g_out, bsz):
    t = proj.shape[0]
    s = t // bsz
    nc = s // CHUNK

    def body(q_ref, k_ref, v_ref, gg_ref, ms_ref, wa_ref, ba_ref, go_ref, o_ref, og_ref, st_ref, la, state):
        z = _dot(ms_ref[...].astype(BF16), wa_ref[...], NN) + ba_ref[...]
        la[...] = _log_sigmoid(z) * (1.0 / GTAU)
        state[...] = jnp.zeros_like(state)
        low = _tri(True)
        gout = go_ref[...]

        def chunk(n, carry):
            rows = pl.ds(pl.multiple_of(n * CHUNK, CHUNK), CHUNK)
            lac = la[rows, :]
            cum = _dot(low, lac, NN, lax.Precision.HIGHEST)
            ce = jnp.sum(lac, axis=0, keepdims=True)
            kd = (k_ref[rows, :] * jnp.exp(ce - cum)).astype(BF16)
            new = state[...] * jnp.exp(ce) + _dot(v_ref[rows, :].astype(BF16), kd, TN)
            state[...] = new
            st_ref[pl.ds(pl.multiple_of(n * GDV, GDV), GDV), :] = new
            qs = (q_ref[rows, :] * (GDK ** -0.5)).astype(BF16)
            o = _dot(qs, new.astype(BF16), NT)
            o_ref[rows, :] = o
            ro = lax.rsqrt(jnp.mean(o * o, axis=1, keepdims=True) + EPS)
            og_ref[rows, :] = (((o * ro) * gout) * _silu(gg_ref[rows, :])).astype(BF16)
            return carry

        lax.fori_loop(0, nc, chunk, 0, unroll=8)

    hk = pl.BlockSpec((s, GDK), lambda b, h: (b, h))
    return pl.pallas_call(
        body, name="gla_fwd", grid=(bsz, GH),
        in_specs=[hk, pl.BlockSpec((s, GDK), lambda b, h: (b, GH + h)), pl.BlockSpec((s, GDV), lambda b, h: (b, 4 + h)),
                  pl.BlockSpec((s, GDV), lambda b, h: (b, 8 + h)), pl.BlockSpec((s, LANES), lambda b, h: (b, 43)),
                  pl.BlockSpec((LANES, GDK), lambda b, h: (0, h)), pl.BlockSpec((1, GDK), lambda b, h: (0, h)),
                  pl.BlockSpec((1, GDV), lambda b, h: (0, 0))],
        out_specs=[pl.BlockSpec((s, GDV), lambda b, h: (b, h)), pl.BlockSpec((s, GDV), lambda b, h: (b, h)),
                   pl.BlockSpec((nc * GDV, GDK), lambda b, h: (b * GH + h, 0))],
        out_shape=[jax.ShapeDtypeStruct((t, GH * GDV), F32), jax.ShapeDtypeStruct((t, GH * GDV), BF16),
                   jax.ShapeDtypeStruct((bsz * GH * nc * GDV, GDK), F32)],
        scratch_shapes=[pltpu.VMEM((s, GDK), F32), pltpu.VMEM((GDV, GDK), F32)],
        compiler_params=_cparams(("arbitrary", "arbitrary")),
    )(proj, proj, proj, proj, proj, wa_pad, b_alpha, g_out)


def _gla_bwd(dog, o, proj, wa_pad, b_alpha, g_out, states, bsz):
    t = proj.shape[0]
    s = t // bsz
    nc = s // CHUNK

    def body(dog_ref, o_ref, q_ref, k_ref, v_ref, gg_ref, ms_ref, wa_ref, ba_ref, go_ref, st_ref,
             dq_ref, dk_ref, dv_ref, dgg_ref, dz_ref, dba, dgo, zs, la, carry_g):
        @pl.when(pl.program_id(1) == 0)
        def _():
            dba[...] = jnp.zeros_like(dba)
            dgo[...] = jnp.zeros_like(dgo)

        z = _dot(ms_ref[...].astype(BF16), wa_ref[...], NN) + ba_ref[...]
        zs[...] = z
        la[...] = _log_sigmoid(z) * (1.0 / GTAU)
        carry_g[...] = jnp.zeros_like(carry_g)
        low, upp = _tri(True), _tri(False)
        gout = go_ref[...]
        last_row = lax.broadcasted_iota(jnp.int32, (CHUNK, GDK), 0) == CHUNK - 1

        def chunk(step, carry):
            n = nc - 1 - step
            rows = pl.ds(pl.multiple_of(n * CHUNK, CHUNK), CHUNK)
            lac = la[rows, :]
            cum = _dot(low, lac, NN, lax.Precision.HIGHEST)
            ce = jnp.sum(lac, axis=0, keepdims=True)
            e = jnp.exp(ce - cum)
            dec = jnp.exp(ce)
            kf = k_ref[rows, :]
            kd = (kf * e).astype(BF16)
            vv = v_ref[rows, :].astype(BF16)
            qs = (q_ref[rows, :] * (GDK ** -0.5)).astype(BF16)
            ov = o_ref[rows, :]
            ro = lax.rsqrt(jnp.mean(ov * ov, axis=1, keepdims=True) + EPS)
            on = ov * ro
            gg = gg_ref[rows, :]
            sg = _sigmoid(gg)
            dogv = dog_ref[rows, :]
            dgg_ref[rows, :] = (dogv * (on * gout) * (sg * (1.0 + gg * (1.0 - sg)))).astype(BF16)
            t1 = dogv * (gg * sg)
            dgo[0:1, :] += jnp.sum(t1 * on, axis=0, keepdims=True)
            don = t1 * gout
            do = ro * (don - on * jnp.mean(don * on, axis=1, keepdims=True))
            dob = do.astype(BF16)
            st_n = st_ref[pl.ds(pl.multiple_of(n * GDV, GDV), GDV), :]
            dq_ref[rows, :] = (_dot(dob, st_n.astype(BF16), NN) * (GDK ** -0.5)).astype(BF16)
            dn = carry_g[...] + _dot(dob, qs, TN)
            prev = jnp.maximum(n - 1, 0)
            st_p = st_ref[pl.ds(pl.multiple_of(prev * GDV, GDV), GDV), :] * jnp.where(n > 0, 1.0, 0.0)
            ddec = jnp.sum(dn * st_p, axis=0, keepdims=True)
            dnb = dn.astype(BF16)
            dkd = _dot(vv, dnb, NN)
            dv_ref[rows, :] = _dot(kd, dnb, NT).astype(BF16)
            dk_ref[rows, :] = (dkd * e).astype(BF16)
            w = dkd * kf * e
            dce = jnp.sum(w, axis=0, keepdims=True) + ddec * dec
            dcum = jnp.where(last_row, dce - w, -w)
            dla = _dot(upp, dcum, NN, lax.Precision.HIGHEST)
            dz = dla * (1.0 / GTAU) * _sigmoid(-zs[rows, :])
            dba[0:1, :] += jnp.sum(dz, axis=0, keepdims=True)
            dz_ref[rows, :] = dz.astype(BF16)
            carry_g[...] = dn * dec
            return carry

        lax.fori_loop(0, nc, chunk, 0, unroll=8)

    hv = pl.BlockSpec((s, GDV), lambda h, b: (b, h))
    hk = pl.BlockSpec((s, GDK), lambda h, b: (b, h))
    return pl.pallas_call(
        body, name="gla_bwd", grid=(GH, bsz),
        in_specs=[hv, hv, hk, pl.BlockSpec((s, GDK), lambda h, b: (b, GH + h)),
                  pl.BlockSpec((s, GDV), lambda h, b: (b, 4 + h)), pl.BlockSpec((s, GDV), lambda h, b: (b, 8 + h)),
                  pl.BlockSpec((s, LANES), lambda h, b: (b, 43)), pl.BlockSpec((LANES, GDK), lambda h, b: (0, h)),
                  pl.BlockSpec((1, GDK), lambda h, b: (0, h)), pl.BlockSpec((1, GDV), lambda h, b: (0, 0)),
                  pl.BlockSpec((nc * GDV, GDK), lambda h, b: (b * GH + h, 0))],
        out_specs=[hk, hk, hv, hv, hk, pl.BlockSpec((8, GDK), lambda h, b: (0, h)),
                   pl.BlockSpec((8, GDV), lambda h, b: (h, 0))],
        out_shape=[jax.ShapeDtypeStruct((t, GH * GDK), BF16), jax.ShapeDtypeStruct((t, GH * GDK), BF16),
                   jax.ShapeDtypeStruct((t, GH * GDV), BF16), jax.ShapeDtypeStruct((t, GH * GDV), BF16),
                   jax.ShapeDtypeStruct((t, GH * GDK), BF16), jax.ShapeDtypeStruct((8, GH * GDK), F32),
                   jax.ShapeDtypeStruct((8 * GH, GDV), F32)],
        scratch_shapes=[pltpu.VMEM((s, GDK), F32), pltpu.VMEM((s, GDK), F32), pltpu.VMEM((GDV, GDK), F32)],
        compiler_params=_cparams(("arbitrary", "arbitrary")),
    )(dog, o, proj, proj, proj, proj, proj, wa_pad, b_alpha, g_out, states)


def _rope_tables(pos_ref, fr_ref, sg_ref):
    ang = pos_ref[...].astype(F32) * fr_ref[...]
    return jnp.cos(ang), jnp.sin(ang) * sg_ref[...]


def _partner(x):
    lane = lax.broadcasted_iota(jnp.int32, x.shape, 1)
    return jnp.where(lane < NOPE + ROPE // 2, pltpu.roll(x, LANES - ROPE // 2, 1), pltpu.roll(x, ROPE // 2, 1))


def _mla_rows(t):
    return _tile(t, 256)


def _mla_pre_fwd(proj, pos, fr, sg, q_lat_g, kv_lat_g, qn_g, kn_g, wuq, wukv):
    t = proj.shape[0]
    tr = _mla_rows(t)

    def body(cq_ref, ckv_ref, ms_ref, pos_ref, fr_ref, sg_ref, qlg, kvlg, qng, kng, wuq_ref, wukv_ref, q_out, k_out, v_out):
        cos, sin = _rope_tables(pos_ref, fr_ref, sg_ref)
        cq = cq_ref[...]
        cqn = (cq * lax.rsqrt(jnp.mean(cq * cq, axis=1, keepdims=True) + EPS) * qlg[...]).astype(BF16)
        ckv = ckv_ref[...]
        ckvn = (ckv * lax.rsqrt(jnp.mean(ckv * ckv, axis=1, keepdims=True) + EPS) * kvlg[...]).astype(BF16)
        lane = lax.broadcasted_iota(jnp.int32, (tr, HP), 1)
        kpe = jnp.where((lane >= NOPE) & (lane < MQK), ms_ref[...], 0.0)
        lane_all = lax.broadcasted_iota(jnp.int32, (tr, MH * HP), 1)
        v_out[...] = jnp.where(lane_all % HP == MV, 1.0, _dot(ckvn, wukv_ref[:, MH * HP:], NN)).astype(BF16)
        for h in range(MH):
            cols = slice(h * HP, (h + 1) * HP)
            qh = _dot(cqn, wuq_ref[:, cols], NN)
            qn = qh * lax.rsqrt(jnp.sum(qh * qh, axis=1, keepdims=True) * (1.0 / MQK) + EPS) * qng[...]
            q_out[:, cols] = (qn * cos + _partner(qn) * sin).astype(BF16)
            kh = _dot(ckvn, wukv_ref[:, cols], NN) + kpe
            kn = kh * lax.rsqrt(jnp.sum(kh * kh, axis=1, keepdims=True) * (1.0 / MQK) + EPS) * kng[...]
            k_out[:, cols] = (kn * cos + _partner(kn) * sin).astype(BF16)

    def full(a):
        return pl.BlockSpec(a.shape, lambda i: (0, 0))

    wide = pl.BlockSpec((tr, MH * HP), lambda i: (i, 0))
    return pl.pallas_call(
        body, name="mla_pre_fwd", grid=(t // tr,),
        in_specs=[pl.BlockSpec((tr, MQR), lambda i: (i, 20)), pl.BlockSpec((tr, MKVR), lambda i: (i, 42)),
                  pl.BlockSpec((tr, LANES), lambda i: (i, 43)), pl.BlockSpec((tr, 1), lambda i: (i, 0)),
                  full(fr), full(sg), full(q_lat_g), full(kv_lat_g), full(qn_g), full(kn_g), full(wuq), full(wukv)],
        out_specs=[wide, wide, wide],
        out_shape=[jax.ShapeDtypeStruct((t, MH * HP), BF16)] * 3,
        compiler_params=_cparams(("arbitrary",)),
    )(proj, proj, proj, pos, fr, sg, q_lat_g, kv_lat_g, qn_g, kn_g, wuq, wukv)


def _mla_pre_bwd(dq2, dk2, dv2, dmisc_gla, proj, pos, fr, sg, q_lat_g, kv_lat_g, qn_g, kn_g, wuq, wukv):
    t = proj.shape[0]
    tr = _mla_rows(t)

    def body(dq_ref, dk_ref, dv_ref, dmg_ref, cq_ref, ckv_ref, ms_ref, pos_ref, fr_ref, sg_ref, qlg, kvlg, qng, kng,
             wuq_ref, wukv_ref, dcq_ref, dckv_ref, dms_ref, dwuq, dwukv, acc, dqf, dkvf):
        @pl.when(pl.program_id(0) == 0)
        def _():
            dwuq[...] = jnp.zeros_like(dwuq)
            dwukv[...] = jnp.zeros_like(dwukv)
            acc[...] = jnp.zeros_like(acc)

        cos, sin = _rope_tables(pos_ref, fr_ref, sg_ref)
        cq = cq_ref[...]
        rc = lax.rsqrt(jnp.mean(cq * cq, axis=1, keepdims=True) + EPS)
        xc = cq * rc
        cqn = (xc * qlg[...]).astype(BF16)
        ckv = ckv_ref[...]
        rkv = lax.rsqrt(jnp.mean(ckv * ckv, axis=1, keepdims=True) + EPS)
        xkv = ckv * rkv
        ckvn = (xkv * kvlg[...]).astype(BF16)
        lane = lax.broadcasted_iota(jnp.int32, (tr, HP), 1)
        is_rope = (lane >= NOPE) & (lane < MQK)
        kpe = jnp.where(is_rope, ms_ref[...], 0.0)
        dkpe = jnp.zeros((tr, HP), F32)
        dqng = jnp.zeros((1, HP), F32)
        dkng = jnp.zeros((1, HP), F32)
        for h in range(MH):
            cols = slice(h * HP, (h + 1) * HP)
            qh = _dot(cqn, wuq_ref[:, cols], NN)
            rq = lax.rsqrt(jnp.sum(qh * qh, axis=1, keepdims=True) * (1.0 / MQK) + EPS)
            xq = qh * rq
            dy = dq_ref[:, cols].astype(F32)
            dqn = dy * cos - _partner(dy) * sin
            dqng += jnp.sum(dqn * xq, axis=0, keepdims=True)
            tq = dqn * qng[...]
            dqf[:, cols] = (rq * (tq - xq * (jnp.sum(tq * xq, axis=1, keepdims=True) * (1.0 / MQK)))).astype(BF16)
            kh = _dot(ckvn, wukv_ref[:, cols], NN) + kpe
            rk = lax.rsqrt(jnp.sum(kh * kh, axis=1, keepdims=True) * (1.0 / MQK) + EPS)
            xk = kh * rk
            dy = dk_ref[:, cols].astype(F32)
            dkn = dy * cos - _partner(dy) * sin
            dkng += jnp.sum(dkn * xk, axis=0, keepdims=True)
            tk = dkn * kng[...]
            dkh = rk * (tk - xk * (jnp.sum(tk * xk, axis=1, keepdims=True) * (1.0 / MQK)))
            dkvf[:, cols] = jnp.where(lane < NOPE, dkh, 0.0).astype(BF16)
            dkpe += jnp.where(is_rope, dkh, 0.0)
        dkvf[:, MH * HP:] = dv_ref[...]
        acc[2:3, 0:HP] += dqng
        acc[3:4, 0:HP] += dkng
        dms_ref[...] = (dmg_ref[...] + dkpe).astype(BF16)

        dqfv = dqf[...]
        dwuq[...] += _dot(cqn, dqfv, TN)
        dcqn = _dot(dqfv, wuq_ref[...], NT)
        acc[0:1, :] += jnp.sum(dcqn * xc, axis=0, keepdims=True)
        tc = dcqn * qlg[...]
        dcq_ref[...] = (rc * (tc - xc * jnp.mean(tc * xc, axis=1, keepdims=True))).astype(BF16)

        dkvfv = dkvf[...]
        dwukv[...] += _dot(ckvn, dkvfv, TN)
        dckvn = _dot(dkvfv, wukv_ref[...], NT)
        acc[1:2, 0:MKVR] += jnp.sum(dckvn * xkv, axis=0, keepdims=True)
        tkv = dckvn * kvlg[...]
        dckv_ref[...] = (rkv * (tkv - xkv * jnp.mean(tkv * xkv, axis=1, keepdims=True))).astype(BF16)

    def full(a):
        return pl.BlockSpec(a.shape, lambda i: (0, 0))

    wide = pl.BlockSpec((tr, MH * HP), lambda i: (i, 0))
    narrow = pl.BlockSpec((tr, LANES), lambda i: (i, 0))
    return pl.pallas_call(
        body, name="mla_pre_bwd", grid=(t // tr,),
        in_specs=[wide, wide, wide, narrow,
                  pl.BlockSpec((tr, MQR), lambda i: (i, 20)), pl.BlockSpec((tr, MKVR), lambda i: (i, 42)),
                  pl.BlockSpec((tr, LANES), lambda i: (i, 43)), pl.BlockSpec((tr, 1), lambda i: (i, 0)),
                  full(fr), full(sg), full(q_lat_g), full(kv_lat_g), full(qn_g), full(kn_g), full(wuq), full(wukv)],
        out_specs=[pl.BlockSpec((tr, MQR), lambda i: (i, 0)), narrow, narrow,
                   pl.BlockSpec((MQR, MH * HP), lambda i: (0, 0)), pl.BlockSpec((MKVR, 2 * MH * HP), lambda i: (0, 0)),
                   pl.BlockSpec((8, MQR), lambda i: (0, 0))],
        out_shape=[jax.ShapeDtypeStruct((t, MQR), BF16), jax.ShapeDtypeStruct((t, MKVR), BF16),
                   jax.ShapeDtypeStruct((t, LANES), BF16), jax.ShapeDtypeStruct((MQR, MH * HP), F32),
                   jax.ShapeDtypeStruct((MKVR, 2 * MH * HP), F32), jax.ShapeDtypeStruct((8, MQR), F32)],
        scratch_shapes=[pltpu.VMEM((tr, MH * HP), BF16), pltpu.VMEM((tr, 2 * MH * HP), BF16)],
        compiler_params=_cparams(("arbitrary",)),
    )(dq2, dk2, dv2, dmisc_gla, proj, proj, proj, pos, fr, sg, q_lat_g, kv_lat_g, qn_g, kn_g, wuq, wukv)


ATT_FWD_TILES = (1024, 512)
ATT_BWD_TILES = (512, 512)
ATT_HEADS = 2
NEG = -1e30
LOG2E = 1.4426950408889634


def _att_mask(q0, k0, tq, tk):
    qc = (q0 + lax.broadcasted_iota(jnp.int32, (tq, tk), 0)) // CHUNK
    kc = (k0 + lax.broadcasted_iota(jnp.int32, (tq, tk), 1)) // CHUNK
    return kc <= qc


def _att_tiles(s, tiles):
    return _tile(s, tiles[0]), _tile(s, tiles[1])


def _lanes(x, n):
    return x if n == 1 else jnp.concatenate([x] * n, axis=1)


def _attn_fwd(q2, k2, v2, bsz):
    t = q2.shape[0]
    s = t // bsz
    tq, tk = _att_tiles(s, ATT_FWD_TILES)
    nq, groups, n_diag = s // tq, tk // HP, max(tq // tk, 1)
    scale = MQK ** -0.5
    c2 = scale * LOG2E
    heads = range(ATT_HEADS)

    def body(q_ref, k_ref, v_ref, o_ref, lse_ref):
        def q_loop(qi, carry):
            q0 = pl.multiple_of(qi * tq, tq)
            rows = pl.ds(q0, tq)
            n_full = q0 // tk
            qs = [q_ref[rows, h * HP:(h + 1) * HP] for h in heads]

            def scores(h, kj, masked):
                k0 = pl.multiple_of(kj * tk, tk)
                sc = _dot(qs[h], k_ref[pl.ds(k0, tk), h * HP:(h + 1) * HP], NT)
                return jnp.where(_att_mask(q0, k0, tq, tk), sc, NEG) if masked else sc

            def fold(mx, sc):
                for j in range(groups):
                    mx = jnp.maximum(mx, sc[:, j * HP:(j + 1) * HP])
                return mx

            mx = lax.fori_loop(0, n_full, lambda kj, mx: tuple(fold(mx[h], scores(h, kj, False)) for h in heads),
                               tuple(jnp.full((tq, HP), NEG, F32) for _ in heads))
            for u in range(n_diag):
                mx = tuple(fold(mx[h], scores(h, n_full + u, True)) for h in heads)
            mb = [jnp.broadcast_to(jnp.max(mx[h], axis=1, keepdims=True), (tq, HP)) for h in heads]

            def weighted(h, kj, masked):
                p = jnp.exp2((scores(h, kj, masked) - _lanes(mb[h], groups)) * c2)
                k0 = pl.multiple_of(kj * tk, tk)
                return _dot(p.astype(BF16), v_ref[pl.ds(k0, tk), h * HP:(h + 1) * HP], NN)

            acc = lax.fori_loop(0, n_full, lambda kj, acc: tuple(acc[h] + weighted(h, kj, False) for h in heads),
                                tuple(jnp.zeros((tq, HP), F32) for _ in heads))
            for u in range(n_diag):
                acc = tuple(acc[h] + weighted(h, n_full + u, True) for h in heads)
            lane = lax.broadcasted_iota(jnp.int32, (tq, HP), 1)
            for h in heads:
                a = acc[h]
                l = jnp.sum(jnp.where(lane == MV, a, 0.0), axis=1, keepdims=True)
                o_ref[rows, h * HP:(h + 1) * HP] = (a / l).astype(BF16)
                lse_ref[rows, h * HP:(h + 1) * HP] = mb[h] * scale + jnp.log(l)
            return carry

        lax.fori_loop(0, nq, q_loop, 0)

    spec = pl.BlockSpec((s, ATT_HEADS * HP), lambda b, h: (b, h))
    return pl.pallas_call(
        body, name="attn_fwd", grid=(bsz, MH // ATT_HEADS), in_specs=[spec] * 3, out_specs=[spec, spec],
        out_shape=[jax.ShapeDtypeStruct((t, MH * HP), BF16), jax.ShapeDtypeStruct((t, MH * HP), F32)],
        compiler_params=_cparams(("arbitrary", "arbitrary")),
    )(q2, k2, v2)


def _attn_bwd(q2, k2, v2, do2, o2, lse2, bsz):
    t = q2.shape[0]
    s = t // bsz
    tq, tk = _att_tiles(s, ATT_BWD_TILES)
    nq, nk, per, groups = s // tq, s // tk, max(tk // tq, 1), tk // HP
    scale = MQK ** -0.5
    c2 = scale * LOG2E
    heads = range(ATT_HEADS)

    def body(q_ref, k_ref, v_ref, do_ref, o_ref, lse_ref, dq_ref, dk_ref, dv_ref, dq_acc, delta, lse_b2):
        dq_acc[...] = jnp.zeros_like(dq_acc)

        def d_loop(i, carry):
            rows = pl.ds(pl.multiple_of(i * tq, tq), tq)
            for h in heads:
                hs = slice(h * HP, (h + 1) * HP)
                dl = jnp.sum(do_ref[rows, hs].astype(F32) * o_ref[rows, hs].astype(F32), axis=1, keepdims=True)
                delta[rows, hs] = jnp.broadcast_to(dl, (tq, HP))
            lse_b2[rows, :] = lse_ref[rows, :] * LOG2E
            return carry

        lax.fori_loop(0, nq, d_loop, 0)

        def k_loop(kj, carry):
            k0 = pl.multiple_of(kj * tk, tk)
            kk = [k_ref[pl.ds(k0, tk), h * HP:(h + 1) * HP] for h in heads]
            vv = [v_ref[pl.ds(k0, tk), h * HP:(h + 1) * HP] for h in heads]

            def tile(qi, c, masked):
                q0 = pl.multiple_of(qi * tq, tq)
                rows = pl.ds(q0, tq)
                out = []
                for h in heads:
                    hs = slice(h * HP, (h + 1) * HP)
                    dk, dv = c[h]
                    q = q_ref[rows, hs]
                    do = do_ref[rows, hs]
                    e = _dot(q, kk[h], NT) * c2 - _lanes(lse_b2[rows, hs], groups)
                    if masked:
                        e = jnp.where(_att_mask(q0, k0, tq, tk), e, NEG)
                    p = jnp.exp2(e)
                    dv = dv + _dot(p.astype(BF16), do, TN)
                    ds = (p * (_dot(do, vv[h], NT) - _lanes(delta[rows, hs], groups))).astype(BF16)
                    dq_acc[rows, hs] += _dot(ds, kk[h], NN)
                    dk = dk + _dot(ds, q, TN)
                    out.append((dk, dv))
                return tuple(out)

            zero = jnp.zeros((tk, HP), F32)
            c = tuple((zero, zero) for _ in heads)
            first = k0 // tq
            for u in range(per):
                c = tile(first + u, c, True)
            c = lax.fori_loop(first + per, nq, lambda qi, c: tile(qi, c, False), c)
            for h in heads:
                dk_ref[pl.ds(k0, tk), h * HP:(h + 1) * HP] = (c[h][0] * scale).astype(BF16)
                dv_ref[pl.ds(k0, tk), h * HP:(h + 1) * HP] = c[h][1].astype(BF16)
            return carry

        lax.fori_loop(0, nk, k_loop, 0)
        dq_ref[...] = (dq_acc[...] * scale).astype(BF16)

    spec = pl.BlockSpec((s, ATT_HEADS * HP), lambda b, h: (b, h))
    return pl.pallas_call(
        body, name="attn_bwd", grid=(bsz, MH // ATT_HEADS), in_specs=[spec] * 6, out_specs=[spec] * 3,
        out_shape=[jax.ShapeDtypeStruct((t, MH * HP), BF16)] * 3,
        scratch_shapes=[pltpu.VMEM((s, ATT_HEADS * HP), F32)] * 3,
        compiler_params=_cparams(("arbitrary", "arbitrary")),
    )(q2, k2, v2, do2, o2, lse2)


def _perm_w_in_t(w):
    z = lambda n: jnp.zeros((n, w.shape[1]), w.dtype)
    return jnp.concatenate([w[:3072], w[3504:5552], w[3088:3344], w[3344:3472], w[3072:3088], z(48), w[3472:3504], z(32)],
                           axis=0)


def _unperm_w_in_t(g):
    return jnp.concatenate([g[:3072], g[5504:5520], g[5120:5376], g[5376:5504], g[5568:5600], g[3072:5120]], axis=0)


def _pad_wa(w):
    return jnp.pad(w, ((0, LANES - GLR), (0, 0)))


def _pad_wuq(w):
    return jnp.pad(w.reshape(MQR, MH, MQK), ((0, 0), (0, 0), (0, HP - MQK))).reshape(MQR, MH * HP)


def _unpad_wuq(g):
    return g.reshape(MQR, MH, HP)[:, :, :MQK].reshape(MQR, MH * MQK)


def _pad_wukv(w):
    w3 = w.reshape(MKVR, MH, NOPE + MV)
    kp = jnp.pad(w3[:, :, :NOPE], ((0, 0), (0, 0), (0, HP - NOPE))).reshape(MKVR, MH * HP)
    vp = jnp.pad(w3[:, :, NOPE:], ((0, 0), (0, 0), (0, HP - MV))).reshape(MKVR, MH * HP)
    return jnp.concatenate([kp, vp], axis=1)


def _unpad_wukv(g):
    kp = g[:, :MH * HP].reshape(MKVR, MH, HP)[:, :, :NOPE]
    vp = g[:, MH * HP:].reshape(MKVR, MH, HP)[:, :, :MV]
    return jnp.concatenate([kp, vp], axis=2).reshape(MKVR, MH * (NOPE + MV))


def _pad_wo(w):
    return jnp.pad(w.reshape(MH, MV, D), ((0, 0), (0, HP - MV), (0, 0))).reshape(MH * HP, D)


def _unpad_wo(g):
    return g.reshape(MH, HP, D)[:, :MV, :].reshape(MH * MV, D)


def _pad_lanes(v, n=HP):
    return jnp.pad(v, ((0, 0), (0, n - v.shape[1])))


def _local_step(x, positions, tgt, wt, sp, mod3):
    bsz, s, _ = x.shape
    t = bsz * s
    x2 = x.reshape(t, D)
    tgt2 = tgt.reshape(t, D)
    pos = positions.reshape(t, 1)
    fr16 = ROPE_THETA ** (-jnp.arange(0, ROPE, 2, dtype=F32) / ROPE)
    zero = lambda n: jnp.zeros((n,), F32)
    fr = jnp.concatenate([zero(NOPE), fr16, fr16, zero(HP - MQK)]).reshape(1, HP)
    sg = jnp.concatenate([zero(NOPE), -jnp.ones((ROPE // 2,), F32), jnp.ones((ROPE // 2,), F32), zero(HP - MQK)]).reshape(1, HP)

    w_in_t = _perm_w_in_t(wt["w_in"])
    wa_pad = _pad_wa(wt["gla_w_alpha"].T)
    wuq = _pad_wuq(wt["mla_w_uq"].T)
    wukv = _pad_wukv(wt["mla_w_ukv"].T)
    wo_pad = _pad_wo(wt["mla_w_o"])
    qn_g, kn_g = _pad_lanes(sp["mla_qn_g"]), _pad_lanes(sp["mla_kn_g"])

    _, h = _norm_mod_fwd(x2, sp["norm1_g"], mod3, 0, 1, "norm1_fwd")
    proj = _mm(h, w_in_t, "nt", (F32,), "proj_fwd")
    o_gla, og, states = _gla_fwd(proj, wa_pad, sp["gla_b_alpha"], sp["gla_out_norm_g"], bsz)
    y_a = _mm(og, wt["gla_w_o"], "nn", (F32,), "gla_out_fwd")
    q2, k2, v2 = _mla_pre_fwd(proj, pos, fr, sg, sp["mla_q_lat_g"], sp["mla_kv_lat_g"], qn_g, kn_g, wuq, wukv)
    o2, lse2 = _attn_fwd(q2, k2, v2, bsz)
    y_b = _mm(o2, wo_pad, "nn", (F32,), "mla_out_fwd")
    mix = _merge_fwd(proj, sp["b_merge"], y_a, y_b)
    mixed = _mm(mix, wt["w_out"], "nn", (F32,), "w_out_fwd")

    x1, h2 = _norm_mod_fwd(x2, sp["norm2_g"], mod3, 3, 4, "norm2_fwd", mixed=mixed, i_gate=2)
    a, f = _mm(h2, wt["mlp_w1"], "nt", (F32, BF16), "mlp1_fwd",
               epi=lambda acc: (acc, jnp.square(jnp.maximum(acc, 0.0))))
    ff = _mm(f, wt["mlp_w2"], "nn", (F32,), "mlp2_fwd")
    dy, dff, acc_g2, acc_loss = _loss_head(x1, ff, tgt2, mod3)

    gw = {}
    gw["mlp_w2"] = _mm(f, dff, "tn", (BF16,), "mlp2_dw")
    da = _mm(dff, wt["mlp_w2"], "nt", (BF16,), "mlp2_dx", extras=(a,),
             epi=lambda acc, av: (acc * (2.0 * jnp.maximum(av, 0.0)),))
    gw["mlp_w1"] = _mm(da, h2, "tn", (BF16,), "mlp1_dw")
    dh2 = _mm(da, wt["mlp_w1"], "nn", (F32,), "mlp1_dx")
    dx1, dmixed, accb2, accg2 = _norm_mod_bwd(x1, dh2, dy, sp["norm2_g"], mod3, 3, 4, "norm2_bwd", mixed=mixed, i_gate=2)

    gw["w_out"] = _mm(mix, dmixed, "tn", (BF16,), "w_out_dw")
    dmix = _mm(dmixed, wt["w_out"], "nt", (F32,), "w_out_dx")
    dy_a, dy_b, dlogits, acc_bm = _merge_bwd(dmix, proj, sp["b_merge"], y_a, y_b)
    gw["gla_w_o"] = _mm(og, dy_a, "tn", (BF16,), "gla_out_dw")
    dog = _mm(dy_a, wt["gla_w_o"], "nt", (F32,), "gla_out_dx")
    dq_g, dk_g, dv_g, dgg, dz, acc_ba, acc_go = _gla_bwd(dog, o_gla, proj, wa_pad, sp["gla_b_alpha"],
                                                         sp["gla_out_norm_g"], states, bsz)
    gw["gla_w_alpha"] = _mm(proj, dz, "tn", (F32,), "gla_alpha_dw", a_off=43, m=LANES)[:GLR].T.astype(BF16)
    dmisc_gla = _mm(dz, wa_pad, "nt", (F32,), "gla_alpha_dx")
    gw["mla_w_o"] = _unpad_wo(_mm(o2, dy_b, "tn", (BF16,), "mla_out_dw"))
    do2 = _mm(dy_b, wo_pad, "nt", (BF16,), "mla_out_dx")
    dq2, dk2, dv2 = _attn_bwd(q2, k2, v2, do2, o2, lse2, bsz)
    dcq, dckv, dmisc, gwuq, gwukv, acc_mla = _mla_pre_bwd(dq2, dk2, dv2, dmisc_gla, proj, pos, fr, sg, sp["mla_q_lat_g"],
                                                         sp["mla_kv_lat_g"], qn_g, kn_g, wuq, wukv)
    gw["mla_w_uq"] = _unpad_wuq(gwuq).T.astype(BF16)
    gw["mla_w_ukv"] = _unpad_wukv(gwukv).T.astype(BF16)
    dproj = jnp.concatenate([dq_g, dk_g, dv_g, dgg, dlogits, dcq, dckv, dmisc], axis=1)
    gw["w_in"] = _unperm_w_in_t(_mm(dproj, h, "tn", (BF16,), "proj_dw"))
    dh = _mm(dproj, w_in_t, "nn", (F32,), "proj_dx")
    grad_x, accb1, accg1 = _norm_mod_bwd(x2, dh, dx1, sp["norm1_g"], mod3, 0, 1, "norm1_bwd")

    dmod = jnp.stack([accb1[:, 0], accb1[:, 1], accb2[:, 2], accb2[:, 0], accb2[:, 1], acc_g2[:, 0]], axis=1)

    rows = {
        "dmod": dmod.reshape(bsz * 6, D),
        "norm1_g": accg1[0:1], "norm2_g": accg2[0:1],
        "b_merge": acc_bm[0:1].reshape(2, D),
        "gla_b_alpha": _pad_lanes(acc_ba[0:1], D),
        "gla_out_norm_g": _pad_lanes(acc_go.reshape(GH, 8, GDV)[:, 0, :], D),
        "mla_q_lat_g": _pad_lanes(acc_mla[0:1], D), "mla_kv_lat_g": _pad_lanes(acc_mla[1:2], D),
        "mla_qn_g": _pad_lanes(acc_mla[2:3], D), "mla_kn_g": _pad_lanes(acc_mla[3:4], D),
        "loss": acc_loss[0:1],
    }
    return grad_x.reshape(bsz, s, D), gw, rows


HBM_SPEC = pl.BlockSpec(memory_space=pltpu.HBM)


def _all_gather(p, name):
    r, cdim = p.shape

    def body(p_ref, out_ref, send_sems, recv_sems, local_sem):
        x, y, c = lax.axis_index("x"), lax.axis_index("y"), lax.axis_index("c")
        me, sibling = (x, y, c), (x, y, 1 - c)
        chips = [(1 - x, y), (x, 1 - y), (1 - x, 1 - y)]

        def slot(px, py, pc):
            return out_ref.at[4 * px + 2 * py + pc]

        def copy(k, block, to, src=None):
            return pltpu.make_async_remote_copy(
                src_ref=slot(*block) if src is None else src, dst_ref=slot(*block),
                send_sem=send_sems.at[k], recv_sem=recv_sems.at[k], device_id=to, device_id_type=MESH)

        mine = pltpu.make_async_copy(p_ref, slot(*me), local_sem)
        mine.start()
        first = [copy(0, me, sibling, src=p_ref)] + [copy(1 + j, me, (*chip, c), src=p_ref) for j, chip in enumerate(chips)]
        for cp in first:
            cp.start()
        passed = [copy(4 + j, (*chip, c), sibling) for j, chip in enumerate(chips)]
        for j, chip in enumerate(chips):
            copy(1 + j, (*chip, c), me).wait_recv()
            passed[j].start()
        copy(0, sibling, me).wait_recv()
        for j, chip in enumerate(chips):
            copy(4 + j, (*chip, 1 - c), me).wait_recv()
        for cp in first + passed:
            cp.wait_send()
        mine.wait()

    return pl.pallas_call(
        body, name=name, out_shape=jax.ShapeDtypeStruct((N_DEV, r, cdim), p.dtype),
        in_specs=[HBM_SPEC], out_specs=HBM_SPEC,
        scratch_shapes=[pltpu.SemaphoreType.DMA((7,)), pltpu.SemaphoreType.DMA((7,)), pltpu.SemaphoreType.DMA(())],
    )(p)


def _sibling_exchange(g):
    def body(g_ref, out_ref, send_sem, recv_sem):
        x, y, c = lax.axis_index("x"), lax.axis_index("y"), lax.axis_index("c")
        cp = pltpu.make_async_remote_copy(src_ref=g_ref, dst_ref=out_ref, send_sem=send_sem, recv_sem=recv_sem,
                                          device_id=(x, y, 1 - c), device_id_type=MESH)
        cp.start()
        cp.wait()

    return pl.pallas_call(
        body, name="rs_sibling_exchange", out_shape=jax.ShapeDtypeStruct(g.shape, g.dtype),
        in_specs=[HBM_SPEC], out_specs=HBM_SPEC,
        scratch_shapes=[pltpu.SemaphoreType.DMA(()), pltpu.SemaphoreType.DMA(())],
    )(g)


def _chip_exchange(tsum):
    def body(t_ref, out_ref, send_sems, recv_sems, local_sem):
        x, y, c = lax.axis_index("x"), lax.axis_index("y"), lax.axis_index("c")
        my_chip = 2 * x + y
        chips = [(1 - x, y), (x, 1 - y), (1 - x, 1 - y)]
        mine = pltpu.make_async_copy(t_ref.at[my_chip], out_ref.at[my_chip], local_sem)
        mine.start()
        sends = [pltpu.make_async_remote_copy(
            src_ref=t_ref.at[2 * px + py], dst_ref=out_ref.at[my_chip], send_sem=send_sems.at[j], recv_sem=recv_sems.at[j],
            device_id=(px, py, c), device_id_type=MESH) for j, (px, py) in enumerate(chips)]
        for cp in sends:
            cp.start()
        for j, (px, py) in enumerate(chips):
            pltpu.make_async_remote_copy(
                src_ref=t_ref.at[my_chip], dst_ref=out_ref.at[2 * px + py], send_sem=send_sems.at[j],
                recv_sem=recv_sems.at[j], device_id=(px, py, c), device_id_type=MESH).wait_recv()
        for cp in sends:
            cp.wait_send()
        mine.wait()

    return pl.pallas_call(
        body, name="rs_chip_exchange", out_shape=jax.ShapeDtypeStruct(tsum.shape, tsum.dtype),
        in_specs=[HBM_SPEC], out_specs=HBM_SPEC,
        scratch_shapes=[pltpu.SemaphoreType.DMA((3,)), pltpu.SemaphoreType.DMA((3,)), pltpu.SemaphoreType.DMA(())],
    )(tsum)


def _pair_sum(a, b):
    n, r, cdim = a.shape
    blk = pl.BlockSpec((1, PACK_BLOCK, cdim), lambda j, i: (j, i, 0))

    def body(a_ref, b_ref, o_ref):
        o_ref[...] = (a_ref[...].astype(F32) + b_ref[...].astype(F32)).astype(BF16)

    return pl.pallas_call(
        body, name="rs_pair_sum", grid=(n, r // PACK_BLOCK), in_specs=[blk, blk], out_specs=blk,
        out_shape=jax.ShapeDtypeStruct(a.shape, BF16), compiler_params=_cparams(("arbitrary", "arbitrary")),
    )(a, b)


def _adamw_math(w, g, m, v):
    m = ADAM_B1 * m + (1.0 - ADAM_B1) * g
    v = ADAM_B2 * v + (1.0 - ADAM_B2) * jnp.square(g)
    m_hat = m / (1.0 - ADAM_B1 ** ADAM_STEP)
    v_hat = v / (1.0 - ADAM_B2 ** ADAM_STEP)
    delta = -ADAM_LR * (m_hat / (jnp.sqrt(v_hat) + ADAM_EPS) + ADAM_WD * w)
    return delta, m, v


def _slab_sum(parts):
    _, r, cdim = parts.shape
    blk = pl.BlockSpec((PACK_BLOCK, cdim), lambda i: (i, 0))

    def body(p_ref, g_out):
        g = p_ref[0].astype(F32)
        for j in range(1, 4):
            g = g + p_ref[j].astype(F32)
        g_out[...] = g

    return pl.pallas_call(
        body, name="rs_slab_sum", grid=(r // PACK_BLOCK,),
        in_specs=[pl.BlockSpec((4, PACK_BLOCK, cdim), lambda i: (0, i, 0))], out_specs=blk,
        out_shape=jax.ShapeDtypeStruct((r, cdim), F32), compiler_params=_cparams(("arbitrary",)),
    )(parts)


def _adamw(g, w, m, v, name):
    r, cdim = w.shape
    rb = _tile(r, 256)
    blk = pl.BlockSpec((rb, cdim), lambda i: (i, 0))

    def body(g_ref, w_ref, m_ref, v_ref, d_out, m_out, v_out):
        d_out[...], m_out[...], v_out[...] = _adamw_math(w_ref[...], g_ref[...], m_ref[...], v_ref[...])

    return pl.pallas_call(
        body, name=name, grid=(r // rb,), in_specs=[blk] * 4, out_specs=[blk] * 3,
        out_shape=[jax.ShapeDtypeStruct((r, cdim), F32)] * 3, compiler_params=_cparams(("arbitrary",)),
    )(g, w, m, v)


def _adamw_small(parts, w, m, v):
    def body(p_ref, w_ref, m_ref, v_ref, g_out, d_out, m_out, v_out, loss_out):
        def total(srcs):
            acc = None
            for r in srcs:
                for j in range(N_DEV):
                    term = p_ref[j, r:r + 1, :]
                    acc = term if acc is None else acc + term
            return acc

        for prow, srcs in enumerate(SMALL_SOURCES):
            one = slice(prow, prow + 1)
            g = total(srcs)
            g_out[one, :] = g
            d_out[one, :], m_out[one, :], v_out[one, :] = _adamw_math(w_ref[one, :], g, m_ref[one, :], v_ref[one, :])
        loss_out[...] = jnp.broadcast_to(jnp.sum(total(LOSS_SOURCE), axis=1, keepdims=True), (8, LANES))

    full = lambda shp: pl.BlockSpec(shp, lambda i: (0,) * len(shp))
    return pl.pallas_call(
        body, name="adamw_small", grid=(1,),
        in_specs=[full((N_DEV, SMALL_ROWS, D)), full((16, D)), full((16, D)), full((16, D))],
        out_specs=[full((16, D))] * 4 + [full((8, LANES))],
        out_shape=[jax.ShapeDtypeStruct((16, D), F32)] * 4 + [jax.ShapeDtypeStruct((8, LANES), F32)],
        compiler_params=_cparams(("arbitrary",)),
    )(parts, w, m, v)


def _to_slab(shard, form):
    if form == "N":
        return shard
    return shard.T if form == "T" else shard.T.reshape(-1, D)


def _from_slab(block, form, shard_shape):
    if form == "N":
        return block
    return block.T if form == "T" else block.reshape(shard_shape[1], shard_shape[0]).T


def _gathered_full(g, form, shard_shape):
    if form == "TR":
        return g.reshape(N_DEV * shard_shape[1], shard_shape[0])
    return g.reshape(N_DEV * g.shape[1], D)


def _small_pack(vals):
    rows = []
    for n in SMALL:
        v = vals[n].reshape(-1)
        k = -(-v.shape[0] // D)
        rows.append(jnp.pad(v, (0, k * D - v.shape[0])).reshape(k, D))
    return jnp.concatenate(rows, axis=0)


def _small_unpack(packed, shapes):
    out = {}
    for n in SMALL:
        k = shapes[n][-1]
        r0 = SMALL_ROW[n]
        out[n] = packed[r0:r0 + -(-k // D)].reshape(-1)[:k].reshape(shapes[n])
    return out


def kernel(x, c, positions, w_ada, b_ada, norm1_g, w_in, b_merge, gla_w_alpha, gla_b_alpha, gla_out_norm_g, gla_w_o, mla_q_lat_g, mla_w_uq, mla_kv_lat_g, mla_w_ukv, mla_qn_g, mla_kn_g, mla_w_o, w_out, norm2_g, mlp_w1, mlp_w2, loss_target, m_w_ada, m_b_ada, m_norm1_g, m_w_in, m_b_merge, m_gla_w_alpha, m_gla_b_alpha, m_gla_out_norm_g, m_gla_w_o, m_mla_q_lat_g, m_mla_w_uq, m_mla_kv_lat_g, m_mla_w_ukv, m_mla_qn_g, m_mla_kn_g, m_mla_w_o, m_w_out, m_norm2_g, m_mlp_w1, m_mlp_w2, v_w_ada, v_b_ada, v_norm1_g, v_w_in, v_b_merge, v_gla_w_alpha, v_gla_b_alpha, v_gla_out_norm_g, v_gla_w_o, v_mla_q_lat_g, v_mla_w_uq, v_mla_kv_lat_g, v_mla_w_ukv, v_mla_qn_g, v_mla_kn_g, v_mla_w_o, v_w_out, v_norm2_g, v_mlp_w1, v_mlp_w2):
    args = dict(locals())
    wts = {n: args[n][0] for n in WEIGHTS}
    mom = {n: args["m_" + n][0] for n in WEIGHTS}
    var = {n: args["v_" + n][0] for n in WEIGHTS}
    my_c = lax.axis_index("c")
    my_dev = 4 * lax.axis_index("x") + 2 * lax.axis_index("y") + my_c
    bsz = x.shape[0]
    sp = {n: wts[n].reshape(1, -1) for n in SMALL}

    slab = jnp.concatenate([jnp.zeros((r, D), BF16) if n is None else _to_slab(wts[n], form).astype(BF16)
                            for n, r, form in SLAB], axis=0)
    gathered = _all_gather(slab, "weights_all_gather")
    wt, off = {}, 0
    for n, r, form in SLAB:
        if n is not None:
            wt[n] = _gathered_full(gathered[:, off:off + r], form, wts[n].shape)
        off += r

    c_all = _all_gather(jnp.pad(c, ((0, 8 - bsz), (0, 0))), "cond_all_gather")[:, :bsz].reshape(N_DEV * bsz, D)
    bias = lax.dynamic_slice_in_dim(sp["b_ada"], my_dev * ADA_COLS, ADA_COLS, axis=1)
    mod_cols = _mm(c_all, wts["w_ada"], "nn", (F32,), "ada_fwd", pro=_silu, epi=lambda acc, b: (acc + b,),
                   extras=(jnp.broadcast_to(bias, (N_DEV * bsz, ADA_COLS)),))
    mod_all = _all_gather(mod_cols, "mod_all_gather")
    mod_mine = lax.dynamic_slice_in_dim(mod_all, my_dev * bsz, bsz, axis=1)
    mod3 = jnp.transpose(mod_mine, (1, 0, 2)).reshape(bsz, 6, D)

    grad_x, gw, rows = _local_step(x, positions, loss_target, wt, sp, mod3)

    per_dev = jnp.concatenate([jnp.zeros((N_DEV, r, D), BF16) if n is None else gw[n].reshape(N_DEV, r, D)
                               for n, r, _ in SLAB], axis=1).reshape(4, 2, PACK_ROWS, D)
    keep = lax.dynamic_index_in_dim(per_dev, my_c, axis=1, keepdims=False)
    give = lax.dynamic_index_in_dim(per_dev, 1 - my_c, axis=1, keepdims=False)
    chip_sums = _pair_sum(keep, _sibling_exchange(give))
    gslab = _slab_sum(_chip_exchange(chip_sums))
    big, off = {}, 0
    for n, r, form in SLAB:
        if n is not None:
            g = _from_slab(gslab[off:off + r], form, wts[n].shape)
            big[n] = (g,) + tuple(_adamw(g, wts[n], mom[n], var[n], "adamw_" + n))
        off += r

    order = ["dmod", "norm1_g", "norm2_g", "b_merge", "gla_b_alpha", "gla_out_norm_g", "mla_q_lat_g", "mla_kv_lat_g",
             "mla_qn_g", "mla_kn_g", "loss"]
    part_rows = jnp.concatenate([rows[n] for n in order], axis=0)
    part_rows = jnp.pad(part_rows, ((0, SMALL_ROWS - part_rows.shape[0]), (0, 0)))
    all_rows = _all_gather(part_rows, "partials_all_gather")

    dmod_all = all_rows[:, :6 * bsz].reshape(N_DEV * bsz, 6 * D)
    dmod_cols = lax.dynamic_slice_in_dim(dmod_all, my_dev * ADA_COLS, ADA_COLS, axis=1)
    g_ada = _mm(c_all, dmod_cols, "tn", (F32,), "ada_dw", pro=_silu)
    big["w_ada"] = (g_ada,) + tuple(_adamw(g_ada, wts["w_ada"], mom["w_ada"], var["w_ada"], "adamw_w_ada"))

    small = _adamw_small(all_rows, _small_pack({n: wts[n] for n in SMALL}), _small_pack({n: mom[n] for n in SMALL}),
                         _small_pack({n: var[n] for n in SMALL}))
    loss = small[4][0, 0]
    small_shapes = {n: wts[n].shape for n in SMALL}
    small = [_small_unpack(o, small_shapes) for o in small[:4]]

    outs = [loss, grad_x]
    for k in range(4):
        for n in WEIGHTS:
            val = big[n][k] if n in BIG else small[k][n]
            outs.append(val.reshape((1,) + tuple(wts[n].shape)))
    return tuple(outs)
```

```python
import functools

import jax
import jax.numpy as jnp
from jax import lax
from jax.experimental import pallas as pl
from jax.experimental.pallas import tpu as pltpu

F32 = jnp.float32
BF16 = jnp.bfloat16
MESH = pl.DeviceIdType.MESH

D = 1024
EPS = 1e-6
CHUNK = 64
GH, GDK, GDV, GLR, GTAU = 4, 128, 256, 16, 16.0
MH, MQR, MKVR, NOPE, ROPE, MV = 16, 256, 128, 64, 32, 64
MQK = NOPE + ROPE
HP = 128
FF = 4 * D
ROPE_THETA = 10000.0
IN_WIDTH = 5552
PW = 5632
N_DEV = 8
LANES = 128
PACK_ROWS = 2240
PACK_BLOCK = 320
ADA_COLS = 6 * D // N_DEV
SMALL_ROWS = 32
SMALL_SOURCES = tuple([(r, 6 + r) for r in range(6)] + [(12,), (13,), (14,), (15,), (16,), (17, 18, 19, 20),
                                                         (21,), (22,), (23,), (24,)])
LOSS_SOURCE = (25,)
VMEM_LIMIT = 56 * 1024 * 1024

ADAM_LR, ADAM_B1, ADAM_B2, ADAM_EPS, ADAM_WD, ADAM_STEP = 0.001, 0.9, 0.999, 1e-08, 0.01, 10

SLAB = (("w_in", 694, "T"), ("gla_w_alpha", 1, "TR"), (None, 9, None), ("mlp_w1", 512, "T"), ("gla_w_o", 128, "N"),
        ("mla_w_o", 128, "N"), ("w_out", 128, "N"), ("mlp_w2", 512, "N"), ("mla_w_uq", 48, "TR"), ("mla_w_ukv", 32, "TR"),
        (None, 48, None))
assert sum(r for _, r, _ in SLAB) == PACK_ROWS
BIG = ("w_ada",) + tuple(n for n, _, _ in SLAB if n is not None)
SMALL = ("b_ada", "norm1_g", "norm2_g", "b_merge", "gla_b_alpha", "gla_out_norm_g", "mla_q_lat_g", "mla_kv_lat_g",
         "mla_qn_g", "mla_kn_g")
SMALL_ROW = {"b_ada": 0, "norm1_g": 6, "norm2_g": 7, "b_merge": 8, "gla_b_alpha": 10, "gla_out_norm_g": 11,
             "mla_q_lat_g": 12, "mla_kv_lat_g": 13, "mla_qn_g": 14, "mla_kn_g": 15}
WEIGHTS = ("w_ada", "b_ada", "norm1_g", "w_in", "b_merge", "gla_w_alpha", "gla_b_alpha", "gla_out_norm_g", "gla_w_o",
           "mla_q_lat_g", "mla_w_uq", "mla_kv_lat_g", "mla_w_ukv", "mla_qn_g", "mla_kn_g", "mla_w_o", "w_out",
           "norm2_g", "mlp_w1", "mlp_w2")


def _cparams(sem=None):
    return pltpu.CompilerParams(dimension_semantics=sem, vmem_limit_bytes=VMEM_LIMIT)


def _tile(n, pref):
    for t in (2048, 1024, 512, 256, 128):
        if t <= pref and n % t == 0:
            return t
    return n


def _dot(a, b, dims, precision=None):
    return lax.dot_general(a, b, (dims, ((), ())), preferred_element_type=F32, precision=precision)


NN = ((1,), (0,))
NT = ((1,), (1,))
TN = ((0,), (0,))


def _sigmoid(x):
    return 1.0 / (1.0 + jnp.exp(-x))


def _silu(x):
    return x * _sigmoid(x)


def _mm(a, b, mode, out_dtypes, name, *, pro=None, pro_b=None, epi=None, extras=(), a_off=0, m=None, tm=1024, tn=512,
        tk=1024):
    if mode == "tn":
        kc, n = b.shape
        m = a.shape[1] if m is None else m
    elif mode == "nn":
        m, kc = a.shape
        n = b.shape[1]
    else:
        m, kc = a.shape
        n = b.shape[0]
    tm, tn, tk = _tile(m, tm), _tile(n, tn), _tile(kc, tk)
    nk = kc // tk
    dims = {"nn": NN, "nt": NT, "tn": TN}[mode]
    if mode == "tn":
        a_spec = pl.BlockSpec((tk, tm), lambda i, j, k: (k, i + a_off))
    else:
        a_spec = pl.BlockSpec((tm, tk), lambda i, j, k: (i + a_off, k))
    if mode == "nt":
        b_spec = pl.BlockSpec((tn, tk), lambda i, j, k: (j, k))
    else:
        b_spec = pl.BlockSpec((tk, tn), lambda i, j, k: (k, j))
    o_spec = pl.BlockSpec((tm, tn), lambda i, j, k: (i, j))
    n_ex, n_out = len(extras), len(out_dtypes)

    def body(a_ref, b_ref, *rest):
        ex, outs, acc = rest[:n_ex], rest[n_ex:n_ex + n_out], rest[-1]
        k = pl.program_id(2)

        @pl.when(k == 0)
        def _():
            acc[...] = jnp.zeros_like(acc)

        av = a_ref[...]
        if pro is not None:
            av = pro(av)
        bv = b_ref[...]
        if pro_b is not None:
            bv = pro_b(bv)
        acc[...] += _dot(av.astype(BF16), bv.astype(BF16), dims)

        @pl.when(k == nk - 1)
        def _():
            res = (acc[...],) if epi is None else epi(acc[...], *[e[...] for e in ex])
            for o_ref, r in zip(outs, res):
                o_ref[...] = r.astype(o_ref.dtype)

    out = pl.pallas_call(
        body, name=name, grid=(m // tm, n // tn, nk),
        in_specs=[a_spec, b_spec] + [o_spec] * n_ex,
        out_specs=[o_spec] * n_out,
        out_shape=[jax.ShapeDtypeStruct((m, n), dt) for dt in out_dtypes],
        scratch_shapes=[pltpu.VMEM((tm, tn), F32)],
        compiler_params=_cparams(("parallel", "parallel", "arbitrary")),
    )(a, b, *extras)
    return out[0] if n_out == 1 else out


def _rows(s):
    return _tile(s, 512)


def _mod_spec():
    return pl.BlockSpec((1, 6, D), lambda b, i: (b, 0, 0))


def _tok_spec(tr, nb, width=D, col=0):
    return pl.BlockSpec((tr, width), lambda b, i: (b * nb + i, col))


def _norm_mod_fwd(x, g, mod3, i_shift, i_scale, name, mixed=None, i_gate=None):
    bsz, _, _ = mod3.shape
    t = x.shape[0]
    s = t // bsz
    tr = _rows(s)
    nb = s // tr
    has_res = mixed is not None

    def body(*refs):
        if has_res:
            x_ref, mx_ref, g_ref, mod_ref, x1_ref, h_ref = refs
            xv = x_ref[...] + mod_ref[0, i_gate:i_gate + 1, :] * mx_ref[...]
            x1_ref[...] = xv
        else:
            x_ref, g_ref, mod_ref, h_ref = refs
            xv = x_ref[...]
        r = lax.rsqrt(jnp.mean(xv * xv, axis=1, keepdims=True) + EPS)
        hn = (xv * r) * g_ref[...]
        h = hn * (1.0 + mod_ref[0, i_scale:i_scale + 1, :]) + mod_ref[0, i_shift:i_shift + 1, :]
        h_ref[...] = h.astype(BF16)

    tok = _tok_spec(tr, nb)
    gspec = pl.BlockSpec((1, D), lambda b, i: (0, 0))
    ins = [x] + ([mixed] if has_res else []) + [g, mod3]
    in_specs = [tok] + ([tok] if has_res else []) + [gspec, _mod_spec()]
    out_shape = ([jax.ShapeDtypeStruct((t, D), F32)] if has_res else []) + [jax.ShapeDtypeStruct((t, D), BF16)]
    out = pl.pallas_call(
        body, name=name, grid=(bsz, nb), in_specs=in_specs, out_specs=[tok] * len(out_shape), out_shape=out_shape,
        compiler_params=_cparams(("arbitrary", "arbitrary")),
    )(*ins)
    return (out[0], out[1]) if has_res else (None, out[0])


def _norm_mod_bwd(x, dh, dres, g, mod3, i_shift, i_scale, name, mixed=None, i_gate=None):
    bsz = mod3.shape[0]
    t = x.shape[0]
    s = t // bsz
    tr = _rows(s)
    nb = s // tr
    has_res = mixed is not None

    def body(*refs):
        if has_res:
            x_ref, dh_ref, dres_ref, mx_ref, g_ref, mod_ref, dx_ref, dmx_ref, accb, accg = refs
        else:
            x_ref, dh_ref, dres_ref, g_ref, mod_ref, dx_ref, accb, accg = refs
        b, i = pl.program_id(0), pl.program_id(1)

        @pl.when(i == 0)
        def _():
            accb[...] = jnp.zeros_like(accb)

        @pl.when((i == 0) & (b == 0))
        def _():
            accg[...] = jnp.zeros_like(accg)

        xv, dhv, gv = x_ref[...], dh_ref[...], g_ref[...]
        r = lax.rsqrt(jnp.mean(xv * xv, axis=1, keepdims=True) + EPS)
        xn = xv * r
        accb[0, 0:1, :] += jnp.sum(dhv, axis=0, keepdims=True)
        accb[0, 1:2, :] += jnp.sum(dhv * (xn * gv), axis=0, keepdims=True)
        tt = dhv * (1.0 + mod_ref[0, i_scale:i_scale + 1, :])
        accg[0:1, :] += jnp.sum(tt * xn, axis=0, keepdims=True)
        dxn = tt * gv
        dx = dres_ref[...] + r * (dxn - xn * jnp.mean(dxn * xn, axis=1, keepdims=True))
        dx_ref[...] = dx
        if has_res:
            accb[0, 2:3, :] += jnp.sum(dx * mx_ref[...], axis=0, keepdims=True)
            dmx_ref[...] = (dx * mod_ref[0, i_gate:i_gate + 1, :]).astype(BF16)

    tok = _tok_spec(tr, nb)
    gspec = pl.BlockSpec((1, D), lambda b, i: (0, 0))
    ins = [x, dh, dres] + ([mixed] if has_res else []) + [g, mod3]
    in_specs = [tok] * (4 if has_res else 3) + [gspec, _mod_spec()]
    out_shape = [jax.ShapeDtypeStruct((t, D), F32)] + ([jax.ShapeDtypeStruct((t, D), BF16)] if has_res else [])
    out_specs = [tok] * len(out_shape)
    out_shape += [jax.ShapeDtypeStruct((bsz, 8, D), F32), jax.ShapeDtypeStruct((8, D), F32)]
    out_specs += [pl.BlockSpec((1, 8, D), lambda b, i: (b, 0, 0)), pl.BlockSpec((8, D), lambda b, i: (0, 0))]
    return pl.pallas_call(
        body, name=name, grid=(bsz, nb), in_specs=in_specs, out_specs=out_specs, out_shape=out_shape,
        compiler_params=_cparams(("arbitrary", "arbitrary")),
    )(*ins)


def _loss_head(x1, ff, tgt, mod3):
    bsz = mod3.shape[0]
    t = x1.shape[0]
    s = t // bsz
    tr = _rows(s)
    nb = s // tr

    def body(x1_ref, ff_ref, tg_ref, mod_ref, dy_ref, dff_ref, accb, accl):
        b, i = pl.program_id(0), pl.program_id(1)

        @pl.when(i == 0)
        def _():
            accb[...] = jnp.zeros_like(accb)

        @pl.when((i == 0) & (b == 0))
        def _():
            accl[...] = jnp.zeros_like(accl)

        gate = mod_ref[0, 5:6, :]
        ffv = ff_ref[...]
        err = x1_ref[...] + gate * ffv - tg_ref[...]
        accl[0:1, :] += jnp.sum(err * err, axis=0, keepdims=True) * (0.5 / D)
        dy = err * (1.0 / D)
        dy_ref[...] = dy
        dff_ref[...] = (dy * gate).astype(BF16)
        accb[0, 0:1, :] += jnp.sum(dy * ffv, axis=0, keepdims=True)

    tok = _tok_spec(tr, nb)
    return pl.pallas_call(
        body, name="loss_head", grid=(bsz, nb), in_specs=[tok, tok, tok, _mod_spec()],
        out_specs=[tok, tok, pl.BlockSpec((1, 8, D), lambda b, i: (b, 0, 0)), pl.BlockSpec((8, D), lambda b, i: (0, 0))],
        out_shape=[jax.ShapeDtypeStruct((t, D), F32), jax.ShapeDtypeStruct((t, D), BF16),
                   jax.ShapeDtypeStruct((bsz, 8, D), F32), jax.ShapeDtypeStruct((8, D), F32)],
        compiler_params=_cparams(("arbitrary", "arbitrary")),
    )(x1, ff, tgt, mod3)


def _merge_fwd(proj, b_merge, y_a, y_b):
    t = proj.shape[0]
    tr = _tile(t, 512)

    def body(la_ref, lb_ref, bm_ref, ya_ref, yb_ref, mix_ref):
        ga = _sigmoid(la_ref[...] + bm_ref[:, 0:D])
        gb = _sigmoid(lb_ref[...] + bm_ref[:, D:2 * D])
        mix_ref[...] = (ga * ya_ref[...].astype(F32) + gb * yb_ref[...].astype(F32)).astype(BF16)

    tok = pl.BlockSpec((tr, D), lambda i: (i, 0))
    return pl.pallas_call(
        body, name="merge_fwd", grid=(t // tr,),
        in_specs=[pl.BlockSpec((tr, D), lambda i: (i, 3)), pl.BlockSpec((tr, D), lambda i: (i, 4)),
                  pl.BlockSpec((1, 2 * D), lambda i: (0, 0)), tok, tok],
        out_specs=tok, out_shape=jax.ShapeDtypeStruct((t, D), BF16),
        compiler_params=_cparams(("arbitrary",)),
    )(proj, proj, b_merge, y_a, y_b)


def _merge_bwd(dmix, proj, b_merge, y_a, y_b):
    t = proj.shape[0]
    tr = _tile(t, 512)

    def body(dm_ref, la_ref, lb_ref, bm_ref, ya_ref, yb_ref, dya_ref, dyb_ref, dl_ref, acc):
        @pl.when(pl.program_id(0) == 0)
        def _():
            acc[...] = jnp.zeros_like(acc)

        dm = dm_ref[...].astype(F32)
        ga = _sigmoid(la_ref[...] + bm_ref[:, 0:D])
        gb = _sigmoid(lb_ref[...] + bm_ref[:, D:2 * D])
        dya_ref[...] = (dm * ga).astype(BF16)
        dyb_ref[...] = (dm * gb).astype(BF16)
        dla = dm * ya_ref[...].astype(F32) * ga * (1.0 - ga)
        dlb = dm * yb_ref[...].astype(F32) * gb * (1.0 - gb)
        dl_ref[:, 0:D] = dla.astype(BF16)
        dl_ref[:, D:2 * D] = dlb.astype(BF16)
        acc[0:1, 0:D] += jnp.sum(dla, axis=0, keepdims=True)
        acc[0:1, D:2 * D] += jnp.sum(dlb, axis=0, keepdims=True)

    tok = pl.BlockSpec((tr, D), lambda i: (i, 0))
    return pl.pallas_call(
        body, name="merge_bwd", grid=(t // tr,),
        in_specs=[tok, pl.BlockSpec((tr, D), lambda i: (i, 3)), pl.BlockSpec((tr, D), lambda i: (i, 4)),
                  pl.BlockSpec((1, 2 * D), lambda i: (0, 0)), tok, tok],
        out_specs=[tok, tok, pl.BlockSpec((tr, 2 * D), lambda i: (i, 0)), pl.BlockSpec((8, 2 * D), lambda i: (0, 0))],
        out_shape=[jax.ShapeDtypeStruct((t, D), BF16), jax.ShapeDtypeStruct((t, D), BF16),
                   jax.ShapeDtypeStruct((t, 2 * D), BF16), jax.ShapeDtypeStruct((8, 2 * D), F32)],
        compiler_params=_cparams(("arbitrary",)),
    )(dmix, proj, proj, b_merge, y_a, y_b)


def _log_sigmoid(z):
    return jnp.minimum(z, 0.0) - jnp.log(1.0 + jnp.exp(-jnp.abs(z)))


def _tri(lower):
    r = lax.broadcasted_iota(jnp.int32, (CHUNK, CHUNK), 0)
    c = lax.broadcasted_iota(jnp.int32, (CHUNK, CHUNK), 1)
    return jnp.where(r >= c if lower else r <= c, 1.0, 0.0).astype(F32)


def _gla_fwd(proj, wa_pad, b_alpha, g_out, bsz):
    t = proj.shape[0]
    s = t // bsz
    nc = s // CHUNK

    def body(q_ref, k_ref, v_ref, gg_ref, ms_ref, wa_ref, ba_ref, go_ref, o_ref, og_ref, st_ref, la, state):
        z = _dot(ms_ref[...].astype(BF16), wa_ref[...], NN) + ba_ref[...]
        la[...] = _log_sigmoid(z) * (1.0 / GTAU)
        state[...] = jnp.zeros_like(state)
        low = _tri(True)
        gout = go_ref[...]

        def chunk(n, carry):
            rows = pl.ds(pl.multiple_of(n * CHUNK, CHUNK), CHUNK)
            lac = la[rows, :]
            cum = _dot(low, lac, NN, lax.Precision.HIGHEST)
            ce = jnp.sum(lac, axis=0, keepdims=True)
            kd = (k_ref[rows, :].astype(F32) * jnp.exp(ce - cum)).astype(BF16)
            new = state[...] * jnp.exp(ce) + _dot(v_ref[rows, :].astype(BF16), kd, TN)
            state[...] = new
            st_ref[pl.ds(pl.multiple_of(n * GDV, GDV), GDV), :] = new
            qs = (q_ref[rows, :].astype(F32) * (GDK ** -0.5)).astype(BF16)
            o = _dot(qs, new.astype(BF16), NT)
            o_ref[rows, :] = o
            ro = lax.rsqrt(jnp.mean(o * o, axis=1, keepdims=True) + EPS)
            og_ref[rows, :] = (((o * ro) * gout) * _silu(gg_ref[rows, :].astype(F32))).astype(BF16)
            return carry

        lax.fori_loop(0, nc, chunk, 0, unroll=8)

    hk = pl.BlockSpec((s, GDK), lambda b, h: (b, h))
    return pl.pallas_call(
        body, name="gla_fwd", grid=(bsz, GH),
        in_specs=[hk, pl.BlockSpec((s, GDK), lambda b, h: (b, GH + h)), pl.BlockSpec((s, GDV), lambda b, h: (b, 4 + h)),
                  pl.BlockSpec((s, GDV), lambda b, h: (b, 8 + h)), pl.BlockSpec((s, LANES), lambda b, h: (b, 43)),
                  pl.BlockSpec((LANES, GDK), lambda b, h: (0, h)), pl.BlockSpec((1, GDK), lambda b, h: (0, h)),
                  pl.BlockSpec((1, GDV), lambda b, h: (0, 0))],
        out_specs=[pl.BlockSpec((s, GDV), lambda b, h: (b, h)), pl.BlockSpec((s, GDV), lambda b, h: (b, h)),
                   pl.BlockSpec((nc * GDV, GDK), lambda b, h: (b * GH + h, 0))],
        out_shape=[jax.ShapeDtypeStruct((t, GH * GDV), F32), jax.ShapeDtypeStruct((t, GH * GDV), BF16),
                   jax.ShapeDtypeStruct((bsz * GH * nc * GDV, GDK), F32)],
        scratch_shapes=[pltpu.VMEM((s, GDK), F32), pltpu.VMEM((GDV, GDK), F32)],
        compiler_params=_cparams(("arbitrary", "arbitrary")),
    )(proj, proj, proj, proj, proj, wa_pad, b_alpha, g_out)


def _gla_bwd(dog, o, proj, wa_pad, b_alpha, g_out, states, bsz):
    t = proj.shape[0]
    s = t // bsz
    nc = s // CHUNK

    def body(dog_ref, o_ref, q_ref, k_ref, v_ref, gg_ref, ms_ref, wa_ref, ba_ref, go_ref, st_ref,
             dq_ref, dk_ref, dv_ref, dgg_ref, dz_ref, dba, dgo, zs, la, carry_g):
        @pl.when(pl.program_id(1) == 0)
        def _():
            dba[...] = jnp.zeros_like(dba)
            dgo[...] = jnp.zeros_like(dgo)

        z = _dot(ms_ref[...].astype(BF16), wa_ref[...], NN) + ba_ref[...]
        zs[...] = z
        la[...] = _log_sigmoid(z) * (1.0 / GTAU)
        carry_g[...] = jnp.zeros_like(carry_g)
        low, upp = _tri(True), _tri(False)
        gout = go_ref[...]
        last_row = lax.broadcasted_iota(jnp.int32, (CHUNK, GDK), 0) == CHUNK - 1

        def chunk(step, carry):
            n = nc - 1 - step
            rows = pl.ds(pl.multiple_of(n * CHUNK, CHUNK), CHUNK)
            lac = la[rows, :]
            cum = _dot(low, lac, NN, lax.Precision.HIGHEST)
            ce = jnp.sum(lac, axis=0, keepdims=True)
            e = jnp.exp(ce - cum)
            dec = jnp.exp(ce)
            kf = k_ref[rows, :].astype(F32)
            kd = (kf * e).astype(BF16)
            vv = v_ref[rows, :].astype(BF16)
            qs = (q_ref[rows, :].astype(F32) * (GDK ** -0.5)).astype(BF16)
            ov = o_ref[rows, :]
            ro = lax.rsqrt(jnp.mean(ov * ov, axis=1, keepdims=True) + EPS)
            on = ov * ro
            gg = gg_ref[rows, :].astype(F32)
            sg = _sigmoid(gg)
            dogv = dog_ref[rows, :].astype(F32)
            dgg_ref[rows, :] = (dogv * (on * gout) * (sg * (1.0 + gg * (1.0 - sg)))).astype(BF16)
            t1 = dogv * (gg * sg)
            dgo[0:1, :] += jnp.sum(t1 * on, axis=0, keepdims=True)
            don = t1 * gout
            do = ro * (don - on * jnp.mean(don * on, axis=1, keepdims=True))
            dob = do.astype(BF16)
            st_n = st_ref[pl.ds(pl.multiple_of(n * GDV, GDV), GDV), :]
            dq_ref[rows, :] = (_dot(dob, st_n.astype(BF16), NN) * (GDK ** -0.5)).astype(BF16)
            dn = carry_g[...] + _dot(dob, qs, TN)
            prev = jnp.maximum(n - 1, 0)
            st_p = st_ref[pl.ds(pl.multiple_of(prev * GDV, GDV), GDV), :] * jnp.where(n > 0, 1.0, 0.0)
            ddec = jnp.sum(dn * st_p, axis=0, keepdims=True)
            dnb = dn.astype(BF16)
            dkd = _dot(vv, dnb, NN)
            dv_ref[rows, :] = _dot(kd, dnb, NT).astype(BF16)
            dk_ref[rows, :] = (dkd * e).astype(BF16)
            w = dkd * kf * e
            dce = jnp.sum(w, axis=0, keepdims=True) + ddec * dec
            dcum = jnp.where(last_row, dce - w, -w)
            dla = _dot(upp, dcum, NN, lax.Precision.HIGHEST)
            dz = dla * (1.0 / GTAU) * _sigmoid(-zs[rows, :])
            dba[0:1, :] += jnp.sum(dz, axis=0, keepdims=True)
            dz_ref[rows, :] = dz.astype(BF16)
            carry_g[...] = dn * dec
            return carry

        lax.fori_loop(0, nc, chunk, 0, unroll=8)

    hv = pl.BlockSpec((s, GDV), lambda h, b: (b, h))
    hk = pl.BlockSpec((s, GDK), lambda h, b: (b, h))
    return pl.pallas_call(
        body, name="gla_bwd", grid=(GH, bsz),
        in_specs=[hv, hv, hk, pl.BlockSpec((s, GDK), lambda h, b: (b, GH + h)),
                  pl.BlockSpec((s, GDV), lambda h, b: (b, 4 + h)), pl.BlockSpec((s, GDV), lambda h, b: (b, 8 + h)),
                  pl.BlockSpec((s, LANES), lambda h, b: (b, 43)), pl.BlockSpec((LANES, GDK), lambda h, b: (0, h)),
                  pl.BlockSpec((1, GDK), lambda h, b: (0, h)), pl.BlockSpec((1, GDV), lambda h, b: (0, 0)),
                  pl.BlockSpec((nc * GDV, GDK), lambda h, b: (b * GH + h, 0))],
        out_specs=[hk, hk, hv, hv, hk, pl.BlockSpec((8, GDK), lambda h, b: (0, h)),
                   pl.BlockSpec((8, GDV), lambda h, b: (h, 0))],
        out_shape=[jax.ShapeDtypeStruct((t, GH * GDK), BF16), jax.ShapeDtypeStruct((t, GH * GDK), BF16),
                   jax.ShapeDtypeStruct((t, GH * GDV), BF16), jax.ShapeDtypeStruct((t, GH * GDV), BF16),
                   jax.ShapeDtypeStruct((t, GH * GDK), BF16), jax.ShapeDtypeStruct((8, GH * GDK), F32),
                   jax.ShapeDtypeStruct((8 * GH, GDV), F32)],
        scratch_shapes=[pltpu.VMEM((s, GDK), F32), pltpu.VMEM((s, GDK), F32), pltpu.VMEM((GDV, GDK), F32)],
        compiler_params=_cparams(("arbitrary", "arbitrary")),
    )(dog, o, proj, proj, proj, proj, proj, wa_pad, b_alpha, g_out, states)


def _rope_tables(pos_ref, fr_ref, sg_ref):
    ang = pos_ref[...].astype(F32) * fr_ref[...]
    return jnp.cos(ang), jnp.sin(ang) * sg_ref[...]


def _partner(x):
    lane = lax.broadcasted_iota(jnp.int32, x.shape, 1)
    return jnp.where(lane < NOPE + ROPE // 2, pltpu.roll(x, LANES - ROPE // 2, 1), pltpu.roll(x, ROPE // 2, 1))


def _mla_rows(t):
    return _tile(t, 256)


def _mla_pre_fwd(proj, pos, fr, sg, q_lat_g, kv_lat_g, qn_g, kn_g, wuq, wukv):
    t = proj.shape[0]
    tr = _mla_rows(t)

    def body(cq_ref, ckv_ref, ms_ref, pos_ref, fr_ref, sg_ref, qlg, kvlg, qng, kng, wuq_ref, wukv_ref, q_out, k_out, v_out):
        cos, sin = _rope_tables(pos_ref, fr_ref, sg_ref)
        cq = cq_ref[...].astype(F32)
        cqn = (cq * lax.rsqrt(jnp.mean(cq * cq, axis=1, keepdims=True) + EPS) * qlg[...]).astype(BF16)
        ckv = ckv_ref[...].astype(F32)
        ckvn = (ckv * lax.rsqrt(jnp.mean(ckv * ckv, axis=1, keepdims=True) + EPS) * kvlg[...]).astype(BF16)
        lane = lax.broadcasted_iota(jnp.int32, (tr, HP), 1)
        kpe = jnp.where((lane >= NOPE) & (lane < MQK), ms_ref[...].astype(F32), 0.0)
        lane_all = lax.broadcasted_iota(jnp.int32, (tr, MH * HP), 1)
        v_out[...] = jnp.where(lane_all % HP == MV, 1.0, _dot(ckvn, wukv_ref[:, MH * HP:], NN)).astype(BF16)
        for h in range(MH):
            cols = slice(h * HP, (h + 1) * HP)
            qh = _dot(cqn, wuq_ref[:, cols], NN)
            qn = qh * lax.rsqrt(jnp.sum(qh * qh, axis=1, keepdims=True) * (1.0 / MQK) + EPS) * qng[...]
            q_out[:, cols] = (qn * cos + _partner(qn) * sin).astype(BF16)
            kh = _dot(ckvn, wukv_ref[:, cols], NN) + kpe
            kn = kh * lax.rsqrt(jnp.sum(kh * kh, axis=1, keepdims=True) * (1.0 / MQK) + EPS) * kng[...]
            k_out[:, cols] = (kn * cos + _partner(kn) * sin).astype(BF16)

    def full(a):
        return pl.BlockSpec(a.shape, lambda i: (0, 0))

    wide = pl.BlockSpec((tr, MH * HP), lambda i: (i, 0))
    return pl.pallas_call(
        body, name="mla_pre_fwd", grid=(t // tr,),
        in_specs=[pl.BlockSpec((tr, MQR), lambda i: (i, 20)), pl.BlockSpec((tr, MKVR), lambda i: (i, 42)),
                  pl.BlockSpec((tr, LANES), lambda i: (i, 43)), pl.BlockSpec((tr, 1), lambda i: (i, 0)),
                  full(fr), full(sg), full(q_lat_g), full(kv_lat_g), full(qn_g), full(kn_g), full(wuq), full(wukv)],
        out_specs=[wide, wide, wide],
        out_shape=[jax.ShapeDtypeStruct((t, MH * HP), BF16)] * 3,
        compiler_params=_cparams(("arbitrary",)),
    )(proj, proj, proj, pos, fr, sg, q_lat_g, kv_lat_g, qn_g, kn_g, wuq, wukv)


def _mla_pre_bwd(dq2, dk2, dv2, dmisc_gla, proj, pos, fr, sg, q_lat_g, kv_lat_g, qn_g, kn_g, wuq, wukv):
    t = proj.shape[0]
    tr = _mla_rows(t)

    def body(dq_ref, dk_ref, dv_ref, dmg_ref, cq_ref, ckv_ref, ms_ref, pos_ref, fr_ref, sg_ref, qlg, kvlg, qng, kng,
             wuq_ref, wukv_ref, dcq_ref, dckv_ref, dms_ref, dwuq, dwukv, acc, dqf, dkvf):
        @pl.when(pl.program_id(0) == 0)
        def _():
            dwuq[...] = jnp.zeros_like(dwuq)
            dwukv[...] = jnp.zeros_like(dwukv)
            acc[...] = jnp.zeros_like(acc)

        cos, sin = _rope_tables(pos_ref, fr_ref, sg_ref)
        cq = cq_ref[...].astype(F32)
        rc = lax.rsqrt(jnp.mean(cq * cq, axis=1, keepdims=True) + EPS)
        xc = cq * rc
        cqn = (xc * qlg[...]).astype(BF16)
        ckv = ckv_ref[...].astype(F32)
        rkv = lax.rsqrt(jnp.mean(ckv * ckv, axis=1, keepdims=True) + EPS)
        xkv = ckv * rkv
        ckvn = (xkv * kvlg[...]).astype(BF16)
        lane = lax.broadcasted_iota(jnp.int32, (tr, HP), 1)
        is_rope = (lane >= NOPE) & (lane < MQK)
        kpe = jnp.where(is_rope, ms_ref[...].astype(F32), 0.0)
        dkpe = jnp.zeros((tr, HP), F32)
        dqng = jnp.zeros((1, HP), F32)
        dkng = jnp.zeros((1, HP), F32)
        for h in range(MH):
            cols = slice(h * HP, (h + 1) * HP)
            qh = _dot(cqn, wuq_ref[:, cols], NN)
            rq = lax.rsqrt(jnp.sum(qh * qh, axis=1, keepdims=True) * (1.0 / MQK) + EPS)
            xq = qh * rq
            dy = dq_ref[:, cols].astype(F32)
            dqn = dy * cos - _partner(dy) * sin
            dqng += jnp.sum(dqn * xq, axis=0, keepdims=True)
            tq = dqn * qng[...]
            dqf[:, cols] = (rq * (tq - xq * (jnp.sum(tq * xq, axis=1, keepdims=True) * (1.0 / MQK)))).astype(BF16)
            kh = _dot(ckvn, wukv_ref[:, cols], NN) + kpe
            rk = lax.rsqrt(jnp.sum(kh * kh, axis=1, keepdims=True) * (1.0 / MQK) + EPS)
            xk = kh * rk
            dy = dk_ref[:, cols].astype(F32)
            dkn = dy * cos - _partner(dy) * sin
            dkng += jnp.sum(dkn * xk, axis=0, keepdims=True)
            tk = dkn * kng[...]
            dkh = rk * (tk - xk * (jnp.sum(tk * xk, axis=1, keepdims=True) * (1.0 / MQK)))
            dkvf[:, cols] = jnp.where(lane < NOPE, dkh, 0.0).astype(BF16)
            dkpe += jnp.where(is_rope, dkh, 0.0)
        dkvf[:, MH * HP:] = dv_ref[...]
        acc[2:3, 0:HP] += dqng
        acc[3:4, 0:HP] += dkng
        dms_ref[...] = (dmg_ref[...] + dkpe).astype(BF16)

        dqfv = dqf[...]
        dwuq[...] += _dot(cqn, dqfv, TN)
        dcqn = _dot(dqfv, wuq_ref[...], NT)
        acc[0:1, :] += jnp.sum(dcqn * xc, axis=0, keepdims=True)
        tc = dcqn * qlg[...]
        dcq_ref[...] = (rc * (tc - xc * jnp.mean(tc * xc, axis=1, keepdims=True))).astype(BF16)

        dkvfv = dkvf[...]
        dwukv[...] += _dot(ckvn, dkvfv, TN)
        dckvn = _dot(dkvfv, wukv_ref[...], NT)
        acc[1:2, 0:MKVR] += jnp.sum(dckvn * xkv, axis=0, keepdims=True)
        tkv = dckvn * kvlg[...]
        dckv_ref[...] = (rkv * (tkv - xkv * jnp.mean(tkv * xkv, axis=1, keepdims=True))).astype(BF16)

    def full(a):
        return pl.BlockSpec(a.shape, lambda i: (0, 0))

    wide = pl.BlockSpec((tr, MH * HP), lambda i: (i, 0))
    narrow = pl.BlockSpec((tr, LANES), lambda i: (i, 0))
    return pl.pallas_call(
        body, name="mla_pre_bwd", grid=(t // tr,),
        in_specs=[wide, wide, wide, narrow,
                  pl.BlockSpec((tr, MQR), lambda i: (i, 20)), pl.BlockSpec((tr, MKVR), lambda i: (i, 42)),
                  pl.BlockSpec((tr, LANES), lambda i: (i, 43)), pl.BlockSpec((tr, 1), lambda i: (i, 0)),
                  full(fr), full(sg), full(q_lat_g), full(kv_lat_g), full(qn_g), full(kn_g), full(wuq), full(wukv)],
        out_specs=[pl.BlockSpec((tr, MQR), lambda i: (i, 0)), narrow, narrow,
                   pl.BlockSpec((MQR, MH * HP), lambda i: (0, 0)), pl.BlockSpec((MKVR, 2 * MH * HP), lambda i: (0, 0)),
                   pl.BlockSpec((8, MQR), lambda i: (0, 0))],
        out_shape=[jax.ShapeDtypeStruct((t, MQR), BF16), jax.ShapeDtypeStruct((t, MKVR), BF16),
                   jax.ShapeDtypeStruct((t, LANES), BF16), jax.ShapeDtypeStruct((MQR, MH * HP), F32),
                   jax.ShapeDtypeStruct((MKVR, 2 * MH * HP), F32), jax.ShapeDtypeStruct((8, MQR), F32)],
        scratch_shapes=[pltpu.VMEM((tr, MH * HP), BF16), pltpu.VMEM((tr, 2 * MH * HP), BF16)],
        compiler_params=_cparams(("arbitrary",)),
    )(dq2, dk2, dv2, dmisc_gla, proj, proj, proj, pos, fr, sg, q_lat_g, kv_lat_g, qn_g, kn_g, wuq, wukv)


ATT_FWD_TILES = (1024, 512)
ATT_BWD_TILES = (512, 512)
ATT_HEADS = 2
NEG = -1e30
LOG2E = 1.4426950408889634


def _att_mask(q0, k0, tq, tk):
    qc = (q0 + lax.broadcasted_iota(jnp.int32, (tq, tk), 0)) // CHUNK
    kc = (k0 + lax.broadcasted_iota(jnp.int32, (tq, tk), 1)) // CHUNK
    return kc <= qc


def _att_tiles(s, tiles):
    return _tile(s, tiles[0]), _tile(s, tiles[1])


def _lanes(x, n):
    return x if n == 1 else jnp.concatenate([x] * n, axis=1)


def _attn_fwd(q2, k2, v2, bsz):
    t = q2.shape[0]
    s = t // bsz
    tq, tk = _att_tiles(s, ATT_FWD_TILES)
    nq, groups, n_diag = s // tq, tk // HP, max(tq // tk, 1)
    scale = MQK ** -0.5
    c2 = scale * LOG2E
    heads = range(ATT_HEADS)

    def body(q_ref, k_ref, v_ref, o_ref, lse_ref):
        def q_loop(qi, carry):
            q0 = pl.multiple_of(qi * tq, tq)
            rows = pl.ds(q0, tq)
            n_full = q0 // tk
            qs = [q_ref[rows, h * HP:(h + 1) * HP] for h in heads]

            def scores(h, kj, masked):
                k0 = pl.multiple_of(kj * tk, tk)
                sc = _dot(qs[h], k_ref[pl.ds(k0, tk), h * HP:(h + 1) * HP], NT)
                return jnp.where(_att_mask(q0, k0, tq, tk), sc, NEG) if masked else sc

            def fold(mx, sc):
                for j in range(groups):
                    mx = jnp.maximum(mx, sc[:, j * HP:(j + 1) * HP])
                return mx

            mx = lax.fori_loop(0, n_full, lambda kj, mx: tuple(fold(mx[h], scores(h, kj, False)) for h in heads),
                               tuple(jnp.full((tq, HP), NEG, F32) for _ in heads))
            for u in range(n_diag):
                mx = tuple(fold(mx[h], scores(h, n_full + u, True)) for h in heads)
            mb = [jnp.broadcast_to(jnp.max(mx[h], axis=1, keepdims=True), (tq, HP)) for h in heads]

            def weighted(h, kj, masked):
                p = jnp.exp2((scores(h, kj, masked) - _lanes(mb[h], groups)) * c2)
                k0 = pl.multiple_of(kj * tk, tk)
                return _dot(p.astype(BF16), v_ref[pl.ds(k0, tk), h * HP:(h + 1) * HP], NN)

            acc = lax.fori_loop(0, n_full, lambda kj, acc: tuple(acc[h] + weighted(h, kj, False) for h in heads),
                                tuple(jnp.zeros((tq, HP), F32) for _ in heads))
            for u in range(n_diag):
                acc = tuple(acc[h] + weighted(h, n_full + u, True) for h in heads)
            lane = lax.broadcasted_iota(jnp.int32, (tq, HP), 1)
            for h in heads:
                a = acc[h]
                l = jnp.sum(jnp.where(lane == MV, a, 0.0), axis=1, keepdims=True)
                o_ref[rows, h * HP:(h + 1) * HP] = (a / l).astype(BF16)
                lse_ref[rows, h * HP:(h + 1) * HP] = mb[h] * scale + jnp.log(l)
            return carry

        lax.fori_loop(0, nq, q_loop, 0)

    spec = pl.BlockSpec((s, ATT_HEADS * HP), lambda b, h: (b, h))
    return pl.pallas_call(
        body, name="attn_fwd", grid=(bsz, MH // ATT_HEADS), in_specs=[spec] * 3, out_specs=[spec, spec],
        out_shape=[jax.ShapeDtypeStruct((t, MH * HP), BF16), jax.ShapeDtypeStruct((t, MH * HP), F32)],
        compiler_params=_cparams(("arbitrary", "arbitrary")),
    )(q2, k2, v2)


def _attn_bwd(q2, k2, v2, do2, o2, lse2, bsz):
    t = q2.shape[0]
    s = t // bsz
    tq, tk = _att_tiles(s, ATT_BWD_TILES)
    nq, nk, per, groups = s // tq, s // tk, max(tk // tq, 1), tk // HP
    scale = MQK ** -0.5
    c2 = scale * LOG2E
    heads = range(ATT_HEADS)

    def body(q_ref, k_ref, v_ref, do_ref, o_ref, lse_ref, dq_ref, dk_ref, dv_ref, dq_acc, delta, lse_b2):
        dq_acc[...] = jnp.zeros_like(dq_acc)

        def d_loop(i, carry):
            rows = pl.ds(pl.multiple_of(i * tq, tq), tq)
            for h in heads:
                hs = slice(h * HP, (h + 1) * HP)
                dl = jnp.sum(do_ref[rows, hs].astype(F32) * o_ref[rows, hs].astype(F32), axis=1, keepdims=True)
                delta[rows, hs] = jnp.broadcast_to(dl, (tq, HP))
            lse_b2[rows, :] = lse_ref[rows, :] * LOG2E
            return carry

        lax.fori_loop(0, nq, d_loop, 0)

        def k_loop(kj, carry):
            k0 = pl.multiple_of(kj * tk, tk)
            kk = [k_ref[pl.ds(k0, tk), h * HP:(h + 1) * HP] for h in heads]
            vv = [v_ref[pl.ds(k0, tk), h * HP:(h + 1) * HP] for h in heads]

            def tile(qi, c, masked):
                q0 = pl.multiple_of(qi * tq, tq)
                rows = pl.ds(q0, tq)
                out = []
                for h in heads:
                    hs = slice(h * HP, (h + 1) * HP)
                    dk, dv = c[h]
                    q = q_ref[rows, hs]
                    do = do_ref[rows, hs]
                    e = _dot(q, kk[h], NT) * c2 - _lanes(lse_b2[rows, hs], groups)
                    if masked:
                        e = jnp.where(_att_mask(q0, k0, tq, tk), e, NEG)
                    p = jnp.exp2(e)
                    dv = dv + _dot(p.astype(BF16), do, TN)
                    ds = (p * (_dot(do, vv[h], NT) - _lanes(delta[rows, hs], groups))).astype(BF16)
                    dq_acc[rows, hs] += _dot(ds, kk[h], NN)
                    dk = dk + _dot(ds, q, TN)
                    out.append((dk, dv))
                return tuple(out)

            zero = jnp.zeros((tk, HP), F32)
            c = tuple((zero, zero) for _ in heads)
            first = k0 // tq
            for u in range(per):
                c = tile(first + u, c, True)
            c = lax.fori_loop(first + per, nq, lambda qi, c: tile(qi, c, False), c)
            for h in heads:
                dk_ref[pl.ds(k0, tk), h * HP:(h + 1) * HP] = (c[h][0] * scale).astype(BF16)
                dv_ref[pl.ds(k0, tk), h * HP:(h + 1) * HP] = c[h][1].astype(BF16)
            return carry

        lax.fori_loop(0, nk, k_loop, 0)
        dq_ref[...] = (dq_acc[...] * scale).astype(BF16)

    spec = pl.BlockSpec((s, ATT_HEADS * HP), lambda b, h: (b, h))
    return pl.pallas_call(
        body, name="attn_bwd", grid=(bsz, MH // ATT_HEADS), in_specs=[spec] * 6, out_specs=[spec] * 3,
        out_shape=[jax.ShapeDtypeStruct((t, MH * HP), BF16)] * 3,
        scratch_shapes=[pltpu.VMEM((s, ATT_HEADS * HP), F32)] * 3,
        compiler_params=_cparams(("arbitrary", "arbitrary")),
    )(q2, k2, v2, do2, o2, lse2)


def _perm_w_in_t(w):
    z = lambda n: jnp.zeros((n, w.shape[1]), w.dtype)
    return jnp.concatenate([w[:3072], w[3504:5552], w[3088:3344], w[3344:3472], w[3072:3088], z(48), w[3472:3504], z(32)],
                           axis=0)


def _unperm_w_in_t(g):
    return jnp.concatenate([g[:3072], g[5504:5520], g[5120:5376], g[5376:5504], g[5568:5600], g[3072:5120]], axis=0)


def _pad_wa(w):
    return jnp.pad(w, ((0, LANES - GLR), (0, 0)))


def _pad_wuq(w):
    return jnp.pad(w.reshape(MQR, MH, MQK), ((0, 0), (0, 0), (0, HP - MQK))).reshape(MQR, MH * HP)


def _unpad_wuq(g):
    return g.reshape(MQR, MH, HP)[:, :, :MQK].reshape(MQR, MH * MQK)


def _pad_wukv(w):
    w3 = w.reshape(MKVR, MH, NOPE + MV)
    kp = jnp.pad(w3[:, :, :NOPE], ((0, 0), (0, 0), (0, HP - NOPE))).reshape(MKVR, MH * HP)
    vp = jnp.pad(w3[:, :, NOPE:], ((0, 0), (0, 0), (0, HP - MV))).reshape(MKVR, MH * HP)
    return jnp.concatenate([kp, vp], axis=1)


def _unpad_wukv(g):
    kp = g[:, :MH * HP].reshape(MKVR, MH, HP)[:, :, :NOPE]
    vp = g[:, MH * HP:].reshape(MKVR, MH, HP)[:, :, :MV]
    return jnp.concatenate([kp, vp], axis=2).reshape(MKVR, MH * (NOPE + MV))


def _pad_wo(w):
    return jnp.pad(w.reshape(MH, MV, D), ((0, 0), (0, HP - MV), (0, 0))).reshape(MH * HP, D)


def _unpad_wo(g):
    return g.reshape(MH, HP, D)[:, :MV, :].reshape(MH * MV, D)


def _pad_lanes(v, n=HP):
    return jnp.pad(v, ((0, 0), (0, n - v.shape[1])))


def _local_step(x, positions, tgt, wt, sp, mod3):
    bsz, s, _ = x.shape
    t = bsz * s
    x2 = x.reshape(t, D)
    tgt2 = tgt.reshape(t, D)
    pos = positions.reshape(t, 1)
    fr16 = ROPE_THETA ** (-jnp.arange(0, ROPE, 2, dtype=F32) / ROPE)
    zero = lambda n: jnp.zeros((n,), F32)
    fr = jnp.concatenate([zero(NOPE), fr16, fr16, zero(HP - MQK)]).reshape(1, HP)
    sg = jnp.concatenate([zero(NOPE), -jnp.ones((ROPE // 2,), F32), jnp.ones((ROPE // 2,), F32), zero(HP - MQK)]).reshape(1, HP)

    w_in_t = _perm_w_in_t(wt["w_in"])
    wa_pad = _pad_wa(wt["gla_w_alpha"].T)
    wuq = _pad_wuq(wt["mla_w_uq"].T)
    wukv = _pad_wukv(wt["mla_w_ukv"].T)
    wo_pad = _pad_wo(wt["mla_w_o"])
    qn_g, kn_g = _pad_lanes(sp["mla_qn_g"]), _pad_lanes(sp["mla_kn_g"])

    _, h = _norm_mod_fwd(x2, sp["norm1_g"], mod3, 0, 1, "norm1_fwd")
    proj = _mm(h, w_in_t, "nt", (BF16,), "proj_fwd")
    o_gla, og, states = _gla_fwd(proj, wa_pad, sp["gla_b_alpha"], sp["gla_out_norm_g"], bsz)
    y_a = _mm(og, wt["gla_w_o"], "nn", (BF16,), "gla_out_fwd")
    q2, k2, v2 = _mla_pre_fwd(proj, pos, fr, sg, sp["mla_q_lat_g"], sp["mla_kv_lat_g"], qn_g, kn_g, wuq, wukv)
    o2, lse2 = _attn_fwd(q2, k2, v2, bsz)
    y_b = _mm(o2, wo_pad, "nn", (BF16,), "mla_out_fwd")
    mix = _merge_fwd(proj, sp["b_merge"], y_a, y_b)
    mixed = _mm(mix, wt["w_out"], "nn", (F32,), "w_out_fwd")

    x1, h2 = _norm_mod_fwd(x2, sp["norm2_g"], mod3, 3, 4, "norm2_fwd", mixed=mixed, i_gate=2)
    a, f = _mm(h2, wt["mlp_w1"], "nt", (BF16, BF16), "mlp1_fwd",
               epi=lambda acc: (acc, jnp.square(jnp.maximum(acc, 0.0))))
    ff = _mm(f, wt["mlp_w2"], "nn", (F32,), "mlp2_fwd")
    dy, dff, acc_g2, acc_loss = _loss_head(x1, ff, tgt2, mod3)

    gw = {}
    gw["mlp_w2"] = _mm(f, dff, "tn", (BF16,), "mlp2_dw")
    da = _mm(dff, wt["mlp_w2"], "nt", (BF16,), "mlp2_dx", extras=(a,),
             epi=lambda acc, av: (acc * (2.0 * jnp.maximum(av.astype(F32), 0.0)),))
    gw["mlp_w1"] = _mm(da, h2, "tn", (BF16,), "mlp1_dw")
    dh2 = _mm(da, wt["mlp_w1"], "nn", (F32,), "mlp1_dx")
    dx1, dmixed, accb2, accg2 = _norm_mod_bwd(x1, dh2, dy, sp["norm2_g"], mod3, 3, 4, "norm2_bwd", mixed=mixed, i_gate=2)

    gw["w_out"] = _mm(mix, dmixed, "tn", (BF16,), "w_out_dw")
    dmix = _mm(dmixed, wt["w_out"], "nt", (BF16,), "w_out_dx")
    dy_a, dy_b, dlogits, acc_bm = _merge_bwd(dmix, proj, sp["b_merge"], y_a, y_b)
    gw["gla_w_o"] = _mm(og, dy_a, "tn", (BF16,), "gla_out_dw")
    dog = _mm(dy_a, wt["gla_w_o"], "nt", (BF16,), "gla_out_dx")
    dq_g, dk_g, dv_g, dgg, dz, acc_ba, acc_go = _gla_bwd(dog, o_gla, proj, wa_pad, sp["gla_b_alpha"],
                                                         sp["gla_out_norm_g"], states, bsz)
    gw["gla_w_alpha"] = _mm(proj, dz, "tn", (F32,), "gla_alpha_dw", a_off=43, m=LANES)[:GLR].T.astype(BF16)
    dmisc_gla = _mm(dz, wa_pad, "nt", (F32,), "gla_alpha_dx")
    gw["mla_w_o"] = _unpad_wo(_mm(o2, dy_b, "tn", (BF16,), "mla_out_dw"))
    do2 = _mm(dy_b, wo_pad, "nt", (BF16,), "mla_out_dx")
    dq2, dk2, dv2 = _attn_bwd(q2, k2, v2, do2, o2, lse2, bsz)
    dcq, dckv, dmisc, gwuq, gwukv, acc_mla = _mla_pre_bwd(dq2, dk2, dv2, dmisc_gla, proj, pos, fr, sg, sp["mla_q_lat_g"],
                                                         sp["mla_kv_lat_g"], qn_g, kn_g, wuq, wukv)
    gw["mla_w_uq"] = _unpad_wuq(gwuq).T.astype(BF16)
    gw["mla_w_ukv"] = _unpad_wukv(gwukv).T.astype(BF16)
    dproj = jnp.concatenate([dq_g, dk_g, dv_g, dgg, dlogits, dcq, dckv, dmisc], axis=1)
    gw["w_in"] = _unperm_w_in_t(_mm(dproj, h, "tn", (BF16,), "proj_dw"))
    dh = _mm(dproj, w_in_t, "nn", (F32,), "proj_dx")
    grad_x, accb1, accg1 = _norm_mod_bwd(x2, dh, dx1, sp["norm1_g"], mod3, 0, 1, "norm1_bwd")

    dmod = jnp.stack([accb1[:, 0], accb1[:, 1], accb2[:, 2], accb2[:, 0], accb2[:, 1], acc_g2[:, 0]], axis=1)

    rows = {
        "dmod": dmod.reshape(bsz * 6, D),
        "norm1_g": accg1[0:1], "norm2_g": accg2[0:1],
        "b_merge": acc_bm[0:1].reshape(2, D),
        "gla_b_alpha": _pad_lanes(acc_ba[0:1], D),
        "gla_out_norm_g": _pad_lanes(acc_go.reshape(GH, 8, GDV)[:, 0, :], D),
        "mla_q_lat_g": _pad_lanes(acc_mla[0:1], D), "mla_kv_lat_g": _pad_lanes(acc_mla[1:2], D),
        "mla_qn_g": _pad_lanes(acc_mla[2:3], D), "mla_kn_g": _pad_lanes(acc_mla[3:4], D),
        "loss": acc_loss[0:1],
    }
    return grad_x.reshape(bsz, s, D), gw, rows


HBM_SPEC = pl.BlockSpec(memory_space=pltpu.HBM)


def _all_gather(p, name):
    r, cdim = p.shape

    def body(p_ref, out_ref, send_sems, recv_sems, local_sem):
        x, y, c = lax.axis_index("x"), lax.axis_index("y"), lax.axis_index("c")
        me, sibling = (x, y, c), (x, y, 1 - c)
        chips = [(1 - x, y), (x, 1 - y), (1 - x, 1 - y)]

        def slot(px, py, pc):
            return out_ref.at[4 * px + 2 * py + pc]

        def copy(k, block, to, src=None):
            return pltpu.make_async_remote_copy(
                src_ref=slot(*block) if src is None else src, dst_ref=slot(*block),
                send_sem=send_sems.at[k], recv_sem=recv_sems.at[k], device_id=to, device_id_type=MESH)

        mine = pltpu.make_async_copy(p_ref, slot(*me), local_sem)
        mine.start()
        first = [copy(0, me, sibling, src=p_ref)] + [copy(1 + j, me, (*chip, c), src=p_ref) for j, chip in enumerate(chips)]
        for cp in first:
            cp.start()
        passed = [copy(4 + j, (*chip, c), sibling) for j, chip in enumerate(chips)]
        for j, chip in enumerate(chips):
            copy(1 + j, (*chip, c), me).wait_recv()
            passed[j].start()
        copy(0, sibling, me).wait_recv()
        for j, chip in enumerate(chips):
            copy(4 + j, (*chip, 1 - c), me).wait_recv()
        for cp in first + passed:
            cp.wait_send()
        mine.wait()

    return pl.pallas_call(
        body, name=name, out_shape=jax.ShapeDtypeStruct((N_DEV, r, cdim), p.dtype),
        in_specs=[HBM_SPEC], out_specs=HBM_SPEC,
        scratch_shapes=[pltpu.SemaphoreType.DMA((7,)), pltpu.SemaphoreType.DMA((7,)), pltpu.SemaphoreType.DMA(())],
    )(p)


def _sibling_exchange(g):
    def body(g_ref, out_ref, send_sem, recv_sem):
        x, y, c = lax.axis_index("x"), lax.axis_index("y"), lax.axis_index("c")
        cp = pltpu.make_async_remote_copy(src_ref=g_ref, dst_ref=out_ref, send_sem=send_sem, recv_sem=recv_sem,
                                          device_id=(x, y, 1 - c), device_id_type=MESH)
        cp.start()
        cp.wait()

    return pl.pallas_call(
        body, name="rs_sibling_exchange", out_shape=jax.ShapeDtypeStruct(g.shape, g.dtype),
        in_specs=[HBM_SPEC], out_specs=HBM_SPEC,
        scratch_shapes=[pltpu.SemaphoreType.DMA(()), pltpu.SemaphoreType.DMA(())],
    )(g)


def _chip_exchange(tsum):
    def body(t_ref, out_ref, send_sems, recv_sems, local_sem):
        x, y, c = lax.axis_index("x"), lax.axis_index("y"), lax.axis_index("c")
        my_chip = 2 * x + y
        chips = [(1 - x, y), (x, 1 - y), (1 - x, 1 - y)]
        mine = pltpu.make_async_copy(t_ref.at[my_chip], out_ref.at[my_chip], local_sem)
        mine.start()
        sends = [pltpu.make_async_remote_copy(
            src_ref=t_ref.at[2 * px + py], dst_ref=out_ref.at[my_chip], send_sem=send_sems.at[j], recv_sem=recv_sems.at[j],
            device_id=(px, py, c), device_id_type=MESH) for j, (px, py) in enumerate(chips)]
        for cp in sends:
            cp.start()
        for j, (px, py) in enumerate(chips):
            pltpu.make_async_remote_copy(
                src_ref=t_ref.at[my_chip], dst_ref=out_ref.at[2 * px + py], send_sem=send_sems.at[j],
                recv_sem=recv_sems.at[j], device_id=(px, py, c), device_id_type=MESH).wait_recv()
        for cp in sends:
            cp.wait_send()
        mine.wait()

    return pl.pallas_call(
        body, name="rs_chip_exchange", out_shape=jax.ShapeDtypeStruct(tsum.shape, tsum.dtype),
        in_specs=[HBM_SPEC], out_specs=HBM_SPEC,
        scratch_shapes=[pltpu.SemaphoreType.DMA((3,)), pltpu.SemaphoreType.DMA((3,)), pltpu.SemaphoreType.DMA(())],
    )(tsum)


def _pair_sum(a, b):
    n, r, cdim = a.shape
    blk = pl.BlockSpec((1, PACK_BLOCK, cdim), lambda j, i: (j, i, 0))

    def body(a_ref, b_ref, o_ref):
        o_ref[...] = (a_ref[...].astype(F32) + b_ref[...].astype(F32)).astype(BF16)

    return pl.pallas_call(
        body, name="rs_pair_sum", grid=(n, r // PACK_BLOCK), in_specs=[blk, blk], out_specs=blk,
        out_shape=jax.ShapeDtypeStruct(a.shape, BF16), compiler_params=_cparams(("arbitrary", "arbitrary")),
    )(a, b)


def _adamw_math(w, g, m, v):
    m = ADAM_B1 * m + (1.0 - ADAM_B1) * g
    v = ADAM_B2 * v + (1.0 - ADAM_B2) * jnp.square(g)
    m_hat = m / (1.0 - ADAM_B1 ** ADAM_STEP)
    v_hat = v / (1.0 - ADAM_B2 ** ADAM_STEP)
    delta = -ADAM_LR * (m_hat / (jnp.sqrt(v_hat) + ADAM_EPS) + ADAM_WD * w)
    return delta, m, v


def _slab_sum(parts):
    _, r, cdim = parts.shape
    blk = pl.BlockSpec((PACK_BLOCK, cdim), lambda i: (i, 0))

    def body(p_ref, g_out):
        g = p_ref[0].astype(F32)
        for j in range(1, 4):
            g = g + p_ref[j].astype(F32)
        g_out[...] = g

    return pl.pallas_call(
        body, name="rs_slab_sum", grid=(r // PACK_BLOCK,),
        in_specs=[pl.BlockSpec((4, PACK_BLOCK, cdim), lambda i: (0, i, 0))], out_specs=blk,
        out_shape=jax.ShapeDtypeStruct((r, cdim), F32), compiler_params=_cparams(("arbitrary",)),
    )(parts)


def _adamw(g, w, m, v, name):
    r, cdim = w.shape
    rb = _tile(r, 256)
    blk = pl.BlockSpec((rb, cdim), lambda i: (i, 0))

    def body(g_ref, w_ref, m_ref, v_ref, d_out, m_out, v_out):
        d_out[...], m_out[...], v_out[...] = _adamw_math(w_ref[...], g_ref[...], m_ref[...], v_ref[...])

    return pl.pallas_call(
        body, name=name, grid=(r // rb,), in_specs=[blk] * 4, out_specs=[blk] * 3,
        out_shape=[jax.ShapeDtypeStruct((r, cdim), F32)] * 3, compiler_params=_cparams(("arbitrary",)),
    )(g, w, m, v)


def _adamw_small(parts, w, m, v):
    def body(p_ref, w_ref, m_ref, v_ref, g_out, d_out, m_out, v_out, loss_out):
        def total(srcs):
            acc = None
            for r in srcs:
                for j in range(N_DEV):
                    term = p_ref[j, r:r + 1, :]
                    acc = term if acc is None else acc + term
            return acc

        for prow, srcs in enumerate(SMALL_SOURCES):
            one = slice(prow, prow + 1)
            g = total(srcs)
            g_out[one, :] = g
            d_out[one, :], m_out[one, :], v_out[one, :] = _adamw_math(w_ref[one, :], g, m_ref[one, :], v_ref[one, :])
        loss_out[...] = jnp.broadcast_to(jnp.sum(total(LOSS_SOURCE), axis=1, keepdims=True), (8, LANES))

    full = lambda shp: pl.BlockSpec(shp, lambda i: (0,) * len(shp))
    return pl.pallas_call(
        body, name="adamw_small", grid=(1,),
        in_specs=[full((N_DEV, SMALL_ROWS, D)), full((16, D)), full((16, D)), full((16, D))],
        out_specs=[full((16, D))] * 4 + [full((8, LANES))],
        out_shape=[jax.ShapeDtypeStruct((16, D), F32)] * 4 + [jax.ShapeDtypeStruct((8, LANES), F32)],
        compiler_params=_cparams(("arbitrary",)),
    )(parts, w, m, v)


def _to_slab(shard, form):
    if form == "N":
        return shard
    return shard.T if form == "T" else shard.T.reshape(-1, D)


def _from_slab(block, form, shard_shape):
    if form == "N":
        return block
    return block.T if form == "T" else block.reshape(shard_shape[1], shard_shape[0]).T


def _gathered_full(g, form, shard_shape):
    if form == "TR":
        return g.reshape(N_DEV * shard_shape[1], shard_shape[0])
    return g.reshape(N_DEV * g.shape[1], D)


def _small_pack(vals):
    rows = []
    for n in SMALL:
        v = vals[n].reshape(-1)
        k = -(-v.shape[0] // D)
        rows.append(jnp.pad(v, (0, k * D - v.shape[0])).reshape(k, D))
    return jnp.concatenate(rows, axis=0)


def _small_unpack(packed, shapes):
    out = {}
    for n in SMALL:
        k = shapes[n][-1]
        r0 = SMALL_ROW[n]
        out[n] = packed[r0:r0 + -(-k // D)].reshape(-1)[:k].reshape(shapes[n])
    return out


def kernel(x, c, positions, w_ada, b_ada, norm1_g, w_in, b_merge, gla_w_alpha, gla_b_alpha, gla_out_norm_g, gla_w_o, mla_q_lat_g, mla_w_uq, mla_kv_lat_g, mla_w_ukv, mla_qn_g, mla_kn_g, mla_w_o, w_out, norm2_g, mlp_w1, mlp_w2, loss_target, m_w_ada, m_b_ada, m_norm1_g, m_w_in, m_b_merge, m_gla_w_alpha, m_gla_b_alpha, m_gla_out_norm_g, m_gla_w_o, m_mla_q_lat_g, m_mla_w_uq, m_mla_kv_lat_g, m_mla_w_ukv, m_mla_qn_g, m_mla_kn_g, m_mla_w_o, m_w_out, m_norm2_g, m_mlp_w1, m_mlp_w2, v_w_ada, v_b_ada, v_norm1_g, v_w_in, v_b_merge, v_gla_w_alpha, v_gla_b_alpha, v_gla_out_norm_g, v_gla_w_o, v_mla_q_lat_g, v_mla_w_uq, v_mla_kv_lat_g, v_mla_w_ukv, v_mla_qn_g, v_mla_kn_g, v_mla_w_o, v_w_out, v_norm2_g, v_mlp_w1, v_mlp_w2):
    args = dict(locals())
    wts = {n: args[n][0] for n in WEIGHTS}
    mom = {n: args["m_" + n][0] for n in WEIGHTS}
    var = {n: args["v_" + n][0] for n in WEIGHTS}
    my_c = lax.axis_index("c")
    my_dev = 4 * lax.axis_index("x") + 2 * lax.axis_index("y") + my_c
    bsz = x.shape[0]
    sp = {n: wts[n].reshape(1, -1) for n in SMALL}

    slab = jnp.concatenate([jnp.zeros((r, D), BF16) if n is None else _to_slab(wts[n], form).astype(BF16)
                            for n, r, form in SLAB], axis=0)
    gathered = _all_gather(slab, "weights_all_gather")
    wt, off = {}, 0
    for n, r, form in SLAB:
        if n is not None:
            wt[n] = _gathered_full(gathered[:, off:off + r], form, wts[n].shape)
        off += r

    c_all = _all_gather(jnp.pad(c, ((0, 8 - bsz), (0, 0))), "cond_all_gather")[:, :bsz].reshape(N_DEV * bsz, D)
    bias = lax.dynamic_slice_in_dim(sp["b_ada"], my_dev * ADA_COLS, ADA_COLS, axis=1)
    mod_cols = _mm(c_all, wts["w_ada"], "nn", (F32,), "ada_fwd", pro=_silu, epi=lambda acc, b: (acc + b,),
                   extras=(jnp.broadcast_to(bias, (N_DEV * bsz, ADA_COLS)),))
    mod_all = _all_gather(mod_cols, "mod_all_gather")
    mod_mine = lax.dynamic_slice_in_dim(mod_all, my_dev * bsz, bsz, axis=1)
    mod3 = jnp.transpose(mod_mine, (1, 0, 2)).reshape(bsz, 6, D)

    grad_x, gw, rows = _local_step(x, positions, loss_target, wt, sp, mod3)

    per_dev = jnp.concatenate([jnp.zeros((N_DEV, r, D), BF16) if n is None else gw[n].reshape(N_DEV, r, D)
                               for n, r, _ in SLAB], axis=1).reshape(4, 2, PACK_ROWS, D)
    keep = lax.dynamic_index_in_dim(per_dev, my_c, axis=1, keepdims=False)
    give = lax.dynamic_index_in_dim(per_dev, 1 - my_c, axis=1, keepdims=False)
    chip_sums = _pair_sum(keep, _sibling_exchange(give))
    gslab = _slab_sum(_chip_exchange(chip_sums))
    big, off = {}, 0
    for n, r, form in SLAB:
        if n is not None:
            g = _from_slab(gslab[off:off + r], form, wts[n].shape)
            big[n] = (g,) + tuple(_adamw(g, wts[n], mom[n], var[n], "adamw_" + n))
        off += r

    order = ["dmod", "norm1_g", "norm2_g", "b_merge", "gla_b_alpha", "gla_out_norm_g", "mla_q_lat_g", "mla_kv_lat_g",
             "mla_qn_g", "mla_kn_g", "loss"]
    part_rows = jnp.concatenate([rows[n] for n in order], axis=0)
    part_rows = jnp.pad(part_rows, ((0, SMALL_ROWS - part_rows.shape[0]), (0, 0)))
    all_rows = _all_gather(part_rows, "partials_all_gather")

    dmod_all = all_rows[:, :6 * bsz].reshape(N_DEV * bsz, 6 * D)
    dmod_cols = lax.dynamic_slice_in_dim(dmod_all, my_dev * ADA_COLS, ADA_COLS, axis=1)
    g_ada = _mm(c_all, dmod_cols, "tn", (F32,), "ada_dw", pro=_silu)
    big["w_ada"] = (g_ada,) + tuple(_adamw(g_ada, wts["w_ada"], mom["w_ada"], var["w_ada"], "adamw_w_ada"))

    small = _adamw_small(all_rows, _small_pack({n: wts[n] for n in SMALL}), _small_pack({n: mom[n] for n in SMALL}),
                         _small_pack({n: var[n] for n in SMALL}))
    loss = small[4][0, 0]
    small_shapes = {n: wts[n].shape for n in SMALL}
    small = [_small_unpack(o, small_shapes) for o in small[:4]]

    outs = [loss, grad_x]
    for k in range(4):
        for n in WEIGHTS:
            val = big[n][k] if n in BIG else small[k][n]
            outs.append(val.reshape((1,) + tuple(wts[n].shape)))
    return tuple(outs)
```

```python
import functools

import jax
import jax.numpy as jnp
from jax import lax
from jax.experimental import pallas as pl
from jax.experimental.pallas import tpu as pltpu

F32 = jnp.float32
BF16 = jnp.bfloat16
MESH = pl.DeviceIdType.MESH

D = 1024
EPS = 1e-6
CHUNK = 64
GH, GDK, GDV, GLR, GTAU = 4, 128, 256, 16, 16.0
MH, MQR, MKVR, NOPE, ROPE, MV = 16, 256, 128, 64, 32, 64
MQK = NOPE + ROPE
HP = 128
FF = 4 * D
ROPE_THETA = 10000.0
IN_WIDTH = 5552
PW = 5632
N_DEV = 8
LANES = 128
SLAB_BLOCK_MAX = 400
ADA_COLS = 6 * D // N_DEV
SMALL_ROWS = 32
SMALL_SOURCES = tuple([(r, 6 + r) for r in range(6)] + [(12,), (13,), (14,), (15,), (16,), (17, 18, 19, 20),
                                                         (21,), (22,), (23,), (24,)])
LOSS_SOURCE = (25,)
VMEM_LIMIT = 56 * 1024 * 1024

ADAM_LR, ADAM_B1, ADAM_B2, ADAM_EPS, ADAM_WD, ADAM_STEP = 0.001, 0.9, 0.999, 1e-08, 0.01, 10

SLAB_A = (("w_in", 694, "T"), ("gla_w_alpha", 1, "TR"), (None, 9, None), ("mla_w_uq", 48, "TR"), ("mla_w_ukv", 32, "TR"))
SLAB_B = (("mlp_w1", 512, "T"), ("gla_w_o", 128, "N"), ("mla_w_o", 128, "N"), ("w_out", 128, "N"), ("mlp_w2", 512, "N"))
BIG = ("w_ada",) + tuple(n for n, _, _ in SLAB_A + SLAB_B if n is not None)
SHARD_SHAPES = {"w_ada": (D, 6 * D // N_DEV), "w_in": (D, IN_WIDTH // N_DEV), "gla_w_alpha": (GLR, GH * GDK // N_DEV),
                "gla_w_o": (GH * GDV // N_DEV, D), "mla_w_uq": (MQR, MH * MQK // N_DEV),
                "mla_w_ukv": (MKVR, MH * (NOPE + MV) // N_DEV), "mla_w_o": (MH * MV // N_DEV, D), "w_out": (D // N_DEV, D),
                "mlp_w1": (D, FF // N_DEV), "mlp_w2": (FF // N_DEV, D)}
SMALL = ("b_ada", "norm1_g", "norm2_g", "b_merge", "gla_b_alpha", "gla_out_norm_g", "mla_q_lat_g", "mla_kv_lat_g",
         "mla_qn_g", "mla_kn_g")
SMALL_ROW = {"b_ada": 0, "norm1_g": 6, "norm2_g": 7, "b_merge": 8, "gla_b_alpha": 10, "gla_out_norm_g": 11,
             "mla_q_lat_g": 12, "mla_kv_lat_g": 13, "mla_qn_g": 14, "mla_kn_g": 15}
WEIGHTS = ("w_ada", "b_ada", "norm1_g", "w_in", "b_merge", "gla_w_alpha", "gla_b_alpha", "gla_out_norm_g", "gla_w_o",
           "mla_q_lat_g", "mla_w_uq", "mla_kv_lat_g", "mla_w_ukv", "mla_qn_g", "mla_kn_g", "mla_w_o", "w_out",
           "norm2_g", "mlp_w1", "mlp_w2")


def _cparams(sem=None):
    return pltpu.CompilerParams(dimension_semantics=sem, vmem_limit_bytes=VMEM_LIMIT)


def _tile(n, pref):
    for t in (2048, 1024, 512, 256, 128):
        if t <= pref and n % t == 0:
            return t
    return n


def _dot(a, b, dims, precision=None):
    return lax.dot_general(a, b, (dims, ((), ())), preferred_element_type=F32, precision=precision)


NN = ((1,), (0,))
NT = ((1,), (1,))
TN = ((0,), (0,))


def _sigmoid(x):
    return 1.0 / (1.0 + jnp.exp(-x))


def _silu(x):
    return x * _sigmoid(x)


def _mm(a, b, mode, out_dtypes, name, *, pro=None, pro_b=None, epi=None, extras=(), a_off=0, m=None, tm=1024, tn=512,
        tk=1024):
    if mode == "tn":
        kc, n = b.shape
        m = a.shape[1] if m is None else m
    elif mode == "nn":
        m, kc = a.shape
        n = b.shape[1]
    else:
        m, kc = a.shape
        n = b.shape[0]
    tm, tn, tk = _tile(m, tm), _tile(n, tn), _tile(kc, tk)
    nk = kc // tk
    dims = {"nn": NN, "nt": NT, "tn": TN}[mode]
    if mode == "tn":
        a_spec = pl.BlockSpec((tk, tm), lambda i, j, k: (k, i + a_off))
    else:
        a_spec = pl.BlockSpec((tm, tk), lambda i, j, k: (i + a_off, k))
    if mode == "nt":
        b_spec = pl.BlockSpec((tn, tk), lambda i, j, k: (j, k))
    else:
        b_spec = pl.BlockSpec((tk, tn), lambda i, j, k: (k, j))
    o_spec = pl.BlockSpec((tm, tn), lambda i, j, k: (i, j))
    n_ex, n_out = len(extras), len(out_dtypes)

    def body(a_ref, b_ref, *rest):
        ex, outs, acc = rest[:n_ex], rest[n_ex:n_ex + n_out], rest[-1]
        k = pl.program_id(2)

        @pl.when(k == 0)
        def _():
            acc[...] = jnp.zeros_like(acc)

        av = a_ref[...]
        if pro is not None:
            av = pro(av)
        bv = b_ref[...]
        if pro_b is not None:
            bv = pro_b(bv)
        acc[...] += _dot(av.astype(BF16), bv.astype(BF16), dims)

        @pl.when(k == nk - 1)
        def _():
            res = (acc[...],) if epi is None else epi(acc[...], *[e[...] for e in ex])
            for o_ref, r in zip(outs, res):
                o_ref[...] = r.astype(o_ref.dtype)

    out = pl.pallas_call(
        body, name=name, grid=(m // tm, n // tn, nk),
        in_specs=[a_spec, b_spec] + [o_spec] * n_ex,
        out_specs=[o_spec] * n_out,
        out_shape=[jax.ShapeDtypeStruct((m, n), dt) for dt in out_dtypes],
        scratch_shapes=[pltpu.VMEM((tm, tn), F32)],
        compiler_params=_cparams(("parallel", "parallel", "arbitrary")),
    )(a, b, *extras)
    return out[0] if n_out == 1 else out


def _rows(s):
    return _tile(s, 512)


def _mod_spec():
    return pl.BlockSpec((1, 6, D), lambda b, i: (b, 0, 0))


def _tok_spec(tr, nb, width=D, col=0):
    return pl.BlockSpec((tr, width), lambda b, i: (b * nb + i, col))


def _norm_mod_fwd(x, g, mod3, i_shift, i_scale, name, mixed=None, i_gate=None):
    bsz, _, _ = mod3.shape
    t = x.shape[0]
    s = t // bsz
    tr = _rows(s)
    nb = s // tr
    has_res = mixed is not None

    def body(*refs):
        if has_res:
            x_ref, mx_ref, g_ref, mod_ref, x1_ref, h_ref = refs
            xv = x_ref[...] + mod_ref[0, i_gate:i_gate + 1, :] * mx_ref[...]
            x1_ref[...] = xv
        else:
            x_ref, g_ref, mod_ref, h_ref = refs
            xv = x_ref[...]
        r = lax.rsqrt(jnp.mean(xv * xv, axis=1, keepdims=True) + EPS)
        hn = (xv * r) * g_ref[...]
        h = hn * (1.0 + mod_ref[0, i_scale:i_scale + 1, :]) + mod_ref[0, i_shift:i_shift + 1, :]
        h_ref[...] = h.astype(BF16)

    tok = _tok_spec(tr, nb)
    gspec = pl.BlockSpec((1, D), lambda b, i: (0, 0))
    ins = [x] + ([mixed] if has_res else []) + [g, mod3]
    in_specs = [tok] + ([tok] if has_res else []) + [gspec, _mod_spec()]
    out_shape = ([jax.ShapeDtypeStruct((t, D), F32)] if has_res else []) + [jax.ShapeDtypeStruct((t, D), BF16)]
    out = pl.pallas_call(
        body, name=name, grid=(bsz, nb), in_specs=in_specs, out_specs=[tok] * len(out_shape), out_shape=out_shape,
        compiler_params=_cparams(("arbitrary", "arbitrary")),
    )(*ins)
    return (out[0], out[1]) if has_res else (None, out[0])


def _norm_mod_bwd(x, dh, dres, g, mod3, i_shift, i_scale, name, mixed=None, i_gate=None):
    bsz = mod3.shape[0]
    t = x.shape[0]
    s = t // bsz
    tr = _rows(s)
    nb = s // tr
    has_res = mixed is not None

    def body(*refs):
        if has_res:
            x_ref, dh_ref, dres_ref, mx_ref, g_ref, mod_ref, dx_ref, dmx_ref, accb, accg = refs
        else:
            x_ref, dh_ref, dres_ref, g_ref, mod_ref, dx_ref, accb, accg = refs
        b, i = pl.program_id(0), pl.program_id(1)

        @pl.when(i == 0)
        def _():
            accb[...] = jnp.zeros_like(accb)

        @pl.when((i == 0) & (b == 0))
        def _():
            accg[...] = jnp.zeros_like(accg)

        xv, dhv, gv = x_ref[...], dh_ref[...], g_ref[...]
        r = lax.rsqrt(jnp.mean(xv * xv, axis=1, keepdims=True) + EPS)
        xn = xv * r
        accb[0, 0:1, :] += jnp.sum(dhv, axis=0, keepdims=True)
        accb[0, 1:2, :] += jnp.sum(dhv * (xn * gv), axis=0, keepdims=True)
        tt = dhv * (1.0 + mod_ref[0, i_scale:i_scale + 1, :])
        accg[0:1, :] += jnp.sum(tt * xn, axis=0, keepdims=True)
        dxn = tt * gv
        dx = dres_ref[...] + r * (dxn - xn * jnp.mean(dxn * xn, axis=1, keepdims=True))
        dx_ref[...] = dx
        if has_res:
            accb[0, 2:3, :] += jnp.sum(dx * mx_ref[...], axis=0, keepdims=True)
            dmx_ref[...] = (dx * mod_ref[0, i_gate:i_gate + 1, :]).astype(BF16)

    tok = _tok_spec(tr, nb)
    gspec = pl.BlockSpec((1, D), lambda b, i: (0, 0))
    ins = [x, dh, dres] + ([mixed] if has_res else []) + [g, mod3]
    in_specs = [tok] * (4 if has_res else 3) + [gspec, _mod_spec()]
    out_shape = [jax.ShapeDtypeStruct((t, D), F32)] + ([jax.ShapeDtypeStruct((t, D), BF16)] if has_res else [])
    out_specs = [tok] * len(out_shape)
    out_shape += [jax.ShapeDtypeStruct((bsz, 8, D), F32), jax.ShapeDtypeStruct((8, D), F32)]
    out_specs += [pl.BlockSpec((1, 8, D), lambda b, i: (b, 0, 0)), pl.BlockSpec((8, D), lambda b, i: (0, 0))]
    return pl.pallas_call(
        body, name=name, grid=(bsz, nb), in_specs=in_specs, out_specs=out_specs, out_shape=out_shape,
        compiler_params=_cparams(("arbitrary", "arbitrary")),
    )(*ins)


def _loss_head(x1, ff, tgt, mod3):
    bsz = mod3.shape[0]
    t = x1.shape[0]
    s = t // bsz
    tr = _rows(s)
    nb = s // tr

    def body(x1_ref, ff_ref, tg_ref, mod_ref, dy_ref, dff_ref, accb, accl):
        b, i = pl.program_id(0), pl.program_id(1)

        @pl.when(i == 0)
        def _():
            accb[...] = jnp.zeros_like(accb)

        @pl.when((i == 0) & (b == 0))
        def _():
            accl[...] = jnp.zeros_like(accl)

        gate = mod_ref[0, 5:6, :]
        ffv = ff_ref[...]
        err = x1_ref[...] + gate * ffv - tg_ref[...]
        accl[0:1, :] += jnp.sum(err * err, axis=0, keepdims=True) * (0.5 / D)
        dy = err * (1.0 / D)
        dy_ref[...] = dy
        dff_ref[...] = (dy * gate).astype(BF16)
        accb[0, 0:1, :] += jnp.sum(dy * ffv, axis=0, keepdims=True)

    tok = _tok_spec(tr, nb)
    return pl.pallas_call(
        body, name="loss_head", grid=(bsz, nb), in_specs=[tok, tok, tok, _mod_spec()],
        out_specs=[tok, tok, pl.BlockSpec((1, 8, D), lambda b, i: (b, 0, 0)), pl.BlockSpec((8, D), lambda b, i: (0, 0))],
        out_shape=[jax.ShapeDtypeStruct((t, D), F32), jax.ShapeDtypeStruct((t, D), BF16),
                   jax.ShapeDtypeStruct((bsz, 8, D), F32), jax.ShapeDtypeStruct((8, D), F32)],
        compiler_params=_cparams(("arbitrary", "arbitrary")),
    )(x1, ff, tgt, mod3)


def _merge_fwd(proj, b_merge, y_a, y_b):
    t = proj.shape[0]
    tr = _tile(t, 512)

    def body(la_ref, lb_ref, bm_ref, ya_ref, yb_ref, mix_ref):
        ga = _sigmoid(la_ref[...] + bm_ref[:, 0:D])
        gb = _sigmoid(lb_ref[...] + bm_ref[:, D:2 * D])
        mix_ref[...] = (ga * ya_ref[...].astype(F32) + gb * yb_ref[...].astype(F32)).astype(BF16)

    tok = pl.BlockSpec((tr, D), lambda i: (i, 0))
    return pl.pallas_call(
        body, name="merge_fwd", grid=(t // tr,),
        in_specs=[pl.BlockSpec((tr, D), lambda i: (i, 3)), pl.BlockSpec((tr, D), lambda i: (i, 4)),
                  pl.BlockSpec((1, 2 * D), lambda i: (0, 0)), tok, tok],
        out_specs=tok, out_shape=jax.ShapeDtypeStruct((t, D), BF16),
        compiler_params=_cparams(("arbitrary",)),
    )(proj, proj, b_merge, y_a, y_b)


def _merge_bwd(dmix, proj, b_merge, y_a, y_b):
    t = proj.shape[0]
    tr = _tile(t, 512)

    def body(dm_ref, la_ref, lb_ref, bm_ref, ya_ref, yb_ref, dya_ref, dyb_ref, dl_ref, acc):
        @pl.when(pl.program_id(0) == 0)
        def _():
            acc[...] = jnp.zeros_like(acc)

        dm = dm_ref[...].astype(F32)
        ga = _sigmoid(la_ref[...] + bm_ref[:, 0:D])
        gb = _sigmoid(lb_ref[...] + bm_ref[:, D:2 * D])
        dya_ref[...] = (dm * ga).astype(BF16)
        dyb_ref[...] = (dm * gb).astype(BF16)
        dla = dm * ya_ref[...].astype(F32) * ga * (1.0 - ga)
        dlb = dm * yb_ref[...].astype(F32) * gb * (1.0 - gb)
        dl_ref[:, 0:D] = dla.astype(BF16)
        dl_ref[:, D:2 * D] = dlb.astype(BF16)
        acc[0:1, 0:D] += jnp.sum(dla, axis=0, keepdims=True)
        acc[0:1, D:2 * D] += jnp.sum(dlb, axis=0, keepdims=True)

    tok = pl.BlockSpec((tr, D), lambda i: (i, 0))
    return pl.pallas_call(
        body, name="merge_bwd", grid=(t // tr,),
        in_specs=[tok, pl.BlockSpec((tr, D), lambda i: (i, 3)), pl.BlockSpec((tr, D), lambda i: (i, 4)),
                  pl.BlockSpec((1, 2 * D), lambda i: (0, 0)), tok, tok],
        out_specs=[tok, tok, pl.BlockSpec((tr, 2 * D), lambda i: (i, 0)), pl.BlockSpec((8, 2 * D), lambda i: (0, 0))],
        out_shape=[jax.ShapeDtypeStruct((t, D), BF16), jax.ShapeDtypeStruct((t, D), BF16),
                   jax.ShapeDtypeStruct((t, 2 * D), BF16), jax.ShapeDtypeStruct((8, 2 * D), F32)],
        compiler_params=_cparams(("arbitrary",)),
    )(dmix, proj, proj, b_merge, y_a, y_b)


def _log_sigmoid(z):
    return jnp.minimum(z, 0.0) - jnp.log(1.0 + jnp.exp(-jnp.abs(z)))


def _tri(lower):
    r = lax.broadcasted_iota(jnp.int32, (CHUNK, CHUNK), 0)
    c = lax.broadcasted_iota(jnp.int32, (CHUNK, CHUNK), 1)
    return jnp.where(r >= c if lower else r <= c, 1.0, 0.0).astype(F32)


def _gla_fwd(proj, wa_pad, b_alpha, g_out, bsz):
    t = proj.shape[0]
    s = t // bsz
    nc = s // CHUNK

    def body(q_ref, k_ref, v_ref, gg_ref, ms_ref, wa_ref, ba_ref, go_ref, o_ref, og_ref, st_ref, la, state):
        z = _dot(ms_ref[...].astype(BF16), wa_ref[...], NN) + ba_ref[...]
        la[...] = _log_sigmoid(z) * (1.0 / GTAU)
        state[...] = jnp.zeros_like(state)
        low = _tri(True)
        gout = go_ref[...]

        def chunk(n, carry):
            rows = pl.ds(pl.multiple_of(n * CHUNK, CHUNK), CHUNK)
            lac = la[rows, :]
            cum = _dot(low, lac, NN, lax.Precision.HIGHEST)
            ce = jnp.sum(lac, axis=0, keepdims=True)
            kd = (k_ref[rows, :].astype(F32) * jnp.exp(ce - cum)).astype(BF16)
            new = state[...] * jnp.exp(ce) + _dot(v_ref[rows, :].astype(BF16), kd, TN)
            state[...] = new
            st_ref[pl.ds(pl.multiple_of(n * GDV, GDV), GDV), :] = new
            qs = (q_ref[rows, :].astype(F32) * (GDK ** -0.5)).astype(BF16)
            o = _dot(qs, new.astype(BF16), NT)
            o_ref[rows, :] = o
            ro = lax.rsqrt(jnp.mean(o * o, axis=1, keepdims=True) + EPS)
            og_ref[rows, :] = (((o * ro) * gout) * _silu(gg_ref[rows, :].astype(F32))).astype(BF16)
            return carry

        lax.fori_loop(0, nc, chunk, 0, unroll=8)

    hk = pl.BlockSpec((s, GDK), lambda b, h: (b, h))
    return pl.pallas_call(
        body, name="gla_fwd", grid=(bsz, GH),
        in_specs=[hk, pl.BlockSpec((s, GDK), lambda b, h: (b, GH + h)), pl.BlockSpec((s, GDV), lambda b, h: (b, 4 + h)),
                  pl.BlockSpec((s, GDV), lambda b, h: (b, 8 + h)), pl.BlockSpec((s, LANES), lambda b, h: (b, 43)),
                  pl.BlockSpec((LANES, GDK), lambda b, h: (0, h)), pl.BlockSpec((1, GDK), lambda b, h: (0, h)),
                  pl.BlockSpec((1, GDV), lambda b, h: (0, 0))],
        out_specs=[pl.BlockSpec((s, GDV), lambda b, h: (b, h)), pl.BlockSpec((s, GDV), lambda b, h: (b, h)),
                   pl.BlockSpec((nc * GDV, GDK), lambda b, h: (b * GH + h, 0))],
        out_shape=[jax.ShapeDtypeStruct((t, GH * GDV), F32), jax.ShapeDtypeStruct((t, GH * GDV), BF16),
                   jax.ShapeDtypeStruct((bsz * GH * nc * GDV, GDK), F32)],
        scratch_shapes=[pltpu.VMEM((s, GDK), F32), pltpu.VMEM((GDV, GDK), F32)],
        compiler_params=_cparams(("arbitrary", "arbitrary")),
    )(proj, proj, proj, proj, proj, wa_pad, b_alpha, g_out)


def _gla_bwd(dog, o, proj, wa_pad, b_alpha, g_out, states, bsz):
    t = proj.shape[0]
    s = t // bsz
    nc = s // CHUNK

    def body(dog_ref, o_ref, q_ref, k_ref, v_ref, gg_ref, ms_ref, wa_ref, ba_ref, go_ref, st_ref,
             dq_ref, dk_ref, dv_ref, dgg_ref, dz_ref, dba, dgo, zs, la, carry_g):
        @pl.when(pl.program_id(1) == 0)
        def _():
            dba[...] = jnp.zeros_like(dba)
            dgo[...] = jnp.zeros_like(dgo)

        z = _dot(ms_ref[...].astype(BF16), wa_ref[...], NN) + ba_ref[...]
        zs[...] = z
        la[...] = _log_sigmoid(z) * (1.0 / GTAU)
        carry_g[...] = jnp.zeros_like(carry_g)
        low, upp = _tri(True), _tri(False)
        gout = go_ref[...]
        last_row = lax.broadcasted_iota(jnp.int32, (CHUNK, GDK), 0) == CHUNK - 1

        def chunk(step, carry):
            n = nc - 1 - step
            rows = pl.ds(pl.multiple_of(n * CHUNK, CHUNK), CHUNK)
            lac = la[rows, :]
            cum = _dot(low, lac, NN, lax.Precision.HIGHEST)
            ce = jnp.sum(lac, axis=0, keepdims=True)
            e = jnp.exp(ce - cum)
            dec = jnp.exp(ce)
            kf = k_ref[rows, :].astype(F32)
            kd = (kf * e).astype(BF16)
            vv = v_ref[rows, :].astype(BF16)
            qs = (q_ref[rows, :].astype(F32) * (GDK ** -0.5)).astype(BF16)
            ov = o_ref[rows, :]
            ro = lax.rsqrt(jnp.mean(ov * ov, axis=1, keepdims=True) + EPS)
            on = ov * ro
            gg = gg_ref[rows, :].astype(F32)
            sg = _sigmoid(gg)
            dogv = dog_ref[rows, :].astype(F32)
            dgg_ref[rows, :] = (dogv * (on * gout) * (sg * (1.0 + gg * (1.0 - sg)))).astype(BF16)
            t1 = dogv * (gg * sg)
            dgo[0:1, :] += jnp.sum(t1 * on, axis=0, keepdims=True)
            don = t1 * gout
            do = ro * (don - on * jnp.mean(don * on, axis=1, keepdims=True))
            dob = do.astype(BF16)
            st_n = st_ref[pl.ds(pl.multiple_of(n * GDV, GDV), GDV), :]
            dq_ref[rows, :] = (_dot(dob, st_n.astype(BF16), NN) * (GDK ** -0.5)).astype(BF16)
            dn = carry_g[...] + _dot(dob, qs, TN)
            prev = jnp.maximum(n - 1, 0)
            st_p = st_ref[pl.ds(pl.multiple_of(prev * GDV, GDV), GDV), :] * jnp.where(n > 0, 1.0, 0.0)
            ddec = jnp.sum(dn * st_p, axis=0, keepdims=True)
            dnb = dn.astype(BF16)
            dkd = _dot(vv, dnb, NN)
            dv_ref[rows, :] = _dot(kd, dnb, NT).astype(BF16)
            dk_ref[rows, :] = (dkd * e).astype(BF16)
            w = dkd * kf * e
            dce = jnp.sum(w, axis=0, keepdims=True) + ddec * dec
            dcum = jnp.where(last_row, dce - w, -w)
            dla = _dot(upp, dcum, NN, lax.Precision.HIGHEST)
            dz = dla * (1.0 / GTAU) * _sigmoid(-zs[rows, :])
            dba[0:1, :] += jnp.sum(dz, axis=0, keepdims=True)
            dz_ref[rows, :] = dz.astype(BF16)
            carry_g[...] = dn * dec
            return carry

        lax.fori_loop(0, nc, chunk, 0, unroll=8)

    hv = pl.BlockSpec((s, GDV), lambda h, b: (b, h))
    hk = pl.BlockSpec((s, GDK), lambda h, b: (b, h))
    return pl.pallas_call(
        body, name="gla_bwd", grid=(GH, bsz),
        in_specs=[hv, hv, hk, pl.BlockSpec((s, GDK), lambda h, b: (b, GH + h)),
                  pl.BlockSpec((s, GDV), lambda h, b: (b, 4 + h)), pl.BlockSpec((s, GDV), lambda h, b: (b, 8 + h)),
                  pl.BlockSpec((s, LANES), lambda h, b: (b, 43)), pl.BlockSpec((LANES, GDK), lambda h, b: (0, h)),
                  pl.BlockSpec((1, GDK), lambda h, b: (0, h)), pl.BlockSpec((1, GDV), lambda h, b: (0, 0)),
                  pl.BlockSpec((nc * GDV, GDK), lambda h, b: (b * GH + h, 0))],
        out_specs=[hk, hk, hv, hv, hk, pl.BlockSpec((8, GDK), lambda h, b: (0, h)),
                   pl.BlockSpec((8, GDV), lambda h, b: (h, 0))],
        out_shape=[jax.ShapeDtypeStruct((t, GH * GDK), BF16), jax.ShapeDtypeStruct((t, GH * GDK), BF16),
                   jax.ShapeDtypeStruct((t, GH * GDV), BF16), jax.ShapeDtypeStruct((t, GH * GDV), BF16),
                   jax.ShapeDtypeStruct((t, GH * GDK), BF16), jax.ShapeDtypeStruct((8, GH * GDK), F32),
                   jax.ShapeDtypeStruct((8 * GH, GDV), F32)],
        scratch_shapes=[pltpu.VMEM((s, GDK), F32), pltpu.VMEM((s, GDK), F32), pltpu.VMEM((GDV, GDK), F32)],
        compiler_params=_cparams(("arbitrary", "arbitrary")),
    )(dog, o, proj, proj, proj, proj, proj, wa_pad, b_alpha, g_out, states)


def _rope_tables(pos_ref, fr_ref, sg_ref):
    ang = pos_ref[...].astype(F32) * fr_ref[...]
    return jnp.cos(ang), jnp.sin(ang) * sg_ref[...]


def _partner(x):
    lane = lax.broadcasted_iota(jnp.int32, x.shape, 1)
    return jnp.where(lane < NOPE + ROPE // 2, pltpu.roll(x, LANES - ROPE // 2, 1), pltpu.roll(x, ROPE // 2, 1))


def _mla_rows(t):
    return _tile(t, 256)


def _mla_pre_fwd(proj, pos, fr, sg, q_lat_g, kv_lat_g, qn_g, kn_g, wuq, wukv):
    t = proj.shape[0]
    tr = _mla_rows(t)

    def body(cq_ref, ckv_ref, ms_ref, pos_ref, fr_ref, sg_ref, qlg, kvlg, qng, kng, wuq_ref, wukv_ref, q_out, k_out, v_out):
        cos, sin = _rope_tables(pos_ref, fr_ref, sg_ref)
        cq = cq_ref[...].astype(F32)
        cqn = (cq * lax.rsqrt(jnp.mean(cq * cq, axis=1, keepdims=True) + EPS) * qlg[...]).astype(BF16)
        ckv = ckv_ref[...].astype(F32)
        ckvn = (ckv * lax.rsqrt(jnp.mean(ckv * ckv, axis=1, keepdims=True) + EPS) * kvlg[...]).astype(BF16)
        lane = lax.broadcasted_iota(jnp.int32, (tr, HP), 1)
        kpe = jnp.where((lane >= NOPE) & (lane < MQK), ms_ref[...].astype(F32), 0.0)
        lane_all = lax.broadcasted_iota(jnp.int32, (tr, MH * HP), 1)
        v_out[...] = jnp.where(lane_all % HP == MV, 1.0, _dot(ckvn, wukv_ref[:, MH * HP:], NN)).astype(BF16)
        for h in range(MH):
            cols = slice(h * HP, (h + 1) * HP)
            qh = _dot(cqn, wuq_ref[:, cols], NN)
            qn = qh * lax.rsqrt(jnp.sum(qh * qh, axis=1, keepdims=True) * (1.0 / MQK) + EPS) * qng[...]
            q_out[:, cols] = (qn * cos + _partner(qn) * sin).astype(BF16)
            kh = _dot(ckvn, wukv_ref[:, cols], NN) + kpe
            kn = kh * lax.rsqrt(jnp.sum(kh * kh, axis=1, keepdims=True) * (1.0 / MQK) + EPS) * kng[...]
            k_out[:, cols] = (kn * cos + _partner(kn) * sin).astype(BF16)

    def full(a):
        return pl.BlockSpec(a.shape, lambda i: (0, 0))

    wide = pl.BlockSpec((tr, MH * HP), lambda i: (i, 0))
    return pl.pallas_call(
        body, name="mla_pre_fwd", grid=(t // tr,),
        in_specs=[pl.BlockSpec((tr, MQR), lambda i: (i, 20)), pl.BlockSpec((tr, MKVR), lambda i: (i, 42)),
                  pl.BlockSpec((tr, LANES), lambda i: (i, 43)), pl.BlockSpec((tr, 1), lambda i: (i, 0)),
                  full(fr), full(sg), full(q_lat_g), full(kv_lat_g), full(qn_g), full(kn_g), full(wuq), full(wukv)],
        out_specs=[wide, wide, wide],
        out_shape=[jax.ShapeDtypeStruct((t, MH * HP), BF16)] * 3,
        compiler_params=_cparams(("arbitrary",)),
    )(proj, proj, proj, pos, fr, sg, q_lat_g, kv_lat_g, qn_g, kn_g, wuq, wukv)


def _mla_pre_bwd(dq2, dk2, dv2, dmisc_gla, proj, pos, fr, sg, q_lat_g, kv_lat_g, qn_g, kn_g, wuq, wukv):
    t = proj.shape[0]
    tr = _mla_rows(t)

    def body(dq_ref, dk_ref, dv_ref, dmg_ref, cq_ref, ckv_ref, ms_ref, pos_ref, fr_ref, sg_ref, qlg, kvlg, qng, kng,
             wuq_ref, wukv_ref, dcq_ref, dckv_ref, dms_ref, dwuq, dwukv, acc, dqf, dkvf):
        @pl.when(pl.program_id(0) == 0)
        def _():
            dwuq[...] = jnp.zeros_like(dwuq)
            dwukv[...] = jnp.zeros_like(dwukv)
            acc[...] = jnp.zeros_like(acc)

        cos, sin = _rope_tables(pos_ref, fr_ref, sg_ref)
        cq = cq_ref[...].astype(F32)
        rc = lax.rsqrt(jnp.mean(cq * cq, axis=1, keepdims=True) + EPS)
        xc = cq * rc
        cqn = (xc * qlg[...]).astype(BF16)
        ckv = ckv_ref[...].astype(F32)
        rkv = lax.rsqrt(jnp.mean(ckv * ckv, axis=1, keepdims=True) + EPS)
        xkv = ckv * rkv
        ckvn = (xkv * kvlg[...]).astype(BF16)
        lane = lax.broadcasted_iota(jnp.int32, (tr, HP), 1)
        is_rope = (lane >= NOPE) & (lane < MQK)
        kpe = jnp.where(is_rope, ms_ref[...].astype(F32), 0.0)
        dkpe = jnp.zeros((tr, HP), F32)
        dqng = jnp.zeros((1, HP), F32)
        dkng = jnp.zeros((1, HP), F32)
        for h in range(MH):
            cols = slice(h * HP, (h + 1) * HP)
            qh = _dot(cqn, wuq_ref[:, cols], NN)
            rq = lax.rsqrt(jnp.sum(qh * qh, axis=1, keepdims=True) * (1.0 / MQK) + EPS)
            xq = qh * rq
            dy = dq_ref[:, cols].astype(F32)
            dqn = dy * cos - _partner(dy) * sin
            dqng += jnp.sum(dqn * xq, axis=0, keepdims=True)
            tq = dqn * qng[...]
            dqf[:, cols] = (rq * (tq - xq * (jnp.sum(tq * xq, axis=1, keepdims=True) * (1.0 / MQK)))).astype(BF16)
            kh = _dot(ckvn, wukv_ref[:, cols], NN) + kpe
            rk = lax.rsqrt(jnp.sum(kh * kh, axis=1, keepdims=True) * (1.0 / MQK) + EPS)
            xk = kh * rk
            dy = dk_ref[:, cols].astype(F32)
            dkn = dy * cos - _partner(dy) * sin
            dkng += jnp.sum(dkn * xk, axis=0, keepdims=True)
            tk = dkn * kng[...]
            dkh = rk * (tk - xk * (jnp.sum(tk * xk, axis=1, keepdims=True) * (1.0 / MQK)))
            dkvf[:, cols] = jnp.where(lane < NOPE, dkh, 0.0).astype(BF16)
            dkpe += jnp.where(is_rope, dkh, 0.0)
        dkvf[:, MH * HP:] = dv_ref[...]
        acc[2:3, 0:HP] += dqng
        acc[3:4, 0:HP] += dkng
        dms_ref[...] = (dmg_ref[...] + dkpe).astype(BF16)

        dqfv = dqf[...]
        dwuq[...] += _dot(cqn, dqfv, TN)
        dcqn = _dot(dqfv, wuq_ref[...], NT)
        acc[0:1, :] += jnp.sum(dcqn * xc, axis=0, keepdims=True)
        tc = dcqn * qlg[...]
        dcq_ref[...] = (rc * (tc - xc * jnp.mean(tc * xc, axis=1, keepdims=True))).astype(BF16)

        dkvfv = dkvf[...]
        dwukv[...] += _dot(ckvn, dkvfv, TN)
        dckvn = _dot(dkvfv, wukv_ref[...], NT)
        acc[1:2, 0:MKVR] += jnp.sum(dckvn * xkv, axis=0, keepdims=True)
        tkv = dckvn * kvlg[...]
        dckv_ref[...] = (rkv * (tkv - xkv * jnp.mean(tkv * xkv, axis=1, keepdims=True))).astype(BF16)

    def full(a):
        return pl.BlockSpec(a.shape, lambda i: (0, 0))

    wide = pl.BlockSpec((tr, MH * HP), lambda i: (i, 0))
    narrow = pl.BlockSpec((tr, LANES), lambda i: (i, 0))
    return pl.pallas_call(
        body, name="mla_pre_bwd", grid=(t // tr,),
        in_specs=[wide, wide, wide, narrow,
                  pl.BlockSpec((tr, MQR), lambda i: (i, 20)), pl.BlockSpec((tr, MKVR), lambda i: (i, 42)),
                  pl.BlockSpec((tr, LANES), lambda i: (i, 43)), pl.BlockSpec((tr, 1), lambda i: (i, 0)),
                  full(fr), full(sg), full(q_lat_g), full(kv_lat_g), full(qn_g), full(kn_g), full(wuq), full(wukv)],
        out_specs=[pl.BlockSpec((tr, MQR), lambda i: (i, 0)), narrow, narrow,
                   pl.BlockSpec((MQR, MH * HP), lambda i: (0, 0)), pl.BlockSpec((MKVR, 2 * MH * HP), lambda i: (0, 0)),
                   pl.BlockSpec((8, MQR), lambda i: (0, 0))],
        out_shape=[jax.ShapeDtypeStruct((t, MQR), BF16), jax.ShapeDtypeStruct((t, MKVR), BF16),
                   jax.ShapeDtypeStruct((t, LANES), BF16), jax.ShapeDtypeStruct((MQR, MH * HP), F32),
                   jax.ShapeDtypeStruct((MKVR, 2 * MH * HP), F32), jax.ShapeDtypeStruct((8, MQR), F32)],
        scratch_shapes=[pltpu.VMEM((tr, MH * HP), BF16), pltpu.VMEM((tr, 2 * MH * HP), BF16)],
        compiler_params=_cparams(("arbitrary",)),
    )(dq2, dk2, dv2, dmisc_gla, proj, proj, proj, pos, fr, sg, q_lat_g, kv_lat_g, qn_g, kn_g, wuq, wukv)


ATT_FWD_TILES = (1024, 512)
ATT_BWD_TILES = (512, 512)
ATT_HEADS = 2
NEG = -1e30
LOG2E = 1.4426950408889634


def _att_mask(q0, k0, tq, tk):
    qc = (q0 + lax.broadcasted_iota(jnp.int32, (tq, tk), 0)) // CHUNK
    kc = (k0 + lax.broadcasted_iota(jnp.int32, (tq, tk), 1)) // CHUNK
    return kc <= qc


def _att_tiles(s, tiles):
    return _tile(s, tiles[0]), _tile(s, tiles[1])


def _lanes(x, n):
    return x if n == 1 else jnp.concatenate([x] * n, axis=1)


def _grid_ends(grid):
    i, j = pl.program_id(0), pl.program_id(1)
    return (i == 0) & (j == 0), (i == grid[0] - 1) & (j == grid[1] - 1)


def _attn_fwd(q2, k2, v2, bsz, slab):
    t = q2.shape[0]
    s = t // bsz
    tq, tk = _att_tiles(s, ATT_FWD_TILES)
    nq, groups, n_diag = s // tq, tk // HP, max(tq // tk, 1)
    scale = MQK ** -0.5
    c2 = scale * LOG2E
    heads = range(ATT_HEADS)

    def body(q_ref, k_ref, v_ref, slab_ref, o_ref, lse_ref, gath_ref, send_sems, recv_sems, local_sem):
        gather = _core_row_gather_copies(slab_ref, gath_ref, send_sems, recv_sems, local_sem)
        first, last = _grid_ends((bsz, MH // ATT_HEADS))
        pl.when(first)(gather.start)

        def q_loop(qi, carry):
            q0 = pl.multiple_of(qi * tq, tq)
            rows = pl.ds(q0, tq)
            n_full = q0 // tk
            qs = [q_ref[rows, h * HP:(h + 1) * HP] for h in heads]

            def scores(h, kj, masked):
                k0 = pl.multiple_of(kj * tk, tk)
                sc = _dot(qs[h], k_ref[pl.ds(k0, tk), h * HP:(h + 1) * HP], NT)
                return jnp.where(_att_mask(q0, k0, tq, tk), sc, NEG) if masked else sc

            def fold(mx, sc):
                for j in range(groups):
                    mx = jnp.maximum(mx, sc[:, j * HP:(j + 1) * HP])
                return mx

            mx = lax.fori_loop(0, n_full, lambda kj, mx: tuple(fold(mx[h], scores(h, kj, False)) for h in heads),
                               tuple(jnp.full((tq, HP), NEG, F32) for _ in heads))
            for u in range(n_diag):
                mx = tuple(fold(mx[h], scores(h, n_full + u, True)) for h in heads)
            mb = [jnp.broadcast_to(jnp.max(mx[h], axis=1, keepdims=True), (tq, HP)) for h in heads]

            def weighted(h, kj, masked):
                p = jnp.exp2((scores(h, kj, masked) - _lanes(mb[h], groups)) * c2)
                k0 = pl.multiple_of(kj * tk, tk)
                return _dot(p.astype(BF16), v_ref[pl.ds(k0, tk), h * HP:(h + 1) * HP], NN)

            acc = lax.fori_loop(0, n_full, lambda kj, acc: tuple(acc[h] + weighted(h, kj, False) for h in heads),
                                tuple(jnp.zeros((tq, HP), F32) for _ in heads))
            for u in range(n_diag):
                acc = tuple(acc[h] + weighted(h, n_full + u, True) for h in heads)
            lane = lax.broadcasted_iota(jnp.int32, (tq, HP), 1)
            for h in heads:
                a = acc[h]
                l = jnp.sum(jnp.where(lane == MV, a, 0.0), axis=1, keepdims=True)
                o_ref[rows, h * HP:(h + 1) * HP] = (a / l).astype(BF16)
                lse_ref[rows, h * HP:(h + 1) * HP] = mb[h] * scale + jnp.log(l)
            return carry

        lax.fori_loop(0, nq, q_loop, 0)
        pl.when(last)(gather.finish)

    spec = pl.BlockSpec((s, ATT_HEADS * HP), lambda b, h: (b, h))
    return pl.pallas_call(
        body, name="attn_fwd", grid=(bsz, MH // ATT_HEADS), in_specs=[spec] * 3 + [HBM_SPEC],
        out_specs=[spec, spec, HBM_SPEC],
        out_shape=[jax.ShapeDtypeStruct((t, MH * HP), BF16), jax.ShapeDtypeStruct((t, MH * HP), F32),
                   jax.ShapeDtypeStruct((N_DEV,) + slab.shape, slab.dtype)],
        scratch_shapes=EXCHANGE_SEMS, compiler_params=_cparams(("arbitrary", "arbitrary")),
    )(q2, k2, v2, slab)


def _attn_bwd(q2, k2, v2, do2, o2, lse2, bsz, tsum):
    t = q2.shape[0]
    s = t // bsz
    tq, tk = _att_tiles(s, ATT_BWD_TILES)
    nq, nk, per, groups = s // tq, s // tk, max(tk // tq, 1), tk // HP
    scale = MQK ** -0.5
    c2 = scale * LOG2E
    heads = range(ATT_HEADS)

    def body(q_ref, k_ref, v_ref, do_ref, o_ref, lse_ref, t_ref, dq_ref, dk_ref, dv_ref, parts_ref, dq_acc, delta, lse_b2,
             send_sems, recv_sems, local_sem):
        exchange = _chip_exchange_copies(t_ref, parts_ref, send_sems, recv_sems, local_sem)
        first, last = _grid_ends((bsz, MH // ATT_HEADS))
        pl.when(first)(exchange.start)
        dq_acc[...] = jnp.zeros_like(dq_acc)

        def d_loop(i, carry):
            rows = pl.ds(pl.multiple_of(i * tq, tq), tq)
            for h in heads:
                hs = slice(h * HP, (h + 1) * HP)
                dl = jnp.sum(do_ref[rows, hs].astype(F32) * o_ref[rows, hs].astype(F32), axis=1, keepdims=True)
                delta[rows, hs] = jnp.broadcast_to(dl, (tq, HP))
            lse_b2[rows, :] = lse_ref[rows, :] * LOG2E
            return carry

        lax.fori_loop(0, nq, d_loop, 0)

        def k_loop(kj, carry):
            k0 = pl.multiple_of(kj * tk, tk)
            kk = [k_ref[pl.ds(k0, tk), h * HP:(h + 1) * HP] for h in heads]
            vv = [v_ref[pl.ds(k0, tk), h * HP:(h + 1) * HP] for h in heads]

            def tile(qi, c, masked):
                q0 = pl.multiple_of(qi * tq, tq)
                rows = pl.ds(q0, tq)
                out = []
                for h in heads:
                    hs = slice(h * HP, (h + 1) * HP)
                    dk, dv = c[h]
                    q = q_ref[rows, hs]
                    do = do_ref[rows, hs]
                    e = _dot(q, kk[h], NT) * c2 - _lanes(lse_b2[rows, hs], groups)
                    if masked:
                        e = jnp.where(_att_mask(q0, k0, tq, tk), e, NEG)
                    p = jnp.exp2(e)
                    dv = dv + _dot(p.astype(BF16), do, TN)
                    ds = (p * (_dot(do, vv[h], NT) - _lanes(delta[rows, hs], groups))).astype(BF16)
                    dq_acc[rows, hs] += _dot(ds, kk[h], NN)
                    dk = dk + _dot(ds, q, TN)
                    out.append((dk, dv))
                return tuple(out)

            zero = jnp.zeros((tk, HP), F32)
            c = tuple((zero, zero) for _ in heads)
            first = k0 // tq
            for u in range(per):
                c = tile(first + u, c, True)
            c = lax.fori_loop(first + per, nq, lambda qi, c: tile(qi, c, False), c)
            for h in heads:
                dk_ref[pl.ds(k0, tk), h * HP:(h + 1) * HP] = (c[h][0] * scale).astype(BF16)
                dv_ref[pl.ds(k0, tk), h * HP:(h + 1) * HP] = c[h][1].astype(BF16)
            return carry

        lax.fori_loop(0, nk, k_loop, 0)
        dq_ref[...] = (dq_acc[...] * scale).astype(BF16)
        pl.when(last)(exchange.finish)

    spec = pl.BlockSpec((s, ATT_HEADS * HP), lambda b, h: (b, h))
    return pl.pallas_call(
        body, name="attn_bwd", grid=(bsz, MH // ATT_HEADS), in_specs=[spec] * 6 + [HBM_SPEC],
        out_specs=[spec] * 3 + [HBM_SPEC],
        out_shape=[jax.ShapeDtypeStruct((t, MH * HP), BF16)] * 3 + [jax.ShapeDtypeStruct(tsum.shape, tsum.dtype)],
        scratch_shapes=[pltpu.VMEM((s, ATT_HEADS * HP), F32)] * 3 + EXCHANGE_SEMS,
        compiler_params=_cparams(("arbitrary", "arbitrary")),
    )(q2, k2, v2, do2, o2, lse2, tsum)


def _perm_w_in_t(w):
    z = lambda n: jnp.zeros((n, w.shape[1]), w.dtype)
    return jnp.concatenate([w[:3072], w[3504:5552], w[3088:3344], w[3344:3472], w[3072:3088], z(48), w[3472:3504], z(32)],
                           axis=0)


def _unperm_w_in_t(g):
    return jnp.concatenate([g[:3072], g[5504:5520], g[5120:5376], g[5376:5504], g[5568:5600], g[3072:5120]], axis=0)


def _pad_wa(w):
    return jnp.pad(w, ((0, LANES - GLR), (0, 0)))


def _pad_wuq(w):
    return jnp.pad(w.reshape(MQR, MH, MQK), ((0, 0), (0, 0), (0, HP - MQK))).reshape(MQR, MH * HP)


def _unpad_wuq(g):
    return g.reshape(MQR, MH, HP)[:, :, :MQK].reshape(MQR, MH * MQK)


def _pad_wukv(w):
    w3 = w.reshape(MKVR, MH, NOPE + MV)
    kp = jnp.pad(w3[:, :, :NOPE], ((0, 0), (0, 0), (0, HP - NOPE))).reshape(MKVR, MH * HP)
    vp = jnp.pad(w3[:, :, NOPE:], ((0, 0), (0, 0), (0, HP - MV))).reshape(MKVR, MH * HP)
    return jnp.concatenate([kp, vp], axis=1)


def _unpad_wukv(g):
    kp = g[:, :MH * HP].reshape(MKVR, MH, HP)[:, :, :NOPE]
    vp = g[:, MH * HP:].reshape(MKVR, MH, HP)[:, :, :MV]
    return jnp.concatenate([kp, vp], axis=2).reshape(MKVR, MH * (NOPE + MV))


def _pad_wo(w):
    return jnp.pad(w.reshape(MH, MV, D), ((0, 0), (0, HP - MV), (0, 0))).reshape(MH * HP, D)


def _unpad_wo(g):
    return g.reshape(MH, HP, D)[:, :MV, :].reshape(MH * MV, D)


def _pad_lanes(v, n=HP):
    return jnp.pad(v, ((0, 0), (0, n - v.shape[1])))


def _local_step(x, positions, tgt, wt, slab_b, sp, mod3):
    bsz, s, _ = x.shape
    t = bsz * s
    x2 = x.reshape(t, D)
    tgt2 = tgt.reshape(t, D)
    pos = positions.reshape(t, 1)
    fr16 = ROPE_THETA ** (-jnp.arange(0, ROPE, 2, dtype=F32) / ROPE)
    zero = lambda n: jnp.zeros((n,), F32)
    fr = jnp.concatenate([zero(NOPE), fr16, fr16, zero(HP - MQK)]).reshape(1, HP)
    sg = jnp.concatenate([zero(NOPE), -jnp.ones((ROPE // 2,), F32), jnp.ones((ROPE // 2,), F32), zero(HP - MQK)]).reshape(1, HP)

    w_in_t = _perm_w_in_t(wt["w_in"])
    wa_pad = _pad_wa(wt["gla_w_alpha"].T)
    wuq = _pad_wuq(wt["mla_w_uq"].T)
    wukv = _pad_wukv(wt["mla_w_ukv"].T)
    qn_g, kn_g = _pad_lanes(sp["mla_qn_g"]), _pad_lanes(sp["mla_kn_g"])

    _, h = _norm_mod_fwd(x2, sp["norm1_g"], mod3, 0, 1, "norm1_fwd")
    proj = _mm(h, w_in_t, "nt", (BF16,), "proj_fwd")
    o_gla, og, states = _gla_fwd(proj, wa_pad, sp["gla_b_alpha"], sp["gla_out_norm_g"], bsz)
    q2, k2, v2 = _mla_pre_fwd(proj, pos, fr, sg, sp["mla_q_lat_g"], sp["mla_kv_lat_g"], qn_g, kn_g, wuq, wukv)
    o2, lse2, core_row = _attn_fwd(q2, k2, v2, bsz, slab_b)
    wt = dict(wt, **_unpack_gathered(_cross_core_fill(core_row), SLAB_B))
    wo_pad = _pad_wo(wt["mla_w_o"])
    y_a = _mm(og, wt["gla_w_o"], "nn", (BF16,), "gla_out_fwd")
    y_b = _mm(o2, wo_pad, "nn", (BF16,), "mla_out_fwd")
    mix = _merge_fwd(proj, sp["b_merge"], y_a, y_b)
    mixed = _mm(mix, wt["w_out"], "nn", (F32,), "w_out_fwd")

    x1, h2 = _norm_mod_fwd(x2, sp["norm2_g"], mod3, 3, 4, "norm2_fwd", mixed=mixed, i_gate=2)
    a, f = _mm(h2, wt["mlp_w1"], "nt", (BF16, BF16), "mlp1_fwd",
               epi=lambda acc: (acc, jnp.square(jnp.maximum(acc, 0.0))))
    ff = _mm(f, wt["mlp_w2"], "nn", (F32,), "mlp2_fwd")
    dy, dff, acc_g2, acc_loss = _loss_head(x1, ff, tgt2, mod3)

    gw = {}
    gw["mlp_w2"] = _mm(f, dff, "tn", (BF16,), "mlp2_dw")
    da = _mm(dff, wt["mlp_w2"], "nt", (BF16,), "mlp2_dx", extras=(a,),
             epi=lambda acc, av: (acc * (2.0 * jnp.maximum(av.astype(F32), 0.0)),))
    gw["mlp_w1"] = _mm(da, h2, "tn", (BF16,), "mlp1_dw")
    dh2 = _mm(da, wt["mlp_w1"], "nn", (F32,), "mlp1_dx")
    dx1, dmixed, accb2, accg2 = _norm_mod_bwd(x1, dh2, dy, sp["norm2_g"], mod3, 3, 4, "norm2_bwd", mixed=mixed, i_gate=2)

    gw["w_out"] = _mm(mix, dmixed, "tn", (BF16,), "w_out_dw")
    dmix = _mm(dmixed, wt["w_out"], "nt", (BF16,), "w_out_dx")
    dy_a, dy_b, dlogits, acc_bm = _merge_bwd(dmix, proj, sp["b_merge"], y_a, y_b)
    gw["gla_w_o"] = _mm(og, dy_a, "tn", (BF16,), "gla_out_dw")
    dog = _mm(dy_a, wt["gla_w_o"], "nt", (BF16,), "gla_out_dx")
    gw["mla_w_o"] = _unpad_wo(_mm(o2, dy_b, "tn", (BF16,), "mla_out_dw"))
    do2 = _mm(dy_b, wo_pad, "nt", (BF16,), "mla_out_dx")
    dq2, dk2, dv2, parts_b = _attn_bwd(q2, k2, v2, do2, o2, lse2, bsz, _sum_over_cores(_pack_per_device(gw, SLAB_B), "b"))
    dq_g, dk_g, dv_g, dgg, dz, acc_ba, acc_go = _gla_bwd(dog, o_gla, proj, wa_pad, sp["gla_b_alpha"],
                                                         sp["gla_out_norm_g"], states, bsz)
    gw["gla_w_alpha"] = _mm(proj, dz, "tn", (F32,), "gla_alpha_dw", a_off=43, m=LANES)[:GLR].T.astype(BF16)
    dmisc_gla = _mm(dz, wa_pad, "nt", (F32,), "gla_alpha_dx")
    dcq, dckv, dmisc, gwuq, gwukv, acc_mla = _mla_pre_bwd(dq2, dk2, dv2, dmisc_gla, proj, pos, fr, sg, sp["mla_q_lat_g"],
                                                         sp["mla_kv_lat_g"], qn_g, kn_g, wuq, wukv)
    gw["mla_w_uq"] = _unpad_wuq(gwuq).T.astype(BF16)
    gw["mla_w_ukv"] = _unpad_wukv(gwukv).T.astype(BF16)
    dproj = jnp.concatenate([dq_g, dk_g, dv_g, dgg, dlogits, dcq, dckv, dmisc], axis=1)
    gw["w_in"] = _unperm_w_in_t(_mm(dproj, h, "tn", (BF16,), "proj_dw"))
    dh = _mm(dproj, w_in_t, "nn", (F32,), "proj_dx")
    grad_x, accb1, accg1 = _norm_mod_bwd(x2, dh, dx1, sp["norm1_g"], mod3, 0, 1, "norm1_bwd")

    dmod = jnp.stack([accb1[:, 0], accb1[:, 1], accb2[:, 2], accb2[:, 0], accb2[:, 1], acc_g2[:, 0]], axis=1)

    rows = {
        "dmod": dmod.reshape(bsz * 6, D),
        "norm1_g": accg1[0:1], "norm2_g": accg2[0:1],
        "b_merge": acc_bm[0:1].reshape(2, D),
        "gla_b_alpha": _pad_lanes(acc_ba[0:1], D),
        "gla_out_norm_g": _pad_lanes(acc_go.reshape(GH, 8, GDV)[:, 0, :], D),
        "mla_q_lat_g": _pad_lanes(acc_mla[0:1], D), "mla_kv_lat_g": _pad_lanes(acc_mla[1:2], D),
        "mla_qn_g": _pad_lanes(acc_mla[2:3], D), "mla_kn_g": _pad_lanes(acc_mla[3:4], D),
        "loss": acc_loss[0:1],
    }
    return grad_x.reshape(bsz, s, D), gw, parts_b, rows


HBM_SPEC = pl.BlockSpec(memory_space=pltpu.HBM)


def _all_gather(p, name):
    r, cdim = p.shape

    def body(p_ref, out_ref, send_sems, recv_sems, local_sem):
        x, y, c = lax.axis_index("x"), lax.axis_index("y"), lax.axis_index("c")
        me, sibling = (x, y, c), (x, y, 1 - c)
        chips = [(1 - x, y), (x, 1 - y), (1 - x, 1 - y)]

        def slot(px, py, pc):
            return out_ref.at[4 * px + 2 * py + pc]

        def copy(k, block, to, src=None):
            return pltpu.make_async_remote_copy(
                src_ref=slot(*block) if src is None else src, dst_ref=slot(*block),
                send_sem=send_sems.at[k], recv_sem=recv_sems.at[k], device_id=to, device_id_type=MESH)

        mine = pltpu.make_async_copy(p_ref, slot(*me), local_sem)
        mine.start()
        first = [copy(0, me, sibling, src=p_ref)] + [copy(1 + j, me, (*chip, c), src=p_ref) for j, chip in enumerate(chips)]
        for cp in first:
            cp.start()
        passed = [copy(4 + j, (*chip, c), sibling) for j, chip in enumerate(chips)]
        for j, chip in enumerate(chips):
            copy(1 + j, (*chip, c), me).wait_recv()
            passed[j].start()
        copy(0, sibling, me).wait_recv()
        for j, chip in enumerate(chips):
            copy(4 + j, (*chip, 1 - c), me).wait_recv()
        for cp in first + passed:
            cp.wait_send()
        mine.wait()

    return pl.pallas_call(
        body, name=name, out_shape=jax.ShapeDtypeStruct((N_DEV, r, cdim), p.dtype),
        in_specs=[HBM_SPEC], out_specs=HBM_SPEC,
        scratch_shapes=[pltpu.SemaphoreType.DMA((7,)), pltpu.SemaphoreType.DMA((7,)), pltpu.SemaphoreType.DMA(())],
    )(p)


def _sibling_exchange(g, name):
    def body(g_ref, out_ref, send_sem, recv_sem):
        x, y, c = lax.axis_index("x"), lax.axis_index("y"), lax.axis_index("c")
        cp = pltpu.make_async_remote_copy(src_ref=g_ref, dst_ref=out_ref, send_sem=send_sem, recv_sem=recv_sem,
                                          device_id=(x, y, 1 - c), device_id_type=MESH)
        cp.start()
        cp.wait()

    return pl.pallas_call(
        body, name=name, out_shape=jax.ShapeDtypeStruct(g.shape, g.dtype),
        in_specs=[HBM_SPEC], out_specs=HBM_SPEC,
        scratch_shapes=[pltpu.SemaphoreType.DMA(()), pltpu.SemaphoreType.DMA(())],
    )(g)


class _Exchange:
    def __init__(self, local, sends, arrivals):
        self.local, self.sends, self.arrivals = local, sends, arrivals

    def start(self):
        self.local.start()
        for cp in self.sends:
            cp.start()

    def finish(self):
        for cp in self.arrivals:
            cp.wait_recv()
        for cp in self.sends:
            cp.wait_send()
        self.local.wait()


EXCHANGE_SEMS = [pltpu.SemaphoreType.DMA((4,)), pltpu.SemaphoreType.DMA((4,)), pltpu.SemaphoreType.DMA(())]


def _chip_exchange_copies(t_ref, out_ref, send_sems, recv_sems, local_sem):
    x, y, c = lax.axis_index("x"), lax.axis_index("y"), lax.axis_index("c")
    my_chip = 2 * x + y
    chips = [(1 - x, y), (x, 1 - y), (1 - x, 1 - y)]

    def copy(j, src, dst, px, py):
        return pltpu.make_async_remote_copy(src_ref=t_ref.at[src], dst_ref=out_ref.at[dst], send_sem=send_sems.at[j],
                                            recv_sem=recv_sems.at[j], device_id=(px, py, c), device_id_type=MESH)

    return _Exchange(pltpu.make_async_copy(t_ref.at[my_chip], out_ref.at[my_chip], local_sem),
                     [copy(j, 2 * px + py, my_chip, px, py) for j, (px, py) in enumerate(chips)],
                     [copy(j, my_chip, 2 * px + py, px, py) for j, (px, py) in enumerate(chips)])


def _core_row_gather_copies(p_ref, out_ref, send_sems, recv_sems, local_sem):
    x, y, c = lax.axis_index("x"), lax.axis_index("y"), lax.axis_index("c")
    peers = [(x, y, 1 - c), (1 - x, y, c), (x, 1 - y, c), (1 - x, 1 - y, c)]

    def slot(px, py, pc):
        return out_ref.at[4 * px + 2 * py + pc]

    def copy(j, block, to):
        return pltpu.make_async_remote_copy(src_ref=p_ref, dst_ref=slot(*block), send_sem=send_sems.at[j],
                                            recv_sem=recv_sems.at[j], device_id=to, device_id_type=MESH)

    return _Exchange(pltpu.make_async_copy(p_ref, slot(x, y, c), local_sem),
                     [copy(j, (x, y, c), peer) for j, peer in enumerate(peers)],
                     [copy(j, peer, peer) for j, peer in enumerate(peers)])


def _chip_exchange(tsum, name):
    def body(t_ref, out_ref, send_sems, recv_sems, local_sem):
        ex = _chip_exchange_copies(t_ref, out_ref, send_sems, recv_sems, local_sem)
        ex.start()
        ex.finish()

    return pl.pallas_call(
        body, name=name, out_shape=jax.ShapeDtypeStruct(tsum.shape, tsum.dtype),
        in_specs=[HBM_SPEC], out_specs=HBM_SPEC, scratch_shapes=EXCHANGE_SEMS,
    )(tsum)


def _cross_core_fill(gathered):
    def body(g_ref, out_ref, send_sems, recv_sems):
        x, y, c = lax.axis_index("x"), lax.axis_index("y"), lax.axis_index("c")
        chips = [(1 - x, y), (x, 1 - y), (1 - x, 1 - y)]

        def copy(j, pc):
            px, py = chips[j]
            slot = 4 * px + 2 * py + pc
            return pltpu.make_async_remote_copy(src_ref=g_ref.at[slot], dst_ref=out_ref.at[slot], send_sem=send_sems.at[j],
                                                recv_sem=recv_sems.at[j], device_id=(x, y, 1 - c), device_id_type=MESH)

        sends = [copy(j, c) for j in range(3)]
        for cp in sends:
            cp.start()
        for j in range(3):
            copy(j, 1 - c).wait_recv()
        for cp in sends:
            cp.wait_send()

    return pl.pallas_call(
        body, name="weights_cross_core_fill", out_shape=jax.ShapeDtypeStruct(gathered.shape, gathered.dtype),
        in_specs=[HBM_SPEC], out_specs=HBM_SPEC, input_output_aliases={0: 0},
        scratch_shapes=[pltpu.SemaphoreType.DMA((3,)), pltpu.SemaphoreType.DMA((3,))],
    )(gathered)


def _slab_block(r):
    return max(b for b in range(16, SLAB_BLOCK_MAX + 1, 16) if r % b == 0)


def _pair_sum(a, b, name):
    n, r, cdim = a.shape
    rb = _slab_block(r)
    blk = pl.BlockSpec((1, rb, cdim), lambda j, i: (j, i, 0))

    def body(a_ref, b_ref, o_ref):
        o_ref[...] = (a_ref[...].astype(F32) + b_ref[...].astype(F32)).astype(BF16)

    return pl.pallas_call(
        body, name=name, grid=(n, r // rb), in_specs=[blk, blk], out_specs=blk,
        out_shape=jax.ShapeDtypeStruct(a.shape, BF16), compiler_params=_cparams(("arbitrary", "arbitrary")),
    )(a, b)


def _adamw_math(w, g, m, v):
    m = ADAM_B1 * m + (1.0 - ADAM_B1) * g
    v = ADAM_B2 * v + (1.0 - ADAM_B2) * jnp.square(g)
    m_hat = m / (1.0 - ADAM_B1 ** ADAM_STEP)
    v_hat = v / (1.0 - ADAM_B2 ** ADAM_STEP)
    delta = -ADAM_LR * (m_hat / (jnp.sqrt(v_hat) + ADAM_EPS) + ADAM_WD * w)
    return delta, m, v


def _slab_sum(parts, name):
    _, r, cdim = parts.shape
    rb = _slab_block(r)
    blk = pl.BlockSpec((rb, cdim), lambda i: (i, 0))

    def body(p_ref, g_out):
        g = p_ref[0].astype(F32)
        for j in range(1, 4):
            g = g + p_ref[j].astype(F32)
        g_out[...] = g

    return pl.pallas_call(
        body, name=name, grid=(r // rb,),
        in_specs=[pl.BlockSpec((4, rb, cdim), lambda i: (0, i, 0))], out_specs=blk,
        out_shape=jax.ShapeDtypeStruct((r, cdim), F32), compiler_params=_cparams(("arbitrary",)),
    )(parts)


def _adamw(g, w, m, v, name):
    r, cdim = w.shape
    rb = _tile(r, 256)
    blk = pl.BlockSpec((rb, cdim), lambda i: (i, 0))

    def body(g_ref, w_ref, m_ref, v_ref, d_out, m_out, v_out):
        d_out[...], m_out[...], v_out[...] = _adamw_math(w_ref[...], g_ref[...], m_ref[...], v_ref[...])

    return pl.pallas_call(
        body, name=name, grid=(r // rb,), in_specs=[blk] * 4, out_specs=[blk] * 3,
        out_shape=[jax.ShapeDtypeStruct((r, cdim), F32)] * 3, compiler_params=_cparams(("arbitrary",)),
    )(g, w, m, v)


def _adamw_small(parts, w, m, v):
    def body(p_ref, w_ref, m_ref, v_ref, g_out, d_out, m_out, v_out, loss_out):
        def total(srcs):
            acc = None
            for r in srcs:
                for j in range(N_DEV):
                    term = p_ref[j, r:r + 1, :]
                    acc = term if acc is None else acc + term
            return acc

        for prow, srcs in enumerate(SMALL_SOURCES):
            one = slice(prow, prow + 1)
            g = total(srcs)
            g_out[one, :] = g
            d_out[one, :], m_out[one, :], v_out[one, :] = _adamw_math(w_ref[one, :], g, m_ref[one, :], v_ref[one, :])
        loss_out[...] = jnp.broadcast_to(jnp.sum(total(LOSS_SOURCE), axis=1, keepdims=True), (8, LANES))

    full = lambda shp: pl.BlockSpec(shp, lambda i: (0,) * len(shp))
    return pl.pallas_call(
        body, name="adamw_small", grid=(1,),
        in_specs=[full((N_DEV, SMALL_ROWS, D)), full((16, D)), full((16, D)), full((16, D))],
        out_specs=[full((16, D))] * 4 + [full((8, LANES))],
        out_shape=[jax.ShapeDtypeStruct((16, D), F32)] * 4 + [jax.ShapeDtypeStruct((8, LANES), F32)],
        compiler_params=_cparams(("arbitrary",)),
    )(parts, w, m, v)


def _to_slab(shard, form):
    if form == "N":
        return shard
    return shard.T if form == "T" else shard.T.reshape(-1, D)


def _from_slab(block, form, shard_shape):
    if form == "N":
        return block
    return block.T if form == "T" else block.reshape(shard_shape[1], shard_shape[0]).T


def _gathered_full(g, form, shard_shape):
    if form == "TR":
        return g.reshape(N_DEV * shard_shape[1], shard_shape[0])
    return g.reshape(N_DEV * g.shape[1], D)


def _pack_slab(shards, layout):
    return jnp.concatenate([jnp.zeros((r, D), BF16) if n is None else _to_slab(shards[n], form).astype(BF16)
                            for n, r, form in layout], axis=0)


def _unpack_gathered(gathered, layout):
    out, off = {}, 0
    for n, r, form in layout:
        if n is not None:
            out[n] = _gathered_full(gathered[:, off:off + r], form, SHARD_SHAPES[n])
        off += r
    return out


def _pack_per_device(gw, layout):
    return jnp.concatenate([jnp.zeros((N_DEV, r, D), BF16) if n is None else gw[n].reshape(N_DEV, r, D)
                            for n, r, _ in layout], axis=1)


def _unpack_shards(gslab, layout):
    out, off = {}, 0
    for n, r, form in layout:
        if n is not None:
            out[n] = _from_slab(gslab[off:off + r], form, SHARD_SHAPES[n])
        off += r
    return out


def _sum_over_cores(per_dev, tag):
    my_c = lax.axis_index("c")
    pairs = per_dev.reshape(4, 2, per_dev.shape[1], D)
    keep = lax.dynamic_index_in_dim(pairs, my_c, axis=1, keepdims=False)
    give = lax.dynamic_index_in_dim(pairs, 1 - my_c, axis=1, keepdims=False)
    return _pair_sum(keep, _sibling_exchange(give, "rs_sibling_exchange_" + tag), "rs_pair_sum_" + tag)


def _small_pack(vals):
    rows = []
    for n in SMALL:
        v = vals[n].reshape(-1)
        k = -(-v.shape[0] // D)
        rows.append(jnp.pad(v, (0, k * D - v.shape[0])).reshape(k, D))
    return jnp.concatenate(rows, axis=0)


def _small_unpack(packed, shapes):
    out = {}
    for n in SMALL:
        k = shapes[n][-1]
        r0 = SMALL_ROW[n]
        out[n] = packed[r0:r0 + -(-k // D)].reshape(-1)[:k].reshape(shapes[n])
    return out


def kernel(x, c, positions, w_ada, b_ada, norm1_g, w_in, b_merge, gla_w_alpha, gla_b_alpha, gla_out_norm_g, gla_w_o, mla_q_lat_g, mla_w_uq, mla_kv_lat_g, mla_w_ukv, mla_qn_g, mla_kn_g, mla_w_o, w_out, norm2_g, mlp_w1, mlp_w2, loss_target, m_w_ada, m_b_ada, m_norm1_g, m_w_in, m_b_merge, m_gla_w_alpha, m_gla_b_alpha, m_gla_out_norm_g, m_gla_w_o, m_mla_q_lat_g, m_mla_w_uq, m_mla_kv_lat_g, m_mla_w_ukv, m_mla_qn_g, m_mla_kn_g, m_mla_w_o, m_w_out, m_norm2_g, m_mlp_w1, m_mlp_w2, v_w_ada, v_b_ada, v_norm1_g, v_w_in, v_b_merge, v_gla_w_alpha, v_gla_b_alpha, v_gla_out_norm_g, v_gla_w_o, v_mla_q_lat_g, v_mla_w_uq, v_mla_kv_lat_g, v_mla_w_ukv, v_mla_qn_g, v_mla_kn_g, v_mla_w_o, v_w_out, v_norm2_g, v_mlp_w1, v_mlp_w2):
    args = dict(locals())
    wts = {n: args[n][0] for n in WEIGHTS}
    mom = {n: args["m_" + n][0] for n in WEIGHTS}
    var = {n: args["v_" + n][0] for n in WEIGHTS}
    my_c = lax.axis_index("c")
    my_dev = 4 * lax.axis_index("x") + 2 * lax.axis_index("y") + my_c
    bsz = x.shape[0]
    sp = {n: wts[n].reshape(1, -1) for n in SMALL}

    wt = _unpack_gathered(_all_gather(_pack_slab(wts, SLAB_A), "weights_all_gather"), SLAB_A)

    c_all = _all_gather(jnp.pad(c, ((0, 8 - bsz), (0, 0))), "cond_all_gather")[:, :bsz].reshape(N_DEV * bsz, D)
    bias = lax.dynamic_slice_in_dim(sp["b_ada"], my_dev * ADA_COLS, ADA_COLS, axis=1)
    mod_cols = _mm(c_all, wts["w_ada"], "nn", (F32,), "ada_fwd", pro=_silu, epi=lambda acc, b: (acc + b,),
                   extras=(jnp.broadcast_to(bias, (N_DEV * bsz, ADA_COLS)),))
    mod_all = _all_gather(mod_cols, "mod_all_gather")
    mod_mine = lax.dynamic_slice_in_dim(mod_all, my_dev * bsz, bsz, axis=1)
    mod3 = jnp.transpose(mod_mine, (1, 0, 2)).reshape(bsz, 6, D)

    grad_x, gw, parts_b, rows = _local_step(x, positions, loss_target, wt, _pack_slab(wts, SLAB_B), sp, mod3)

    parts_a = _chip_exchange(_sum_over_cores(_pack_per_device(gw, SLAB_A), "a"), "rs_chip_exchange_a")
    grads = dict(_unpack_shards(_slab_sum(parts_a, "rs_slab_sum_a"), SLAB_A),
                 **_unpack_shards(_slab_sum(parts_b, "rs_slab_sum_b"), SLAB_B))
    big = {n: (g,) + tuple(_adamw(g, wts[n], mom[n], var[n], "adamw_" + n)) for n, g in grads.items()}

    order = ["dmod", "norm1_g", "norm2_g", "b_merge", "gla_b_alpha", "gla_out_norm_g", "mla_q_lat_g", "mla_kv_lat_g",
             "mla_qn_g", "mla_kn_g", "loss"]
    part_rows = jnp.concatenate([rows[n] for n in order], axis=0)
    part_rows = jnp.pad(part_rows, ((0, SMALL_ROWS - part_rows.shape[0]), (0, 0)))
    all_rows = _all_gather(part_rows, "partials_all_gather")

    dmod_all = all_rows[:, :6 * bsz].reshape(N_DEV * bsz, 6 * D)
    dmod_cols = lax.dynamic_slice_in_dim(dmod_all, my_dev * ADA_COLS, ADA_COLS, axis=1)
    g_ada = _mm(c_all, dmod_cols, "tn", (F32,), "ada_dw", pro=_silu)
    big["w_ada"] = (g_ada,) + tuple(_adamw(g_ada, wts["w_ada"], mom["w_ada"], var["w_ada"], "adamw_w_ada"))

    small = _adamw_small(all_rows, _small_pack({n: wts[n] for n in SMALL}), _small_pack({n: mom[n] for n in SMALL}),
                         _small_pack({n: var[n] for n in SMALL}))
    loss = small[4][0, 0]
    small_shapes = {n: wts[n].shape for n in SMALL}
    small = [_small_unpack(o, small_shapes) for o in small[:4]]

    outs = [loss, grad_x]
    for k in range(4):
        for n in WEIGHTS:
            val = big[n][k] if n in BIG else small[k][n]
            outs.append(val.reshape((1,) + tuple(wts[n].shape)))
    return tuple(outs)
```

```python
import functools

import jax
import jax.numpy as jnp
from jax import lax
from jax.experimental import pallas as pl
from jax.experimental.pallas import tpu as pltpu

F32 = jnp.float32
BF16 = jnp.bfloat16
MESH = pl.DeviceIdType.MESH

D = 1024
EPS = 1e-6
CHUNK = 64
GH, GDK, GDV, GLR, GTAU = 4, 128, 256, 16, 16.0
MH, MQR, MKVR, NOPE, ROPE, MV = 16, 256, 128, 64, 32, 64
MQK = NOPE + ROPE
HP = 128
FF = 4 * D
ROPE_THETA = 10000.0
IN_WIDTH = 5552
PW = 5632
N_DEV = 8
LANES = 128
SLAB_BLOCK_MAX = 400
ADA_COLS = 6 * D // N_DEV
SMALL_ROWS = 32
SMALL_SOURCES = tuple([(r, 6 + r) for r in range(6)] + [(12,), (13,), (14,), (15,), (16,), (17, 18, 19, 20),
                                                         (21,), (22,), (23,), (24,)])
LOSS_SOURCE = (25,)
VMEM_LIMIT = 56 * 1024 * 1024

ADAM_LR, ADAM_B1, ADAM_B2, ADAM_EPS, ADAM_WD, ADAM_STEP = 0.001, 0.9, 0.999, 1e-08, 0.01, 10

SLAB_A = (("w_in", 694, "T"), ("gla_w_alpha", 1, "TR"), (None, 9, None), ("mla_w_uq", 48, "TR"), ("mla_w_ukv", 32, "TR"))
SLAB_B = (("mlp_w1", 512, "T"), ("gla_w_o", 128, "N"), ("mla_w_o", 128, "N"), ("w_out", 128, "N"), ("mlp_w2", 512, "N"))
BIG = ("w_ada",) + tuple(n for n, _, _ in SLAB_A + SLAB_B if n is not None)
SHARD_SHAPES = {"w_ada": (D, 6 * D // N_DEV), "w_in": (D, IN_WIDTH // N_DEV), "gla_w_alpha": (GLR, GH * GDK // N_DEV),
                "gla_w_o": (GH * GDV // N_DEV, D), "mla_w_uq": (MQR, MH * MQK // N_DEV),
                "mla_w_ukv": (MKVR, MH * (NOPE + MV) // N_DEV), "mla_w_o": (MH * MV // N_DEV, D), "w_out": (D // N_DEV, D),
                "mlp_w1": (D, FF // N_DEV), "mlp_w2": (FF // N_DEV, D)}
SMALL = ("b_ada", "norm1_g", "norm2_g", "b_merge", "gla_b_alpha", "gla_out_norm_g", "mla_q_lat_g", "mla_kv_lat_g",
         "mla_qn_g", "mla_kn_g")
SMALL_ROW = {"b_ada": 0, "norm1_g": 6, "norm2_g": 7, "b_merge": 8, "gla_b_alpha": 10, "gla_out_norm_g": 11,
             "mla_q_lat_g": 12, "mla_kv_lat_g": 13, "mla_qn_g": 14, "mla_kn_g": 15}
WEIGHTS = ("w_ada", "b_ada", "norm1_g", "w_in", "b_merge", "gla_w_alpha", "gla_b_alpha", "gla_out_norm_g", "gla_w_o",
           "mla_q_lat_g", "mla_w_uq", "mla_kv_lat_g", "mla_w_ukv", "mla_qn_g", "mla_kn_g", "mla_w_o", "w_out",
           "norm2_g", "mlp_w1", "mlp_w2")


def _cparams(sem=None):
    return pltpu.CompilerParams(dimension_semantics=sem, vmem_limit_bytes=VMEM_LIMIT)


def _tile(n, pref):
    for t in (2048, 1024, 512, 256, 128):
        if t <= pref and n % t == 0:
            return t
    return n


def _dot(a, b, dims, precision=None):
    return lax.dot_general(a, b, (dims, ((), ())), preferred_element_type=F32, precision=precision)


NN = ((1,), (0,))
NT = ((1,), (1,))
TN = ((0,), (0,))


def _sigmoid(x):
    return 1.0 / (1.0 + jnp.exp(-x))


def _silu(x):
    return x * _sigmoid(x)


def _mm(a, b, mode, out_dtypes, name, *, pro=None, pro_b=None, epi=None, extras=(), a_off=0, m=None, tm=2048, tn=1024,
        tk=1024, cargo=None):
    if mode == "tn":
        kc, n = b.shape
        m = a.shape[1] if m is None else m
    elif mode == "nn":
        m, kc = a.shape
        n = b.shape[1]
    else:
        m, kc = a.shape
        n = b.shape[0]
    tm, tn, tk = _tile(m, tm), _tile(n, tn), _tile(kc, tk)
    nk = kc // tk
    dims = {"nn": NN, "nt": NT, "tn": TN}[mode]
    if mode == "tn":
        a_spec = pl.BlockSpec((tk, tm), lambda i, j, k: (k, i + a_off))
    else:
        a_spec = pl.BlockSpec((tm, tk), lambda i, j, k: (i + a_off, k))
    if mode == "nt":
        b_spec = pl.BlockSpec((tn, tk), lambda i, j, k: (j, k))
    else:
        b_spec = pl.BlockSpec((tk, tn), lambda i, j, k: (k, j))
    o_spec = pl.BlockSpec((tm, tn), lambda i, j, k: (i, j))
    n_ex, n_out = len(extras), len(out_dtypes)
    grid = (m // tm, n // tn, nk)
    has_cargo = cargo is not None

    def body(a_ref, b_ref, *rest):
        ex, rest = rest[:n_ex], rest[n_ex:]
        if has_cargo:
            outs, acc = rest[1:1 + n_out], rest[2 + n_out]
            exchange = _chip_exchange_copies(rest[0], rest[1 + n_out], *rest[3 + n_out:])
            steps = [pl.program_id(axis) for axis in range(3)]
            first = (steps[0] == 0) & (steps[1] == 0) & (steps[2] == 0)
            last = (steps[0] == grid[0] - 1) & (steps[1] == grid[1] - 1) & (steps[2] == grid[2] - 1)
            pl.when(first)(exchange.start)
        else:
            outs, acc = rest[:n_out], rest[n_out]
        k = pl.program_id(2)

        @pl.when(k == 0)
        def _():
            acc[...] = jnp.zeros_like(acc)

        av = a_ref[...]
        if pro is not None:
            av = pro(av)
        bv = b_ref[...]
        if pro_b is not None:
            bv = pro_b(bv)
        acc[...] += _dot(av.astype(BF16), bv.astype(BF16), dims)

        @pl.when(k == nk - 1)
        def _():
            res = (acc[...],) if epi is None else epi(acc[...], *[e[...] for e in ex])
            for o_ref, r in zip(outs, res):
                o_ref[...] = r.astype(o_ref.dtype)

        if has_cargo:
            pl.when(last)(exchange.finish)

    cargo_in = [cargo] if has_cargo else []
    cargo_spec = [HBM_SPEC] * len(cargo_in)
    out = pl.pallas_call(
        body, name=name, grid=grid,
        in_specs=[a_spec, b_spec] + [o_spec] * n_ex + cargo_spec,
        out_specs=[o_spec] * n_out + cargo_spec,
        out_shape=[jax.ShapeDtypeStruct((m, n), dt) for dt in out_dtypes]
        + [jax.ShapeDtypeStruct(c.shape, c.dtype) for c in cargo_in],
        scratch_shapes=[pltpu.VMEM((tm, tn), F32)] + (EXCHANGE_SEMS if has_cargo else []),
        compiler_params=_cparams(("arbitrary",) * 3 if has_cargo else ("parallel", "parallel", "arbitrary")),
    )(a, b, *extras, *cargo_in)
    return out[0] if len(out) == 1 else out


def _rows(s):
    return _tile(s, 512)


def _mod_spec():
    return pl.BlockSpec((1, 6, D), lambda b, i: (b, 0, 0))


def _tok_spec(tr, nb, width=D, col=0):
    return pl.BlockSpec((tr, width), lambda b, i: (b * nb + i, col))


def _norm_mod_fwd(x, g, mod3, i_shift, i_scale, name, mixed=None, i_gate=None):
    bsz, _, _ = mod3.shape
    t = x.shape[0]
    s = t // bsz
    tr = _rows(s)
    nb = s // tr
    has_res = mixed is not None

    def body(*refs):
        if has_res:
            x_ref, mx_ref, g_ref, mod_ref, x1_ref, h_ref = refs
            xv = x_ref[...] + mod_ref[0, i_gate:i_gate + 1, :] * mx_ref[...]
            x1_ref[...] = xv
        else:
            x_ref, g_ref, mod_ref, h_ref = refs
            xv = x_ref[...]
        r = lax.rsqrt(jnp.mean(xv * xv, axis=1, keepdims=True) + EPS)
        hn = (xv * r) * g_ref[...]
        h = hn * (1.0 + mod_ref[0, i_scale:i_scale + 1, :]) + mod_ref[0, i_shift:i_shift + 1, :]
        h_ref[...] = h.astype(BF16)

    tok = _tok_spec(tr, nb)
    gspec = pl.BlockSpec((1, D), lambda b, i: (0, 0))
    ins = [x] + ([mixed] if has_res else []) + [g, mod3]
    in_specs = [tok] + ([tok] if has_res else []) + [gspec, _mod_spec()]
    out_shape = ([jax.ShapeDtypeStruct((t, D), F32)] if has_res else []) + [jax.ShapeDtypeStruct((t, D), BF16)]
    out = pl.pallas_call(
        body, name=name, grid=(bsz, nb), in_specs=in_specs, out_specs=[tok] * len(out_shape), out_shape=out_shape,
        compiler_params=_cparams(("arbitrary", "arbitrary")),
    )(*ins)
    return (out[0], out[1]) if has_res else (None, out[0])


def _norm_mod_bwd(x, dh, dres, g, mod3, i_shift, i_scale, name, mixed=None, i_gate=None):
    bsz = mod3.shape[0]
    t = x.shape[0]
    s = t // bsz
    tr = _rows(s)
    nb = s // tr
    has_res = mixed is not None

    def body(*refs):
        if has_res:
            x_ref, dh_ref, dres_ref, mx_ref, g_ref, mod_ref, dx_ref, dmx_ref, accb, accg = refs
        else:
            x_ref, dh_ref, dres_ref, g_ref, mod_ref, dx_ref, accb, accg = refs
        b, i = pl.program_id(0), pl.program_id(1)

        @pl.when(i == 0)
        def _():
            accb[...] = jnp.zeros_like(accb)

        @pl.when((i == 0) & (b == 0))
        def _():
            accg[...] = jnp.zeros_like(accg)

        xv, dhv, gv = x_ref[...], dh_ref[...], g_ref[...]
        r = lax.rsqrt(jnp.mean(xv * xv, axis=1, keepdims=True) + EPS)
        xn = xv * r
        accb[0, 0:1, :] += jnp.sum(dhv, axis=0, keepdims=True)
        accb[0, 1:2, :] += jnp.sum(dhv * (xn * gv), axis=0, keepdims=True)
        tt = dhv * (1.0 + mod_ref[0, i_scale:i_scale + 1, :])
        accg[0:1, :] += jnp.sum(tt * xn, axis=0, keepdims=True)
        dxn = tt * gv
        dx = dres_ref[...] + r * (dxn - xn * jnp.mean(dxn * xn, axis=1, keepdims=True))
        dx_ref[...] = dx
        if has_res:
            accb[0, 2:3, :] += jnp.sum(dx * mx_ref[...], axis=0, keepdims=True)
            dmx_ref[...] = (dx * mod_ref[0, i_gate:i_gate + 1, :]).astype(BF16)

    tok = _tok_spec(tr, nb)
    gspec = pl.BlockSpec((1, D), lambda b, i: (0, 0))
    ins = [x, dh, dres] + ([mixed] if has_res else []) + [g, mod3]
    in_specs = [tok] * (4 if has_res else 3) + [gspec, _mod_spec()]
    out_shape = [jax.ShapeDtypeStruct((t, D), F32)] + ([jax.ShapeDtypeStruct((t, D), BF16)] if has_res else [])
    out_specs = [tok] * len(out_shape)
    out_shape += [jax.ShapeDtypeStruct((bsz, 8, D), F32), jax.ShapeDtypeStruct((8, D), F32)]
    out_specs += [pl.BlockSpec((1, 8, D), lambda b, i: (b, 0, 0)), pl.BlockSpec((8, D), lambda b, i: (0, 0))]
    return pl.pallas_call(
        body, name=name, grid=(bsz, nb), in_specs=in_specs, out_specs=out_specs, out_shape=out_shape,
        compiler_params=_cparams(("arbitrary", "arbitrary")),
    )(*ins)


def _loss_head(x1, ff, tgt, mod3):
    bsz = mod3.shape[0]
    t = x1.shape[0]
    s = t // bsz
    tr = _rows(s)
    nb = s // tr

    def body(x1_ref, ff_ref, tg_ref, mod_ref, dy_ref, dff_ref, accb, accl):
        b, i = pl.program_id(0), pl.program_id(1)

        @pl.when(i == 0)
        def _():
            accb[...] = jnp.zeros_like(accb)

        @pl.when((i == 0) & (b == 0))
        def _():
            accl[...] = jnp.zeros_like(accl)

        gate = mod_ref[0, 5:6, :]
        ffv = ff_ref[...]
        err = x1_ref[...] + gate * ffv - tg_ref[...]
        accl[0:1, :] += jnp.sum(err * err, axis=0, keepdims=True) * (0.5 / D)
        dy = err * (1.0 / D)
        dy_ref[...] = dy
        dff_ref[...] = (dy * gate).astype(BF16)
        accb[0, 0:1, :] += jnp.sum(dy * ffv, axis=0, keepdims=True)

    tok = _tok_spec(tr, nb)
    return pl.pallas_call(
        body, name="loss_head", grid=(bsz, nb), in_specs=[tok, tok, tok, _mod_spec()],
        out_specs=[tok, tok, pl.BlockSpec((1, 8, D), lambda b, i: (b, 0, 0)), pl.BlockSpec((8, D), lambda b, i: (0, 0))],
        out_shape=[jax.ShapeDtypeStruct((t, D), F32), jax.ShapeDtypeStruct((t, D), BF16),
                   jax.ShapeDtypeStruct((bsz, 8, D), F32), jax.ShapeDtypeStruct((8, D), F32)],
        compiler_params=_cparams(("arbitrary", "arbitrary")),
    )(x1, ff, tgt, mod3)


def _merge_fwd(proj, b_merge, y_a, y_b):
    t = proj.shape[0]
    tr = _tile(t, 512)

    def body(la_ref, lb_ref, bm_ref, ya_ref, yb_ref, mix_ref):
        ga = _sigmoid(la_ref[...] + bm_ref[:, 0:D])
        gb = _sigmoid(lb_ref[...] + bm_ref[:, D:2 * D])
        mix_ref[...] = (ga * ya_ref[...].astype(F32) + gb * yb_ref[...].astype(F32)).astype(BF16)

    tok = pl.BlockSpec((tr, D), lambda i: (i, 0))
    return pl.pallas_call(
        body, name="merge_fwd", grid=(t // tr,),
        in_specs=[pl.BlockSpec((tr, D), lambda i: (i, 3)), pl.BlockSpec((tr, D), lambda i: (i, 4)),
                  pl.BlockSpec((1, 2 * D), lambda i: (0, 0)), tok, tok],
        out_specs=tok, out_shape=jax.ShapeDtypeStruct((t, D), BF16),
        compiler_params=_cparams(("arbitrary",)),
    )(proj, proj, b_merge, y_a, y_b)


def _merge_bwd(dmix, proj, b_merge, y_a, y_b):
    t = proj.shape[0]
    tr = _tile(t, 512)

    def body(dm_ref, la_ref, lb_ref, bm_ref, ya_ref, yb_ref, dya_ref, dyb_ref, dl_ref, acc):
        @pl.when(pl.program_id(0) == 0)
        def _():
            acc[...] = jnp.zeros_like(acc)

        dm = dm_ref[...].astype(F32)
        ga = _sigmoid(la_ref[...] + bm_ref[:, 0:D])
        gb = _sigmoid(lb_ref[...] + bm_ref[:, D:2 * D])
        dya_ref[...] = (dm * ga).astype(BF16)
        dyb_ref[...] = (dm * gb).astype(BF16)
        dla = dm * ya_ref[...].astype(F32) * ga * (1.0 - ga)
        dlb = dm * yb_ref[...].astype(F32) * gb * (1.0 - gb)
        dl_ref[:, 0:D] = dla.astype(BF16)
        dl_ref[:, D:2 * D] = dlb.astype(BF16)
        acc[0:1, 0:D] += jnp.sum(dla, axis=0, keepdims=True)
        acc[0:1, D:2 * D] += jnp.sum(dlb, axis=0, keepdims=True)

    tok = pl.BlockSpec((tr, D), lambda i: (i, 0))
    return pl.pallas_call(
        body, name="merge_bwd", grid=(t // tr,),
        in_specs=[tok, pl.BlockSpec((tr, D), lambda i: (i, 3)), pl.BlockSpec((tr, D), lambda i: (i, 4)),
                  pl.BlockSpec((1, 2 * D), lambda i: (0, 0)), tok, tok],
        out_specs=[tok, tok, pl.BlockSpec((tr, 2 * D), lambda i: (i, 0)), pl.BlockSpec((8, 2 * D), lambda i: (0, 0))],
        out_shape=[jax.ShapeDtypeStruct((t, D), BF16), jax.ShapeDtypeStruct((t, D), BF16),
                   jax.ShapeDtypeStruct((t, 2 * D), BF16), jax.ShapeDtypeStruct((8, 2 * D), F32)],
        compiler_params=_cparams(("arbitrary",)),
    )(dmix, proj, proj, b_merge, y_a, y_b)


def _log_sigmoid(z):
    return jnp.minimum(z, 0.0) - jnp.log(1.0 + jnp.exp(-jnp.abs(z)))


def _tri(lower):
    r = lax.broadcasted_iota(jnp.int32, (CHUNK, CHUNK), 0)
    c = lax.broadcasted_iota(jnp.int32, (CHUNK, CHUNK), 1)
    return jnp.where(r >= c if lower else r <= c, 1.0, 0.0).astype(F32)


def _gla_fwd(proj, wa_pad, b_alpha, g_out, bsz):
    t = proj.shape[0]
    s = t // bsz
    nc = s // CHUNK

    def body(q_ref, k_ref, v_ref, gg_ref, ms_ref, wa_ref, ba_ref, go_ref, o_ref, og_ref, st_ref, la, state):
        z = _dot(ms_ref[...].astype(BF16), wa_ref[...], NN) + ba_ref[...]
        la[...] = _log_sigmoid(z) * (1.0 / GTAU)
        state[...] = jnp.zeros_like(state)
        low = _tri(True)
        gout = go_ref[...]

        def chunk(n, carry):
            rows = pl.ds(pl.multiple_of(n * CHUNK, CHUNK), CHUNK)
            lac = la[rows, :]
            cum = _dot(low, lac, NN, lax.Precision.HIGHEST)
            ce = jnp.sum(lac, axis=0, keepdims=True)
            kd = (k_ref[rows, :].astype(F32) * jnp.exp(ce - cum)).astype(BF16)
            new = state[...] * jnp.exp(ce) + _dot(v_ref[rows, :].astype(BF16), kd, TN)
            state[...] = new
            st_ref[pl.ds(pl.multiple_of(n * GDV, GDV), GDV), :] = new
            qs = (q_ref[rows, :].astype(F32) * (GDK ** -0.5)).astype(BF16)
            o = _dot(qs, new.astype(BF16), NT)
            o_ref[rows, :] = o
            ro = lax.rsqrt(jnp.mean(o * o, axis=1, keepdims=True) + EPS)
            og_ref[rows, :] = (((o * ro) * gout) * _silu(gg_ref[rows, :].astype(F32))).astype(BF16)
            return carry

        lax.fori_loop(0, nc, chunk, 0, unroll=8)

    hk = pl.BlockSpec((s, GDK), lambda b, h: (b, h))
    return pl.pallas_call(
        body, name="gla_fwd", grid=(bsz, GH),
        in_specs=[hk, pl.BlockSpec((s, GDK), lambda b, h: (b, GH + h)), pl.BlockSpec((s, GDV), lambda b, h: (b, 4 + h)),
                  pl.BlockSpec((s, GDV), lambda b, h: (b, 8 + h)), pl.BlockSpec((s, LANES), lambda b, h: (b, 43)),
                  pl.BlockSpec((LANES, GDK), lambda b, h: (0, h)), pl.BlockSpec((1, GDK), lambda b, h: (0, h)),
                  pl.BlockSpec((1, GDV), lambda b, h: (0, 0))],
        out_specs=[pl.BlockSpec((s, GDV), lambda b, h: (b, h)), pl.BlockSpec((s, GDV), lambda b, h: (b, h)),
                   pl.BlockSpec((nc * GDV, GDK), lambda b, h: (b * GH + h, 0))],
        out_shape=[jax.ShapeDtypeStruct((t, GH * GDV), F32), jax.ShapeDtypeStruct((t, GH * GDV), BF16),
                   jax.ShapeDtypeStruct((bsz * GH * nc * GDV, GDK), F32)],
        scratch_shapes=[pltpu.VMEM((s, GDK), F32), pltpu.VMEM((GDV, GDK), F32)],
        compiler_params=_cparams(("arbitrary", "arbitrary")),
    )(proj, proj, proj, proj, proj, wa_pad, b_alpha, g_out)


def _gla_bwd(dog, o, proj, wa_pad, b_alpha, g_out, states, bsz):
    t = proj.shape[0]
    s = t // bsz
    nc = s // CHUNK

    def body(dog_ref, o_ref, q_ref, k_ref, v_ref, gg_ref, ms_ref, wa_ref, ba_ref, go_ref, st_ref,
             dq_ref, dk_ref, dv_ref, dgg_ref, dz_ref, dba, dgo, zs, la, carry_g):
        @pl.when(pl.program_id(1) == 0)
        def _():
            dba[...] = jnp.zeros_like(dba)
            dgo[...] = jnp.zeros_like(dgo)

        z = _dot(ms_ref[...].astype(BF16), wa_ref[...], NN) + ba_ref[...]
        zs[...] = z
        la[...] = _log_sigmoid(z) * (1.0 / GTAU)
        carry_g[...] = jnp.zeros_like(carry_g)
        low, upp = _tri(True), _tri(False)
        gout = go_ref[...]
        last_row = lax.broadcasted_iota(jnp.int32, (CHUNK, GDK), 0) == CHUNK - 1

        def chunk(step, carry):
            n = nc - 1 - step
            rows = pl.ds(pl.multiple_of(n * CHUNK, CHUNK), CHUNK)
            lac = la[rows, :]
            cum = _dot(low, lac, NN, lax.Precision.HIGHEST)
            ce = jnp.sum(lac, axis=0, keepdims=True)
            e = jnp.exp(ce - cum)
            dec = jnp.exp(ce)
            kf = k_ref[rows, :].astype(F32)
            kd = (kf * e).astype(BF16)
            vv = v_ref[rows, :].astype(BF16)
            qs = (q_ref[rows, :].astype(F32) * (GDK ** -0.5)).astype(BF16)
            ov = o_ref[rows, :]
            ro = lax.rsqrt(jnp.mean(ov * ov, axis=1, keepdims=True) + EPS)
            on = ov * ro
            gg = gg_ref[rows, :].astype(F32)
            sg = _sigmoid(gg)
            dogv = dog_ref[rows, :].astype(F32)
            dgg_ref[rows, :] = (dogv * (on * gout) * (sg * (1.0 + gg * (1.0 - sg)))).astype(BF16)
            t1 = dogv * (gg * sg)
            dgo[0:1, :] += jnp.sum(t1 * on, axis=0, keepdims=True)
            don = t1 * gout
            do = ro * (don - on * jnp.mean(don * on, axis=1, keepdims=True))
            dob = do.astype(BF16)
            st_n = st_ref[pl.ds(pl.multiple_of(n * GDV, GDV), GDV), :]
            dq_ref[rows, :] = (_dot(dob, st_n.astype(BF16), NN) * (GDK ** -0.5)).astype(BF16)
            dn = carry_g[...] + _dot(dob, qs, TN)
            prev = jnp.maximum(n - 1, 0)
            st_p = st_ref[pl.ds(pl.multiple_of(prev * GDV, GDV), GDV), :] * jnp.where(n > 0, 1.0, 0.0)
            ddec = jnp.sum(dn * st_p, axis=0, keepdims=True)
            dnb = dn.astype(BF16)
            dkd = _dot(vv, dnb, NN)
            dv_ref[rows, :] = _dot(kd, dnb, NT).astype(BF16)
            dk_ref[rows, :] = (dkd * e).astype(BF16)
            w = dkd * kf * e
            dce = jnp.sum(w, axis=0, keepdims=True) + ddec * dec
            dcum = jnp.where(last_row, dce - w, -w)
            dla = _dot(upp, dcum, NN, lax.Precision.HIGHEST)
            dz = dla * (1.0 / GTAU) * _sigmoid(-zs[rows, :])
            dba[0:1, :] += jnp.sum(dz, axis=0, keepdims=True)
            dz_ref[rows, :] = dz.astype(BF16)
            carry_g[...] = dn * dec
            return carry

        lax.fori_loop(0, nc, chunk, 0, unroll=8)

    hv = pl.BlockSpec((s, GDV), lambda h, b: (b, h))
    hk = pl.BlockSpec((s, GDK), lambda h, b: (b, h))
    return pl.pallas_call(
        body, name="gla_bwd", grid=(GH, bsz),
        in_specs=[hv, hv, hk, pl.BlockSpec((s, GDK), lambda h, b: (b, GH + h)),
                  pl.BlockSpec((s, GDV), lambda h, b: (b, 4 + h)), pl.BlockSpec((s, GDV), lambda h, b: (b, 8 + h)),
                  pl.BlockSpec((s, LANES), lambda h, b: (b, 43)), pl.BlockSpec((LANES, GDK), lambda h, b: (0, h)),
                  pl.BlockSpec((1, GDK), lambda h, b: (0, h)), pl.BlockSpec((1, GDV), lambda h, b: (0, 0)),
                  pl.BlockSpec((nc * GDV, GDK), lambda h, b: (b * GH + h, 0))],
        out_specs=[hk, hk, hv, hv, hk, pl.BlockSpec((8, GDK), lambda h, b: (0, h)),
                   pl.BlockSpec((8, GDV), lambda h, b: (h, 0))],
        out_shape=[jax.ShapeDtypeStruct((t, GH * GDK), BF16), jax.ShapeDtypeStruct((t, GH * GDK), BF16),
                   jax.ShapeDtypeStruct((t, GH * GDV), BF16), jax.ShapeDtypeStruct((t, GH * GDV), BF16),
                   jax.ShapeDtypeStruct((t, GH * GDK), BF16), jax.ShapeDtypeStruct((8, GH * GDK), F32),
                   jax.ShapeDtypeStruct((8 * GH, GDV), F32)],
        scratch_shapes=[pltpu.VMEM((s, GDK), F32), pltpu.VMEM((s, GDK), F32), pltpu.VMEM((GDV, GDK), F32)],
        compiler_params=_cparams(("arbitrary", "arbitrary")),
    )(dog, o, proj, proj, proj, proj, proj, wa_pad, b_alpha, g_out, states)


def _rope_tables(pos_ref, fr_ref, sg_ref):
    ang = pos_ref[...].astype(F32) * fr_ref[...]
    return jnp.cos(ang), jnp.sin(ang) * sg_ref[...]


def _partner(x):
    lane = lax.broadcasted_iota(jnp.int32, x.shape, 1)
    return jnp.where(lane < NOPE + ROPE // 2, pltpu.roll(x, LANES - ROPE // 2, 1), pltpu.roll(x, ROPE // 2, 1))


def _mla_rows(t):
    return _tile(t, 256)


def _mla_pre_fwd(proj, pos, fr, sg, q_lat_g, kv_lat_g, qn_g, kn_g, wuq, wukv):
    t = proj.shape[0]
    tr = _mla_rows(t)

    def body(cq_ref, ckv_ref, ms_ref, pos_ref, fr_ref, sg_ref, qlg, kvlg, qng, kng, wuq_ref, wukv_ref, q_out, k_out, v_out):
        cos, sin = _rope_tables(pos_ref, fr_ref, sg_ref)
        cq = cq_ref[...].astype(F32)
        cqn = (cq * lax.rsqrt(jnp.mean(cq * cq, axis=1, keepdims=True) + EPS) * qlg[...]).astype(BF16)
        ckv = ckv_ref[...].astype(F32)
        ckvn = (ckv * lax.rsqrt(jnp.mean(ckv * ckv, axis=1, keepdims=True) + EPS) * kvlg[...]).astype(BF16)
        lane = lax.broadcasted_iota(jnp.int32, (tr, HP), 1)
        kpe = jnp.where((lane >= NOPE) & (lane < MQK), ms_ref[...].astype(F32), 0.0)
        lane_all = lax.broadcasted_iota(jnp.int32, (tr, MH * HP), 1)
        v_out[...] = jnp.where(lane_all % HP == MV, 1.0, _dot(ckvn, wukv_ref[:, MH * HP:], NN)).astype(BF16)
        for h in range(MH):
            cols = slice(h * HP, (h + 1) * HP)
            qh = _dot(cqn, wuq_ref[:, cols], NN)
            qn = qh * lax.rsqrt(jnp.sum(qh * qh, axis=1, keepdims=True) * (1.0 / MQK) + EPS) * qng[...]
            q_out[:, cols] = (qn * cos + _partner(qn) * sin).astype(BF16)
            kh = _dot(ckvn, wukv_ref[:, cols], NN) + kpe
            kn = kh * lax.rsqrt(jnp.sum(kh * kh, axis=1, keepdims=True) * (1.0 / MQK) + EPS) * kng[...]
            k_out[:, cols] = (kn * cos + _partner(kn) * sin).astype(BF16)

    def full(a):
        return pl.BlockSpec(a.shape, lambda i: (0, 0))

    wide = pl.BlockSpec((tr, MH * HP), lambda i: (i, 0))
    return pl.pallas_call(
        body, name="mla_pre_fwd", grid=(t // tr,),
        in_specs=[pl.BlockSpec((tr, MQR), lambda i: (i, 20)), pl.BlockSpec((tr, MKVR), lambda i: (i, 42)),
                  pl.BlockSpec((tr, LANES), lambda i: (i, 43)), pl.BlockSpec((tr, 1), lambda i: (i, 0)),
                  full(fr), full(sg), full(q_lat_g), full(kv_lat_g), full(qn_g), full(kn_g), full(wuq), full(wukv)],
        out_specs=[wide, wide, wide],
        out_shape=[jax.ShapeDtypeStruct((t, MH * HP), BF16)] * 3,
        compiler_params=_cparams(("arbitrary",)),
    )(proj, proj, proj, pos, fr, sg, q_lat_g, kv_lat_g, qn_g, kn_g, wuq, wukv)


def _mla_pre_bwd(dq2, dk2, dv2, dmisc_gla, proj, pos, fr, sg, q_lat_g, kv_lat_g, qn_g, kn_g, wuq, wukv):
    t = proj.shape[0]
    tr = _mla_rows(t)

    def body(dq_ref, dk_ref, dv_ref, dmg_ref, cq_ref, ckv_ref, ms_ref, pos_ref, fr_ref, sg_ref, qlg, kvlg, qng, kng,
             wuq_ref, wukv_ref, dcq_ref, dckv_ref, dms_ref, dwuq, dwukv, acc, dqf, dkvf):
        @pl.when(pl.program_id(0) == 0)
        def _():
            dwuq[...] = jnp.zeros_like(dwuq)
            dwukv[...] = jnp.zeros_like(dwukv)
            acc[...] = jnp.zeros_like(acc)

        cos, sin = _rope_tables(pos_ref, fr_ref, sg_ref)
        cq = cq_ref[...].astype(F32)
        rc = lax.rsqrt(jnp.mean(cq * cq, axis=1, keepdims=True) + EPS)
        xc = cq * rc
        cqn = (xc * qlg[...]).astype(BF16)
        ckv = ckv_ref[...].astype(F32)
        rkv = lax.rsqrt(jnp.mean(ckv * ckv, axis=1, keepdims=True) + EPS)
        xkv = ckv * rkv
        ckvn = (xkv * kvlg[...]).astype(BF16)
        lane = lax.broadcasted_iota(jnp.int32, (tr, HP), 1)
        is_rope = (lane >= NOPE) & (lane < MQK)
        kpe = jnp.where(is_rope, ms_ref[...].astype(F32), 0.0)
        dkpe = jnp.zeros((tr, HP), F32)
        dqng = jnp.zeros((1, HP), F32)
        dkng = jnp.zeros((1, HP), F32)
        for h in range(MH):
            cols = slice(h * HP, (h + 1) * HP)
            qh = _dot(cqn, wuq_ref[:, cols], NN)
            rq = lax.rsqrt(jnp.sum(qh * qh, axis=1, keepdims=True) * (1.0 / MQK) + EPS)
            xq = qh * rq
            dy = dq_ref[:, cols].astype(F32)
            dqn = dy * cos - _partner(dy) * sin
            dqng += jnp.sum(dqn * xq, axis=0, keepdims=True)
            tq = dqn * qng[...]
            dqf[:, cols] = (rq * (tq - xq * (jnp.sum(tq * xq, axis=1, keepdims=True) * (1.0 / MQK)))).astype(BF16)
            kh = _dot(ckvn, wukv_ref[:, cols], NN) + kpe
            rk = lax.rsqrt(jnp.sum(kh * kh, axis=1, keepdims=True) * (1.0 / MQK) + EPS)
            xk = kh * rk
            dy = dk_ref[:, cols].astype(F32)
            dkn = dy * cos - _partner(dy) * sin
            dkng += jnp.sum(dkn * xk, axis=0, keepdims=True)
            tk = dkn * kng[...]
            dkh = rk * (tk - xk * (jnp.sum(tk * xk, axis=1, keepdims=True) * (1.0 / MQK)))
            dkvf[:, cols] = jnp.where(lane < NOPE, dkh, 0.0).astype(BF16)
            dkpe += jnp.where(is_rope, dkh, 0.0)
        dkvf[:, MH * HP:] = dv_ref[...]
        acc[2:3, 0:HP] += dqng
        acc[3:4, 0:HP] += dkng
        dms_ref[...] = (dmg_ref[...] + dkpe).astype(BF16)

        dqfv = dqf[...]
        dwuq[...] += _dot(cqn, dqfv, TN)
        dcqn = _dot(dqfv, wuq_ref[...], NT)
        acc[0:1, :] += jnp.sum(dcqn * xc, axis=0, keepdims=True)
        tc = dcqn * qlg[...]
        dcq_ref[...] = (rc * (tc - xc * jnp.mean(tc * xc, axis=1, keepdims=True))).astype(BF16)

        dkvfv = dkvf[...]
        dwukv[...] += _dot(ckvn, dkvfv, TN)
        dckvn = _dot(dkvfv, wukv_ref[...], NT)
        acc[1:2, 0:MKVR] += jnp.sum(dckvn * xkv, axis=0, keepdims=True)
        tkv = dckvn * kvlg[...]
        dckv_ref[...] = (rkv * (tkv - xkv * jnp.mean(tkv * xkv, axis=1, keepdims=True))).astype(BF16)

    def full(a):
        return pl.BlockSpec(a.shape, lambda i: (0, 0))

    wide = pl.BlockSpec((tr, MH * HP), lambda i: (i, 0))
    narrow = pl.BlockSpec((tr, LANES), lambda i: (i, 0))
    return pl.pallas_call(
        body, name="mla_pre_bwd", grid=(t // tr,),
        in_specs=[wide, wide, wide, narrow,
                  pl.BlockSpec((tr, MQR), lambda i: (i, 20)), pl.BlockSpec((tr, MKVR), lambda i: (i, 42)),
                  pl.BlockSpec((tr, LANES), lambda i: (i, 43)), pl.BlockSpec((tr, 1), lambda i: (i, 0)),
                  full(fr), full(sg), full(q_lat_g), full(kv_lat_g), full(qn_g), full(kn_g), full(wuq), full(wukv)],
        out_specs=[pl.BlockSpec((tr, MQR), lambda i: (i, 0)), narrow, narrow,
                   pl.BlockSpec((MQR, MH * HP), lambda i: (0, 0)), pl.BlockSpec((MKVR, 2 * MH * HP), lambda i: (0, 0)),
                   pl.BlockSpec((8, MQR), lambda i: (0, 0))],
        out_shape=[jax.ShapeDtypeStruct((t, MQR), BF16), jax.ShapeDtypeStruct((t, MKVR), BF16),
                   jax.ShapeDtypeStruct((t, LANES), BF16), jax.ShapeDtypeStruct((MQR, MH * HP), F32),
                   jax.ShapeDtypeStruct((MKVR, 2 * MH * HP), F32), jax.ShapeDtypeStruct((8, MQR), F32)],
        scratch_shapes=[pltpu.VMEM((tr, MH * HP), BF16), pltpu.VMEM((tr, 2 * MH * HP), BF16)],
        compiler_params=_cparams(("arbitrary",)),
    )(dq2, dk2, dv2, dmisc_gla, proj, proj, proj, pos, fr, sg, q_lat_g, kv_lat_g, qn_g, kn_g, wuq, wukv)


ATT_FWD_TILES = (1024, 512)
ATT_BWD_TILES = (512, 512)
ATT_HEADS = 2
NEG = -1e30
LOG2E = 1.4426950408889634


def _att_mask(q0, k0, tq, tk):
    qc = (q0 + lax.broadcasted_iota(jnp.int32, (tq, tk), 0)) // CHUNK
    kc = (k0 + lax.broadcasted_iota(jnp.int32, (tq, tk), 1)) // CHUNK
    return kc <= qc


def _att_tiles(s, tiles):
    return _tile(s, tiles[0]), _tile(s, tiles[1])


def _lanes(x, n):
    return x if n == 1 else jnp.concatenate([x] * n, axis=1)


def _grid_ends(grid):
    i, j = pl.program_id(0), pl.program_id(1)
    return (i == 0) & (j == 0), (i == grid[0] - 1) & (j == grid[1] - 1)


def _attn_fwd(q2, k2, v2, bsz, slab):
    t = q2.shape[0]
    s = t // bsz
    tq, tk = _att_tiles(s, ATT_FWD_TILES)
    nq, groups, n_diag = s // tq, tk // HP, max(tq // tk, 1)
    scale = MQK ** -0.5
    c2 = scale * LOG2E
    heads = range(ATT_HEADS)

    def body(q_ref, k_ref, v_ref, slab_ref, o_ref, lse_ref, gath_ref, send_sems, recv_sems, local_sem):
        gather = _core_row_gather_copies(slab_ref, gath_ref, send_sems, recv_sems, local_sem)
        first, last = _grid_ends((bsz, MH // ATT_HEADS))
        pl.when(first)(gather.start)

        def q_loop(qi, carry):
            q0 = pl.multiple_of(qi * tq, tq)
            rows = pl.ds(q0, tq)
            n_full = q0 // tk
            qs = [q_ref[rows, h * HP:(h + 1) * HP] for h in heads]

            def scores(h, kj, masked):
                k0 = pl.multiple_of(kj * tk, tk)
                sc = _dot(qs[h], k_ref[pl.ds(k0, tk), h * HP:(h + 1) * HP], NT)
                return jnp.where(_att_mask(q0, k0, tq, tk), sc, NEG) if masked else sc

            def fold(mx, sc):
                for j in range(groups):
                    mx = jnp.maximum(mx, sc[:, j * HP:(j + 1) * HP])
                return mx

            mx = lax.fori_loop(0, n_full, lambda kj, mx: tuple(fold(mx[h], scores(h, kj, False)) for h in heads),
                               tuple(jnp.full((tq, HP), NEG, F32) for _ in heads))
            for u in range(n_diag):
                mx = tuple(fold(mx[h], scores(h, n_full + u, True)) for h in heads)
            mb = [jnp.broadcast_to(jnp.max(mx[h], axis=1, keepdims=True), (tq, HP)) for h in heads]

            def weighted(h, kj, masked):
                p = jnp.exp2((scores(h, kj, masked) - _lanes(mb[h], groups)) * c2)
                k0 = pl.multiple_of(kj * tk, tk)
                return _dot(p.astype(BF16), v_ref[pl.ds(k0, tk), h * HP:(h + 1) * HP], NN)

            acc = lax.fori_loop(0, n_full, lambda kj, acc: tuple(acc[h] + weighted(h, kj, False) for h in heads),
                                tuple(jnp.zeros((tq, HP), F32) for _ in heads))
            for u in range(n_diag):
                acc = tuple(acc[h] + weighted(h, n_full + u, True) for h in heads)
            lane = lax.broadcasted_iota(jnp.int32, (tq, HP), 1)
            for h in heads:
                a = acc[h]
                l = jnp.sum(jnp.where(lane == MV, a, 0.0), axis=1, keepdims=True)
                o_ref[rows, h * HP:(h + 1) * HP] = (a / l).astype(BF16)
                lse_ref[rows, h * HP:(h + 1) * HP] = mb[h] * scale + jnp.log(l)
            return carry

        lax.fori_loop(0, nq, q_loop, 0)
        pl.when(last)(gather.finish)

    spec = pl.BlockSpec((s, ATT_HEADS * HP), lambda b, h: (b, h))
    return pl.pallas_call(
        body, name="attn_fwd", grid=(bsz, MH // ATT_HEADS), in_specs=[spec] * 3 + [HBM_SPEC],
        out_specs=[spec, spec, HBM_SPEC],
        out_shape=[jax.ShapeDtypeStruct((t, MH * HP), BF16), jax.ShapeDtypeStruct((t, MH * HP), F32),
                   jax.ShapeDtypeStruct((N_DEV,) + slab.shape, slab.dtype)],
        scratch_shapes=EXCHANGE_SEMS, compiler_params=_cparams(("arbitrary", "arbitrary")),
    )(q2, k2, v2, slab)


def _attn_bwd(q2, k2, v2, do2, o2, lse2, bsz, tsum):
    t = q2.shape[0]
    s = t // bsz
    tq, tk = _att_tiles(s, ATT_BWD_TILES)
    nq, nk, per, groups = s // tq, s // tk, max(tk // tq, 1), tk // HP
    scale = MQK ** -0.5
    c2 = scale * LOG2E
    heads = range(ATT_HEADS)

    def body(q_ref, k_ref, v_ref, do_ref, o_ref, lse_ref, t_ref, dq_ref, dk_ref, dv_ref, parts_ref, dq_acc, delta, lse_b2,
             send_sems, recv_sems, local_sem):
        exchange = _chip_exchange_copies(t_ref, parts_ref, send_sems, recv_sems, local_sem)
        first, last = _grid_ends((bsz, MH // ATT_HEADS))
        pl.when(first)(exchange.start)
        dq_acc[...] = jnp.zeros_like(dq_acc)

        def d_loop(i, carry):
            rows = pl.ds(pl.multiple_of(i * tq, tq), tq)
            for h in heads:
                hs = slice(h * HP, (h + 1) * HP)
                dl = jnp.sum(do_ref[rows, hs].astype(F32) * o_ref[rows, hs].astype(F32), axis=1, keepdims=True)
                delta[rows, hs] = jnp.broadcast_to(dl, (tq, HP))
            lse_b2[rows, :] = lse_ref[rows, :] * LOG2E
            return carry

        lax.fori_loop(0, nq, d_loop, 0)

        def k_loop(kj, carry):
            k0 = pl.multiple_of(kj * tk, tk)
            kk = [k_ref[pl.ds(k0, tk), h * HP:(h + 1) * HP] for h in heads]
            vv = [v_ref[pl.ds(k0, tk), h * HP:(h + 1) * HP] for h in heads]

            def tile(qi, c, masked):
                q0 = pl.multiple_of(qi * tq, tq)
                rows = pl.ds(q0, tq)
                out = []
                for h in heads:
                    hs = slice(h * HP, (h + 1) * HP)
                    dk, dv = c[h]
                    q = q_ref[rows, hs]
                    do = do_ref[rows, hs]
                    e = _dot(q, kk[h], NT) * c2 - _lanes(lse_b2[rows, hs], groups)
                    if masked:
                        e = jnp.where(_att_mask(q0, k0, tq, tk), e, NEG)
                    p = jnp.exp2(e)
                    dv = dv + _dot(p.astype(BF16), do, TN)
                    ds = (p * (_dot(do, vv[h], NT) - _lanes(delta[rows, hs], groups))).astype(BF16)
                    dq_acc[rows, hs] += _dot(ds, kk[h], NN)
                    dk = dk + _dot(ds, q, TN)
                    out.append((dk, dv))
                return tuple(out)

            zero = jnp.zeros((tk, HP), F32)
            c = tuple((zero, zero) for _ in heads)
            first = k0 // tq
            for u in range(per):
                c = tile(first + u, c, True)
            c = lax.fori_loop(first + per, nq, lambda qi, c: tile(qi, c, False), c)
            for h in heads:
                dk_ref[pl.ds(k0, tk), h * HP:(h + 1) * HP] = (c[h][0] * scale).astype(BF16)
                dv_ref[pl.ds(k0, tk), h * HP:(h + 1) * HP] = c[h][1].astype(BF16)
            return carry

        lax.fori_loop(0, nk, k_loop, 0)
        dq_ref[...] = (dq_acc[...] * scale).astype(BF16)
        pl.when(last)(exchange.finish)

    spec = pl.BlockSpec((s, ATT_HEADS * HP), lambda b, h: (b, h))
    return pl.pallas_call(
        body, name="attn_bwd", grid=(bsz, MH // ATT_HEADS), in_specs=[spec] * 6 + [HBM_SPEC],
        out_specs=[spec] * 3 + [HBM_SPEC],
        out_shape=[jax.ShapeDtypeStruct((t, MH * HP), BF16)] * 3 + [jax.ShapeDtypeStruct(tsum.shape, tsum.dtype)],
        scratch_shapes=[pltpu.VMEM((s, ATT_HEADS * HP), F32)] * 3 + EXCHANGE_SEMS,
        compiler_params=_cparams(("arbitrary", "arbitrary")),
    )(q2, k2, v2, do2, o2, lse2, tsum)


def _perm_w_in_t(w):
    z = lambda n: jnp.zeros((n, w.shape[1]), w.dtype)
    return jnp.concatenate([w[:3072], w[3504:5552], w[3088:3344], w[3344:3472], w[3072:3088], z(48), w[3472:3504], z(32)],
                           axis=0)


def _unperm_w_in_t(g):
    return jnp.concatenate([g[:3072], g[5504:5520], g[5120:5376], g[5376:5504], g[5568:5600], g[3072:5120]], axis=0)


def _pad_wa(w):
    return jnp.pad(w, ((0, LANES - GLR), (0, 0)))


def _pad_wuq(w):
    return jnp.pad(w.reshape(MQR, MH, MQK), ((0, 0), (0, 0), (0, HP - MQK))).reshape(MQR, MH * HP)


def _unpad_wuq(g):
    return g.reshape(MQR, MH, HP)[:, :, :MQK].reshape(MQR, MH * MQK)


def _pad_wukv(w):
    w3 = w.reshape(MKVR, MH, NOPE + MV)
    kp = jnp.pad(w3[:, :, :NOPE], ((0, 0), (0, 0), (0, HP - NOPE))).reshape(MKVR, MH * HP)
    vp = jnp.pad(w3[:, :, NOPE:], ((0, 0), (0, 0), (0, HP - MV))).reshape(MKVR, MH * HP)
    return jnp.concatenate([kp, vp], axis=1)


def _unpad_wukv(g):
    kp = g[:, :MH * HP].reshape(MKVR, MH, HP)[:, :, :NOPE]
    vp = g[:, MH * HP:].reshape(MKVR, MH, HP)[:, :, :MV]
    return jnp.concatenate([kp, vp], axis=2).reshape(MKVR, MH * (NOPE + MV))


def _pad_wo(w):
    return jnp.pad(w.reshape(MH, MV, D), ((0, 0), (0, HP - MV), (0, 0))).reshape(MH * HP, D)


def _unpad_wo(g):
    return g.reshape(MH, HP, D)[:, :MV, :].reshape(MH * MV, D)


def _pad_lanes(v, n=HP):
    return jnp.pad(v, ((0, 0), (0, n - v.shape[1])))


def _local_step(x, positions, tgt, wt, slab_b, sp, mod3):
    bsz, s, _ = x.shape
    t = bsz * s
    x2 = x.reshape(t, D)
    tgt2 = tgt.reshape(t, D)
    pos = positions.reshape(t, 1)
    fr16 = ROPE_THETA ** (-jnp.arange(0, ROPE, 2, dtype=F32) / ROPE)
    zero = lambda n: jnp.zeros((n,), F32)
    fr = jnp.concatenate([zero(NOPE), fr16, fr16, zero(HP - MQK)]).reshape(1, HP)
    sg = jnp.concatenate([zero(NOPE), -jnp.ones((ROPE // 2,), F32), jnp.ones((ROPE // 2,), F32), zero(HP - MQK)]).reshape(1, HP)

    w_in_t = _perm_w_in_t(wt["w_in"])
    wa_pad = _pad_wa(wt["gla_w_alpha"].T)
    wuq = _pad_wuq(wt["mla_w_uq"].T)
    wukv = _pad_wukv(wt["mla_w_ukv"].T)
    qn_g, kn_g = _pad_lanes(sp["mla_qn_g"]), _pad_lanes(sp["mla_kn_g"])

    _, h = _norm_mod_fwd(x2, sp["norm1_g"], mod3, 0, 1, "norm1_fwd")
    proj = _mm(h, w_in_t, "nt", (BF16,), "proj_fwd")
    o_gla, og, states = _gla_fwd(proj, wa_pad, sp["gla_b_alpha"], sp["gla_out_norm_g"], bsz)
    q2, k2, v2 = _mla_pre_fwd(proj, pos, fr, sg, sp["mla_q_lat_g"], sp["mla_kv_lat_g"], qn_g, kn_g, wuq, wukv)
    o2, lse2, core_row = _attn_fwd(q2, k2, v2, bsz, slab_b)
    wt = dict(wt, **_unpack_gathered(_cross_core_fill(core_row), SLAB_B))
    wo_pad = _pad_wo(wt["mla_w_o"])
    y_a = _mm(og, wt["gla_w_o"], "nn", (BF16,), "gla_out_fwd")
    y_b = _mm(o2, wo_pad, "nn", (BF16,), "mla_out_fwd")
    mix = _merge_fwd(proj, sp["b_merge"], y_a, y_b)
    mixed = _mm(mix, wt["w_out"], "nn", (F32,), "w_out_fwd")

    x1, h2 = _norm_mod_fwd(x2, sp["norm2_g"], mod3, 3, 4, "norm2_fwd", mixed=mixed, i_gate=2)
    a, f = _mm(h2, wt["mlp_w1"], "nt", (BF16, BF16), "mlp1_fwd",
               epi=lambda acc: (acc, jnp.square(jnp.maximum(acc, 0.0))))
    ff = _mm(f, wt["mlp_w2"], "nn", (F32,), "mlp2_fwd")
    dy, dff, acc_g2, acc_loss = _loss_head(x1, ff, tgt2, mod3)

    gw = {}
    gw["mlp_w2"] = _mm(f, dff, "tn", (BF16,), "mlp2_dw")
    da = _mm(dff, wt["mlp_w2"], "nt", (BF16,), "mlp2_dx", extras=(a,),
             epi=lambda acc, av: (acc * (2.0 * jnp.maximum(av.astype(F32), 0.0)),))
    gw["mlp_w1"] = _mm(da, h2, "tn", (BF16,), "mlp1_dw")
    dh2 = _mm(da, wt["mlp_w1"], "nn", (F32,), "mlp1_dx")
    dx1, dmixed, accb2, accg2 = _norm_mod_bwd(x1, dh2, dy, sp["norm2_g"], mod3, 3, 4, "norm2_bwd", mixed=mixed, i_gate=2)

    gw["w_out"] = _mm(mix, dmixed, "tn", (BF16,), "w_out_dw")
    dmix = _mm(dmixed, wt["w_out"], "nt", (BF16,), "w_out_dx")
    dy_a, dy_b, dlogits, acc_bm = _merge_bwd(dmix, proj, sp["b_merge"], y_a, y_b)
    gw["gla_w_o"] = _mm(og, dy_a, "tn", (BF16,), "gla_out_dw")
    dog = _mm(dy_a, wt["gla_w_o"], "nt", (BF16,), "gla_out_dx")
    gw["mla_w_o"] = _unpad_wo(_mm(o2, dy_b, "tn", (BF16,), "mla_out_dw"))
    do2 = _mm(dy_b, wo_pad, "nt", (BF16,), "mla_out_dx")
    dq2, dk2, dv2, parts_b = _attn_bwd(q2, k2, v2, do2, o2, lse2, bsz, _sum_over_cores(_pack_per_device(gw, SLAB_B), "b"))
    dq_g, dk_g, dv_g, dgg, dz, acc_ba, acc_go = _gla_bwd(dog, o_gla, proj, wa_pad, sp["gla_b_alpha"],
                                                         sp["gla_out_norm_g"], states, bsz)
    gw["gla_w_alpha"] = _mm(proj, dz, "tn", (F32,), "gla_alpha_dw", a_off=43, m=LANES)[:GLR].T.astype(BF16)
    dmisc_gla = _mm(dz, wa_pad, "nt", (F32,), "gla_alpha_dx")
    dcq, dckv, dmisc, gwuq, gwukv, acc_mla = _mla_pre_bwd(dq2, dk2, dv2, dmisc_gla, proj, pos, fr, sg, sp["mla_q_lat_g"],
                                                         sp["mla_kv_lat_g"], qn_g, kn_g, wuq, wukv)
    gw["mla_w_uq"] = _unpad_wuq(gwuq).T.astype(BF16)
    gw["mla_w_ukv"] = _unpad_wukv(gwukv).T.astype(BF16)
    dproj = jnp.concatenate([dq_g, dk_g, dv_g, dgg, dlogits, dcq, dckv, dmisc], axis=1)
    gw["w_in"] = _unperm_w_in_t(_mm(dproj, h, "tn", (BF16,), "proj_dw"))
    dh, parts_a = _mm(dproj, w_in_t, "nn", (F32,), "proj_dx", cargo=_sum_over_cores(_pack_per_device(gw, SLAB_A), "a"))
    grad_x, accb1, accg1 = _norm_mod_bwd(x2, dh, dx1, sp["norm1_g"], mod3, 0, 1, "norm1_bwd")

    dmod = jnp.stack([accb1[:, 0], accb1[:, 1], accb2[:, 2], accb2[:, 0], accb2[:, 1], acc_g2[:, 0]], axis=1)

    rows = {
        "dmod": dmod.reshape(bsz * 6, D),
        "norm1_g": accg1[0:1], "norm2_g": accg2[0:1],
        "b_merge": acc_bm[0:1].reshape(2, D),
        "gla_b_alpha": _pad_lanes(acc_ba[0:1], D),
        "gla_out_norm_g": _pad_lanes(acc_go.reshape(GH, 8, GDV)[:, 0, :], D),
        "mla_q_lat_g": _pad_lanes(acc_mla[0:1], D), "mla_kv_lat_g": _pad_lanes(acc_mla[1:2], D),
        "mla_qn_g": _pad_lanes(acc_mla[2:3], D), "mla_kn_g": _pad_lanes(acc_mla[3:4], D),
        "loss": acc_loss[0:1],
    }
    return grad_x.reshape(bsz, s, D), parts_a, parts_b, rows


HBM_SPEC = pl.BlockSpec(memory_space=pltpu.HBM)


def _all_gather(p, name):
    r, cdim = p.shape

    def body(p_ref, out_ref, send_sems, recv_sems, local_sem):
        x, y, c = lax.axis_index("x"), lax.axis_index("y"), lax.axis_index("c")
        me, sibling = (x, y, c), (x, y, 1 - c)
        chips = [(1 - x, y), (x, 1 - y), (1 - x, 1 - y)]

        def slot(px, py, pc):
            return out_ref.at[4 * px + 2 * py + pc]

        def copy(k, block, to, src=None):
            return pltpu.make_async_remote_copy(
                src_ref=slot(*block) if src is None else src, dst_ref=slot(*block),
                send_sem=send_sems.at[k], recv_sem=recv_sems.at[k], device_id=to, device_id_type=MESH)

        mine = pltpu.make_async_copy(p_ref, slot(*me), local_sem)
        mine.start()
        first = [copy(0, me, sibling, src=p_ref)] + [copy(1 + j, me, (*chip, c), src=p_ref) for j, chip in enumerate(chips)]
        for cp in first:
            cp.start()
        passed = [copy(4 + j, (*chip, c), sibling) for j, chip in enumerate(chips)]
        for j, chip in enumerate(chips):
            copy(1 + j, (*chip, c), me).wait_recv()
            passed[j].start()
        copy(0, sibling, me).wait_recv()
        for j, chip in enumerate(chips):
            copy(4 + j, (*chip, 1 - c), me).wait_recv()
        for cp in first + passed:
            cp.wait_send()
        mine.wait()

    return pl.pallas_call(
        body, name=name, out_shape=jax.ShapeDtypeStruct((N_DEV, r, cdim), p.dtype),
        in_specs=[HBM_SPEC], out_specs=HBM_SPEC,
        scratch_shapes=[pltpu.SemaphoreType.DMA((7,)), pltpu.SemaphoreType.DMA((7,)), pltpu.SemaphoreType.DMA(())],
    )(p)


def _sibling_exchange(g, name):
    def body(g_ref, out_ref, send_sem, recv_sem):
        x, y, c = lax.axis_index("x"), lax.axis_index("y"), lax.axis_index("c")
        cp = pltpu.make_async_remote_copy(src_ref=g_ref, dst_ref=out_ref, send_sem=send_sem, recv_sem=recv_sem,
                                          device_id=(x, y, 1 - c), device_id_type=MESH)
        cp.start()
        cp.wait()

    return pl.pallas_call(
        body, name=name, out_shape=jax.ShapeDtypeStruct(g.shape, g.dtype),
        in_specs=[HBM_SPEC], out_specs=HBM_SPEC,
        scratch_shapes=[pltpu.SemaphoreType.DMA(()), pltpu.SemaphoreType.DMA(())],
    )(g)


class _Exchange:
    def __init__(self, local, sends, arrivals):
        self.local, self.sends, self.arrivals = local, sends, arrivals

    def start(self):
        self.local.start()
        for cp in self.sends:
            cp.start()

    def finish(self):
        for cp in self.arrivals:
            cp.wait_recv()
        for cp in self.sends:
            cp.wait_send()
        self.local.wait()


EXCHANGE_SEMS = [pltpu.SemaphoreType.DMA((4,)), pltpu.SemaphoreType.DMA((4,)), pltpu.SemaphoreType.DMA(())]


def _chip_exchange_copies(t_ref, out_ref, send_sems, recv_sems, local_sem):
    x, y, c = lax.axis_index("x"), lax.axis_index("y"), lax.axis_index("c")
    my_chip = 2 * x + y
    chips = [(1 - x, y), (x, 1 - y), (1 - x, 1 - y)]

    def copy(j, src, dst, px, py):
        return pltpu.make_async_remote_copy(src_ref=t_ref.at[src], dst_ref=out_ref.at[dst], send_sem=send_sems.at[j],
                                            recv_sem=recv_sems.at[j], device_id=(px, py, c), device_id_type=MESH)

    return _Exchange(pltpu.make_async_copy(t_ref.at[my_chip], out_ref.at[my_chip], local_sem),
                     [copy(j, 2 * px + py, my_chip, px, py) for j, (px, py) in enumerate(chips)],
                     [copy(j, my_chip, 2 * px + py, px, py) for j, (px, py) in enumerate(chips)])


def _core_row_gather_copies(p_ref, out_ref, send_sems, recv_sems, local_sem):
    x, y, c = lax.axis_index("x"), lax.axis_index("y"), lax.axis_index("c")
    peers = [(x, y, 1 - c), (1 - x, y, c), (x, 1 - y, c), (1 - x, 1 - y, c)]

    def slot(px, py, pc):
        return out_ref.at[4 * px + 2 * py + pc]

    def copy(j, block, to):
        return pltpu.make_async_remote_copy(src_ref=p_ref, dst_ref=slot(*block), send_sem=send_sems.at[j],
                                            recv_sem=recv_sems.at[j], device_id=to, device_id_type=MESH)

    return _Exchange(pltpu.make_async_copy(p_ref, slot(x, y, c), local_sem),
                     [copy(j, (x, y, c), peer) for j, peer in enumerate(peers)],
                     [copy(j, peer, peer) for j, peer in enumerate(peers)])


def _cross_core_fill(gathered):
    def body(g_ref, out_ref, send_sems, recv_sems):
        x, y, c = lax.axis_index("x"), lax.axis_index("y"), lax.axis_index("c")
        chips = [(1 - x, y), (x, 1 - y), (1 - x, 1 - y)]

        def copy(j, pc):
            px, py = chips[j]
            slot = 4 * px + 2 * py + pc
            return pltpu.make_async_remote_copy(src_ref=g_ref.at[slot], dst_ref=out_ref.at[slot], send_sem=send_sems.at[j],
                                                recv_sem=recv_sems.at[j], device_id=(x, y, 1 - c), device_id_type=MESH)

        sends = [copy(j, c) for j in range(3)]
        for cp in sends:
            cp.start()
        for j in range(3):
            copy(j, 1 - c).wait_recv()
        for cp in sends:
            cp.wait_send()

    return pl.pallas_call(
        body, name="weights_cross_core_fill", out_shape=jax.ShapeDtypeStruct(gathered.shape, gathered.dtype),
        in_specs=[HBM_SPEC], out_specs=HBM_SPEC, input_output_aliases={0: 0},
        scratch_shapes=[pltpu.SemaphoreType.DMA((3,)), pltpu.SemaphoreType.DMA((3,))],
    )(gathered)


def _slab_block(r):
    return max(b for b in range(16, SLAB_BLOCK_MAX + 1, 16) if r % b == 0)


def _pair_sum(a, b, name):
    n, r, cdim = a.shape
    rb = _slab_block(r)
    blk = pl.BlockSpec((1, rb, cdim), lambda j, i: (j, i, 0))

    def body(a_ref, b_ref, o_ref):
        o_ref[...] = (a_ref[...].astype(F32) + b_ref[...].astype(F32)).astype(BF16)

    return pl.pallas_call(
        body, name=name, grid=(n, r // rb), in_specs=[blk, blk], out_specs=blk,
        out_shape=jax.ShapeDtypeStruct(a.shape, BF16), compiler_params=_cparams(("arbitrary", "arbitrary")),
    )(a, b)


def _adamw_math(w, g, m, v):
    m = ADAM_B1 * m + (1.0 - ADAM_B1) * g
    v = ADAM_B2 * v + (1.0 - ADAM_B2) * jnp.square(g)
    m_hat = m / (1.0 - ADAM_B1 ** ADAM_STEP)
    v_hat = v / (1.0 - ADAM_B2 ** ADAM_STEP)
    delta = -ADAM_LR * (m_hat / (jnp.sqrt(v_hat) + ADAM_EPS) + ADAM_WD * w)
    return delta, m, v


def _slab_sum(parts, name):
    _, r, cdim = parts.shape
    rb = _slab_block(r)
    blk = pl.BlockSpec((rb, cdim), lambda i: (i, 0))

    def body(p_ref, g_out):
        g = p_ref[0].astype(F32)
        for j in range(1, 4):
            g = g + p_ref[j].astype(F32)
        g_out[...] = g

    return pl.pallas_call(
        body, name=name, grid=(r // rb,),
        in_specs=[pl.BlockSpec((4, rb, cdim), lambda i: (0, i, 0))], out_specs=blk,
        out_shape=jax.ShapeDtypeStruct((r, cdim), F32), compiler_params=_cparams(("arbitrary",)),
    )(parts)


def _adamw(g, w, m, v, name):
    r, cdim = w.shape
    rb = _tile(r, 256)
    blk = pl.BlockSpec((rb, cdim), lambda i: (i, 0))

    def body(g_ref, w_ref, m_ref, v_ref, d_out, m_out, v_out):
        d_out[...], m_out[...], v_out[...] = _adamw_math(w_ref[...], g_ref[...], m_ref[...], v_ref[...])

    return pl.pallas_call(
        body, name=name, grid=(r // rb,), in_specs=[blk] * 4, out_specs=[blk] * 3,
        out_shape=[jax.ShapeDtypeStruct((r, cdim), F32)] * 3, compiler_params=_cparams(("arbitrary",)),
    )(g, w, m, v)


def _adamw_small(parts, w, m, v):
    def body(p_ref, w_ref, m_ref, v_ref, g_out, d_out, m_out, v_out, loss_out):
        def total(srcs):
            acc = None
            for r in srcs:
                for j in range(N_DEV):
                    term = p_ref[j, r:r + 1, :]
                    acc = term if acc is None else acc + term
            return acc

        for prow, srcs in enumerate(SMALL_SOURCES):
            one = slice(prow, prow + 1)
            g = total(srcs)
            g_out[one, :] = g
            d_out[one, :], m_out[one, :], v_out[one, :] = _adamw_math(w_ref[one, :], g, m_ref[one, :], v_ref[one, :])
        loss_out[...] = jnp.broadcast_to(jnp.sum(total(LOSS_SOURCE), axis=1, keepdims=True), (8, LANES))

    full = lambda shp: pl.BlockSpec(shp, lambda i: (0,) * len(shp))
    return pl.pallas_call(
        body, name="adamw_small", grid=(1,),
        in_specs=[full((N_DEV, SMALL_ROWS, D)), full((16, D)), full((16, D)), full((16, D))],
        out_specs=[full((16, D))] * 4 + [full((8, LANES))],
        out_shape=[jax.ShapeDtypeStruct((16, D), F32)] * 4 + [jax.ShapeDtypeStruct((8, LANES), F32)],
        compiler_params=_cparams(("arbitrary",)),
    )(parts, w, m, v)


def _to_slab(shard, form):
    if form == "N":
        return shard
    return shard.T if form == "T" else shard.T.reshape(-1, D)


def _from_slab(block, form, shard_shape):
    if form == "N":
        return block
    return block.T if form == "T" else block.reshape(shard_shape[1], shard_shape[0]).T


def _gathered_full(g, form, shard_shape):
    if form == "TR":
        return g.reshape(N_DEV * shard_shape[1], shard_shape[0])
    return g.reshape(N_DEV * g.shape[1], D)


def _pack_slab(shards, layout):
    return jnp.concatenate([jnp.zeros((r, D), BF16) if n is None else _to_slab(shards[n], form).astype(BF16)
                            for n, r, form in layout], axis=0)


def _unpack_gathered(gathered, layout):
    out, off = {}, 0
    for n, r, form in layout:
        if n is not None:
            out[n] = _gathered_full(gathered[:, off:off + r], form, SHARD_SHAPES[n])
        off += r
    return out


def _pack_per_device(gw, layout):
    return jnp.concatenate([jnp.zeros((N_DEV, r, D), BF16) if n is None else gw[n].reshape(N_DEV, r, D)
                            for n, r, _ in layout], axis=1)


def _unpack_shards(gslab, layout):
    out, off = {}, 0
    for n, r, form in layout:
        if n is not None:
            out[n] = _from_slab(gslab[off:off + r], form, SHARD_SHAPES[n])
        off += r
    return out


def _sum_over_cores(per_dev, tag):
    my_c = lax.axis_index("c")
    pairs = per_dev.reshape(4, 2, per_dev.shape[1], D)
    keep = lax.dynamic_index_in_dim(pairs, my_c, axis=1, keepdims=False)
    give = lax.dynamic_index_in_dim(pairs, 1 - my_c, axis=1, keepdims=False)
    return _pair_sum(keep, _sibling_exchange(give, "rs_sibling_exchange_" + tag), "rs_pair_sum_" + tag)


def _small_pack(vals):
    rows = []
    for n in SMALL:
        v = vals[n].reshape(-1)
        k = -(-v.shape[0] // D)
        rows.append(jnp.pad(v, (0, k * D - v.shape[0])).reshape(k, D))
    return jnp.concatenate(rows, axis=0)


def _small_unpack(packed, shapes):
    out = {}
    for n in SMALL:
        k = shapes[n][-1]
        r0 = SMALL_ROW[n]
        out[n] = packed[r0:r0 + -(-k // D)].reshape(-1)[:k].reshape(shapes[n])
    return out


def kernel(x, c, positions, w_ada, b_ada, norm1_g, w_in, b_merge, gla_w_alpha, gla_b_alpha, gla_out_norm_g, gla_w_o, mla_q_lat_g, mla_w_uq, mla_kv_lat_g, mla_w_ukv, mla_qn_g, mla_kn_g, mla_w_o, w_out, norm2_g, mlp_w1, mlp_w2, loss_target, m_w_ada, m_b_ada, m_norm1_g, m_w_in, m_b_merge, m_gla_w_alpha, m_gla_b_alpha, m_gla_out_norm_g, m_gla_w_o, m_mla_q_lat_g, m_mla_w_uq, m_mla_kv_lat_g, m_mla_w_ukv, m_mla_qn_g, m_mla_kn_g, m_mla_w_o, m_w_out, m_norm2_g, m_mlp_w1, m_mlp_w2, v_w_ada, v_b_ada, v_norm1_g, v_w_in, v_b_merge, v_gla_w_alpha, v_gla_b_alpha, v_gla_out_norm_g, v_gla_w_o, v_mla_q_lat_g, v_mla_w_uq, v_mla_kv_lat_g, v_mla_w_ukv, v_mla_qn_g, v_mla_kn_g, v_mla_w_o, v_w_out, v_norm2_g, v_mlp_w1, v_mlp_w2):
    args = dict(locals())
    wts = {n: args[n][0] for n in WEIGHTS}
    mom = {n: args["m_" + n][0] for n in WEIGHTS}
    var = {n: args["v_" + n][0] for n in WEIGHTS}
    my_c = lax.axis_index("c")
    my_dev = 4 * lax.axis_index("x") + 2 * lax.axis_index("y") + my_c
    bsz = x.shape[0]
    sp = {n: wts[n].reshape(1, -1) for n in SMALL}

    wt = _unpack_gathered(_all_gather(_pack_slab(wts, SLAB_A), "weights_all_gather"), SLAB_A)

    c_all = _all_gather(jnp.pad(c, ((0, 8 - bsz), (0, 0))), "cond_all_gather")[:, :bsz].reshape(N_DEV * bsz, D)
    bias = lax.dynamic_slice_in_dim(sp["b_ada"], my_dev * ADA_COLS, ADA_COLS, axis=1)
    mod_cols = _mm(c_all, wts["w_ada"], "nn", (F32,), "ada_fwd", pro=_silu, epi=lambda acc, b: (acc + b,),
                   extras=(jnp.broadcast_to(bias, (N_DEV * bsz, ADA_COLS)),))
    mod_all = _all_gather(mod_cols, "mod_all_gather")
    mod_mine = lax.dynamic_slice_in_dim(mod_all, my_dev * bsz, bsz, axis=1)
    mod3 = jnp.transpose(mod_mine, (1, 0, 2)).reshape(bsz, 6, D)

    grad_x, parts_a, parts_b, rows = _local_step(x, positions, loss_target, wt, _pack_slab(wts, SLAB_B), sp, mod3)

    grads = dict(_unpack_shards(_slab_sum(parts_a, "rs_slab_sum_a"), SLAB_A),
                 **_unpack_shards(_slab_sum(parts_b, "rs_slab_sum_b"), SLAB_B))
    big = {n: (g,) + tuple(_adamw(g, wts[n], mom[n], var[n], "adamw_" + n)) for n, g in grads.items()}

    order = ["dmod", "norm1_g", "norm2_g", "b_merge", "gla_b_alpha", "gla_out_norm_g", "mla_q_lat_g", "mla_kv_lat_g",
             "mla_qn_g", "mla_kn_g", "loss"]
    part_rows = jnp.concatenate([rows[n] for n in order], axis=0)
    part_rows = jnp.pad(part_rows, ((0, SMALL_ROWS - part_rows.shape[0]), (0, 0)))
    all_rows = _all_gather(part_rows, "partials_all_gather")

    dmod_all = all_rows[:, :6 * bsz].reshape(N_DEV * bsz, 6 * D)
    dmod_cols = lax.dynamic_slice_in_dim(dmod_all, my_dev * ADA_COLS, ADA_COLS, axis=1)
    g_ada = _mm(c_all, dmod_cols, "tn", (F32,), "ada_dw", pro=_silu)
    big["w_ada"] = (g_ada,) + tuple(_adamw(g_ada, wts["w_ada"], mom["w_ada"], var["w_ada"], "adamw_w_ada"))

    small = _adamw_small(all_rows, _small_pack({n: wts[n] for n in SMALL}), _small_pack({n: mom[n] for n in SMALL}),
                         _small_pack({n: var[n] for n in SMALL}))
    loss = small[4][0, 0]
    small_shapes = {n: wts[n].shape for n in SMALL}
    small = [_small_unpack(o, small_shapes) for o in small[:4]]

    outs = [loss, grad_x]
    for k in range(4):
        for n in WEIGHTS:
            val = big[n][k] if n in BIG else small[k][n]
            outs.append(val.reshape((1,) + tuple(wts[n].shape)))
    return tuple(outs)
```

```python
import functools

import jax
import jax.numpy as jnp
from jax import lax
from jax.experimental import pallas as pl
from jax.experimental.pallas import tpu as pltpu

F32 = jnp.float32
BF16 = jnp.bfloat16
MESH = pl.DeviceIdType.MESH

D = 1024
EPS = 1e-6
CHUNK = 64
GH, GDK, GDV, GLR, GTAU = 4, 128, 256, 16, 16.0
MH, MQR, MKVR, NOPE, ROPE, MV = 16, 256, 128, 64, 32, 64
MQK = NOPE + ROPE
HP = 128
FF = 4 * D
ROPE_THETA = 10000.0
IN_WIDTH = 5552
PW = 5632
N_DEV = 8
LANES = 128
SLAB_BLOCK_MAX = 400
ADA_COLS = 6 * D // N_DEV
SMALL_ROWS = 32
SMALL_SOURCES = tuple([(r, 6 + r) for r in range(6)] + [(12,), (13,), (14,), (15,), (16,), (17, 18, 19, 20),
                                                         (21,), (22,), (23,), (24,)])
LOSS_SOURCE = (25,)
VMEM_LIMIT = 56 * 1024 * 1024

ADAM_LR, ADAM_B1, ADAM_B2, ADAM_EPS, ADAM_WD, ADAM_STEP = 0.001, 0.9, 0.999, 1e-08, 0.01, 10

SLAB_A = (("w_in", 694, "T"), ("gla_w_alpha", 1, "TR"), (None, 9, None), ("mla_w_uq", 48, "TR"), ("mla_w_ukv", 32, "TR"))
SLAB_B = (("mlp_w1", 512, "T"), ("gla_w_o", 128, "N"), ("mla_w_o", 128, "N"), ("w_out", 128, "N"), ("mlp_w2", 512, "N"))
BIG = ("w_ada",) + tuple(n for n, _, _ in SLAB_A + SLAB_B if n is not None)
SHARD_SHAPES = {"w_ada": (D, 6 * D // N_DEV), "w_in": (D, IN_WIDTH // N_DEV), "gla_w_alpha": (GLR, GH * GDK // N_DEV),
                "gla_w_o": (GH * GDV // N_DEV, D), "mla_w_uq": (MQR, MH * MQK // N_DEV),
                "mla_w_ukv": (MKVR, MH * (NOPE + MV) // N_DEV), "mla_w_o": (MH * MV // N_DEV, D), "w_out": (D // N_DEV, D),
                "mlp_w1": (D, FF // N_DEV), "mlp_w2": (FF // N_DEV, D)}
SMALL = ("b_ada", "norm1_g", "norm2_g", "b_merge", "gla_b_alpha", "gla_out_norm_g", "mla_q_lat_g", "mla_kv_lat_g",
         "mla_qn_g", "mla_kn_g")
SMALL_ROW = {"b_ada": 0, "norm1_g": 6, "norm2_g": 7, "b_merge": 8, "gla_b_alpha": 10, "gla_out_norm_g": 11,
             "mla_q_lat_g": 12, "mla_kv_lat_g": 13, "mla_qn_g": 14, "mla_kn_g": 15}
WEIGHTS = ("w_ada", "b_ada", "norm1_g", "w_in", "b_merge", "gla_w_alpha", "gla_b_alpha", "gla_out_norm_g", "gla_w_o",
           "mla_q_lat_g", "mla_w_uq", "mla_kv_lat_g", "mla_w_ukv", "mla_qn_g", "mla_kn_g", "mla_w_o", "w_out",
           "norm2_g", "mlp_w1", "mlp_w2")


def _cparams(sem=None):
    return pltpu.CompilerParams(dimension_semantics=sem, vmem_limit_bytes=VMEM_LIMIT)


def _tile(n, pref):
    for t in (2048, 1024, 512, 256, 128):
        if t <= pref and n % t == 0:
            return t
    return n


def _dot(a, b, dims, precision=None):
    return lax.dot_general(a, b, (dims, ((), ())), preferred_element_type=F32, precision=precision)


NN = ((1,), (0,))
NT = ((1,), (1,))
TN = ((0,), (0,))


def _sigmoid(x):
    return 1.0 / (1.0 + jnp.exp(-x))


def _silu(x):
    return x * _sigmoid(x)


def _mm(a, b, mode, out_dtypes, name, *, pro=None, pro_b=None, epi=None, extras=(), a_off=0, m=None, tm=2048, tn=1024,
        tk=1024, cargo=None):
    if mode == "tn":
        kc, n = b.shape
        m = a.shape[1] if m is None else m
    elif mode == "nn":
        m, kc = a.shape
        n = b.shape[1]
    else:
        m, kc = a.shape
        n = b.shape[0]
    tm, tn, tk = _tile(m, tm), _tile(n, tn), _tile(kc, tk)
    nk = kc // tk
    dims = {"nn": NN, "nt": NT, "tn": TN}[mode]
    if mode == "tn":
        a_spec = pl.BlockSpec((tk, tm), lambda i, j, k: (k, i + a_off))
    else:
        a_spec = pl.BlockSpec((tm, tk), lambda i, j, k: (i + a_off, k))
    if mode == "nt":
        b_spec = pl.BlockSpec((tn, tk), lambda i, j, k: (j, k))
    else:
        b_spec = pl.BlockSpec((tk, tn), lambda i, j, k: (k, j))
    o_spec = pl.BlockSpec((tm, tn), lambda i, j, k: (i, j))
    n_ex, n_out = len(extras), len(out_dtypes)
    grid = (m // tm, n // tn, nk)
    has_cargo = cargo is not None

    def body(a_ref, b_ref, *rest):
        ex, rest = rest[:n_ex], rest[n_ex:]
        if has_cargo:
            outs, acc = rest[1:1 + n_out], rest[2 + n_out]
            exchange = _chip_exchange_copies(rest[0], rest[1 + n_out], *rest[3 + n_out:])
            steps = [pl.program_id(axis) for axis in range(3)]
            first = (steps[0] == 0) & (steps[1] == 0) & (steps[2] == 0)
            last = (steps[0] == grid[0] - 1) & (steps[1] == grid[1] - 1) & (steps[2] == grid[2] - 1)
            pl.when(first)(exchange.start)
        else:
            outs, acc = rest[:n_out], rest[n_out]
        k = pl.program_id(2)

        @pl.when(k == 0)
        def _():
            acc[...] = jnp.zeros_like(acc)

        av = a_ref[...]
        if pro is not None:
            av = pro(av)
        bv = b_ref[...]
        if pro_b is not None:
            bv = pro_b(bv)
        acc[...] += _dot(av.astype(BF16), bv.astype(BF16), dims)

        @pl.when(k == nk - 1)
        def _():
            res = (acc[...],) if epi is None else epi(acc[...], *[e[...] for e in ex])
            for o_ref, r in zip(outs, res):
                o_ref[...] = r.astype(o_ref.dtype)

        if has_cargo:
            pl.when(last)(exchange.finish)

    cargo_in = [cargo] if has_cargo else []
    cargo_spec = [HBM_SPEC] * len(cargo_in)
    out = pl.pallas_call(
        body, name=name, grid=grid,
        in_specs=[a_spec, b_spec] + [o_spec] * n_ex + cargo_spec,
        out_specs=[o_spec] * n_out + cargo_spec,
        out_shape=[jax.ShapeDtypeStruct((m, n), dt) for dt in out_dtypes]
        + [jax.ShapeDtypeStruct(c.shape, c.dtype) for c in cargo_in],
        scratch_shapes=[pltpu.VMEM((tm, tn), F32)] + (EXCHANGE_SEMS if has_cargo else []),
        compiler_params=_cparams(("arbitrary",) * 3 if has_cargo else ("parallel", "parallel", "arbitrary")),
    )(a, b, *extras, *cargo_in)
    return out[0] if len(out) == 1 else out


def _rows(s):
    return _tile(s, 512)


def _mod_spec():
    return pl.BlockSpec((1, 6, D), lambda b, i: (b, 0, 0))


def _tok_spec(tr, nb, width=D, col=0):
    return pl.BlockSpec((tr, width), lambda b, i: (b * nb + i, col))


def _norm_mod_fwd(x, g, mod3, i_shift, i_scale, name, mixed=None, i_gate=None):
    bsz, _, _ = mod3.shape
    t = x.shape[0]
    s = t // bsz
    tr = _rows(s)
    nb = s // tr
    has_res = mixed is not None

    def body(*refs):
        if has_res:
            x_ref, mx_ref, g_ref, mod_ref, x1_ref, h_ref = refs
            xv = x_ref[...] + mod_ref[0, i_gate:i_gate + 1, :] * mx_ref[...]
            x1_ref[...] = xv
        else:
            x_ref, g_ref, mod_ref, h_ref = refs
            xv = x_ref[...]
        r = lax.rsqrt(jnp.mean(xv * xv, axis=1, keepdims=True) + EPS)
        hn = (xv * r) * g_ref[...]
        h = hn * (1.0 + mod_ref[0, i_scale:i_scale + 1, :]) + mod_ref[0, i_shift:i_shift + 1, :]
        h_ref[...] = h.astype(BF16)

    tok = _tok_spec(tr, nb)
    gspec = pl.BlockSpec((1, D), lambda b, i: (0, 0))
    ins = [x] + ([mixed] if has_res else []) + [g, mod3]
    in_specs = [tok] + ([tok] if has_res else []) + [gspec, _mod_spec()]
    out_shape = ([jax.ShapeDtypeStruct((t, D), F32)] if has_res else []) + [jax.ShapeDtypeStruct((t, D), BF16)]
    out = pl.pallas_call(
        body, name=name, grid=(bsz, nb), in_specs=in_specs, out_specs=[tok] * len(out_shape), out_shape=out_shape,
        compiler_params=_cparams(("arbitrary", "arbitrary")),
    )(*ins)
    return (out[0], out[1]) if has_res else (None, out[0])


def _norm_mod_bwd(x, dh, dres, g, mod3, i_shift, i_scale, name, mixed=None, i_gate=None):
    bsz = mod3.shape[0]
    t = x.shape[0]
    s = t // bsz
    tr = _rows(s)
    nb = s // tr
    has_res = mixed is not None

    def body(*refs):
        if has_res:
            x_ref, dh_ref, dres_ref, mx_ref, g_ref, mod_ref, dx_ref, dmx_ref, accb, accg = refs
        else:
            x_ref, dh_ref, dres_ref, g_ref, mod_ref, dx_ref, accb, accg = refs
        b, i = pl.program_id(0), pl.program_id(1)

        @pl.when(i == 0)
        def _():
            accb[...] = jnp.zeros_like(accb)

        @pl.when((i == 0) & (b == 0))
        def _():
            accg[...] = jnp.zeros_like(accg)

        xv, dhv, gv = x_ref[...], dh_ref[...], g_ref[...]
        r = lax.rsqrt(jnp.mean(xv * xv, axis=1, keepdims=True) + EPS)
        xn = xv * r
        accb[0, 0:1, :] += jnp.sum(dhv, axis=0, keepdims=True)
        accb[0, 1:2, :] += jnp.sum(dhv * (xn * gv), axis=0, keepdims=True)
        tt = dhv * (1.0 + mod_ref[0, i_scale:i_scale + 1, :])
        accg[0:1, :] += jnp.sum(tt * xn, axis=0, keepdims=True)
        dxn = tt * gv
        dx = dres_ref[...] + r * (dxn - xn * jnp.mean(dxn * xn, axis=1, keepdims=True))
        dx_ref[...] = dx
        if has_res:
            accb[0, 2:3, :] += jnp.sum(dx * mx_ref[...], axis=0, keepdims=True)
            dmx_ref[...] = (dx * mod_ref[0, i_gate:i_gate + 1, :]).astype(BF16)

    tok = _tok_spec(tr, nb)
    gspec = pl.BlockSpec((1, D), lambda b, i: (0, 0))
    ins = [x, dh, dres] + ([mixed] if has_res else []) + [g, mod3]
    in_specs = [tok] * (4 if has_res else 3) + [gspec, _mod_spec()]
    out_shape = [jax.ShapeDtypeStruct((t, D), F32)] + ([jax.ShapeDtypeStruct((t, D), BF16)] if has_res else [])
    out_specs = [tok] * len(out_shape)
    out_shape += [jax.ShapeDtypeStruct((bsz, 8, D), F32), jax.ShapeDtypeStruct((8, D), F32)]
    out_specs += [pl.BlockSpec((1, 8, D), lambda b, i: (b, 0, 0)), pl.BlockSpec((8, D), lambda b, i: (0, 0))]
    return pl.pallas_call(
        body, name=name, grid=(bsz, nb), in_specs=in_specs, out_specs=out_specs, out_shape=out_shape,
        compiler_params=_cparams(("arbitrary", "arbitrary")),
    )(*ins)


def _loss_head(x1, ff, tgt, mod3):
    bsz = mod3.shape[0]
    t = x1.shape[0]
    s = t // bsz
    tr = _rows(s)
    nb = s // tr

    def body(x1_ref, ff_ref, tg_ref, mod_ref, dy_ref, dff_ref, accb, accl):
        b, i = pl.program_id(0), pl.program_id(1)

        @pl.when(i == 0)
        def _():
            accb[...] = jnp.zeros_like(accb)

        @pl.when((i == 0) & (b == 0))
        def _():
            accl[...] = jnp.zeros_like(accl)

        gate = mod_ref[0, 5:6, :]
        ffv = ff_ref[...]
        err = x1_ref[...] + gate * ffv - tg_ref[...]
        accl[0:1, :] += jnp.sum(err * err, axis=0, keepdims=True) * (0.5 / D)
        dy = err * (1.0 / D)
        dy_ref[...] = dy
        dff_ref[...] = (dy * gate).astype(BF16)
        accb[0, 0:1, :] += jnp.sum(dy * ffv, axis=0, keepdims=True)

    tok = _tok_spec(tr, nb)
    return pl.pallas_call(
        body, name="loss_head", grid=(bsz, nb), in_specs=[tok, tok, tok, _mod_spec()],
        out_specs=[tok, tok, pl.BlockSpec((1, 8, D), lambda b, i: (b, 0, 0)), pl.BlockSpec((8, D), lambda b, i: (0, 0))],
        out_shape=[jax.ShapeDtypeStruct((t, D), F32), jax.ShapeDtypeStruct((t, D), BF16),
                   jax.ShapeDtypeStruct((bsz, 8, D), F32), jax.ShapeDtypeStruct((8, D), F32)],
        compiler_params=_cparams(("arbitrary", "arbitrary")),
    )(x1, ff, tgt, mod3)


def _merge_fwd(proj, b_merge, y_a, y_b):
    t = proj.shape[0]
    tr = _tile(t, 512)

    def body(la_ref, lb_ref, bm_ref, ya_ref, yb_ref, mix_ref):
        ga = _sigmoid(la_ref[...] + bm_ref[:, 0:D])
        gb = _sigmoid(lb_ref[...] + bm_ref[:, D:2 * D])
        mix_ref[...] = (ga * ya_ref[...].astype(F32) + gb * yb_ref[...].astype(F32)).astype(BF16)

    tok = pl.BlockSpec((tr, D), lambda i: (i, 0))
    return pl.pallas_call(
        body, name="merge_fwd", grid=(t // tr,),
        in_specs=[pl.BlockSpec((tr, D), lambda i: (i, 3)), pl.BlockSpec((tr, D), lambda i: (i, 4)),
                  pl.BlockSpec((1, 2 * D), lambda i: (0, 0)), tok, tok],
        out_specs=tok, out_shape=jax.ShapeDtypeStruct((t, D), BF16),
        compiler_params=_cparams(("arbitrary",)),
    )(proj, proj, b_merge, y_a, y_b)


def _merge_bwd(dmix, proj, b_merge, y_a, y_b):
    t = proj.shape[0]
    tr = _tile(t, 512)

    def body(dm_ref, la_ref, lb_ref, bm_ref, ya_ref, yb_ref, dya_ref, dyb_ref, dl_ref, acc):
        @pl.when(pl.program_id(0) == 0)
        def _():
            acc[...] = jnp.zeros_like(acc)

        dm = dm_ref[...].astype(F32)
        ga = _sigmoid(la_ref[...] + bm_ref[:, 0:D])
        gb = _sigmoid(lb_ref[...] + bm_ref[:, D:2 * D])
        dya_ref[...] = (dm * ga).astype(BF16)
        dyb_ref[...] = (dm * gb).astype(BF16)
        dla = dm * ya_ref[...].astype(F32) * ga * (1.0 - ga)
        dlb = dm * yb_ref[...].astype(F32) * gb * (1.0 - gb)
        dl_ref[:, 0:D] = dla.astype(BF16)
        dl_ref[:, D:2 * D] = dlb.astype(BF16)
        acc[0:1, 0:D] += jnp.sum(dla, axis=0, keepdims=True)
        acc[0:1, D:2 * D] += jnp.sum(dlb, axis=0, keepdims=True)

    tok = pl.BlockSpec((tr, D), lambda i: (i, 0))
    return pl.pallas_call(
        body, name="merge_bwd", grid=(t // tr,),
        in_specs=[tok, pl.BlockSpec((tr, D), lambda i: (i, 3)), pl.BlockSpec((tr, D), lambda i: (i, 4)),
                  pl.BlockSpec((1, 2 * D), lambda i: (0, 0)), tok, tok],
        out_specs=[tok, tok, pl.BlockSpec((tr, 2 * D), lambda i: (i, 0)), pl.BlockSpec((8, 2 * D), lambda i: (0, 0))],
        out_shape=[jax.ShapeDtypeStruct((t, D), BF16), jax.ShapeDtypeStruct((t, D), BF16),
                   jax.ShapeDtypeStruct((t, 2 * D), BF16), jax.ShapeDtypeStruct((8, 2 * D), F32)],
        compiler_params=_cparams(("arbitrary",)),
    )(dmix, proj, proj, b_merge, y_a, y_b)


def _log_sigmoid(z):
    return jnp.minimum(z, 0.0) - jnp.log(1.0 + jnp.exp(-jnp.abs(z)))


def _tri(lower):
    r = lax.broadcasted_iota(jnp.int32, (CHUNK, CHUNK), 0)
    c = lax.broadcasted_iota(jnp.int32, (CHUNK, CHUNK), 1)
    return jnp.where(r >= c if lower else r <= c, 1.0, 0.0).astype(F32)


def _gla_fwd(proj, wa_pad, b_alpha, g_out, bsz):
    t = proj.shape[0]
    s = t // bsz
    nc = s // CHUNK

    def body(q_ref, k_ref, v_ref, gg_ref, ms_ref, wa_ref, ba_ref, go_ref, o_ref, og_ref, st_ref, la, state):
        z = _dot(ms_ref[...].astype(BF16), wa_ref[...], NN) + ba_ref[...]
        la[...] = _log_sigmoid(z) * (1.0 / GTAU)
        state[...] = jnp.zeros_like(state)
        low = _tri(True)
        gout = go_ref[...]

        def chunk(n, carry):
            rows = pl.ds(pl.multiple_of(n * CHUNK, CHUNK), CHUNK)
            lac = la[rows, :]
            cum = _dot(low, lac, NN, lax.Precision.HIGHEST)
            ce = jnp.sum(lac, axis=0, keepdims=True)
            kd = (k_ref[rows, :].astype(F32) * jnp.exp(ce - cum)).astype(BF16)
            new = state[...] * jnp.exp(ce) + _dot(v_ref[rows, :].astype(BF16), kd, TN)
            state[...] = new
            st_ref[pl.ds(pl.multiple_of(n * GDV, GDV), GDV), :] = new
            qs = (q_ref[rows, :].astype(F32) * (GDK ** -0.5)).astype(BF16)
            o = _dot(qs, new.astype(BF16), NT)
            o_ref[rows, :] = o
            ro = lax.rsqrt(jnp.mean(o * o, axis=1, keepdims=True) + EPS)
            og_ref[rows, :] = (((o * ro) * gout) * _silu(gg_ref[rows, :].astype(F32))).astype(BF16)
            return carry

        lax.fori_loop(0, nc, chunk, 0, unroll=8)

    hk = pl.BlockSpec((s, GDK), lambda b, h: (b, h))
    return pl.pallas_call(
        body, name="gla_fwd", grid=(bsz, GH),
        in_specs=[hk, pl.BlockSpec((s, GDK), lambda b, h: (b, GH + h)), pl.BlockSpec((s, GDV), lambda b, h: (b, 4 + h)),
                  pl.BlockSpec((s, GDV), lambda b, h: (b, 8 + h)), pl.BlockSpec((s, LANES), lambda b, h: (b, 43)),
                  pl.BlockSpec((LANES, GDK), lambda b, h: (0, h)), pl.BlockSpec((1, GDK), lambda b, h: (0, h)),
                  pl.BlockSpec((1, GDV), lambda b, h: (0, 0))],
        out_specs=[pl.BlockSpec((s, GDV), lambda b, h: (b, h)), pl.BlockSpec((s, GDV), lambda b, h: (b, h)),
                   pl.BlockSpec((nc * GDV, GDK), lambda b, h: (b * GH + h, 0))],
        out_shape=[jax.ShapeDtypeStruct((t, GH * GDV), F32), jax.ShapeDtypeStruct((t, GH * GDV), BF16),
                   jax.ShapeDtypeStruct((bsz * GH * nc * GDV, GDK), F32)],
        scratch_shapes=[pltpu.VMEM((s, GDK), F32), pltpu.VMEM((GDV, GDK), F32)],
        compiler_params=_cparams(("arbitrary", "arbitrary")),
    )(proj, proj, proj, proj, proj, wa_pad, b_alpha, g_out)


def _gla_bwd(dog, o, proj, wa_pad, b_alpha, g_out, states, bsz):
    t = proj.shape[0]
    s = t // bsz
    nc = s // CHUNK

    def body(dog_ref, o_ref, q_ref, k_ref, v_ref, gg_ref, ms_ref, wa_ref, ba_ref, go_ref, st_ref,
             dq_ref, dk_ref, dv_ref, dgg_ref, dz_ref, dba, dgo, zs, la, carry_g):
        @pl.when(pl.program_id(1) == 0)
        def _():
            dba[...] = jnp.zeros_like(dba)
            dgo[...] = jnp.zeros_like(dgo)

        z = _dot(ms_ref[...].astype(BF16), wa_ref[...], NN) + ba_ref[...]
        zs[...] = z
        la[...] = _log_sigmoid(z) * (1.0 / GTAU)
        carry_g[...] = jnp.zeros_like(carry_g)
        low, upp = _tri(True), _tri(False)
        gout = go_ref[...]
        last_row = lax.broadcasted_iota(jnp.int32, (CHUNK, GDK), 0) == CHUNK - 1

        def chunk(step, carry):
            n = nc - 1 - step
            rows = pl.ds(pl.multiple_of(n * CHUNK, CHUNK), CHUNK)
            lac = la[rows, :]
            cum = _dot(low, lac, NN, lax.Precision.HIGHEST)
            ce = jnp.sum(lac, axis=0, keepdims=True)
            e = jnp.exp(ce - cum)
            dec = jnp.exp(ce)
            kf = k_ref[rows, :].astype(F32)
            kd = (kf * e).astype(BF16)
            vv = v_ref[rows, :].astype(BF16)
            qs = (q_ref[rows, :].astype(F32) * (GDK ** -0.5)).astype(BF16)
            ov = o_ref[rows, :]
            ro = lax.rsqrt(jnp.mean(ov * ov, axis=1, keepdims=True) + EPS)
            on = ov * ro
            gg = gg_ref[rows, :].astype(F32)
            sg = _sigmoid(gg)
            dogv = dog_ref[rows, :].astype(F32)
            dgg_ref[rows, :] = (dogv * (on * gout) * (sg * (1.0 + gg * (1.0 - sg)))).astype(BF16)
            t1 = dogv * (gg * sg)
            dgo[0:1, :] += jnp.sum(t1 * on, axis=0, keepdims=True)
            don = t1 * gout
            do = ro * (don - on * jnp.mean(don * on, axis=1, keepdims=True))
            dob = do.astype(BF16)
            st_n = st_ref[pl.ds(pl.multiple_of(n * GDV, GDV), GDV), :]
            dq_ref[rows, :] = (_dot(dob, st_n.astype(BF16), NN) * (GDK ** -0.5)).astype(BF16)
            dn = carry_g[...] + _dot(dob, qs, TN)
            prev = jnp.maximum(n - 1, 0)
            st_p = st_ref[pl.ds(pl.multiple_of(prev * GDV, GDV), GDV), :] * jnp.where(n > 0, 1.0, 0.0)
            ddec = jnp.sum(dn * st_p, axis=0, keepdims=True)
            dnb = dn.astype(BF16)
            dkd = _dot(vv, dnb, NN)
            dv_ref[rows, :] = _dot(kd, dnb, NT).astype(BF16)
            dk_ref[rows, :] = (dkd * e).astype(BF16)
            w = dkd * kf * e
            dce = jnp.sum(w, axis=0, keepdims=True) + ddec * dec
            dcum = jnp.where(last_row, dce - w, -w)
            dla = _dot(upp, dcum, NN, lax.Precision.HIGHEST)
            dz = dla * (1.0 / GTAU) * _sigmoid(-zs[rows, :])
            dba[0:1, :] += jnp.sum(dz, axis=0, keepdims=True)
            dz_ref[rows, :] = dz.astype(BF16)
            carry_g[...] = dn * dec
            return carry

        lax.fori_loop(0, nc, chunk, 0, unroll=8)

    hv = pl.BlockSpec((s, GDV), lambda h, b: (b, h))
    hk = pl.BlockSpec((s, GDK), lambda h, b: (b, h))
    return pl.pallas_call(
        body, name="gla_bwd", grid=(GH, bsz),
        in_specs=[hv, hv, hk, pl.BlockSpec((s, GDK), lambda h, b: (b, GH + h)),
                  pl.BlockSpec((s, GDV), lambda h, b: (b, 4 + h)), pl.BlockSpec((s, GDV), lambda h, b: (b, 8 + h)),
                  pl.BlockSpec((s, LANES), lambda h, b: (b, 43)), pl.BlockSpec((LANES, GDK), lambda h, b: (0, h)),
                  pl.BlockSpec((1, GDK), lambda h, b: (0, h)), pl.BlockSpec((1, GDV), lambda h, b: (0, 0)),
                  pl.BlockSpec((nc * GDV, GDK), lambda h, b: (b * GH + h, 0))],
        out_specs=[hk, hk, hv, hv, hk, pl.BlockSpec((8, GDK), lambda h, b: (0, h)),
                   pl.BlockSpec((8, GDV), lambda h, b: (h, 0))],
        out_shape=[jax.ShapeDtypeStruct((t, GH * GDK), BF16), jax.ShapeDtypeStruct((t, GH * GDK), BF16),
                   jax.ShapeDtypeStruct((t, GH * GDV), BF16), jax.ShapeDtypeStruct((t, GH * GDV), BF16),
                   jax.ShapeDtypeStruct((t, GH * GDK), BF16), jax.ShapeDtypeStruct((8, GH * GDK), F32),
                   jax.ShapeDtypeStruct((8 * GH, GDV), F32)],
        scratch_shapes=[pltpu.VMEM((s, GDK), F32), pltpu.VMEM((s, GDK), F32), pltpu.VMEM((GDV, GDK), F32)],
        compiler_params=_cparams(("arbitrary", "arbitrary")),
    )(dog, o, proj, proj, proj, proj, proj, wa_pad, b_alpha, g_out, states)


def _rope_tables(pos_ref, fr_ref, sg_ref):
    ang = pos_ref[...].astype(F32) * fr_ref[...]
    return jnp.cos(ang), jnp.sin(ang) * sg_ref[...]


def _partner(x):
    lane = lax.broadcasted_iota(jnp.int32, x.shape, 1)
    return jnp.where(lane < NOPE + ROPE // 2, pltpu.roll(x, LANES - ROPE // 2, 1), pltpu.roll(x, ROPE // 2, 1))


def _mla_rows(t):
    return _tile(t, 256)


def _mla_pre_fwd(proj, pos, fr, sg, q_lat_g, kv_lat_g, qn_g, kn_g, wuq, wukv):
    t = proj.shape[0]
    tr = _mla_rows(t)

    def body(cq_ref, ckv_ref, ms_ref, pos_ref, fr_ref, sg_ref, qlg, kvlg, qng, kng, wuq_ref, wukv_ref, q_out, k_out, v_out):
        cos, sin = _rope_tables(pos_ref, fr_ref, sg_ref)
        cq = cq_ref[...].astype(F32)
        cqn = (cq * lax.rsqrt(jnp.mean(cq * cq, axis=1, keepdims=True) + EPS) * qlg[...]).astype(BF16)
        ckv = ckv_ref[...].astype(F32)
        ckvn = (ckv * lax.rsqrt(jnp.mean(ckv * ckv, axis=1, keepdims=True) + EPS) * kvlg[...]).astype(BF16)
        lane = lax.broadcasted_iota(jnp.int32, (tr, HP), 1)
        kpe = jnp.where((lane >= NOPE) & (lane < MQK), ms_ref[...].astype(F32), 0.0)
        lane_all = lax.broadcasted_iota(jnp.int32, (tr, MH * HP), 1)
        v_out[...] = jnp.where(lane_all % HP == MV, 1.0, _dot(ckvn, wukv_ref[:, MH * HP:], NN)).astype(BF16)
        for h in range(MH):
            cols = slice(h * HP, (h + 1) * HP)
            qh = _dot(cqn, wuq_ref[:, cols], NN)
            qn = qh * lax.rsqrt(jnp.sum(qh * qh, axis=1, keepdims=True) * (1.0 / MQK) + EPS) * qng[...]
            q_out[:, cols] = (qn * cos + _partner(qn) * sin).astype(BF16)
            kh = _dot(ckvn, wukv_ref[:, cols], NN) + kpe
            kn = kh * lax.rsqrt(jnp.sum(kh * kh, axis=1, keepdims=True) * (1.0 / MQK) + EPS) * kng[...]
            k_out[:, cols] = (kn * cos + _partner(kn) * sin).astype(BF16)

    def full(a):
        return pl.BlockSpec(a.shape, lambda i: (0, 0))

    wide = pl.BlockSpec((tr, MH * HP), lambda i: (i, 0))
    return pl.pallas_call(
        body, name="mla_pre_fwd", grid=(t // tr,),
        in_specs=[pl.BlockSpec((tr, MQR), lambda i: (i, 20)), pl.BlockSpec((tr, MKVR), lambda i: (i, 42)),
                  pl.BlockSpec((tr, LANES), lambda i: (i, 43)), pl.BlockSpec((tr, 1), lambda i: (i, 0)),
                  full(fr), full(sg), full(q_lat_g), full(kv_lat_g), full(qn_g), full(kn_g), full(wuq), full(wukv)],
        out_specs=[wide, wide, wide],
        out_shape=[jax.ShapeDtypeStruct((t, MH * HP), BF16)] * 3,
        compiler_params=_cparams(("arbitrary",)),
    )(proj, proj, proj, pos, fr, sg, q_lat_g, kv_lat_g, qn_g, kn_g, wuq, wukv)


def _mla_pre_bwd(dq2, dk2, dv2, dmisc_gla, proj, pos, fr, sg, q_lat_g, kv_lat_g, qn_g, kn_g, wuq, wukv):
    t = proj.shape[0]
    tr = _mla_rows(t)

    def body(dq_ref, dk_ref, dv_ref, dmg_ref, cq_ref, ckv_ref, ms_ref, pos_ref, fr_ref, sg_ref, qlg, kvlg, qng, kng,
             wuq_ref, wukv_ref, dcq_ref, dckv_ref, dms_ref, dwuq, dwukv, acc, dqf, dkvf):
        @pl.when(pl.program_id(0) == 0)
        def _():
            dwuq[...] = jnp.zeros_like(dwuq)
            dwukv[...] = jnp.zeros_like(dwukv)
            acc[...] = jnp.zeros_like(acc)

        cos, sin = _rope_tables(pos_ref, fr_ref, sg_ref)
        cq = cq_ref[...].astype(F32)
        rc = lax.rsqrt(jnp.mean(cq * cq, axis=1, keepdims=True) + EPS)
        xc = cq * rc
        cqn = (xc * qlg[...]).astype(BF16)
        ckv = ckv_ref[...].astype(F32)
        rkv = lax.rsqrt(jnp.mean(ckv * ckv, axis=1, keepdims=True) + EPS)
        xkv = ckv * rkv
        ckvn = (xkv * kvlg[...]).astype(BF16)
        lane = lax.broadcasted_iota(jnp.int32, (tr, HP), 1)
        is_rope = (lane >= NOPE) & (lane < MQK)
        kpe = jnp.where(is_rope, ms_ref[...].astype(F32), 0.0)
        dkpe = jnp.zeros((tr, HP), F32)
        dqng = jnp.zeros((1, HP), F32)
        dkng = jnp.zeros((1, HP), F32)
        for h in range(MH):
            cols = slice(h * HP, (h + 1) * HP)
            qh = _dot(cqn, wuq_ref[:, cols], NN)
            rq = lax.rsqrt(jnp.sum(qh * qh, axis=1, keepdims=True) * (1.0 / MQK) + EPS)
            xq = qh * rq
            dy = dq_ref[:, cols].astype(F32)
            dqn = dy * cos - _partner(dy) * sin
            dqng += jnp.sum(dqn * xq, axis=0, keepdims=True)
            tq = dqn * qng[...]
            dqf[:, cols] = (rq * (tq - xq * (jnp.sum(tq * xq, axis=1, keepdims=True) * (1.0 / MQK)))).astype(BF16)
            kh = _dot(ckvn, wukv_ref[:, cols], NN) + kpe
            rk = lax.rsqrt(jnp.sum(kh * kh, axis=1, keepdims=True) * (1.0 / MQK) + EPS)
            xk = kh * rk
            dy = dk_ref[:, cols].astype(F32)
            dkn = dy * cos - _partner(dy) * sin
            dkng += jnp.sum(dkn * xk, axis=0, keepdims=True)
            tk = dkn * kng[...]
            dkh = rk * (tk - xk * (jnp.sum(tk * xk, axis=1, keepdims=True) * (1.0 / MQK)))
            dkvf[:, cols] = jnp.where(lane < NOPE, dkh, 0.0).astype(BF16)
            dkpe += jnp.where(is_rope, dkh, 0.0)
        dkvf[:, MH * HP:] = dv_ref[...]
        acc[2:3, 0:HP] += dqng
        acc[3:4, 0:HP] += dkng
        dms_ref[...] = (dmg_ref[...] + dkpe).astype(BF16)

        dqfv = dqf[...]
        dwuq[...] += _dot(cqn, dqfv, TN)
        dcqn = _dot(dqfv, wuq_ref[...], NT)
        acc[0:1, :] += jnp.sum(dcqn * xc, axis=0, keepdims=True)
        tc = dcqn * qlg[...]
        dcq_ref[...] = (rc * (tc - xc * jnp.mean(tc * xc, axis=1, keepdims=True))).astype(BF16)

        dkvfv = dkvf[...]
        dwukv[...] += _dot(ckvn, dkvfv, TN)
        dckvn = _dot(dkvfv, wukv_ref[...], NT)
        acc[1:2, 0:MKVR] += jnp.sum(dckvn * xkv, axis=0, keepdims=True)
        tkv = dckvn * kvlg[...]
        dckv_ref[...] = (rkv * (tkv - xkv * jnp.mean(tkv * xkv, axis=1, keepdims=True))).astype(BF16)

    def full(a):
        return pl.BlockSpec(a.shape, lambda i: (0, 0))

    wide = pl.BlockSpec((tr, MH * HP), lambda i: (i, 0))
    narrow = pl.BlockSpec((tr, LANES), lambda i: (i, 0))
    return pl.pallas_call(
        body, name="mla_pre_bwd", grid=(t // tr,),
        in_specs=[wide, wide, wide, narrow,
                  pl.BlockSpec((tr, MQR), lambda i: (i, 20)), pl.BlockSpec((tr, MKVR), lambda i: (i, 42)),
                  pl.BlockSpec((tr, LANES), lambda i: (i, 43)), pl.BlockSpec((tr, 1), lambda i: (i, 0)),
                  full(fr), full(sg), full(q_lat_g), full(kv_lat_g), full(qn_g), full(kn_g), full(wuq), full(wukv)],
        out_specs=[pl.BlockSpec((tr, MQR), lambda i: (i, 0)), narrow, narrow,
                   pl.BlockSpec((MQR, MH * HP), lambda i: (0, 0)), pl.BlockSpec((MKVR, 2 * MH * HP), lambda i: (0, 0)),
                   pl.BlockSpec((8, MQR), lambda i: (0, 0))],
        out_shape=[jax.ShapeDtypeStruct((t, MQR), BF16), jax.ShapeDtypeStruct((t, MKVR), BF16),
                   jax.ShapeDtypeStruct((t, LANES), BF16), jax.ShapeDtypeStruct((MQR, MH * HP), F32),
                   jax.ShapeDtypeStruct((MKVR, 2 * MH * HP), F32), jax.ShapeDtypeStruct((8, MQR), F32)],
        scratch_shapes=[pltpu.VMEM((tr, MH * HP), BF16), pltpu.VMEM((tr, 2 * MH * HP), BF16)],
        compiler_params=_cparams(("arbitrary",)),
    )(dq2, dk2, dv2, dmisc_gla, proj, proj, proj, pos, fr, sg, q_lat_g, kv_lat_g, qn_g, kn_g, wuq, wukv)


ATT_FWD_TILES = (1024, 512)
ATT_BWD_TILES = (512, 512)
ATT_HEADS = 2
NEG = -1e30
LOG2E = 1.4426950408889634


def _att_mask(q0, k0, tq, tk):
    qc = (q0 + lax.broadcasted_iota(jnp.int32, (tq, tk), 0)) // CHUNK
    kc = (k0 + lax.broadcasted_iota(jnp.int32, (tq, tk), 1)) // CHUNK
    return kc <= qc


def _att_tiles(s, tiles):
    return _tile(s, tiles[0]), _tile(s, tiles[1])


def _lanes(x, n):
    return x if n == 1 else jnp.concatenate([x] * n, axis=1)


def _grid_ends(grid):
    i, j = pl.program_id(0), pl.program_id(1)
    return (i == 0) & (j == 0), (i == grid[0] - 1) & (j == grid[1] - 1)


def _attn_fwd(q2, k2, v2, bsz, slab):
    t = q2.shape[0]
    s = t // bsz
    tq, tk = _att_tiles(s, ATT_FWD_TILES)
    nq, groups, n_diag = s // tq, tk // HP, max(tq // tk, 1)
    scale = MQK ** -0.5
    c2 = scale * LOG2E
    heads = range(ATT_HEADS)

    def body(q_ref, k_ref, v_ref, slab_ref, o_ref, lse_ref, gath_ref, send_sems, recv_sems, local_sem):
        gather = _core_row_gather_copies(slab_ref, gath_ref, send_sems, recv_sems, local_sem)
        first, last = _grid_ends((bsz, MH // ATT_HEADS))
        pl.when(first)(gather.start)

        def q_loop(qi, carry):
            q0 = pl.multiple_of(qi * tq, tq)
            rows = pl.ds(q0, tq)
            n_full = q0 // tk
            qs = [q_ref[rows, h * HP:(h + 1) * HP] for h in heads]

            def scores(h, kj, masked):
                k0 = pl.multiple_of(kj * tk, tk)
                sc = _dot(qs[h], k_ref[pl.ds(k0, tk), h * HP:(h + 1) * HP], NT)
                return jnp.where(_att_mask(q0, k0, tq, tk), sc, NEG) if masked else sc

            def fold(mx, sc):
                for j in range(groups):
                    mx = jnp.maximum(mx, sc[:, j * HP:(j + 1) * HP])
                return mx

            mx = lax.fori_loop(0, n_full, lambda kj, mx: tuple(fold(mx[h], scores(h, kj, False)) for h in heads),
                               tuple(jnp.full((tq, HP), NEG, F32) for _ in heads))
            for u in range(n_diag):
                mx = tuple(fold(mx[h], scores(h, n_full + u, True)) for h in heads)
            mb = [jnp.broadcast_to(jnp.max(mx[h], axis=1, keepdims=True), (tq, HP)) for h in heads]

            def weighted(h, kj, masked):
                p = jnp.exp2((scores(h, kj, masked) - _lanes(mb[h], groups)) * c2)
                k0 = pl.multiple_of(kj * tk, tk)
                return _dot(p.astype(BF16), v_ref[pl.ds(k0, tk), h * HP:(h + 1) * HP], NN)

            acc = lax.fori_loop(0, n_full, lambda kj, acc: tuple(acc[h] + weighted(h, kj, False) for h in heads),
                                tuple(jnp.zeros((tq, HP), F32) for _ in heads))
            for u in range(n_diag):
                acc = tuple(acc[h] + weighted(h, n_full + u, True) for h in heads)
            lane = lax.broadcasted_iota(jnp.int32, (tq, HP), 1)
            for h in heads:
                a = acc[h]
                l = jnp.sum(jnp.where(lane == MV, a, 0.0), axis=1, keepdims=True)
                o_ref[rows, h * HP:(h + 1) * HP] = (a / l).astype(BF16)
                lse_ref[rows, h * HP:(h + 1) * HP] = mb[h] * scale + jnp.log(l)
            return carry

        lax.fori_loop(0, nq, q_loop, 0)
        pl.when(last)(gather.finish)

    spec = pl.BlockSpec((s, ATT_HEADS * HP), lambda b, h: (b, h))
    return pl.pallas_call(
        body, name="attn_fwd", grid=(bsz, MH // ATT_HEADS), in_specs=[spec] * 3 + [HBM_SPEC],
        out_specs=[spec, spec, HBM_SPEC],
        out_shape=[jax.ShapeDtypeStruct((t, MH * HP), BF16), jax.ShapeDtypeStruct((t, MH * HP), F32),
                   jax.ShapeDtypeStruct((N_DEV,) + slab.shape, slab.dtype)],
        scratch_shapes=EXCHANGE_SEMS, compiler_params=_cparams(("arbitrary", "arbitrary")),
    )(q2, k2, v2, slab)


def _attn_bwd(q2, k2, v2, do2, o2, lse2, bsz, tsum):
    t = q2.shape[0]
    s = t // bsz
    tq, tk = _att_tiles(s, ATT_BWD_TILES)
    nq, nk, per, groups = s // tq, s // tk, max(tk // tq, 1), tk // HP
    scale = MQK ** -0.5
    c2 = scale * LOG2E
    heads = range(ATT_HEADS)

    def body(q_ref, k_ref, v_ref, do_ref, o_ref, lse_ref, t_ref, dq_ref, dk_ref, dv_ref, parts_ref, dq_acc, delta, lse_b2,
             send_sems, recv_sems, local_sem):
        exchange = _all_to_all_copies(t_ref, parts_ref, send_sems, recv_sems, local_sem)
        first, last = _grid_ends((bsz, MH // ATT_HEADS))
        pl.when(first)(exchange.start)
        dq_acc[...] = jnp.zeros_like(dq_acc)

        def d_loop(i, carry):
            rows = pl.ds(pl.multiple_of(i * tq, tq), tq)
            for h in heads:
                hs = slice(h * HP, (h + 1) * HP)
                dl = jnp.sum(do_ref[rows, hs].astype(F32) * o_ref[rows, hs].astype(F32), axis=1, keepdims=True)
                delta[rows, hs] = jnp.broadcast_to(dl, (tq, HP))
            lse_b2[rows, :] = lse_ref[rows, :] * LOG2E
            return carry

        lax.fori_loop(0, nq, d_loop, 0)

        def k_loop(kj, carry):
            k0 = pl.multiple_of(kj * tk, tk)
            kk = [k_ref[pl.ds(k0, tk), h * HP:(h + 1) * HP] for h in heads]
            vv = [v_ref[pl.ds(k0, tk), h * HP:(h + 1) * HP] for h in heads]

            def tile(qi, c, masked):
                q0 = pl.multiple_of(qi * tq, tq)
                rows = pl.ds(q0, tq)
                out = []
                for h in heads:
                    hs = slice(h * HP, (h + 1) * HP)
                    dk, dv = c[h]
                    q = q_ref[rows, hs]
                    do = do_ref[rows, hs]
                    e = _dot(q, kk[h], NT) * c2 - _lanes(lse_b2[rows, hs], groups)
                    if masked:
                        e = jnp.where(_att_mask(q0, k0, tq, tk), e, NEG)
                    p = jnp.exp2(e)
                    dv = dv + _dot(p.astype(BF16), do, TN)
                    ds = (p * (_dot(do, vv[h], NT) - _lanes(delta[rows, hs], groups))).astype(BF16)
                    dq_acc[rows, hs] += _dot(ds, kk[h], NN)
                    dk = dk + _dot(ds, q, TN)
                    out.append((dk, dv))
                return tuple(out)

            zero = jnp.zeros((tk, HP), F32)
            c = tuple((zero, zero) for _ in heads)
            first = k0 // tq
            for u in range(per):
                c = tile(first + u, c, True)
            c = lax.fori_loop(first + per, nq, lambda qi, c: tile(qi, c, False), c)
            for h in heads:
                dk_ref[pl.ds(k0, tk), h * HP:(h + 1) * HP] = (c[h][0] * scale).astype(BF16)
                dv_ref[pl.ds(k0, tk), h * HP:(h + 1) * HP] = c[h][1].astype(BF16)
            return carry

        lax.fori_loop(0, nk, k_loop, 0)
        dq_ref[...] = (dq_acc[...] * scale).astype(BF16)
        pl.when(last)(exchange.finish)

    spec = pl.BlockSpec((s, ATT_HEADS * HP), lambda b, h: (b, h))
    return pl.pallas_call(
        body, name="attn_bwd", grid=(bsz, MH // ATT_HEADS), in_specs=[spec] * 6 + [HBM_SPEC],
        out_specs=[spec] * 3 + [HBM_SPEC],
        out_shape=[jax.ShapeDtypeStruct((t, MH * HP), BF16)] * 3 + [jax.ShapeDtypeStruct(tsum.shape, tsum.dtype)],
        scratch_shapes=[pltpu.VMEM((s, ATT_HEADS * HP), F32)] * 3 + EXCHANGE_SEMS,
        compiler_params=_cparams(("arbitrary", "arbitrary")),
    )(q2, k2, v2, do2, o2, lse2, tsum)


def _perm_w_in_t(w):
    z = lambda n: jnp.zeros((n, w.shape[1]), w.dtype)
    return jnp.concatenate([w[:3072], w[3504:5552], w[3088:3344], w[3344:3472], w[3072:3088], z(48), w[3472:3504], z(32)],
                           axis=0)


def _unperm_w_in_t(g):
    return jnp.concatenate([g[:3072], g[5504:5520], g[5120:5376], g[5376:5504], g[5568:5600], g[3072:5120]], axis=0)


def _pad_wa(w):
    return jnp.pad(w, ((0, LANES - GLR), (0, 0)))


def _pad_wuq(w):
    return jnp.pad(w.reshape(MQR, MH, MQK), ((0, 0), (0, 0), (0, HP - MQK))).reshape(MQR, MH * HP)


def _unpad_wuq(g):
    return g.reshape(MQR, MH, HP)[:, :, :MQK].reshape(MQR, MH * MQK)


def _pad_wukv(w):
    w3 = w.reshape(MKVR, MH, NOPE + MV)
    kp = jnp.pad(w3[:, :, :NOPE], ((0, 0), (0, 0), (0, HP - NOPE))).reshape(MKVR, MH * HP)
    vp = jnp.pad(w3[:, :, NOPE:], ((0, 0), (0, 0), (0, HP - MV))).reshape(MKVR, MH * HP)
    return jnp.concatenate([kp, vp], axis=1)


def _unpad_wukv(g):
    kp = g[:, :MH * HP].reshape(MKVR, MH, HP)[:, :, :NOPE]
    vp = g[:, MH * HP:].reshape(MKVR, MH, HP)[:, :, :MV]
    return jnp.concatenate([kp, vp], axis=2).reshape(MKVR, MH * (NOPE + MV))


def _pad_wo(w):
    return jnp.pad(w.reshape(MH, MV, D), ((0, 0), (0, HP - MV), (0, 0))).reshape(MH * HP, D)


def _unpad_wo(g):
    return g.reshape(MH, HP, D)[:, :MV, :].reshape(MH * MV, D)


def _pad_lanes(v, n=HP):
    return jnp.pad(v, ((0, 0), (0, n - v.shape[1])))


def _local_step(x, positions, tgt, wt, slab_b, sp, mod3):
    bsz, s, _ = x.shape
    t = bsz * s
    x2 = x.reshape(t, D)
    tgt2 = tgt.reshape(t, D)
    pos = positions.reshape(t, 1)
    fr16 = ROPE_THETA ** (-jnp.arange(0, ROPE, 2, dtype=F32) / ROPE)
    zero = lambda n: jnp.zeros((n,), F32)
    fr = jnp.concatenate([zero(NOPE), fr16, fr16, zero(HP - MQK)]).reshape(1, HP)
    sg = jnp.concatenate([zero(NOPE), -jnp.ones((ROPE // 2,), F32), jnp.ones((ROPE // 2,), F32), zero(HP - MQK)]).reshape(1, HP)

    w_in_t = _perm_w_in_t(wt["w_in"])
    wa_pad = _pad_wa(wt["gla_w_alpha"].T)
    wuq = _pad_wuq(wt["mla_w_uq"].T)
    wukv = _pad_wukv(wt["mla_w_ukv"].T)
    qn_g, kn_g = _pad_lanes(sp["mla_qn_g"]), _pad_lanes(sp["mla_kn_g"])

    _, h = _norm_mod_fwd(x2, sp["norm1_g"], mod3, 0, 1, "norm1_fwd")
    proj = _mm(h, w_in_t, "nt", (BF16,), "proj_fwd")
    o_gla, og, states = _gla_fwd(proj, wa_pad, sp["gla_b_alpha"], sp["gla_out_norm_g"], bsz)
    q2, k2, v2 = _mla_pre_fwd(proj, pos, fr, sg, sp["mla_q_lat_g"], sp["mla_kv_lat_g"], qn_g, kn_g, wuq, wukv)
    o2, lse2, core_row = _attn_fwd(q2, k2, v2, bsz, slab_b)
    wt = dict(wt, **_unpack_gathered(_cross_core_fill(core_row), SLAB_B))
    wo_pad = _pad_wo(wt["mla_w_o"])
    y_a = _mm(og, wt["gla_w_o"], "nn", (BF16,), "gla_out_fwd")
    y_b = _mm(o2, wo_pad, "nn", (BF16,), "mla_out_fwd")
    mix = _merge_fwd(proj, sp["b_merge"], y_a, y_b)
    mixed = _mm(mix, wt["w_out"], "nn", (F32,), "w_out_fwd")

    x1, h2 = _norm_mod_fwd(x2, sp["norm2_g"], mod3, 3, 4, "norm2_fwd", mixed=mixed, i_gate=2)
    a, f = _mm(h2, wt["mlp_w1"], "nt", (BF16, BF16), "mlp1_fwd",
               epi=lambda acc: (acc, jnp.square(jnp.maximum(acc, 0.0))))
    ff = _mm(f, wt["mlp_w2"], "nn", (F32,), "mlp2_fwd")
    dy, dff, acc_g2, acc_loss = _loss_head(x1, ff, tgt2, mod3)

    gw = {}
    gw["mlp_w2"] = _mm(f, dff, "tn", (BF16,), "mlp2_dw")
    da = _mm(dff, wt["mlp_w2"], "nt", (BF16,), "mlp2_dx", extras=(a,),
             epi=lambda acc, av: (acc * (2.0 * jnp.maximum(av.astype(F32), 0.0)),))
    gw["mlp_w1"] = _mm(da, h2, "tn", (BF16,), "mlp1_dw")
    dh2 = _mm(da, wt["mlp_w1"], "nn", (F32,), "mlp1_dx")
    dx1, dmixed, accb2, accg2 = _norm_mod_bwd(x1, dh2, dy, sp["norm2_g"], mod3, 3, 4, "norm2_bwd", mixed=mixed, i_gate=2)

    gw["w_out"] = _mm(mix, dmixed, "tn", (BF16,), "w_out_dw")
    dmix = _mm(dmixed, wt["w_out"], "nt", (BF16,), "w_out_dx")
    dy_a, dy_b, dlogits, acc_bm = _merge_bwd(dmix, proj, sp["b_merge"], y_a, y_b)
    gw["gla_w_o"] = _mm(og, dy_a, "tn", (BF16,), "gla_out_dw")
    dog = _mm(dy_a, wt["gla_w_o"], "nt", (BF16,), "gla_out_dx")
    gw["mla_w_o"] = _unpad_wo(_mm(o2, dy_b, "tn", (BF16,), "mla_out_dw"))
    do2 = _mm(dy_b, wo_pad, "nt", (BF16,), "mla_out_dx")
    dq2, dk2, dv2, parts_b = _attn_bwd(q2, k2, v2, do2, o2, lse2, bsz, _pack_per_device(gw, SLAB_B))
    dq_g, dk_g, dv_g, dgg, dz, acc_ba, acc_go = _gla_bwd(dog, o_gla, proj, wa_pad, sp["gla_b_alpha"],
                                                         sp["gla_out_norm_g"], states, bsz)
    gw["gla_w_alpha"] = _mm(proj, dz, "tn", (F32,), "gla_alpha_dw", a_off=43, m=LANES)[:GLR].T.astype(BF16)
    dmisc_gla = _mm(dz, wa_pad, "nt", (F32,), "gla_alpha_dx")
    dcq, dckv, dmisc, gwuq, gwukv, acc_mla = _mla_pre_bwd(dq2, dk2, dv2, dmisc_gla, proj, pos, fr, sg, sp["mla_q_lat_g"],
                                                         sp["mla_kv_lat_g"], qn_g, kn_g, wuq, wukv)
    gw["mla_w_uq"] = _unpad_wuq(gwuq).T.astype(BF16)
    gw["mla_w_ukv"] = _unpad_wukv(gwukv).T.astype(BF16)
    dproj = jnp.concatenate([dq_g, dk_g, dv_g, dgg, dlogits, dcq, dckv, dmisc], axis=1)
    gw["w_in"] = _unperm_w_in_t(_mm(dproj, h, "tn", (BF16,), "proj_dw"))
    dh, parts_a = _mm(dproj, w_in_t, "nn", (F32,), "proj_dx", cargo=_sum_over_cores(_pack_per_device(gw, SLAB_A), "a"))
    grad_x, accb1, accg1 = _norm_mod_bwd(x2, dh, dx1, sp["norm1_g"], mod3, 0, 1, "norm1_bwd")

    dmod = jnp.stack([accb1[:, 0], accb1[:, 1], accb2[:, 2], accb2[:, 0], accb2[:, 1], acc_g2[:, 0]], axis=1)

    rows = {
        "dmod": dmod.reshape(bsz * 6, D),
        "norm1_g": accg1[0:1], "norm2_g": accg2[0:1],
        "b_merge": acc_bm[0:1].reshape(2, D),
        "gla_b_alpha": _pad_lanes(acc_ba[0:1], D),
        "gla_out_norm_g": _pad_lanes(acc_go.reshape(GH, 8, GDV)[:, 0, :], D),
        "mla_q_lat_g": _pad_lanes(acc_mla[0:1], D), "mla_kv_lat_g": _pad_lanes(acc_mla[1:2], D),
        "mla_qn_g": _pad_lanes(acc_mla[2:3], D), "mla_kn_g": _pad_lanes(acc_mla[3:4], D),
        "loss": acc_loss[0:1],
    }
    return grad_x.reshape(bsz, s, D), parts_a, parts_b, rows


HBM_SPEC = pl.BlockSpec(memory_space=pltpu.HBM)


def _all_gather(p, name):
    r, cdim = p.shape

    def body(p_ref, out_ref, send_sems, recv_sems, local_sem):
        x, y, c = lax.axis_index("x"), lax.axis_index("y"), lax.axis_index("c")
        me, sibling = (x, y, c), (x, y, 1 - c)
        chips = [(1 - x, y), (x, 1 - y), (1 - x, 1 - y)]

        def slot(px, py, pc):
            return out_ref.at[4 * px + 2 * py + pc]

        def copy(k, block, to, src=None):
            return pltpu.make_async_remote_copy(
                src_ref=slot(*block) if src is None else src, dst_ref=slot(*block),
                send_sem=send_sems.at[k], recv_sem=recv_sems.at[k], device_id=to, device_id_type=MESH)

        mine = pltpu.make_async_copy(p_ref, slot(*me), local_sem)
        mine.start()
        first = [copy(0, me, sibling, src=p_ref)] + [copy(1 + j, me, (*chip, c), src=p_ref) for j, chip in enumerate(chips)]
        for cp in first:
            cp.start()
        passed = [copy(4 + j, (*chip, c), sibling) for j, chip in enumerate(chips)]
        for j, chip in enumerate(chips):
            copy(1 + j, (*chip, c), me).wait_recv()
            passed[j].start()
        copy(0, sibling, me).wait_recv()
        for j, chip in enumerate(chips):
            copy(4 + j, (*chip, 1 - c), me).wait_recv()
        for cp in first + passed:
            cp.wait_send()
        mine.wait()

    return pl.pallas_call(
        body, name=name, out_shape=jax.ShapeDtypeStruct((N_DEV, r, cdim), p.dtype),
        in_specs=[HBM_SPEC], out_specs=HBM_SPEC,
        scratch_shapes=[pltpu.SemaphoreType.DMA((7,)), pltpu.SemaphoreType.DMA((7,)), pltpu.SemaphoreType.DMA(())],
    )(p)


def _sibling_exchange(g, name):
    def body(g_ref, out_ref, send_sem, recv_sem):
        x, y, c = lax.axis_index("x"), lax.axis_index("y"), lax.axis_index("c")
        cp = pltpu.make_async_remote_copy(src_ref=g_ref, dst_ref=out_ref, send_sem=send_sem, recv_sem=recv_sem,
                                          device_id=(x, y, 1 - c), device_id_type=MESH)
        cp.start()
        cp.wait()

    return pl.pallas_call(
        body, name=name, out_shape=jax.ShapeDtypeStruct(g.shape, g.dtype),
        in_specs=[HBM_SPEC], out_specs=HBM_SPEC,
        scratch_shapes=[pltpu.SemaphoreType.DMA(()), pltpu.SemaphoreType.DMA(())],
    )(g)


class _Exchange:
    def __init__(self, local, sends, arrivals):
        self.local, self.sends, self.arrivals = local, sends, arrivals

    def start(self):
        self.local.start()
        for cp in self.sends:
            cp.start()

    def finish(self):
        for cp in self.arrivals:
            cp.wait_recv()
        for cp in self.sends:
            cp.wait_send()
        self.local.wait()


EXCHANGE_SEMS = [pltpu.SemaphoreType.DMA((N_DEV,)), pltpu.SemaphoreType.DMA((N_DEV,)), pltpu.SemaphoreType.DMA(())]


def _all_to_all_copies(t_ref, out_ref, send_sems, recv_sems, local_sem):
    x, y, c = lax.axis_index("x"), lax.axis_index("y"), lax.axis_index("c")
    me = 4 * x + 2 * y + c

    def copy(k, src, dst):
        px, py, pc = (1 - x if k & 4 else x), (1 - y if k & 2 else y), (1 - c if k & 1 else c)
        peer = 4 * px + 2 * py + pc
        return pltpu.make_async_remote_copy(src_ref=t_ref.at[peer if src is None else src],
                                            dst_ref=out_ref.at[peer if dst is None else dst], send_sem=send_sems.at[k],
                                            recv_sem=recv_sems.at[k], device_id=(px, py, pc), device_id_type=MESH)

    return _Exchange(pltpu.make_async_copy(t_ref.at[me], out_ref.at[me], local_sem),
                     [copy(k, None, me) for k in range(1, N_DEV)], [copy(k, me, None) for k in range(1, N_DEV)])


def _chip_exchange_copies(t_ref, out_ref, send_sems, recv_sems, local_sem):
    x, y, c = lax.axis_index("x"), lax.axis_index("y"), lax.axis_index("c")
    my_chip = 2 * x + y
    chips = [(1 - x, y), (x, 1 - y), (1 - x, 1 - y)]

    def copy(j, src, dst, px, py):
        return pltpu.make_async_remote_copy(src_ref=t_ref.at[src], dst_ref=out_ref.at[dst], send_sem=send_sems.at[j],
                                            recv_sem=recv_sems.at[j], device_id=(px, py, c), device_id_type=MESH)

    return _Exchange(pltpu.make_async_copy(t_ref.at[my_chip], out_ref.at[my_chip], local_sem),
                     [copy(j, 2 * px + py, my_chip, px, py) for j, (px, py) in enumerate(chips)],
                     [copy(j, my_chip, 2 * px + py, px, py) for j, (px, py) in enumerate(chips)])


def _core_row_gather_copies(p_ref, out_ref, send_sems, recv_sems, local_sem):
    x, y, c = lax.axis_index("x"), lax.axis_index("y"), lax.axis_index("c")
    peers = [(x, y, 1 - c), (1 - x, y, c), (x, 1 - y, c), (1 - x, 1 - y, c)]

    def slot(px, py, pc):
        return out_ref.at[4 * px + 2 * py + pc]

    def copy(j, block, to):
        return pltpu.make_async_remote_copy(src_ref=p_ref, dst_ref=slot(*block), send_sem=send_sems.at[j],
                                            recv_sem=recv_sems.at[j], device_id=to, device_id_type=MESH)

    return _Exchange(pltpu.make_async_copy(p_ref, slot(x, y, c), local_sem),
                     [copy(j, (x, y, c), peer) for j, peer in enumerate(peers)],
                     [copy(j, peer, peer) for j, peer in enumerate(peers)])


def _cross_core_fill(gathered):
    def body(g_ref, out_ref, send_sems, recv_sems):
        x, y, c = lax.axis_index("x"), lax.axis_index("y"), lax.axis_index("c")
        chips = [(1 - x, y), (x, 1 - y), (1 - x, 1 - y)]

        def copy(j, pc):
            px, py = chips[j]
            slot = 4 * px + 2 * py + pc
            return pltpu.make_async_remote_copy(src_ref=g_ref.at[slot], dst_ref=out_ref.at[slot], send_sem=send_sems.at[j],
                                                recv_sem=recv_sems.at[j], device_id=(x, y, 1 - c), device_id_type=MESH)

        sends = [copy(j, c) for j in range(3)]
        for cp in sends:
            cp.start()
        for j in range(3):
            copy(j, 1 - c).wait_recv()
        for cp in sends:
            cp.wait_send()

    return pl.pallas_call(
        body, name="weights_cross_core_fill", out_shape=jax.ShapeDtypeStruct(gathered.shape, gathered.dtype),
        in_specs=[HBM_SPEC], out_specs=HBM_SPEC, input_output_aliases={0: 0},
        scratch_shapes=[pltpu.SemaphoreType.DMA((3,)), pltpu.SemaphoreType.DMA((3,))],
    )(gathered)


def _slab_block(r):
    return max(b for b in range(16, SLAB_BLOCK_MAX + 1, 16) if r % b == 0)


def _pair_sum(a, b, name):
    n, r, cdim = a.shape
    rb = _slab_block(r)
    blk = pl.BlockSpec((1, rb, cdim), lambda j, i: (j, i, 0))

    def body(a_ref, b_ref, o_ref):
        o_ref[...] = (a_ref[...].astype(F32) + b_ref[...].astype(F32)).astype(BF16)

    return pl.pallas_call(
        body, name=name, grid=(n, r // rb), in_specs=[blk, blk], out_specs=blk,
        out_shape=jax.ShapeDtypeStruct(a.shape, BF16), compiler_params=_cparams(("arbitrary", "arbitrary")),
    )(a, b)


def _adamw_math(w, g, m, v):
    m = ADAM_B1 * m + (1.0 - ADAM_B1) * g
    v = ADAM_B2 * v + (1.0 - ADAM_B2) * jnp.square(g)
    m_hat = m / (1.0 - ADAM_B1 ** ADAM_STEP)
    v_hat = v / (1.0 - ADAM_B2 ** ADAM_STEP)
    delta = -ADAM_LR * (m_hat / (jnp.sqrt(v_hat) + ADAM_EPS) + ADAM_WD * w)
    return delta, m, v


def _slab_sum(parts, name):
    n, r, cdim = parts.shape
    rb = _slab_block(r)
    blk = pl.BlockSpec((rb, cdim), lambda i: (i, 0))

    def body(p_ref, g_out):
        g = p_ref[0].astype(F32)
        for j in range(1, n):
            g = g + p_ref[j].astype(F32)
        g_out[...] = g

    return pl.pallas_call(
        body, name=name, grid=(r // rb,),
        in_specs=[pl.BlockSpec((n, rb, cdim), lambda i: (0, i, 0))], out_specs=blk,
        out_shape=jax.ShapeDtypeStruct((r, cdim), F32), compiler_params=_cparams(("arbitrary",)),
    )(parts)


def _adamw(g, w, m, v, name):
    r, cdim = w.shape
    rb = _tile(r, 256)
    blk = pl.BlockSpec((rb, cdim), lambda i: (i, 0))

    def body(g_ref, w_ref, m_ref, v_ref, d_out, m_out, v_out):
        d_out[...], m_out[...], v_out[...] = _adamw_math(w_ref[...], g_ref[...], m_ref[...], v_ref[...])

    return pl.pallas_call(
        body, name=name, grid=(r // rb,), in_specs=[blk] * 4, out_specs=[blk] * 3,
        out_shape=[jax.ShapeDtypeStruct((r, cdim), F32)] * 3, compiler_params=_cparams(("arbitrary",)),
    )(g, w, m, v)


def _adamw_small(parts, w, m, v):
    def body(p_ref, w_ref, m_ref, v_ref, g_out, d_out, m_out, v_out, loss_out):
        def total(srcs):
            acc = None
            for r in srcs:
                for j in range(N_DEV):
                    term = p_ref[j, r:r + 1, :]
                    acc = term if acc is None else acc + term
            return acc

        for prow, srcs in enumerate(SMALL_SOURCES):
            one = slice(prow, prow + 1)
            g = total(srcs)
            g_out[one, :] = g
            d_out[one, :], m_out[one, :], v_out[one, :] = _adamw_math(w_ref[one, :], g, m_ref[one, :], v_ref[one, :])
        loss_out[...] = jnp.broadcast_to(jnp.sum(total(LOSS_SOURCE), axis=1, keepdims=True), (8, LANES))

    full = lambda shp: pl.BlockSpec(shp, lambda i: (0,) * len(shp))
    return pl.pallas_call(
        body, name="adamw_small", grid=(1,),
        in_specs=[full((N_DEV, SMALL_ROWS, D)), full((16, D)), full((16, D)), full((16, D))],
        out_specs=[full((16, D))] * 4 + [full((8, LANES))],
        out_shape=[jax.ShapeDtypeStruct((16, D), F32)] * 4 + [jax.ShapeDtypeStruct((8, LANES), F32)],
        compiler_params=_cparams(("arbitrary",)),
    )(parts, w, m, v)


def _to_slab(shard, form):
    if form == "N":
        return shard
    return shard.T if form == "T" else shard.T.reshape(-1, D)


def _from_slab(block, form, shard_shape):
    if form == "N":
        return block
    return block.T if form == "T" else block.reshape(shard_shape[1], shard_shape[0]).T


def _gathered_full(g, form, shard_shape):
    if form == "TR":
        return g.reshape(N_DEV * shard_shape[1], shard_shape[0])
    return g.reshape(N_DEV * g.shape[1], D)


def _pack_slab(shards, layout):
    return jnp.concatenate([jnp.zeros((r, D), BF16) if n is None else _to_slab(shards[n], form).astype(BF16)
                            for n, r, form in layout], axis=0)


def _unpack_gathered(gathered, layout):
    out, off = {}, 0
    for n, r, form in layout:
        if n is not None:
            out[n] = _gathered_full(gathered[:, off:off + r], form, SHARD_SHAPES[n])
        off += r
    return out


def _pack_per_device(gw, layout):
    return jnp.concatenate([jnp.zeros((N_DEV, r, D), BF16) if n is None else gw[n].reshape(N_DEV, r, D)
                            for n, r, _ in layout], axis=1)


def _unpack_shards(gslab, layout):
    out, off = {}, 0
    for n, r, form in layout:
        if n is not None:
            out[n] = _from_slab(gslab[off:off + r], form, SHARD_SHAPES[n])
        off += r
    return out


def _sum_over_cores(per_dev, tag):
    my_c = lax.axis_index("c")
    pairs = per_dev.reshape(4, 2, per_dev.shape[1], D)
    keep = lax.dynamic_index_in_dim(pairs, my_c, axis=1, keepdims=False)
    give = lax.dynamic_index_in_dim(pairs, 1 - my_c, axis=1, keepdims=False)
    return _pair_sum(keep, _sibling_exchange(give, "rs_sibling_exchange_" + tag), "rs_pair_sum_" + tag)


def _small_pack(vals):
    rows = []
    for n in SMALL:
        v = vals[n].reshape(-1)
        k = -(-v.shape[0] // D)
        rows.append(jnp.pad(v, (0, k * D - v.shape[0])).reshape(k, D))
    return jnp.concatenate(rows, axis=0)


def _small_unpack(packed, shapes):
    out = {}
    for n in SMALL:
        k = shapes[n][-1]
        r0 = SMALL_ROW[n]
        out[n] = packed[r0:r0 + -(-k // D)].reshape(-1)[:k].reshape(shapes[n])
    return out


def kernel(x, c, positions, w_ada, b_ada, norm1_g, w_in, b_merge, gla_w_alpha, gla_b_alpha, gla_out_norm_g, gla_w_o, mla_q_lat_g, mla_w_uq, mla_kv_lat_g, mla_w_ukv, mla_qn_g, mla_kn_g, mla_w_o, w_out, norm2_g, mlp_w1, mlp_w2, loss_target, m_w_ada, m_b_ada, m_norm1_g, m_w_in, m_b_merge, m_gla_w_alpha, m_gla_b_alpha, m_gla_out_norm_g, m_gla_w_o, m_mla_q_lat_g, m_mla_w_uq, m_mla_kv_lat_g, m_mla_w_ukv, m_mla_qn_g, m_mla_kn_g, m_mla_w_o, m_w_out, m_norm2_g, m_mlp_w1, m_mlp_w2, v_w_ada, v_b_ada, v_norm1_g, v_w_in, v_b_merge, v_gla_w_alpha, v_gla_b_alpha, v_gla_out_norm_g, v_gla_w_o, v_mla_q_lat_g, v_mla_w_uq, v_mla_kv_lat_g, v_mla_w_ukv, v_mla_qn_g, v_mla_kn_g, v_mla_w_o, v_w_out, v_norm2_g, v_mlp_w1, v_mlp_w2):
    args = dict(locals())
    wts = {n: args[n][0] for n in WEIGHTS}
    mom = {n: args["m_" + n][0] for n in WEIGHTS}
    var = {n: args["v_" + n][0] for n in WEIGHTS}
    my_c = lax.axis_index("c")
    my_dev = 4 * lax.axis_index("x") + 2 * lax.axis_index("y") + my_c
    bsz = x.shape[0]
    sp = {n: wts[n].reshape(1, -1) for n in SMALL}

    wt = _unpack_gathered(_all_gather(_pack_slab(wts, SLAB_A), "weights_all_gather"), SLAB_A)

    c_all = _all_gather(jnp.pad(c, ((0, 8 - bsz), (0, 0))), "cond_all_gather")[:, :bsz].reshape(N_DEV * bsz, D)
    bias = lax.dynamic_slice_in_dim(sp["b_ada"], my_dev * ADA_COLS, ADA_COLS, axis=1)
    mod_cols = _mm(c_all, wts["w_ada"], "nn", (F32,), "ada_fwd", pro=_silu, epi=lambda acc, b: (acc + b,),
                   extras=(jnp.broadcast_to(bias, (N_DEV * bsz, ADA_COLS)),))
    mod_all = _all_gather(mod_cols, "mod_all_gather")
    mod_mine = lax.dynamic_slice_in_dim(mod_all, my_dev * bsz, bsz, axis=1)
    mod3 = jnp.transpose(mod_mine, (1, 0, 2)).reshape(bsz, 6, D)

    grad_x, parts_a, parts_b, rows = _local_step(x, positions, loss_target, wt, _pack_slab(wts, SLAB_B), sp, mod3)

    grads = dict(_unpack_shards(_slab_sum(parts_a, "rs_slab_sum_a"), SLAB_A),
                 **_unpack_shards(_slab_sum(parts_b, "rs_slab_sum_b"), SLAB_B))
    big = {n: (g,) + tuple(_adamw(g, wts[n], mom[n], var[n], "adamw_" + n)) for n, g in grads.items()}

    order = ["dmod", "norm1_g", "norm2_g", "b_merge", "gla_b_alpha", "gla_out_norm_g", "mla_q_lat_g", "mla_kv_lat_g",
             "mla_qn_g", "mla_kn_g", "loss"]
    part_rows = jnp.concatenate([rows[n] for n in order], axis=0)
    part_rows = jnp.pad(part_rows, ((0, SMALL_ROWS - part_rows.shape[0]), (0, 0)))
    all_rows = _all_gather(part_rows, "partials_all_gather")

    dmod_all = all_rows[:, :6 * bsz].reshape(N_DEV * bsz, 6 * D)
    dmod_cols = lax.dynamic_slice_in_dim(dmod_all, my_dev * ADA_COLS, ADA_COLS, axis=1)
    g_ada = _mm(c_all, dmod_cols, "tn", (F32,), "ada_dw", pro=_silu)
    big["w_ada"] = (g_ada,) + tuple(_adamw(g_ada, wts["w_ada"], mom["w_ada"], var["w_ada"], "adamw_w_ada"))

    small = _adamw_small(all_rows, _small_pack({n: wts[n] for n in SMALL}), _small_pack({n: mom[n] for n in SMALL}),
                         _small_pack({n: var[n] for n in SMALL}))
    loss = small[4][0, 0]
    small_shapes = {n: wts[n].shape for n in SMALL}
    small = [_small_unpack(o, small_shapes) for o in small[:4]]

    outs = [loss, grad_x]
    for k in range(4):
        for n in WEIGHTS:
            val = big[n][k] if n in BIG else small[k][n]
            outs.append(val.reshape((1,) + tuple(wts[n].shape)))
    return tuple(outs)
```

```python
import functools

import jax
import jax.numpy as jnp
from jax import lax
from jax.experimental import pallas as pl
from jax.experimental.pallas import tpu as pltpu

F32 = jnp.float32
BF16 = jnp.bfloat16
MESH = pl.DeviceIdType.MESH

D = 1024
EPS = 1e-6
CHUNK = 64
GH, GDK, GDV, GLR, GTAU = 4, 128, 256, 16, 16.0
MH, MQR, MKVR, NOPE, ROPE, MV = 16, 256, 128, 64, 32, 64
MQK = NOPE + ROPE
HP = 128
FF = 4 * D
ROPE_THETA = 10000.0
IN_WIDTH = 5552
PW = 5632
N_DEV = 8
LANES = 128
SLAB_BLOCK_MAX = 400
ADA_COLS = 6 * D // N_DEV
SMALL_ROWS = 32
SMALL_SOURCES = tuple([(r, 6 + r) for r in range(6)] + [(12,), (13,), (14,), (15,), (16,), (17, 18, 19, 20),
                                                         (21,), (22,), (23,), (24,)])
LOSS_SOURCE = (25,)
VMEM_LIMIT = 56 * 1024 * 1024

ADAM_LR, ADAM_B1, ADAM_B2, ADAM_EPS, ADAM_WD, ADAM_STEP = 0.001, 0.9, 0.999, 1e-08, 0.01, 10

SLAB_A = (("w_in", 694, "T"), ("gla_w_alpha", 1, "TR"), (None, 9, None), ("mla_w_uq", 48, "TR"), ("mla_w_ukv", 32, "TR"))
SLAB_B = (("mlp_w1", 512, "T"), ("gla_w_o", 128, "N"), ("mla_w_o", 128, "N"), ("w_out", 128, "N"), ("mlp_w2", 512, "N"))
BIG = ("w_ada",) + tuple(n for n, _, _ in SLAB_A + SLAB_B if n is not None)
SHARD_SHAPES = {"w_ada": (D, 6 * D // N_DEV), "w_in": (D, IN_WIDTH // N_DEV), "gla_w_alpha": (GLR, GH * GDK // N_DEV),
                "gla_w_o": (GH * GDV // N_DEV, D), "mla_w_uq": (MQR, MH * MQK // N_DEV),
                "mla_w_ukv": (MKVR, MH * (NOPE + MV) // N_DEV), "mla_w_o": (MH * MV // N_DEV, D), "w_out": (D // N_DEV, D),
                "mlp_w1": (D, FF // N_DEV), "mlp_w2": (FF // N_DEV, D)}
SMALL = ("b_ada", "norm1_g", "norm2_g", "b_merge", "gla_b_alpha", "gla_out_norm_g", "mla_q_lat_g", "mla_kv_lat_g",
         "mla_qn_g", "mla_kn_g")
SMALL_ROW = {"b_ada": 0, "norm1_g": 6, "norm2_g": 7, "b_merge": 8, "gla_b_alpha": 10, "gla_out_norm_g": 11,
             "mla_q_lat_g": 12, "mla_kv_lat_g": 13, "mla_qn_g": 14, "mla_kn_g": 15}
WEIGHTS = ("w_ada", "b_ada", "norm1_g", "w_in", "b_merge", "gla_w_alpha", "gla_b_alpha", "gla_out_norm_g", "gla_w_o",
           "mla_q_lat_g", "mla_w_uq", "mla_kv_lat_g", "mla_w_ukv", "mla_qn_g", "mla_kn_g", "mla_w_o", "w_out",
           "norm2_g", "mlp_w1", "mlp_w2")


def _cparams(sem=None):
    return pltpu.CompilerParams(dimension_semantics=sem, vmem_limit_bytes=VMEM_LIMIT)


def _tile(n, pref):
    for t in (2048, 1024, 512, 256, 128):
        if t <= pref and n % t == 0:
            return t
    return n


def _dot(a, b, dims, precision=None):
    return lax.dot_general(a, b, (dims, ((), ())), preferred_element_type=F32, precision=precision)


NN = ((1,), (0,))
NT = ((1,), (1,))
TN = ((0,), (0,))


def _sigmoid(x):
    return 1.0 / (1.0 + jnp.exp(-x))


def _silu(x):
    return x * _sigmoid(x)


def _mm(a, b, mode, out_dtypes, name, *, pro=None, pro_b=None, epi=None, extras=(), a_off=0, m=None, tm=2048, tn=1024,
        tk=1024, cargo=None):
    if mode == "tn":
        kc, n = b.shape
        m = a.shape[1] if m is None else m
    elif mode == "nn":
        m, kc = a.shape
        n = b.shape[1]
    else:
        m, kc = a.shape
        n = b.shape[0]
    tm, tn, tk = _tile(m, tm), _tile(n, tn), _tile(kc, tk)
    nk = kc // tk
    dims = {"nn": NN, "nt": NT, "tn": TN}[mode]
    if mode == "tn":
        a_spec = pl.BlockSpec((tk, tm), lambda i, j, k: (k, i + a_off))
    else:
        a_spec = pl.BlockSpec((tm, tk), lambda i, j, k: (i + a_off, k))
    if mode == "nt":
        b_spec = pl.BlockSpec((tn, tk), lambda i, j, k: (j, k))
    else:
        b_spec = pl.BlockSpec((tk, tn), lambda i, j, k: (k, j))
    o_spec = pl.BlockSpec((tm, tn), lambda i, j, k: (i, j))
    n_ex, n_out = len(extras), len(out_dtypes)
    grid = (m // tm, n // tn, nk)
    has_cargo = cargo is not None

    def body(a_ref, b_ref, *rest):
        ex, rest = rest[:n_ex], rest[n_ex:]
        if has_cargo:
            outs, acc = rest[1:1 + n_out], rest[2 + n_out]
            exchange = _chip_exchange_copies(rest[0], rest[1 + n_out], *rest[3 + n_out:])
            steps = [pl.program_id(axis) for axis in range(3)]
            first = (steps[0] == 0) & (steps[1] == 0) & (steps[2] == 0)
            last = (steps[0] == grid[0] - 1) & (steps[1] == grid[1] - 1) & (steps[2] == grid[2] - 1)
            pl.when(first)(exchange.start)
        else:
            outs, acc = rest[:n_out], rest[n_out]
        k = pl.program_id(2)

        @pl.when(k == 0)
        def _():
            acc[...] = jnp.zeros_like(acc)

        av = a_ref[...]
        if pro is not None:
            av = pro(av)
        bv = b_ref[...]
        if pro_b is not None:
            bv = pro_b(bv)
        acc[...] += _dot(av.astype(BF16), bv.astype(BF16), dims)

        @pl.when(k == nk - 1)
        def _():
            res = (acc[...],) if epi is None else epi(acc[...], *[e[...] for e in ex])
            for o_ref, r in zip(outs, res):
                o_ref[...] = r.astype(o_ref.dtype)

        if has_cargo:
            pl.when(last)(exchange.finish)

    cargo_in = [cargo] if has_cargo else []
    cargo_spec = [HBM_SPEC] * len(cargo_in)
    out = pl.pallas_call(
        body, name=name, grid=grid,
        in_specs=[a_spec, b_spec] + [o_spec] * n_ex + cargo_spec,
        out_specs=[o_spec] * n_out + cargo_spec,
        out_shape=[jax.ShapeDtypeStruct((m, n), dt) for dt in out_dtypes]
        + [jax.ShapeDtypeStruct(c.shape, c.dtype) for c in cargo_in],
        scratch_shapes=[pltpu.VMEM((tm, tn), F32)] + (EXCHANGE_SEMS if has_cargo else []),
        compiler_params=_cparams(("arbitrary",) * 3 if has_cargo else ("parallel", "parallel", "arbitrary")),
    )(a, b, *extras, *cargo_in)
    return out[0] if len(out) == 1 else out


def _rows(s):
    return _tile(s, 512)


def _mod_spec():
    return pl.BlockSpec((1, 6, D), lambda b, i: (b, 0, 0))


def _tok_spec(tr, nb, width=D, col=0):
    return pl.BlockSpec((tr, width), lambda b, i: (b * nb + i, col))


def _norm_mod_fwd(x, g, mod3, i_shift, i_scale, name, mixed=None, i_gate=None):
    bsz, _, _ = mod3.shape
    t = x.shape[0]
    s = t // bsz
    tr = _rows(s)
    nb = s // tr
    has_res = mixed is not None

    def body(*refs):
        if has_res:
            x_ref, mx_ref, g_ref, mod_ref, x1_ref, h_ref = refs
            xv = x_ref[...] + mod_ref[0, i_gate:i_gate + 1, :] * mx_ref[...]
            x1_ref[...] = xv
        else:
            x_ref, g_ref, mod_ref, h_ref = refs
            xv = x_ref[...]
        r = lax.rsqrt(jnp.mean(xv * xv, axis=1, keepdims=True) + EPS)
        hn = (xv * r) * g_ref[...]
        h = hn * (1.0 + mod_ref[0, i_scale:i_scale + 1, :]) + mod_ref[0, i_shift:i_shift + 1, :]
        h_ref[...] = h.astype(BF16)

    tok = _tok_spec(tr, nb)
    gspec = pl.BlockSpec((1, D), lambda b, i: (0, 0))
    ins = [x] + ([mixed] if has_res else []) + [g, mod3]
    in_specs = [tok] + ([tok] if has_res else []) + [gspec, _mod_spec()]
    out_shape = ([jax.ShapeDtypeStruct((t, D), F32)] if has_res else []) + [jax.ShapeDtypeStruct((t, D), BF16)]
    out = pl.pallas_call(
        body, name=name, grid=(bsz, nb), in_specs=in_specs, out_specs=[tok] * len(out_shape), out_shape=out_shape,
        compiler_params=_cparams(("arbitrary", "arbitrary")),
    )(*ins)
    return (out[0], out[1]) if has_res else (None, out[0])


def _norm_mod_bwd(x, dh, dres, g, mod3, i_shift, i_scale, name, mixed=None, i_gate=None):
    bsz = mod3.shape[0]
    t = x.shape[0]
    s = t // bsz
    tr = _rows(s)
    nb = s // tr
    has_res = mixed is not None

    def body(*refs):
        if has_res:
            x_ref, dh_ref, dres_ref, mx_ref, g_ref, mod_ref, dx_ref, dmx_ref, accb, accg = refs
        else:
            x_ref, dh_ref, dres_ref, g_ref, mod_ref, dx_ref, accb, accg = refs
        b, i = pl.program_id(0), pl.program_id(1)

        @pl.when(i == 0)
        def _():
            accb[...] = jnp.zeros_like(accb)

        @pl.when((i == 0) & (b == 0))
        def _():
            accg[...] = jnp.zeros_like(accg)

        xv, dhv, gv = x_ref[...], dh_ref[...], g_ref[...]
        r = lax.rsqrt(jnp.mean(xv * xv, axis=1, keepdims=True) + EPS)
        xn = xv * r
        accb[0, 0:1, :] += jnp.sum(dhv, axis=0, keepdims=True)
        accb[0, 1:2, :] += jnp.sum(dhv * (xn * gv), axis=0, keepdims=True)
        tt = dhv * (1.0 + mod_ref[0, i_scale:i_scale + 1, :])
        accg[0:1, :] += jnp.sum(tt * xn, axis=0, keepdims=True)
        dxn = tt * gv
        dx = dres_ref[...] + r * (dxn - xn * jnp.mean(dxn * xn, axis=1, keepdims=True))
        dx_ref[...] = dx
        if has_res:
            accb[0, 2:3, :] += jnp.sum(dx * mx_ref[...], axis=0, keepdims=True)
            dmx_ref[...] = (dx * mod_ref[0, i_gate:i_gate + 1, :]).astype(BF16)

    tok = _tok_spec(tr, nb)
    gspec = pl.BlockSpec((1, D), lambda b, i: (0, 0))
    ins = [x, dh, dres] + ([mixed] if has_res else []) + [g, mod3]
    in_specs = [tok] * (4 if has_res else 3) + [gspec, _mod_spec()]
    out_shape = [jax.ShapeDtypeStruct((t, D), F32)] + ([jax.ShapeDtypeStruct((t, D), BF16)] if has_res else [])
    out_specs = [tok] * len(out_shape)
    out_shape += [jax.ShapeDtypeStruct((bsz, 8, D), F32), jax.ShapeDtypeStruct((8, D), F32)]
    out_specs += [pl.BlockSpec((1, 8, D), lambda b, i: (b, 0, 0)), pl.BlockSpec((8, D), lambda b, i: (0, 0))]
    return pl.pallas_call(
        body, name=name, grid=(bsz, nb), in_specs=in_specs, out_specs=out_specs, out_shape=out_shape,
        compiler_params=_cparams(("arbitrary", "arbitrary")),
    )(*ins)


def _loss_head(x1, ff, tgt, mod3):
    bsz = mod3.shape[0]
    t = x1.shape[0]
    s = t // bsz
    tr = _rows(s)
    nb = s // tr

    def body(x1_ref, ff_ref, tg_ref, mod_ref, dy_ref, dff_ref, accb, accl):
        b, i = pl.program_id(0), pl.program_id(1)

        @pl.when(i == 0)
        def _():
            accb[...] = jnp.zeros_like(accb)

        @pl.when((i == 0) & (b == 0))
        def _():
            accl[...] = jnp.zeros_like(accl)

        gate = mod_ref[0, 5:6, :]
        ffv = ff_ref[...]
        err = x1_ref[...] + gate * ffv - tg_ref[...]
        accl[0:1, :] += jnp.sum(err * err, axis=0, keepdims=True) * (0.5 / D)
        dy = err * (1.0 / D)
        dy_ref[...] = dy
        dff_ref[...] = (dy * gate).astype(BF16)
        accb[0, 0:1, :] += jnp.sum(dy * ffv, axis=0, keepdims=True)

    tok = _tok_spec(tr, nb)
    return pl.pallas_call(
        body, name="loss_head", grid=(bsz, nb), in_specs=[tok, tok, tok, _mod_spec()],
        out_specs=[tok, tok, pl.BlockSpec((1, 8, D), lambda b, i: (b, 0, 0)), pl.BlockSpec((8, D), lambda b, i: (0, 0))],
        out_shape=[jax.ShapeDtypeStruct((t, D), F32), jax.ShapeDtypeStruct((t, D), BF16),
                   jax.ShapeDtypeStruct((bsz, 8, D), F32), jax.ShapeDtypeStruct((8, D), F32)],
        compiler_params=_cparams(("arbitrary", "arbitrary")),
    )(x1, ff, tgt, mod3)


def _merge_fwd(proj, b_merge, y_a, y_b):
    t = proj.shape[0]
    tr = _tile(t, 512)

    def body(la_ref, lb_ref, bm_ref, ya_ref, yb_ref, mix_ref):
        ga = _sigmoid(la_ref[...] + bm_ref[:, 0:D])
        gb = _sigmoid(lb_ref[...] + bm_ref[:, D:2 * D])
        mix_ref[...] = (ga * ya_ref[...].astype(F32) + gb * yb_ref[...].astype(F32)).astype(BF16)

    tok = pl.BlockSpec((tr, D), lambda i: (i, 0))
    return pl.pallas_call(
        body, name="merge_fwd", grid=(t // tr,),
        in_specs=[pl.BlockSpec((tr, D), lambda i: (i, 3)), pl.BlockSpec((tr, D), lambda i: (i, 4)),
                  pl.BlockSpec((1, 2 * D), lambda i: (0, 0)), tok, tok],
        out_specs=tok, out_shape=jax.ShapeDtypeStruct((t, D), BF16),
        compiler_params=_cparams(("arbitrary",)),
    )(proj, proj, b_merge, y_a, y_b)


def _merge_bwd(dmix, proj, b_merge, y_a, y_b):
    t = proj.shape[0]
    tr = _tile(t, 512)

    def body(dm_ref, la_ref, lb_ref, bm_ref, ya_ref, yb_ref, dya_ref, dyb_ref, dl_ref, acc):
        @pl.when(pl.program_id(0) == 0)
        def _():
            acc[...] = jnp.zeros_like(acc)

        dm = dm_ref[...].astype(F32)
        ga = _sigmoid(la_ref[...] + bm_ref[:, 0:D])
        gb = _sigmoid(lb_ref[...] + bm_ref[:, D:2 * D])
        dya_ref[...] = (dm * ga).astype(BF16)
        dyb_ref[...] = (dm * gb).astype(BF16)
        dla = dm * ya_ref[...].astype(F32) * ga * (1.0 - ga)
        dlb = dm * yb_ref[...].astype(F32) * gb * (1.0 - gb)
        dl_ref[:, 0:D] = dla.astype(BF16)
        dl_ref[:, D:2 * D] = dlb.astype(BF16)
        acc[0:1, 0:D] += jnp.sum(dla, axis=0, keepdims=True)
        acc[0:1, D:2 * D] += jnp.sum(dlb, axis=0, keepdims=True)

    tok = pl.BlockSpec((tr, D), lambda i: (i, 0))
    return pl.pallas_call(
        body, name="merge_bwd", grid=(t // tr,),
        in_specs=[tok, pl.BlockSpec((tr, D), lambda i: (i, 3)), pl.BlockSpec((tr, D), lambda i: (i, 4)),
                  pl.BlockSpec((1, 2 * D), lambda i: (0, 0)), tok, tok],
        out_specs=[tok, tok, pl.BlockSpec((tr, 2 * D), lambda i: (i, 0)), pl.BlockSpec((8, 2 * D), lambda i: (0, 0))],
        out_shape=[jax.ShapeDtypeStruct((t, D), BF16), jax.ShapeDtypeStruct((t, D), BF16),
                   jax.ShapeDtypeStruct((t, 2 * D), BF16), jax.ShapeDtypeStruct((8, 2 * D), F32)],
        compiler_params=_cparams(("arbitrary",)),
    )(dmix, proj, proj, b_merge, y_a, y_b)


def _log_sigmoid(z):
    return jnp.minimum(z, 0.0) - jnp.log(1.0 + jnp.exp(-jnp.abs(z)))


def _tri(lower):
    r = lax.broadcasted_iota(jnp.int32, (CHUNK, CHUNK), 0)
    c = lax.broadcasted_iota(jnp.int32, (CHUNK, CHUNK), 1)
    return jnp.where(r >= c if lower else r <= c, 1.0, 0.0).astype(F32)


def _gla_fwd(proj, wa_pad, b_alpha, g_out, bsz):
    t = proj.shape[0]
    s = t // bsz
    nc = s // CHUNK

    def body(q_ref, k_ref, v_ref, gg_ref, ms_ref, wa_ref, ba_ref, go_ref, o_ref, og_ref, st_ref, la, state):
        z = _dot(ms_ref[...].astype(BF16), wa_ref[...], NN) + ba_ref[...]
        la[...] = _log_sigmoid(z) * (1.0 / GTAU)
        state[...] = jnp.zeros_like(state)
        low = _tri(True)
        gout = go_ref[...]

        def chunk(n, carry):
            rows = pl.ds(pl.multiple_of(n * CHUNK, CHUNK), CHUNK)
            lac = la[rows, :]
            cum = _dot(low, lac, NN, lax.Precision.HIGHEST)
            ce = jnp.sum(lac, axis=0, keepdims=True)
            kd = (k_ref[rows, :].astype(F32) * jnp.exp(ce - cum)).astype(BF16)
            new = state[...] * jnp.exp(ce) + _dot(v_ref[rows, :].astype(BF16), kd, TN)
            state[...] = new
            st_ref[pl.ds(pl.multiple_of(n * GDV, GDV), GDV), :] = new
            qs = (q_ref[rows, :].astype(F32) * (GDK ** -0.5)).astype(BF16)
            o = _dot(qs, new.astype(BF16), NT)
            o_ref[rows, :] = o
            ro = lax.rsqrt(jnp.mean(o * o, axis=1, keepdims=True) + EPS)
            og_ref[rows, :] = (((o * ro) * gout) * _silu(gg_ref[rows, :].astype(F32))).astype(BF16)
            return carry

        lax.fori_loop(0, nc, chunk, 0, unroll=8)

    hk = pl.BlockSpec((s, GDK), lambda b, h: (b, h))
    return pl.pallas_call(
        body, name="gla_fwd", grid=(bsz, GH),
        in_specs=[hk, pl.BlockSpec((s, GDK), lambda b, h: (b, GH + h)), pl.BlockSpec((s, GDV), lambda b, h: (b, 4 + h)),
                  pl.BlockSpec((s, GDV), lambda b, h: (b, 8 + h)), pl.BlockSpec((s, LANES), lambda b, h: (b, 43)),
                  pl.BlockSpec((LANES, GDK), lambda b, h: (0, h)), pl.BlockSpec((1, GDK), lambda b, h: (0, h)),
                  pl.BlockSpec((1, GDV), lambda b, h: (0, 0))],
        out_specs=[pl.BlockSpec((s, GDV), lambda b, h: (b, h)), pl.BlockSpec((s, GDV), lambda b, h: (b, h)),
                   pl.BlockSpec((nc * GDV, GDK), lambda b, h: (b * GH + h, 0))],
        out_shape=[jax.ShapeDtypeStruct((t, GH * GDV), F32), jax.ShapeDtypeStruct((t, GH * GDV), BF16),
                   jax.ShapeDtypeStruct((bsz * GH * nc * GDV, GDK), F32)],
        scratch_shapes=[pltpu.VMEM((s, GDK), F32), pltpu.VMEM((GDV, GDK), F32)],
        compiler_params=_cparams(("arbitrary", "arbitrary")),
    )(proj, proj, proj, proj, proj, wa_pad, b_alpha, g_out)


def _gla_bwd(dog, o, proj, wa_pad, b_alpha, g_out, states, bsz):
    t = proj.shape[0]
    s = t // bsz
    nc = s // CHUNK

    def body(dog_ref, o_ref, q_ref, k_ref, v_ref, gg_ref, ms_ref, wa_ref, ba_ref, go_ref, st_ref,
             dq_ref, dk_ref, dv_ref, dgg_ref, dz_ref, dba, dgo, zs, la, carry_g):
        @pl.when(pl.program_id(1) == 0)
        def _():
            dba[...] = jnp.zeros_like(dba)
            dgo[...] = jnp.zeros_like(dgo)

        z = _dot(ms_ref[...].astype(BF16), wa_ref[...], NN) + ba_ref[...]
        zs[...] = z
        la[...] = _log_sigmoid(z) * (1.0 / GTAU)
        carry_g[...] = jnp.zeros_like(carry_g)
        low, upp = _tri(True), _tri(False)
        gout = go_ref[...]
        last_row = lax.broadcasted_iota(jnp.int32, (CHUNK, GDK), 0) == CHUNK - 1

        def chunk(step, carry):
            n = nc - 1 - step
            rows = pl.ds(pl.multiple_of(n * CHUNK, CHUNK), CHUNK)
            lac = la[rows, :]
            cum = _dot(low, lac, NN, lax.Precision.HIGHEST)
            ce = jnp.sum(lac, axis=0, keepdims=True)
            e = jnp.exp(ce - cum)
            dec = jnp.exp(ce)
            kf = k_ref[rows, :].astype(F32)
            kd = (kf * e).astype(BF16)
            vv = v_ref[rows, :].astype(BF16)
            qs = (q_ref[rows, :].astype(F32) * (GDK ** -0.5)).astype(BF16)
            ov = o_ref[rows, :]
            ro = lax.rsqrt(jnp.mean(ov * ov, axis=1, keepdims=True) + EPS)
            on = ov * ro
            gg = gg_ref[rows, :].astype(F32)
            sg = _sigmoid(gg)
            dogv = dog_ref[rows, :].astype(F32)
            dgg_ref[rows, :] = (dogv * (on * gout) * (sg * (1.0 + gg * (1.0 - sg)))).astype(BF16)
            t1 = dogv * (gg * sg)
            dgo[0:1, :] += jnp.sum(t1 * on, axis=0, keepdims=True)
            don = t1 * gout
            do = ro * (don - on * jnp.mean(don * on, axis=1, keepdims=True))
            dob = do.astype(BF16)
            st_n = st_ref[pl.ds(pl.multiple_of(n * GDV, GDV), GDV), :]
            dq_ref[rows, :] = (_dot(dob, st_n.astype(BF16), NN) * (GDK ** -0.5)).astype(BF16)
            dn = carry_g[...] + _dot(dob, qs, TN)
            prev = jnp.maximum(n - 1, 0)
            st_p = st_ref[pl.ds(pl.multiple_of(prev * GDV, GDV), GDV), :] * jnp.where(n > 0, 1.0, 0.0)
            ddec = jnp.sum(dn * st_p, axis=0, keepdims=True)
            dnb = dn.astype(BF16)
            dkd = _dot(vv, dnb, NN)
            dv_ref[rows, :] = _dot(kd, dnb, NT).astype(BF16)
            dk_ref[rows, :] = (dkd * e).astype(BF16)
            w = dkd * kf * e
            dce = jnp.sum(w, axis=0, keepdims=True) + ddec * dec
            dcum = jnp.where(last_row, dce - w, -w)
            dla = _dot(upp, dcum, NN, lax.Precision.HIGHEST)
            dz = dla * (1.0 / GTAU) * _sigmoid(-zs[rows, :])
            dba[0:1, :] += jnp.sum(dz, axis=0, keepdims=True)
            dz_ref[rows, :] = dz.astype(BF16)
            carry_g[...] = dn * dec
            return carry

        lax.fori_loop(0, nc, chunk, 0, unroll=8)

    hv = pl.BlockSpec((s, GDV), lambda h, b: (b, h))
    hk = pl.BlockSpec((s, GDK), lambda h, b: (b, h))
    return pl.pallas_call(
        body, name="gla_bwd", grid=(GH, bsz),
        in_specs=[hv, hv, hk, pl.BlockSpec((s, GDK), lambda h, b: (b, GH + h)),
                  pl.BlockSpec((s, GDV), lambda h, b: (b, 4 + h)), pl.BlockSpec((s, GDV), lambda h, b: (b, 8 + h)),
                  pl.BlockSpec((s, LANES), lambda h, b: (b, 43)), pl.BlockSpec((LANES, GDK), lambda h, b: (0, h)),
                  pl.BlockSpec((1, GDK), lambda h, b: (0, h)), pl.BlockSpec((1, GDV), lambda h, b: (0, 0)),
                  pl.BlockSpec((nc * GDV, GDK), lambda h, b: (b * GH + h, 0))],
        out_specs=[hk, hk, hv, hv, hk, pl.BlockSpec((8, GDK), lambda h, b: (0, h)),
                   pl.BlockSpec((8, GDV), lambda h, b: (h, 0))],
        out_shape=[jax.ShapeDtypeStruct((t, GH * GDK), BF16), jax.ShapeDtypeStruct((t, GH * GDK), BF16),
                   jax.ShapeDtypeStruct((t, GH * GDV), BF16), jax.ShapeDtypeStruct((t, GH * GDV), BF16),
                   jax.ShapeDtypeStruct((t, GH * GDK), BF16), jax.ShapeDtypeStruct((8, GH * GDK), F32),
                   jax.ShapeDtypeStruct((8 * GH, GDV), F32)],
        scratch_shapes=[pltpu.VMEM((s, GDK), F32), pltpu.VMEM((s, GDK), F32), pltpu.VMEM((GDV, GDK), F32)],
        compiler_params=_cparams(("arbitrary", "arbitrary")),
    )(dog, o, proj, proj, proj, proj, proj, wa_pad, b_alpha, g_out, states)


def _rope_tables(pos_ref, fr_ref, sg_ref):
    ang = pos_ref[...].astype(F32) * fr_ref[...]
    return jnp.cos(ang), jnp.sin(ang) * sg_ref[...]


def _partner(x):
    lane = lax.broadcasted_iota(jnp.int32, x.shape, 1)
    return jnp.where(lane < NOPE + ROPE // 2, pltpu.roll(x, LANES - ROPE // 2, 1), pltpu.roll(x, ROPE // 2, 1))


def _mla_rows(t):
    return _tile(t, 512)


def _mla_pre_fwd(proj, pos, fr, sg, q_lat_g, kv_lat_g, qn_g, kn_g, wuq, wukv):
    t = proj.shape[0]
    tr = _mla_rows(t)

    def body(cq_ref, ckv_ref, ms_ref, pos_ref, fr_ref, sg_ref, qlg, kvlg, qng, kng, wuq_ref, wukv_ref, q_out, k_out, v_out):
        cos, sin = _rope_tables(pos_ref, fr_ref, sg_ref)
        cq = cq_ref[...].astype(F32)
        cqn = (cq * lax.rsqrt(jnp.mean(cq * cq, axis=1, keepdims=True) + EPS) * qlg[...]).astype(BF16)
        ckv = ckv_ref[...].astype(F32)
        ckvn = (ckv * lax.rsqrt(jnp.mean(ckv * ckv, axis=1, keepdims=True) + EPS) * kvlg[...]).astype(BF16)
        lane = lax.broadcasted_iota(jnp.int32, (tr, HP), 1)
        kpe = jnp.where((lane >= NOPE) & (lane < MQK), ms_ref[...].astype(F32), 0.0)
        lane_all = lax.broadcasted_iota(jnp.int32, (tr, MH * HP), 1)
        v_out[...] = jnp.where(lane_all % HP == MV, 1.0, _dot(ckvn, wukv_ref[:, MH * HP:], NN)).astype(BF16)
        for h in range(MH):
            cols = slice(h * HP, (h + 1) * HP)
            qh = _dot(cqn, wuq_ref[:, cols], NN)
            qn = qh * lax.rsqrt(jnp.sum(qh * qh, axis=1, keepdims=True) * (1.0 / MQK) + EPS) * qng[...]
            q_out[:, cols] = (qn * cos + _partner(qn) * sin).astype(BF16)
            kh = _dot(ckvn, wukv_ref[:, cols], NN) + kpe
            kn = kh * lax.rsqrt(jnp.sum(kh * kh, axis=1, keepdims=True) * (1.0 / MQK) + EPS) * kng[...]
            k_out[:, cols] = (kn * cos + _partner(kn) * sin).astype(BF16)

    def full(a):
        return pl.BlockSpec(a.shape, lambda i: (0, 0))

    wide = pl.BlockSpec((tr, MH * HP), lambda i: (i, 0))
    return pl.pallas_call(
        body, name="mla_pre_fwd", grid=(t // tr,),
        in_specs=[pl.BlockSpec((tr, MQR), lambda i: (i, 20)), pl.BlockSpec((tr, MKVR), lambda i: (i, 42)),
                  pl.BlockSpec((tr, LANES), lambda i: (i, 43)), pl.BlockSpec((tr, 1), lambda i: (i, 0)),
                  full(fr), full(sg), full(q_lat_g), full(kv_lat_g), full(qn_g), full(kn_g), full(wuq), full(wukv)],
        out_specs=[wide, wide, wide],
        out_shape=[jax.ShapeDtypeStruct((t, MH * HP), BF16)] * 3,
        compiler_params=_cparams(("arbitrary",)),
    )(proj, proj, proj, pos, fr, sg, q_lat_g, kv_lat_g, qn_g, kn_g, wuq, wukv)


def _mla_pre_bwd(dq2, dk2, dv2, dmisc_gla, proj, pos, fr, sg, q_lat_g, kv_lat_g, qn_g, kn_g, wuq, wukv):
    t = proj.shape[0]
    tr = _mla_rows(t)

    def body(dq_ref, dk_ref, dv_ref, dmg_ref, cq_ref, ckv_ref, ms_ref, pos_ref, fr_ref, sg_ref, qlg, kvlg, qng, kng,
             wuq_ref, wukv_ref, dcq_ref, dckv_ref, dms_ref, dwuq, dwukv, acc, dqf, dkvf):
        @pl.when(pl.program_id(0) == 0)
        def _():
            dwuq[...] = jnp.zeros_like(dwuq)
            dwukv[...] = jnp.zeros_like(dwukv)
            acc[...] = jnp.zeros_like(acc)

        cos, sin = _rope_tables(pos_ref, fr_ref, sg_ref)
        cq = cq_ref[...].astype(F32)
        rc = lax.rsqrt(jnp.mean(cq * cq, axis=1, keepdims=True) + EPS)
        xc = cq * rc
        cqn = (xc * qlg[...]).astype(BF16)
        ckv = ckv_ref[...].astype(F32)
        rkv = lax.rsqrt(jnp.mean(ckv * ckv, axis=1, keepdims=True) + EPS)
        xkv = ckv * rkv
        ckvn = (xkv * kvlg[...]).astype(BF16)
        lane = lax.broadcasted_iota(jnp.int32, (tr, HP), 1)
        is_rope = (lane >= NOPE) & (lane < MQK)
        kpe = jnp.where(is_rope, ms_ref[...].astype(F32), 0.0)
        dkpe = jnp.zeros((tr, HP), F32)
        dqng = jnp.zeros((1, HP), F32)
        dkng = jnp.zeros((1, HP), F32)
        for h in range(MH):
            cols = slice(h * HP, (h + 1) * HP)
            qh = _dot(cqn, wuq_ref[:, cols], NN)
            rq = lax.rsqrt(jnp.sum(qh * qh, axis=1, keepdims=True) * (1.0 / MQK) + EPS)
            xq = qh * rq
            dy = dq_ref[:, cols].astype(F32)
            dqn = dy * cos - _partner(dy) * sin
            dqng += jnp.sum(dqn * xq, axis=0, keepdims=True)
            tq = dqn * qng[...]
            dqf[:, cols] = (rq * (tq - xq * (jnp.sum(tq * xq, axis=1, keepdims=True) * (1.0 / MQK)))).astype(BF16)
            kh = _dot(ckvn, wukv_ref[:, cols], NN) + kpe
            rk = lax.rsqrt(jnp.sum(kh * kh, axis=1, keepdims=True) * (1.0 / MQK) + EPS)
            xk = kh * rk
            dy = dk_ref[:, cols].astype(F32)
            dkn = dy * cos - _partner(dy) * sin
            dkng += jnp.sum(dkn * xk, axis=0, keepdims=True)
            tk = dkn * kng[...]
            dkh = rk * (tk - xk * (jnp.sum(tk * xk, axis=1, keepdims=True) * (1.0 / MQK)))
            dkvf[:, cols] = jnp.where(lane < NOPE, dkh, 0.0).astype(BF16)
            dkpe += jnp.where(is_rope, dkh, 0.0)
        dkvf[:, MH * HP:] = dv_ref[...]
        acc[2:3, 0:HP] += dqng
        acc[3:4, 0:HP] += dkng
        dms_ref[...] = (dmg_ref[...] + dkpe).astype(BF16)

        dqfv = dqf[...]
        dwuq[...] += _dot(cqn, dqfv, TN)
        dcqn = _dot(dqfv, wuq_ref[...], NT)
        acc[0:1, :] += jnp.sum(dcqn * xc, axis=0, keepdims=True)
        tc = dcqn * qlg[...]
        dcq_ref[...] = (rc * (tc - xc * jnp.mean(tc * xc, axis=1, keepdims=True))).astype(BF16)

        dkvfv = dkvf[...]
        dwukv[...] += _dot(ckvn, dkvfv, TN)
        dckvn = _dot(dkvfv, wukv_ref[...], NT)
        acc[1:2, 0:MKVR] += jnp.sum(dckvn * xkv, axis=0, keepdims=True)
        tkv = dckvn * kvlg[...]
        dckv_ref[...] = (rkv * (tkv - xkv * jnp.mean(tkv * xkv, axis=1, keepdims=True))).astype(BF16)

    def full(a):
        return pl.BlockSpec(a.shape, lambda i: (0, 0))

    wide = pl.BlockSpec((tr, MH * HP), lambda i: (i, 0))
    narrow = pl.BlockSpec((tr, LANES), lambda i: (i, 0))
    return pl.pallas_call(
        body, name="mla_pre_bwd", grid=(t // tr,),
        in_specs=[wide, wide, wide, narrow,
                  pl.BlockSpec((tr, MQR), lambda i: (i, 20)), pl.BlockSpec((tr, MKVR), lambda i: (i, 42)),
                  pl.BlockSpec((tr, LANES), lambda i: (i, 43)), pl.BlockSpec((tr, 1), lambda i: (i, 0)),
                  full(fr), full(sg), full(q_lat_g), full(kv_lat_g), full(qn_g), full(kn_g), full(wuq), full(wukv)],
        out_specs=[pl.BlockSpec((tr, MQR), lambda i: (i, 0)), narrow, narrow,
                   pl.BlockSpec((MQR, MH * HP), lambda i: (0, 0)), pl.BlockSpec((MKVR, 2 * MH * HP), lambda i: (0, 0)),
                   pl.BlockSpec((8, MQR), lambda i: (0, 0))],
        out_shape=[jax.ShapeDtypeStruct((t, MQR), BF16), jax.ShapeDtypeStruct((t, MKVR), BF16),
                   jax.ShapeDtypeStruct((t, LANES), BF16), jax.ShapeDtypeStruct((MQR, MH * HP), F32),
                   jax.ShapeDtypeStruct((MKVR, 2 * MH * HP), F32), jax.ShapeDtypeStruct((8, MQR), F32)],
        scratch_shapes=[pltpu.VMEM((tr, MH * HP), BF16), pltpu.VMEM((tr, 2 * MH * HP), BF16)],
        compiler_params=_cparams(("arbitrary",)),
    )(dq2, dk2, dv2, dmisc_gla, proj, proj, proj, pos, fr, sg, q_lat_g, kv_lat_g, qn_g, kn_g, wuq, wukv)


ATT_FWD_TILES = (1024, 512)
ATT_BWD_TILES = (512, 512)
ATT_HEADS = 2
NEG = -1e30
LOG2E = 1.4426950408889634


def _att_mask(q0, k0, tq, tk):
    qc = (q0 + lax.broadcasted_iota(jnp.int32, (tq, tk), 0)) // CHUNK
    kc = (k0 + lax.broadcasted_iota(jnp.int32, (tq, tk), 1)) // CHUNK
    return kc <= qc


def _att_tiles(s, tiles):
    return _tile(s, tiles[0]), _tile(s, tiles[1])


def _lanes(x, n):
    return x if n == 1 else jnp.concatenate([x] * n, axis=1)


def _grid_ends(grid):
    i, j = pl.program_id(0), pl.program_id(1)
    return (i == 0) & (j == 0), (i == grid[0] - 1) & (j == grid[1] - 1)


def _attn_fwd(q2, k2, v2, bsz, slab):
    t = q2.shape[0]
    s = t // bsz
    tq, tk = _att_tiles(s, ATT_FWD_TILES)
    nq, groups, n_diag = s // tq, tk // HP, max(tq // tk, 1)
    sub_rows = tq // n_diag
    scale = MQK ** -0.5
    c2 = scale * LOG2E
    heads = range(ATT_HEADS)

    def body(q_ref, k_ref, v_ref, slab_ref, o_ref, lse_ref, gath_ref, send_sems, recv_sems, local_sem):
        gather = _core_row_gather_copies(slab_ref, gath_ref, send_sems, recv_sems, local_sem)
        first, last = _grid_ends((bsz, MH // ATT_HEADS))
        pl.when(first)(gather.start)

        def q_loop(qi, carry):
            q0 = pl.multiple_of(qi * tq, tq)
            rows = pl.ds(q0, tq)
            n_full = q0 // tk
            qs = [q_ref[rows, h * HP:(h + 1) * HP] for h in heads]

            def scores(h, kj, sub=None, masked=False):
                k0 = pl.multiple_of(kj * tk, tk)
                qv = qs[h] if sub is None else qs[h][sub * sub_rows:(sub + 1) * sub_rows]
                sc = _dot(qv, k_ref[pl.ds(k0, tk), h * HP:(h + 1) * HP], NT)
                return jnp.where(_att_mask(q0 + sub * sub_rows, k0, sub_rows, tk), sc, NEG) if masked else sc

            def fold(mx, sc):
                for j in range(groups):
                    mx = jnp.maximum(mx, sc[:, j * HP:(j + 1) * HP])
                return mx

            def over_diagonal(vals, step):
                out = []
                for h in heads:
                    blocks = []
                    for r in range(n_diag):
                        v = vals[h][r * sub_rows:(r + 1) * sub_rows]
                        for u in range(r + 1):
                            v = step(v, h, n_full + u, r, u == r)
                        blocks.append(v)
                    out.append(blocks[0] if n_diag == 1 else jnp.concatenate(blocks, axis=0))
                return tuple(out)

            mx = lax.fori_loop(0, n_full, lambda kj, mx: tuple(fold(mx[h], scores(h, kj)) for h in heads),
                               tuple(jnp.full((tq, HP), NEG, F32) for _ in heads))
            mx = over_diagonal(mx, lambda v, h, kj, r, masked: fold(v, scores(h, kj, r, masked)))
            mb = [jnp.broadcast_to(jnp.max(mx[h], axis=1, keepdims=True), (tq, HP)) for h in heads]

            def weighted(h, kj, sub=None, masked=False):
                m = mb[h] if sub is None else mb[h][sub * sub_rows:(sub + 1) * sub_rows]
                p = jnp.exp2((scores(h, kj, sub, masked) - _lanes(m, groups)) * c2)
                k0 = pl.multiple_of(kj * tk, tk)
                return _dot(p.astype(BF16), v_ref[pl.ds(k0, tk), h * HP:(h + 1) * HP], NN)

            acc = lax.fori_loop(0, n_full, lambda kj, acc: tuple(acc[h] + weighted(h, kj) for h in heads),
                                tuple(jnp.zeros((tq, HP), F32) for _ in heads))
            acc = over_diagonal(acc, lambda v, h, kj, r, masked: v + weighted(h, kj, r, masked))
            lane = lax.broadcasted_iota(jnp.int32, (tq, HP), 1)
            for h in heads:
                a = acc[h]
                l = jnp.sum(jnp.where(lane == MV, a, 0.0), axis=1, keepdims=True)
                o_ref[rows, h * HP:(h + 1) * HP] = (a / l).astype(BF16)
                lse_ref[rows, h * HP:(h + 1) * HP] = mb[h] * scale + jnp.log(l)
            return carry

        lax.fori_loop(0, nq, q_loop, 0)
        pl.when(last)(gather.finish)

    spec = pl.BlockSpec((s, ATT_HEADS * HP), lambda b, h: (b, h))
    return pl.pallas_call(
        body, name="attn_fwd", grid=(bsz, MH // ATT_HEADS), in_specs=[spec] * 3 + [HBM_SPEC],
        out_specs=[spec, spec, HBM_SPEC],
        out_shape=[jax.ShapeDtypeStruct((t, MH * HP), BF16), jax.ShapeDtypeStruct((t, MH * HP), F32),
                   jax.ShapeDtypeStruct((N_DEV,) + slab.shape, slab.dtype)],
        scratch_shapes=EXCHANGE_SEMS, compiler_params=_cparams(("arbitrary", "arbitrary")),
    )(q2, k2, v2, slab)


def _attn_bwd(q2, k2, v2, do2, o2, lse2, bsz, tsum):
    t = q2.shape[0]
    s = t // bsz
    tq, tk = _att_tiles(s, ATT_BWD_TILES)
    nq, nk, per, groups = s // tq, s // tk, max(tk // tq, 1), tk // HP
    scale = MQK ** -0.5
    c2 = scale * LOG2E
    heads = range(ATT_HEADS)

    def body(q_ref, k_ref, v_ref, do_ref, o_ref, lse_ref, t_ref, dq_ref, dk_ref, dv_ref, parts_ref, dq_acc, delta, lse_b2,
             send_sems, recv_sems, local_sem):
        exchange = _all_to_all_copies(t_ref, parts_ref, send_sems, recv_sems, local_sem)
        first, last = _grid_ends((bsz, MH // ATT_HEADS))
        pl.when(first)(exchange.start)
        dq_acc[...] = jnp.zeros_like(dq_acc)

        def d_loop(i, carry):
            rows = pl.ds(pl.multiple_of(i * tq, tq), tq)
            for h in heads:
                hs = slice(h * HP, (h + 1) * HP)
                dl = jnp.sum(do_ref[rows, hs].astype(F32) * o_ref[rows, hs].astype(F32), axis=1, keepdims=True)
                delta[rows, hs] = jnp.broadcast_to(dl, (tq, HP))
            lse_b2[rows, :] = lse_ref[rows, :] * LOG2E
            return carry

        lax.fori_loop(0, nq, d_loop, 0)

        def k_loop(kj, carry):
            k0 = pl.multiple_of(kj * tk, tk)
            kk = [k_ref[pl.ds(k0, tk), h * HP:(h + 1) * HP] for h in heads]
            vv = [v_ref[pl.ds(k0, tk), h * HP:(h + 1) * HP] for h in heads]

            def tile(qi, c, masked):
                q0 = pl.multiple_of(qi * tq, tq)
                rows = pl.ds(q0, tq)
                out = []
                for h in heads:
                    hs = slice(h * HP, (h + 1) * HP)
                    dk, dv = c[h]
                    q = q_ref[rows, hs]
                    do = do_ref[rows, hs]
                    e = _dot(q, kk[h], NT) * c2 - _lanes(lse_b2[rows, hs], groups)
                    if masked:
                        e = jnp.where(_att_mask(q0, k0, tq, tk), e, NEG)
                    p = jnp.exp2(e)
                    dv = dv + _dot(p.astype(BF16), do, TN)
                    ds = (p * (_dot(do, vv[h], NT) - _lanes(delta[rows, hs], groups))).astype(BF16)
                    dq_acc[rows, hs] += _dot(ds, kk[h], NN)
                    dk = dk + _dot(ds, q, TN)
                    out.append((dk, dv))
                return tuple(out)

            zero = jnp.zeros((tk, HP), F32)
            c = tuple((zero, zero) for _ in heads)
            first = k0 // tq
            for u in range(per):
                c = tile(first + u, c, True)
            c = lax.fori_loop(first + per, nq, lambda qi, c: tile(qi, c, False), c)
            for h in heads:
                dk_ref[pl.ds(k0, tk), h * HP:(h + 1) * HP] = (c[h][0] * scale).astype(BF16)
                dv_ref[pl.ds(k0, tk), h * HP:(h + 1) * HP] = c[h][1].astype(BF16)
            return carry

        lax.fori_loop(0, nk, k_loop, 0)
        dq_ref[...] = (dq_acc[...] * scale).astype(BF16)
        pl.when(last)(exchange.finish)

    spec = pl.BlockSpec((s, ATT_HEADS * HP), lambda b, h: (b, h))
    return pl.pallas_call(
        body, name="attn_bwd", grid=(bsz, MH // ATT_HEADS), in_specs=[spec] * 6 + [HBM_SPEC],
        out_specs=[spec] * 3 + [HBM_SPEC],
        out_shape=[jax.ShapeDtypeStruct((t, MH * HP), BF16)] * 3 + [jax.ShapeDtypeStruct(tsum.shape, tsum.dtype)],
        scratch_shapes=[pltpu.VMEM((s, ATT_HEADS * HP), F32)] * 3 + EXCHANGE_SEMS,
        compiler_params=_cparams(("arbitrary", "arbitrary")),
    )(q2, k2, v2, do2, o2, lse2, tsum)


def _perm_w_in_t(w):
    z = lambda n: jnp.zeros((n, w.shape[1]), w.dtype)
    return jnp.concatenate([w[:3072], w[3504:5552], w[3088:3344], w[3344:3472], w[3072:3088], z(48), w[3472:3504], z(32)],
                           axis=0)


def _unperm_w_in_t(g):
    return jnp.concatenate([g[:3072], g[5504:5520], g[5120:5376], g[5376:5504], g[5568:5600], g[3072:5120]], axis=0)


def _pad_wa(w):
    return jnp.pad(w, ((0, LANES - GLR), (0, 0)))


def _pad_wuq(w):
    return jnp.pad(w.reshape(MQR, MH, MQK), ((0, 0), (0, 0), (0, HP - MQK))).reshape(MQR, MH * HP)


def _unpad_wuq(g):
    return g.reshape(MQR, MH, HP)[:, :, :MQK].reshape(MQR, MH * MQK)


def _pad_wukv(w):
    w3 = w.reshape(MKVR, MH, NOPE + MV)
    kp = jnp.pad(w3[:, :, :NOPE], ((0, 0), (0, 0), (0, HP - NOPE))).reshape(MKVR, MH * HP)
    vp = jnp.pad(w3[:, :, NOPE:], ((0, 0), (0, 0), (0, HP - MV))).reshape(MKVR, MH * HP)
    return jnp.concatenate([kp, vp], axis=1)


def _unpad_wukv(g):
    kp = g[:, :MH * HP].reshape(MKVR, MH, HP)[:, :, :NOPE]
    vp = g[:, MH * HP:].reshape(MKVR, MH, HP)[:, :, :MV]
    return jnp.concatenate([kp, vp], axis=2).reshape(MKVR, MH * (NOPE + MV))


def _pad_wo(w):
    return jnp.pad(w.reshape(MH, MV, D), ((0, 0), (0, HP - MV), (0, 0))).reshape(MH * HP, D)


def _unpad_wo(g):
    return g.reshape(MH, HP, D)[:, :MV, :].reshape(MH * MV, D)


def _pad_lanes(v, n=HP):
    return jnp.pad(v, ((0, 0), (0, n - v.shape[1])))


def _local_step(x, positions, tgt, wt, slab_b, sp, mod3):
    bsz, s, _ = x.shape
    t = bsz * s
    x2 = x.reshape(t, D)
    tgt2 = tgt.reshape(t, D)
    pos = positions.reshape(t, 1)
    fr16 = ROPE_THETA ** (-jnp.arange(0, ROPE, 2, dtype=F32) / ROPE)
    zero = lambda n: jnp.zeros((n,), F32)
    fr = jnp.concatenate([zero(NOPE), fr16, fr16, zero(HP - MQK)]).reshape(1, HP)
    sg = jnp.concatenate([zero(NOPE), -jnp.ones((ROPE // 2,), F32), jnp.ones((ROPE // 2,), F32), zero(HP - MQK)]).reshape(1, HP)

    w_in_t = _perm_w_in_t(wt["w_in"])
    wa_pad = _pad_wa(wt["gla_w_alpha"].T)
    wuq = _pad_wuq(wt["mla_w_uq"].T)
    wukv = _pad_wukv(wt["mla_w_ukv"].T)
    qn_g, kn_g = _pad_lanes(sp["mla_qn_g"]), _pad_lanes(sp["mla_kn_g"])

    _, h = _norm_mod_fwd(x2, sp["norm1_g"], mod3, 0, 1, "norm1_fwd")
    proj = _mm(h, w_in_t, "nt", (BF16,), "proj_fwd")
    o_gla, og, states = _gla_fwd(proj, wa_pad, sp["gla_b_alpha"], sp["gla_out_norm_g"], bsz)
    q2, k2, v2 = _mla_pre_fwd(proj, pos, fr, sg, sp["mla_q_lat_g"], sp["mla_kv_lat_g"], qn_g, kn_g, wuq, wukv)
    o2, lse2, core_row = _attn_fwd(q2, k2, v2, bsz, slab_b)
    wt = dict(wt, **_unpack_gathered(_cross_core_fill(core_row), SLAB_B))
    wo_pad = _pad_wo(wt["mla_w_o"])
    y_a = _mm(og, wt["gla_w_o"], "nn", (BF16,), "gla_out_fwd")
    y_b = _mm(o2, wo_pad, "nn", (BF16,), "mla_out_fwd")
    mix = _merge_fwd(proj, sp["b_merge"], y_a, y_b)
    mixed = _mm(mix, wt["w_out"], "nn", (F32,), "w_out_fwd")

    x1, h2 = _norm_mod_fwd(x2, sp["norm2_g"], mod3, 3, 4, "norm2_fwd", mixed=mixed, i_gate=2)
    a, f = _mm(h2, wt["mlp_w1"], "nt", (BF16, BF16), "mlp1_fwd",
               epi=lambda acc: (acc, jnp.square(jnp.maximum(acc, 0.0))))
    ff = _mm(f, wt["mlp_w2"], "nn", (F32,), "mlp2_fwd")
    dy, dff, acc_g2, acc_loss = _loss_head(x1, ff, tgt2, mod3)

    gw = {}
    gw["mlp_w2"] = _mm(f, dff, "tn", (BF16,), "mlp2_dw")
    da = _mm(dff, wt["mlp_w2"], "nt", (BF16,), "mlp2_dx", extras=(a,),
             epi=lambda acc, av: (acc * (2.0 * jnp.maximum(av.astype(F32), 0.0)),))
    gw["mlp_w1"] = _mm(da, h2, "tn", (BF16,), "mlp1_dw")
    dh2 = _mm(da, wt["mlp_w1"], "nn", (F32,), "mlp1_dx")
    dx1, dmixed, accb2, accg2 = _norm_mod_bwd(x1, dh2, dy, sp["norm2_g"], mod3, 3, 4, "norm2_bwd", mixed=mixed, i_gate=2)

    gw["w_out"] = _mm(mix, dmixed, "tn", (BF16,), "w_out_dw")
    dmix = _mm(dmixed, wt["w_out"], "nt", (BF16,), "w_out_dx")
    dy_a, dy_b, dlogits, acc_bm = _merge_bwd(dmix, proj, sp["b_merge"], y_a, y_b)
    gw["gla_w_o"] = _mm(og, dy_a, "tn", (BF16,), "gla_out_dw")
    dog = _mm(dy_a, wt["gla_w_o"], "nt", (BF16,), "gla_out_dx")
    gw["mla_w_o"] = _unpad_wo(_mm(o2, dy_b, "tn", (BF16,), "mla_out_dw"))
    do2 = _mm(dy_b, wo_pad, "nt", (BF16,), "mla_out_dx")
    dq2, dk2, dv2, parts_b = _attn_bwd(q2, k2, v2, do2, o2, lse2, bsz, _pack_per_device(gw, SLAB_B))
    dq_g, dk_g, dv_g, dgg, dz, acc_ba, acc_go = _gla_bwd(dog, o_gla, proj, wa_pad, sp["gla_b_alpha"],
                                                         sp["gla_out_norm_g"], states, bsz)
    gw["gla_w_alpha"] = _mm(proj, dz, "tn", (F32,), "gla_alpha_dw", a_off=43, m=LANES)[:GLR].T.astype(BF16)
    dmisc_gla = _mm(dz, wa_pad, "nt", (F32,), "gla_alpha_dx")
    dcq, dckv, dmisc, gwuq, gwukv, acc_mla = _mla_pre_bwd(dq2, dk2, dv2, dmisc_gla, proj, pos, fr, sg, sp["mla_q_lat_g"],
                                                         sp["mla_kv_lat_g"], qn_g, kn_g, wuq, wukv)
    gw["mla_w_uq"] = _unpad_wuq(gwuq).T.astype(BF16)
    gw["mla_w_ukv"] = _unpad_wukv(gwukv).T.astype(BF16)
    dproj = jnp.concatenate([dq_g, dk_g, dv_g, dgg, dlogits, dcq, dckv, dmisc], axis=1)
    gw["w_in"] = _unperm_w_in_t(_mm(dproj, h, "tn", (BF16,), "proj_dw"))
    dh, parts_a = _mm(dproj, w_in_t, "nn", (F32,), "proj_dx", cargo=_sum_over_cores(_pack_per_device(gw, SLAB_A), "a"))
    grad_x, accb1, accg1 = _norm_mod_bwd(x2, dh, dx1, sp["norm1_g"], mod3, 0, 1, "norm1_bwd")

    dmod = jnp.stack([accb1[:, 0], accb1[:, 1], accb2[:, 2], accb2[:, 0], accb2[:, 1], acc_g2[:, 0]], axis=1)

    rows = {
        "dmod": dmod.reshape(bsz * 6, D),
        "norm1_g": accg1[0:1], "norm2_g": accg2[0:1],
        "b_merge": acc_bm[0:1].reshape(2, D),
        "gla_b_alpha": _pad_lanes(acc_ba[0:1], D),
        "gla_out_norm_g": _pad_lanes(acc_go.reshape(GH, 8, GDV)[:, 0, :], D),
        "mla_q_lat_g": _pad_lanes(acc_mla[0:1], D), "mla_kv_lat_g": _pad_lanes(acc_mla[1:2], D),
        "mla_qn_g": _pad_lanes(acc_mla[2:3], D), "mla_kn_g": _pad_lanes(acc_mla[3:4], D),
        "loss": acc_loss[0:1],
    }
    return grad_x.reshape(bsz, s, D), parts_a, parts_b, rows


HBM_SPEC = pl.BlockSpec(memory_space=pltpu.HBM)


def _all_gather(p, name):
    r, cdim = p.shape

    def body(p_ref, out_ref, send_sems, recv_sems, local_sem):
        x, y, c = lax.axis_index("x"), lax.axis_index("y"), lax.axis_index("c")
        me, sibling = (x, y, c), (x, y, 1 - c)
        chips = [(1 - x, y), (x, 1 - y), (1 - x, 1 - y)]

        def slot(px, py, pc):
            return out_ref.at[4 * px + 2 * py + pc]

        def copy(k, block, to, src=None):
            return pltpu.make_async_remote_copy(
                src_ref=slot(*block) if src is None else src, dst_ref=slot(*block),
                send_sem=send_sems.at[k], recv_sem=recv_sems.at[k], device_id=to, device_id_type=MESH)

        mine = pltpu.make_async_copy(p_ref, slot(*me), local_sem)
        mine.start()
        first = [copy(0, me, sibling, src=p_ref)] + [copy(1 + j, me, (*chip, c), src=p_ref) for j, chip in enumerate(chips)]
        for cp in first:
            cp.start()
        passed = [copy(4 + j, (*chip, c), sibling) for j, chip in enumerate(chips)]
        for j, chip in enumerate(chips):
            copy(1 + j, (*chip, c), me).wait_recv()
            passed[j].start()
        copy(0, sibling, me).wait_recv()
        for j, chip in enumerate(chips):
            copy(4 + j, (*chip, 1 - c), me).wait_recv()
        for cp in first + passed:
            cp.wait_send()
        mine.wait()

    return pl.pallas_call(
        body, name=name, out_shape=jax.ShapeDtypeStruct((N_DEV, r, cdim), p.dtype),
        in_specs=[HBM_SPEC], out_specs=HBM_SPEC,
        scratch_shapes=[pltpu.SemaphoreType.DMA((7,)), pltpu.SemaphoreType.DMA((7,)), pltpu.SemaphoreType.DMA(())],
    )(p)


def _sibling_exchange(g, name):
    def body(g_ref, out_ref, send_sem, recv_sem):
        x, y, c = lax.axis_index("x"), lax.axis_index("y"), lax.axis_index("c")
        cp = pltpu.make_async_remote_copy(src_ref=g_ref, dst_ref=out_ref, send_sem=send_sem, recv_sem=recv_sem,
                                          device_id=(x, y, 1 - c), device_id_type=MESH)
        cp.start()
        cp.wait()

    return pl.pallas_call(
        body, name=name, out_shape=jax.ShapeDtypeStruct(g.shape, g.dtype),
        in_specs=[HBM_SPEC], out_specs=HBM_SPEC,
        scratch_shapes=[pltpu.SemaphoreType.DMA(()), pltpu.SemaphoreType.DMA(())],
    )(g)


class _Exchange:
    def __init__(self, local, sends, arrivals):
        self.local, self.sends, self.arrivals = local, sends, arrivals

    def start(self):
        self.local.start()
        for cp in self.sends:
            cp.start()

    def finish(self):
        for cp in self.arrivals:
            cp.wait_recv()
        for cp in self.sends:
            cp.wait_send()
        self.local.wait()


EXCHANGE_SEMS = [pltpu.SemaphoreType.DMA((N_DEV,)), pltpu.SemaphoreType.DMA((N_DEV,)), pltpu.SemaphoreType.DMA(())]


def _all_to_all_copies(t_ref, out_ref, send_sems, recv_sems, local_sem):
    x, y, c = lax.axis_index("x"), lax.axis_index("y"), lax.axis_index("c")
    me = 4 * x + 2 * y + c

    def copy(k, src, dst):
        px, py, pc = (1 - x if k & 4 else x), (1 - y if k & 2 else y), (1 - c if k & 1 else c)
        peer = 4 * px + 2 * py + pc
        return pltpu.make_async_remote_copy(src_ref=t_ref.at[peer if src is None else src],
                                            dst_ref=out_ref.at[peer if dst is None else dst], send_sem=send_sems.at[k],
                                            recv_sem=recv_sems.at[k], device_id=(px, py, pc), device_id_type=MESH)

    return _Exchange(pltpu.make_async_copy(t_ref.at[me], out_ref.at[me], local_sem),
                     [copy(k, None, me) for k in range(1, N_DEV)], [copy(k, me, None) for k in range(1, N_DEV)])


def _chip_exchange_copies(t_ref, out_ref, send_sems, recv_sems, local_sem):
    x, y, c = lax.axis_index("x"), lax.axis_index("y"), lax.axis_index("c")
    my_chip = 2 * x + y
    chips = [(1 - x, y), (x, 1 - y), (1 - x, 1 - y)]

    def copy(j, src, dst, px, py):
        return pltpu.make_async_remote_copy(src_ref=t_ref.at[src], dst_ref=out_ref.at[dst], send_sem=send_sems.at[j],
                                            recv_sem=recv_sems.at[j], device_id=(px, py, c), device_id_type=MESH)

    return _Exchange(pltpu.make_async_copy(t_ref.at[my_chip], out_ref.at[my_chip], local_sem),
                     [copy(j, 2 * px + py, my_chip, px, py) for j, (px, py) in enumerate(chips)],
                     [copy(j, my_chip, 2 * px + py, px, py) for j, (px, py) in enumerate(chips)])


def _core_row_gather_copies(p_ref, out_ref, send_sems, recv_sems, local_sem):
    x, y, c = lax.axis_index("x"), lax.axis_index("y"), lax.axis_index("c")
    peers = [(x, y, 1 - c), (1 - x, y, c), (x, 1 - y, c), (1 - x, 1 - y, c)]

    def slot(px, py, pc):
        return out_ref.at[4 * px + 2 * py + pc]

    def copy(j, block, to):
        return pltpu.make_async_remote_copy(src_ref=p_ref, dst_ref=slot(*block), send_sem=send_sems.at[j],
                                            recv_sem=recv_sems.at[j], device_id=to, device_id_type=MESH)

    return _Exchange(pltpu.make_async_copy(p_ref, slot(x, y, c), local_sem),
                     [copy(j, (x, y, c), peer) for j, peer in enumerate(peers)],
                     [copy(j, peer, peer) for j, peer in enumerate(peers)])


def _cross_core_fill(gathered):
    def body(g_ref, out_ref, send_sems, recv_sems):
        x, y, c = lax.axis_index("x"), lax.axis_index("y"), lax.axis_index("c")
        chips = [(1 - x, y), (x, 1 - y), (1 - x, 1 - y)]

        def copy(j, pc):
            px, py = chips[j]
            slot = 4 * px + 2 * py + pc
            return pltpu.make_async_remote_copy(src_ref=g_ref.at[slot], dst_ref=out_ref.at[slot], send_sem=send_sems.at[j],
                                                recv_sem=recv_sems.at[j], device_id=(x, y, 1 - c), device_id_type=MESH)

        sends = [copy(j, c) for j in range(3)]
        for cp in sends:
            cp.start()
        for j in range(3):
            copy(j, 1 - c).wait_recv()
        for cp in sends:
            cp.wait_send()

    return pl.pallas_call(
        body, name="weights_cross_core_fill", out_shape=jax.ShapeDtypeStruct(gathered.shape, gathered.dtype),
        in_specs=[HBM_SPEC], out_specs=HBM_SPEC, input_output_aliases={0: 0},
        scratch_shapes=[pltpu.SemaphoreType.DMA((3,)), pltpu.SemaphoreType.DMA((3,))],
    )(gathered)


def _slab_block(r):
    return max(b for b in range(16, SLAB_BLOCK_MAX + 1, 16) if r % b == 0)


def _pair_sum(a, b, name):
    n, r, cdim = a.shape
    rb = _slab_block(r)
    blk = pl.BlockSpec((1, rb, cdim), lambda j, i: (j, i, 0))

    def body(a_ref, b_ref, o_ref):
        o_ref[...] = (a_ref[...].astype(F32) + b_ref[...].astype(F32)).astype(BF16)

    return pl.pallas_call(
        body, name=name, grid=(n, r // rb), in_specs=[blk, blk], out_specs=blk,
        out_shape=jax.ShapeDtypeStruct(a.shape, BF16), compiler_params=_cparams(("arbitrary", "arbitrary")),
    )(a, b)


def _adamw_math(w, g, m, v):
    m = ADAM_B1 * m + (1.0 - ADAM_B1) * g
    v = ADAM_B2 * v + (1.0 - ADAM_B2) * jnp.square(g)
    m_hat = m / (1.0 - ADAM_B1 ** ADAM_STEP)
    v_hat = v / (1.0 - ADAM_B2 ** ADAM_STEP)
    delta = -ADAM_LR * (m_hat / (jnp.sqrt(v_hat) + ADAM_EPS) + ADAM_WD * w)
    return delta, m, v


def _slab_sum(parts, name):
    n, r, cdim = parts.shape
    rb = _slab_block(r)
    blk = pl.BlockSpec((rb, cdim), lambda i: (i, 0))

    def body(p_ref, g_out):
        g = p_ref[0].astype(F32)
        for j in range(1, n):
            g = g + p_ref[j].astype(F32)
        g_out[...] = g

    return pl.pallas_call(
        body, name=name, grid=(r // rb,),
        in_specs=[pl.BlockSpec((n, rb, cdim), lambda i: (0, i, 0))], out_specs=blk,
        out_shape=jax.ShapeDtypeStruct((r, cdim), F32), compiler_params=_cparams(("arbitrary",)),
    )(parts)


def _adamw(g, w, m, v, name):
    r, cdim = w.shape
    rb = _tile(r, 256)
    blk = pl.BlockSpec((rb, cdim), lambda i: (i, 0))

    def body(g_ref, w_ref, m_ref, v_ref, d_out, m_out, v_out):
        d_out[...], m_out[...], v_out[...] = _adamw_math(w_ref[...], g_ref[...], m_ref[...], v_ref[...])

    return pl.pallas_call(
        body, name=name, grid=(r // rb,), in_specs=[blk] * 4, out_specs=[blk] * 3,
        out_shape=[jax.ShapeDtypeStruct((r, cdim), F32)] * 3, compiler_params=_cparams(("arbitrary",)),
    )(g, w, m, v)


def _adamw_small(parts, w, m, v):
    def body(p_ref, w_ref, m_ref, v_ref, g_out, d_out, m_out, v_out, loss_out):
        def total(srcs):
            acc = None
            for r in srcs:
                for j in range(N_DEV):
                    term = p_ref[j, r:r + 1, :]
                    acc = term if acc is None else acc + term
            return acc

        for prow, srcs in enumerate(SMALL_SOURCES):
            one = slice(prow, prow + 1)
            g = total(srcs)
            g_out[one, :] = g
            d_out[one, :], m_out[one, :], v_out[one, :] = _adamw_math(w_ref[one, :], g, m_ref[one, :], v_ref[one, :])
        loss_out[...] = jnp.broadcast_to(jnp.sum(total(LOSS_SOURCE), axis=1, keepdims=True), (8, LANES))

    full = lambda shp: pl.BlockSpec(shp, lambda i: (0,) * len(shp))
    return pl.pallas_call(
        body, name="adamw_small", grid=(1,),
        in_specs=[full((N_DEV, SMALL_ROWS, D)), full((16, D)), full((16, D)), full((16, D))],
        out_specs=[full((16, D))] * 4 + [full((8, LANES))],
        out_shape=[jax.ShapeDtypeStruct((16, D), F32)] * 4 + [jax.ShapeDtypeStruct((8, LANES), F32)],
        compiler_params=_cparams(("arbitrary",)),
    )(parts, w, m, v)


def _to_slab(shard, form):
    if form == "N":
        return shard
    return shard.T if form == "T" else shard.T.reshape(-1, D)


def _from_slab(block, form, shard_shape):
    if form == "N":
        return block
    return block.T if form == "T" else block.reshape(shard_shape[1], shard_shape[0]).T


def _gathered_full(g, form, shard_shape):
    if form == "TR":
        return g.reshape(N_DEV * shard_shape[1], shard_shape[0])
    return g.reshape(N_DEV * g.shape[1], D)


def _pack_slab(shards, layout):
    return jnp.concatenate([jnp.zeros((r, D), BF16) if n is None else _to_slab(shards[n], form).astype(BF16)
                            for n, r, form in layout], axis=0)


def _unpack_gathered(gathered, layout):
    out, off = {}, 0
    for n, r, form in layout:
        if n is not None:
            out[n] = _gathered_full(gathered[:, off:off + r], form, SHARD_SHAPES[n])
        off += r
    return out


def _pack_per_device(gw, layout):
    return jnp.concatenate([jnp.zeros((N_DEV, r, D), BF16) if n is None else gw[n].reshape(N_DEV, r, D)
                            for n, r, _ in layout], axis=1)


def _unpack_shards(gslab, layout):
    out, off = {}, 0
    for n, r, form in layout:
        if n is not None:
            out[n] = _from_slab(gslab[off:off + r], form, SHARD_SHAPES[n])
        off += r
    return out


def _sum_over_cores(per_dev, tag):
    my_c = lax.axis_index("c")
    pairs = per_dev.reshape(4, 2, per_dev.shape[1], D)
    keep = lax.dynamic_index_in_dim(pairs, my_c, axis=1, keepdims=False)
    give = lax.dynamic_index_in_dim(pairs, 1 - my_c, axis=1, keepdims=False)
    return _pair_sum(keep, _sibling_exchange(give, "rs_sibling_exchange_" + tag), "rs_pair_sum_" + tag)


def _small_pack(vals):
    rows = []
    for n in SMALL:
        v = vals[n].reshape(-1)
        k = -(-v.shape[0] // D)
        rows.append(jnp.pad(v, (0, k * D - v.shape[0])).reshape(k, D))
    return jnp.concatenate(rows, axis=0)


def _small_unpack(packed, shapes):
    out = {}
    for n in SMALL:
        k = shapes[n][-1]
        r0 = SMALL_ROW[n]
        out[n] = packed[r0:r0 + -(-k // D)].reshape(-1)[:k].reshape(shapes[n])
    return out


def kernel(x, c, positions, w_ada, b_ada, norm1_g, w_in, b_merge, gla_w_alpha, gla_b_alpha, gla_out_norm_g, gla_w_o, mla_q_lat_g, mla_w_uq, mla_kv_lat_g, mla_w_ukv, mla_qn_g, mla_kn_g, mla_w_o, w_out, norm2_g, mlp_w1, mlp_w2, loss_target, m_w_ada, m_b_ada, m_norm1_g, m_w_in, m_b_merge, m_gla_w_alpha, m_gla_b_alpha, m_gla_out_norm_g, m_gla_w_o, m_mla_q_lat_g, m_mla_w_uq, m_mla_kv_lat_g, m_mla_w_ukv, m_mla_qn_g, m_mla_kn_g, m_mla_w_o, m_w_out, m_norm2_g, m_mlp_w1, m_mlp_w2, v_w_ada, v_b_ada, v_norm1_g, v_w_in, v_b_merge, v_gla_w_alpha, v_gla_b_alpha, v_gla_out_norm_g, v_gla_w_o, v_mla_q_lat_g, v_mla_w_uq, v_mla_kv_lat_g, v_mla_w_ukv, v_mla_qn_g, v_mla_kn_g, v_mla_w_o, v_w_out, v_norm2_g, v_mlp_w1, v_mlp_w2):
    args = dict(locals())
    wts = {n: args[n][0] for n in WEIGHTS}
    mom = {n: args["m_" + n][0] for n in WEIGHTS}
    var = {n: args["v_" + n][0] for n in WEIGHTS}
    my_c = lax.axis_index("c")
    my_dev = 4 * lax.axis_index("x") + 2 * lax.axis_index("y") + my_c
    bsz = x.shape[0]
    sp = {n: wts[n].reshape(1, -1) for n in SMALL}

    wt = _unpack_gathered(_all_gather(_pack_slab(wts, SLAB_A), "weights_all_gather"), SLAB_A)

    c_all = _all_gather(jnp.pad(c, ((0, 8 - bsz), (0, 0))), "cond_all_gather")[:, :bsz].reshape(N_DEV * bsz, D)
    bias = lax.dynamic_slice_in_dim(sp["b_ada"], my_dev * ADA_COLS, ADA_COLS, axis=1)
    mod_cols = _mm(c_all, wts["w_ada"], "nn", (F32,), "ada_fwd", pro=_silu, epi=lambda acc, b: (acc + b,),
                   extras=(jnp.broadcast_to(bias, (N_DEV * bsz, ADA_COLS)),))
    mod_all = _all_gather(mod_cols, "mod_all_gather")
    mod_mine = lax.dynamic_slice_in_dim(mod_all, my_dev * bsz, bsz, axis=1)
    mod3 = jnp.transpose(mod_mine, (1, 0, 2)).reshape(bsz, 6, D)

    grad_x, parts_a, parts_b, rows = _local_step(x, positions, loss_target, wt, _pack_slab(wts, SLAB_B), sp, mod3)

    grads = dict(_unpack_shards(_slab_sum(parts_a, "rs_slab_sum_a"), SLAB_A),
                 **_unpack_shards(_slab_sum(parts_b, "rs_slab_sum_b"), SLAB_B))
    big = {n: (g,) + tuple(_adamw(g, wts[n], mom[n], var[n], "adamw_" + n)) for n, g in grads.items()}

    order = ["dmod", "norm1_g", "norm2_g", "b_merge", "gla_b_alpha", "gla_out_norm_g", "mla_q_lat_g", "mla_kv_lat_g",
             "mla_qn_g", "mla_kn_g", "loss"]
    part_rows = jnp.concatenate([rows[n] for n in order], axis=0)
    part_rows = jnp.pad(part_rows, ((0, SMALL_ROWS - part_rows.shape[0]), (0, 0)))
    all_rows = _all_gather(part_rows, "partials_all_gather")

    dmod_all = all_rows[:, :6 * bsz].reshape(N_DEV * bsz, 6 * D)
    dmod_cols = lax.dynamic_slice_in_dim(dmod_all, my_dev * ADA_COLS, ADA_COLS, axis=1)
    g_ada = _mm(c_all, dmod_cols, "tn", (F32,), "ada_dw", pro=_silu)
    big["w_ada"] = (g_ada,) + tuple(_adamw(g_ada, wts["w_ada"], mom["w_ada"], var["w_ada"], "adamw_w_ada"))

    small = _adamw_small(all_rows, _small_pack({n: wts[n] for n in SMALL}), _small_pack({n: mom[n] for n in SMALL}),
                         _small_pack({n: var[n] for n in SMALL}))
    loss = small[4][0, 0]
    small_shapes = {n: wts[n].shape for n in SMALL}
    small = [_small_unpack(o, small_shapes) for o in small[:4]]

    outs = [loss, grad_x]
    for k in range(4):
        for n in WEIGHTS:
            val = big[n][k] if n in BIG else small[k][n]
            outs.append(val.reshape((1,) + tuple(wts[n].shape)))
    return tuple(outs)
```

```python
import functools

import jax
import jax.numpy as jnp
from jax import lax
from jax.experimental import pallas as pl
from jax.experimental.pallas import tpu as pltpu

F32 = jnp.float32
BF16 = jnp.bfloat16
MESH = pl.DeviceIdType.MESH

D = 1024
EPS = 1e-6
CHUNK = 64
GH, GDK, GDV, GLR, GTAU = 4, 128, 256, 16, 16.0
MH, MQR, MKVR, NOPE, ROPE, MV = 16, 256, 128, 64, 32, 64
MQK = NOPE + ROPE
HP = 128
FF = 4 * D
ROPE_THETA = 10000.0
IN_WIDTH = 5552
PW = 5632
N_DEV = 8
LANES = 128
SLAB_BLOCK_MAX = 400
ADA_COLS = 6 * D // N_DEV
SMALL_ROWS = 32
SMALL_SOURCES = tuple([(r, 6 + r) for r in range(6)] + [(12,), (13,), (14,), (15,), (16,), (17, 18, 19, 20),
                                                         (21,), (22,), (23,), (24,)])
LOSS_SOURCE = (25,)
VMEM_LIMIT = 56 * 1024 * 1024

ADAM_LR, ADAM_B1, ADAM_B2, ADAM_EPS, ADAM_WD, ADAM_STEP = 0.001, 0.9, 0.999, 1e-08, 0.01, 10

SLAB_A = (("w_in", 694, "T"), ("gla_w_alpha", 1, "TR"), (None, 9, None), ("mla_w_uq", 48, "TR"), ("mla_w_ukv", 32, "TR"))
SLAB_B = (("mlp_w1", 512, "T"), ("gla_w_o", 128, "N"), ("mla_w_o", 128, "N"), ("w_out", 128, "N"), ("mlp_w2", 512, "N"))
BIG = ("w_ada",) + tuple(n for n, _, _ in SLAB_A + SLAB_B if n is not None)
SHARD_SHAPES = {"w_ada": (D, 6 * D // N_DEV), "w_in": (D, IN_WIDTH // N_DEV), "gla_w_alpha": (GLR, GH * GDK // N_DEV),
                "gla_w_o": (GH * GDV // N_DEV, D), "mla_w_uq": (MQR, MH * MQK // N_DEV),
                "mla_w_ukv": (MKVR, MH * (NOPE + MV) // N_DEV), "mla_w_o": (MH * MV // N_DEV, D), "w_out": (D // N_DEV, D),
                "mlp_w1": (D, FF // N_DEV), "mlp_w2": (FF // N_DEV, D)}
SMALL = ("b_ada", "norm1_g", "norm2_g", "b_merge", "gla_b_alpha", "gla_out_norm_g", "mla_q_lat_g", "mla_kv_lat_g",
         "mla_qn_g", "mla_kn_g")
SMALL_ROW = {"b_ada": 0, "norm1_g": 6, "norm2_g": 7, "b_merge": 8, "gla_b_alpha": 10, "gla_out_norm_g": 11,
             "mla_q_lat_g": 12, "mla_kv_lat_g": 13, "mla_qn_g": 14, "mla_kn_g": 15}
WEIGHTS = ("w_ada", "b_ada", "norm1_g", "w_in", "b_merge", "gla_w_alpha", "gla_b_alpha", "gla_out_norm_g", "gla_w_o",
           "mla_q_lat_g", "mla_w_uq", "mla_kv_lat_g", "mla_w_ukv", "mla_qn_g", "mla_kn_g", "mla_w_o", "w_out",
           "norm2_g", "mlp_w1", "mlp_w2")


def _cparams(sem=None):
    return pltpu.CompilerParams(dimension_semantics=sem, vmem_limit_bytes=VMEM_LIMIT)


def _tile(n, pref):
    for t in (2048, PW // 4, 1024, 512, 256, 128):
        if t <= pref and n % t == 0:
            return t
    return n


def _dot(a, b, dims, precision=None):
    return lax.dot_general(a, b, (dims, ((), ())), preferred_element_type=F32, precision=precision)


NN = ((1,), (0,))
NT = ((1,), (1,))
TN = ((0,), (0,))


def _sigmoid(x):
    return 1.0 / (1.0 + jnp.exp(-x))


def _silu(x):
    return x * _sigmoid(x)


def _mm(a, b, mode, out_dtypes, name, *, pro=None, pro_b=None, epi=None, extras=(), a_off=0, m=None, tm=2048, tn=1024,
        tk=1024, cargo=None):
    if mode == "tn":
        kc, n = b.shape
        m = a.shape[1] if m is None else m
    elif mode == "nn":
        m, kc = a.shape
        n = b.shape[1]
    else:
        m, kc = a.shape
        n = b.shape[0]
    tm, tn, tk = _tile(m, tm), _tile(n, tn), _tile(kc, tk)
    nk = kc // tk
    dims = {"nn": NN, "nt": NT, "tn": TN}[mode]
    if mode == "tn":
        a_spec = pl.BlockSpec((tk, tm), lambda i, j, k: (k, i + a_off))
    else:
        a_spec = pl.BlockSpec((tm, tk), lambda i, j, k: (i + a_off, k))
    if mode == "nt":
        b_spec = pl.BlockSpec((tn, tk), lambda i, j, k: (j, k))
    else:
        b_spec = pl.BlockSpec((tk, tn), lambda i, j, k: (k, j))
    o_spec = pl.BlockSpec((tm, tn), lambda i, j, k: (i, j))
    n_ex, n_out = len(extras), len(out_dtypes)
    grid = (m // tm, n // tn, nk)
    has_cargo = cargo is not None

    def body(a_ref, b_ref, *rest):
        ex, rest = rest[:n_ex], rest[n_ex:]
        if has_cargo:
            outs, acc = rest[1:1 + n_out], rest[2 + n_out]
            exchange = _chip_exchange_copies(rest[0], rest[1 + n_out], *rest[3 + n_out:])
            steps = [pl.program_id(axis) for axis in range(3)]
            first = (steps[0] == 0) & (steps[1] == 0) & (steps[2] == 0)
            last = (steps[0] == grid[0] - 1) & (steps[1] == grid[1] - 1) & (steps[2] == grid[2] - 1)
            pl.when(first)(exchange.start)
        else:
            outs, acc = rest[:n_out], rest[n_out]
        k = pl.program_id(2)

        @pl.when(k == 0)
        def _():
            acc[...] = jnp.zeros_like(acc)

        av = a_ref[...]
        if pro is not None:
            av = pro(av)
        bv = b_ref[...]
        if pro_b is not None:
            bv = pro_b(bv)
        acc[...] += _dot(av.astype(BF16), bv.astype(BF16), dims)

        @pl.when(k == nk - 1)
        def _():
            res = (acc[...],) if epi is None else epi(acc[...], *[e[...] for e in ex])
            for o_ref, r in zip(outs, res):
                o_ref[...] = r.astype(o_ref.dtype)

        if has_cargo:
            pl.when(last)(exchange.finish)

    cargo_in = [cargo] if has_cargo else []
    cargo_spec = [HBM_SPEC] * len(cargo_in)
    out = pl.pallas_call(
        body, name=name, grid=grid,
        in_specs=[a_spec, b_spec] + [o_spec] * n_ex + cargo_spec,
        out_specs=[o_spec] * n_out + cargo_spec,
        out_shape=[jax.ShapeDtypeStruct((m, n), dt) for dt in out_dtypes]
        + [jax.ShapeDtypeStruct(c.shape, c.dtype) for c in cargo_in],
        scratch_shapes=[pltpu.VMEM((tm, tn), F32)] + (EXCHANGE_SEMS if has_cargo else []),
        compiler_params=_cparams(("arbitrary",) * 3 if has_cargo else ("parallel", "parallel", "arbitrary")),
    )(a, b, *extras, *cargo_in)
    return out[0] if len(out) == 1 else out


def _rows(s):
    return _tile(s, 512)


def _mod_spec():
    return pl.BlockSpec((1, 6, D), lambda b, i: (b, 0, 0))


def _tok_spec(tr, nb, width=D, col=0):
    return pl.BlockSpec((tr, width), lambda b, i: (b * nb + i, col))


def _norm_mod_fwd(x, g, mod3, i_shift, i_scale, name, mixed=None, i_gate=None):
    bsz, _, _ = mod3.shape
    t = x.shape[0]
    s = t // bsz
    tr = _rows(s)
    nb = s // tr
    has_res = mixed is not None

    def body(*refs):
        if has_res:
            x_ref, mx_ref, g_ref, mod_ref, x1_ref, h_ref = refs
            xv = x_ref[...] + mod_ref[0, i_gate:i_gate + 1, :] * mx_ref[...]
            x1_ref[...] = xv
        else:
            x_ref, g_ref, mod_ref, h_ref = refs
            xv = x_ref[...]
        r = lax.rsqrt(jnp.mean(xv * xv, axis=1, keepdims=True) + EPS)
        hn = (xv * r) * g_ref[...]
        h = hn * (1.0 + mod_ref[0, i_scale:i_scale + 1, :]) + mod_ref[0, i_shift:i_shift + 1, :]
        h_ref[...] = h.astype(BF16)

    tok = _tok_spec(tr, nb)
    gspec = pl.BlockSpec((1, D), lambda b, i: (0, 0))
    ins = [x] + ([mixed] if has_res else []) + [g, mod3]
    in_specs = [tok] + ([tok] if has_res else []) + [gspec, _mod_spec()]
    out_shape = ([jax.ShapeDtypeStruct((t, D), F32)] if has_res else []) + [jax.ShapeDtypeStruct((t, D), BF16)]
    out = pl.pallas_call(
        body, name=name, grid=(bsz, nb), in_specs=in_specs, out_specs=[tok] * len(out_shape), out_shape=out_shape,
        compiler_params=_cparams(("arbitrary", "arbitrary")),
    )(*ins)
    return (out[0], out[1]) if has_res else (None, out[0])


def _norm_mod_bwd(x, dh, dres, g, mod3, i_shift, i_scale, name, mixed=None, i_gate=None):
    bsz = mod3.shape[0]
    t = x.shape[0]
    s = t // bsz
    tr = _rows(s)
    nb = s // tr
    has_res = mixed is not None

    def body(*refs):
        if has_res:
            x_ref, dh_ref, dres_ref, mx_ref, g_ref, mod_ref, dx_ref, dmx_ref, accb, accg = refs
        else:
            x_ref, dh_ref, dres_ref, g_ref, mod_ref, dx_ref, accb, accg = refs
        b, i = pl.program_id(0), pl.program_id(1)

        @pl.when(i == 0)
        def _():
            accb[...] = jnp.zeros_like(accb)

        @pl.when((i == 0) & (b == 0))
        def _():
            accg[...] = jnp.zeros_like(accg)

        xv, dhv, gv = x_ref[...], dh_ref[...], g_ref[...]
        r = lax.rsqrt(jnp.mean(xv * xv, axis=1, keepdims=True) + EPS)
        xn = xv * r
        accb[0, 0:1, :] += jnp.sum(dhv, axis=0, keepdims=True)
        accb[0, 1:2, :] += jnp.sum(dhv * (xn * gv), axis=0, keepdims=True)
        tt = dhv * (1.0 + mod_ref[0, i_scale:i_scale + 1, :])
        accg[0:1, :] += jnp.sum(tt * xn, axis=0, keepdims=True)
        dxn = tt * gv
        dx = dres_ref[...] + r * (dxn - xn * jnp.mean(dxn * xn, axis=1, keepdims=True))
        dx_ref[...] = dx
        if has_res:
            accb[0, 2:3, :] += jnp.sum(dx * mx_ref[...], axis=0, keepdims=True)
            dmx_ref[...] = (dx * mod_ref[0, i_gate:i_gate + 1, :]).astype(BF16)

    tok = _tok_spec(tr, nb)
    gspec = pl.BlockSpec((1, D), lambda b, i: (0, 0))
    ins = [x, dh, dres] + ([mixed] if has_res else []) + [g, mod3]
    in_specs = [tok] * (4 if has_res else 3) + [gspec, _mod_spec()]
    out_shape = [jax.ShapeDtypeStruct((t, D), F32)] + ([jax.ShapeDtypeStruct((t, D), BF16)] if has_res else [])
    out_specs = [tok] * len(out_shape)
    out_shape += [jax.ShapeDtypeStruct((bsz, 8, D), F32), jax.ShapeDtypeStruct((8, D), F32)]
    out_specs += [pl.BlockSpec((1, 8, D), lambda b, i: (b, 0, 0)), pl.BlockSpec((8, D), lambda b, i: (0, 0))]
    return pl.pallas_call(
        body, name=name, grid=(bsz, nb), in_specs=in_specs, out_specs=out_specs, out_shape=out_shape,
        compiler_params=_cparams(("arbitrary", "arbitrary")),
    )(*ins)


def _loss_head(x1, ff, tgt, mod3):
    bsz = mod3.shape[0]
    t = x1.shape[0]
    s = t // bsz
    tr = _rows(s)
    nb = s // tr

    def body(x1_ref, ff_ref, tg_ref, mod_ref, dy_ref, dff_ref, accb, accl):
        b, i = pl.program_id(0), pl.program_id(1)

        @pl.when(i == 0)
        def _():
            accb[...] = jnp.zeros_like(accb)

        @pl.when((i == 0) & (b == 0))
        def _():
            accl[...] = jnp.zeros_like(accl)

        gate = mod_ref[0, 5:6, :]
        ffv = ff_ref[...]
        err = x1_ref[...] + gate * ffv - tg_ref[...]
        accl[0:1, :] += jnp.sum(err * err, axis=0, keepdims=True) * (0.5 / D)
        dy = err * (1.0 / D)
        dy_ref[...] = dy
        dff_ref[...] = (dy * gate).astype(BF16)
        accb[0, 0:1, :] += jnp.sum(dy * ffv, axis=0, keepdims=True)

    tok = _tok_spec(tr, nb)
    return pl.pallas_call(
        body, name="loss_head", grid=(bsz, nb), in_specs=[tok, tok, tok, _mod_spec()],
        out_specs=[tok, tok, pl.BlockSpec((1, 8, D), lambda b, i: (b, 0, 0)), pl.BlockSpec((8, D), lambda b, i: (0, 0))],
        out_shape=[jax.ShapeDtypeStruct((t, D), F32), jax.ShapeDtypeStruct((t, D), BF16),
                   jax.ShapeDtypeStruct((bsz, 8, D), F32), jax.ShapeDtypeStruct((8, D), F32)],
        compiler_params=_cparams(("arbitrary", "arbitrary")),
    )(x1, ff, tgt, mod3)


def _merge_fwd(proj, b_merge, y_a, y_b):
    t = proj.shape[0]
    tr = _tile(t, 512)

    def body(la_ref, lb_ref, bm_ref, ya_ref, yb_ref, mix_ref):
        ga = _sigmoid(la_ref[...] + bm_ref[:, 0:D])
        gb = _sigmoid(lb_ref[...] + bm_ref[:, D:2 * D])
        mix_ref[...] = (ga * ya_ref[...].astype(F32) + gb * yb_ref[...].astype(F32)).astype(BF16)

    tok = pl.BlockSpec((tr, D), lambda i: (i, 0))
    return pl.pallas_call(
        body, name="merge_fwd", grid=(t // tr,),
        in_specs=[pl.BlockSpec((tr, D), lambda i: (i, 3)), pl.BlockSpec((tr, D), lambda i: (i, 4)),
                  pl.BlockSpec((1, 2 * D), lambda i: (0, 0)), tok, tok],
        out_specs=tok, out_shape=jax.ShapeDtypeStruct((t, D), BF16),
        compiler_params=_cparams(("arbitrary",)),
    )(proj, proj, b_merge, y_a, y_b)


def _merge_bwd(dmix, proj, b_merge, y_a, y_b):
    t = proj.shape[0]
    tr = _tile(t, 512)

    def body(dm_ref, la_ref, lb_ref, bm_ref, ya_ref, yb_ref, dya_ref, dyb_ref, dl_ref, acc):
        @pl.when(pl.program_id(0) == 0)
        def _():
            acc[...] = jnp.zeros_like(acc)

        dm = dm_ref[...].astype(F32)
        ga = _sigmoid(la_ref[...] + bm_ref[:, 0:D])
        gb = _sigmoid(lb_ref[...] + bm_ref[:, D:2 * D])
        dya_ref[...] = (dm * ga).astype(BF16)
        dyb_ref[...] = (dm * gb).astype(BF16)
        dla = dm * ya_ref[...].astype(F32) * ga * (1.0 - ga)
        dlb = dm * yb_ref[...].astype(F32) * gb * (1.0 - gb)
        dl_ref[:, 0:D] = dla.astype(BF16)
        dl_ref[:, D:2 * D] = dlb.astype(BF16)
        acc[0:1, 0:D] += jnp.sum(dla, axis=0, keepdims=True)
        acc[0:1, D:2 * D] += jnp.sum(dlb, axis=0, keepdims=True)

    tok = pl.BlockSpec((tr, D), lambda i: (i, 0))
    return pl.pallas_call(
        body, name="merge_bwd", grid=(t // tr,),
        in_specs=[tok, pl.BlockSpec((tr, D), lambda i: (i, 3)), pl.BlockSpec((tr, D), lambda i: (i, 4)),
                  pl.BlockSpec((1, 2 * D), lambda i: (0, 0)), tok, tok],
        out_specs=[tok, tok, pl.BlockSpec((tr, 2 * D), lambda i: (i, 0)), pl.BlockSpec((8, 2 * D), lambda i: (0, 0))],
        out_shape=[jax.ShapeDtypeStruct((t, D), BF16), jax.ShapeDtypeStruct((t, D), BF16),
                   jax.ShapeDtypeStruct((t, 2 * D), BF16), jax.ShapeDtypeStruct((8, 2 * D), F32)],
        compiler_params=_cparams(("arbitrary",)),
    )(dmix, proj, proj, b_merge, y_a, y_b)


def _log_sigmoid(z):
    return jnp.minimum(z, 0.0) - jnp.log(1.0 + jnp.exp(-jnp.abs(z)))


def _tri(lower):
    r = lax.broadcasted_iota(jnp.int32, (CHUNK, CHUNK), 0)
    c = lax.broadcasted_iota(jnp.int32, (CHUNK, CHUNK), 1)
    return jnp.where(r >= c if lower else r <= c, 1.0, 0.0).astype(F32)


def _gla_fwd(proj, wa_pad, b_alpha, g_out, bsz):
    t = proj.shape[0]
    s = t // bsz
    nc = s // CHUNK

    def body(q_ref, k_ref, v_ref, gg_ref, ms_ref, wa_ref, ba_ref, go_ref, o_ref, og_ref, st_ref, la, state):
        z = _dot(ms_ref[...].astype(BF16), wa_ref[...], NN) + ba_ref[...]
        la[...] = _log_sigmoid(z) * (1.0 / GTAU)
        state[...] = jnp.zeros_like(state)
        low = _tri(True)
        gout = go_ref[...]

        def chunk(n, carry):
            rows = pl.ds(pl.multiple_of(n * CHUNK, CHUNK), CHUNK)
            lac = la[rows, :]
            cum = _dot(low, lac, NN, lax.Precision.HIGHEST)
            ce = jnp.sum(lac, axis=0, keepdims=True)
            kd = (k_ref[rows, :].astype(F32) * jnp.exp(ce - cum)).astype(BF16)
            new = state[...] * jnp.exp(ce) + _dot(v_ref[rows, :].astype(BF16), kd, TN)
            state[...] = new
            st_ref[pl.ds(pl.multiple_of(n * GDV, GDV), GDV), :] = new
            qs = (q_ref[rows, :].astype(F32) * (GDK ** -0.5)).astype(BF16)
            o = _dot(qs, new.astype(BF16), NT)
            o_ref[rows, :] = o
            ro = lax.rsqrt(jnp.mean(o * o, axis=1, keepdims=True) + EPS)
            og_ref[rows, :] = (((o * ro) * gout) * _silu(gg_ref[rows, :].astype(F32))).astype(BF16)
            return carry

        lax.fori_loop(0, nc, chunk, 0, unroll=8)

    hk = pl.BlockSpec((s, GDK), lambda b, h: (b, h))
    return pl.pallas_call(
        body, name="gla_fwd", grid=(bsz, GH),
        in_specs=[hk, pl.BlockSpec((s, GDK), lambda b, h: (b, GH + h)), pl.BlockSpec((s, GDV), lambda b, h: (b, 4 + h)),
                  pl.BlockSpec((s, GDV), lambda b, h: (b, 8 + h)), pl.BlockSpec((s, LANES), lambda b, h: (b, 43)),
                  pl.BlockSpec((LANES, GDK), lambda b, h: (0, h)), pl.BlockSpec((1, GDK), lambda b, h: (0, h)),
                  pl.BlockSpec((1, GDV), lambda b, h: (0, 0))],
        out_specs=[pl.BlockSpec((s, GDV), lambda b, h: (b, h)), pl.BlockSpec((s, GDV), lambda b, h: (b, h)),
                   pl.BlockSpec((nc * GDV, GDK), lambda b, h: (b * GH + h, 0))],
        out_shape=[jax.ShapeDtypeStruct((t, GH * GDV), F32), jax.ShapeDtypeStruct((t, GH * GDV), BF16),
                   jax.ShapeDtypeStruct((bsz * GH * nc * GDV, GDK), F32)],
        scratch_shapes=[pltpu.VMEM((s, GDK), F32), pltpu.VMEM((GDV, GDK), F32)],
        compiler_params=_cparams(("arbitrary", "arbitrary")),
    )(proj, proj, proj, proj, proj, wa_pad, b_alpha, g_out)


def _gla_bwd(dog, o, proj, wa_pad, b_alpha, g_out, states, bsz):
    t = proj.shape[0]
    s = t // bsz
    nc = s // CHUNK

    def body(dog_ref, o_ref, q_ref, k_ref, v_ref, gg_ref, ms_ref, wa_ref, ba_ref, go_ref, st_ref,
             dq_ref, dk_ref, dv_ref, dgg_ref, dz_ref, dba, dgo, zs, la, carry_g):
        @pl.when(pl.program_id(1) == 0)
        def _():
            dba[...] = jnp.zeros_like(dba)
            dgo[...] = jnp.zeros_like(dgo)

        z = _dot(ms_ref[...].astype(BF16), wa_ref[...], NN) + ba_ref[...]
        zs[...] = z
        la[...] = _log_sigmoid(z) * (1.0 / GTAU)
        carry_g[...] = jnp.zeros_like(carry_g)
        low, upp = _tri(True), _tri(False)
        gout = go_ref[...]
        last_row = lax.broadcasted_iota(jnp.int32, (CHUNK, GDK), 0) == CHUNK - 1

        def chunk(step, carry):
            n = nc - 1 - step
            rows = pl.ds(pl.multiple_of(n * CHUNK, CHUNK), CHUNK)
            lac = la[rows, :]
            cum = _dot(low, lac, NN, lax.Precision.HIGHEST)
            ce = jnp.sum(lac, axis=0, keepdims=True)
            e = jnp.exp(ce - cum)
            dec = jnp.exp(ce)
            kf = k_ref[rows, :].astype(F32)
            kd = (kf * e).astype(BF16)
            vv = v_ref[rows, :].astype(BF16)
            qs = (q_ref[rows, :].astype(F32) * (GDK ** -0.5)).astype(BF16)
            ov = o_ref[rows, :]
            ro = lax.rsqrt(jnp.mean(ov * ov, axis=1, keepdims=True) + EPS)
            on = ov * ro
            gg = gg_ref[rows, :].astype(F32)
            sg = _sigmoid(gg)
            dogv = dog_ref[rows, :].astype(F32)
            dgg_ref[rows, :] = (dogv * (on * gout) * (sg * (1.0 + gg * (1.0 - sg)))).astype(BF16)
            t1 = dogv * (gg * sg)
            dgo[0:1, :] += jnp.sum(t1 * on, axis=0, keepdims=True)
            don = t1 * gout
            do = ro * (don - on * jnp.mean(don * on, axis=1, keepdims=True))
            dob = do.astype(BF16)
            st_n = st_ref[pl.ds(pl.multiple_of(n * GDV, GDV), GDV), :]
            dq_ref[rows, :] = (_dot(dob, st_n.astype(BF16), NN) * (GDK ** -0.5)).astype(BF16)
            dn = carry_g[...] + _dot(dob, qs, TN)
            prev = jnp.maximum(n - 1, 0)
            st_p = st_ref[pl.ds(pl.multiple_of(prev * GDV, GDV), GDV), :] * jnp.where(n > 0, 1.0, 0.0)
            ddec = jnp.sum(dn * st_p, axis=0, keepdims=True)
            dnb = dn.astype(BF16)
            dkd = _dot(vv, dnb, NN)
            dv_ref[rows, :] = _dot(kd, dnb, NT).astype(BF16)
            dk_ref[rows, :] = (dkd * e).astype(BF16)
            w = dkd * kf * e
            dce = jnp.sum(w, axis=0, keepdims=True) + ddec * dec
            dcum = jnp.where(last_row, dce - w, -w)
            dla = _dot(upp, dcum, NN, lax.Precision.HIGHEST)
            dz = dla * (1.0 / GTAU) * _sigmoid(-zs[rows, :])
            dba[0:1, :] += jnp.sum(dz, axis=0, keepdims=True)
            dz_ref[rows, :] = dz.astype(BF16)
            carry_g[...] = dn * dec
            return carry

        lax.fori_loop(0, nc, chunk, 0, unroll=8)

    hv = pl.BlockSpec((s, GDV), lambda h, b: (b, h))
    hk = pl.BlockSpec((s, GDK), lambda h, b: (b, h))
    return pl.pallas_call(
        body, name="gla_bwd", grid=(GH, bsz),
        in_specs=[hv, hv, hk, pl.BlockSpec((s, GDK), lambda h, b: (b, GH + h)),
                  pl.BlockSpec((s, GDV), lambda h, b: (b, 4 + h)), pl.BlockSpec((s, GDV), lambda h, b: (b, 8 + h)),
                  pl.BlockSpec((s, LANES), lambda h, b: (b, 43)), pl.BlockSpec((LANES, GDK), lambda h, b: (0, h)),
                  pl.BlockSpec((1, GDK), lambda h, b: (0, h)), pl.BlockSpec((1, GDV), lambda h, b: (0, 0)),
                  pl.BlockSpec((nc * GDV, GDK), lambda h, b: (b * GH + h, 0))],
        out_specs=[hk, hk, hv, hv, hk, pl.BlockSpec((8, GDK), lambda h, b: (0, h)),
                   pl.BlockSpec((8, GDV), lambda h, b: (h, 0))],
        out_shape=[jax.ShapeDtypeStruct((t, GH * GDK), BF16), jax.ShapeDtypeStruct((t, GH * GDK), BF16),
                   jax.ShapeDtypeStruct((t, GH * GDV), BF16), jax.ShapeDtypeStruct((t, GH * GDV), BF16),
                   jax.ShapeDtypeStruct((t, GH * GDK), BF16), jax.ShapeDtypeStruct((8, GH * GDK), F32),
                   jax.ShapeDtypeStruct((8 * GH, GDV), F32)],
        scratch_shapes=[pltpu.VMEM((s, GDK), F32), pltpu.VMEM((s, GDK), F32), pltpu.VMEM((GDV, GDK), F32)],
        compiler_params=_cparams(("arbitrary", "arbitrary")),
    )(dog, o, proj, proj, proj, proj, proj, wa_pad, b_alpha, g_out, states)


def _rope_tables(pos_ref, fr_ref, sg_ref):
    ang = pos_ref[...].astype(F32) * fr_ref[...]
    return jnp.cos(ang), jnp.sin(ang) * sg_ref[...]


def _partner(x):
    lane = lax.broadcasted_iota(jnp.int32, x.shape, 1)
    return jnp.where(lane < NOPE + ROPE // 2, pltpu.roll(x, LANES - ROPE // 2, 1), pltpu.roll(x, ROPE // 2, 1))


def _mla_rows(t):
    return _tile(t, 512)


def _mla_pre_fwd(proj, pos, fr, sg, q_lat_g, kv_lat_g, qn_g, kn_g, wuq, wukv):
    t = proj.shape[0]
    tr = _mla_rows(t)

    def body(cq_ref, ckv_ref, ms_ref, pos_ref, fr_ref, sg_ref, qlg, kvlg, qng, kng, wuq_ref, wukv_ref, q_out, k_out, v_out):
        cos, sin = _rope_tables(pos_ref, fr_ref, sg_ref)
        cq = cq_ref[...].astype(F32)
        cqn = (cq * lax.rsqrt(jnp.mean(cq * cq, axis=1, keepdims=True) + EPS) * qlg[...]).astype(BF16)
        ckv = ckv_ref[...].astype(F32)
        ckvn = (ckv * lax.rsqrt(jnp.mean(ckv * ckv, axis=1, keepdims=True) + EPS) * kvlg[...]).astype(BF16)
        lane = lax.broadcasted_iota(jnp.int32, (tr, HP), 1)
        kpe = jnp.where((lane >= NOPE) & (lane < MQK), ms_ref[...].astype(F32), 0.0)
        lane_all = lax.broadcasted_iota(jnp.int32, (tr, MH * HP), 1)
        v_out[...] = jnp.where(lane_all % HP == MV, 1.0, _dot(ckvn, wukv_ref[:, MH * HP:], NN)).astype(BF16)
        for h in range(MH):
            cols = slice(h * HP, (h + 1) * HP)
            qh = _dot(cqn, wuq_ref[:, cols], NN)
            qn = qh * lax.rsqrt(jnp.sum(qh * qh, axis=1, keepdims=True) * (1.0 / MQK) + EPS) * qng[...]
            q_out[:, cols] = (qn * cos + _partner(qn) * sin).astype(BF16)
            kh = _dot(ckvn, wukv_ref[:, cols], NN) + kpe
            kn = kh * lax.rsqrt(jnp.sum(kh * kh, axis=1, keepdims=True) * (1.0 / MQK) + EPS) * kng[...]
            k_out[:, cols] = (kn * cos + _partner(kn) * sin).astype(BF16)

    def full(a):
        return pl.BlockSpec(a.shape, lambda i: (0, 0))

    wide = pl.BlockSpec((tr, MH * HP), lambda i: (i, 0))
    return pl.pallas_call(
        body, name="mla_pre_fwd", grid=(t // tr,),
        in_specs=[pl.BlockSpec((tr, MQR), lambda i: (i, 20)), pl.BlockSpec((tr, MKVR), lambda i: (i, 42)),
                  pl.BlockSpec((tr, LANES), lambda i: (i, 43)), pl.BlockSpec((tr, 1), lambda i: (i, 0)),
                  full(fr), full(sg), full(q_lat_g), full(kv_lat_g), full(qn_g), full(kn_g), full(wuq), full(wukv)],
        out_specs=[wide, wide, wide],
        out_shape=[jax.ShapeDtypeStruct((t, MH * HP), BF16)] * 3,
        compiler_params=_cparams(("arbitrary",)),
    )(proj, proj, proj, pos, fr, sg, q_lat_g, kv_lat_g, qn_g, kn_g, wuq, wukv)


def _mla_pre_bwd(dq2, dk2, dv2, dmisc_gla, proj, pos, fr, sg, q_lat_g, kv_lat_g, qn_g, kn_g, wuq, wukv):
    t = proj.shape[0]
    tr = _mla_rows(t)

    def body(dq_ref, dk_ref, dv_ref, dmg_ref, cq_ref, ckv_ref, ms_ref, pos_ref, fr_ref, sg_ref, qlg, kvlg, qng, kng,
             wuq_ref, wukv_ref, dcq_ref, dckv_ref, dms_ref, dwuq, dwukv, acc, dqf, dkvf):
        @pl.when(pl.program_id(0) == 0)
        def _():
            dwuq[...] = jnp.zeros_like(dwuq)
            dwukv[...] = jnp.zeros_like(dwukv)
            acc[...] = jnp.zeros_like(acc)

        cos, sin = _rope_tables(pos_ref, fr_ref, sg_ref)
        cq = cq_ref[...].astype(F32)
        rc = lax.rsqrt(jnp.mean(cq * cq, axis=1, keepdims=True) + EPS)
        xc = cq * rc
        cqn = (xc * qlg[...]).astype(BF16)
        ckv = ckv_ref[...].astype(F32)
        rkv = lax.rsqrt(jnp.mean(ckv * ckv, axis=1, keepdims=True) + EPS)
        xkv = ckv * rkv
        ckvn = (xkv * kvlg[...]).astype(BF16)
        lane = lax.broadcasted_iota(jnp.int32, (tr, HP), 1)
        is_rope = (lane >= NOPE) & (lane < MQK)
        kpe = jnp.where(is_rope, ms_ref[...].astype(F32), 0.0)
        dkpe = jnp.zeros((tr, HP), F32)
        dqng = jnp.zeros((1, HP), F32)
        dkng = jnp.zeros((1, HP), F32)
        for h in range(MH):
            cols = slice(h * HP, (h + 1) * HP)
            qh = _dot(cqn, wuq_ref[:, cols], NN)
            rq = lax.rsqrt(jnp.sum(qh * qh, axis=1, keepdims=True) * (1.0 / MQK) + EPS)
            xq = qh * rq
            dy = dq_ref[:, cols].astype(F32)
            dqn = dy * cos - _partner(dy) * sin
            dqng += jnp.sum(dqn * xq, axis=0, keepdims=True)
            tq = dqn * qng[...]
            dqf[:, cols] = (rq * (tq - xq * (jnp.sum(tq * xq, axis=1, keepdims=True) * (1.0 / MQK)))).astype(BF16)
            kh = _dot(ckvn, wukv_ref[:, cols], NN) + kpe
            rk = lax.rsqrt(jnp.sum(kh * kh, axis=1, keepdims=True) * (1.0 / MQK) + EPS)
            xk = kh * rk
            dy = dk_ref[:, cols].astype(F32)
            dkn = dy * cos - _partner(dy) * sin
            dkng += jnp.sum(dkn * xk, axis=0, keepdims=True)
            tk = dkn * kng[...]
            dkh = rk * (tk - xk * (jnp.sum(tk * xk, axis=1, keepdims=True) * (1.0 / MQK)))
            dkvf[:, cols] = jnp.where(lane < NOPE, dkh, 0.0).astype(BF16)
            dkpe += jnp.where(is_rope, dkh, 0.0)
        dkvf[:, MH * HP:] = dv_ref[...]
        acc[2:3, 0:HP] += dqng
        acc[3:4, 0:HP] += dkng
        dms_ref[...] = (dmg_ref[...] + dkpe).astype(BF16)

        dqfv = dqf[...]
        dwuq[...] += _dot(cqn, dqfv, TN)
        dcqn = _dot(dqfv, wuq_ref[...], NT)
        acc[0:1, :] += jnp.sum(dcqn * xc, axis=0, keepdims=True)
        tc = dcqn * qlg[...]
        dcq_ref[...] = (rc * (tc - xc * jnp.mean(tc * xc, axis=1, keepdims=True))).astype(BF16)

        dkvfv = dkvf[...]
        dwukv[...] += _dot(ckvn, dkvfv, TN)
        dckvn = _dot(dkvfv, wukv_ref[...], NT)
        acc[1:2, 0:MKVR] += jnp.sum(dckvn * xkv, axis=0, keepdims=True)
        tkv = dckvn * kvlg[...]
        dckv_ref[...] = (rkv * (tkv - xkv * jnp.mean(tkv * xkv, axis=1, keepdims=True))).astype(BF16)

    def full(a):
        return pl.BlockSpec(a.shape, lambda i: (0, 0))

    wide = pl.BlockSpec((tr, MH * HP), lambda i: (i, 0))
    narrow = pl.BlockSpec((tr, LANES), lambda i: (i, 0))
    return pl.pallas_call(
        body, name="mla_pre_bwd", grid=(t // tr,),
        in_specs=[wide, wide, wide, narrow,
                  pl.BlockSpec((tr, MQR), lambda i: (i, 20)), pl.BlockSpec((tr, MKVR), lambda i: (i, 42)),
                  pl.BlockSpec((tr, LANES), lambda i: (i, 43)), pl.BlockSpec((tr, 1), lambda i: (i, 0)),
                  full(fr), full(sg), full(q_lat_g), full(kv_lat_g), full(qn_g), full(kn_g), full(wuq), full(wukv)],
        out_specs=[pl.BlockSpec((tr, MQR), lambda i: (i, 0)), narrow, narrow,
                   pl.BlockSpec((MQR, MH * HP), lambda i: (0, 0)), pl.BlockSpec((MKVR, 2 * MH * HP), lambda i: (0, 0)),
                   pl.BlockSpec((8, MQR), lambda i: (0, 0))],
        out_shape=[jax.ShapeDtypeStruct((t, MQR), BF16), jax.ShapeDtypeStruct((t, MKVR), BF16),
                   jax.ShapeDtypeStruct((t, LANES), BF16), jax.ShapeDtypeStruct((MQR, MH * HP), F32),
                   jax.ShapeDtypeStruct((MKVR, 2 * MH * HP), F32), jax.ShapeDtypeStruct((8, MQR), F32)],
        scratch_shapes=[pltpu.VMEM((tr, MH * HP), BF16), pltpu.VMEM((tr, 2 * MH * HP), BF16)],
        compiler_params=_cparams(("arbitrary",)),
    )(dq2, dk2, dv2, dmisc_gla, proj, proj, proj, pos, fr, sg, q_lat_g, kv_lat_g, qn_g, kn_g, wuq, wukv)


ATT_FWD_TILES = (1024, 512)
ATT_BWD_TILES = (512, 512)
ATT_HEADS = 2
NEG = -1e30
LOG2E = 1.4426950408889634


def _att_mask(q0, k0, tq, tk):
    qc = (q0 + lax.broadcasted_iota(jnp.int32, (tq, tk), 0)) // CHUNK
    kc = (k0 + lax.broadcasted_iota(jnp.int32, (tq, tk), 1)) // CHUNK
    return kc <= qc


def _att_tiles(s, tiles):
    return _tile(s, tiles[0]), _tile(s, tiles[1])


def _lanes(x, n):
    return x if n == 1 else jnp.concatenate([x] * n, axis=1)


def _grid_ends(grid):
    i, j = pl.program_id(0), pl.program_id(1)
    return (i == 0) & (j == 0), (i == grid[0] - 1) & (j == grid[1] - 1)


def _attn_fwd(q2, k2, v2, bsz, slab):
    t = q2.shape[0]
    s = t // bsz
    tq, tk = _att_tiles(s, ATT_FWD_TILES)
    nq, groups, n_diag = s // tq, tk // HP, max(tq // tk, 1)
    sub_rows = tq // n_diag
    scale = MQK ** -0.5
    c2 = scale * LOG2E
    heads = range(ATT_HEADS)

    def body(q_ref, k_ref, v_ref, slab_ref, o_ref, lse_ref, gath_ref, send_sems, recv_sems, local_sem):
        gather = _core_row_gather_copies(slab_ref, gath_ref, send_sems, recv_sems, local_sem)
        first, last = _grid_ends((bsz, MH // ATT_HEADS))
        pl.when(first)(gather.start)

        def q_loop(qi, carry):
            q0 = pl.multiple_of(qi * tq, tq)
            rows = pl.ds(q0, tq)
            n_full = q0 // tk
            qs = [q_ref[rows, h * HP:(h + 1) * HP] for h in heads]

            def scores(h, kj, sub=None, masked=False):
                k0 = pl.multiple_of(kj * tk, tk)
                qv = qs[h] if sub is None else qs[h][sub * sub_rows:(sub + 1) * sub_rows]
                sc = _dot(qv, k_ref[pl.ds(k0, tk), h * HP:(h + 1) * HP], NT)
                return jnp.where(_att_mask(q0 + sub * sub_rows, k0, sub_rows, tk), sc, NEG) if masked else sc

            def fold(mx, sc):
                for j in range(groups):
                    mx = jnp.maximum(mx, sc[:, j * HP:(j + 1) * HP])
                return mx

            def over_diagonal(vals, step):
                out = []
                for h in heads:
                    blocks = []
                    for r in range(n_diag):
                        v = vals[h][r * sub_rows:(r + 1) * sub_rows]
                        for u in range(r + 1):
                            v = step(v, h, n_full + u, r, u == r)
                        blocks.append(v)
                    out.append(blocks[0] if n_diag == 1 else jnp.concatenate(blocks, axis=0))
                return tuple(out)

            mx = lax.fori_loop(0, n_full, lambda kj, mx: tuple(fold(mx[h], scores(h, kj)) for h in heads),
                               tuple(jnp.full((tq, HP), NEG, F32) for _ in heads))
            mx = over_diagonal(mx, lambda v, h, kj, r, masked: fold(v, scores(h, kj, r, masked)))
            mb = [jnp.broadcast_to(jnp.max(mx[h], axis=1, keepdims=True), (tq, HP)) for h in heads]

            def weighted(h, kj, sub=None, masked=False):
                m = mb[h] if sub is None else mb[h][sub * sub_rows:(sub + 1) * sub_rows]
                p = jnp.exp2((scores(h, kj, sub, masked) - _lanes(m, groups)) * c2)
                k0 = pl.multiple_of(kj * tk, tk)
                return _dot(p.astype(BF16), v_ref[pl.ds(k0, tk), h * HP:(h + 1) * HP], NN)

            acc = lax.fori_loop(0, n_full, lambda kj, acc: tuple(acc[h] + weighted(h, kj) for h in heads),
                                tuple(jnp.zeros((tq, HP), F32) for _ in heads))
            acc = over_diagonal(acc, lambda v, h, kj, r, masked: v + weighted(h, kj, r, masked))
            lane = lax.broadcasted_iota(jnp.int32, (tq, HP), 1)
            for h in heads:
                a = acc[h]
                l = jnp.sum(jnp.where(lane == MV, a, 0.0), axis=1, keepdims=True)
                o_ref[rows, h * HP:(h + 1) * HP] = (a / l).astype(BF16)
                lse_ref[rows, h * HP:(h + 1) * HP] = mb[h] * scale + jnp.log(l)
            return carry

        lax.fori_loop(0, nq, q_loop, 0)
        pl.when(last)(gather.finish)

    spec = pl.BlockSpec((s, ATT_HEADS * HP), lambda b, h: (b, h))
    return pl.pallas_call(
        body, name="attn_fwd", grid=(bsz, MH // ATT_HEADS), in_specs=[spec] * 3 + [HBM_SPEC],
        out_specs=[spec, spec, HBM_SPEC],
        out_shape=[jax.ShapeDtypeStruct((t, MH * HP), BF16), jax.ShapeDtypeStruct((t, MH * HP), F32),
                   jax.ShapeDtypeStruct((N_DEV,) + slab.shape, slab.dtype)],
        scratch_shapes=EXCHANGE_SEMS, compiler_params=_cparams(("arbitrary", "arbitrary")),
    )(q2, k2, v2, slab)


def _attn_bwd(q2, k2, v2, do2, o2, lse2, bsz, tsum):
    t = q2.shape[0]
    s = t // bsz
    tq, tk = _att_tiles(s, ATT_BWD_TILES)
    nq, nk, per, groups = s // tq, s // tk, max(tk // tq, 1), tk // HP
    scale = MQK ** -0.5
    c2 = scale * LOG2E
    heads = range(ATT_HEADS)

    def body(q_ref, k_ref, v_ref, do_ref, o_ref, lse_ref, t_ref, dq_ref, dk_ref, dv_ref, parts_ref, dq_acc, delta, lse_b2,
             send_sems, recv_sems, local_sem):
        exchange = _all_to_all_copies(t_ref, parts_ref, send_sems, recv_sems, local_sem)
        first, last = _grid_ends((bsz, MH // ATT_HEADS))
        pl.when(first)(exchange.start)
        dq_acc[...] = jnp.zeros_like(dq_acc)

        def d_loop(i, carry):
            rows = pl.ds(pl.multiple_of(i * tq, tq), tq)
            for h in heads:
                hs = slice(h * HP, (h + 1) * HP)
                dl = jnp.sum(do_ref[rows, hs].astype(F32) * o_ref[rows, hs].astype(F32), axis=1, keepdims=True)
                delta[rows, hs] = jnp.broadcast_to(dl, (tq, HP))
            lse_b2[rows, :] = lse_ref[rows, :] * LOG2E
            return carry

        lax.fori_loop(0, nq, d_loop, 0)

        def k_loop(kj, carry):
            k0 = pl.multiple_of(kj * tk, tk)
            kk = [k_ref[pl.ds(k0, tk), h * HP:(h + 1) * HP] for h in heads]
            vv = [v_ref[pl.ds(k0, tk), h * HP:(h + 1) * HP] for h in heads]

            def tile(qi, c, masked):
                q0 = pl.multiple_of(qi * tq, tq)
                rows = pl.ds(q0, tq)
                out = []
                for h in heads:
                    hs = slice(h * HP, (h + 1) * HP)
                    dk, dv = c[h]
                    q = q_ref[rows, hs]
                    do = do_ref[rows, hs]
                    e = _dot(q, kk[h], NT) * c2 - _lanes(lse_b2[rows, hs], groups)
                    if masked:
                        e = jnp.where(_att_mask(q0, k0, tq, tk), e, NEG)
                    p = jnp.exp2(e)
                    dv = dv + _dot(p.astype(BF16), do, TN)
                    ds = (p * (_dot(do, vv[h], NT) - _lanes(delta[rows, hs], groups))).astype(BF16)
                    dq_acc[rows, hs] += _dot(ds, kk[h], NN)
                    dk = dk + _dot(ds, q, TN)
                    out.append((dk, dv))
                return tuple(out)

            zero = jnp.zeros((tk, HP), F32)
            c = tuple((zero, zero) for _ in heads)
            first = k0 // tq
            for u in range(per):
                c = tile(first + u, c, True)
            c = lax.fori_loop(first + per, nq, lambda qi, c: tile(qi, c, False), c)
            for h in heads:
                dk_ref[pl.ds(k0, tk), h * HP:(h + 1) * HP] = (c[h][0] * scale).astype(BF16)
                dv_ref[pl.ds(k0, tk), h * HP:(h + 1) * HP] = c[h][1].astype(BF16)
            return carry

        lax.fori_loop(0, nk, k_loop, 0)
        dq_ref[...] = (dq_acc[...] * scale).astype(BF16)
        pl.when(last)(exchange.finish)

    spec = pl.BlockSpec((s, ATT_HEADS * HP), lambda b, h: (b, h))
    return pl.pallas_call(
        body, name="attn_bwd", grid=(bsz, MH // ATT_HEADS), in_specs=[spec] * 6 + [HBM_SPEC],
        out_specs=[spec] * 3 + [HBM_SPEC],
        out_shape=[jax.ShapeDtypeStruct((t, MH * HP), BF16)] * 3 + [jax.ShapeDtypeStruct(tsum.shape, tsum.dtype)],
        scratch_shapes=[pltpu.VMEM((s, ATT_HEADS * HP), F32)] * 3 + EXCHANGE_SEMS,
        compiler_params=_cparams(("arbitrary", "arbitrary")),
    )(q2, k2, v2, do2, o2, lse2, tsum)


def _perm_w_in_t(w):
    z = lambda n: jnp.zeros((n, w.shape[1]), w.dtype)
    return jnp.concatenate([w[:3072], w[3504:5552], w[3088:3344], w[3344:3472], w[3072:3088], z(48), w[3472:3504], z(32)],
                           axis=0)


def _unperm_w_in_t(g):
    return jnp.concatenate([g[:3072], g[5504:5520], g[5120:5376], g[5376:5504], g[5568:5600], g[3072:5120]], axis=0)


def _pad_wa(w):
    return jnp.pad(w, ((0, LANES - GLR), (0, 0)))


def _pad_wuq(w):
    return jnp.pad(w.reshape(MQR, MH, MQK), ((0, 0), (0, 0), (0, HP - MQK))).reshape(MQR, MH * HP)


def _unpad_wuq(g):
    return g.reshape(MQR, MH, HP)[:, :, :MQK].reshape(MQR, MH * MQK)


def _pad_wukv(w):
    w3 = w.reshape(MKVR, MH, NOPE + MV)
    kp = jnp.pad(w3[:, :, :NOPE], ((0, 0), (0, 0), (0, HP - NOPE))).reshape(MKVR, MH * HP)
    vp = jnp.pad(w3[:, :, NOPE:], ((0, 0), (0, 0), (0, HP - MV))).reshape(MKVR, MH * HP)
    return jnp.concatenate([kp, vp], axis=1)


def _unpad_wukv(g):
    kp = g[:, :MH * HP].reshape(MKVR, MH, HP)[:, :, :NOPE]
    vp = g[:, MH * HP:].reshape(MKVR, MH, HP)[:, :, :MV]
    return jnp.concatenate([kp, vp], axis=2).reshape(MKVR, MH * (NOPE + MV))


def _pad_wo(w):
    return jnp.pad(w.reshape(MH, MV, D), ((0, 0), (0, HP - MV), (0, 0))).reshape(MH * HP, D)


def _unpad_wo(g):
    return g.reshape(MH, HP, D)[:, :MV, :].reshape(MH * MV, D)


def _pad_lanes(v, n=HP):
    return jnp.pad(v, ((0, 0), (0, n - v.shape[1])))


def _local_step(x, positions, tgt, wt, slab_b, sp, mod3):
    bsz, s, _ = x.shape
    t = bsz * s
    x2 = x.reshape(t, D)
    tgt2 = tgt.reshape(t, D)
    pos = positions.reshape(t, 1)
    fr16 = ROPE_THETA ** (-jnp.arange(0, ROPE, 2, dtype=F32) / ROPE)
    zero = lambda n: jnp.zeros((n,), F32)
    fr = jnp.concatenate([zero(NOPE), fr16, fr16, zero(HP - MQK)]).reshape(1, HP)
    sg = jnp.concatenate([zero(NOPE), -jnp.ones((ROPE // 2,), F32), jnp.ones((ROPE // 2,), F32), zero(HP - MQK)]).reshape(1, HP)

    w_in_t = _perm_w_in_t(wt["w_in"])
    wa_pad = _pad_wa(wt["gla_w_alpha"].T)
    wuq = _pad_wuq(wt["mla_w_uq"].T)
    wukv = _pad_wukv(wt["mla_w_ukv"].T)
    qn_g, kn_g = _pad_lanes(sp["mla_qn_g"]), _pad_lanes(sp["mla_kn_g"])

    _, h = _norm_mod_fwd(x2, sp["norm1_g"], mod3, 0, 1, "norm1_fwd")
    proj = _mm(h, w_in_t, "nt", (BF16,), "proj_fwd", tn=PW // 4)
    o_gla, og, states = _gla_fwd(proj, wa_pad, sp["gla_b_alpha"], sp["gla_out_norm_g"], bsz)
    q2, k2, v2 = _mla_pre_fwd(proj, pos, fr, sg, sp["mla_q_lat_g"], sp["mla_kv_lat_g"], qn_g, kn_g, wuq, wukv)
    o2, lse2, core_row = _attn_fwd(q2, k2, v2, bsz, slab_b)
    wt = dict(wt, **_unpack_gathered(_cross_core_fill(core_row), SLAB_B))
    wo_pad = _pad_wo(wt["mla_w_o"])
    y_a = _mm(og, wt["gla_w_o"], "nn", (BF16,), "gla_out_fwd")
    y_b = _mm(o2, wo_pad, "nn", (BF16,), "mla_out_fwd")
    mix = _merge_fwd(proj, sp["b_merge"], y_a, y_b)
    mixed = _mm(mix, wt["w_out"], "nn", (F32,), "w_out_fwd")

    x1, h2 = _norm_mod_fwd(x2, sp["norm2_g"], mod3, 3, 4, "norm2_fwd", mixed=mixed, i_gate=2)
    a, f = _mm(h2, wt["mlp_w1"], "nt", (BF16, BF16), "mlp1_fwd",
               epi=lambda acc: (acc, jnp.square(jnp.maximum(acc, 0.0))))
    ff = _mm(f, wt["mlp_w2"], "nn", (F32,), "mlp2_fwd")
    dy, dff, acc_g2, acc_loss = _loss_head(x1, ff, tgt2, mod3)

    gw = {}
    gw["mlp_w2"] = _mm(f, dff, "tn", (BF16,), "mlp2_dw")
    da = _mm(dff, wt["mlp_w2"], "nt", (BF16,), "mlp2_dx", extras=(a,),
             epi=lambda acc, av: (acc * (2.0 * jnp.maximum(av.astype(F32), 0.0)),))
    gw["mlp_w1"] = _mm(da, h2, "tn", (BF16,), "mlp1_dw")
    dh2 = _mm(da, wt["mlp_w1"], "nn", (F32,), "mlp1_dx")
    dx1, dmixed, accb2, accg2 = _norm_mod_bwd(x1, dh2, dy, sp["norm2_g"], mod3, 3, 4, "norm2_bwd", mixed=mixed, i_gate=2)

    gw["w_out"] = _mm(mix, dmixed, "tn", (BF16,), "w_out_dw")
    dmix = _mm(dmixed, wt["w_out"], "nt", (BF16,), "w_out_dx")
    dy_a, dy_b, dlogits, acc_bm = _merge_bwd(dmix, proj, sp["b_merge"], y_a, y_b)
    gw["gla_w_o"] = _mm(og, dy_a, "tn", (BF16,), "gla_out_dw")
    dog = _mm(dy_a, wt["gla_w_o"], "nt", (BF16,), "gla_out_dx")
    gw["mla_w_o"] = _unpad_wo(_mm(o2, dy_b, "tn", (BF16,), "mla_out_dw"))
    do2 = _mm(dy_b, wo_pad, "nt", (BF16,), "mla_out_dx")
    dq2, dk2, dv2, parts_b = _attn_bwd(q2, k2, v2, do2, o2, lse2, bsz, _pack_per_device(gw, SLAB_B))
    dq_g, dk_g, dv_g, dgg, dz, acc_ba, acc_go = _gla_bwd(dog, o_gla, proj, wa_pad, sp["gla_b_alpha"],
                                                         sp["gla_out_norm_g"], states, bsz)
    gw["gla_w_alpha"] = _mm(proj, dz, "tn", (F32,), "gla_alpha_dw", a_off=43, m=LANES)[:GLR].T.astype(BF16)
    dmisc_gla = _mm(dz, wa_pad, "nt", (F32,), "gla_alpha_dx")
    dcq, dckv, dmisc, gwuq, gwukv, acc_mla = _mla_pre_bwd(dq2, dk2, dv2, dmisc_gla, proj, pos, fr, sg, sp["mla_q_lat_g"],
                                                         sp["mla_kv_lat_g"], qn_g, kn_g, wuq, wukv)
    gw["mla_w_uq"] = _unpad_wuq(gwuq).T.astype(BF16)
    gw["mla_w_ukv"] = _unpad_wukv(gwukv).T.astype(BF16)
    dproj = jnp.concatenate([dq_g, dk_g, dv_g, dgg, dlogits, dcq, dckv, dmisc], axis=1)
    gw["w_in"] = _unperm_w_in_t(_mm(dproj, h, "tn", (BF16,), "proj_dw", tm=PW // 4))
    dh, parts_a = _mm(dproj, w_in_t, "nn", (F32,), "proj_dx", tk=PW // 4,
                      cargo=_sum_over_cores(_pack_per_device(gw, SLAB_A), "a"))
    grad_x, accb1, accg1 = _norm_mod_bwd(x2, dh, dx1, sp["norm1_g"], mod3, 0, 1, "norm1_bwd")

    dmod = jnp.stack([accb1[:, 0], accb1[:, 1], accb2[:, 2], accb2[:, 0], accb2[:, 1], acc_g2[:, 0]], axis=1)

    rows = {
        "dmod": dmod.reshape(bsz * 6, D),
        "norm1_g": accg1[0:1], "norm2_g": accg2[0:1],
        "b_merge": acc_bm[0:1].reshape(2, D),
        "gla_b_alpha": _pad_lanes(acc_ba[0:1], D),
        "gla_out_norm_g": _pad_lanes(acc_go.reshape(GH, 8, GDV)[:, 0, :], D),
        "mla_q_lat_g": _pad_lanes(acc_mla[0:1], D), "mla_kv_lat_g": _pad_lanes(acc_mla[1:2], D),
        "mla_qn_g": _pad_lanes(acc_mla[2:3], D), "mla_kn_g": _pad_lanes(acc_mla[3:4], D),
        "loss": acc_loss[0:1],
    }
    return grad_x.reshape(bsz, s, D), parts_a, parts_b, rows


HBM_SPEC = pl.BlockSpec(memory_space=pltpu.HBM)


def _all_gather(p, name):
    r, cdim = p.shape

    def body(p_ref, out_ref, send_sems, recv_sems, local_sem):
        x, y, c = lax.axis_index("x"), lax.axis_index("y"), lax.axis_index("c")
        me, sibling = (x, y, c), (x, y, 1 - c)
        chips = [(1 - x, y), (x, 1 - y), (1 - x, 1 - y)]

        def slot(px, py, pc):
            return out_ref.at[4 * px + 2 * py + pc]

        def copy(k, block, to, src=None):
            return pltpu.make_async_remote_copy(
                src_ref=slot(*block) if src is None else src, dst_ref=slot(*block),
                send_sem=send_sems.at[k], recv_sem=recv_sems.at[k], device_id=to, device_id_type=MESH)

        mine = pltpu.make_async_copy(p_ref, slot(*me), local_sem)
        mine.start()
        first = [copy(0, me, sibling, src=p_ref)] + [copy(1 + j, me, (*chip, c), src=p_ref) for j, chip in enumerate(chips)]
        for cp in first:
            cp.start()
        passed = [copy(4 + j, (*chip, c), sibling) for j, chip in enumerate(chips)]
        for j, chip in enumerate(chips):
            copy(1 + j, (*chip, c), me).wait_recv()
            passed[j].start()
        copy(0, sibling, me).wait_recv()
        for j, chip in enumerate(chips):
            copy(4 + j, (*chip, 1 - c), me).wait_recv()
        for cp in first + passed:
            cp.wait_send()
        mine.wait()

    return pl.pallas_call(
        body, name=name, out_shape=jax.ShapeDtypeStruct((N_DEV, r, cdim), p.dtype),
        in_specs=[HBM_SPEC], out_specs=HBM_SPEC,
        scratch_shapes=[pltpu.SemaphoreType.DMA((7,)), pltpu.SemaphoreType.DMA((7,)), pltpu.SemaphoreType.DMA(())],
    )(p)


def _sibling_exchange(g, name):
    def body(g_ref, out_ref, send_sem, recv_sem):
        x, y, c = lax.axis_index("x"), lax.axis_index("y"), lax.axis_index("c")
        cp = pltpu.make_async_remote_copy(src_ref=g_ref, dst_ref=out_ref, send_sem=send_sem, recv_sem=recv_sem,
                                          device_id=(x, y, 1 - c), device_id_type=MESH)
        cp.start()
        cp.wait()

    return pl.pallas_call(
        body, name=name, out_shape=jax.ShapeDtypeStruct(g.shape, g.dtype),
        in_specs=[HBM_SPEC], out_specs=HBM_SPEC,
        scratch_shapes=[pltpu.SemaphoreType.DMA(()), pltpu.SemaphoreType.DMA(())],
    )(g)


class _Exchange:
    def __init__(self, local, sends, arrivals):
        self.local, self.sends, self.arrivals = local, sends, arrivals

    def start(self):
        self.local.start()
        for cp in self.sends:
            cp.start()

    def finish(self):
        for cp in self.arrivals:
            cp.wait_recv()
        for cp in self.sends:
            cp.wait_send()
        self.local.wait()


EXCHANGE_SEMS = [pltpu.SemaphoreType.DMA((N_DEV,)), pltpu.SemaphoreType.DMA((N_DEV,)), pltpu.SemaphoreType.DMA(())]


def _all_to_all_copies(t_ref, out_ref, send_sems, recv_sems, local_sem):
    x, y, c = lax.axis_index("x"), lax.axis_index("y"), lax.axis_index("c")
    me = 4 * x + 2 * y + c

    def copy(k, src, dst):
        px, py, pc = (1 - x if k & 4 else x), (1 - y if k & 2 else y), (1 - c if k & 1 else c)
        peer = 4 * px + 2 * py + pc
        return pltpu.make_async_remote_copy(src_ref=t_ref.at[peer if src is None else src],
                                            dst_ref=out_ref.at[peer if dst is None else dst], send_sem=send_sems.at[k],
                                            recv_sem=recv_sems.at[k], device_id=(px, py, pc), device_id_type=MESH)

    return _Exchange(pltpu.make_async_copy(t_ref.at[me], out_ref.at[me], local_sem),
                     [copy(k, None, me) for k in range(1, N_DEV)], [copy(k, me, None) for k in range(1, N_DEV)])


def _chip_exchange_copies(t_ref, out_ref, send_sems, recv_sems, local_sem):
    x, y, c = lax.axis_index("x"), lax.axis_index("y"), lax.axis_index("c")
    my_chip = 2 * x + y
    chips = [(1 - x, y), (x, 1 - y), (1 - x, 1 - y)]

    def copy(j, src, dst, px, py):
        return pltpu.make_async_remote_copy(src_ref=t_ref.at[src], dst_ref=out_ref.at[dst], send_sem=send_sems.at[j],
                                            recv_sem=recv_sems.at[j], device_id=(px, py, c), device_id_type=MESH)

    return _Exchange(pltpu.make_async_copy(t_ref.at[my_chip], out_ref.at[my_chip], local_sem),
                     [copy(j, 2 * px + py, my_chip, px, py) for j, (px, py) in enumerate(chips)],
                     [copy(j, my_chip, 2 * px + py, px, py) for j, (px, py) in enumerate(chips)])


def _core_row_gather_copies(p_ref, out_ref, send_sems, recv_sems, local_sem):
    x, y, c = lax.axis_index("x"), lax.axis_index("y"), lax.axis_index("c")
    peers = [(x, y, 1 - c), (1 - x, y, c), (x, 1 - y, c), (1 - x, 1 - y, c)]

    def slot(px, py, pc):
        return out_ref.at[4 * px + 2 * py + pc]

    def copy(j, block, to):
        return pltpu.make_async_remote_copy(src_ref=p_ref, dst_ref=slot(*block), send_sem=send_sems.at[j],
                                            recv_sem=recv_sems.at[j], device_id=to, device_id_type=MESH)

    return _Exchange(pltpu.make_async_copy(p_ref, slot(x, y, c), local_sem),
                     [copy(j, (x, y, c), peer) for j, peer in enumerate(peers)],
                     [copy(j, peer, peer) for j, peer in enumerate(peers)])


def _cross_core_fill(gathered):
    def body(g_ref, out_ref, send_sems, recv_sems):
        x, y, c = lax.axis_index("x"), lax.axis_index("y"), lax.axis_index("c")
        chips = [(1 - x, y), (x, 1 - y), (1 - x, 1 - y)]

        def copy(j, pc):
            px, py = chips[j]
            slot = 4 * px + 2 * py + pc
            return pltpu.make_async_remote_copy(src_ref=g_ref.at[slot], dst_ref=out_ref.at[slot], send_sem=send_sems.at[j],
                                                recv_sem=recv_sems.at[j], device_id=(x, y, 1 - c), device_id_type=MESH)

        sends = [copy(j, c) for j in range(3)]
        for cp in sends:
            cp.start()
        for j in range(3):
            copy(j, 1 - c).wait_recv()
        for cp in sends:
            cp.wait_send()

    return pl.pallas_call(
        body, name="weights_cross_core_fill", out_shape=jax.ShapeDtypeStruct(gathered.shape, gathered.dtype),
        in_specs=[HBM_SPEC], out_specs=HBM_SPEC, input_output_aliases={0: 0},
        scratch_shapes=[pltpu.SemaphoreType.DMA((3,)), pltpu.SemaphoreType.DMA((3,))],
    )(gathered)


def _slab_block(r):
    return max(b for b in range(16, SLAB_BLOCK_MAX + 1, 16) if r % b == 0)


def _pair_sum(a, b, name):
    n, r, cdim = a.shape
    rb = _slab_block(r)
    blk = pl.BlockSpec((1, rb, cdim), lambda j, i: (j, i, 0))

    def body(a_ref, b_ref, o_ref):
        o_ref[...] = (a_ref[...].astype(F32) + b_ref[...].astype(F32)).astype(BF16)

    return pl.pallas_call(
        body, name=name, grid=(n, r // rb), in_specs=[blk, blk], out_specs=blk,
        out_shape=jax.ShapeDtypeStruct(a.shape, BF16), compiler_params=_cparams(("arbitrary", "arbitrary")),
    )(a, b)


def _adamw_math(w, g, m, v):
    m = ADAM_B1 * m + (1.0 - ADAM_B1) * g
    v = ADAM_B2 * v + (1.0 - ADAM_B2) * jnp.square(g)
    m_hat = m / (1.0 - ADAM_B1 ** ADAM_STEP)
    v_hat = v / (1.0 - ADAM_B2 ** ADAM_STEP)
    delta = -ADAM_LR * (m_hat / (jnp.sqrt(v_hat) + ADAM_EPS) + ADAM_WD * w)
    return delta, m, v


def _slab_sum(parts, name):
    n, r, cdim = parts.shape
    rb = _slab_block(r)
    blk = pl.BlockSpec((rb, cdim), lambda i: (i, 0))

    def body(p_ref, g_out):
        g = p_ref[0].astype(F32)
        for j in range(1, n):
            g = g + p_ref[j].astype(F32)
        g_out[...] = g

    return pl.pallas_call(
        body, name=name, grid=(r // rb,),
        in_specs=[pl.BlockSpec((n, rb, cdim), lambda i: (0, i, 0))], out_specs=blk,
        out_shape=jax.ShapeDtypeStruct((r, cdim), F32), compiler_params=_cparams(("arbitrary",)),
    )(parts)


def _adamw(g, w, m, v, name):
    r, cdim = w.shape
    rb = _tile(r, 256)
    blk = pl.BlockSpec((rb, cdim), lambda i: (i, 0))

    def body(g_ref, w_ref, m_ref, v_ref, d_out, m_out, v_out):
        d_out[...], m_out[...], v_out[...] = _adamw_math(w_ref[...], g_ref[...], m_ref[...], v_ref[...])

    return pl.pallas_call(
        body, name=name, grid=(r // rb,), in_specs=[blk] * 4, out_specs=[blk] * 3,
        out_shape=[jax.ShapeDtypeStruct((r, cdim), F32)] * 3, compiler_params=_cparams(("arbitrary",)),
    )(g, w, m, v)


def _adamw_small(parts, w, m, v):
    def body(p_ref, w_ref, m_ref, v_ref, g_out, d_out, m_out, v_out, loss_out):
        def total(srcs):
            acc = None
            for r in srcs:
                for j in range(N_DEV):
                    term = p_ref[j, r:r + 1, :]
                    acc = term if acc is None else acc + term
            return acc

        for prow, srcs in enumerate(SMALL_SOURCES):
            one = slice(prow, prow + 1)
            g = total(srcs)
            g_out[one, :] = g
            d_out[one, :], m_out[one, :], v_out[one, :] = _adamw_math(w_ref[one, :], g, m_ref[one, :], v_ref[one, :])
        loss_out[...] = jnp.broadcast_to(jnp.sum(total(LOSS_SOURCE), axis=1, keepdims=True), (8, LANES))

    full = lambda shp: pl.BlockSpec(shp, lambda i: (0,) * len(shp))
    return pl.pallas_call(
        body, name="adamw_small", grid=(1,),
        in_specs=[full((N_DEV, SMALL_ROWS, D)), full((16, D)), full((16, D)), full((16, D))],
        out_specs=[full((16, D))] * 4 + [full((8, LANES))],
        out_shape=[jax.ShapeDtypeStruct((16, D), F32)] * 4 + [jax.ShapeDtypeStruct((8, LANES), F32)],
        compiler_params=_cparams(("arbitrary",)),
    )(parts, w, m, v)


def _to_slab(shard, form):
    if form == "N":
        return shard
    return shard.T if form == "T" else shard.T.reshape(-1, D)


def _from_slab(block, form, shard_shape):
    if form == "N":
        return block
    return block.T if form == "T" else block.reshape(shard_shape[1], shard_shape[0]).T


def _gathered_full(g, form, shard_shape):
    if form == "TR":
        return g.reshape(N_DEV * shard_shape[1], shard_shape[0])
    return g.reshape(N_DEV * g.shape[1], D)


def _pack_slab(shards, layout):
    return jnp.concatenate([jnp.zeros((r, D), BF16) if n is None else _to_slab(shards[n], form).astype(BF16)
                            for n, r, form in layout], axis=0)


def _unpack_gathered(gathered, layout):
    out, off = {}, 0
    for n, r, form in layout:
        if n is not None:
            out[n] = _gathered_full(gathered[:, off:off + r], form, SHARD_SHAPES[n])
        off += r
    return out


def _pack_per_device(gw, layout):
    return jnp.concatenate([jnp.zeros((N_DEV, r, D), BF16) if n is None else gw[n].reshape(N_DEV, r, D)
                            for n, r, _ in layout], axis=1)


def _unpack_shards(gslab, layout):
    out, off = {}, 0
    for n, r, form in layout:
        if n is not None:
            out[n] = _from_slab(gslab[off:off + r], form, SHARD_SHAPES[n])
        off += r
    return out


def _sum_over_cores(per_dev, tag):
    my_c = lax.axis_index("c")
    pairs = per_dev.reshape(4, 2, per_dev.shape[1], D)
    keep = lax.dynamic_index_in_dim(pairs, my_c, axis=1, keepdims=False)
    give = lax.dynamic_index_in_dim(pairs, 1 - my_c, axis=1, keepdims=False)
    return _pair_sum(keep, _sibling_exchange(give, "rs_sibling_exchange_" + tag), "rs_pair_sum_" + tag)


def _small_pack(vals):
    rows = []
    for n in SMALL:
        v = vals[n].reshape(-1)
        k = -(-v.shape[0] // D)
        rows.append(jnp.pad(v, (0, k * D - v.shape[0])).reshape(k, D))
    return jnp.concatenate(rows, axis=0)


def _small_unpack(packed, shapes):
    out = {}
    for n in SMALL:
        k = shapes[n][-1]
        r0 = SMALL_ROW[n]
        out[n] = packed[r0:r0 + -(-k // D)].reshape(-1)[:k].reshape(shapes[n])
    return out


def kernel(x, c, positions, w_ada, b_ada, norm1_g, w_in, b_merge, gla_w_alpha, gla_b_alpha, gla_out_norm_g, gla_w_o, mla_q_lat_g, mla_w_uq, mla_kv_lat_g, mla_w_ukv, mla_qn_g, mla_kn_g, mla_w_o, w_out, norm2_g, mlp_w1, mlp_w2, loss_target, m_w_ada, m_b_ada, m_norm1_g, m_w_in, m_b_merge, m_gla_w_alpha, m_gla_b_alpha, m_gla_out_norm_g, m_gla_w_o, m_mla_q_lat_g, m_mla_w_uq, m_mla_kv_lat_g, m_mla_w_ukv, m_mla_qn_g, m_mla_kn_g, m_mla_w_o, m_w_out, m_norm2_g, m_mlp_w1, m_mlp_w2, v_w_ada, v_b_ada, v_norm1_g, v_w_in, v_b_merge, v_gla_w_alpha, v_gla_b_alpha, v_gla_out_norm_g, v_gla_w_o, v_mla_q_lat_g, v_mla_w_uq, v_mla_kv_lat_g, v_mla_w_ukv, v_mla_qn_g, v_mla_kn_g, v_mla_w_o, v_w_out, v_norm2_g, v_mlp_w1, v_mlp_w2):
    args = dict(locals())
    wts = {n: args[n][0] for n in WEIGHTS}
    mom = {n: args["m_" + n][0] for n in WEIGHTS}
    var = {n: args["v_" + n][0] for n in WEIGHTS}
    my_c = lax.axis_index("c")
    my_dev = 4 * lax.axis_index("x") + 2 * lax.axis_index("y") + my_c
    bsz = x.shape[0]
    sp = {n: wts[n].reshape(1, -1) for n in SMALL}

    wt = _unpack_gathered(_all_gather(_pack_slab(wts, SLAB_A), "weights_all_gather"), SLAB_A)

    c_all = _all_gather(jnp.pad(c, ((0, 8 - bsz), (0, 0))), "cond_all_gather")[:, :bsz].reshape(N_DEV * bsz, D)
    bias = lax.dynamic_slice_in_dim(sp["b_ada"], my_dev * ADA_COLS, ADA_COLS, axis=1)
    mod_cols = _mm(c_all, wts["w_ada"], "nn", (F32,), "ada_fwd", pro=_silu, epi=lambda acc, b: (acc + b,),
                   extras=(jnp.broadcast_to(bias, (N_DEV * bsz, ADA_COLS)),))
    mod_all = _all_gather(mod_cols, "mod_all_gather")
    mod_mine = lax.dynamic_slice_in_dim(mod_all, my_dev * bsz, bsz, axis=1)
    mod3 = jnp.transpose(mod_mine, (1, 0, 2)).reshape(bsz, 6, D)

    grad_x, parts_a, parts_b, rows = _local_step(x, positions, loss_target, wt, _pack_slab(wts, SLAB_B), sp, mod3)

    grads = dict(_unpack_shards(_slab_sum(parts_a, "rs_slab_sum_a"), SLAB_A),
                 **_unpack_shards(_slab_sum(parts_b, "rs_slab_sum_b"), SLAB_B))
    big = {n: (g,) + tuple(_adamw(g, wts[n], mom[n], var[n], "adamw_" + n)) for n, g in grads.items()}

    order = ["dmod", "norm1_g", "norm2_g", "b_merge", "gla_b_alpha", "gla_out_norm_g", "mla_q_lat_g", "mla_kv_lat_g",
             "mla_qn_g", "mla_kn_g", "loss"]
    part_rows = jnp.concatenate([rows[n] for n in order], axis=0)
    part_rows = jnp.pad(part_rows, ((0, SMALL_ROWS - part_rows.shape[0]), (0, 0)))
    all_rows = _all_gather(part_rows, "partials_all_gather")

    dmod_all = all_rows[:, :6 * bsz].reshape(N_DEV * bsz, 6 * D)
    dmod_cols = lax.dynamic_slice_in_dim(dmod_all, my_dev * ADA_COLS, ADA_COLS, axis=1)
    g_ada = _mm(c_all, dmod_cols, "tn", (F32,), "ada_dw", pro=_silu)
    big["w_ada"] = (g_ada,) + tuple(_adamw(g_ada, wts["w_ada"], mom["w_ada"], var["w_ada"], "adamw_w_ada"))

    small = _adamw_small(all_rows, _small_pack({n: wts[n] for n in SMALL}), _small_pack({n: mom[n] for n in SMALL}),
                         _small_pack({n: var[n] for n in SMALL}))
    loss = small[4][0, 0]
    small_shapes = {n: wts[n].shape for n in SMALL}
    small = [_small_unpack(o, small_shapes) for o in small[:4]]

    outs = [loss, grad_x]
    for k in range(4):
        for n in WEIGHTS:
            val = big[n][k] if n in BIG else small[k][n]
            outs.append(val.reshape((1,) + tuple(wts[n].shape)))
    return tuple(outs)
```

```python
import functools

import jax
import jax.numpy as jnp
from jax import lax
from jax.experimental import pallas as pl
from jax.experimental.pallas import tpu as pltpu

F32 = jnp.float32
BF16 = jnp.bfloat16
MESH = pl.DeviceIdType.MESH

D = 1024
EPS = 1e-6
CHUNK = 64
GH, GDK, GDV, GLR, GTAU = 4, 128, 256, 16, 16.0
MH, MQR, MKVR, NOPE, ROPE, MV = 16, 256, 128, 64, 32, 64
MQK = NOPE + ROPE
HP = 128
FF = 4 * D
ROPE_THETA = 10000.0
IN_WIDTH = 5552
PW = 5632
N_DEV = 8
LANES = 128
SLAB_BLOCK_MAX = 400
ADA_COLS = 6 * D // N_DEV
SMALL_ROWS = 32
SMALL_SOURCES = tuple([(r, 6 + r) for r in range(6)] + [(12,), (13,), (14,), (15,), (16,), (17, 18, 19, 20),
                                                         (21,), (22,), (23,), (24,)])
LOSS_SOURCE = (25,)
VMEM_LIMIT = 56 * 1024 * 1024

ADAM_LR, ADAM_B1, ADAM_B2, ADAM_EPS, ADAM_WD, ADAM_STEP = 0.001, 0.9, 0.999, 1e-08, 0.01, 10

SLAB_A = (("w_in", 694, "T"), ("gla_w_alpha", 1, "TR"), (None, 9, None), ("mla_w_uq", 48, "TR"), ("mla_w_ukv", 32, "TR"))
SLAB_B = (("mlp_w1", 512, "T"), ("gla_w_o", 128, "N"), ("mla_w_o", 128, "N"), ("w_out", 128, "N"), ("mlp_w2", 512, "N"))
BIG = ("w_ada",) + tuple(n for n, _, _ in SLAB_A + SLAB_B if n is not None)
SHARD_SHAPES = {"w_ada": (D, 6 * D // N_DEV), "w_in": (D, IN_WIDTH // N_DEV), "gla_w_alpha": (GLR, GH * GDK // N_DEV),
                "gla_w_o": (GH * GDV // N_DEV, D), "mla_w_uq": (MQR, MH * MQK // N_DEV),
                "mla_w_ukv": (MKVR, MH * (NOPE + MV) // N_DEV), "mla_w_o": (MH * MV // N_DEV, D), "w_out": (D // N_DEV, D),
                "mlp_w1": (D, FF // N_DEV), "mlp_w2": (FF // N_DEV, D)}
SMALL = ("b_ada", "norm1_g", "norm2_g", "b_merge", "gla_b_alpha", "gla_out_norm_g", "mla_q_lat_g", "mla_kv_lat_g",
         "mla_qn_g", "mla_kn_g")
SMALL_ROW = {"b_ada": 0, "norm1_g": 6, "norm2_g": 7, "b_merge": 8, "gla_b_alpha": 10, "gla_out_norm_g": 11,
             "mla_q_lat_g": 12, "mla_kv_lat_g": 13, "mla_qn_g": 14, "mla_kn_g": 15}
WEIGHTS = ("w_ada", "b_ada", "norm1_g", "w_in", "b_merge", "gla_w_alpha", "gla_b_alpha", "gla_out_norm_g", "gla_w_o",
           "mla_q_lat_g", "mla_w_uq", "mla_kv_lat_g", "mla_w_ukv", "mla_qn_g", "mla_kn_g", "mla_w_o", "w_out",
           "norm2_g", "mlp_w1", "mlp_w2")


def _cparams(sem=None):
    return pltpu.CompilerParams(dimension_semantics=sem, vmem_limit_bytes=VMEM_LIMIT)


def _tile(n, pref):
    for t in (2048, PW // 4, 1024, 512, 256, 128):
        if t <= pref and n % t == 0:
            return t
    return n


def _dot(a, b, dims, precision=None):
    return lax.dot_general(a, b, (dims, ((), ())), preferred_element_type=F32, precision=precision)


NN = ((1,), (0,))
NT = ((1,), (1,))
TN = ((0,), (0,))


def _sigmoid(x):
    return 1.0 / (1.0 + jnp.exp(-x))


def _silu(x):
    return x * _sigmoid(x)


def _mm(a, b, mode, out_dtypes, name, *, pro=None, pro_b=None, epi=None, extras=(), a_off=0, m=None, tm=2048, tn=1024,
        tk=1024, cargo=None):
    if mode == "tn":
        kc, n = b.shape
        m = a.shape[1] if m is None else m
    elif mode == "nn":
        m, kc = a.shape
        n = b.shape[1]
    else:
        m, kc = a.shape
        n = b.shape[0]
    tm, tn, tk = _tile(m, tm), _tile(n, tn), _tile(kc, tk)
    nk = kc // tk
    dims = {"nn": NN, "nt": NT, "tn": TN}[mode]
    if mode == "tn":
        a_spec = pl.BlockSpec((tk, tm), lambda i, j, k: (k, i + a_off))
    else:
        a_spec = pl.BlockSpec((tm, tk), lambda i, j, k: (i + a_off, k))
    if mode == "nt":
        b_spec = pl.BlockSpec((tn, tk), lambda i, j, k: (j, k))
    else:
        b_spec = pl.BlockSpec((tk, tn), lambda i, j, k: (k, j))
    o_spec = pl.BlockSpec((tm, tn), lambda i, j, k: (i, j))
    n_ex, n_out = len(extras), len(out_dtypes)
    grid = (m // tm, n // tn, nk)
    has_cargo = cargo is not None

    def body(a_ref, b_ref, *rest):
        ex, rest = rest[:n_ex], rest[n_ex:]
        if has_cargo:
            outs, acc = rest[1:1 + n_out], rest[2 + n_out]
            exchange = _chip_exchange_copies(rest[0], rest[1 + n_out], *rest[3 + n_out:])
            steps = [pl.program_id(axis) for axis in range(3)]
            first = (steps[0] == 0) & (steps[1] == 0) & (steps[2] == 0)
            last = (steps[0] == grid[0] - 1) & (steps[1] == grid[1] - 1) & (steps[2] == grid[2] - 1)
            pl.when(first)(exchange.start)
        else:
            outs, acc = rest[:n_out], rest[n_out]
        k = pl.program_id(2)

        @pl.when(k == 0)
        def _():
            acc[...] = jnp.zeros_like(acc)

        av = a_ref[...]
        if pro is not None:
            av = pro(av)
        bv = b_ref[...]
        if pro_b is not None:
            bv = pro_b(bv)
        acc[...] += _dot(av.astype(BF16), bv.astype(BF16), dims)

        @pl.when(k == nk - 1)
        def _():
            res = (acc[...],) if epi is None else epi(acc[...], *[e[...] for e in ex])
            for o_ref, r in zip(outs, res):
                o_ref[...] = r.astype(o_ref.dtype)

        if has_cargo:
            pl.when(last)(exchange.finish)

    cargo_in = [cargo] if has_cargo else []
    cargo_spec = [HBM_SPEC] * len(cargo_in)
    out = pl.pallas_call(
        body, name=name, grid=grid,
        in_specs=[a_spec, b_spec] + [o_spec] * n_ex + cargo_spec,
        out_specs=[o_spec] * n_out + cargo_spec,
        out_shape=[jax.ShapeDtypeStruct((m, n), dt) for dt in out_dtypes]
        + [jax.ShapeDtypeStruct(c.shape, c.dtype) for c in cargo_in],
        scratch_shapes=[pltpu.VMEM((tm, tn), F32)] + (EXCHANGE_SEMS if has_cargo else []),
        compiler_params=_cparams(("arbitrary",) * 3 if has_cargo else ("parallel", "parallel", "arbitrary")),
    )(a, b, *extras, *cargo_in)
    return out[0] if len(out) == 1 else out


def _rows(s):
    return _tile(s, 512)


def _mod_spec():
    return pl.BlockSpec((1, 6, D), lambda b, i: (b, 0, 0))


def _tok_spec(tr, nb, width=D, col=0):
    return pl.BlockSpec((tr, width), lambda b, i: (b * nb + i, col))


def _norm_mod_fwd(x, g, mod3, i_shift, i_scale, name, mixed=None, i_gate=None):
    bsz, _, _ = mod3.shape
    t = x.shape[0]
    s = t // bsz
    tr = _rows(s)
    nb = s // tr
    has_res = mixed is not None

    def body(*refs):
        if has_res:
            x_ref, mx_ref, g_ref, mod_ref, x1_ref, h_ref = refs
            xv = x_ref[...] + mod_ref[0, i_gate:i_gate + 1, :] * mx_ref[...]
            x1_ref[...] = xv
        else:
            x_ref, g_ref, mod_ref, h_ref = refs
            xv = x_ref[...]
        r = lax.rsqrt(jnp.mean(xv * xv, axis=1, keepdims=True) + EPS)
        hn = (xv * r) * g_ref[...]
        h = hn * (1.0 + mod_ref[0, i_scale:i_scale + 1, :]) + mod_ref[0, i_shift:i_shift + 1, :]
        h_ref[...] = h.astype(BF16)

    tok = _tok_spec(tr, nb)
    gspec = pl.BlockSpec((1, D), lambda b, i: (0, 0))
    ins = [x] + ([mixed] if has_res else []) + [g, mod3]
    in_specs = [tok] + ([tok] if has_res else []) + [gspec, _mod_spec()]
    out_shape = ([jax.ShapeDtypeStruct((t, D), F32)] if has_res else []) + [jax.ShapeDtypeStruct((t, D), BF16)]
    out = pl.pallas_call(
        body, name=name, grid=(bsz, nb), in_specs=in_specs, out_specs=[tok] * len(out_shape), out_shape=out_shape,
        compiler_params=_cparams(("arbitrary", "arbitrary")),
    )(*ins)
    return (out[0], out[1]) if has_res else (None, out[0])


def _norm_mod_bwd(x, dh, dres, g, mod3, i_shift, i_scale, name, mixed=None, i_gate=None):
    bsz = mod3.shape[0]
    t = x.shape[0]
    s = t // bsz
    tr = _rows(s)
    nb = s // tr
    has_res = mixed is not None

    def body(*refs):
        if has_res:
            x_ref, dh_ref, dres_ref, mx_ref, g_ref, mod_ref, dx_ref, dmx_ref, accb, accg = refs
        else:
            x_ref, dh_ref, dres_ref, g_ref, mod_ref, dx_ref, accb, accg = refs
        b, i = pl.program_id(0), pl.program_id(1)

        @pl.when(i == 0)
        def _():
            accb[...] = jnp.zeros_like(accb)

        @pl.when((i == 0) & (b == 0))
        def _():
            accg[...] = jnp.zeros_like(accg)

        xv, dhv, gv = x_ref[...], dh_ref[...], g_ref[...]
        r = lax.rsqrt(jnp.mean(xv * xv, axis=1, keepdims=True) + EPS)
        xn = xv * r
        accb[0, 0:1, :] += jnp.sum(dhv, axis=0, keepdims=True)
        accb[0, 1:2, :] += jnp.sum(dhv * (xn * gv), axis=0, keepdims=True)
        tt = dhv * (1.0 + mod_ref[0, i_scale:i_scale + 1, :])
        accg[0:1, :] += jnp.sum(tt * xn, axis=0, keepdims=True)
        dxn = tt * gv
        dx = dres_ref[...] + r * (dxn - xn * jnp.mean(dxn * xn, axis=1, keepdims=True))
        dx_ref[...] = dx
        if has_res:
            accb[0, 2:3, :] += jnp.sum(dx * mx_ref[...], axis=0, keepdims=True)
            dmx_ref[...] = (dx * mod_ref[0, i_gate:i_gate + 1, :]).astype(BF16)

    tok = _tok_spec(tr, nb)
    gspec = pl.BlockSpec((1, D), lambda b, i: (0, 0))
    ins = [x, dh, dres] + ([mixed] if has_res else []) + [g, mod3]
    in_specs = [tok] * (4 if has_res else 3) + [gspec, _mod_spec()]
    out_shape = [jax.ShapeDtypeStruct((t, D), F32)] + ([jax.ShapeDtypeStruct((t, D), BF16)] if has_res else [])
    out_specs = [tok] * len(out_shape)
    out_shape += [jax.ShapeDtypeStruct((bsz, 8, D), F32), jax.ShapeDtypeStruct((8, D), F32)]
    out_specs += [pl.BlockSpec((1, 8, D), lambda b, i: (b, 0, 0)), pl.BlockSpec((8, D), lambda b, i: (0, 0))]
    return pl.pallas_call(
        body, name=name, grid=(bsz, nb), in_specs=in_specs, out_specs=out_specs, out_shape=out_shape,
        compiler_params=_cparams(("arbitrary", "arbitrary")),
    )(*ins)


def _loss_head(x1, ff, tgt, mod3):
    bsz = mod3.shape[0]
    t = x1.shape[0]
    s = t // bsz
    tr = _rows(s)
    nb = s // tr

    def body(x1_ref, ff_ref, tg_ref, mod_ref, dy_ref, dff_ref, accb, accl):
        b, i = pl.program_id(0), pl.program_id(1)

        @pl.when(i == 0)
        def _():
            accb[...] = jnp.zeros_like(accb)

        @pl.when((i == 0) & (b == 0))
        def _():
            accl[...] = jnp.zeros_like(accl)

        gate = mod_ref[0, 5:6, :]
        ffv = ff_ref[...]
        err = x1_ref[...] + gate * ffv - tg_ref[...]
        accl[0:1, :] += jnp.sum(err * err, axis=0, keepdims=True) * (0.5 / D)
        dy = err * (1.0 / D)
        dy_ref[...] = dy
        dff_ref[...] = (dy * gate).astype(BF16)
        accb[0, 0:1, :] += jnp.sum(dy * ffv, axis=0, keepdims=True)

    tok = _tok_spec(tr, nb)
    return pl.pallas_call(
        body, name="loss_head", grid=(bsz, nb), in_specs=[tok, tok, tok, _mod_spec()],
        out_specs=[tok, tok, pl.BlockSpec((1, 8, D), lambda b, i: (b, 0, 0)), pl.BlockSpec((8, D), lambda b, i: (0, 0))],
        out_shape=[jax.ShapeDtypeStruct((t, D), F32), jax.ShapeDtypeStruct((t, D), BF16),
                   jax.ShapeDtypeStruct((bsz, 8, D), F32), jax.ShapeDtypeStruct((8, D), F32)],
        compiler_params=_cparams(("arbitrary", "arbitrary")),
    )(x1, ff, tgt, mod3)


def _merge_fwd(proj, b_merge, y_a, y_b):
    t = proj.shape[0]
    tr = _tile(t, 512)

    def body(la_ref, lb_ref, bm_ref, ya_ref, yb_ref, mix_ref):
        ga = _sigmoid(la_ref[...] + bm_ref[:, 0:D])
        gb = _sigmoid(lb_ref[...] + bm_ref[:, D:2 * D])
        mix_ref[...] = (ga * ya_ref[...].astype(F32) + gb * yb_ref[...].astype(F32)).astype(BF16)

    tok = pl.BlockSpec((tr, D), lambda i: (i, 0))
    return pl.pallas_call(
        body, name="merge_fwd", grid=(t // tr,),
        in_specs=[pl.BlockSpec((tr, D), lambda i: (i, 3)), pl.BlockSpec((tr, D), lambda i: (i, 4)),
                  pl.BlockSpec((1, 2 * D), lambda i: (0, 0)), tok, tok],
        out_specs=tok, out_shape=jax.ShapeDtypeStruct((t, D), BF16),
        compiler_params=_cparams(("arbitrary",)),
    )(proj, proj, b_merge, y_a, y_b)


def _merge_bwd(dmix, proj, b_merge, y_a, y_b):
    t = proj.shape[0]
    tr = _tile(t, 512)

    def body(dm_ref, la_ref, lb_ref, bm_ref, ya_ref, yb_ref, dya_ref, dyb_ref, dl_ref, acc):
        @pl.when(pl.program_id(0) == 0)
        def _():
            acc[...] = jnp.zeros_like(acc)

        dm = dm_ref[...].astype(F32)
        ga = _sigmoid(la_ref[...] + bm_ref[:, 0:D])
        gb = _sigmoid(lb_ref[...] + bm_ref[:, D:2 * D])
        dya_ref[...] = (dm * ga).astype(BF16)
        dyb_ref[...] = (dm * gb).astype(BF16)
        dla = dm * ya_ref[...].astype(F32) * ga * (1.0 - ga)
        dlb = dm * yb_ref[...].astype(F32) * gb * (1.0 - gb)
        dl_ref[:, 0:D] = dla.astype(BF16)
        dl_ref[:, D:2 * D] = dlb.astype(BF16)
        acc[0:1, 0:D] += jnp.sum(dla, axis=0, keepdims=True)
        acc[0:1, D:2 * D] += jnp.sum(dlb, axis=0, keepdims=True)

    tok = pl.BlockSpec((tr, D), lambda i: (i, 0))
    return pl.pallas_call(
        body, name="merge_bwd", grid=(t // tr,),
        in_specs=[tok, pl.BlockSpec((tr, D), lambda i: (i, 3)), pl.BlockSpec((tr, D), lambda i: (i, 4)),
                  pl.BlockSpec((1, 2 * D), lambda i: (0, 0)), tok, tok],
        out_specs=[tok, tok, pl.BlockSpec((tr, 2 * D), lambda i: (i, 0)), pl.BlockSpec((8, 2 * D), lambda i: (0, 0))],
        out_shape=[jax.ShapeDtypeStruct((t, D), BF16), jax.ShapeDtypeStruct((t, D), BF16),
                   jax.ShapeDtypeStruct((t, 2 * D), BF16), jax.ShapeDtypeStruct((8, 2 * D), F32)],
        compiler_params=_cparams(("arbitrary",)),
    )(dmix, proj, proj, b_merge, y_a, y_b)


GLA_HEADS = 2
GLA_UNROLL = 8


def _log_sigmoid(z):
    return jnp.minimum(z, 0.0) - jnp.log(1.0 + jnp.exp(-jnp.abs(z)))


def _tri(lower):
    r = lax.broadcasted_iota(jnp.int32, (CHUNK, CHUNK), 0)
    c = lax.broadcasted_iota(jnp.int32, (CHUNK, CHUNK), 1)
    return jnp.where(r >= c if lower else r <= c, 1.0, 0.0).astype(F32)


def _gla_fwd(proj, wa_pad, b_alpha, g_out, bsz):
    t = proj.shape[0]
    s = t // bsz
    nc = s // CHUNK
    p, kw, vw = GLA_HEADS, GLA_HEADS * GDK, GLA_HEADS * GDV

    def body(q_ref, k_ref, v_ref, gg_ref, ms_ref, wa_ref, ba_ref, go_ref, o_ref, og_ref, st_ref, la, state):
        z = _dot(ms_ref[...].astype(BF16), wa_ref[...], NN) + ba_ref[...]
        la[...] = _log_sigmoid(z) * (1.0 / GTAU)
        state[...] = jnp.zeros_like(state)
        low = _tri(True)
        gout = go_ref[...]

        def chunk(n, carry):
            rows = pl.ds(pl.multiple_of(n * CHUNK, CHUNK), CHUNK)
            for hh in range(p):
                kc, vc = slice(hh * GDK, (hh + 1) * GDK), slice(hh * GDV, (hh + 1) * GDV)
                lac = la[rows, kc]
                cum = _dot(low, lac, NN, lax.Precision.HIGHEST)
                ce = jnp.sum(lac, axis=0, keepdims=True)
                kd = (k_ref[rows, kc].astype(F32) * jnp.exp(ce - cum)).astype(BF16)
                new = state[vc, :] * jnp.exp(ce) + _dot(v_ref[rows, vc].astype(BF16), kd, TN)
                state[vc, :] = new
                st_ref[pl.ds(pl.multiple_of((hh * nc + n) * GDV, GDV), GDV), :] = new.astype(BF16)
                qs = (q_ref[rows, kc].astype(F32) * (GDK ** -0.5)).astype(BF16)
                o = _dot(qs, new.astype(BF16), NT)
                o_ref[rows, vc] = o
                ro = lax.rsqrt(jnp.mean(o * o, axis=1, keepdims=True) + EPS)
                og_ref[rows, vc] = (((o * ro) * gout) * _silu(gg_ref[rows, vc].astype(F32))).astype(BF16)
            return carry

        lax.fori_loop(0, nc, chunk, 0, unroll=GLA_UNROLL)

    return pl.pallas_call(
        body, name="gla_fwd", grid=(bsz, GH // p),
        in_specs=[pl.BlockSpec((s, kw), lambda b, h: (b, h)), pl.BlockSpec((s, kw), lambda b, h: (b, GH // p + h)),
                  pl.BlockSpec((s, vw), lambda b, h: (b, GH // p + h)), pl.BlockSpec((s, vw), lambda b, h: (b, 2 * GH // p + h)),
                  pl.BlockSpec((s, LANES), lambda b, h: (b, 43)),
                  pl.BlockSpec((LANES, kw), lambda b, h: (0, h)), pl.BlockSpec((1, kw), lambda b, h: (0, h)),
                  pl.BlockSpec((1, GDV), lambda b, h: (0, 0))],
        out_specs=[pl.BlockSpec((s, vw), lambda b, h: (b, h)), pl.BlockSpec((s, vw), lambda b, h: (b, h)),
                   pl.BlockSpec((p * nc * GDV, GDK), lambda b, h: (b * (GH // p) + h, 0))],
        out_shape=[jax.ShapeDtypeStruct((t, GH * GDV), F32), jax.ShapeDtypeStruct((t, GH * GDV), BF16),
                   jax.ShapeDtypeStruct((bsz * GH * nc * GDV, GDK), BF16)],
        scratch_shapes=[pltpu.VMEM((s, kw), F32), pltpu.VMEM((vw, GDK), F32)],
        compiler_params=_cparams(("arbitrary", "arbitrary")),
    )(proj, proj, proj, proj, proj, wa_pad, b_alpha, g_out)


def _gla_bwd(dog, o, proj, wa_pad, b_alpha, g_out, states, bsz):
    t = proj.shape[0]
    s = t // bsz
    nc = s // CHUNK
    p, kw, vw = GLA_HEADS, GLA_HEADS * GDK, GLA_HEADS * GDV

    def body(dog_ref, o_ref, q_ref, k_ref, v_ref, gg_ref, ms_ref, wa_ref, ba_ref, go_ref, st_ref,
             dq_ref, dk_ref, dv_ref, dgg_ref, dz_ref, dba, dgo, zs, la, carry_g):
        @pl.when(pl.program_id(1) == 0)
        def _():
            dba[...] = jnp.zeros_like(dba)
            dgo[...] = jnp.zeros_like(dgo)

        z = _dot(ms_ref[...].astype(BF16), wa_ref[...], NN) + ba_ref[...]
        zs[...] = z
        la[...] = _log_sigmoid(z) * (1.0 / GTAU)
        carry_g[...] = jnp.zeros_like(carry_g)
        low, upp = _tri(True), _tri(False)
        gout = go_ref[...]
        last_row = lax.broadcasted_iota(jnp.int32, (CHUNK, GDK), 0) == CHUNK - 1

        def chunk(step, carry):
            n = nc - 1 - step
            rows = pl.ds(pl.multiple_of(n * CHUNK, CHUNK), CHUNK)
            for hh in range(p):
                kc, vc = slice(hh * GDK, (hh + 1) * GDK), slice(hh * GDV, (hh + 1) * GDV)
                lac = la[rows, kc]
                cum = _dot(low, lac, NN, lax.Precision.HIGHEST)
                ce = jnp.sum(lac, axis=0, keepdims=True)
                e = jnp.exp(ce - cum)
                dec = jnp.exp(ce)
                kf = k_ref[rows, kc].astype(F32)
                kd = (kf * e).astype(BF16)
                vv = v_ref[rows, vc].astype(BF16)
                qs = (q_ref[rows, kc].astype(F32) * (GDK ** -0.5)).astype(BF16)
                ov = o_ref[rows, vc]
                ro = lax.rsqrt(jnp.mean(ov * ov, axis=1, keepdims=True) + EPS)
                on = ov * ro
                gg = gg_ref[rows, vc].astype(F32)
                sg = _sigmoid(gg)
                dogv = dog_ref[rows, vc].astype(F32)
                dgg_ref[rows, vc] = (dogv * (on * gout) * (sg * (1.0 + gg * (1.0 - sg)))).astype(BF16)
                t1 = dogv * (gg * sg)
                dgo[8 * hh:8 * hh + 1, :] += jnp.sum(t1 * on, axis=0, keepdims=True)
                don = t1 * gout
                do = ro * (don - on * jnp.mean(don * on, axis=1, keepdims=True))
                dob = do.astype(BF16)
                st_n = st_ref[pl.ds(pl.multiple_of((hh * nc + n) * GDV, GDV), GDV), :]
                dq_ref[rows, kc] = (_dot(dob, st_n, NN) * (GDK ** -0.5)).astype(BF16)
                dn = carry_g[vc, :] + _dot(dob, qs, TN)
                prev = hh * nc + jnp.maximum(n - 1, 0)
                st_p = st_ref[pl.ds(pl.multiple_of(prev * GDV, GDV), GDV), :].astype(F32) * jnp.where(n > 0, 1.0, 0.0)
                ddec = jnp.sum(dn * st_p, axis=0, keepdims=True)
                dnb = dn.astype(BF16)
                dkd = _dot(vv, dnb, NN)
                dv_ref[rows, vc] = _dot(kd, dnb, NT).astype(BF16)
                dk_ref[rows, kc] = (dkd * e).astype(BF16)
                w = dkd * kf * e
                dce = jnp.sum(w, axis=0, keepdims=True) + ddec * dec
                dcum = jnp.where(last_row, dce - w, -w)
                dla = _dot(upp, dcum, NN, lax.Precision.HIGHEST)
                dz = dla * (1.0 / GTAU) * _sigmoid(-zs[rows, kc])
                dba[0:1, kc] += jnp.sum(dz, axis=0, keepdims=True)
                dz_ref[rows, kc] = dz.astype(BF16)
                carry_g[vc, :] = dn * dec
            return carry

        lax.fori_loop(0, nc, chunk, 0, unroll=GLA_UNROLL)

    hv = pl.BlockSpec((s, vw), lambda h, b: (b, h))
    hk = pl.BlockSpec((s, kw), lambda h, b: (b, h))
    return pl.pallas_call(
        body, name="gla_bwd", grid=(GH // p, bsz),
        in_specs=[hv, hv, hk, pl.BlockSpec((s, kw), lambda h, b: (b, GH // p + h)),
                  pl.BlockSpec((s, vw), lambda h, b: (b, GH // p + h)), pl.BlockSpec((s, vw), lambda h, b: (b, 2 * GH // p + h)),
                  pl.BlockSpec((s, LANES), lambda h, b: (b, 43)), pl.BlockSpec((LANES, kw), lambda h, b: (0, h)),
                  pl.BlockSpec((1, kw), lambda h, b: (0, h)), pl.BlockSpec((1, GDV), lambda h, b: (0, 0)),
                  pl.BlockSpec((p * nc * GDV, GDK), lambda h, b: (b * (GH // p) + h, 0))],
        out_specs=[hk, hk, hv, hv, hk, pl.BlockSpec((8, kw), lambda h, b: (0, h)),
                   pl.BlockSpec((8 * p, GDV), lambda h, b: (h, 0))],
        out_shape=[jax.ShapeDtypeStruct((t, GH * GDK), BF16), jax.ShapeDtypeStruct((t, GH * GDK), BF16),
                   jax.ShapeDtypeStruct((t, GH * GDV), BF16), jax.ShapeDtypeStruct((t, GH * GDV), BF16),
                   jax.ShapeDtypeStruct((t, GH * GDK), BF16), jax.ShapeDtypeStruct((8, GH * GDK), F32),
                   jax.ShapeDtypeStruct((8 * GH, GDV), F32)],
        scratch_shapes=[pltpu.VMEM((s, kw), F32), pltpu.VMEM((s, kw), F32), pltpu.VMEM((vw, GDK), F32)],
        compiler_params=_cparams(("arbitrary", "arbitrary")),
    )(dog, o, proj, proj, proj, proj, proj, wa_pad, b_alpha, g_out, states)


def _rope_tables(pos_ref, fr_ref, sg_ref):
    ang = pos_ref[...].astype(F32) * fr_ref[...]
    return jnp.cos(ang), jnp.sin(ang) * sg_ref[...]


def _partner(x):
    lane = lax.broadcasted_iota(jnp.int32, x.shape, 1)
    return jnp.where(lane < NOPE + ROPE // 2, pltpu.roll(x, LANES - ROPE // 2, 1), pltpu.roll(x, ROPE // 2, 1))


def _mla_rows(t):
    return _tile(t, 512)


def _mla_pre_fwd(proj, pos, fr, sg, q_lat_g, kv_lat_g, qn_g, kn_g, wuq, wukv):
    t = proj.shape[0]
    tr = _mla_rows(t)

    def body(cq_ref, ckv_ref, ms_ref, pos_ref, fr_ref, sg_ref, qlg, kvlg, qng, kng, wuq_ref, wukv_ref, q_out, k_out, v_out):
        cos, sin = _rope_tables(pos_ref, fr_ref, sg_ref)
        cq = cq_ref[...].astype(F32)
        cqn = (cq * lax.rsqrt(jnp.mean(cq * cq, axis=1, keepdims=True) + EPS) * qlg[...]).astype(BF16)
        ckv = ckv_ref[...].astype(F32)
        ckvn = (ckv * lax.rsqrt(jnp.mean(ckv * ckv, axis=1, keepdims=True) + EPS) * kvlg[...]).astype(BF16)
        lane = lax.broadcasted_iota(jnp.int32, (tr, HP), 1)
        kpe = jnp.where((lane >= NOPE) & (lane < MQK), ms_ref[...].astype(F32), 0.0)
        lane_all = lax.broadcasted_iota(jnp.int32, (tr, MH * HP), 1)
        v_out[...] = jnp.where(lane_all % HP == MV, 1.0, _dot(ckvn, wukv_ref[:, MH * HP:], NN)).astype(BF16)
        for h in range(MH):
            cols = slice(h * HP, (h + 1) * HP)
            qh = _dot(cqn, wuq_ref[:, cols], NN)
            qn = qh * lax.rsqrt(jnp.sum(qh * qh, axis=1, keepdims=True) * (1.0 / MQK) + EPS) * qng[...]
            q_out[:, cols] = (qn * cos + _partner(qn) * sin).astype(BF16)
            kh = _dot(ckvn, wukv_ref[:, cols], NN) + kpe
            kn = kh * lax.rsqrt(jnp.sum(kh * kh, axis=1, keepdims=True) * (1.0 / MQK) + EPS) * kng[...]
            k_out[:, cols] = (kn * cos + _partner(kn) * sin).astype(BF16)

    def full(a):
        return pl.BlockSpec(a.shape, lambda i: (0, 0))

    wide = pl.BlockSpec((tr, MH * HP), lambda i: (i, 0))
    return pl.pallas_call(
        body, name="mla_pre_fwd", grid=(t // tr,),
        in_specs=[pl.BlockSpec((tr, MQR), lambda i: (i, 20)), pl.BlockSpec((tr, MKVR), lambda i: (i, 42)),
                  pl.BlockSpec((tr, LANES), lambda i: (i, 43)), pl.BlockSpec((tr, 1), lambda i: (i, 0)),
                  full(fr), full(sg), full(q_lat_g), full(kv_lat_g), full(qn_g), full(kn_g), full(wuq), full(wukv)],
        out_specs=[wide, wide, wide],
        out_shape=[jax.ShapeDtypeStruct((t, MH * HP), BF16)] * 3,
        compiler_params=_cparams(("arbitrary",)),
    )(proj, proj, proj, pos, fr, sg, q_lat_g, kv_lat_g, qn_g, kn_g, wuq, wukv)


def _mla_pre_bwd(dq2, dk2, dv2, dmisc_gla, proj, pos, fr, sg, q_lat_g, kv_lat_g, qn_g, kn_g, wuq, wukv):
    t = proj.shape[0]
    tr = _mla_rows(t)

    def body(dq_ref, dk_ref, dv_ref, dmg_ref, cq_ref, ckv_ref, ms_ref, pos_ref, fr_ref, sg_ref, qlg, kvlg, qng, kng,
             wuq_ref, wukv_ref, dcq_ref, dckv_ref, dms_ref, dwuq, dwukv, acc, dqf, dkvf):
        @pl.when(pl.program_id(0) == 0)
        def _():
            dwuq[...] = jnp.zeros_like(dwuq)
            dwukv[...] = jnp.zeros_like(dwukv)
            acc[...] = jnp.zeros_like(acc)

        cos, sin = _rope_tables(pos_ref, fr_ref, sg_ref)
        cq = cq_ref[...].astype(F32)
        rc = lax.rsqrt(jnp.mean(cq * cq, axis=1, keepdims=True) + EPS)
        xc = cq * rc
        cqn = (xc * qlg[...]).astype(BF16)
        ckv = ckv_ref[...].astype(F32)
        rkv = lax.rsqrt(jnp.mean(ckv * ckv, axis=1, keepdims=True) + EPS)
        xkv = ckv * rkv
        ckvn = (xkv * kvlg[...]).astype(BF16)
        lane = lax.broadcasted_iota(jnp.int32, (tr, HP), 1)
        is_rope = (lane >= NOPE) & (lane < MQK)
        kpe = jnp.where(is_rope, ms_ref[...].astype(F32), 0.0)
        dkpe = jnp.zeros((tr, HP), F32)
        dqng = jnp.zeros((1, HP), F32)
        dkng = jnp.zeros((1, HP), F32)
        for h in range(MH):
            cols = slice(h * HP, (h + 1) * HP)
            qh = _dot(cqn, wuq_ref[:, cols], NN)
            rq = lax.rsqrt(jnp.sum(qh * qh, axis=1, keepdims=True) * (1.0 / MQK) + EPS)
            xq = qh * rq
            dy = dq_ref[:, cols].astype(F32)
            dqn = dy * cos - _partner(dy) * sin
            dqng += jnp.sum(dqn * xq, axis=0, keepdims=True)
            tq = dqn * qng[...]
            dqf[:, cols] = (rq * (tq - xq * (jnp.sum(tq * xq, axis=1, keepdims=True) * (1.0 / MQK)))).astype(BF16)
            kh = _dot(ckvn, wukv_ref[:, cols], NN) + kpe
            rk = lax.rsqrt(jnp.sum(kh * kh, axis=1, keepdims=True) * (1.0 / MQK) + EPS)
            xk = kh * rk
            dy = dk_ref[:, cols].astype(F32)
            dkn = dy * cos - _partner(dy) * sin
            dkng += jnp.sum(dkn * xk, axis=0, keepdims=True)
            tk = dkn * kng[...]
            dkh = rk * (tk - xk * (jnp.sum(tk * xk, axis=1, keepdims=True) * (1.0 / MQK)))
            dkvf[:, cols] = jnp.where(lane < NOPE, dkh, 0.0).astype(BF16)
            dkpe += jnp.where(is_rope, dkh, 0.0)
        dkvf[:, MH * HP:] = dv_ref[...]
        acc[2:3, 0:HP] += dqng
        acc[3:4, 0:HP] += dkng
        dms_ref[...] = (dmg_ref[...] + dkpe).astype(BF16)

        dqfv = dqf[...]
        dwuq[...] += _dot(cqn, dqfv, TN)
        dcqn = _dot(dqfv, wuq_ref[...], NT)
        acc[0:1, :] += jnp.sum(dcqn * xc, axis=0, keepdims=True)
        tc = dcqn * qlg[...]
        dcq_ref[...] = (rc * (tc - xc * jnp.mean(tc * xc, axis=1, keepdims=True))).astype(BF16)

        dkvfv = dkvf[...]
        dwukv[...] += _dot(ckvn, dkvfv, TN)
        dckvn = _dot(dkvfv, wukv_ref[...], NT)
        acc[1:2, 0:MKVR] += jnp.sum(dckvn * xkv, axis=0, keepdims=True)
        tkv = dckvn * kvlg[...]
        dckv_ref[...] = (rkv * (tkv - xkv * jnp.mean(tkv * xkv, axis=1, keepdims=True))).astype(BF16)

    def full(a):
        return pl.BlockSpec(a.shape, lambda i: (0, 0))

    wide = pl.BlockSpec((tr, MH * HP), lambda i: (i, 0))
    narrow = pl.BlockSpec((tr, LANES), lambda i: (i, 0))
    return pl.pallas_call(
        body, name="mla_pre_bwd", grid=(t // tr,),
        in_specs=[wide, wide, wide, narrow,
                  pl.BlockSpec((tr, MQR), lambda i: (i, 20)), pl.BlockSpec((tr, MKVR), lambda i: (i, 42)),
                  pl.BlockSpec((tr, LANES), lambda i: (i, 43)), pl.BlockSpec((tr, 1), lambda i: (i, 0)),
                  full(fr), full(sg), full(q_lat_g), full(kv_lat_g), full(qn_g), full(kn_g), full(wuq), full(wukv)],
        out_specs=[pl.BlockSpec((tr, MQR), lambda i: (i, 0)), narrow, narrow,
                   pl.BlockSpec((MQR, MH * HP), lambda i: (0, 0)), pl.BlockSpec((MKVR, 2 * MH * HP), lambda i: (0, 0)),
                   pl.BlockSpec((8, MQR), lambda i: (0, 0))],
        out_shape=[jax.ShapeDtypeStruct((t, MQR), BF16), jax.ShapeDtypeStruct((t, MKVR), BF16),
                   jax.ShapeDtypeStruct((t, LANES), BF16), jax.ShapeDtypeStruct((MQR, MH * HP), F32),
                   jax.ShapeDtypeStruct((MKVR, 2 * MH * HP), F32), jax.ShapeDtypeStruct((8, MQR), F32)],
        scratch_shapes=[pltpu.VMEM((tr, MH * HP), BF16), pltpu.VMEM((tr, 2 * MH * HP), BF16)],
        compiler_params=_cparams(("arbitrary",)),
    )(dq2, dk2, dv2, dmisc_gla, proj, proj, proj, pos, fr, sg, q_lat_g, kv_lat_g, qn_g, kn_g, wuq, wukv)


ATT_FWD_TILES = (1024, 512)
ATT_BWD_TILES = (512, 512)
ATT_HEADS = 2
NEG = -1e30
LOG2E = 1.4426950408889634


def _att_mask(q0, k0, tq, tk):
    qc = (q0 + lax.broadcasted_iota(jnp.int32, (tq, tk), 0)) // CHUNK
    kc = (k0 + lax.broadcasted_iota(jnp.int32, (tq, tk), 1)) // CHUNK
    return kc <= qc


def _att_tiles(s, tiles):
    return _tile(s, tiles[0]), _tile(s, tiles[1])


def _lanes(x, n):
    return x if n == 1 else jnp.concatenate([x] * n, axis=1)


def _grid_ends(grid):
    i, j = pl.program_id(0), pl.program_id(1)
    return (i == 0) & (j == 0), (i == grid[0] - 1) & (j == grid[1] - 1)


def _attn_fwd(q2, k2, v2, bsz, slab):
    t = q2.shape[0]
    s = t // bsz
    tq, tk = _att_tiles(s, ATT_FWD_TILES)
    nq, groups, n_diag = s // tq, tk // HP, max(tq // tk, 1)
    sub_rows = tq // n_diag
    scale = MQK ** -0.5
    c2 = scale * LOG2E
    heads = range(ATT_HEADS)

    def body(q_ref, k_ref, v_ref, slab_ref, o_ref, lse_ref, gath_ref, send_sems, recv_sems, local_sem):
        gather = _core_row_gather_copies(slab_ref, gath_ref, send_sems, recv_sems, local_sem)
        first, last = _grid_ends((bsz, MH // ATT_HEADS))
        pl.when(first)(gather.start)

        def q_loop(qi, carry):
            q0 = pl.multiple_of(qi * tq, tq)
            rows = pl.ds(q0, tq)
            n_full = q0 // tk
            qs = [q_ref[rows, h * HP:(h + 1) * HP] for h in heads]

            def scores(h, kj, sub=None, masked=False):
                k0 = pl.multiple_of(kj * tk, tk)
                qv = qs[h] if sub is None else qs[h][sub * sub_rows:(sub + 1) * sub_rows]
                sc = _dot(qv, k_ref[pl.ds(k0, tk), h * HP:(h + 1) * HP], NT)
                return jnp.where(_att_mask(q0 + sub * sub_rows, k0, sub_rows, tk), sc, NEG) if masked else sc

            def fold(mx, sc):
                for j in range(groups):
                    mx = jnp.maximum(mx, sc[:, j * HP:(j + 1) * HP])
                return mx

            def over_diagonal(vals, step):
                out = []
                for h in heads:
                    blocks = []
                    for r in range(n_diag):
                        v = vals[h][r * sub_rows:(r + 1) * sub_rows]
                        for u in range(r + 1):
                            v = step(v, h, n_full + u, r, u == r)
                        blocks.append(v)
                    out.append(blocks[0] if n_diag == 1 else jnp.concatenate(blocks, axis=0))
                return tuple(out)

            mx = lax.fori_loop(0, n_full, lambda kj, mx: tuple(fold(mx[h], scores(h, kj)) for h in heads),
                               tuple(jnp.full((tq, HP), NEG, F32) for _ in heads))
            mx = over_diagonal(mx, lambda v, h, kj, r, masked: fold(v, scores(h, kj, r, masked)))
            mb = [jnp.broadcast_to(jnp.max(mx[h], axis=1, keepdims=True), (tq, HP)) for h in heads]

            def weighted(h, kj, sub=None, masked=False):
                m = mb[h] if sub is None else mb[h][sub * sub_rows:(sub + 1) * sub_rows]
                p = jnp.exp2((scores(h, kj, sub, masked) - _lanes(m, groups)) * c2)
                k0 = pl.multiple_of(kj * tk, tk)
                return _dot(p.astype(BF16), v_ref[pl.ds(k0, tk), h * HP:(h + 1) * HP], NN)

            acc = lax.fori_loop(0, n_full, lambda kj, acc: tuple(acc[h] + weighted(h, kj) for h in heads),
                                tuple(jnp.zeros((tq, HP), F32) for _ in heads))
            acc = over_diagonal(acc, lambda v, h, kj, r, masked: v + weighted(h, kj, r, masked))
            lane = lax.broadcasted_iota(jnp.int32, (tq, HP), 1)
            for h in heads:
                a = acc[h]
                l = jnp.sum(jnp.where(lane == MV, a, 0.0), axis=1, keepdims=True)
                o_ref[rows, h * HP:(h + 1) * HP] = (a / l).astype(BF16)
                lse_ref[rows, h * HP:(h + 1) * HP] = mb[h] * scale + jnp.log(l)
            return carry

        lax.fori_loop(0, nq, q_loop, 0)
        pl.when(last)(gather.finish)

    spec = pl.BlockSpec((s, ATT_HEADS * HP), lambda b, h: (b, h))
    return pl.pallas_call(
        body, name="attn_fwd", grid=(bsz, MH // ATT_HEADS), in_specs=[spec] * 3 + [HBM_SPEC],
        out_specs=[spec, spec, HBM_SPEC],
        out_shape=[jax.ShapeDtypeStruct((t, MH * HP), BF16), jax.ShapeDtypeStruct((t, MH * HP), F32),
                   jax.ShapeDtypeStruct((N_DEV,) + slab.shape, slab.dtype)],
        scratch_shapes=EXCHANGE_SEMS, compiler_params=_cparams(("arbitrary", "arbitrary")),
    )(q2, k2, v2, slab)


def _attn_bwd(q2, k2, v2, do2, o2, lse2, bsz, tsum):
    t = q2.shape[0]
    s = t // bsz
    tq, tk = _att_tiles(s, ATT_BWD_TILES)
    nq, nk, per, groups = s // tq, s // tk, max(tk // tq, 1), tk // HP
    scale = MQK ** -0.5
    c2 = scale * LOG2E
    heads = range(ATT_HEADS)

    def body(q_ref, k_ref, v_ref, do_ref, o_ref, lse_ref, t_ref, dq_ref, dk_ref, dv_ref, parts_ref, dq_acc, delta, lse_b2,
             send_sems, recv_sems, local_sem):
        exchange = _all_to_all_copies(t_ref, parts_ref, send_sems, recv_sems, local_sem)
        first, last = _grid_ends((bsz, MH // ATT_HEADS))
        pl.when(first)(exchange.start)
        dq_acc[...] = jnp.zeros_like(dq_acc)

        def d_loop(i, carry):
            rows = pl.ds(pl.multiple_of(i * tq, tq), tq)
            for h in heads:
                hs = slice(h * HP, (h + 1) * HP)
                dl = jnp.sum(do_ref[rows, hs].astype(F32) * o_ref[rows, hs].astype(F32), axis=1, keepdims=True)
                delta[rows, hs] = jnp.broadcast_to(dl, (tq, HP))
            lse_b2[rows, :] = lse_ref[rows, :] * LOG2E
            return carry

        lax.fori_loop(0, nq, d_loop, 0)

        def k_loop(kj, carry):
            k0 = pl.multiple_of(kj * tk, tk)
            kk = [k_ref[pl.ds(k0, tk), h * HP:(h + 1) * HP] for h in heads]
            vv = [v_ref[pl.ds(k0, tk), h * HP:(h + 1) * HP] for h in heads]

            def tile(qi, c, masked):
                q0 = pl.multiple_of(qi * tq, tq)
                rows = pl.ds(q0, tq)
                out = []
                for h in heads:
                    hs = slice(h * HP, (h + 1) * HP)
                    dk, dv = c[h]
                    q = q_ref[rows, hs]
                    do = do_ref[rows, hs]
                    e = _dot(q, kk[h], NT) * c2 - _lanes(lse_b2[rows, hs], groups)
                    if masked:
                        e = jnp.where(_att_mask(q0, k0, tq, tk), e, NEG)
                    p = jnp.exp2(e)
                    dv = dv + _dot(p.astype(BF16), do, TN)
                    ds = (p * (_dot(do, vv[h], NT) - _lanes(delta[rows, hs], groups))).astype(BF16)
                    dq_acc[rows, hs] += _dot(ds, kk[h], NN)
                    dk = dk + _dot(ds, q, TN)
                    out.append((dk, dv))
                return tuple(out)

            zero = jnp.zeros((tk, HP), F32)
            c = tuple((zero, zero) for _ in heads)
            first = k0 // tq
            for u in range(per):
                c = tile(first + u, c, True)
            c = lax.fori_loop(first + per, nq, lambda qi, c: tile(qi, c, False), c)
            for h in heads:
                dk_ref[pl.ds(k0, tk), h * HP:(h + 1) * HP] = (c[h][0] * scale).astype(BF16)
                dv_ref[pl.ds(k0, tk), h * HP:(h + 1) * HP] = c[h][1].astype(BF16)
            return carry

        lax.fori_loop(0, nk, k_loop, 0)
        dq_ref[...] = (dq_acc[...] * scale).astype(BF16)
        pl.when(last)(exchange.finish)

    spec = pl.BlockSpec((s, ATT_HEADS * HP), lambda b, h: (b, h))
    return pl.pallas_call(
        body, name="attn_bwd", grid=(bsz, MH // ATT_HEADS), in_specs=[spec] * 6 + [HBM_SPEC],
        out_specs=[spec] * 3 + [HBM_SPEC],
        out_shape=[jax.ShapeDtypeStruct((t, MH * HP), BF16)] * 3 + [jax.ShapeDtypeStruct(tsum.shape, tsum.dtype)],
        scratch_shapes=[pltpu.VMEM((s, ATT_HEADS * HP), F32)] * 3 + EXCHANGE_SEMS,
        compiler_params=_cparams(("arbitrary", "arbitrary")),
    )(q2, k2, v2, do2, o2, lse2, tsum)


def _perm_w_in_t(w):
    z = lambda n: jnp.zeros((n, w.shape[1]), w.dtype)
    return jnp.concatenate([w[:3072], w[3504:5552], w[3088:3344], w[3344:3472], w[3072:3088], z(48), w[3472:3504], z(32)],
                           axis=0)


def _unperm_w_in_t(g):
    return jnp.concatenate([g[:3072], g[5504:5520], g[5120:5376], g[5376:5504], g[5568:5600], g[3072:5120]], axis=0)


def _pad_wa(w):
    return jnp.pad(w, ((0, LANES - GLR), (0, 0)))


def _pad_wuq(w):
    return jnp.pad(w.reshape(MQR, MH, MQK), ((0, 0), (0, 0), (0, HP - MQK))).reshape(MQR, MH * HP)


def _unpad_wuq(g):
    return g.reshape(MQR, MH, HP)[:, :, :MQK].reshape(MQR, MH * MQK)


def _pad_wukv(w):
    w3 = w.reshape(MKVR, MH, NOPE + MV)
    kp = jnp.pad(w3[:, :, :NOPE], ((0, 0), (0, 0), (0, HP - NOPE))).reshape(MKVR, MH * HP)
    vp = jnp.pad(w3[:, :, NOPE:], ((0, 0), (0, 0), (0, HP - MV))).reshape(MKVR, MH * HP)
    return jnp.concatenate([kp, vp], axis=1)


def _unpad_wukv(g):
    kp = g[:, :MH * HP].reshape(MKVR, MH, HP)[:, :, :NOPE]
    vp = g[:, MH * HP:].reshape(MKVR, MH, HP)[:, :, :MV]
    return jnp.concatenate([kp, vp], axis=2).reshape(MKVR, MH * (NOPE + MV))


def _pad_wo(w):
    return jnp.pad(w.reshape(MH, MV, D), ((0, 0), (0, HP - MV), (0, 0))).reshape(MH * HP, D)


def _unpad_wo(g):
    return g.reshape(MH, HP, D)[:, :MV, :].reshape(MH * MV, D)


def _pad_lanes(v, n=HP):
    return jnp.pad(v, ((0, 0), (0, n - v.shape[1])))


def _local_step(x, positions, tgt, wt, slab_b, sp, mod3):
    bsz, s, _ = x.shape
    t = bsz * s
    x2 = x.reshape(t, D)
    tgt2 = tgt.reshape(t, D)
    pos = positions.reshape(t, 1)
    fr16 = ROPE_THETA ** (-jnp.arange(0, ROPE, 2, dtype=F32) / ROPE)
    zero = lambda n: jnp.zeros((n,), F32)
    fr = jnp.concatenate([zero(NOPE), fr16, fr16, zero(HP - MQK)]).reshape(1, HP)
    sg = jnp.concatenate([zero(NOPE), -jnp.ones((ROPE // 2,), F32), jnp.ones((ROPE // 2,), F32), zero(HP - MQK)]).reshape(1, HP)

    w_in_t = _perm_w_in_t(wt["w_in"])
    wa_pad = _pad_wa(wt["gla_w_alpha"].T)
    wuq = _pad_wuq(wt["mla_w_uq"].T)
    wukv = _pad_wukv(wt["mla_w_ukv"].T)
    qn_g, kn_g = _pad_lanes(sp["mla_qn_g"]), _pad_lanes(sp["mla_kn_g"])

    _, h = _norm_mod_fwd(x2, sp["norm1_g"], mod3, 0, 1, "norm1_fwd")
    proj = _mm(h, w_in_t, "nt", (BF16,), "proj_fwd", tn=PW // 4)
    o_gla, og, states = _gla_fwd(proj, wa_pad, sp["gla_b_alpha"], sp["gla_out_norm_g"], bsz)
    q2, k2, v2 = _mla_pre_fwd(proj, pos, fr, sg, sp["mla_q_lat_g"], sp["mla_kv_lat_g"], qn_g, kn_g, wuq, wukv)
    o2, lse2, core_row = _attn_fwd(q2, k2, v2, bsz, slab_b)
    wt = dict(wt, **_unpack_gathered(_cross_core_fill(core_row), SLAB_B))
    wo_pad = _pad_wo(wt["mla_w_o"])
    y_a = _mm(og, wt["gla_w_o"], "nn", (BF16,), "gla_out_fwd")
    y_b = _mm(o2, wo_pad, "nn", (BF16,), "mla_out_fwd")
    mix = _merge_fwd(proj, sp["b_merge"], y_a, y_b)
    mixed = _mm(mix, wt["w_out"], "nn", (F32,), "w_out_fwd")

    x1, h2 = _norm_mod_fwd(x2, sp["norm2_g"], mod3, 3, 4, "norm2_fwd", mixed=mixed, i_gate=2)
    a, f = _mm(h2, wt["mlp_w1"], "nt", (BF16, BF16), "mlp1_fwd",
               epi=lambda acc: (acc, jnp.square(jnp.maximum(acc, 0.0))))
    ff = _mm(f, wt["mlp_w2"], "nn", (F32,), "mlp2_fwd")
    dy, dff, acc_g2, acc_loss = _loss_head(x1, ff, tgt2, mod3)

    gw = {}
    gw["mlp_w2"] = _mm(f, dff, "tn", (BF16,), "mlp2_dw")
    da = _mm(dff, wt["mlp_w2"], "nt", (BF16,), "mlp2_dx", extras=(a,),
             epi=lambda acc, av: (acc * (2.0 * jnp.maximum(av.astype(F32), 0.0)),))
    gw["mlp_w1"] = _mm(da, h2, "tn", (BF16,), "mlp1_dw")
    dh2 = _mm(da, wt["mlp_w1"], "nn", (F32,), "mlp1_dx")
    dx1, dmixed, accb2, accg2 = _norm_mod_bwd(x1, dh2, dy, sp["norm2_g"], mod3, 3, 4, "norm2_bwd", mixed=mixed, i_gate=2)

    gw["w_out"] = _mm(mix, dmixed, "tn", (BF16,), "w_out_dw")
    dmix = _mm(dmixed, wt["w_out"], "nt", (BF16,), "w_out_dx")
    dy_a, dy_b, dlogits, acc_bm = _merge_bwd(dmix, proj, sp["b_merge"], y_a, y_b)
    gw["gla_w_o"] = _mm(og, dy_a, "tn", (BF16,), "gla_out_dw")
    dog = _mm(dy_a, wt["gla_w_o"], "nt", (BF16,), "gla_out_dx")
    gw["mla_w_o"] = _unpad_wo(_mm(o2, dy_b, "tn", (BF16,), "mla_out_dw"))
    do2 = _mm(dy_b, wo_pad, "nt", (BF16,), "mla_out_dx")
    dq2, dk2, dv2, parts_b = _attn_bwd(q2, k2, v2, do2, o2, lse2, bsz, _pack_per_device(gw, SLAB_B))
    dq_g, dk_g, dv_g, dgg, dz, acc_ba, acc_go = _gla_bwd(dog, o_gla, proj, wa_pad, sp["gla_b_alpha"],
                                                         sp["gla_out_norm_g"], states, bsz)
    gw["gla_w_alpha"] = _mm(proj, dz, "tn", (F32,), "gla_alpha_dw", a_off=43, m=LANES)[:GLR].T.astype(BF16)
    dmisc_gla = _mm(dz, wa_pad, "nt", (F32,), "gla_alpha_dx")
    dcq, dckv, dmisc, gwuq, gwukv, acc_mla = _mla_pre_bwd(dq2, dk2, dv2, dmisc_gla, proj, pos, fr, sg, sp["mla_q_lat_g"],
                                                         sp["mla_kv_lat_g"], qn_g, kn_g, wuq, wukv)
    gw["mla_w_uq"] = _unpad_wuq(gwuq).T.astype(BF16)
    gw["mla_w_ukv"] = _unpad_wukv(gwukv).T.astype(BF16)
    dproj = jnp.concatenate([dq_g, dk_g, dv_g, dgg, dlogits, dcq, dckv, dmisc], axis=1)
    gw["w_in"] = _unperm_w_in_t(_mm(dproj, h, "tn", (BF16,), "proj_dw", tm=PW // 4))
    dh, parts_a = _mm(dproj, w_in_t, "nn", (F32,), "proj_dx", tk=PW // 4,
                      cargo=_sum_over_cores(_pack_per_device(gw, SLAB_A), "a"))
    grad_x, accb1, accg1 = _norm_mod_bwd(x2, dh, dx1, sp["norm1_g"], mod3, 0, 1, "norm1_bwd")

    dmod = jnp.stack([accb1[:, 0], accb1[:, 1], accb2[:, 2], accb2[:, 0], accb2[:, 1], acc_g2[:, 0]], axis=1)

    rows = {
        "dmod": dmod.reshape(bsz * 6, D),
        "norm1_g": accg1[0:1], "norm2_g": accg2[0:1],
        "b_merge": acc_bm[0:1].reshape(2, D),
        "gla_b_alpha": _pad_lanes(acc_ba[0:1], D),
        "gla_out_norm_g": _pad_lanes(acc_go.reshape(GH, 8, GDV)[:, 0, :], D),
        "mla_q_lat_g": _pad_lanes(acc_mla[0:1], D), "mla_kv_lat_g": _pad_lanes(acc_mla[1:2], D),
        "mla_qn_g": _pad_lanes(acc_mla[2:3], D), "mla_kn_g": _pad_lanes(acc_mla[3:4], D),
        "loss": acc_loss[0:1],
    }
    return grad_x.reshape(bsz, s, D), parts_a, parts_b, rows


HBM_SPEC = pl.BlockSpec(memory_space=pltpu.HBM)


def _all_gather(p, name):
    r, cdim = p.shape

    def body(p_ref, out_ref, send_sems, recv_sems, local_sem):
        x, y, c = lax.axis_index("x"), lax.axis_index("y"), lax.axis_index("c")
        me, sibling = (x, y, c), (x, y, 1 - c)
        chips = [(1 - x, y), (x, 1 - y), (1 - x, 1 - y)]

        def slot(px, py, pc):
            return out_ref.at[4 * px + 2 * py + pc]

        def copy(k, block, to, src=None):
            return pltpu.make_async_remote_copy(
                src_ref=slot(*block) if src is None else src, dst_ref=slot(*block),
                send_sem=send_sems.at[k], recv_sem=recv_sems.at[k], device_id=to, device_id_type=MESH)

        mine = pltpu.make_async_copy(p_ref, slot(*me), local_sem)
        mine.start()
        first = [copy(0, me, sibling, src=p_ref)] + [copy(1 + j, me, (*chip, c), src=p_ref) for j, chip in enumerate(chips)]
        for cp in first:
            cp.start()
        passed = [copy(4 + j, (*chip, c), sibling) for j, chip in enumerate(chips)]
        for j, chip in enumerate(chips):
            copy(1 + j, (*chip, c), me).wait_recv()
            passed[j].start()
        copy(0, sibling, me).wait_recv()
        for j, chip in enumerate(chips):
            copy(4 + j, (*chip, 1 - c), me).wait_recv()
        for cp in first + passed:
            cp.wait_send()
        mine.wait()

    return pl.pallas_call(
        body, name=name, out_shape=jax.ShapeDtypeStruct((N_DEV, r, cdim), p.dtype),
        in_specs=[HBM_SPEC], out_specs=HBM_SPEC,
        scratch_shapes=[pltpu.SemaphoreType.DMA((7,)), pltpu.SemaphoreType.DMA((7,)), pltpu.SemaphoreType.DMA(())],
    )(p)


def _sibling_exchange(g, name):
    def body(g_ref, out_ref, send_sem, recv_sem):
        x, y, c = lax.axis_index("x"), lax.axis_index("y"), lax.axis_index("c")
        cp = pltpu.make_async_remote_copy(src_ref=g_ref, dst_ref=out_ref, send_sem=send_sem, recv_sem=recv_sem,
                                          device_id=(x, y, 1 - c), device_id_type=MESH)
        cp.start()
        cp.wait()

    return pl.pallas_call(
        body, name=name, out_shape=jax.ShapeDtypeStruct(g.shape, g.dtype),
        in_specs=[HBM_SPEC], out_specs=HBM_SPEC,
        scratch_shapes=[pltpu.SemaphoreType.DMA(()), pltpu.SemaphoreType.DMA(())],
    )(g)


class _Exchange:
    def __init__(self, local, sends, arrivals):
        self.local, self.sends, self.arrivals = local, sends, arrivals

    def start(self):
        self.local.start()
        for cp in self.sends:
            cp.start()

    def finish(self):
        for cp in self.arrivals:
            cp.wait_recv()
        for cp in self.sends:
            cp.wait_send()
        self.local.wait()


EXCHANGE_SEMS = [pltpu.SemaphoreType.DMA((N_DEV,)), pltpu.SemaphoreType.DMA((N_DEV,)), pltpu.SemaphoreType.DMA(())]


def _all_to_all_copies(t_ref, out_ref, send_sems, recv_sems, local_sem):
    x, y, c = lax.axis_index("x"), lax.axis_index("y"), lax.axis_index("c")
    me = 4 * x + 2 * y + c

    def copy(k, src, dst):
        px, py, pc = (1 - x if k & 4 else x), (1 - y if k & 2 else y), (1 - c if k & 1 else c)
        peer = 4 * px + 2 * py + pc
        return pltpu.make_async_remote_copy(src_ref=t_ref.at[peer if src is None else src],
                                            dst_ref=out_ref.at[peer if dst is None else dst], send_sem=send_sems.at[k],
                                            recv_sem=recv_sems.at[k], device_id=(px, py, pc), device_id_type=MESH)

    return _Exchange(pltpu.make_async_copy(t_ref.at[me], out_ref.at[me], local_sem),
                     [copy(k, None, me) for k in range(1, N_DEV)], [copy(k, me, None) for k in range(1, N_DEV)])


def _chip_exchange_copies(t_ref, out_ref, send_sems, recv_sems, local_sem):
    x, y, c = lax.axis_index("x"), lax.axis_index("y"), lax.axis_index("c")
    my_chip = 2 * x + y
    chips = [(1 - x, y), (x, 1 - y), (1 - x, 1 - y)]

    def copy(j, src, dst, px, py):
        return pltpu.make_async_remote_copy(src_ref=t_ref.at[src], dst_ref=out_ref.at[dst], send_sem=send_sems.at[j],
                                            recv_sem=recv_sems.at[j], device_id=(px, py, c), device_id_type=MESH)

    return _Exchange(pltpu.make_async_copy(t_ref.at[my_chip], out_ref.at[my_chip], local_sem),
                     [copy(j, 2 * px + py, my_chip, px, py) for j, (px, py) in enumerate(chips)],
                     [copy(j, my_chip, 2 * px + py, px, py) for j, (px, py) in enumerate(chips)])


def _core_row_gather_copies(p_ref, out_ref, send_sems, recv_sems, local_sem):
    x, y, c = lax.axis_index("x"), lax.axis_index("y"), lax.axis_index("c")
    peers = [(x, y, 1 - c), (1 - x, y, c), (x, 1 - y, c), (1 - x, 1 - y, c)]

    def slot(px, py, pc):
        return out_ref.at[4 * px + 2 * py + pc]

    def copy(j, block, to):
        return pltpu.make_async_remote_copy(src_ref=p_ref, dst_ref=slot(*block), send_sem=send_sems.at[j],
                                            recv_sem=recv_sems.at[j], device_id=to, device_id_type=MESH)

    return _Exchange(pltpu.make_async_copy(p_ref, slot(x, y, c), local_sem),
                     [copy(j, (x, y, c), peer) for j, peer in enumerate(peers)],
                     [copy(j, peer, peer) for j, peer in enumerate(peers)])


def _cross_core_fill(gathered):
    def body(g_ref, out_ref, send_sems, recv_sems):
        x, y, c = lax.axis_index("x"), lax.axis_index("y"), lax.axis_index("c")
        chips = [(1 - x, y), (x, 1 - y), (1 - x, 1 - y)]

        def copy(j, pc):
            px, py = chips[j]
            slot = 4 * px + 2 * py + pc
            return pltpu.make_async_remote_copy(src_ref=g_ref.at[slot], dst_ref=out_ref.at[slot], send_sem=send_sems.at[j],
                                                recv_sem=recv_sems.at[j], device_id=(x, y, 1 - c), device_id_type=MESH)

        sends = [copy(j, c) for j in range(3)]
        for cp in sends:
            cp.start()
        for j in range(3):
            copy(j, 1 - c).wait_recv()
        for cp in sends:
            cp.wait_send()

    return pl.pallas_call(
        body, name="weights_cross_core_fill", out_shape=jax.ShapeDtypeStruct(gathered.shape, gathered.dtype),
        in_specs=[HBM_SPEC], out_specs=HBM_SPEC, input_output_aliases={0: 0},
        scratch_shapes=[pltpu.SemaphoreType.DMA((3,)), pltpu.SemaphoreType.DMA((3,))],
    )(gathered)


def _slab_block(r):
    return max(b for b in range(16, SLAB_BLOCK_MAX + 1, 16) if r % b == 0)


def _pair_sum(a, b, name):
    n, r, cdim = a.shape
    rb = _slab_block(r)
    blk = pl.BlockSpec((1, rb, cdim), lambda j, i: (j, i, 0))

    def body(a_ref, b_ref, o_ref):
        o_ref[...] = (a_ref[...].astype(F32) + b_ref[...].astype(F32)).astype(BF16)

    return pl.pallas_call(
        body, name=name, grid=(n, r // rb), in_specs=[blk, blk], out_specs=blk,
        out_shape=jax.ShapeDtypeStruct(a.shape, BF16), compiler_params=_cparams(("arbitrary", "arbitrary")),
    )(a, b)


def _adamw_math(w, g, m, v):
    m = ADAM_B1 * m + (1.0 - ADAM_B1) * g
    v = ADAM_B2 * v + (1.0 - ADAM_B2) * jnp.square(g)
    m_hat = m / (1.0 - ADAM_B1 ** ADAM_STEP)
    v_hat = v / (1.0 - ADAM_B2 ** ADAM_STEP)
    delta = -ADAM_LR * (m_hat / (jnp.sqrt(v_hat) + ADAM_EPS) + ADAM_WD * w)
    return delta, m, v


def _slab_sum(parts, name):
    n, r, cdim = parts.shape
    rb = _slab_block(r)
    blk = pl.BlockSpec((rb, cdim), lambda i: (i, 0))

    def body(p_ref, g_out):
        g = p_ref[0].astype(F32)
        for j in range(1, n):
            g = g + p_ref[j].astype(F32)
        g_out[...] = g

    return pl.pallas_call(
        body, name=name, grid=(r // rb,),
        in_specs=[pl.BlockSpec((n, rb, cdim), lambda i: (0, i, 0))], out_specs=blk,
        out_shape=jax.ShapeDtypeStruct((r, cdim), F32), compiler_params=_cparams(("arbitrary",)),
    )(parts)


def _adamw(g, w, m, v, name):
    r, cdim = w.shape
    rb = _tile(r, 256)
    blk = pl.BlockSpec((rb, cdim), lambda i: (i, 0))

    def body(g_ref, w_ref, m_ref, v_ref, d_out, m_out, v_out):
        d_out[...], m_out[...], v_out[...] = _adamw_math(w_ref[...], g_ref[...], m_ref[...], v_ref[...])

    return pl.pallas_call(
        body, name=name, grid=(r // rb,), in_specs=[blk] * 4, out_specs=[blk] * 3,
        out_shape=[jax.ShapeDtypeStruct((r, cdim), F32)] * 3, compiler_params=_cparams(("arbitrary",)),
    )(g, w, m, v)


def _adamw_small(parts, w, m, v):
    def body(p_ref, w_ref, m_ref, v_ref, g_out, d_out, m_out, v_out, loss_out):
        def total(srcs):
            acc = None
            for r in srcs:
                for j in range(N_DEV):
                    term = p_ref[j, r:r + 1, :]
                    acc = term if acc is None else acc + term
            return acc

        for prow, srcs in enumerate(SMALL_SOURCES):
            one = slice(prow, prow + 1)
            g = total(srcs)
            g_out[one, :] = g
            d_out[one, :], m_out[one, :], v_out[one, :] = _adamw_math(w_ref[one, :], g, m_ref[one, :], v_ref[one, :])
        loss_out[...] = jnp.broadcast_to(jnp.sum(total(LOSS_SOURCE), axis=1, keepdims=True), (8, LANES))

    full = lambda shp: pl.BlockSpec(shp, lambda i: (0,) * len(shp))
    return pl.pallas_call(
        body, name="adamw_small", grid=(1,),
        in_specs=[full((N_DEV, SMALL_ROWS, D)), full((16, D)), full((16, D)), full((16, D))],
        out_specs=[full((16, D))] * 4 + [full((8, LANES))],
        out_shape=[jax.ShapeDtypeStruct((16, D), F32)] * 4 + [jax.ShapeDtypeStruct((8, LANES), F32)],
        compiler_params=_cparams(("arbitrary",)),
    )(parts, w, m, v)


def _to_slab(shard, form):
    if form == "N":
        return shard
    return shard.T if form == "T" else shard.T.reshape(-1, D)


def _from_slab(block, form, shard_shape):
    if form == "N":
        return block
    return block.T if form == "T" else block.reshape(shard_shape[1], shard_shape[0]).T


def _gathered_full(g, form, shard_shape):
    if form == "TR":
        return g.reshape(N_DEV * shard_shape[1], shard_shape[0])
    return g.reshape(N_DEV * g.shape[1], D)


def _pack_slab(shards, layout):
    return jnp.concatenate([jnp.zeros((r, D), BF16) if n is None else _to_slab(shards[n], form).astype(BF16)
                            for n, r, form in layout], axis=0)


def _unpack_gathered(gathered, layout):
    out, off = {}, 0
    for n, r, form in layout:
        if n is not None:
            out[n] = _gathered_full(gathered[:, off:off + r], form, SHARD_SHAPES[n])
        off += r
    return out


def _pack_per_device(gw, layout):
    return jnp.concatenate([jnp.zeros((N_DEV, r, D), BF16) if n is None else gw[n].reshape(N_DEV, r, D)
                            for n, r, _ in layout], axis=1)


def _adamw_shards(gslab, layout, wts, mom, var):
    out, off = {}, 0
    for n, r, form in layout:
        if n is not None:
            g = _from_slab(gslab[off:off + r], form, SHARD_SHAPES[n])
            out[n] = (g,) + tuple(_adamw(g, wts[n], mom[n], var[n], "adamw_" + n))
        off += r
    return out


def _sum_over_cores(per_dev, tag):
    my_c = lax.axis_index("c")
    pairs = per_dev.reshape(4, 2, per_dev.shape[1], D)
    keep = lax.dynamic_index_in_dim(pairs, my_c, axis=1, keepdims=False)
    give = lax.dynamic_index_in_dim(pairs, 1 - my_c, axis=1, keepdims=False)
    return _pair_sum(keep, _sibling_exchange(give, "rs_sibling_exchange_" + tag), "rs_pair_sum_" + tag)


def _small_pack(vals):
    rows = []
    for n in SMALL:
        v = vals[n].reshape(-1)
        k = -(-v.shape[0] // D)
        rows.append(jnp.pad(v, (0, k * D - v.shape[0])).reshape(k, D))
    return jnp.concatenate(rows, axis=0)


def _small_unpack(packed, shapes):
    out = {}
    for n in SMALL:
        k = shapes[n][-1]
        r0 = SMALL_ROW[n]
        out[n] = packed[r0:r0 + -(-k // D)].reshape(-1)[:k].reshape(shapes[n])
    return out


def kernel(x, c, positions, w_ada, b_ada, norm1_g, w_in, b_merge, gla_w_alpha, gla_b_alpha, gla_out_norm_g, gla_w_o, mla_q_lat_g, mla_w_uq, mla_kv_lat_g, mla_w_ukv, mla_qn_g, mla_kn_g, mla_w_o, w_out, norm2_g, mlp_w1, mlp_w2, loss_target, m_w_ada, m_b_ada, m_norm1_g, m_w_in, m_b_merge, m_gla_w_alpha, m_gla_b_alpha, m_gla_out_norm_g, m_gla_w_o, m_mla_q_lat_g, m_mla_w_uq, m_mla_kv_lat_g, m_mla_w_ukv, m_mla_qn_g, m_mla_kn_g, m_mla_w_o, m_w_out, m_norm2_g, m_mlp_w1, m_mlp_w2, v_w_ada, v_b_ada, v_norm1_g, v_w_in, v_b_merge, v_gla_w_alpha, v_gla_b_alpha, v_gla_out_norm_g, v_gla_w_o, v_mla_q_lat_g, v_mla_w_uq, v_mla_kv_lat_g, v_mla_w_ukv, v_mla_qn_g, v_mla_kn_g, v_mla_w_o, v_w_out, v_norm2_g, v_mlp_w1, v_mlp_w2):
    args = dict(locals())
    wts = {n: args[n][0] for n in WEIGHTS}
    mom = {n: args["m_" + n][0] for n in WEIGHTS}
    var = {n: args["v_" + n][0] for n in WEIGHTS}
    my_c = lax.axis_index("c")
    my_dev = 4 * lax.axis_index("x") + 2 * lax.axis_index("y") + my_c
    bsz = x.shape[0]
    sp = {n: wts[n].reshape(1, -1) for n in SMALL}

    wt = _unpack_gathered(_all_gather(_pack_slab(wts, SLAB_A), "weights_all_gather"), SLAB_A)

    c_all = _all_gather(jnp.pad(c, ((0, 8 - bsz), (0, 0))), "cond_all_gather")[:, :bsz].reshape(N_DEV * bsz, D)
    bias = lax.dynamic_slice_in_dim(sp["b_ada"], my_dev * ADA_COLS, ADA_COLS, axis=1)
    mod_cols = _mm(c_all, wts["w_ada"], "nn", (F32,), "ada_fwd", pro=_silu, epi=lambda acc, b: (acc + b,),
                   extras=(jnp.broadcast_to(bias, (N_DEV * bsz, ADA_COLS)),))
    mod_all = _all_gather(mod_cols, "mod_all_gather")
    mod_mine = lax.dynamic_slice_in_dim(mod_all, my_dev * bsz, bsz, axis=1)
    mod3 = jnp.transpose(mod_mine, (1, 0, 2)).reshape(bsz, 6, D)

    grad_x, parts_a, parts_b, rows = _local_step(x, positions, loss_target, wt, _pack_slab(wts, SLAB_B), sp, mod3)

    big = dict(_adamw_shards(_slab_sum(parts_a, "rs_slab_sum_a"), SLAB_A, wts, mom, var),
               **_adamw_shards(_slab_sum(parts_b, "rs_slab_sum_b"), SLAB_B, wts, mom, var))

    order = ["dmod", "norm1_g", "norm2_g", "b_merge", "gla_b_alpha", "gla_out_norm_g", "mla_q_lat_g", "mla_kv_lat_g",
             "mla_qn_g", "mla_kn_g", "loss"]
    part_rows = jnp.concatenate([rows[n] for n in order], axis=0)
    part_rows = jnp.pad(part_rows, ((0, SMALL_ROWS - part_rows.shape[0]), (0, 0)))
    all_rows = _all_gather(part_rows, "partials_all_gather")

    dmod_all = all_rows[:, :6 * bsz].reshape(N_DEV * bsz, 6 * D)
    dmod_cols = lax.dynamic_slice_in_dim(dmod_all, my_dev * ADA_COLS, ADA_COLS, axis=1)
    g_ada = _mm(c_all, dmod_cols, "tn", (F32,), "ada_dw", pro=_silu)
    big["w_ada"] = (g_ada,) + tuple(_adamw(g_ada, wts["w_ada"], mom["w_ada"], var["w_ada"], "adamw_w_ada"))

    small = _adamw_small(all_rows, _small_pack({n: wts[n] for n in SMALL}), _small_pack({n: mom[n] for n in SMALL}),
                         _small_pack({n: var[n] for n in SMALL}))
    loss = small[4][0, 0]
    small_shapes = {n: wts[n].shape for n in SMALL}
    small = [_small_unpack(o, small_shapes) for o in small[:4]]

    outs = [loss, grad_x]
    for k in range(4):
        for n in WEIGHTS:
            val = big[n][k] if n in BIG else small[k][n]
            outs.append(val.reshape((1,) + tuple(wts[n].shape)))
    return tuple(outs)
```

```python
import functools

import jax
import jax.numpy as jnp
from jax import lax
from jax.experimental import pallas as pl
from jax.experimental.pallas import tpu as pltpu

F32 = jnp.float32
BF16 = jnp.bfloat16
MESH = pl.DeviceIdType.MESH

D = 1024
EPS = 1e-6
CHUNK = 64
GH, GDK, GDV, GLR, GTAU = 4, 128, 256, 16, 16.0
MH, MQR, MKVR, NOPE, ROPE, MV = 16, 256, 128, 64, 32, 64
MQK = NOPE + ROPE
HP = 128
FF = 4 * D
ROPE_THETA = 10000.0
IN_WIDTH = 5552
PW = 5632
N_DEV = 8
LANES = 128
SLAB_BLOCK_MAX = 400
ADA_COLS = 6 * D // N_DEV
SMALL_ROWS = 32
SMALL_SOURCES = tuple([(r, 6 + r) for r in range(6)] + [(12,), (13,), (14,), (15,), (16,), (17, 18, 19, 20),
                                                         (21,), (22,), (23,), (24,)])
LOSS_SOURCE = (25,)
VMEM_LIMIT = 56 * 1024 * 1024

ADAM_LR, ADAM_B1, ADAM_B2, ADAM_EPS, ADAM_WD, ADAM_STEP = 0.001, 0.9, 0.999, 1e-08, 0.01, 10

SLAB_A = (("w_in", 694, "T"), ("gla_w_alpha", 1, "TR"), (None, 9, None), ("mla_w_uq", 48, "TR"), ("mla_w_ukv", 32, "TR"))
SLAB_B = (("mlp_w1", 512, "T"), ("gla_w_o", 128, "N"), ("mla_w_o", 128, "N"), ("w_out", 128, "N"), ("mlp_w2", 512, "N"))
BIG = ("w_ada",) + tuple(n for n, _, _ in SLAB_A + SLAB_B if n is not None)
SHARD_SHAPES = {"w_ada": (D, 6 * D // N_DEV), "w_in": (D, IN_WIDTH // N_DEV), "gla_w_alpha": (GLR, GH * GDK // N_DEV),
                "gla_w_o": (GH * GDV // N_DEV, D), "mla_w_uq": (MQR, MH * MQK // N_DEV),
                "mla_w_ukv": (MKVR, MH * (NOPE + MV) // N_DEV), "mla_w_o": (MH * MV // N_DEV, D), "w_out": (D // N_DEV, D),
                "mlp_w1": (D, FF // N_DEV), "mlp_w2": (FF // N_DEV, D)}
SMALL = ("b_ada", "norm1_g", "norm2_g", "b_merge", "gla_b_alpha", "gla_out_norm_g", "mla_q_lat_g", "mla_kv_lat_g",
         "mla_qn_g", "mla_kn_g")
SMALL_ROW = {"b_ada": 0, "norm1_g": 6, "norm2_g": 7, "b_merge": 8, "gla_b_alpha": 10, "gla_out_norm_g": 11,
             "mla_q_lat_g": 12, "mla_kv_lat_g": 13, "mla_qn_g": 14, "mla_kn_g": 15}
WEIGHTS = ("w_ada", "b_ada", "norm1_g", "w_in", "b_merge", "gla_w_alpha", "gla_b_alpha", "gla_out_norm_g", "gla_w_o",
           "mla_q_lat_g", "mla_w_uq", "mla_kv_lat_g", "mla_w_ukv", "mla_qn_g", "mla_kn_g", "mla_w_o", "w_out",
           "norm2_g", "mlp_w1", "mlp_w2")


def _cparams(sem=None):
    return pltpu.CompilerParams(dimension_semantics=sem, vmem_limit_bytes=VMEM_LIMIT)


def _tile(n, pref):
    for t in (2048, PW // 4, 1024, 512, 256, 128):
        if t <= pref and n % t == 0:
            return t
    return n


def _dot(a, b, dims, precision=None):
    return lax.dot_general(a, b, (dims, ((), ())), preferred_element_type=F32, precision=precision)


NN = ((1,), (0,))
NT = ((1,), (1,))
TN = ((0,), (0,))


def _sigmoid(x):
    return 1.0 / (1.0 + jnp.exp(-x))


def _silu(x):
    return x * _sigmoid(x)


def _mm(a, b, mode, out_dtypes, name, *, pro=None, pro_b=None, epi=None, extras=(), a_off=0, m=None, tm=2048, tn=1024,
        tk=1024, cargo=None):
    if mode == "tn":
        kc, n = b.shape
        m = a.shape[1] if m is None else m
    elif mode == "nn":
        m, kc = a.shape
        n = b.shape[1]
    else:
        m, kc = a.shape
        n = b.shape[0]
    tm, tn, tk = _tile(m, tm), _tile(n, tn), _tile(kc, tk)
    nk = kc // tk
    dims = {"nn": NN, "nt": NT, "tn": TN}[mode]
    if mode == "tn":
        a_spec = pl.BlockSpec((tk, tm), lambda i, j, k: (k, i + a_off))
    else:
        a_spec = pl.BlockSpec((tm, tk), lambda i, j, k: (i + a_off, k))
    if mode == "nt":
        b_spec = pl.BlockSpec((tn, tk), lambda i, j, k: (j, k))
    else:
        b_spec = pl.BlockSpec((tk, tn), lambda i, j, k: (k, j))
    o_spec = pl.BlockSpec((tm, tn), lambda i, j, k: (i, j))
    n_ex, n_out = len(extras), len(out_dtypes)
    grid = (m // tm, n // tn, nk)
    has_cargo = cargo is not None

    def body(a_ref, b_ref, *rest):
        ex, rest = rest[:n_ex], rest[n_ex:]
        if has_cargo:
            outs, acc = rest[1:1 + n_out], rest[2 + n_out]
            exchange = _chip_exchange_copies(rest[0], rest[1 + n_out], *rest[3 + n_out:])
            steps = [pl.program_id(axis) for axis in range(3)]
            first = (steps[0] == 0) & (steps[1] == 0) & (steps[2] == 0)
            last = (steps[0] == grid[0] - 1) & (steps[1] == grid[1] - 1) & (steps[2] == grid[2] - 1)
            pl.when(first)(exchange.start)
        else:
            outs, acc = rest[:n_out], rest[n_out]
        k = pl.program_id(2)

        @pl.when(k == 0)
        def _():
            acc[...] = jnp.zeros_like(acc)

        av = a_ref[...]
        if pro is not None:
            av = pro(av)
        bv = b_ref[...]
        if pro_b is not None:
            bv = pro_b(bv)
        acc[...] += _dot(av.astype(BF16), bv.astype(BF16), dims)

        @pl.when(k == nk - 1)
        def _():
            res = (acc[...],) if epi is None else epi(acc[...], *[e[...] for e in ex])
            for o_ref, r in zip(outs, res):
                o_ref[...] = r.astype(o_ref.dtype)

        if has_cargo:
            pl.when(last)(exchange.finish)

    cargo_in = [cargo] if has_cargo else []
    cargo_spec = [HBM_SPEC] * len(cargo_in)
    out = pl.pallas_call(
        body, name=name, grid=grid,
        in_specs=[a_spec, b_spec] + [o_spec] * n_ex + cargo_spec,
        out_specs=[o_spec] * n_out + cargo_spec,
        out_shape=[jax.ShapeDtypeStruct((m, n), dt) for dt in out_dtypes]
        + [jax.ShapeDtypeStruct(c.shape, c.dtype) for c in cargo_in],
        scratch_shapes=[pltpu.VMEM((tm, tn), F32)] + (EXCHANGE_SEMS if has_cargo else []),
        compiler_params=_cparams(("arbitrary",) * 3 if has_cargo else ("parallel", "parallel", "arbitrary")),
    )(a, b, *extras, *cargo_in)
    return out[0] if len(out) == 1 else out


def _rows(s):
    return _tile(s, 512)


def _mod_spec():
    return pl.BlockSpec((1, 6, D), lambda b, i: (b, 0, 0))


def _tok_spec(tr, nb, width=D, col=0):
    return pl.BlockSpec((tr, width), lambda b, i: (b * nb + i, col))


def _norm_mod_fwd(x, g, mod3, i_shift, i_scale, name, mixed=None, i_gate=None):
    bsz, _, _ = mod3.shape
    t = x.shape[0]
    s = t // bsz
    tr = _rows(s)
    nb = s // tr
    has_res = mixed is not None

    def body(*refs):
        if has_res:
            x_ref, mx_ref, g_ref, mod_ref, x1_ref, h_ref = refs
            xv = x_ref[...] + mod_ref[0, i_gate:i_gate + 1, :] * mx_ref[...]
            x1_ref[...] = xv
        else:
            x_ref, g_ref, mod_ref, h_ref = refs
            xv = x_ref[...]
        r = lax.rsqrt(jnp.mean(xv * xv, axis=1, keepdims=True) + EPS)
        hn = (xv * r) * g_ref[...]
        h = hn * (1.0 + mod_ref[0, i_scale:i_scale + 1, :]) + mod_ref[0, i_shift:i_shift + 1, :]
        h_ref[...] = h.astype(BF16)

    tok = _tok_spec(tr, nb)
    gspec = pl.BlockSpec((1, D), lambda b, i: (0, 0))
    ins = [x] + ([mixed] if has_res else []) + [g, mod3]
    in_specs = [tok] + ([tok] if has_res else []) + [gspec, _mod_spec()]
    out_shape = ([jax.ShapeDtypeStruct((t, D), F32)] if has_res else []) + [jax.ShapeDtypeStruct((t, D), BF16)]
    out = pl.pallas_call(
        body, name=name, grid=(bsz, nb), in_specs=in_specs, out_specs=[tok] * len(out_shape), out_shape=out_shape,
        compiler_params=_cparams(("arbitrary", "arbitrary")),
    )(*ins)
    return (out[0], out[1]) if has_res else (None, out[0])


def _norm_mod_bwd(x, dh, dres, g, mod3, i_shift, i_scale, name, mixed=None, i_gate=None):
    bsz = mod3.shape[0]
    t = x.shape[0]
    s = t // bsz
    tr = _rows(s)
    nb = s // tr
    has_res = mixed is not None

    def body(*refs):
        if has_res:
            x_ref, dh_ref, dres_ref, mx_ref, g_ref, mod_ref, dx_ref, dmx_ref, accb, accg = refs
        else:
            x_ref, dh_ref, dres_ref, g_ref, mod_ref, dx_ref, accb, accg = refs
        b, i = pl.program_id(0), pl.program_id(1)

        @pl.when(i == 0)
        def _():
            accb[...] = jnp.zeros_like(accb)

        @pl.when((i == 0) & (b == 0))
        def _():
            accg[...] = jnp.zeros_like(accg)

        xv, dhv, gv = x_ref[...], dh_ref[...], g_ref[...]
        r = lax.rsqrt(jnp.mean(xv * xv, axis=1, keepdims=True) + EPS)
        xn = xv * r
        accb[0, 0:1, :] += jnp.sum(dhv, axis=0, keepdims=True)
        accb[0, 1:2, :] += jnp.sum(dhv * (xn * gv), axis=0, keepdims=True)
        tt = dhv * (1.0 + mod_ref[0, i_scale:i_scale + 1, :])
        accg[0:1, :] += jnp.sum(tt * xn, axis=0, keepdims=True)
        dxn = tt * gv
        dx = dres_ref[...] + r * (dxn - xn * jnp.mean(dxn * xn, axis=1, keepdims=True))
        dx_ref[...] = dx
        if has_res:
            accb[0, 2:3, :] += jnp.sum(dx * mx_ref[...], axis=0, keepdims=True)
            dmx_ref[...] = (dx * mod_ref[0, i_gate:i_gate + 1, :]).astype(BF16)

    tok = _tok_spec(tr, nb)
    gspec = pl.BlockSpec((1, D), lambda b, i: (0, 0))
    ins = [x, dh, dres] + ([mixed] if has_res else []) + [g, mod3]
    in_specs = [tok] * (4 if has_res else 3) + [gspec, _mod_spec()]
    out_shape = [jax.ShapeDtypeStruct((t, D), F32)] + ([jax.ShapeDtypeStruct((t, D), BF16)] if has_res else [])
    out_specs = [tok] * len(out_shape)
    out_shape += [jax.ShapeDtypeStruct((bsz, 8, D), F32), jax.ShapeDtypeStruct((8, D), F32)]
    out_specs += [pl.BlockSpec((1, 8, D), lambda b, i: (b, 0, 0)), pl.BlockSpec((8, D), lambda b, i: (0, 0))]
    return pl.pallas_call(
        body, name=name, grid=(bsz, nb), in_specs=in_specs, out_specs=out_specs, out_shape=out_shape,
        compiler_params=_cparams(("arbitrary", "arbitrary")),
    )(*ins)


def _loss_head(x1, ff, tgt, mod3):
    bsz = mod3.shape[0]
    t = x1.shape[0]
    s = t // bsz
    tr = _rows(s)
    nb = s // tr

    def body(x1_ref, ff_ref, tg_ref, mod_ref, dy_ref, dff_ref, accb, accl):
        b, i = pl.program_id(0), pl.program_id(1)

        @pl.when(i == 0)
        def _():
            accb[...] = jnp.zeros_like(accb)

        @pl.when((i == 0) & (b == 0))
        def _():
            accl[...] = jnp.zeros_like(accl)

        gate = mod_ref[0, 5:6, :]
        ffv = ff_ref[...]
        err = x1_ref[...] + gate * ffv - tg_ref[...]
        accl[0:1, :] += jnp.sum(err * err, axis=0, keepdims=True) * (0.5 / D)
        dy = err * (1.0 / D)
        dy_ref[...] = dy
        dff_ref[...] = (dy * gate).astype(BF16)
        accb[0, 0:1, :] += jnp.sum(dy * ffv, axis=0, keepdims=True)

    tok = _tok_spec(tr, nb)
    return pl.pallas_call(
        body, name="loss_head", grid=(bsz, nb), in_specs=[tok, tok, tok, _mod_spec()],
        out_specs=[tok, tok, pl.BlockSpec((1, 8, D), lambda b, i: (b, 0, 0)), pl.BlockSpec((8, D), lambda b, i: (0, 0))],
        out_shape=[jax.ShapeDtypeStruct((t, D), F32), jax.ShapeDtypeStruct((t, D), BF16),
                   jax.ShapeDtypeStruct((bsz, 8, D), F32), jax.ShapeDtypeStruct((8, D), F32)],
        compiler_params=_cparams(("arbitrary", "arbitrary")),
    )(x1, ff, tgt, mod3)


def _merge_fwd(proj, b_merge, y_a, y_b):
    t = proj.shape[0]
    tr = _tile(t, 512)

    def body(la_ref, lb_ref, bm_ref, ya_ref, yb_ref, mix_ref):
        ga = _sigmoid(la_ref[...] + bm_ref[:, 0:D])
        gb = _sigmoid(lb_ref[...] + bm_ref[:, D:2 * D])
        mix_ref[...] = (ga * ya_ref[...].astype(F32) + gb * yb_ref[...].astype(F32)).astype(BF16)

    tok = pl.BlockSpec((tr, D), lambda i: (i, 0))
    return pl.pallas_call(
        body, name="merge_fwd", grid=(t // tr,),
        in_specs=[pl.BlockSpec((tr, D), lambda i: (i, 3)), pl.BlockSpec((tr, D), lambda i: (i, 4)),
                  pl.BlockSpec((1, 2 * D), lambda i: (0, 0)), tok, tok],
        out_specs=tok, out_shape=jax.ShapeDtypeStruct((t, D), BF16),
        compiler_params=_cparams(("arbitrary",)),
    )(proj, proj, b_merge, y_a, y_b)


def _merge_bwd(dmix, proj, b_merge, y_a, y_b):
    t = proj.shape[0]
    tr = _tile(t, 512)

    def body(dm_ref, la_ref, lb_ref, bm_ref, ya_ref, yb_ref, dya_ref, dyb_ref, dl_ref, acc):
        @pl.when(pl.program_id(0) == 0)
        def _():
            acc[...] = jnp.zeros_like(acc)

        dm = dm_ref[...].astype(F32)
        ga = _sigmoid(la_ref[...] + bm_ref[:, 0:D])
        gb = _sigmoid(lb_ref[...] + bm_ref[:, D:2 * D])
        dya_ref[...] = (dm * ga).astype(BF16)
        dyb_ref[...] = (dm * gb).astype(BF16)
        dla = dm * ya_ref[...].astype(F32) * ga * (1.0 - ga)
        dlb = dm * yb_ref[...].astype(F32) * gb * (1.0 - gb)
        dl_ref[:, 0:D] = dla.astype(BF16)
        dl_ref[:, D:2 * D] = dlb.astype(BF16)
        acc[0:1, 0:D] += jnp.sum(dla, axis=0, keepdims=True)
        acc[0:1, D:2 * D] += jnp.sum(dlb, axis=0, keepdims=True)

    tok = pl.BlockSpec((tr, D), lambda i: (i, 0))
    return pl.pallas_call(
        body, name="merge_bwd", grid=(t // tr,),
        in_specs=[tok, pl.BlockSpec((tr, D), lambda i: (i, 3)), pl.BlockSpec((tr, D), lambda i: (i, 4)),
                  pl.BlockSpec((1, 2 * D), lambda i: (0, 0)), tok, tok],
        out_specs=[tok, tok, pl.BlockSpec((tr, 2 * D), lambda i: (i, 0)), pl.BlockSpec((8, 2 * D), lambda i: (0, 0))],
        out_shape=[jax.ShapeDtypeStruct((t, D), BF16), jax.ShapeDtypeStruct((t, D), BF16),
                   jax.ShapeDtypeStruct((t, 2 * D), BF16), jax.ShapeDtypeStruct((8, 2 * D), F32)],
        compiler_params=_cparams(("arbitrary",)),
    )(dmix, proj, proj, b_merge, y_a, y_b)


GLA_HEADS = 2
GLA_UNROLL = 8


def _log_sigmoid(z):
    return jnp.minimum(z, 0.0) - jnp.log(1.0 + jnp.exp(-jnp.abs(z)))


def _tri(lower):
    r = lax.broadcasted_iota(jnp.int32, (CHUNK, CHUNK), 0)
    c = lax.broadcasted_iota(jnp.int32, (CHUNK, CHUNK), 1)
    return jnp.where(r >= c if lower else r <= c, 1.0, 0.0).astype(F32)


def _gla_fwd(proj, wa_pad, b_alpha, g_out, bsz):
    t = proj.shape[0]
    s = t // bsz
    nc = s // CHUNK
    p, kw, vw = GLA_HEADS, GLA_HEADS * GDK, GLA_HEADS * GDV

    def body(q_ref, k_ref, v_ref, gg_ref, ms_ref, wa_ref, ba_ref, go_ref, o_ref, og_ref, st_ref, la, state):
        z = _dot(ms_ref[...].astype(BF16), wa_ref[...], NN) + ba_ref[...]
        la[...] = _log_sigmoid(z) * (1.0 / GTAU)
        state[...] = jnp.zeros_like(state)
        low = _tri(True)
        gout = go_ref[...]

        def chunk(n, carry):
            rows = pl.ds(pl.multiple_of(n * CHUNK, CHUNK), CHUNK)
            for hh in range(p):
                kc, vc = slice(hh * GDK, (hh + 1) * GDK), slice(hh * GDV, (hh + 1) * GDV)
                lac = la[rows, kc]
                cum = _dot(low, lac, NN, lax.Precision.HIGHEST)
                ce = jnp.sum(lac, axis=0, keepdims=True)
                kd = (k_ref[rows, kc].astype(F32) * jnp.exp(ce - cum)).astype(BF16)
                new = state[vc, :] * jnp.exp(ce) + _dot(v_ref[rows, vc].astype(BF16), kd, TN)
                state[vc, :] = new
                st_ref[pl.ds(pl.multiple_of((hh * nc + n) * GDV, GDV), GDV), :] = new.astype(BF16)
                qs = (q_ref[rows, kc].astype(F32) * (GDK ** -0.5)).astype(BF16)
                o = _dot(qs, new.astype(BF16), NT)
                o_ref[rows, vc] = o
                ro = lax.rsqrt(jnp.mean(o * o, axis=1, keepdims=True) + EPS)
                og_ref[rows, vc] = (((o * ro) * gout) * _silu(gg_ref[rows, vc].astype(F32))).astype(BF16)
            return carry

        lax.fori_loop(0, nc, chunk, 0, unroll=GLA_UNROLL)

    return pl.pallas_call(
        body, name="gla_fwd", grid=(bsz, GH // p),
        in_specs=[pl.BlockSpec((s, kw), lambda b, h: (b, h)), pl.BlockSpec((s, kw), lambda b, h: (b, GH // p + h)),
                  pl.BlockSpec((s, vw), lambda b, h: (b, GH // p + h)), pl.BlockSpec((s, vw), lambda b, h: (b, 2 * GH // p + h)),
                  pl.BlockSpec((s, LANES), lambda b, h: (b, 43)),
                  pl.BlockSpec((LANES, kw), lambda b, h: (0, h)), pl.BlockSpec((1, kw), lambda b, h: (0, h)),
                  pl.BlockSpec((1, GDV), lambda b, h: (0, 0))],
        out_specs=[pl.BlockSpec((s, vw), lambda b, h: (b, h)), pl.BlockSpec((s, vw), lambda b, h: (b, h)),
                   pl.BlockSpec((p * nc * GDV, GDK), lambda b, h: (b * (GH // p) + h, 0))],
        out_shape=[jax.ShapeDtypeStruct((t, GH * GDV), F32), jax.ShapeDtypeStruct((t, GH * GDV), BF16),
                   jax.ShapeDtypeStruct((bsz * GH * nc * GDV, GDK), BF16)],
        scratch_shapes=[pltpu.VMEM((s, kw), F32), pltpu.VMEM((vw, GDK), F32)],
        compiler_params=_cparams(("arbitrary", "arbitrary")),
    )(proj, proj, proj, proj, proj, wa_pad, b_alpha, g_out)


def _gla_bwd(dog, o, proj, wa_pad, b_alpha, g_out, states, bsz):
    t = proj.shape[0]
    s = t // bsz
    nc = s // CHUNK
    p, kw, vw = GLA_HEADS, GLA_HEADS * GDK, GLA_HEADS * GDV

    def body(dog_ref, o_ref, q_ref, k_ref, v_ref, gg_ref, ms_ref, wa_ref, ba_ref, go_ref, st_ref,
             dq_ref, dk_ref, dv_ref, dgg_ref, dz_ref, dba, dgo, zs, la, carry_g):
        @pl.when(pl.program_id(1) == 0)
        def _():
            dba[...] = jnp.zeros_like(dba)
            dgo[...] = jnp.zeros_like(dgo)

        z = _dot(ms_ref[...].astype(BF16), wa_ref[...], NN) + ba_ref[...]
        zs[...] = z
        la[...] = _log_sigmoid(z) * (1.0 / GTAU)
        carry_g[...] = jnp.zeros_like(carry_g)
        low, upp = _tri(True), _tri(False)
        gout = go_ref[...]
        last_row = lax.broadcasted_iota(jnp.int32, (CHUNK, GDK), 0) == CHUNK - 1

        def chunk(step, carry):
            n = nc - 1 - step
            rows = pl.ds(pl.multiple_of(n * CHUNK, CHUNK), CHUNK)
            for hh in range(p):
                kc, vc = slice(hh * GDK, (hh + 1) * GDK), slice(hh * GDV, (hh + 1) * GDV)
                lac = la[rows, kc]
                cum = _dot(low, lac, NN, lax.Precision.HIGHEST)
                ce = jnp.sum(lac, axis=0, keepdims=True)
                e = jnp.exp(ce - cum)
                dec = jnp.exp(ce)
                kf = k_ref[rows, kc].astype(F32)
                kd = (kf * e).astype(BF16)
                vv = v_ref[rows, vc].astype(BF16)
                qs = (q_ref[rows, kc].astype(F32) * (GDK ** -0.5)).astype(BF16)
                ov = o_ref[rows, vc]
                ro = lax.rsqrt(jnp.mean(ov * ov, axis=1, keepdims=True) + EPS)
                on = ov * ro
                gg = gg_ref[rows, vc].astype(F32)
                sg = _sigmoid(gg)
                dogv = dog_ref[rows, vc].astype(F32)
                dgg_ref[rows, vc] = (dogv * (on * gout) * (sg * (1.0 + gg * (1.0 - sg)))).astype(BF16)
                t1 = dogv * (gg * sg)
                dgo[8 * hh:8 * hh + 1, :] += jnp.sum(t1 * on, axis=0, keepdims=True)
                don = t1 * gout
                do = ro * (don - on * jnp.mean(don * on, axis=1, keepdims=True))
                dob = do.astype(BF16)
                st_n = st_ref[pl.ds(pl.multiple_of((hh * nc + n) * GDV, GDV), GDV), :]
                dq_ref[rows, kc] = (_dot(dob, st_n, NN) * (GDK ** -0.5)).astype(BF16)
                dn = carry_g[vc, :] + _dot(dob, qs, TN)
                prev = hh * nc + jnp.maximum(n - 1, 0)
                st_p = st_ref[pl.ds(pl.multiple_of(prev * GDV, GDV), GDV), :].astype(F32) * jnp.where(n > 0, 1.0, 0.0)
                ddec = jnp.sum(dn * st_p, axis=0, keepdims=True)
                dnb = dn.astype(BF16)
                dkd = _dot(vv, dnb, NN)
                dv_ref[rows, vc] = _dot(kd, dnb, NT).astype(BF16)
                dk_ref[rows, kc] = (dkd * e).astype(BF16)
                w = dkd * kf * e
                dce = jnp.sum(w, axis=0, keepdims=True) + ddec * dec
                dcum = jnp.where(last_row, dce - w, -w)
                dla = _dot(upp, dcum, NN, lax.Precision.HIGHEST)
                dz = dla * (1.0 / GTAU) * _sigmoid(-zs[rows, kc])
                dba[0:1, kc] += jnp.sum(dz, axis=0, keepdims=True)
                dz_ref[rows, kc] = dz.astype(BF16)
                carry_g[vc, :] = dn * dec
            return carry

        lax.fori_loop(0, nc, chunk, 0, unroll=GLA_UNROLL)

    hv = pl.BlockSpec((s, vw), lambda h, b: (b, h))
    hk = pl.BlockSpec((s, kw), lambda h, b: (b, h))
    return pl.pallas_call(
        body, name="gla_bwd", grid=(GH // p, bsz),
        in_specs=[hv, hv, hk, pl.BlockSpec((s, kw), lambda h, b: (b, GH // p + h)),
                  pl.BlockSpec((s, vw), lambda h, b: (b, GH // p + h)), pl.BlockSpec((s, vw), lambda h, b: (b, 2 * GH // p + h)),
                  pl.BlockSpec((s, LANES), lambda h, b: (b, 43)), pl.BlockSpec((LANES, kw), lambda h, b: (0, h)),
                  pl.BlockSpec((1, kw), lambda h, b: (0, h)), pl.BlockSpec((1, GDV), lambda h, b: (0, 0)),
                  pl.BlockSpec((p * nc * GDV, GDK), lambda h, b: (b * (GH // p) + h, 0))],
        out_specs=[hk, hk, hv, hv, hk, pl.BlockSpec((8, kw), lambda h, b: (0, h)),
                   pl.BlockSpec((8 * p, GDV), lambda h, b: (h, 0))],
        out_shape=[jax.ShapeDtypeStruct((t, GH * GDK), BF16), jax.ShapeDtypeStruct((t, GH * GDK), BF16),
                   jax.ShapeDtypeStruct((t, GH * GDV), BF16), jax.ShapeDtypeStruct((t, GH * GDV), BF16),
                   jax.ShapeDtypeStruct((t, GH * GDK), BF16), jax.ShapeDtypeStruct((8, GH * GDK), F32),
                   jax.ShapeDtypeStruct((8 * GH, GDV), F32)],
        scratch_shapes=[pltpu.VMEM((s, kw), F32), pltpu.VMEM((s, kw), F32), pltpu.VMEM((vw, GDK), F32)],
        compiler_params=_cparams(("arbitrary", "arbitrary")),
    )(dog, o, proj, proj, proj, proj, proj, wa_pad, b_alpha, g_out, states)


def _rope_tables(pos_ref, fr_ref, sg_ref):
    ang = pos_ref[...].astype(F32) * fr_ref[...]
    return jnp.cos(ang), jnp.sin(ang) * sg_ref[...]


def _partner(x):
    lane = lax.broadcasted_iota(jnp.int32, x.shape, 1)
    return jnp.where(lane < NOPE + ROPE // 2, pltpu.roll(x, LANES - ROPE // 2, 1), pltpu.roll(x, ROPE // 2, 1))


def _mla_rows(t):
    return _tile(t, 512)


def _mla_pre_fwd(proj, pos, fr, sg, q_lat_g, kv_lat_g, qn_g, kn_g, wuq, wukv):
    t = proj.shape[0]
    tr = _mla_rows(t)

    def body(cq_ref, ckv_ref, ms_ref, pos_ref, fr_ref, sg_ref, qlg, kvlg, qng, kng, wuq_ref, wukv_ref, q_out, k_out, v_out):
        lane = lax.broadcasted_iota(jnp.int32, (tr, HP), 1)
        real = jnp.where(lane < MQK, 1.0, 0.0)
        cos, sin = _rope_tables(pos_ref, fr_ref, sg_ref)
        cos = cos * real
        cq = cq_ref[...].astype(F32)
        cqn = (cq * lax.rsqrt(jnp.mean(cq * cq, axis=1, keepdims=True) + EPS) * qlg[...]).astype(BF16)
        ckv = ckv_ref[...].astype(F32)
        ckvn = (ckv * lax.rsqrt(jnp.mean(ckv * ckv, axis=1, keepdims=True) + EPS) * kvlg[...]).astype(BF16)
        kpe = jnp.where((lane >= NOPE) & (lane < MQK), ms_ref[...].astype(F32), 0.0)
        kpe = kpe + jnp.where(lane < MQK + ROPE // 2, pltpu.roll(kpe, ROPE, 1), 0.0)
        lane_all = lax.broadcasted_iota(jnp.int32, (tr, MH * HP), 1)
        v_out[...] = jnp.where(lane_all % HP == MV, 1.0, _dot(ckvn, wukv_ref[:, MH * HP:], NN)).astype(BF16)

        def norm_rope(x, gain):
            xn = x * lax.rsqrt(jnp.sum(x * x * real, axis=1, keepdims=True) * (1.0 / MQK) + EPS) * gain
            return (xn * cos + pltpu.roll(xn, LANES - ROPE // 2, 1) * sin).astype(BF16)

        for h in range(MH):
            cols = slice(h * HP, (h + 1) * HP)
            q_out[:, cols] = norm_rope(_dot(cqn, wuq_ref[:, cols], NN), qng[...])
            k_out[:, cols] = norm_rope(_dot(ckvn, wukv_ref[:, cols], NN) + kpe, kng[...])

    def full(a):
        return pl.BlockSpec(a.shape, lambda i: (0, 0))

    wide = pl.BlockSpec((tr, MH * HP), lambda i: (i, 0))
    return pl.pallas_call(
        body, name="mla_pre_fwd", grid=(t // tr,),
        in_specs=[pl.BlockSpec((tr, MQR), lambda i: (i, 20)), pl.BlockSpec((tr, MKVR), lambda i: (i, 42)),
                  pl.BlockSpec((tr, LANES), lambda i: (i, 43)), pl.BlockSpec((tr, 1), lambda i: (i, 0)),
                  full(fr), full(sg), full(q_lat_g), full(kv_lat_g), full(qn_g), full(kn_g), full(wuq), full(wukv)],
        out_specs=[wide, wide, wide],
        out_shape=[jax.ShapeDtypeStruct((t, MH * HP), BF16)] * 3,
        compiler_params=_cparams(("arbitrary",)),
    )(proj, proj, proj, pos, fr, sg, q_lat_g, kv_lat_g, qn_g, kn_g, wuq, wukv)


def _mla_pre_bwd(dq2, dk2, dv2, dmisc_gla, proj, pos, fr, sg, q_lat_g, kv_lat_g, qn_g, kn_g, wuq, wukv):
    t = proj.shape[0]
    tr = _mla_rows(t)

    def body(dq_ref, dk_ref, dv_ref, dmg_ref, cq_ref, ckv_ref, ms_ref, pos_ref, fr_ref, sg_ref, qlg, kvlg, qng, kng,
             wuq_ref, wukv_ref, dcq_ref, dckv_ref, dms_ref, dwuq, dwukv, acc, dqf, dkvf):
        @pl.when(pl.program_id(0) == 0)
        def _():
            dwuq[...] = jnp.zeros_like(dwuq)
            dwukv[...] = jnp.zeros_like(dwukv)
            acc[...] = jnp.zeros_like(acc)

        cos, sin = _rope_tables(pos_ref, fr_ref, sg_ref)
        cq = cq_ref[...].astype(F32)
        rc = lax.rsqrt(jnp.mean(cq * cq, axis=1, keepdims=True) + EPS)
        xc = cq * rc
        cqn = (xc * qlg[...]).astype(BF16)
        ckv = ckv_ref[...].astype(F32)
        rkv = lax.rsqrt(jnp.mean(ckv * ckv, axis=1, keepdims=True) + EPS)
        xkv = ckv * rkv
        ckvn = (xkv * kvlg[...]).astype(BF16)
        lane = lax.broadcasted_iota(jnp.int32, (tr, HP), 1)
        is_rope = (lane >= NOPE) & (lane < MQK)
        kpe = jnp.where(is_rope, ms_ref[...].astype(F32), 0.0)
        dkpe = jnp.zeros((tr, HP), F32)
        dqng = jnp.zeros((1, HP), F32)
        dkng = jnp.zeros((1, HP), F32)
        for h in range(MH):
            cols = slice(h * HP, (h + 1) * HP)
            qh = _dot(cqn, wuq_ref[:, cols], NN)
            rq = lax.rsqrt(jnp.sum(qh * qh, axis=1, keepdims=True) * (1.0 / MQK) + EPS)
            xq = qh * rq
            dy = dq_ref[:, cols].astype(F32)
            dqn = dy * cos - _partner(dy) * sin
            dqng += jnp.sum(dqn * xq, axis=0, keepdims=True)
            tq = dqn * qng[...]
            dqf[:, cols] = (rq * (tq - xq * (jnp.sum(tq * xq, axis=1, keepdims=True) * (1.0 / MQK)))).astype(BF16)
            kh = _dot(ckvn, wukv_ref[:, cols], NN) + kpe
            rk = lax.rsqrt(jnp.sum(kh * kh, axis=1, keepdims=True) * (1.0 / MQK) + EPS)
            xk = kh * rk
            dy = dk_ref[:, cols].astype(F32)
            dkn = dy * cos - _partner(dy) * sin
            dkng += jnp.sum(dkn * xk, axis=0, keepdims=True)
            tk = dkn * kng[...]
            dkh = rk * (tk - xk * (jnp.sum(tk * xk, axis=1, keepdims=True) * (1.0 / MQK)))
            dkvf[:, cols] = jnp.where(lane < NOPE, dkh, 0.0).astype(BF16)
            dkpe += jnp.where(is_rope, dkh, 0.0)
        dkvf[:, MH * HP:] = dv_ref[...]
        acc[2:3, 0:HP] += dqng
        acc[3:4, 0:HP] += dkng
        dms_ref[...] = (dmg_ref[...] + dkpe).astype(BF16)

        dqfv = dqf[...]
        dwuq[...] += _dot(cqn, dqfv, TN)
        dcqn = _dot(dqfv, wuq_ref[...], NT)
        acc[0:1, :] += jnp.sum(dcqn * xc, axis=0, keepdims=True)
        tc = dcqn * qlg[...]
        dcq_ref[...] = (rc * (tc - xc * jnp.mean(tc * xc, axis=1, keepdims=True))).astype(BF16)

        dkvfv = dkvf[...]
        dwukv[...] += _dot(ckvn, dkvfv, TN)
        dckvn = _dot(dkvfv, wukv_ref[...], NT)
        acc[1:2, 0:MKVR] += jnp.sum(dckvn * xkv, axis=0, keepdims=True)
        tkv = dckvn * kvlg[...]
        dckv_ref[...] = (rkv * (tkv - xkv * jnp.mean(tkv * xkv, axis=1, keepdims=True))).astype(BF16)

    def full(a):
        return pl.BlockSpec(a.shape, lambda i: (0, 0))

    wide = pl.BlockSpec((tr, MH * HP), lambda i: (i, 0))
    narrow = pl.BlockSpec((tr, LANES), lambda i: (i, 0))
    return pl.pallas_call(
        body, name="mla_pre_bwd", grid=(t // tr,),
        in_specs=[wide, wide, wide, narrow,
                  pl.BlockSpec((tr, MQR), lambda i: (i, 20)), pl.BlockSpec((tr, MKVR), lambda i: (i, 42)),
                  pl.BlockSpec((tr, LANES), lambda i: (i, 43)), pl.BlockSpec((tr, 1), lambda i: (i, 0)),
                  full(fr), full(sg), full(q_lat_g), full(kv_lat_g), full(qn_g), full(kn_g), full(wuq), full(wukv)],
        out_specs=[pl.BlockSpec((tr, MQR), lambda i: (i, 0)), narrow, narrow,
                   pl.BlockSpec((MQR, MH * HP), lambda i: (0, 0)), pl.BlockSpec((MKVR, 2 * MH * HP), lambda i: (0, 0)),
                   pl.BlockSpec((8, MQR), lambda i: (0, 0))],
        out_shape=[jax.ShapeDtypeStruct((t, MQR), BF16), jax.ShapeDtypeStruct((t, MKVR), BF16),
                   jax.ShapeDtypeStruct((t, LANES), BF16), jax.ShapeDtypeStruct((MQR, MH * HP), F32),
                   jax.ShapeDtypeStruct((MKVR, 2 * MH * HP), F32), jax.ShapeDtypeStruct((8, MQR), F32)],
        scratch_shapes=[pltpu.VMEM((tr, MH * HP), BF16), pltpu.VMEM((tr, 2 * MH * HP), BF16)],
        compiler_params=_cparams(("arbitrary",)),
    )(dq2, dk2, dv2, dmisc_gla, proj, proj, proj, pos, fr, sg, q_lat_g, kv_lat_g, qn_g, kn_g, wuq, wukv)


ATT_FWD_TILES = (1024, 512)
ATT_BWD_TILES = (512, 512)
ATT_HEADS = 2
NEG = -1e30
LOG2E = 1.4426950408889634


def _att_mask(q0, k0, tq, tk):
    qc = (q0 + lax.broadcasted_iota(jnp.int32, (tq, tk), 0)) // CHUNK
    kc = (k0 + lax.broadcasted_iota(jnp.int32, (tq, tk), 1)) // CHUNK
    return kc <= qc


def _att_tiles(s, tiles):
    return _tile(s, tiles[0]), _tile(s, tiles[1])


def _lanes(x, n):
    return x if n == 1 else jnp.concatenate([x] * n, axis=1)


def _grid_ends(grid):
    i, j = pl.program_id(0), pl.program_id(1)
    return (i == 0) & (j == 0), (i == grid[0] - 1) & (j == grid[1] - 1)


def _attn_fwd(q2, k2, v2, bsz, slab):
    t = q2.shape[0]
    s = t // bsz
    tq, tk = _att_tiles(s, ATT_FWD_TILES)
    nq, groups, n_diag = s // tq, tk // HP, max(tq // tk, 1)
    sub_rows = tq // n_diag
    scale = MQK ** -0.5
    c2 = scale * LOG2E
    heads = range(ATT_HEADS)

    def body(q_ref, k_ref, v_ref, slab_ref, o_ref, lse_ref, gath_ref, send_sems, recv_sems, local_sem):
        gather = _core_row_gather_copies(slab_ref, gath_ref, send_sems, recv_sems, local_sem)
        first, last = _grid_ends((bsz, MH // ATT_HEADS))
        pl.when(first)(gather.start)

        def q_loop(qi, carry):
            q0 = pl.multiple_of(qi * tq, tq)
            rows = pl.ds(q0, tq)
            n_full = q0 // tk
            qs = [q_ref[rows, h * HP:(h + 1) * HP] for h in heads]

            def scores(h, kj, sub=None, masked=False):
                k0 = pl.multiple_of(kj * tk, tk)
                qv = qs[h] if sub is None else qs[h][sub * sub_rows:(sub + 1) * sub_rows]
                sc = _dot(qv, k_ref[pl.ds(k0, tk), h * HP:(h + 1) * HP], NT)
                return jnp.where(_att_mask(q0 + sub * sub_rows, k0, sub_rows, tk), sc, NEG) if masked else sc

            def fold(mx, sc):
                for j in range(groups):
                    mx = jnp.maximum(mx, sc[:, j * HP:(j + 1) * HP])
                return mx

            def over_diagonal(vals, step):
                out = []
                for h in heads:
                    blocks = []
                    for r in range(n_diag):
                        v = vals[h][r * sub_rows:(r + 1) * sub_rows]
                        for u in range(r + 1):
                            v = step(v, h, n_full + u, r, u == r)
                        blocks.append(v)
                    out.append(blocks[0] if n_diag == 1 else jnp.concatenate(blocks, axis=0))
                return tuple(out)

            mx = lax.fori_loop(0, n_full, lambda kj, mx: tuple(fold(mx[h], scores(h, kj)) for h in heads),
                               tuple(jnp.full((tq, HP), NEG, F32) for _ in heads))
            mx = over_diagonal(mx, lambda v, h, kj, r, masked: fold(v, scores(h, kj, r, masked)))
            mb = [jnp.broadcast_to(jnp.max(mx[h], axis=1, keepdims=True), (tq, HP)) for h in heads]

            def weighted(h, kj, sub=None, masked=False):
                m = mb[h] if sub is None else mb[h][sub * sub_rows:(sub + 1) * sub_rows]
                p = jnp.exp2((scores(h, kj, sub, masked) - _lanes(m, groups)) * c2)
                k0 = pl.multiple_of(kj * tk, tk)
                return _dot(p.astype(BF16), v_ref[pl.ds(k0, tk), h * HP:(h + 1) * HP], NN)

            acc = lax.fori_loop(0, n_full, lambda kj, acc: tuple(acc[h] + weighted(h, kj) for h in heads),
                                tuple(jnp.zeros((tq, HP), F32) for _ in heads))
            acc = over_diagonal(acc, lambda v, h, kj, r, masked: v + weighted(h, kj, r, masked))
            lane = lax.broadcasted_iota(jnp.int32, (tq, HP), 1)
            for h in heads:
                a = acc[h]
                l = jnp.sum(jnp.where(lane == MV, a, 0.0), axis=1, keepdims=True)
                o_ref[rows, h * HP:(h + 1) * HP] = (a / l).astype(BF16)
                lse_ref[rows, h * HP:(h + 1) * HP] = mb[h] * scale + jnp.log(l)
            return carry

        lax.fori_loop(0, nq, q_loop, 0)
        pl.when(last)(gather.finish)

    spec = pl.BlockSpec((s, ATT_HEADS * HP), lambda b, h: (b, h))
    return pl.pallas_call(
        body, name="attn_fwd", grid=(bsz, MH // ATT_HEADS), in_specs=[spec] * 3 + [HBM_SPEC],
        out_specs=[spec, spec, HBM_SPEC],
        out_shape=[jax.ShapeDtypeStruct((t, MH * HP), BF16), jax.ShapeDtypeStruct((t, MH * HP), F32),
                   jax.ShapeDtypeStruct((N_DEV,) + slab.shape, slab.dtype)],
        scratch_shapes=EXCHANGE_SEMS, compiler_params=_cparams(("arbitrary", "arbitrary")),
    )(q2, k2, v2, slab)


def _attn_bwd(q2, k2, v2, do2, o2, lse2, bsz, tsum):
    t = q2.shape[0]
    s = t // bsz
    tq, tk = _att_tiles(s, ATT_BWD_TILES)
    nq, nk, per, groups = s // tq, s // tk, max(tk // tq, 1), tk // HP
    scale = MQK ** -0.5
    c2 = scale * LOG2E
    heads = range(ATT_HEADS)

    def body(q_ref, k_ref, v_ref, do_ref, o_ref, lse_ref, t_ref, dq_ref, dk_ref, dv_ref, parts_ref, dq_acc, delta, lse_b2,
             send_sems, recv_sems, local_sem):
        exchange = _all_to_all_copies(t_ref, parts_ref, send_sems, recv_sems, local_sem)
        first, last = _grid_ends((bsz, MH // ATT_HEADS))
        pl.when(first)(exchange.start)
        dq_acc[...] = jnp.zeros_like(dq_acc)

        def d_loop(i, carry):
            rows = pl.ds(pl.multiple_of(i * tq, tq), tq)
            for h in heads:
                hs = slice(h * HP, (h + 1) * HP)
                dl = jnp.sum(do_ref[rows, hs].astype(F32) * o_ref[rows, hs].astype(F32), axis=1, keepdims=True)
                delta[rows, hs] = jnp.broadcast_to(dl, (tq, HP))
            lse_b2[rows, :] = lse_ref[rows, :] * LOG2E
            return carry

        lax.fori_loop(0, nq, d_loop, 0)

        def k_loop(kj, carry):
            k0 = pl.multiple_of(kj * tk, tk)
            kk = [k_ref[pl.ds(k0, tk), h * HP:(h + 1) * HP] for h in heads]
            vv = [v_ref[pl.ds(k0, tk), h * HP:(h + 1) * HP] for h in heads]

            def tile(qi, c, masked):
                q0 = pl.multiple_of(qi * tq, tq)
                rows = pl.ds(q0, tq)
                out = []
                for h in heads:
                    hs = slice(h * HP, (h + 1) * HP)
                    dk, dv = c[h]
                    q = q_ref[rows, hs]
                    do = do_ref[rows, hs]
                    e = _dot(q, kk[h], NT) * c2 - _lanes(lse_b2[rows, hs], groups)
                    if masked:
                        e = jnp.where(_att_mask(q0, k0, tq, tk), e, NEG)
                    p = jnp.exp2(e)
                    dv = dv + _dot(p.astype(BF16), do, TN)
                    ds = (p * (_dot(do, vv[h], NT) - _lanes(delta[rows, hs], groups))).astype(BF16)
                    dq_acc[rows, hs] += _dot(ds, kk[h], NN)
                    dk = dk + _dot(ds, q, TN)
                    out.append((dk, dv))
                return tuple(out)

            zero = jnp.zeros((tk, HP), F32)
            c = tuple((zero, zero) for _ in heads)
            first = k0 // tq
            for u in range(per):
                c = tile(first + u, c, True)
            c = lax.fori_loop(first + per, nq, lambda qi, c: tile(qi, c, False), c)
            for h in heads:
                dk_ref[pl.ds(k0, tk), h * HP:(h + 1) * HP] = (c[h][0] * scale).astype(BF16)
                dv_ref[pl.ds(k0, tk), h * HP:(h + 1) * HP] = c[h][1].astype(BF16)
            return carry

        lax.fori_loop(0, nk, k_loop, 0)
        dq_ref[...] = (dq_acc[...] * scale).astype(BF16)
        pl.when(last)(exchange.finish)

    spec = pl.BlockSpec((s, ATT_HEADS * HP), lambda b, h: (b, h))
    return pl.pallas_call(
        body, name="attn_bwd", grid=(bsz, MH // ATT_HEADS), in_specs=[spec] * 6 + [HBM_SPEC],
        out_specs=[spec] * 3 + [HBM_SPEC],
        out_shape=[jax.ShapeDtypeStruct((t, MH * HP), BF16)] * 3 + [jax.ShapeDtypeStruct(tsum.shape, tsum.dtype)],
        scratch_shapes=[pltpu.VMEM((s, ATT_HEADS * HP), F32)] * 3 + EXCHANGE_SEMS,
        compiler_params=_cparams(("arbitrary", "arbitrary")),
    )(q2, k2, v2, do2, o2, lse2, tsum)


def _perm_w_in_t(w):
    z = lambda n: jnp.zeros((n, w.shape[1]), w.dtype)
    return jnp.concatenate([w[:3072], w[3504:5552], w[3088:3344], w[3344:3472], w[3072:3088], z(48), w[3472:3504], z(32)],
                           axis=0)


def _unperm_w_in_t(g):
    return jnp.concatenate([g[:3072], g[5504:5520], g[5120:5376], g[5376:5504], g[5568:5600], g[3072:5120]], axis=0)


def _pad_wa(w):
    return jnp.pad(w, ((0, LANES - GLR), (0, 0)))


def _pad_wuq(w):
    return jnp.pad(w.reshape(MQR, MH, MQK), ((0, 0), (0, 0), (0, HP - MQK))).reshape(MQR, MH * HP)


def _unpad_wuq(g):
    return g.reshape(MQR, MH, HP)[:, :, :MQK].reshape(MQR, MH * MQK)


def _pad_wukv(w):
    w3 = w.reshape(MKVR, MH, NOPE + MV)
    kp = jnp.pad(w3[:, :, :NOPE], ((0, 0), (0, 0), (0, HP - NOPE))).reshape(MKVR, MH * HP)
    vp = jnp.pad(w3[:, :, NOPE:], ((0, 0), (0, 0), (0, HP - MV))).reshape(MKVR, MH * HP)
    return jnp.concatenate([kp, vp], axis=1)


def _unpad_wukv(g):
    kp = g[:, :MH * HP].reshape(MKVR, MH, HP)[:, :, :NOPE]
    vp = g[:, MH * HP:].reshape(MKVR, MH, HP)[:, :, :MV]
    return jnp.concatenate([kp, vp], axis=2).reshape(MKVR, MH * (NOPE + MV))


def _pad_wo(w):
    return jnp.pad(w.reshape(MH, MV, D), ((0, 0), (0, HP - MV), (0, 0))).reshape(MH * HP, D)


def _unpad_wo(g):
    return g.reshape(MH, HP, D)[:, :MV, :].reshape(MH * MV, D)


def _repeat_half(a):
    a3 = a.reshape(a.shape[0], -1, HP)
    a3 = jnp.concatenate([a3[:, :, :MQK], a3[:, :, NOPE:NOPE + ROPE // 2], a3[:, :, MQK + ROPE // 2:]], axis=2)
    return a3.reshape(a.shape)


def _pad_lanes(v, n=HP):
    return jnp.pad(v, ((0, 0), (0, n - v.shape[1])))


def _local_step(x, positions, tgt, wt, slab_b, sp, mod3):
    bsz, s, _ = x.shape
    t = bsz * s
    x2 = x.reshape(t, D)
    tgt2 = tgt.reshape(t, D)
    pos = positions.reshape(t, 1)
    fr16 = ROPE_THETA ** (-jnp.arange(0, ROPE, 2, dtype=F32) / ROPE)
    zero = lambda n: jnp.zeros((n,), F32)
    fr = jnp.concatenate([zero(NOPE), fr16, fr16, zero(HP - MQK)]).reshape(1, HP)
    sg = jnp.concatenate([zero(NOPE), -jnp.ones((ROPE // 2,), F32), jnp.ones((ROPE // 2,), F32), zero(HP - MQK)]).reshape(1, HP)

    w_in_t = _perm_w_in_t(wt["w_in"])
    wa_pad = _pad_wa(wt["gla_w_alpha"].T)
    wuq = _pad_wuq(wt["mla_w_uq"].T)
    wukv = _pad_wukv(wt["mla_w_ukv"].T)
    qn_g, kn_g = _pad_lanes(sp["mla_qn_g"]), _pad_lanes(sp["mla_kn_g"])

    _, h = _norm_mod_fwd(x2, sp["norm1_g"], mod3, 0, 1, "norm1_fwd")
    proj = _mm(h, w_in_t, "nt", (BF16,), "proj_fwd", tn=PW // 4)
    o_gla, og, states = _gla_fwd(proj, wa_pad, sp["gla_b_alpha"], sp["gla_out_norm_g"], bsz)
    q2, k2, v2 = _mla_pre_fwd(proj, pos, fr, sg, sp["mla_q_lat_g"], sp["mla_kv_lat_g"], _repeat_half(qn_g), _repeat_half(kn_g),
                              _repeat_half(wuq), wukv)
    o2, lse2, core_row = _attn_fwd(q2, k2, v2, bsz, slab_b)
    wt = dict(wt, **_unpack_gathered(_cross_core_fill(core_row), SLAB_B))
    wo_pad = _pad_wo(wt["mla_w_o"])
    y_a = _mm(og, wt["gla_w_o"], "nn", (BF16,), "gla_out_fwd")
    y_b = _mm(o2, wo_pad, "nn", (BF16,), "mla_out_fwd")
    mix = _merge_fwd(proj, sp["b_merge"], y_a, y_b)
    mixed = _mm(mix, wt["w_out"], "nn", (F32,), "w_out_fwd")

    x1, h2 = _norm_mod_fwd(x2, sp["norm2_g"], mod3, 3, 4, "norm2_fwd", mixed=mixed, i_gate=2)
    a, f = _mm(h2, wt["mlp_w1"], "nt", (BF16, BF16), "mlp1_fwd",
               epi=lambda acc: (acc, jnp.square(jnp.maximum(acc, 0.0))))
    ff = _mm(f, wt["mlp_w2"], "nn", (F32,), "mlp2_fwd")
    dy, dff, acc_g2, acc_loss = _loss_head(x1, ff, tgt2, mod3)

    gw = {}
    gw["mlp_w2"] = _mm(f, dff, "tn", (BF16,), "mlp2_dw")
    da = _mm(dff, wt["mlp_w2"], "nt", (BF16,), "mlp2_dx", extras=(a,),
             epi=lambda acc, av: (acc * (2.0 * jnp.maximum(av.astype(F32), 0.0)),))
    gw["mlp_w1"] = _mm(da, h2, "tn", (BF16,), "mlp1_dw")
    dh2 = _mm(da, wt["mlp_w1"], "nn", (F32,), "mlp1_dx")
    dx1, dmixed, accb2, accg2 = _norm_mod_bwd(x1, dh2, dy, sp["norm2_g"], mod3, 3, 4, "norm2_bwd", mixed=mixed, i_gate=2)

    gw["w_out"] = _mm(mix, dmixed, "tn", (BF16,), "w_out_dw")
    dmix = _mm(dmixed, wt["w_out"], "nt", (BF16,), "w_out_dx")
    dy_a, dy_b, dlogits, acc_bm = _merge_bwd(dmix, proj, sp["b_merge"], y_a, y_b)
    gw["gla_w_o"] = _mm(og, dy_a, "tn", (BF16,), "gla_out_dw")
    dog = _mm(dy_a, wt["gla_w_o"], "nt", (BF16,), "gla_out_dx")
    gw["mla_w_o"] = _unpad_wo(_mm(o2, dy_b, "tn", (BF16,), "mla_out_dw"))
    do2 = _mm(dy_b, wo_pad, "nt", (BF16,), "mla_out_dx")
    dq2, dk2, dv2, parts_b = _attn_bwd(q2, k2, v2, do2, o2, lse2, bsz, _pack_per_device(gw, SLAB_B))
    dq_g, dk_g, dv_g, dgg, dz, acc_ba, acc_go = _gla_bwd(dog, o_gla, proj, wa_pad, sp["gla_b_alpha"],
                                                         sp["gla_out_norm_g"], states, bsz)
    gw["gla_w_alpha"] = _mm(proj, dz, "tn", (F32,), "gla_alpha_dw", a_off=43, m=LANES)[:GLR].T.astype(BF16)
    dmisc_gla = _mm(dz, wa_pad, "nt", (F32,), "gla_alpha_dx")
    dcq, dckv, dmisc, gwuq, gwukv, acc_mla = _mla_pre_bwd(dq2, dk2, dv2, dmisc_gla, proj, pos, fr, sg, sp["mla_q_lat_g"],
                                                         sp["mla_kv_lat_g"], qn_g, kn_g, wuq, wukv)
    gw["mla_w_uq"] = _unpad_wuq(gwuq).T.astype(BF16)
    gw["mla_w_ukv"] = _unpad_wukv(gwukv).T.astype(BF16)
    dproj = jnp.concatenate([dq_g, dk_g, dv_g, dgg, dlogits, dcq, dckv, dmisc], axis=1)
    gw["w_in"] = _unperm_w_in_t(_mm(dproj, h, "tn", (BF16,), "proj_dw", tm=PW // 4))
    dh, parts_a = _mm(dproj, w_in_t, "nn", (F32,), "proj_dx", tk=PW // 4,
                      cargo=_sum_over_cores(_pack_per_device(gw, SLAB_A), "a"))
    grad_x, accb1, accg1 = _norm_mod_bwd(x2, dh, dx1, sp["norm1_g"], mod3, 0, 1, "norm1_bwd")

    dmod = jnp.stack([accb1[:, 0], accb1[:, 1], accb2[:, 2], accb2[:, 0], accb2[:, 1], acc_g2[:, 0]], axis=1)

    rows = {
        "dmod": dmod.reshape(bsz * 6, D),
        "norm1_g": accg1[0:1], "norm2_g": accg2[0:1],
        "b_merge": acc_bm[0:1].reshape(2, D),
        "gla_b_alpha": _pad_lanes(acc_ba[0:1], D),
        "gla_out_norm_g": _pad_lanes(acc_go.reshape(GH, 8, GDV)[:, 0, :], D),
        "mla_q_lat_g": _pad_lanes(acc_mla[0:1], D), "mla_kv_lat_g": _pad_lanes(acc_mla[1:2], D),
        "mla_qn_g": _pad_lanes(acc_mla[2:3], D), "mla_kn_g": _pad_lanes(acc_mla[3:4], D),
        "loss": acc_loss[0:1],
    }
    return grad_x.reshape(bsz, s, D), parts_a, parts_b, rows


HBM_SPEC = pl.BlockSpec(memory_space=pltpu.HBM)


def _all_gather(ps, name):
    n = len(ps)

    def body(*refs):
        p_refs, out_refs, (send_sems, recv_sems, local_sems) = refs[:n], refs[n:2 * n], refs[2 * n:]
        x, y, c = lax.axis_index("x"), lax.axis_index("y"), lax.axis_index("c")
        me, sibling = (x, y, c), (x, y, 1 - c)
        chips = [(1 - x, y), (x, 1 - y), (1 - x, 1 - y)]

        def copy(a, k, block, to, own=False):
            slot = out_refs[a].at[4 * block[0] + 2 * block[1] + block[2]]
            return pltpu.make_async_remote_copy(
                src_ref=p_refs[a] if own else slot, dst_ref=slot, send_sem=send_sems.at[7 * a + k],
                recv_sem=recv_sems.at[7 * a + k], device_id=to, device_id_type=MESH)

        mine = [pltpu.make_async_copy(p_refs[a], out_refs[a].at[4 * x + 2 * y + c], local_sems.at[a]) for a in range(n)]
        first = [copy(a, 0, me, sibling, own=True) for a in range(n)]
        first += [copy(a, 1 + j, me, (*chip, c), own=True) for a in range(n) for j, chip in enumerate(chips)]
        for cp in mine + first:
            cp.start()
        passed = []
        for j, chip in enumerate(chips):
            for a in range(n):
                copy(a, 1 + j, (*chip, c), me).wait_recv()
                passed.append(copy(a, 4 + j, (*chip, c), sibling))
                passed[-1].start()
        for a in range(n):
            copy(a, 0, sibling, me).wait_recv()
            for j, chip in enumerate(chips):
                copy(a, 4 + j, (*chip, 1 - c), me).wait_recv()
        for cp in first + passed:
            cp.wait_send()
        for cp in mine:
            cp.wait()

    return pl.pallas_call(
        body, name=name, out_shape=[jax.ShapeDtypeStruct((N_DEV,) + p.shape, p.dtype) for p in ps],
        in_specs=[HBM_SPEC] * n, out_specs=[HBM_SPEC] * n,
        scratch_shapes=[pltpu.SemaphoreType.DMA((7 * n,)), pltpu.SemaphoreType.DMA((7 * n,)), pltpu.SemaphoreType.DMA((n,))],
    )(*ps)


def _sibling_exchange(g, name):
    def body(g_ref, out_ref, send_sem, recv_sem):
        x, y, c = lax.axis_index("x"), lax.axis_index("y"), lax.axis_index("c")
        cp = pltpu.make_async_remote_copy(src_ref=g_ref, dst_ref=out_ref, send_sem=send_sem, recv_sem=recv_sem,
                                          device_id=(x, y, 1 - c), device_id_type=MESH)
        cp.start()
        cp.wait()

    return pl.pallas_call(
        body, name=name, out_shape=jax.ShapeDtypeStruct(g.shape, g.dtype),
        in_specs=[HBM_SPEC], out_specs=HBM_SPEC,
        scratch_shapes=[pltpu.SemaphoreType.DMA(()), pltpu.SemaphoreType.DMA(())],
    )(g)


class _Exchange:
    def __init__(self, local, sends, arrivals):
        self.local, self.sends, self.arrivals = local, sends, arrivals

    def start(self):
        self.local.start()
        for cp in self.sends:
            cp.start()

    def finish(self):
        for cp in self.arrivals:
            cp.wait_recv()
        for cp in self.sends:
            cp.wait_send()
        self.local.wait()


EXCHANGE_SEMS = [pltpu.SemaphoreType.DMA((N_DEV,)), pltpu.SemaphoreType.DMA((N_DEV,)), pltpu.SemaphoreType.DMA(())]


def _all_to_all_copies(t_ref, out_ref, send_sems, recv_sems, local_sem):
    x, y, c = lax.axis_index("x"), lax.axis_index("y"), lax.axis_index("c")
    me = 4 * x + 2 * y + c

    def copy(k, src, dst):
        px, py, pc = (1 - x if k & 4 else x), (1 - y if k & 2 else y), (1 - c if k & 1 else c)
        peer = 4 * px + 2 * py + pc
        return pltpu.make_async_remote_copy(src_ref=t_ref.at[peer if src is None else src],
                                            dst_ref=out_ref.at[peer if dst is None else dst], send_sem=send_sems.at[k],
                                            recv_sem=recv_sems.at[k], device_id=(px, py, pc), device_id_type=MESH)

    return _Exchange(pltpu.make_async_copy(t_ref.at[me], out_ref.at[me], local_sem),
                     [copy(k, None, me) for k in range(1, N_DEV)], [copy(k, me, None) for k in range(1, N_DEV)])


def _chip_exchange_copies(t_ref, out_ref, send_sems, recv_sems, local_sem):
    x, y, c = lax.axis_index("x"), lax.axis_index("y"), lax.axis_index("c")
    my_chip = 2 * x + y
    chips = [(1 - x, y), (x, 1 - y), (1 - x, 1 - y)]

    def copy(j, src, dst, px, py):
        return pltpu.make_async_remote_copy(src_ref=t_ref.at[src], dst_ref=out_ref.at[dst], send_sem=send_sems.at[j],
                                            recv_sem=recv_sems.at[j], device_id=(px, py, c), device_id_type=MESH)

    return _Exchange(pltpu.make_async_copy(t_ref.at[my_chip], out_ref.at[my_chip], local_sem),
                     [copy(j, 2 * px + py, my_chip, px, py) for j, (px, py) in enumerate(chips)],
                     [copy(j, my_chip, 2 * px + py, px, py) for j, (px, py) in enumerate(chips)])


def _core_row_gather_copies(p_ref, out_ref, send_sems, recv_sems, local_sem):
    x, y, c = lax.axis_index("x"), lax.axis_index("y"), lax.axis_index("c")
    peers = [(x, y, 1 - c), (1 - x, y, c), (x, 1 - y, c), (1 - x, 1 - y, c)]

    def slot(px, py, pc):
        return out_ref.at[4 * px + 2 * py + pc]

    def copy(j, block, to):
        return pltpu.make_async_remote_copy(src_ref=p_ref, dst_ref=slot(*block), send_sem=send_sems.at[j],
                                            recv_sem=recv_sems.at[j], device_id=to, device_id_type=MESH)

    return _Exchange(pltpu.make_async_copy(p_ref, slot(x, y, c), local_sem),
                     [copy(j, (x, y, c), peer) for j, peer in enumerate(peers)],
                     [copy(j, peer, peer) for j, peer in enumerate(peers)])


def _cross_core_fill(gathered):
    def body(g_ref, out_ref, send_sems, recv_sems):
        x, y, c = lax.axis_index("x"), lax.axis_index("y"), lax.axis_index("c")
        chips = [(1 - x, y), (x, 1 - y), (1 - x, 1 - y)]

        def copy(j, pc):
            px, py = chips[j]
            slot = 4 * px + 2 * py + pc
            return pltpu.make_async_remote_copy(src_ref=g_ref.at[slot], dst_ref=out_ref.at[slot], send_sem=send_sems.at[j],
                                                recv_sem=recv_sems.at[j], device_id=(x, y, 1 - c), device_id_type=MESH)

        sends = [copy(j, c) for j in range(3)]
        for cp in sends:
            cp.start()
        for j in range(3):
            copy(j, 1 - c).wait_recv()
        for cp in sends:
            cp.wait_send()

    return pl.pallas_call(
        body, name="weights_cross_core_fill", out_shape=jax.ShapeDtypeStruct(gathered.shape, gathered.dtype),
        in_specs=[HBM_SPEC], out_specs=HBM_SPEC, input_output_aliases={0: 0},
        scratch_shapes=[pltpu.SemaphoreType.DMA((3,)), pltpu.SemaphoreType.DMA((3,))],
    )(gathered)


def _slab_block(r):
    return max(b for b in range(16, SLAB_BLOCK_MAX + 1, 16) if r % b == 0)


def _pair_sum(a, b, name):
    n, r, cdim = a.shape
    rb = _slab_block(r)
    blk = pl.BlockSpec((1, rb, cdim), lambda j, i: (j, i, 0))

    def body(a_ref, b_ref, o_ref):
        o_ref[...] = (a_ref[...].astype(F32) + b_ref[...].astype(F32)).astype(BF16)

    return pl.pallas_call(
        body, name=name, grid=(n, r // rb), in_specs=[blk, blk], out_specs=blk,
        out_shape=jax.ShapeDtypeStruct(a.shape, BF16), compiler_params=_cparams(("arbitrary", "arbitrary")),
    )(a, b)


def _adamw_math(w, g, m, v):
    m = ADAM_B1 * m + (1.0 - ADAM_B1) * g
    v = ADAM_B2 * v + (1.0 - ADAM_B2) * jnp.square(g)
    m_hat = m / (1.0 - ADAM_B1 ** ADAM_STEP)
    v_hat = v / (1.0 - ADAM_B2 ** ADAM_STEP)
    delta = -ADAM_LR * (m_hat / (jnp.sqrt(v_hat) + ADAM_EPS) + ADAM_WD * w)
    return delta, m, v


def _slab_sum(parts, name):
    n, r, cdim = parts.shape
    rb = _slab_block(r)
    blk = pl.BlockSpec((rb, cdim), lambda i: (i, 0))

    def body(p_ref, g_out):
        g = p_ref[0].astype(F32)
        for j in range(1, n):
            g = g + p_ref[j].astype(F32)
        g_out[...] = g

    return pl.pallas_call(
        body, name=name, grid=(r // rb,),
        in_specs=[pl.BlockSpec((n, rb, cdim), lambda i: (0, i, 0))], out_specs=blk,
        out_shape=jax.ShapeDtypeStruct((r, cdim), F32), compiler_params=_cparams(("arbitrary",)),
    )(parts)


def _adamw(g, w, m, v, name):
    r, cdim = w.shape
    rb = _tile(r, 256)
    blk = pl.BlockSpec((rb, cdim), lambda i: (i, 0))

    def body(g_ref, w_ref, m_ref, v_ref, d_out, m_out, v_out):
        d_out[...], m_out[...], v_out[...] = _adamw_math(w_ref[...], g_ref[...], m_ref[...], v_ref[...])

    return pl.pallas_call(
        body, name=name, grid=(r // rb,), in_specs=[blk] * 4, out_specs=[blk] * 3,
        out_shape=[jax.ShapeDtypeStruct((r, cdim), F32)] * 3, compiler_params=_cparams(("arbitrary",)),
    )(g, w, m, v)


def _adamw_small(parts, w, m, v):
    def body(p_ref, w_ref, m_ref, v_ref, g_out, d_out, m_out, v_out, loss_out):
        def total(srcs):
            acc = None
            for r in srcs:
                for j in range(N_DEV):
                    term = p_ref[j, r:r + 1, :]
                    acc = term if acc is None else acc + term
            return acc

        for prow, srcs in enumerate(SMALL_SOURCES):
            one = slice(prow, prow + 1)
            g = total(srcs)
            g_out[one, :] = g
            d_out[one, :], m_out[one, :], v_out[one, :] = _adamw_math(w_ref[one, :], g, m_ref[one, :], v_ref[one, :])
        loss_out[...] = jnp.broadcast_to(jnp.sum(total(LOSS_SOURCE), axis=1, keepdims=True), (8, LANES))

    full = lambda shp: pl.BlockSpec(shp, lambda i: (0,) * len(shp))
    return pl.pallas_call(
        body, name="adamw_small", grid=(1,),
        in_specs=[full((N_DEV, SMALL_ROWS, D)), full((16, D)), full((16, D)), full((16, D))],
        out_specs=[full((16, D))] * 4 + [full((8, LANES))],
        out_shape=[jax.ShapeDtypeStruct((16, D), F32)] * 4 + [jax.ShapeDtypeStruct((8, LANES), F32)],
        compiler_params=_cparams(("arbitrary",)),
    )(parts, w, m, v)


def _to_slab(shard, form):
    if form == "N":
        return shard
    return shard.T if form == "T" else shard.T.reshape(-1, D)


def _from_slab(block, form, shard_shape):
    if form == "N":
        return block
    return block.T if form == "T" else block.reshape(shard_shape[1], shard_shape[0]).T


def _gathered_full(g, form, shard_shape):
    if form == "TR":
        return g.reshape(N_DEV * shard_shape[1], shard_shape[0])
    return g.reshape(N_DEV * g.shape[1], D)


def _pack_slab(shards, layout):
    return jnp.concatenate([jnp.zeros((r, D), BF16) if n is None else _to_slab(shards[n], form).astype(BF16)
                            for n, r, form in layout], axis=0)


def _unpack_gathered(gathered, layout):
    out, off = {}, 0
    for n, r, form in layout:
        if n is not None:
            out[n] = _gathered_full(gathered[:, off:off + r], form, SHARD_SHAPES[n])
        off += r
    return out


def _pack_per_device(gw, layout):
    return jnp.concatenate([jnp.zeros((N_DEV, r, D), BF16) if n is None else gw[n].reshape(N_DEV, r, D)
                            for n, r, _ in layout], axis=1)


def _adamw_shards(gslab, layout, wts, mom, var):
    out, off = {}, 0
    for n, r, form in layout:
        if n is not None:
            g = _from_slab(gslab[off:off + r], form, SHARD_SHAPES[n])
            out[n] = (g,) + tuple(_adamw(g, wts[n], mom[n], var[n], "adamw_" + n))
        off += r
    return out


def _sum_over_cores(per_dev, tag):
    my_c = lax.axis_index("c")
    pairs = per_dev.reshape(4, 2, per_dev.shape[1], D)
    keep = lax.dynamic_index_in_dim(pairs, my_c, axis=1, keepdims=False)
    give = lax.dynamic_index_in_dim(pairs, 1 - my_c, axis=1, keepdims=False)
    return _pair_sum(keep, _sibling_exchange(give, "rs_sibling_exchange_" + tag), "rs_pair_sum_" + tag)


def _small_pack(vals):
    rows = []
    for n in SMALL:
        v = vals[n].reshape(-1)
        k = -(-v.shape[0] // D)
        rows.append(jnp.pad(v, (0, k * D - v.shape[0])).reshape(k, D))
    return jnp.concatenate(rows, axis=0)


def _small_unpack(packed, shapes):
    out = {}
    for n in SMALL:
        k = shapes[n][-1]
        r0 = SMALL_ROW[n]
        out[n] = packed[r0:r0 + -(-k // D)].reshape(-1)[:k].reshape(shapes[n])
    return out


def kernel(x, c, positions, w_ada, b_ada, norm1_g, w_in, b_merge, gla_w_alpha, gla_b_alpha, gla_out_norm_g, gla_w_o, mla_q_lat_g, mla_w_uq, mla_kv_lat_g, mla_w_ukv, mla_qn_g, mla_kn_g, mla_w_o, w_out, norm2_g, mlp_w1, mlp_w2, loss_target, m_w_ada, m_b_ada, m_norm1_g, m_w_in, m_b_merge, m_gla_w_alpha, m_gla_b_alpha, m_gla_out_norm_g, m_gla_w_o, m_mla_q_lat_g, m_mla_w_uq, m_mla_kv_lat_g, m_mla_w_ukv, m_mla_qn_g, m_mla_kn_g, m_mla_w_o, m_w_out, m_norm2_g, m_mlp_w1, m_mlp_w2, v_w_ada, v_b_ada, v_norm1_g, v_w_in, v_b_merge, v_gla_w_alpha, v_gla_b_alpha, v_gla_out_norm_g, v_gla_w_o, v_mla_q_lat_g, v_mla_w_uq, v_mla_kv_lat_g, v_mla_w_ukv, v_mla_qn_g, v_mla_kn_g, v_mla_w_o, v_w_out, v_norm2_g, v_mlp_w1, v_mlp_w2):
    args = dict(locals())
    wts = {n: args[n][0] for n in WEIGHTS}
    mom = {n: args["m_" + n][0] for n in WEIGHTS}
    var = {n: args["v_" + n][0] for n in WEIGHTS}
    my_c = lax.axis_index("c")
    my_dev = 4 * lax.axis_index("x") + 2 * lax.axis_index("y") + my_c
    bsz = x.shape[0]
    sp = {n: wts[n].reshape(1, -1) for n in SMALL}

    gathered_a, c_gathered = _all_gather([_pack_slab(wts, SLAB_A), jnp.pad(c, ((0, 8 - bsz), (0, 0)))], "weights_all_gather")
    wt = _unpack_gathered(gathered_a, SLAB_A)

    c_all = c_gathered[:, :bsz].reshape(N_DEV * bsz, D)
    bias = lax.dynamic_slice_in_dim(sp["b_ada"], my_dev * ADA_COLS, ADA_COLS, axis=1)
    mod_cols = _mm(c_all, wts["w_ada"], "nn", (F32,), "ada_fwd", pro=_silu, epi=lambda acc, b: (acc + b,),
                   extras=(jnp.broadcast_to(bias, (N_DEV * bsz, ADA_COLS)),))
    mod_all, = _all_gather([mod_cols], "mod_all_gather")
    mod_mine = lax.dynamic_slice_in_dim(mod_all, my_dev * bsz, bsz, axis=1)
    mod3 = jnp.transpose(mod_mine, (1, 0, 2)).reshape(bsz, 6, D)

    grad_x, parts_a, parts_b, rows = _local_step(x, positions, loss_target, wt, _pack_slab(wts, SLAB_B), sp, mod3)

    big = dict(_adamw_shards(_slab_sum(parts_a, "rs_slab_sum_a"), SLAB_A, wts, mom, var),
               **_adamw_shards(_slab_sum(parts_b, "rs_slab_sum_b"), SLAB_B, wts, mom, var))

    order = ["dmod", "norm1_g", "norm2_g", "b_merge", "gla_b_alpha", "gla_out_norm_g", "mla_q_lat_g", "mla_kv_lat_g",
             "mla_qn_g", "mla_kn_g", "loss"]
    part_rows = jnp.concatenate([rows[n] for n in order], axis=0)
    part_rows = jnp.pad(part_rows, ((0, SMALL_ROWS - part_rows.shape[0]), (0, 0)))
    all_rows, = _all_gather([part_rows], "partials_all_gather")

    dmod_all = all_rows[:, :6 * bsz].reshape(N_DEV * bsz, 6 * D)
    dmod_cols = lax.dynamic_slice_in_dim(dmod_all, my_dev * ADA_COLS, ADA_COLS, axis=1)
    g_ada = _mm(c_all, dmod_cols, "tn", (F32,), "ada_dw", pro=_silu)
    big["w_ada"] = (g_ada,) + tuple(_adamw(g_ada, wts["w_ada"], mom["w_ada"], var["w_ada"], "adamw_w_ada"))

    small = _adamw_small(all_rows, _small_pack({n: wts[n] for n in SMALL}), _small_pack({n: mom[n] for n in SMALL}),
                         _small_pack({n: var[n] for n in SMALL}))
    loss = small[4][0, 0]
    small_shapes = {n: wts[n].shape for n in SMALL}
    small = [_small_unpack(o, small_shapes) for o in small[:4]]

    outs = [loss, grad_x]
    for k in range(4):
        for n in WEIGHTS:
            val = big[n][k] if n in BIG else small[k][n]
            outs.append(val.reshape((1,) + tuple(wts[n].shape)))
    return tuple(outs)
```

```python
import functools

import jax
import jax.numpy as jnp
from jax import lax
from jax.experimental import pallas as pl
from jax.experimental.pallas import tpu as pltpu

F32 = jnp.float32
BF16 = jnp.bfloat16
MESH = pl.DeviceIdType.MESH

D = 1024
EPS = 1e-6
CHUNK = 64
GH, GDK, GDV, GLR, GTAU = 4, 128, 256, 16, 16.0
MH, MQR, MKVR, NOPE, ROPE, MV = 16, 256, 128, 64, 32, 64
MQK = NOPE + ROPE
HP = 128
FF = 4 * D
ROPE_THETA = 10000.0
IN_WIDTH = 5552
PW = 5632
N_DEV = 8
LANES = 128
SLAB_BLOCK_MAX = 400
ADA_COLS = 6 * D // N_DEV
SMALL_ROWS = 32
SMALL_SOURCES = tuple([(r, 6 + r) for r in range(6)] + [(12,), (13,), (14,), (15,), (16,), (17, 18, 19, 20),
                                                         (21,), (22,), (23,), (24,)])
LOSS_SOURCE = (25,)
VMEM_LIMIT = 56 * 1024 * 1024

ADAM_LR, ADAM_B1, ADAM_B2, ADAM_EPS, ADAM_WD, ADAM_STEP = 0.001, 0.9, 0.999, 1e-08, 0.01, 10

SLAB_A = (("w_in", 694, "T"), ("gla_w_alpha", 1, "TR"), (None, 9, None), ("mla_w_uq", 48, "TR"), ("mla_w_ukv", 32, "TR"))
SLAB_B = (("mlp_w1", 512, "T"), ("gla_w_o", 128, "N"), ("mla_w_o", 128, "N"), ("w_out", 128, "N"), ("mlp_w2", 512, "N"))
BIG = ("w_ada",) + tuple(n for n, _, _ in SLAB_A + SLAB_B if n is not None)
SHARD_SHAPES = {"w_ada": (D, 6 * D // N_DEV), "w_in": (D, IN_WIDTH // N_DEV), "gla_w_alpha": (GLR, GH * GDK // N_DEV),
                "gla_w_o": (GH * GDV // N_DEV, D), "mla_w_uq": (MQR, MH * MQK // N_DEV),
                "mla_w_ukv": (MKVR, MH * (NOPE + MV) // N_DEV), "mla_w_o": (MH * MV // N_DEV, D), "w_out": (D // N_DEV, D),
                "mlp_w1": (D, FF // N_DEV), "mlp_w2": (FF // N_DEV, D)}
SMALL = ("b_ada", "norm1_g", "norm2_g", "b_merge", "gla_b_alpha", "gla_out_norm_g", "mla_q_lat_g", "mla_kv_lat_g",
         "mla_qn_g", "mla_kn_g")
SMALL_ROW = {"b_ada": 0, "norm1_g": 6, "norm2_g": 7, "b_merge": 8, "gla_b_alpha": 10, "gla_out_norm_g": 11,
             "mla_q_lat_g": 12, "mla_kv_lat_g": 13, "mla_qn_g": 14, "mla_kn_g": 15}
WEIGHTS = ("w_ada", "b_ada", "norm1_g", "w_in", "b_merge", "gla_w_alpha", "gla_b_alpha", "gla_out_norm_g", "gla_w_o",
           "mla_q_lat_g", "mla_w_uq", "mla_kv_lat_g", "mla_w_ukv", "mla_qn_g", "mla_kn_g", "mla_w_o", "w_out",
           "norm2_g", "mlp_w1", "mlp_w2")


def _cparams(sem=None):
    return pltpu.CompilerParams(dimension_semantics=sem, vmem_limit_bytes=VMEM_LIMIT)


def _tile(n, pref):
    for t in (2048, PW // 4, 1024, 512, 256, 128):
        if t <= pref and n % t == 0:
            return t
    return n


def _dot(a, b, dims, precision=None):
    return lax.dot_general(a, b, (dims, ((), ())), preferred_element_type=F32, precision=precision)


NN = ((1,), (0,))
NT = ((1,), (1,))
TN = ((0,), (0,))


def _sigmoid(x):
    return 1.0 / (1.0 + jnp.exp(-x))


def _silu(x):
    return x * _sigmoid(x)


def _mm(a, b, mode, out_dtypes, name, *, pro=None, pro_b=None, epi=None, extras=(), a_off=0, m=None, tm=2048, tn=1024,
        tk=1024, cargo=None, tail=None):
    if mode == "tn":
        kc, n = b.shape
        m = a.shape[1] if m is None else m
    elif mode == "nn":
        m, kc = a.shape
        n = b.shape[1]
    else:
        m, kc = a.shape
        n = b.shape[0]
    tm, tn, tk = _tile(m, tm), _tile(n, tn), _tile(kc, tk)
    nk = kc // tk
    dims = {"nn": NN, "nt": NT, "tn": TN}[mode]
    if mode == "tn":
        a_spec = pl.BlockSpec((tk, tm), lambda i, j, k: (k, i + a_off))
    else:
        a_spec = pl.BlockSpec((tm, tk), lambda i, j, k: (i + a_off, k))
    if mode == "nt":
        b_spec = pl.BlockSpec((tn, tk), lambda i, j, k: (j, k))
    else:
        b_spec = pl.BlockSpec((tk, tn), lambda i, j, k: (k, j))
    o_spec = pl.BlockSpec((tm, tn), lambda i, j, k: (i, j))
    n_ex, n_out = len(extras), len(out_dtypes)
    grid = (m // tm, n // tn, nk)
    has_cargo = cargo is not None
    t_ins, t_outs = (tail["ins"], tail["outs"]) if tail else ([], [])
    assert not tail or grid[1] == 1

    def body(a_ref, b_ref, *rest):
        rest = list(rest)
        take = lambda count: [rest.pop(0) for _ in range(count)]
        ex, cargo_ref, tail_in = take(n_ex), take(has_cargo), take(len(t_ins))
        outs, parts_ref, tail_out = take(n_out), take(has_cargo), take(len(t_outs))
        acc = rest.pop(0)
        steps = [pl.program_id(axis) for axis in range(3)]
        if has_cargo:
            exchange = _chip_exchange_copies(cargo_ref[0], parts_ref[0], *rest)
            first = (steps[0] == 0) & (steps[1] == 0) & (steps[2] == 0)
            last = (steps[0] == grid[0] - 1) & (steps[1] == grid[1] - 1) & (steps[2] == grid[2] - 1)
            pl.when(first)(exchange.start)
        k = steps[2]

        @pl.when(k == 0)
        def _():
            acc[...] = jnp.zeros_like(acc)

        av = a_ref[...]
        if pro is not None:
            av = pro(av)
        bv = b_ref[...]
        if pro_b is not None:
            bv = pro_b(bv)
        acc[...] += _dot(av.astype(BF16), bv.astype(BF16), dims)

        @pl.when(k == nk - 1)
        def _():
            if tail:
                tail["fn"](acc[...], steps[0], *tail_in, *tail_out)
            res = (acc[...],) if epi is None else epi(acc[...], *[e[...] for e in ex])
            for o_ref, r in zip(outs, res):
                o_ref[...] = r.astype(o_ref.dtype)

        if has_cargo:
            pl.when(last)(exchange.finish)

    cargo_in = [cargo] if has_cargo else []
    cargo_spec = [HBM_SPEC] * len(cargo_in)
    sequential = has_cargo or bool(tail)
    out = pl.pallas_call(
        body, name=name, grid=grid,
        in_specs=[a_spec, b_spec] + [o_spec] * n_ex + cargo_spec + (tail["in_specs"] if tail else []),
        out_specs=[o_spec] * n_out + cargo_spec + (tail["out_specs"] if tail else []),
        out_shape=[jax.ShapeDtypeStruct((m, n), dt) for dt in out_dtypes]
        + [jax.ShapeDtypeStruct(c.shape, c.dtype) for c in cargo_in] + t_outs,
        scratch_shapes=[pltpu.VMEM((tm, tn), F32)] + (EXCHANGE_SEMS if has_cargo else []),
        compiler_params=_cparams(("arbitrary",) * 3 if sequential else ("parallel", "parallel", "arbitrary")),
    )(a, b, *extras, *cargo_in, *t_ins)
    return out[0] if len(out) == 1 else out


def _rows(s):
    return _tile(s, 512)


def _mod_spec():
    return pl.BlockSpec((1, 6, D), lambda b, i: (b, 0, 0))


def _tok_spec(tr, nb, width=D, col=0):
    return pl.BlockSpec((tr, width), lambda b, i: (b * nb + i, col))


def _norm_mod_fwd(x, g, mod3, i_shift, i_scale, name, mixed=None, i_gate=None):
    bsz, _, _ = mod3.shape
    t = x.shape[0]
    s = t // bsz
    tr = _rows(s)
    nb = s // tr
    has_res = mixed is not None

    def body(*refs):
        if has_res:
            x_ref, mx_ref, g_ref, mod_ref, x1_ref, h_ref = refs
            xv = x_ref[...] + mod_ref[0, i_gate:i_gate + 1, :] * mx_ref[...]
            x1_ref[...] = xv
        else:
            x_ref, g_ref, mod_ref, h_ref = refs
            xv = x_ref[...]
        r = lax.rsqrt(jnp.mean(xv * xv, axis=1, keepdims=True) + EPS)
        hn = (xv * r) * g_ref[...]
        h = hn * (1.0 + mod_ref[0, i_scale:i_scale + 1, :]) + mod_ref[0, i_shift:i_shift + 1, :]
        h_ref[...] = h.astype(BF16)

    tok = _tok_spec(tr, nb)
    gspec = pl.BlockSpec((1, D), lambda b, i: (0, 0))
    ins = [x] + ([mixed] if has_res else []) + [g, mod3]
    in_specs = [tok] + ([tok] if has_res else []) + [gspec, _mod_spec()]
    out_shape = ([jax.ShapeDtypeStruct((t, D), F32)] if has_res else []) + [jax.ShapeDtypeStruct((t, D), BF16)]
    out = pl.pallas_call(
        body, name=name, grid=(bsz, nb), in_specs=in_specs, out_specs=[tok] * len(out_shape), out_shape=out_shape,
        compiler_params=_cparams(("arbitrary", "arbitrary")),
    )(*ins)
    return (out[0], out[1]) if has_res else (None, out[0])


def _norm_bwd_rows(xv, dhv, dresv, gv, mod_ref, i_scale, accb, accg):
    r = lax.rsqrt(jnp.mean(xv * xv, axis=1, keepdims=True) + EPS)
    xn = xv * r
    accb[0, 0:1, :] += jnp.sum(dhv, axis=0, keepdims=True)
    accb[0, 1:2, :] += jnp.sum(dhv * (xn * gv), axis=0, keepdims=True)
    tt = dhv * (1.0 + mod_ref[0, i_scale:i_scale + 1, :])
    accg[0:1, :] += jnp.sum(tt * xn, axis=0, keepdims=True)
    dxn = tt * gv
    return dresv + r * (dxn - xn * jnp.mean(dxn * xn, axis=1, keepdims=True))


def _norm_bwd_tail(x, dres, g, mod3, i_scale, tm):
    bsz, t = mod3.shape[0], x.shape[0]
    per_b = t // bsz // tm

    def fn(dhv, i, x_ref, dres_ref, g_ref, mod_ref, dx_ref, accb, accg):
        @pl.when(i % per_b == 0)
        def _():
            accb[...] = jnp.zeros_like(accb)

        @pl.when(i == 0)
        def _():
            accg[...] = jnp.zeros_like(accg)

        dx_ref[...] = _norm_bwd_rows(x_ref[...], dhv, dres_ref[...], g_ref[...], mod_ref, i_scale, accb, accg)

    tok = pl.BlockSpec((tm, D), lambda i, j, k: (i, 0))
    per_batch = pl.BlockSpec((1, 8, D), lambda i, j, k: (i // per_b, 0, 0))
    return dict(
        fn=fn, ins=[x, dres, g, mod3],
        in_specs=[tok, tok, pl.BlockSpec((1, D), lambda i, j, k: (0, 0)), pl.BlockSpec((1, 6, D), lambda i, j, k: (i // per_b, 0, 0))],
        outs=[jax.ShapeDtypeStruct((t, D), F32), jax.ShapeDtypeStruct((bsz, 8, D), F32), jax.ShapeDtypeStruct((8, D), F32)],
        out_specs=[tok, per_batch, pl.BlockSpec((8, D), lambda i, j, k: (0, 0))])


def _norm_mod_bwd(x, dh, dres, g, mod3, i_shift, i_scale, name, mixed=None, i_gate=None):
    bsz = mod3.shape[0]
    t = x.shape[0]
    s = t // bsz
    tr = _rows(s)
    nb = s // tr
    has_res = mixed is not None

    def body(*refs):
        if has_res:
            x_ref, dh_ref, dres_ref, mx_ref, g_ref, mod_ref, dx_ref, dmx_ref, accb, accg = refs
        else:
            x_ref, dh_ref, dres_ref, g_ref, mod_ref, dx_ref, accb, accg = refs
        b, i = pl.program_id(0), pl.program_id(1)

        @pl.when(i == 0)
        def _():
            accb[...] = jnp.zeros_like(accb)

        @pl.when((i == 0) & (b == 0))
        def _():
            accg[...] = jnp.zeros_like(accg)

        dx = _norm_bwd_rows(x_ref[...], dh_ref[...], dres_ref[...], g_ref[...], mod_ref, i_scale, accb, accg)
        dx_ref[...] = dx
        if has_res:
            accb[0, 2:3, :] += jnp.sum(dx * mx_ref[...], axis=0, keepdims=True)
            dmx_ref[...] = (dx * mod_ref[0, i_gate:i_gate + 1, :]).astype(BF16)

    tok = _tok_spec(tr, nb)
    gspec = pl.BlockSpec((1, D), lambda b, i: (0, 0))
    ins = [x, dh, dres] + ([mixed] if has_res else []) + [g, mod3]
    in_specs = [tok] * (4 if has_res else 3) + [gspec, _mod_spec()]
    out_shape = [jax.ShapeDtypeStruct((t, D), F32)] + ([jax.ShapeDtypeStruct((t, D), BF16)] if has_res else [])
    out_specs = [tok] * len(out_shape)
    out_shape += [jax.ShapeDtypeStruct((bsz, 8, D), F32), jax.ShapeDtypeStruct((8, D), F32)]
    out_specs += [pl.BlockSpec((1, 8, D), lambda b, i: (b, 0, 0)), pl.BlockSpec((8, D), lambda b, i: (0, 0))]
    return pl.pallas_call(
        body, name=name, grid=(bsz, nb), in_specs=in_specs, out_specs=out_specs, out_shape=out_shape,
        compiler_params=_cparams(("arbitrary", "arbitrary")),
    )(*ins)


def _loss_head(x1, ff, tgt, mod3):
    bsz = mod3.shape[0]
    t = x1.shape[0]
    s = t // bsz
    tr = _rows(s)
    nb = s // tr

    def body(x1_ref, ff_ref, tg_ref, mod_ref, dy_ref, dff_ref, accb, accl):
        b, i = pl.program_id(0), pl.program_id(1)

        @pl.when(i == 0)
        def _():
            accb[...] = jnp.zeros_like(accb)

        @pl.when((i == 0) & (b == 0))
        def _():
            accl[...] = jnp.zeros_like(accl)

        gate = mod_ref[0, 5:6, :]
        ffv = ff_ref[...]
        err = x1_ref[...] + gate * ffv - tg_ref[...]
        accl[0:1, :] += jnp.sum(err * err, axis=0, keepdims=True) * (0.5 / D)
        dy = err * (1.0 / D)
        dy_ref[...] = dy
        dff_ref[...] = (dy * gate).astype(BF16)
        accb[0, 0:1, :] += jnp.sum(dy * ffv, axis=0, keepdims=True)

    tok = _tok_spec(tr, nb)
    return pl.pallas_call(
        body, name="loss_head", grid=(bsz, nb), in_specs=[tok, tok, tok, _mod_spec()],
        out_specs=[tok, tok, pl.BlockSpec((1, 8, D), lambda b, i: (b, 0, 0)), pl.BlockSpec((8, D), lambda b, i: (0, 0))],
        out_shape=[jax.ShapeDtypeStruct((t, D), F32), jax.ShapeDtypeStruct((t, D), BF16),
                   jax.ShapeDtypeStruct((bsz, 8, D), F32), jax.ShapeDtypeStruct((8, D), F32)],
        compiler_params=_cparams(("arbitrary", "arbitrary")),
    )(x1, ff, tgt, mod3)


def _merge_fwd(proj, b_merge, y_a, y_b):
    t = proj.shape[0]
    tr = _tile(t, 512)

    def body(la_ref, lb_ref, bm_ref, ya_ref, yb_ref, mix_ref):
        ga = _sigmoid(la_ref[...] + bm_ref[:, 0:D])
        gb = _sigmoid(lb_ref[...] + bm_ref[:, D:2 * D])
        mix_ref[...] = (ga * ya_ref[...].astype(F32) + gb * yb_ref[...].astype(F32)).astype(BF16)

    tok = pl.BlockSpec((tr, D), lambda i: (i, 0))
    return pl.pallas_call(
        body, name="merge_fwd", grid=(t // tr,),
        in_specs=[pl.BlockSpec((tr, D), lambda i: (i, 3)), pl.BlockSpec((tr, D), lambda i: (i, 4)),
                  pl.BlockSpec((1, 2 * D), lambda i: (0, 0)), tok, tok],
        out_specs=tok, out_shape=jax.ShapeDtypeStruct((t, D), BF16),
        compiler_params=_cparams(("arbitrary",)),
    )(proj, proj, b_merge, y_a, y_b)


def _merge_bwd(dmix, proj, b_merge, y_a, y_b):
    t = proj.shape[0]
    tr = _tile(t, 512)

    def body(dm_ref, la_ref, lb_ref, bm_ref, ya_ref, yb_ref, dya_ref, dyb_ref, dl_ref, acc):
        @pl.when(pl.program_id(0) == 0)
        def _():
            acc[...] = jnp.zeros_like(acc)

        dm = dm_ref[...].astype(F32)
        ga = _sigmoid(la_ref[...] + bm_ref[:, 0:D])
        gb = _sigmoid(lb_ref[...] + bm_ref[:, D:2 * D])
        dya_ref[...] = (dm * ga).astype(BF16)
        dyb_ref[...] = (dm * gb).astype(BF16)
        dla = dm * ya_ref[...].astype(F32) * ga * (1.0 - ga)
        dlb = dm * yb_ref[...].astype(F32) * gb * (1.0 - gb)
        dl_ref[:, 0:D] = dla.astype(BF16)
        dl_ref[:, D:2 * D] = dlb.astype(BF16)
        acc[0:1, 0:D] += jnp.sum(dla, axis=0, keepdims=True)
        acc[0:1, D:2 * D] += jnp.sum(dlb, axis=0, keepdims=True)

    tok = pl.BlockSpec((tr, D), lambda i: (i, 0))
    return pl.pallas_call(
        body, name="merge_bwd", grid=(t // tr,),
        in_specs=[tok, pl.BlockSpec((tr, D), lambda i: (i, 3)), pl.BlockSpec((tr, D), lambda i: (i, 4)),
                  pl.BlockSpec((1, 2 * D), lambda i: (0, 0)), tok, tok],
        out_specs=[tok, tok, pl.BlockSpec((tr, 2 * D), lambda i: (i, 0)), pl.BlockSpec((8, 2 * D), lambda i: (0, 0))],
        out_shape=[jax.ShapeDtypeStruct((t, D), BF16), jax.ShapeDtypeStruct((t, D), BF16),
                   jax.ShapeDtypeStruct((t, 2 * D), BF16), jax.ShapeDtypeStruct((8, 2 * D), F32)],
        compiler_params=_cparams(("arbitrary",)),
    )(dmix, proj, proj, b_merge, y_a, y_b)


GLA_HEADS = 2
GLA_UNROLL = 8


def _log_sigmoid(z):
    return jnp.minimum(z, 0.0) - jnp.log(1.0 + jnp.exp(-jnp.abs(z)))


def _tri(lower):
    r = lax.broadcasted_iota(jnp.int32, (CHUNK, CHUNK), 0)
    c = lax.broadcasted_iota(jnp.int32, (CHUNK, CHUNK), 1)
    return jnp.where(r >= c if lower else r <= c, 1.0, 0.0).astype(F32)


def _gla_fwd(proj, wa_pad, b_alpha, g_out, bsz):
    t = proj.shape[0]
    s = t // bsz
    nc = s // CHUNK
    p, kw, vw = GLA_HEADS, GLA_HEADS * GDK, GLA_HEADS * GDV

    def body(q_ref, k_ref, v_ref, gg_ref, ms_ref, wa_ref, ba_ref, go_ref, o_ref, og_ref, st_ref, la, state):
        z = _dot(ms_ref[...].astype(BF16), wa_ref[...], NN) + ba_ref[...]
        la[...] = _log_sigmoid(z) * (1.0 / GTAU)
        state[...] = jnp.zeros_like(state)
        low = _tri(True)
        gout = go_ref[...]

        def chunk(n, carry):
            rows = pl.ds(pl.multiple_of(n * CHUNK, CHUNK), CHUNK)
            for hh in range(p):
                kc, vc = slice(hh * GDK, (hh + 1) * GDK), slice(hh * GDV, (hh + 1) * GDV)
                lac = la[rows, kc]
                cum = _dot(low, lac, NN, lax.Precision.HIGHEST)
                ce = jnp.sum(lac, axis=0, keepdims=True)
                kd = (k_ref[rows, kc].astype(F32) * jnp.exp(ce - cum)).astype(BF16)
                new = state[vc, :] * jnp.exp(ce) + _dot(v_ref[rows, vc].astype(BF16), kd, TN)
                state[vc, :] = new
                st_ref[pl.ds(pl.multiple_of((hh * nc + n) * GDV, GDV), GDV), :] = new.astype(BF16)
                qs = (q_ref[rows, kc].astype(F32) * (GDK ** -0.5)).astype(BF16)
                o = _dot(qs, new.astype(BF16), NT)
                o_ref[rows, vc] = o
                ro = lax.rsqrt(jnp.mean(o * o, axis=1, keepdims=True) + EPS)
                og_ref[rows, vc] = (((o * ro) * gout) * _silu(gg_ref[rows, vc].astype(F32))).astype(BF16)
            return carry

        lax.fori_loop(0, nc, chunk, 0, unroll=GLA_UNROLL)

    return pl.pallas_call(
        body, name="gla_fwd", grid=(bsz, GH // p),
        in_specs=[pl.BlockSpec((s, kw), lambda b, h: (b, h)), pl.BlockSpec((s, kw), lambda b, h: (b, GH // p + h)),
                  pl.BlockSpec((s, vw), lambda b, h: (b, GH // p + h)), pl.BlockSpec((s, vw), lambda b, h: (b, 2 * GH // p + h)),
                  pl.BlockSpec((s, LANES), lambda b, h: (b, 43)),
                  pl.BlockSpec((LANES, kw), lambda b, h: (0, h)), pl.BlockSpec((1, kw), lambda b, h: (0, h)),
                  pl.BlockSpec((1, GDV), lambda b, h: (0, 0))],
        out_specs=[pl.BlockSpec((s, vw), lambda b, h: (b, h)), pl.BlockSpec((s, vw), lambda b, h: (b, h)),
                   pl.BlockSpec((p * nc * GDV, GDK), lambda b, h: (b * (GH // p) + h, 0))],
        out_shape=[jax.ShapeDtypeStruct((t, GH * GDV), F32), jax.ShapeDtypeStruct((t, GH * GDV), BF16),
                   jax.ShapeDtypeStruct((bsz * GH * nc * GDV, GDK), BF16)],
        scratch_shapes=[pltpu.VMEM((s, kw), F32), pltpu.VMEM((vw, GDK), F32)],
        compiler_params=_cparams(("arbitrary", "arbitrary")),
    )(proj, proj, proj, proj, proj, wa_pad, b_alpha, g_out)


def _gla_bwd(dog, o, proj, wa_pad, b_alpha, g_out, states, bsz):
    t = proj.shape[0]
    s = t // bsz
    nc = s // CHUNK
    p, kw, vw = GLA_HEADS, GLA_HEADS * GDK, GLA_HEADS * GDV

    def body(dog_ref, o_ref, q_ref, k_ref, v_ref, gg_ref, ms_ref, wa_ref, ba_ref, go_ref, st_ref,
             dq_ref, dk_ref, dv_ref, dgg_ref, dz_ref, dba, dgo, zs, la, carry_g):
        @pl.when(pl.program_id(1) == 0)
        def _():
            dba[...] = jnp.zeros_like(dba)
            dgo[...] = jnp.zeros_like(dgo)

        z = _dot(ms_ref[...].astype(BF16), wa_ref[...], NN) + ba_ref[...]
        zs[...] = z
        la[...] = _log_sigmoid(z) * (1.0 / GTAU)
        carry_g[...] = jnp.zeros_like(carry_g)
        low, upp = _tri(True), _tri(False)
        gout = go_ref[...]
        last_row = lax.broadcasted_iota(jnp.int32, (CHUNK, GDK), 0) == CHUNK - 1

        def chunk(step, carry):
            n = nc - 1 - step
            rows = pl.ds(pl.multiple_of(n * CHUNK, CHUNK), CHUNK)
            for hh in range(p):
                kc, vc = slice(hh * GDK, (hh + 1) * GDK), slice(hh * GDV, (hh + 1) * GDV)
                lac = la[rows, kc]
                cum = _dot(low, lac, NN, lax.Precision.HIGHEST)
                ce = jnp.sum(lac, axis=0, keepdims=True)
                e = jnp.exp(ce - cum)
                dec = jnp.exp(ce)
                kf = k_ref[rows, kc].astype(F32)
                kd = (kf * e).astype(BF16)
                vv = v_ref[rows, vc].astype(BF16)
                qs = (q_ref[rows, kc].astype(F32) * (GDK ** -0.5)).astype(BF16)
                ov = o_ref[rows, vc]
                ro = lax.rsqrt(jnp.mean(ov * ov, axis=1, keepdims=True) + EPS)
                on = ov * ro
                gg = gg_ref[rows, vc].astype(F32)
                sg = _sigmoid(gg)
                dogv = dog_ref[rows, vc].astype(F32)
                dgg_ref[rows, vc] = (dogv * (on * gout) * (sg * (1.0 + gg * (1.0 - sg)))).astype(BF16)
                t1 = dogv * (gg * sg)
                dgo[8 * hh:8 * hh + 1, :] += jnp.sum(t1 * on, axis=0, keepdims=True)
                don = t1 * gout
                do = ro * (don - on * jnp.mean(don * on, axis=1, keepdims=True))
                dob = do.astype(BF16)
                st_n = st_ref[pl.ds(pl.multiple_of((hh * nc + n) * GDV, GDV), GDV), :]
                dq_ref[rows, kc] = (_dot(dob, st_n, NN) * (GDK ** -0.5)).astype(BF16)
                dn = carry_g[vc, :] + _dot(dob, qs, TN)
                prev = hh * nc + jnp.maximum(n - 1, 0)
                st_p = st_ref[pl.ds(pl.multiple_of(prev * GDV, GDV), GDV), :].astype(F32) * jnp.where(n > 0, 1.0, 0.0)
                ddec = jnp.sum(dn * st_p, axis=0, keepdims=True)
                dnb = dn.astype(BF16)
                dkd = _dot(vv, dnb, NN)
                dv_ref[rows, vc] = _dot(kd, dnb, NT).astype(BF16)
                dk_ref[rows, kc] = (dkd * e).astype(BF16)
                w = dkd * kf * e
                dce = jnp.sum(w, axis=0, keepdims=True) + ddec * dec
                dcum = jnp.where(last_row, dce - w, -w)
                dla = _dot(upp, dcum, NN, lax.Precision.HIGHEST)
                dz = dla * (1.0 / GTAU) * _sigmoid(-zs[rows, kc])
                dba[0:1, kc] += jnp.sum(dz, axis=0, keepdims=True)
                dz_ref[rows, kc] = dz.astype(BF16)
                carry_g[vc, :] = dn * dec
            return carry

        lax.fori_loop(0, nc, chunk, 0, unroll=GLA_UNROLL)

    hv = pl.BlockSpec((s, vw), lambda h, b: (b, h))
    hk = pl.BlockSpec((s, kw), lambda h, b: (b, h))
    return pl.pallas_call(
        body, name="gla_bwd", grid=(GH // p, bsz),
        in_specs=[hv, hv, hk, pl.BlockSpec((s, kw), lambda h, b: (b, GH // p + h)),
                  pl.BlockSpec((s, vw), lambda h, b: (b, GH // p + h)), pl.BlockSpec((s, vw), lambda h, b: (b, 2 * GH // p + h)),
                  pl.BlockSpec((s, LANES), lambda h, b: (b, 43)), pl.BlockSpec((LANES, kw), lambda h, b: (0, h)),
                  pl.BlockSpec((1, kw), lambda h, b: (0, h)), pl.BlockSpec((1, GDV), lambda h, b: (0, 0)),
                  pl.BlockSpec((p * nc * GDV, GDK), lambda h, b: (b * (GH // p) + h, 0))],
        out_specs=[hk, hk, hv, hv, hk, pl.BlockSpec((8, kw), lambda h, b: (0, h)),
                   pl.BlockSpec((8 * p, GDV), lambda h, b: (h, 0))],
        out_shape=[jax.ShapeDtypeStruct((t, GH * GDK), BF16), jax.ShapeDtypeStruct((t, GH * GDK), BF16),
                   jax.ShapeDtypeStruct((t, GH * GDV), BF16), jax.ShapeDtypeStruct((t, GH * GDV), BF16),
                   jax.ShapeDtypeStruct((t, GH * GDK), BF16), jax.ShapeDtypeStruct((8, GH * GDK), F32),
                   jax.ShapeDtypeStruct((8 * GH, GDV), F32)],
        scratch_shapes=[pltpu.VMEM((s, kw), F32), pltpu.VMEM((s, kw), F32), pltpu.VMEM((vw, GDK), F32)],
        compiler_params=_cparams(("arbitrary", "arbitrary")),
    )(dog, o, proj, proj, proj, proj, proj, wa_pad, b_alpha, g_out, states)


def _rope_tables(pos_ref, fr_ref, sg_ref):
    ang = pos_ref[...].astype(F32) * fr_ref[...]
    return jnp.cos(ang), jnp.sin(ang) * sg_ref[...]


def _partner(x):
    lane = lax.broadcasted_iota(jnp.int32, x.shape, 1)
    return jnp.where(lane < NOPE + ROPE // 2, pltpu.roll(x, LANES - ROPE // 2, 1), pltpu.roll(x, ROPE // 2, 1))


def _mla_rows(t):
    return _tile(t, 512)


def _mla_pre_fwd(proj, pos, fr, sg, q_lat_g, kv_lat_g, qn_g, kn_g, wuq, wukv):
    t = proj.shape[0]
    tr = _mla_rows(t)

    def body(cq_ref, ckv_ref, ms_ref, pos_ref, fr_ref, sg_ref, qlg, kvlg, qng, kng, wuq_ref, wukv_ref, q_out, k_out, v_out):
        lane = lax.broadcasted_iota(jnp.int32, (tr, HP), 1)
        real = jnp.where(lane < MQK, 1.0, 0.0)
        cos, sin = _rope_tables(pos_ref, fr_ref, sg_ref)
        cos = cos * real
        cq = cq_ref[...].astype(F32)
        cqn = (cq * lax.rsqrt(jnp.mean(cq * cq, axis=1, keepdims=True) + EPS) * qlg[...]).astype(BF16)
        ckv = ckv_ref[...].astype(F32)
        ckvn = (ckv * lax.rsqrt(jnp.mean(ckv * ckv, axis=1, keepdims=True) + EPS) * kvlg[...]).astype(BF16)
        kpe = jnp.where((lane >= NOPE) & (lane < MQK), ms_ref[...].astype(F32), 0.0)
        kpe = kpe + jnp.where(lane < MQK + ROPE // 2, pltpu.roll(kpe, ROPE, 1), 0.0)
        lane_all = lax.broadcasted_iota(jnp.int32, (tr, MH * HP), 1)
        v_out[...] = jnp.where(lane_all % HP == MV, 1.0, _dot(ckvn, wukv_ref[:, MH * HP:], NN)).astype(BF16)

        def norm_rope(x, gain):
            xn = x * lax.rsqrt(jnp.sum(x * x * real, axis=1, keepdims=True) * (1.0 / MQK) + EPS) * gain
            return (xn * cos + pltpu.roll(xn, LANES - ROPE // 2, 1) * sin).astype(BF16)

        for h in range(MH):
            cols = slice(h * HP, (h + 1) * HP)
            q_out[:, cols] = norm_rope(_dot(cqn, wuq_ref[:, cols], NN), qng[...])
            k_out[:, cols] = norm_rope(_dot(ckvn, wukv_ref[:, cols], NN) + kpe, kng[...])

    def full(a):
        return pl.BlockSpec(a.shape, lambda i: (0, 0))

    wide = pl.BlockSpec((tr, MH * HP), lambda i: (i, 0))
    return pl.pallas_call(
        body, name="mla_pre_fwd", grid=(t // tr,),
        in_specs=[pl.BlockSpec((tr, MQR), lambda i: (i, 20)), pl.BlockSpec((tr, MKVR), lambda i: (i, 42)),
                  pl.BlockSpec((tr, LANES), lambda i: (i, 43)), pl.BlockSpec((tr, 1), lambda i: (i, 0)),
                  full(fr), full(sg), full(q_lat_g), full(kv_lat_g), full(qn_g), full(kn_g), full(wuq), full(wukv)],
        out_specs=[wide, wide, wide],
        out_shape=[jax.ShapeDtypeStruct((t, MH * HP), BF16)] * 3,
        compiler_params=_cparams(("arbitrary",)),
    )(proj, proj, proj, pos, fr, sg, q_lat_g, kv_lat_g, qn_g, kn_g, wuq, wukv)


def _mla_pre_bwd(dq2, dk2, dv2, dmisc_gla, proj, pos, fr, sg, q_lat_g, kv_lat_g, qn_g, kn_g, wuq, wukv):
    t = proj.shape[0]
    tr = _mla_rows(t)

    def body(dq_ref, dk_ref, dv_ref, dmg_ref, cq_ref, ckv_ref, ms_ref, pos_ref, fr_ref, sg_ref, qlg, kvlg, qng, kng,
             wuq_ref, wukv_ref, dcq_ref, dckv_ref, dms_ref, dwuq, dwukv, acc, dqf, dkvf):
        @pl.when(pl.program_id(0) == 0)
        def _():
            dwuq[...] = jnp.zeros_like(dwuq)
            dwukv[...] = jnp.zeros_like(dwukv)
            acc[...] = jnp.zeros_like(acc)

        cos, sin = _rope_tables(pos_ref, fr_ref, sg_ref)
        cq = cq_ref[...].astype(F32)
        rc = lax.rsqrt(jnp.mean(cq * cq, axis=1, keepdims=True) + EPS)
        xc = cq * rc
        cqn = (xc * qlg[...]).astype(BF16)
        ckv = ckv_ref[...].astype(F32)
        rkv = lax.rsqrt(jnp.mean(ckv * ckv, axis=1, keepdims=True) + EPS)
        xkv = ckv * rkv
        ckvn = (xkv * kvlg[...]).astype(BF16)
        lane = lax.broadcasted_iota(jnp.int32, (tr, HP), 1)
        is_rope = (lane >= NOPE) & (lane < MQK)
        kpe = jnp.where(is_rope, ms_ref[...].astype(F32), 0.0)
        dkpe = jnp.zeros((tr, HP), F32)
        dqng = jnp.zeros((1, HP), F32)
        dkng = jnp.zeros((1, HP), F32)
        for h in range(MH):
            cols = slice(h * HP, (h + 1) * HP)
            qh = _dot(cqn, wuq_ref[:, cols], NN)
            rq = lax.rsqrt(jnp.sum(qh * qh, axis=1, keepdims=True) * (1.0 / MQK) + EPS)
            xq = qh * rq
            dy = dq_ref[:, cols].astype(F32)
            dqn = dy * cos - _partner(dy) * sin
            dqng += jnp.sum(dqn * xq, axis=0, keepdims=True)
            tq = dqn * qng[...]
            dqf[:, cols] = (rq * (tq - xq * (jnp.sum(tq * xq, axis=1, keepdims=True) * (1.0 / MQK)))).astype(BF16)
            kh = _dot(ckvn, wukv_ref[:, cols], NN) + kpe
            rk = lax.rsqrt(jnp.sum(kh * kh, axis=1, keepdims=True) * (1.0 / MQK) + EPS)
            xk = kh * rk
            dy = dk_ref[:, cols].astype(F32)
            dkn = dy * cos - _partner(dy) * sin
            dkng += jnp.sum(dkn * xk, axis=0, keepdims=True)
            tk = dkn * kng[...]
            dkh = rk * (tk - xk * (jnp.sum(tk * xk, axis=1, keepdims=True) * (1.0 / MQK)))
            dkvf[:, cols] = jnp.where(lane < NOPE, dkh, 0.0).astype(BF16)
            dkpe += jnp.where(is_rope, dkh, 0.0)
        dkvf[:, MH * HP:] = dv_ref[...]
        acc[2:3, 0:HP] += dqng
        acc[3:4, 0:HP] += dkng
        dms_ref[...] = (dmg_ref[...] + dkpe).astype(BF16)

        dqfv = dqf[...]
        dwuq[...] += _dot(cqn, dqfv, TN)
        dcqn = _dot(dqfv, wuq_ref[...], NT)
        acc[0:1, :] += jnp.sum(dcqn * xc, axis=0, keepdims=True)
        tc = dcqn * qlg[...]
        dcq_ref[...] = (rc * (tc - xc * jnp.mean(tc * xc, axis=1, keepdims=True))).astype(BF16)

        dkvfv = dkvf[...]
        dwukv[...] += _dot(ckvn, dkvfv, TN)
        dckvn = _dot(dkvfv, wukv_ref[...], NT)
        acc[1:2, 0:MKVR] += jnp.sum(dckvn * xkv, axis=0, keepdims=True)
        tkv = dckvn * kvlg[...]
        dckv_ref[...] = (rkv * (tkv - xkv * jnp.mean(tkv * xkv, axis=1, keepdims=True))).astype(BF16)

    def full(a):
        return pl.BlockSpec(a.shape, lambda i: (0, 0))

    wide = pl.BlockSpec((tr, MH * HP), lambda i: (i, 0))
    narrow = pl.BlockSpec((tr, LANES), lambda i: (i, 0))
    return pl.pallas_call(
        body, name="mla_pre_bwd", grid=(t // tr,),
        in_specs=[wide, wide, wide, narrow,
                  pl.BlockSpec((tr, MQR), lambda i: (i, 20)), pl.BlockSpec((tr, MKVR), lambda i: (i, 42)),
                  pl.BlockSpec((tr, LANES), lambda i: (i, 43)), pl.BlockSpec((tr, 1), lambda i: (i, 0)),
                  full(fr), full(sg), full(q_lat_g), full(kv_lat_g), full(qn_g), full(kn_g), full(wuq), full(wukv)],
        out_specs=[pl.BlockSpec((tr, MQR), lambda i: (i, 0)), narrow, narrow,
                   pl.BlockSpec((MQR, MH * HP), lambda i: (0, 0)), pl.BlockSpec((MKVR, 2 * MH * HP), lambda i: (0, 0)),
                   pl.BlockSpec((8, MQR), lambda i: (0, 0))],
        out_shape=[jax.ShapeDtypeStruct((t, MQR), BF16), jax.ShapeDtypeStruct((t, MKVR), BF16),
                   jax.ShapeDtypeStruct((t, LANES), BF16), jax.ShapeDtypeStruct((MQR, MH * HP), F32),
                   jax.ShapeDtypeStruct((MKVR, 2 * MH * HP), F32), jax.ShapeDtypeStruct((8, MQR), F32)],
        scratch_shapes=[pltpu.VMEM((tr, MH * HP), BF16), pltpu.VMEM((tr, 2 * MH * HP), BF16)],
        compiler_params=_cparams(("arbitrary",)),
    )(dq2, dk2, dv2, dmisc_gla, proj, proj, proj, pos, fr, sg, q_lat_g, kv_lat_g, qn_g, kn_g, wuq, wukv)


ATT_FWD_TILES = (1024, 512)
ATT_BWD_TILES = (512, 512)
ATT_HEADS = 2
NEG = -1e30
LOG2E = 1.4426950408889634


def _att_mask(q0, k0, tq, tk):
    qc = (q0 + lax.broadcasted_iota(jnp.int32, (tq, tk), 0)) // CHUNK
    kc = (k0 + lax.broadcasted_iota(jnp.int32, (tq, tk), 1)) // CHUNK
    return kc <= qc


def _att_tiles(s, tiles):
    return _tile(s, tiles[0]), _tile(s, tiles[1])


def _lanes(x, n):
    return x if n == 1 else jnp.concatenate([x] * n, axis=1)


def _grid_ends(grid):
    i, j = pl.program_id(0), pl.program_id(1)
    return (i == 0) & (j == 0), (i == grid[0] - 1) & (j == grid[1] - 1)


def _attn_fwd(q2, k2, v2, bsz, slab):
    t = q2.shape[0]
    s = t // bsz
    tq, tk = _att_tiles(s, ATT_FWD_TILES)
    nq, groups, n_diag = s // tq, tk // HP, max(tq // tk, 1)
    sub_rows = tq // n_diag
    scale = MQK ** -0.5
    c2 = scale * LOG2E
    heads = range(ATT_HEADS)

    def body(q_ref, k_ref, v_ref, slab_ref, o_ref, lse_ref, gath_ref, send_sems, recv_sems, local_sem):
        gather = _core_row_gather_copies(slab_ref, gath_ref, send_sems, recv_sems, local_sem)
        first, last = _grid_ends((bsz, MH // ATT_HEADS))
        pl.when(first)(gather.start)

        def q_loop(qi, carry):
            q0 = pl.multiple_of(qi * tq, tq)
            rows = pl.ds(q0, tq)
            n_full = q0 // tk
            qs = [q_ref[rows, h * HP:(h + 1) * HP] for h in heads]

            def scores(h, kj, sub=None, masked=False):
                k0 = pl.multiple_of(kj * tk, tk)
                qv = qs[h] if sub is None else qs[h][sub * sub_rows:(sub + 1) * sub_rows]
                sc = _dot(qv, k_ref[pl.ds(k0, tk), h * HP:(h + 1) * HP], NT)
                return jnp.where(_att_mask(q0 + sub * sub_rows, k0, sub_rows, tk), sc, NEG) if masked else sc

            def fold(mx, sc):
                for j in range(groups):
                    mx = jnp.maximum(mx, sc[:, j * HP:(j + 1) * HP])
                return mx

            def over_diagonal(vals, step):
                out = []
                for h in heads:
                    blocks = []
                    for r in range(n_diag):
                        v = vals[h][r * sub_rows:(r + 1) * sub_rows]
                        for u in range(r + 1):
                            v = step(v, h, n_full + u, r, u == r)
                        blocks.append(v)
                    out.append(blocks[0] if n_diag == 1 else jnp.concatenate(blocks, axis=0))
                return tuple(out)

            mx = lax.fori_loop(0, n_full, lambda kj, mx: tuple(fold(mx[h], scores(h, kj)) for h in heads),
                               tuple(jnp.full((tq, HP), NEG, F32) for _ in heads))
            mx = over_diagonal(mx, lambda v, h, kj, r, masked: fold(v, scores(h, kj, r, masked)))
            mb = [jnp.broadcast_to(jnp.max(mx[h], axis=1, keepdims=True), (tq, HP)) for h in heads]

            def weighted(h, kj, sub=None, masked=False):
                m = mb[h] if sub is None else mb[h][sub * sub_rows:(sub + 1) * sub_rows]
                p = jnp.exp2((scores(h, kj, sub, masked) - _lanes(m, groups)) * c2)
                k0 = pl.multiple_of(kj * tk, tk)
                return _dot(p.astype(BF16), v_ref[pl.ds(k0, tk), h * HP:(h + 1) * HP], NN)

            acc = lax.fori_loop(0, n_full, lambda kj, acc: tuple(acc[h] + weighted(h, kj) for h in heads),
                                tuple(jnp.zeros((tq, HP), F32) for _ in heads))
            acc = over_diagonal(acc, lambda v, h, kj, r, masked: v + weighted(h, kj, r, masked))
            lane = lax.broadcasted_iota(jnp.int32, (tq, HP), 1)
            for h in heads:
                a = acc[h]
                l = jnp.sum(jnp.where(lane == MV, a, 0.0), axis=1, keepdims=True)
                o_ref[rows, h * HP:(h + 1) * HP] = (a / l).astype(BF16)
                lse_ref[rows, h * HP:(h + 1) * HP] = mb[h] * scale + jnp.log(l)
            return carry

        lax.fori_loop(0, nq, q_loop, 0)
        pl.when(last)(gather.finish)

    spec = pl.BlockSpec((s, ATT_HEADS * HP), lambda b, h: (b, h))
    return pl.pallas_call(
        body, name="attn_fwd", grid=(bsz, MH // ATT_HEADS), in_specs=[spec] * 3 + [HBM_SPEC],
        out_specs=[spec, spec, HBM_SPEC],
        out_shape=[jax.ShapeDtypeStruct((t, MH * HP), BF16), jax.ShapeDtypeStruct((t, MH * HP), F32),
                   jax.ShapeDtypeStruct((N_DEV,) + slab.shape, slab.dtype)],
        scratch_shapes=EXCHANGE_SEMS, compiler_params=_cparams(("arbitrary", "arbitrary")),
    )(q2, k2, v2, slab)


def _attn_bwd(q2, k2, v2, do2, o2, lse2, bsz, tsum):
    t = q2.shape[0]
    s = t // bsz
    tq, tk = _att_tiles(s, ATT_BWD_TILES)
    nq, nk, per, groups = s // tq, s // tk, max(tk // tq, 1), tk // HP
    scale = MQK ** -0.5
    c2 = scale * LOG2E
    heads = range(ATT_HEADS)

    def body(q_ref, k_ref, v_ref, do_ref, o_ref, lse_ref, t_ref, dq_ref, dk_ref, dv_ref, parts_ref, dq_acc, delta, lse_b2,
             send_sems, recv_sems, local_sem):
        exchange = _all_to_all_copies(t_ref, parts_ref, send_sems, recv_sems, local_sem)
        first, last = _grid_ends((bsz, MH // ATT_HEADS))
        pl.when(first)(exchange.start)
        dq_acc[...] = jnp.zeros_like(dq_acc)

        def d_loop(i, carry):
            rows = pl.ds(pl.multiple_of(i * tq, tq), tq)
            for h in heads:
                hs = slice(h * HP, (h + 1) * HP)
                dl = jnp.sum(do_ref[rows, hs].astype(F32) * o_ref[rows, hs].astype(F32), axis=1, keepdims=True)
                delta[rows, hs] = jnp.broadcast_to(dl, (tq, HP))
            lse_b2[rows, :] = lse_ref[rows, :] * LOG2E
            return carry

        lax.fori_loop(0, nq, d_loop, 0)

        def k_loop(kj, carry):
            k0 = pl.multiple_of(kj * tk, tk)
            kk = [k_ref[pl.ds(k0, tk), h * HP:(h + 1) * HP] for h in heads]
            vv = [v_ref[pl.ds(k0, tk), h * HP:(h + 1) * HP] for h in heads]

            def tile(qi, c, masked):
                q0 = pl.multiple_of(qi * tq, tq)
                rows = pl.ds(q0, tq)
                out = []
                for h in heads:
                    hs = slice(h * HP, (h + 1) * HP)
                    dk, dv = c[h]
                    q = q_ref[rows, hs]
                    do = do_ref[rows, hs]
                    e = _dot(q, kk[h], NT) * c2 - _lanes(lse_b2[rows, hs], groups)
                    if masked:
                        e = jnp.where(_att_mask(q0, k0, tq, tk), e, NEG)
                    p = jnp.exp2(e)
                    dv = dv + _dot(p.astype(BF16), do, TN)
                    ds = (p * (_dot(do, vv[h], NT) - _lanes(delta[rows, hs], groups))).astype(BF16)
                    dq_acc[rows, hs] += _dot(ds, kk[h], NN)
                    dk = dk + _dot(ds, q, TN)
                    out.append((dk, dv))
                return tuple(out)

            zero = jnp.zeros((tk, HP), F32)
            c = tuple((zero, zero) for _ in heads)
            first = k0 // tq
            for u in range(per):
                c = tile(first + u, c, True)
            c = lax.fori_loop(first + per, nq, lambda qi, c: tile(qi, c, False), c)
            for h in heads:
                dk_ref[pl.ds(k0, tk), h * HP:(h + 1) * HP] = (c[h][0] * scale).astype(BF16)
                dv_ref[pl.ds(k0, tk), h * HP:(h + 1) * HP] = c[h][1].astype(BF16)
            return carry

        lax.fori_loop(0, nk, k_loop, 0)
        dq_ref[...] = (dq_acc[...] * scale).astype(BF16)
        pl.when(last)(exchange.finish)

    spec = pl.BlockSpec((s, ATT_HEADS * HP), lambda b, h: (b, h))
    return pl.pallas_call(
        body, name="attn_bwd", grid=(bsz, MH // ATT_HEADS), in_specs=[spec] * 6 + [HBM_SPEC],
        out_specs=[spec] * 3 + [HBM_SPEC],
        out_shape=[jax.ShapeDtypeStruct((t, MH * HP), BF16)] * 3 + [jax.ShapeDtypeStruct(tsum.shape, tsum.dtype)],
        scratch_shapes=[pltpu.VMEM((s, ATT_HEADS * HP), F32)] * 3 + EXCHANGE_SEMS,
        compiler_params=_cparams(("arbitrary", "arbitrary")),
    )(q2, k2, v2, do2, o2, lse2, tsum)


def _perm_w_in_t(w):
    z = lambda n: jnp.zeros((n, w.shape[1]), w.dtype)
    return jnp.concatenate([w[:3072], w[3504:5552], w[3088:3344], w[3344:3472], w[3072:3088], z(48), w[3472:3504], z(32)],
                           axis=0)


def _unperm_w_in_t(g):
    return jnp.concatenate([g[:3072], g[5504:5520], g[5120:5376], g[5376:5504], g[5568:5600], g[3072:5120]], axis=0)


def _pad_wa(w):
    return jnp.pad(w, ((0, LANES - GLR), (0, 0)))


def _pad_wuq(w):
    return jnp.pad(w.reshape(MQR, MH, MQK), ((0, 0), (0, 0), (0, HP - MQK))).reshape(MQR, MH * HP)


def _unpad_wuq(g):
    return g.reshape(MQR, MH, HP)[:, :, :MQK].reshape(MQR, MH * MQK)


def _pad_wukv(w):
    w3 = w.reshape(MKVR, MH, NOPE + MV)
    kp = jnp.pad(w3[:, :, :NOPE], ((0, 0), (0, 0), (0, HP - NOPE))).reshape(MKVR, MH * HP)
    vp = jnp.pad(w3[:, :, NOPE:], ((0, 0), (0, 0), (0, HP - MV))).reshape(MKVR, MH * HP)
    return jnp.concatenate([kp, vp], axis=1)


def _unpad_wukv(g):
    kp = g[:, :MH * HP].reshape(MKVR, MH, HP)[:, :, :NOPE]
    vp = g[:, MH * HP:].reshape(MKVR, MH, HP)[:, :, :MV]
    return jnp.concatenate([kp, vp], axis=2).reshape(MKVR, MH * (NOPE + MV))


def _pad_wo(w):
    return jnp.pad(w.reshape(MH, MV, D), ((0, 0), (0, HP - MV), (0, 0))).reshape(MH * HP, D)


def _unpad_wo(g):
    return g.reshape(MH, HP, D)[:, :MV, :].reshape(MH * MV, D)


def _repeat_half(a):
    a3 = a.reshape(a.shape[0], -1, HP)
    a3 = jnp.concatenate([a3[:, :, :MQK], a3[:, :, NOPE:NOPE + ROPE // 2], a3[:, :, MQK + ROPE // 2:]], axis=2)
    return a3.reshape(a.shape)


def _pad_lanes(v, n=HP):
    return jnp.pad(v, ((0, 0), (0, n - v.shape[1])))


def _local_step(x, positions, tgt, wt, slab_b, sp, mod3):
    bsz, s, _ = x.shape
    t = bsz * s
    x2 = x.reshape(t, D)
    tgt2 = tgt.reshape(t, D)
    pos = positions.reshape(t, 1)
    fr16 = ROPE_THETA ** (-jnp.arange(0, ROPE, 2, dtype=F32) / ROPE)
    zero = lambda n: jnp.zeros((n,), F32)
    fr = jnp.concatenate([zero(NOPE), fr16, fr16, zero(HP - MQK)]).reshape(1, HP)
    sg = jnp.concatenate([zero(NOPE), -jnp.ones((ROPE // 2,), F32), jnp.ones((ROPE // 2,), F32), zero(HP - MQK)]).reshape(1, HP)

    w_in_t = _perm_w_in_t(wt["w_in"])
    wa_pad = _pad_wa(wt["gla_w_alpha"].T)
    wuq = _pad_wuq(wt["mla_w_uq"].T)
    wukv = _pad_wukv(wt["mla_w_ukv"].T)
    qn_g, kn_g = _pad_lanes(sp["mla_qn_g"]), _pad_lanes(sp["mla_kn_g"])

    _, h = _norm_mod_fwd(x2, sp["norm1_g"], mod3, 0, 1, "norm1_fwd")
    proj = _mm(h, w_in_t, "nt", (BF16,), "proj_fwd", tn=PW // 4)
    o_gla, og, states = _gla_fwd(proj, wa_pad, sp["gla_b_alpha"], sp["gla_out_norm_g"], bsz)
    q2, k2, v2 = _mla_pre_fwd(proj, pos, fr, sg, sp["mla_q_lat_g"], sp["mla_kv_lat_g"], _repeat_half(qn_g), _repeat_half(kn_g),
                              _repeat_half(wuq), wukv)
    o2, lse2, core_row = _attn_fwd(q2, k2, v2, bsz, slab_b)
    wt = dict(wt, **_unpack_gathered(_cross_core_fill(core_row), SLAB_B))
    wo_pad = _pad_wo(wt["mla_w_o"])
    y_a = _mm(og, wt["gla_w_o"], "nn", (BF16,), "gla_out_fwd")
    y_b = _mm(o2, wo_pad, "nn", (BF16,), "mla_out_fwd")
    mix = _merge_fwd(proj, sp["b_merge"], y_a, y_b)
    mixed = _mm(mix, wt["w_out"], "nn", (F32,), "w_out_fwd")

    x1, h2 = _norm_mod_fwd(x2, sp["norm2_g"], mod3, 3, 4, "norm2_fwd", mixed=mixed, i_gate=2)
    a, f = _mm(h2, wt["mlp_w1"], "nt", (BF16, BF16), "mlp1_fwd",
               epi=lambda acc: (acc, jnp.square(jnp.maximum(acc, 0.0))))
    ff = _mm(f, wt["mlp_w2"], "nn", (F32,), "mlp2_fwd")
    dy, dff, acc_g2, acc_loss = _loss_head(x1, ff, tgt2, mod3)

    gw = {}
    gw["mlp_w2"] = _mm(f, dff, "tn", (BF16,), "mlp2_dw")
    da = _mm(dff, wt["mlp_w2"], "nt", (BF16,), "mlp2_dx", extras=(a,),
             epi=lambda acc, av: (acc * (2.0 * jnp.maximum(av.astype(F32), 0.0)),))
    gw["mlp_w1"] = _mm(da, h2, "tn", (BF16,), "mlp1_dw")
    dh2 = _mm(da, wt["mlp_w1"], "nn", (F32,), "mlp1_dx")
    dx1, dmixed, accb2, accg2 = _norm_mod_bwd(x1, dh2, dy, sp["norm2_g"], mod3, 3, 4, "norm2_bwd", mixed=mixed, i_gate=2)

    gw["w_out"] = _mm(mix, dmixed, "tn", (BF16,), "w_out_dw")
    dmix = _mm(dmixed, wt["w_out"], "nt", (BF16,), "w_out_dx")
    dy_a, dy_b, dlogits, acc_bm = _merge_bwd(dmix, proj, sp["b_merge"], y_a, y_b)
    gw["gla_w_o"] = _mm(og, dy_a, "tn", (BF16,), "gla_out_dw")
    dog = _mm(dy_a, wt["gla_w_o"], "nt", (BF16,), "gla_out_dx")
    gw["mla_w_o"] = _unpad_wo(_mm(o2, dy_b, "tn", (BF16,), "mla_out_dw"))
    do2 = _mm(dy_b, wo_pad, "nt", (BF16,), "mla_out_dx")
    dq2, dk2, dv2, parts_b = _attn_bwd(q2, k2, v2, do2, o2, lse2, bsz, _pack_per_device(gw, SLAB_B))
    dq_g, dk_g, dv_g, dgg, dz, acc_ba, acc_go = _gla_bwd(dog, o_gla, proj, wa_pad, sp["gla_b_alpha"],
                                                         sp["gla_out_norm_g"], states, bsz)
    gw["gla_w_alpha"] = _mm(proj, dz, "tn", (F32,), "gla_alpha_dw", a_off=43, m=LANES)[:GLR].T.astype(BF16)
    dmisc_gla = _mm(dz, wa_pad, "nt", (F32,), "gla_alpha_dx")
    dcq, dckv, dmisc, gwuq, gwukv, acc_mla = _mla_pre_bwd(dq2, dk2, dv2, dmisc_gla, proj, pos, fr, sg, sp["mla_q_lat_g"],
                                                         sp["mla_kv_lat_g"], qn_g, kn_g, wuq, wukv)
    gw["mla_w_uq"] = _unpad_wuq(gwuq).T.astype(BF16)
    gw["mla_w_ukv"] = _unpad_wukv(gwukv).T.astype(BF16)
    dproj = jnp.concatenate([dq_g, dk_g, dv_g, dgg, dlogits, dcq, dckv, dmisc], axis=1)
    gw["w_in"] = _unperm_w_in_t(_mm(dproj, h, "tn", (BF16,), "proj_dw", tm=PW // 4))
    tm = _tile(s, 1024)
    parts_a, grad_x, accb1, accg1 = _mm(dproj, w_in_t, "nn", (), "proj_dx", tm=tm, tk=PW // 4,
                                        cargo=_sum_over_cores(_pack_per_device(gw, SLAB_A), "a"),
                                        tail=_norm_bwd_tail(x2, dx1, sp["norm1_g"], mod3, 1, tm))

    dmod = jnp.stack([accb1[:, 0], accb1[:, 1], accb2[:, 2], accb2[:, 0], accb2[:, 1], acc_g2[:, 0]], axis=1)

    rows = {
        "dmod": dmod.reshape(bsz * 6, D),
        "norm1_g": accg1[0:1], "norm2_g": accg2[0:1],
        "b_merge": acc_bm[0:1].reshape(2, D),
        "gla_b_alpha": _pad_lanes(acc_ba[0:1], D),
        "gla_out_norm_g": _pad_lanes(acc_go.reshape(GH, 8, GDV)[:, 0, :], D),
        "mla_q_lat_g": _pad_lanes(acc_mla[0:1], D), "mla_kv_lat_g": _pad_lanes(acc_mla[1:2], D),
        "mla_qn_g": _pad_lanes(acc_mla[2:3], D), "mla_kn_g": _pad_lanes(acc_mla[3:4], D),
        "loss": acc_loss[0:1],
    }
    return grad_x.reshape(bsz, s, D), parts_a, parts_b, rows


HBM_SPEC = pl.BlockSpec(memory_space=pltpu.HBM)


def _all_gather(ps, name):
    n = len(ps)

    def body(*refs):
        p_refs, out_refs, (send_sems, recv_sems, local_sems) = refs[:n], refs[n:2 * n], refs[2 * n:]
        x, y, c = lax.axis_index("x"), lax.axis_index("y"), lax.axis_index("c")
        me, sibling = (x, y, c), (x, y, 1 - c)
        chips = [(1 - x, y), (x, 1 - y), (1 - x, 1 - y)]

        def copy(a, k, block, to, own=False):
            slot = out_refs[a].at[4 * block[0] + 2 * block[1] + block[2]]
            return pltpu.make_async_remote_copy(
                src_ref=p_refs[a] if own else slot, dst_ref=slot, send_sem=send_sems.at[7 * a + k],
                recv_sem=recv_sems.at[7 * a + k], device_id=to, device_id_type=MESH)

        mine = [pltpu.make_async_copy(p_refs[a], out_refs[a].at[4 * x + 2 * y + c], local_sems.at[a]) for a in range(n)]
        first = [copy(a, 0, me, sibling, own=True) for a in range(n)]
        first += [copy(a, 1 + j, me, (*chip, c), own=True) for a in range(n) for j, chip in enumerate(chips)]
        for cp in mine + first:
            cp.start()
        passed = []
        for j, chip in enumerate(chips):
            for a in range(n):
                copy(a, 1 + j, (*chip, c), me).wait_recv()
                passed.append(copy(a, 4 + j, (*chip, c), sibling))
                passed[-1].start()
        for a in range(n):
            copy(a, 0, sibling, me).wait_recv()
            for j, chip in enumerate(chips):
                copy(a, 4 + j, (*chip, 1 - c), me).wait_recv()
        for cp in first + passed:
            cp.wait_send()
        for cp in mine:
            cp.wait()

    return pl.pallas_call(
        body, name=name, out_shape=[jax.ShapeDtypeStruct((N_DEV,) + p.shape, p.dtype) for p in ps],
        in_specs=[HBM_SPEC] * n, out_specs=[HBM_SPEC] * n,
        scratch_shapes=[pltpu.SemaphoreType.DMA((7 * n,)), pltpu.SemaphoreType.DMA((7 * n,)), pltpu.SemaphoreType.DMA((n,))],
    )(*ps)


def _sibling_exchange(g, name):
    def body(g_ref, out_ref, send_sem, recv_sem):
        x, y, c = lax.axis_index("x"), lax.axis_index("y"), lax.axis_index("c")
        cp = pltpu.make_async_remote_copy(src_ref=g_ref, dst_ref=out_ref, send_sem=send_sem, recv_sem=recv_sem,
                                          device_id=(x, y, 1 - c), device_id_type=MESH)
        cp.start()
        cp.wait()

    return pl.pallas_call(
        body, name=name, out_shape=jax.ShapeDtypeStruct(g.shape, g.dtype),
        in_specs=[HBM_SPEC], out_specs=HBM_SPEC,
        scratch_shapes=[pltpu.SemaphoreType.DMA(()), pltpu.SemaphoreType.DMA(())],
    )(g)


class _Exchange:
    def __init__(self, local, sends, arrivals):
        self.local, self.sends, self.arrivals = local, sends, arrivals

    def start(self):
        self.local.start()
        for cp in self.sends:
            cp.start()

    def finish(self):
        for cp in self.arrivals:
            cp.wait_recv()
        for cp in self.sends:
            cp.wait_send()
        self.local.wait()


EXCHANGE_SEMS = [pltpu.SemaphoreType.DMA((N_DEV,)), pltpu.SemaphoreType.DMA((N_DEV,)), pltpu.SemaphoreType.DMA(())]


def _all_to_all_copies(t_ref, out_ref, send_sems, recv_sems, local_sem):
    x, y, c = lax.axis_index("x"), lax.axis_index("y"), lax.axis_index("c")
    me = 4 * x + 2 * y + c

    def copy(k, src, dst):
        px, py, pc = (1 - x if k & 4 else x), (1 - y if k & 2 else y), (1 - c if k & 1 else c)
        peer = 4 * px + 2 * py + pc
        return pltpu.make_async_remote_copy(src_ref=t_ref.at[peer if src is None else src],
                                            dst_ref=out_ref.at[peer if dst is None else dst], send_sem=send_sems.at[k],
                                            recv_sem=recv_sems.at[k], device_id=(px, py, pc), device_id_type=MESH)

    return _Exchange(pltpu.make_async_copy(t_ref.at[me], out_ref.at[me], local_sem),
                     [copy(k, None, me) for k in range(1, N_DEV)], [copy(k, me, None) for k in range(1, N_DEV)])


def _chip_exchange_copies(t_ref, out_ref, send_sems, recv_sems, local_sem):
    x, y, c = lax.axis_index("x"), lax.axis_index("y"), lax.axis_index("c")
    my_chip = 2 * x + y
    chips = [(1 - x, y), (x, 1 - y), (1 - x, 1 - y)]

    def copy(j, src, dst, px, py):
        return pltpu.make_async_remote_copy(src_ref=t_ref.at[src], dst_ref=out_ref.at[dst], send_sem=send_sems.at[j],
                                            recv_sem=recv_sems.at[j], device_id=(px, py, c), device_id_type=MESH)

    return _Exchange(pltpu.make_async_copy(t_ref.at[my_chip], out_ref.at[my_chip], local_sem),
                     [copy(j, 2 * px + py, my_chip, px, py) for j, (px, py) in enumerate(chips)],
                     [copy(j, my_chip, 2 * px + py, px, py) for j, (px, py) in enumerate(chips)])


def _core_row_gather_copies(p_ref, out_ref, send_sems, recv_sems, local_sem):
    x, y, c = lax.axis_index("x"), lax.axis_index("y"), lax.axis_index("c")
    peers = [(x, y, 1 - c), (1 - x, y, c), (x, 1 - y, c), (1 - x, 1 - y, c)]

    def slot(px, py, pc):
        return out_ref.at[4 * px + 2 * py + pc]

    def copy(j, block, to):
        return pltpu.make_async_remote_copy(src_ref=p_ref, dst_ref=slot(*block), send_sem=send_sems.at[j],
                                            recv_sem=recv_sems.at[j], device_id=to, device_id_type=MESH)

    return _Exchange(pltpu.make_async_copy(p_ref, slot(x, y, c), local_sem),
                     [copy(j, (x, y, c), peer) for j, peer in enumerate(peers)],
                     [copy(j, peer, peer) for j, peer in enumerate(peers)])


def _cross_core_fill(gathered):
    def body(g_ref, out_ref, send_sems, recv_sems):
        x, y, c = lax.axis_index("x"), lax.axis_index("y"), lax.axis_index("c")
        chips = [(1 - x, y), (x, 1 - y), (1 - x, 1 - y)]

        def copy(j, pc):
            px, py = chips[j]
            slot = 4 * px + 2 * py + pc
            return pltpu.make_async_remote_copy(src_ref=g_ref.at[slot], dst_ref=out_ref.at[slot], send_sem=send_sems.at[j],
                                                recv_sem=recv_sems.at[j], device_id=(x, y, 1 - c), device_id_type=MESH)

        sends = [copy(j, c) for j in range(3)]
        for cp in sends:
            cp.start()
        for j in range(3):
            copy(j, 1 - c).wait_recv()
        for cp in sends:
            cp.wait_send()

    return pl.pallas_call(
        body, name="weights_cross_core_fill", out_shape=jax.ShapeDtypeStruct(gathered.shape, gathered.dtype),
        in_specs=[HBM_SPEC], out_specs=HBM_SPEC, input_output_aliases={0: 0},
        scratch_shapes=[pltpu.SemaphoreType.DMA((3,)), pltpu.SemaphoreType.DMA((3,))],
    )(gathered)


def _slab_block(r):
    return max(b for b in range(16, SLAB_BLOCK_MAX + 1, 16) if r % b == 0)


def _pair_sum(a, b, name):
    n, r, cdim = a.shape
    rb = _slab_block(r)
    blk = pl.BlockSpec((1, rb, cdim), lambda j, i: (j, i, 0))

    def body(a_ref, b_ref, o_ref):
        o_ref[...] = (a_ref[...].astype(F32) + b_ref[...].astype(F32)).astype(BF16)

    return pl.pallas_call(
        body, name=name, grid=(n, r // rb), in_specs=[blk, blk], out_specs=blk,
        out_shape=jax.ShapeDtypeStruct(a.shape, BF16), compiler_params=_cparams(("arbitrary", "arbitrary")),
    )(a, b)


def _adamw_math(w, g, m, v):
    m = ADAM_B1 * m + (1.0 - ADAM_B1) * g
    v = ADAM_B2 * v + (1.0 - ADAM_B2) * jnp.square(g)
    m_hat = m / (1.0 - ADAM_B1 ** ADAM_STEP)
    v_hat = v / (1.0 - ADAM_B2 ** ADAM_STEP)
    delta = -ADAM_LR * (m_hat / (jnp.sqrt(v_hat) + ADAM_EPS) + ADAM_WD * w)
    return delta, m, v


def _slab_sum(parts, name):
    n, r, cdim = parts.shape
    rb = _slab_block(r)
    blk = pl.BlockSpec((rb, cdim), lambda i: (i, 0))

    def body(p_ref, g_out):
        g = p_ref[0].astype(F32)
        for j in range(1, n):
            g = g + p_ref[j].astype(F32)
        g_out[...] = g

    return pl.pallas_call(
        body, name=name, grid=(r // rb,),
        in_specs=[pl.BlockSpec((n, rb, cdim), lambda i: (0, i, 0))], out_specs=blk,
        out_shape=jax.ShapeDtypeStruct((r, cdim), F32), compiler_params=_cparams(("arbitrary",)),
    )(parts)


def _adamw(g, w, m, v, name):
    r, cdim = w.shape
    rb = _tile(r, 256)
    blk = pl.BlockSpec((rb, cdim), lambda i: (i, 0))

    def body(g_ref, w_ref, m_ref, v_ref, d_out, m_out, v_out):
        d_out[...], m_out[...], v_out[...] = _adamw_math(w_ref[...], g_ref[...], m_ref[...], v_ref[...])

    return pl.pallas_call(
        body, name=name, grid=(r // rb,), in_specs=[blk] * 4, out_specs=[blk] * 3,
        out_shape=[jax.ShapeDtypeStruct((r, cdim), F32)] * 3, compiler_params=_cparams(("arbitrary",)),
    )(g, w, m, v)


def _adamw_small(parts, w, m, v):
    def body(p_ref, w_ref, m_ref, v_ref, g_out, d_out, m_out, v_out, loss_out):
        def total(srcs):
            acc = None
            for r in srcs:
                for j in range(N_DEV):
                    term = p_ref[j, r:r + 1, :]
                    acc = term if acc is None else acc + term
            return acc

        for prow, srcs in enumerate(SMALL_SOURCES):
            one = slice(prow, prow + 1)
            g = total(srcs)
            g_out[one, :] = g
            d_out[one, :], m_out[one, :], v_out[one, :] = _adamw_math(w_ref[one, :], g, m_ref[one, :], v_ref[one, :])
        loss_out[...] = jnp.broadcast_to(jnp.sum(total(LOSS_SOURCE), axis=1, keepdims=True), (8, LANES))

    full = lambda shp: pl.BlockSpec(shp, lambda i: (0,) * len(shp))
    return pl.pallas_call(
        body, name="adamw_small", grid=(1,),
        in_specs=[full((N_DEV, SMALL_ROWS, D)), full((16, D)), full((16, D)), full((16, D))],
        out_specs=[full((16, D))] * 4 + [full((8, LANES))],
        out_shape=[jax.ShapeDtypeStruct((16, D), F32)] * 4 + [jax.ShapeDtypeStruct((8, LANES), F32)],
        compiler_params=_cparams(("arbitrary",)),
    )(parts, w, m, v)


def _to_slab(shard, form):
    if form == "N":
        return shard
    return shard.T if form == "T" else shard.T.reshape(-1, D)


def _from_slab(block, form, shard_shape):
    if form == "N":
        return block
    return block.T if form == "T" else block.reshape(shard_shape[1], shard_shape[0]).T


def _gathered_full(g, form, shard_shape):
    if form == "TR":
        return g.reshape(N_DEV * shard_shape[1], shard_shape[0])
    return g.reshape(N_DEV * g.shape[1], D)


def _pack_slab(shards, layout):
    return jnp.concatenate([jnp.zeros((r, D), BF16) if n is None else _to_slab(shards[n], form).astype(BF16)
                            for n, r, form in layout], axis=0)


def _unpack_gathered(gathered, layout):
    out, off = {}, 0
    for n, r, form in layout:
        if n is not None:
            out[n] = _gathered_full(gathered[:, off:off + r], form, SHARD_SHAPES[n])
        off += r
    return out


def _pack_per_device(gw, layout):
    return jnp.concatenate([jnp.zeros((N_DEV, r, D), BF16) if n is None else gw[n].reshape(N_DEV, r, D)
                            for n, r, _ in layout], axis=1)


def _adamw_shards(gslab, layout, wts, mom, var):
    out, off = {}, 0
    for n, r, form in layout:
        if n is not None:
            g = _from_slab(gslab[off:off + r], form, SHARD_SHAPES[n])
            out[n] = (g,) + tuple(_adamw(g, wts[n], mom[n], var[n], "adamw_" + n))
        off += r
    return out


def _sum_over_cores(per_dev, tag):
    my_c = lax.axis_index("c")
    pairs = per_dev.reshape(4, 2, per_dev.shape[1], D)
    keep = lax.dynamic_index_in_dim(pairs, my_c, axis=1, keepdims=False)
    give = lax.dynamic_index_in_dim(pairs, 1 - my_c, axis=1, keepdims=False)
    return _pair_sum(keep, _sibling_exchange(give, "rs_sibling_exchange_" + tag), "rs_pair_sum_" + tag)


def _small_pack(vals):
    rows = []
    for n in SMALL:
        v = vals[n].reshape(-1)
        k = -(-v.shape[0] // D)
        rows.append(jnp.pad(v, (0, k * D - v.shape[0])).reshape(k, D))
    return jnp.concatenate(rows, axis=0)


def _small_unpack(packed, shapes):
    out = {}
    for n in SMALL:
        k = shapes[n][-1]
        r0 = SMALL_ROW[n]
        out[n] = packed[r0:r0 + -(-k // D)].reshape(-1)[:k].reshape(shapes[n])
    return out


def kernel(x, c, positions, w_ada, b_ada, norm1_g, w_in, b_merge, gla_w_alpha, gla_b_alpha, gla_out_norm_g, gla_w_o, mla_q_lat_g, mla_w_uq, mla_kv_lat_g, mla_w_ukv, mla_qn_g, mla_kn_g, mla_w_o, w_out, norm2_g, mlp_w1, mlp_w2, loss_target, m_w_ada, m_b_ada, m_norm1_g, m_w_in, m_b_merge, m_gla_w_alpha, m_gla_b_alpha, m_gla_out_norm_g, m_gla_w_o, m_mla_q_lat_g, m_mla_w_uq, m_mla_kv_lat_g, m_mla_w_ukv, m_mla_qn_g, m_mla_kn_g, m_mla_w_o, m_w_out, m_norm2_g, m_mlp_w1, m_mlp_w2, v_w_ada, v_b_ada, v_norm1_g, v_w_in, v_b_merge, v_gla_w_alpha, v_gla_b_alpha, v_gla_out_norm_g, v_gla_w_o, v_mla_q_lat_g, v_mla_w_uq, v_mla_kv_lat_g, v_mla_w_ukv, v_mla_qn_g, v_mla_kn_g, v_mla_w_o, v_w_out, v_norm2_g, v_mlp_w1, v_mlp_w2):
    args = dict(locals())
    wts = {n: args[n][0] for n in WEIGHTS}
    mom = {n: args["m_" + n][0] for n in WEIGHTS}
    var = {n: args["v_" + n][0] for n in WEIGHTS}
    my_c = lax.axis_index("c")
    my_dev = 4 * lax.axis_index("x") + 2 * lax.axis_index("y") + my_c
    bsz = x.shape[0]
    sp = {n: wts[n].reshape(1, -1) for n in SMALL}

    gathered_a, c_gathered = _all_gather([_pack_slab(wts, SLAB_A), jnp.pad(c, ((0, 8 - bsz), (0, 0)))], "weights_all_gather")
    wt = _unpack_gathered(gathered_a, SLAB_A)

    c_all = c_gathered[:, :bsz].reshape(N_DEV * bsz, D)
    bias = lax.dynamic_slice_in_dim(sp["b_ada"], my_dev * ADA_COLS, ADA_COLS, axis=1)
    mod_cols = _mm(c_all, wts["w_ada"], "nn", (F32,), "ada_fwd", pro=_silu, epi=lambda acc, b: (acc + b,),
                   extras=(jnp.broadcast_to(bias, (N_DEV * bsz, ADA_COLS)),))
    mod_all, = _all_gather([mod_cols], "mod_all_gather")
    mod_mine = lax.dynamic_slice_in_dim(mod_all, my_dev * bsz, bsz, axis=1)
    mod3 = jnp.transpose(mod_mine, (1, 0, 2)).reshape(bsz, 6, D)

    grad_x, parts_a, parts_b, rows = _local_step(x, positions, loss_target, wt, _pack_slab(wts, SLAB_B), sp, mod3)

    big = dict(_adamw_shards(_slab_sum(parts_a, "rs_slab_sum_a"), SLAB_A, wts, mom, var),
               **_adamw_shards(_slab_sum(parts_b, "rs_slab_sum_b"), SLAB_B, wts, mom, var))

    order = ["dmod", "norm1_g", "norm2_g", "b_merge", "gla_b_alpha", "gla_out_norm_g", "mla_q_lat_g", "mla_kv_lat_g",
             "mla_qn_g", "mla_kn_g", "loss"]
    part_rows = jnp.concatenate([rows[n] for n in order], axis=0)
    part_rows = jnp.pad(part_rows, ((0, SMALL_ROWS - part_rows.shape[0]), (0, 0)))
    all_rows, = _all_gather([part_rows], "partials_all_gather")

    dmod_all = all_rows[:, :6 * bsz].reshape(N_DEV * bsz, 6 * D)
    dmod_cols = lax.dynamic_slice_in_dim(dmod_all, my_dev * ADA_COLS, ADA_COLS, axis=1)
    g_ada = _mm(c_all, dmod_cols, "tn", (F32,), "ada_dw", pro=_silu)
    big["w_ada"] = (g_ada,) + tuple(_adamw(g_ada, wts["w_ada"], mom["w_ada"], var["w_ada"], "adamw_w_ada"))

    small = _adamw_small(all_rows, _small_pack({n: wts[n] for n in SMALL}), _small_pack({n: mom[n] for n in SMALL}),
                         _small_pack({n: var[n] for n in SMALL}))
    loss = small[4][0, 0]
    small_shapes = {n: wts[n].shape for n in SMALL}
    small = [_small_unpack(o, small_shapes) for o in small[:4]]

    outs = [loss, grad_x]
    for k in range(4):
        for n in WEIGHTS:
            val = big[n][k] if n in BIG else small[k][n]
            outs.append(val.reshape((1,) + tuple(wts[n].shape)))
    return tuple(outs)
```

```python
import functools

import jax
import jax.numpy as jnp
from jax import lax
from jax.experimental import pallas as pl
from jax.experimental.pallas import tpu as pltpu

F32 = jnp.float32
BF16 = jnp.bfloat16
MESH = pl.DeviceIdType.MESH

D = 1024
EPS = 1e-6
CHUNK = 64
GH, GDK, GDV, GLR, GTAU = 4, 128, 256, 16, 16.0
MH, MQR, MKVR, NOPE, ROPE, MV = 16, 256, 128, 64, 32, 64
MQK = NOPE + ROPE
HP = 128
FF = 4 * D
ROPE_THETA = 10000.0
IN_WIDTH = 5552
PW = 5632
N_DEV = 8
LANES = 128
SLAB_BLOCK_MAX = 400
ADA_COLS = 6 * D // N_DEV
SMALL_ROWS = 32
SMALL_SOURCES = tuple([(r, 6 + r) for r in range(6)] + [(12,), (13,), (14,), (15,), (16,), (17, 18, 19, 20),
                                                         (21,), (22,), (23,), (24,)])
LOSS_SOURCE = (25,)
VMEM_LIMIT = 56 * 1024 * 1024

ADAM_LR, ADAM_B1, ADAM_B2, ADAM_EPS, ADAM_WD, ADAM_STEP = 0.001, 0.9, 0.999, 1e-08, 0.01, 10

SLAB_A = (("w_in", 694, "T"), ("gla_w_alpha", 1, "TR"), (None, 9, None), ("mla_w_uq", 48, "TR"), ("mla_w_ukv", 32, "TR"))
SLAB_B = (("mlp_w1", 512, "T"), ("gla_w_o", 128, "N"), ("mla_w_o", 128, "N"), ("w_out", 128, "N"), ("mlp_w2", 512, "N"))
BIG = ("w_ada",) + tuple(n for n, _, _ in SLAB_A + SLAB_B if n is not None)
SHARD_SHAPES = {"w_ada": (D, 6 * D // N_DEV), "w_in": (D, IN_WIDTH // N_DEV), "gla_w_alpha": (GLR, GH * GDK // N_DEV),
                "gla_w_o": (GH * GDV // N_DEV, D), "mla_w_uq": (MQR, MH * MQK // N_DEV),
                "mla_w_ukv": (MKVR, MH * (NOPE + MV) // N_DEV), "mla_w_o": (MH * MV // N_DEV, D), "w_out": (D // N_DEV, D),
                "mlp_w1": (D, FF // N_DEV), "mlp_w2": (FF // N_DEV, D)}
SMALL = ("b_ada", "norm1_g", "norm2_g", "b_merge", "gla_b_alpha", "gla_out_norm_g", "mla_q_lat_g", "mla_kv_lat_g",
         "mla_qn_g", "mla_kn_g")
SMALL_ROW = {"b_ada": 0, "norm1_g": 6, "norm2_g": 7, "b_merge": 8, "gla_b_alpha": 10, "gla_out_norm_g": 11,
             "mla_q_lat_g": 12, "mla_kv_lat_g": 13, "mla_qn_g": 14, "mla_kn_g": 15}
WEIGHTS = ("w_ada", "b_ada", "norm1_g", "w_in", "b_merge", "gla_w_alpha", "gla_b_alpha", "gla_out_norm_g", "gla_w_o",
           "mla_q_lat_g", "mla_w_uq", "mla_kv_lat_g", "mla_w_ukv", "mla_qn_g", "mla_kn_g", "mla_w_o", "w_out",
           "norm2_g", "mlp_w1", "mlp_w2")


def _cparams(sem=None):
    return pltpu.CompilerParams(dimension_semantics=sem, vmem_limit_bytes=VMEM_LIMIT)


def _tile(n, pref):
    for t in (2048, PW // 4, 1024, 512, 256, 128):
        if t <= pref and n % t == 0:
            return t
    return n


def _dot(a, b, dims, precision=None):
    return lax.dot_general(a, b, (dims, ((), ())), preferred_element_type=F32, precision=precision)


NN = ((1,), (0,))
NT = ((1,), (1,))
TN = ((0,), (0,))


def _sigmoid(x):
    return 1.0 / (1.0 + jnp.exp(-x))


def _silu(x):
    return x * _sigmoid(x)


def _mm(a, b, mode, out_dtypes, name, *, pro=None, pro_b=None, epi=None, extras=(), a_off=0, m=None, tm=2048, tn=1024,
        tk=1024, cargo=None, tail=None):
    if mode == "tn":
        kc, n = b.shape
        m = a.shape[1] if m is None else m
    elif mode == "nn":
        m, kc = a.shape
        n = b.shape[1]
    else:
        m, kc = a.shape
        n = b.shape[0]
    tm, tn, tk = _tile(m, tm), _tile(n, tn), _tile(kc, tk)
    nk = kc // tk
    dims = {"nn": NN, "nt": NT, "tn": TN}[mode]
    if mode == "tn":
        a_spec = pl.BlockSpec((tk, tm), lambda i, j, k: (k, i + a_off))
    else:
        a_spec = pl.BlockSpec((tm, tk), lambda i, j, k: (i + a_off, k))
    if mode == "nt":
        b_spec = pl.BlockSpec((tn, tk), lambda i, j, k: (j, k))
    else:
        b_spec = pl.BlockSpec((tk, tn), lambda i, j, k: (k, j))
    o_spec = pl.BlockSpec((tm, tn), lambda i, j, k: (i, j))
    n_ex, n_out = len(extras), len(out_dtypes)
    grid = (m // tm, n // tn, nk)
    has_cargo = cargo is not None
    t_ins, t_outs = (tail["ins"], tail["outs"]) if tail else ([], [])
    assert not tail or grid[1] == 1

    def body(a_ref, b_ref, *rest):
        rest = list(rest)
        take = lambda count: [rest.pop(0) for _ in range(count)]
        ex, cargo_ref, tail_in = take(n_ex), take(has_cargo), take(len(t_ins))
        outs, parts_ref, tail_out = take(n_out), take(has_cargo), take(len(t_outs))
        acc = rest.pop(0)
        steps = [pl.program_id(axis) for axis in range(3)]
        if has_cargo:
            exchange = _chip_exchange_copies(cargo_ref[0], parts_ref[0], *rest)
            first = (steps[0] == 0) & (steps[1] == 0) & (steps[2] == 0)
            last = (steps[0] == grid[0] - 1) & (steps[1] == grid[1] - 1) & (steps[2] == grid[2] - 1)
            pl.when(first)(exchange.start)
        k = steps[2]

        @pl.when(k == 0)
        def _():
            acc[...] = jnp.zeros_like(acc)

        av = a_ref[...]
        if pro is not None:
            av = pro(av)
        bv = b_ref[...]
        if pro_b is not None:
            bv = pro_b(bv)
        acc[...] += _dot(av.astype(BF16), bv.astype(BF16), dims)

        @pl.when(k == nk - 1)
        def _():
            if tail:
                tail["fn"](acc[...], steps[0], *tail_in, *tail_out)
            res = (acc[...],) if epi is None else epi(acc[...], *[e[...] for e in ex])
            for o_ref, r in zip(outs, res):
                o_ref[...] = r.astype(o_ref.dtype)

        if has_cargo:
            pl.when(last)(exchange.finish)

    cargo_in = [cargo] if has_cargo else []
    cargo_spec = [HBM_SPEC] * len(cargo_in)
    sequential = has_cargo or bool(tail)
    out = pl.pallas_call(
        body, name=name, grid=grid,
        in_specs=[a_spec, b_spec] + [o_spec] * n_ex + cargo_spec + (tail["in_specs"] if tail else []),
        out_specs=[o_spec] * n_out + cargo_spec + (tail["out_specs"] if tail else []),
        out_shape=[jax.ShapeDtypeStruct((m, n), dt) for dt in out_dtypes]
        + [jax.ShapeDtypeStruct(c.shape, c.dtype) for c in cargo_in] + t_outs,
        scratch_shapes=[pltpu.VMEM((tm, tn), F32)] + (EXCHANGE_SEMS if has_cargo else []),
        compiler_params=_cparams(("arbitrary",) * 3 if sequential else ("parallel", "parallel", "arbitrary")),
    )(a, b, *extras, *cargo_in, *t_ins)
    return out[0] if len(out) == 1 else out


def _rows(s):
    return _tile(s, 512)


def _mod_spec():
    return pl.BlockSpec((1, 6, D), lambda b, i: (b, 0, 0))


def _tok_spec(tr, nb, width=D, col=0):
    return pl.BlockSpec((tr, width), lambda b, i: (b * nb + i, col))


def _norm_mod_fwd(x, g, mod3, i_shift, i_scale, name, mixed=None, i_gate=None):
    bsz, _, _ = mod3.shape
    t = x.shape[0]
    s = t // bsz
    tr = _rows(s)
    nb = s // tr
    has_res = mixed is not None

    def body(*refs):
        if has_res:
            x_ref, mx_ref, g_ref, mod_ref, x1_ref, h_ref = refs
            xv = x_ref[...] + mod_ref[0, i_gate:i_gate + 1, :] * mx_ref[...]
            x1_ref[...] = xv
        else:
            x_ref, g_ref, mod_ref, h_ref = refs
            xv = x_ref[...]
        r = lax.rsqrt(jnp.mean(xv * xv, axis=1, keepdims=True) + EPS)
        hn = (xv * r) * g_ref[...]
        h = hn * (1.0 + mod_ref[0, i_scale:i_scale + 1, :]) + mod_ref[0, i_shift:i_shift + 1, :]
        h_ref[...] = h.astype(BF16)

    tok = _tok_spec(tr, nb)
    gspec = pl.BlockSpec((1, D), lambda b, i: (0, 0))
    ins = [x] + ([mixed] if has_res else []) + [g, mod3]
    in_specs = [tok] + ([tok] if has_res else []) + [gspec, _mod_spec()]
    out_shape = ([jax.ShapeDtypeStruct((t, D), F32)] if has_res else []) + [jax.ShapeDtypeStruct((t, D), BF16)]
    out = pl.pallas_call(
        body, name=name, grid=(bsz, nb), in_specs=in_specs, out_specs=[tok] * len(out_shape), out_shape=out_shape,
        compiler_params=_cparams(("arbitrary", "arbitrary")),
    )(*ins)
    return (out[0], out[1]) if has_res else (None, out[0])


def _norm_bwd_rows(xv, dhv, dresv, gv, mod_ref, i_scale, accb, accg):
    r = lax.rsqrt(jnp.mean(xv * xv, axis=1, keepdims=True) + EPS)
    xn = xv * r
    accb[0, 0:1, :] += jnp.sum(dhv, axis=0, keepdims=True)
    accb[0, 1:2, :] += jnp.sum(dhv * (xn * gv), axis=0, keepdims=True)
    tt = dhv * (1.0 + mod_ref[0, i_scale:i_scale + 1, :])
    accg[0:1, :] += jnp.sum(tt * xn, axis=0, keepdims=True)
    dxn = tt * gv
    return dresv + r * (dxn - xn * jnp.mean(dxn * xn, axis=1, keepdims=True))


def _norm_bwd_tail(x, dres, g, mod3, i_scale, tm, mixed=None, i_gate=None):
    bsz, t = mod3.shape[0], x.shape[0]
    per_b = t // bsz // tm
    has_res = mixed is not None

    def fn(dhv, i, *refs):
        if has_res:
            x_ref, dres_ref, mx_ref, g_ref, mod_ref, dx_ref, dmx_ref, accb, accg = refs
        else:
            x_ref, dres_ref, g_ref, mod_ref, dx_ref, accb, accg = refs

        @pl.when(i % per_b == 0)
        def _():
            accb[...] = jnp.zeros_like(accb)

        @pl.when(i == 0)
        def _():
            accg[...] = jnp.zeros_like(accg)

        dx = _norm_bwd_rows(x_ref[...], dhv, dres_ref[...], g_ref[...], mod_ref, i_scale, accb, accg)
        dx_ref[...] = dx
        if has_res:
            accb[0, 2:3, :] += jnp.sum(dx * mx_ref[...], axis=0, keepdims=True)
            dmx_ref[...] = (dx * mod_ref[0, i_gate:i_gate + 1, :]).astype(BF16)

    tok = pl.BlockSpec((tm, D), lambda i, j, k: (i, 0))
    res = [mixed] if has_res else []
    return dict(
        fn=fn, ins=[x, dres] + res + [g, mod3],
        in_specs=[tok, tok] + [tok] * len(res) + [pl.BlockSpec((1, D), lambda i, j, k: (0, 0)), _tail_mod_spec(per_b)],
        outs=[jax.ShapeDtypeStruct((t, D), F32)] + [jax.ShapeDtypeStruct((t, D), BF16)] * len(res)
        + [jax.ShapeDtypeStruct((bsz, 8, D), F32), jax.ShapeDtypeStruct((8, D), F32)],
        out_specs=[tok] * (1 + len(res)) + [_tail_batch_spec(per_b), pl.BlockSpec((8, D), lambda i, j, k: (0, 0))])


def _tail_mod_spec(per_b):
    return pl.BlockSpec((1, 6, D), lambda i, j, k: (i // per_b, 0, 0))


def _tail_batch_spec(per_b):
    return pl.BlockSpec((1, 8, D), lambda i, j, k: (i // per_b, 0, 0))


def _loss_tail(x1, tgt, mod3, tm):
    bsz, t = mod3.shape[0], x1.shape[0]
    per_b = t // bsz // tm

    def fn(ffv, i, x1_ref, tg_ref, mod_ref, dy_ref, dff_ref, accb, accl):
        @pl.when(i % per_b == 0)
        def _():
            accb[...] = jnp.zeros_like(accb)

        @pl.when(i == 0)
        def _():
            accl[...] = jnp.zeros_like(accl)

        gate = mod_ref[0, 5:6, :]
        err = x1_ref[...] + gate * ffv - tg_ref[...]
        accl[0:1, :] += jnp.sum(err * err, axis=0, keepdims=True) * (0.5 / D)
        dy = err * (1.0 / D)
        dy_ref[...] = dy
        dff_ref[...] = (dy * gate).astype(BF16)
        accb[0, 0:1, :] += jnp.sum(dy * ffv, axis=0, keepdims=True)

    tok = pl.BlockSpec((tm, D), lambda i, j, k: (i, 0))
    return dict(
        fn=fn, ins=[x1, tgt, mod3], in_specs=[tok, tok, _tail_mod_spec(per_b)],
        outs=[jax.ShapeDtypeStruct((t, D), F32), jax.ShapeDtypeStruct((t, D), BF16),
              jax.ShapeDtypeStruct((bsz, 8, D), F32), jax.ShapeDtypeStruct((8, D), F32)],
        out_specs=[tok, tok, _tail_batch_spec(per_b), pl.BlockSpec((8, D), lambda i, j, k: (0, 0))])


def _merge_fwd(proj, b_merge, y_a, y_b):
    t = proj.shape[0]
    tr = _tile(t, 512)

    def body(la_ref, lb_ref, bm_ref, ya_ref, yb_ref, mix_ref):
        ga = _sigmoid(la_ref[...] + bm_ref[:, 0:D])
        gb = _sigmoid(lb_ref[...] + bm_ref[:, D:2 * D])
        mix_ref[...] = (ga * ya_ref[...].astype(F32) + gb * yb_ref[...].astype(F32)).astype(BF16)

    tok = pl.BlockSpec((tr, D), lambda i: (i, 0))
    return pl.pallas_call(
        body, name="merge_fwd", grid=(t // tr,),
        in_specs=[pl.BlockSpec((tr, D), lambda i: (i, 3)), pl.BlockSpec((tr, D), lambda i: (i, 4)),
                  pl.BlockSpec((1, 2 * D), lambda i: (0, 0)), tok, tok],
        out_specs=tok, out_shape=jax.ShapeDtypeStruct((t, D), BF16),
        compiler_params=_cparams(("arbitrary",)),
    )(proj, proj, b_merge, y_a, y_b)


def _merge_bwd(dmix, proj, b_merge, y_a, y_b):
    t = proj.shape[0]
    tr = _tile(t, 512)

    def body(dm_ref, la_ref, lb_ref, bm_ref, ya_ref, yb_ref, dya_ref, dyb_ref, dl_ref, acc):
        @pl.when(pl.program_id(0) == 0)
        def _():
            acc[...] = jnp.zeros_like(acc)

        dm = dm_ref[...].astype(F32)
        ga = _sigmoid(la_ref[...] + bm_ref[:, 0:D])
        gb = _sigmoid(lb_ref[...] + bm_ref[:, D:2 * D])
        dya_ref[...] = (dm * ga).astype(BF16)
        dyb_ref[...] = (dm * gb).astype(BF16)
        dla = dm * ya_ref[...].astype(F32) * ga * (1.0 - ga)
        dlb = dm * yb_ref[...].astype(F32) * gb * (1.0 - gb)
        dl_ref[:, 0:D] = dla.astype(BF16)
        dl_ref[:, D:2 * D] = dlb.astype(BF16)
        acc[0:1, 0:D] += jnp.sum(dla, axis=0, keepdims=True)
        acc[0:1, D:2 * D] += jnp.sum(dlb, axis=0, keepdims=True)

    tok = pl.BlockSpec((tr, D), lambda i: (i, 0))
    return pl.pallas_call(
        body, name="merge_bwd", grid=(t // tr,),
        in_specs=[tok, pl.BlockSpec((tr, D), lambda i: (i, 3)), pl.BlockSpec((tr, D), lambda i: (i, 4)),
                  pl.BlockSpec((1, 2 * D), lambda i: (0, 0)), tok, tok],
        out_specs=[tok, tok, pl.BlockSpec((tr, 2 * D), lambda i: (i, 0)), pl.BlockSpec((8, 2 * D), lambda i: (0, 0))],
        out_shape=[jax.ShapeDtypeStruct((t, D), BF16), jax.ShapeDtypeStruct((t, D), BF16),
                   jax.ShapeDtypeStruct((t, 2 * D), BF16), jax.ShapeDtypeStruct((8, 2 * D), F32)],
        compiler_params=_cparams(("arbitrary",)),
    )(dmix, proj, proj, b_merge, y_a, y_b)


GLA_HEADS = 2
GLA_UNROLL = 8


def _log_sigmoid(z):
    return jnp.minimum(z, 0.0) - jnp.log(1.0 + jnp.exp(-jnp.abs(z)))


def _tri(lower):
    r = lax.broadcasted_iota(jnp.int32, (CHUNK, CHUNK), 0)
    c = lax.broadcasted_iota(jnp.int32, (CHUNK, CHUNK), 1)
    return jnp.where(r >= c if lower else r <= c, 1.0, 0.0).astype(F32)


def _gla_fwd(proj, wa_pad, b_alpha, g_out, bsz):
    t = proj.shape[0]
    s = t // bsz
    nc = s // CHUNK
    p, kw, vw = GLA_HEADS, GLA_HEADS * GDK, GLA_HEADS * GDV

    def body(q_ref, k_ref, v_ref, gg_ref, ms_ref, wa_ref, ba_ref, go_ref, o_ref, og_ref, st_ref, la, state):
        z = _dot(ms_ref[...].astype(BF16), wa_ref[...], NN) + ba_ref[...]
        la[...] = _log_sigmoid(z) * (1.0 / GTAU)
        state[...] = jnp.zeros_like(state)
        low = _tri(True)
        gout = go_ref[...]

        def chunk(n, carry):
            rows = pl.ds(pl.multiple_of(n * CHUNK, CHUNK), CHUNK)
            for hh in range(p):
                kc, vc = slice(hh * GDK, (hh + 1) * GDK), slice(hh * GDV, (hh + 1) * GDV)
                lac = la[rows, kc]
                cum = _dot(low, lac, NN, lax.Precision.HIGHEST)
                ce = jnp.sum(lac, axis=0, keepdims=True)
                kd = (k_ref[rows, kc].astype(F32) * jnp.exp(ce - cum)).astype(BF16)
                new = state[vc, :] * jnp.exp(ce) + _dot(v_ref[rows, vc].astype(BF16), kd, TN)
                state[vc, :] = new
                st_ref[pl.ds(pl.multiple_of((hh * nc + n) * GDV, GDV), GDV), :] = new.astype(BF16)
                qs = (q_ref[rows, kc].astype(F32) * (GDK ** -0.5)).astype(BF16)
                o = _dot(qs, new.astype(BF16), NT)
                o_ref[rows, vc] = o
                ro = lax.rsqrt(jnp.mean(o * o, axis=1, keepdims=True) + EPS)
                og_ref[rows, vc] = (((o * ro) * gout) * _silu(gg_ref[rows, vc].astype(F32))).astype(BF16)
            return carry

        lax.fori_loop(0, nc, chunk, 0, unroll=GLA_UNROLL)

    return pl.pallas_call(
        body, name="gla_fwd", grid=(bsz, GH // p),
        in_specs=[pl.BlockSpec((s, kw), lambda b, h: (b, h)), pl.BlockSpec((s, kw), lambda b, h: (b, GH // p + h)),
                  pl.BlockSpec((s, vw), lambda b, h: (b, GH // p + h)), pl.BlockSpec((s, vw), lambda b, h: (b, 2 * GH // p + h)),
                  pl.BlockSpec((s, LANES), lambda b, h: (b, 43)),
                  pl.BlockSpec((LANES, kw), lambda b, h: (0, h)), pl.BlockSpec((1, kw), lambda b, h: (0, h)),
                  pl.BlockSpec((1, GDV), lambda b, h: (0, 0))],
        out_specs=[pl.BlockSpec((s, vw), lambda b, h: (b, h)), pl.BlockSpec((s, vw), lambda b, h: (b, h)),
                   pl.BlockSpec((p * nc * GDV, GDK), lambda b, h: (b * (GH // p) + h, 0))],
        out_shape=[jax.ShapeDtypeStruct((t, GH * GDV), F32), jax.ShapeDtypeStruct((t, GH * GDV), BF16),
                   jax.ShapeDtypeStruct((bsz * GH * nc * GDV, GDK), BF16)],
        scratch_shapes=[pltpu.VMEM((s, kw), F32), pltpu.VMEM((vw, GDK), F32)],
        compiler_params=_cparams(("arbitrary", "arbitrary")),
    )(proj, proj, proj, proj, proj, wa_pad, b_alpha, g_out)


def _gla_bwd(dog, o, proj, wa_pad, b_alpha, g_out, states, bsz):
    t = proj.shape[0]
    s = t // bsz
    nc = s // CHUNK
    p, kw, vw = GLA_HEADS, GLA_HEADS * GDK, GLA_HEADS * GDV

    def body(dog_ref, o_ref, q_ref, k_ref, v_ref, gg_ref, ms_ref, wa_ref, ba_ref, go_ref, st_ref,
             dq_ref, dk_ref, dv_ref, dgg_ref, dz_ref, dba, dgo, zs, la, carry_g):
        @pl.when(pl.program_id(1) == 0)
        def _():
            dba[...] = jnp.zeros_like(dba)
            dgo[...] = jnp.zeros_like(dgo)

        z = _dot(ms_ref[...].astype(BF16), wa_ref[...], NN) + ba_ref[...]
        zs[...] = z
        la[...] = _log_sigmoid(z) * (1.0 / GTAU)
        carry_g[...] = jnp.zeros_like(carry_g)
        low, upp = _tri(True), _tri(False)
        gout = go_ref[...]
        last_row = lax.broadcasted_iota(jnp.int32, (CHUNK, GDK), 0) == CHUNK - 1

        def chunk(step, carry):
            n = nc - 1 - step
            rows = pl.ds(pl.multiple_of(n * CHUNK, CHUNK), CHUNK)
            for hh in range(p):
                kc, vc = slice(hh * GDK, (hh + 1) * GDK), slice(hh * GDV, (hh + 1) * GDV)
                lac = la[rows, kc]
                cum = _dot(low, lac, NN, lax.Precision.HIGHEST)
                ce = jnp.sum(lac, axis=0, keepdims=True)
                e = jnp.exp(ce - cum)
                dec = jnp.exp(ce)
                kf = k_ref[rows, kc].astype(F32)
                kd = (kf * e).astype(BF16)
                vv = v_ref[rows, vc].astype(BF16)
                qs = (q_ref[rows, kc].astype(F32) * (GDK ** -0.5)).astype(BF16)
                ov = o_ref[rows, vc]
                ro = lax.rsqrt(jnp.mean(ov * ov, axis=1, keepdims=True) + EPS)
                on = ov * ro
                gg = gg_ref[rows, vc].astype(F32)
                sg = _sigmoid(gg)
                dogv = dog_ref[rows, vc].astype(F32)
                dgg_ref[rows, vc] = (dogv * (on * gout) * (sg * (1.0 + gg * (1.0 - sg)))).astype(BF16)
                t1 = dogv * (gg * sg)
                dgo[8 * hh:8 * hh + 1, :] += jnp.sum(t1 * on, axis=0, keepdims=True)
                don = t1 * gout
                do = ro * (don - on * jnp.mean(don * on, axis=1, keepdims=True))
                dob = do.astype(BF16)
                st_n = st_ref[pl.ds(pl.multiple_of((hh * nc + n) * GDV, GDV), GDV), :]
                dq_ref[rows, kc] = (_dot(dob, st_n, NN) * (GDK ** -0.5)).astype(BF16)
                dn = carry_g[vc, :] + _dot(dob, qs, TN)
                prev = hh * nc + jnp.maximum(n - 1, 0)
                st_p = st_ref[pl.ds(pl.multiple_of(prev * GDV, GDV), GDV), :].astype(F32) * jnp.where(n > 0, 1.0, 0.0)
                ddec = jnp.sum(dn * st_p, axis=0, keepdims=True)
                dnb = dn.astype(BF16)
                dkd = _dot(vv, dnb, NN)
                dv_ref[rows, vc] = _dot(kd, dnb, NT).astype(BF16)
                dk_ref[rows, kc] = (dkd * e).astype(BF16)
                w = dkd * kf * e
                dce = jnp.sum(w, axis=0, keepdims=True) + ddec * dec
                dcum = jnp.where(last_row, dce - w, -w)
                dla = _dot(upp, dcum, NN, lax.Precision.HIGHEST)
                dz = dla * (1.0 / GTAU) * _sigmoid(-zs[rows, kc])
                dba[0:1, kc] += jnp.sum(dz, axis=0, keepdims=True)
                dz_ref[rows, kc] = dz.astype(BF16)
                carry_g[vc, :] = dn * dec
            return carry

        lax.fori_loop(0, nc, chunk, 0, unroll=GLA_UNROLL)

    hv = pl.BlockSpec((s, vw), lambda h, b: (b, h))
    hk = pl.BlockSpec((s, kw), lambda h, b: (b, h))
    return pl.pallas_call(
        body, name="gla_bwd", grid=(GH // p, bsz),
        in_specs=[hv, hv, hk, pl.BlockSpec((s, kw), lambda h, b: (b, GH // p + h)),
                  pl.BlockSpec((s, vw), lambda h, b: (b, GH // p + h)), pl.BlockSpec((s, vw), lambda h, b: (b, 2 * GH // p + h)),
                  pl.BlockSpec((s, LANES), lambda h, b: (b, 43)), pl.BlockSpec((LANES, kw), lambda h, b: (0, h)),
                  pl.BlockSpec((1, kw), lambda h, b: (0, h)), pl.BlockSpec((1, GDV), lambda h, b: (0, 0)),
                  pl.BlockSpec((p * nc * GDV, GDK), lambda h, b: (b * (GH // p) + h, 0))],
        out_specs=[hk, hk, hv, hv, hk, pl.BlockSpec((8, kw), lambda h, b: (0, h)),
                   pl.BlockSpec((8 * p, GDV), lambda h, b: (h, 0))],
        out_shape=[jax.ShapeDtypeStruct((t, GH * GDK), BF16), jax.ShapeDtypeStruct((t, GH * GDK), BF16),
                   jax.ShapeDtypeStruct((t, GH * GDV), BF16), jax.ShapeDtypeStruct((t, GH * GDV), BF16),
                   jax.ShapeDtypeStruct((t, GH * GDK), BF16), jax.ShapeDtypeStruct((8, GH * GDK), F32),
                   jax.ShapeDtypeStruct((8 * GH, GDV), F32)],
        scratch_shapes=[pltpu.VMEM((s, kw), F32), pltpu.VMEM((s, kw), F32), pltpu.VMEM((vw, GDK), F32)],
        compiler_params=_cparams(("arbitrary", "arbitrary")),
    )(dog, o, proj, proj, proj, proj, proj, wa_pad, b_alpha, g_out, states)


def _rope_tables(pos_ref, fr_ref, sg_ref):
    ang = pos_ref[...].astype(F32) * fr_ref[...]
    return jnp.cos(ang), jnp.sin(ang) * sg_ref[...]


def _partner(x):
    lane = lax.broadcasted_iota(jnp.int32, x.shape, 1)
    return jnp.where(lane < NOPE + ROPE // 2, pltpu.roll(x, LANES - ROPE // 2, 1), pltpu.roll(x, ROPE // 2, 1))


def _mla_rows(t):
    return _tile(t, 512)


def _mla_pre_fwd(proj, pos, fr, sg, q_lat_g, kv_lat_g, qn_g, kn_g, wuq, wukv):
    t = proj.shape[0]
    tr = _mla_rows(t)

    def body(cq_ref, ckv_ref, ms_ref, pos_ref, fr_ref, sg_ref, qlg, kvlg, qng, kng, wuq_ref, wukv_ref, q_out, k_out, v_out):
        lane = lax.broadcasted_iota(jnp.int32, (tr, HP), 1)
        real = jnp.where(lane < MQK, 1.0, 0.0)
        cos, sin = _rope_tables(pos_ref, fr_ref, sg_ref)
        cos = cos * real
        cq = cq_ref[...].astype(F32)
        cqn = (cq * lax.rsqrt(jnp.mean(cq * cq, axis=1, keepdims=True) + EPS) * qlg[...]).astype(BF16)
        ckv = ckv_ref[...].astype(F32)
        ckvn = (ckv * lax.rsqrt(jnp.mean(ckv * ckv, axis=1, keepdims=True) + EPS) * kvlg[...]).astype(BF16)
        kpe = jnp.where((lane >= NOPE) & (lane < MQK), ms_ref[...].astype(F32), 0.0)
        kpe = kpe + jnp.where(lane < MQK + ROPE // 2, pltpu.roll(kpe, ROPE, 1), 0.0)
        lane_all = lax.broadcasted_iota(jnp.int32, (tr, MH * HP), 1)
        v_out[...] = jnp.where(lane_all % HP == MV, 1.0, _dot(ckvn, wukv_ref[:, MH * HP:], NN)).astype(BF16)

        def norm_rope(x, gain):
            xn = x * lax.rsqrt(jnp.sum(x * x * real, axis=1, keepdims=True) * (1.0 / MQK) + EPS) * gain
            return (xn * cos + pltpu.roll(xn, LANES - ROPE // 2, 1) * sin).astype(BF16)

        for h in range(MH):
            cols = slice(h * HP, (h + 1) * HP)
            q_out[:, cols] = norm_rope(_dot(cqn, wuq_ref[:, cols], NN), qng[...])
            k_out[:, cols] = norm_rope(_dot(ckvn, wukv_ref[:, cols], NN) + kpe, kng[...])

    def full(a):
        return pl.BlockSpec(a.shape, lambda i: (0, 0))

    wide = pl.BlockSpec((tr, MH * HP), lambda i: (i, 0))
    return pl.pallas_call(
        body, name="mla_pre_fwd", grid=(t // tr,),
        in_specs=[pl.BlockSpec((tr, MQR), lambda i: (i, 20)), pl.BlockSpec((tr, MKVR), lambda i: (i, 42)),
                  pl.BlockSpec((tr, LANES), lambda i: (i, 43)), pl.BlockSpec((tr, 1), lambda i: (i, 0)),
                  full(fr), full(sg), full(q_lat_g), full(kv_lat_g), full(qn_g), full(kn_g), full(wuq), full(wukv)],
        out_specs=[wide, wide, wide],
        out_shape=[jax.ShapeDtypeStruct((t, MH * HP), BF16)] * 3,
        compiler_params=_cparams(("arbitrary",)),
    )(proj, proj, proj, pos, fr, sg, q_lat_g, kv_lat_g, qn_g, kn_g, wuq, wukv)


def _mla_pre_bwd(dq2, dk2, dv2, dmisc_gla, proj, pos, fr, sg, q_lat_g, kv_lat_g, qn_g, kn_g, wuq, wukv):
    t = proj.shape[0]
    tr = _mla_rows(t)

    def body(dq_ref, dk_ref, dv_ref, dmg_ref, cq_ref, ckv_ref, ms_ref, pos_ref, fr_ref, sg_ref, qlg, kvlg, qng, kng,
             wuq_ref, wukv_ref, dcq_ref, dckv_ref, dms_ref, dwuq, dwukv, acc, dqf, dkvf):
        @pl.when(pl.program_id(0) == 0)
        def _():
            dwuq[...] = jnp.zeros_like(dwuq)
            dwukv[...] = jnp.zeros_like(dwukv)
            acc[...] = jnp.zeros_like(acc)

        cos, sin = _rope_tables(pos_ref, fr_ref, sg_ref)
        cq = cq_ref[...].astype(F32)
        rc = lax.rsqrt(jnp.mean(cq * cq, axis=1, keepdims=True) + EPS)
        xc = cq * rc
        cqn = (xc * qlg[...]).astype(BF16)
        ckv = ckv_ref[...].astype(F32)
        rkv = lax.rsqrt(jnp.mean(ckv * ckv, axis=1, keepdims=True) + EPS)
        xkv = ckv * rkv
        ckvn = (xkv * kvlg[...]).astype(BF16)
        lane = lax.broadcasted_iota(jnp.int32, (tr, HP), 1)
        is_rope = (lane >= NOPE) & (lane < MQK)
        kpe = jnp.where(is_rope, ms_ref[...].astype(F32), 0.0)
        dkpe = jnp.zeros((tr, HP), F32)
        dqng = jnp.zeros((1, HP), F32)
        dkng = jnp.zeros((1, HP), F32)
        for h in range(MH):
            cols = slice(h * HP, (h + 1) * HP)
            qh = _dot(cqn, wuq_ref[:, cols], NN)
            rq = lax.rsqrt(jnp.sum(qh * qh, axis=1, keepdims=True) * (1.0 / MQK) + EPS)
            xq = qh * rq
            dy = dq_ref[:, cols].astype(F32)
            dqn = dy * cos - _partner(dy) * sin
            dqng += jnp.sum(dqn * xq, axis=0, keepdims=True)
            tq = dqn * qng[...]
            dqf[:, cols] = (rq * (tq - xq * (jnp.sum(tq * xq, axis=1, keepdims=True) * (1.0 / MQK)))).astype(BF16)
            kh = _dot(ckvn, wukv_ref[:, cols], NN) + kpe
            rk = lax.rsqrt(jnp.sum(kh * kh, axis=1, keepdims=True) * (1.0 / MQK) + EPS)
            xk = kh * rk
            dy = dk_ref[:, cols].astype(F32)
            dkn = dy * cos - _partner(dy) * sin
            dkng += jnp.sum(dkn * xk, axis=0, keepdims=True)
            tk = dkn * kng[...]
            dkh = rk * (tk - xk * (jnp.sum(tk * xk, axis=1, keepdims=True) * (1.0 / MQK)))
            dkvf[:, cols] = jnp.where(lane < NOPE, dkh, 0.0).astype(BF16)
            dkpe += jnp.where(is_rope, dkh, 0.0)
        dkvf[:, MH * HP:] = dv_ref[...]
        acc[2:3, 0:HP] += dqng
        acc[3:4, 0:HP] += dkng
        dms_ref[...] = (dmg_ref[...] + dkpe).astype(BF16)

        dqfv = dqf[...]
        dwuq[...] += _dot(cqn, dqfv, TN)
        dcqn = _dot(dqfv, wuq_ref[...], NT)
        acc[0:1, :] += jnp.sum(dcqn * xc, axis=0, keepdims=True)
        tc = dcqn * qlg[...]
        dcq_ref[...] = (rc * (tc - xc * jnp.mean(tc * xc, axis=1, keepdims=True))).astype(BF16)

        dkvfv = dkvf[...]
        dwukv[...] += _dot(ckvn, dkvfv, TN)
        dckvn = _dot(dkvfv, wukv_ref[...], NT)
        acc[1:2, 0:MKVR] += jnp.sum(dckvn * xkv, axis=0, keepdims=True)
        tkv = dckvn * kvlg[...]
        dckv_ref[...] = (rkv * (tkv - xkv * jnp.mean(tkv * xkv, axis=1, keepdims=True))).astype(BF16)

    def full(a):
        return pl.BlockSpec(a.shape, lambda i: (0, 0))

    wide = pl.BlockSpec((tr, MH * HP), lambda i: (i, 0))
    narrow = pl.BlockSpec((tr, LANES), lambda i: (i, 0))
    return pl.pallas_call(
        body, name="mla_pre_bwd", grid=(t // tr,),
        in_specs=[wide, wide, wide, narrow,
                  pl.BlockSpec((tr, MQR), lambda i: (i, 20)), pl.BlockSpec((tr, MKVR), lambda i: (i, 42)),
                  pl.BlockSpec((tr, LANES), lambda i: (i, 43)), pl.BlockSpec((tr, 1), lambda i: (i, 0)),
                  full(fr), full(sg), full(q_lat_g), full(kv_lat_g), full(qn_g), full(kn_g), full(wuq), full(wukv)],
        out_specs=[pl.BlockSpec((tr, MQR), lambda i: (i, 0)), narrow, narrow,
                   pl.BlockSpec((MQR, MH * HP), lambda i: (0, 0)), pl.BlockSpec((MKVR, 2 * MH * HP), lambda i: (0, 0)),
                   pl.BlockSpec((8, MQR), lambda i: (0, 0))],
        out_shape=[jax.ShapeDtypeStruct((t, MQR), BF16), jax.ShapeDtypeStruct((t, MKVR), BF16),
                   jax.ShapeDtypeStruct((t, LANES), BF16), jax.ShapeDtypeStruct((MQR, MH * HP), F32),
                   jax.ShapeDtypeStruct((MKVR, 2 * MH * HP), F32), jax.ShapeDtypeStruct((8, MQR), F32)],
        scratch_shapes=[pltpu.VMEM((tr, MH * HP), BF16), pltpu.VMEM((tr, 2 * MH * HP), BF16)],
        compiler_params=_cparams(("arbitrary",)),
    )(dq2, dk2, dv2, dmisc_gla, proj, proj, proj, pos, fr, sg, q_lat_g, kv_lat_g, qn_g, kn_g, wuq, wukv)


ATT_FWD_TILES = (1024, 512)
ATT_BWD_TILES = (512, 512)
ATT_HEADS = 2
NEG = -1e30
LOG2E = 1.4426950408889634


def _att_mask(q0, k0, tq, tk):
    qc = (q0 + lax.broadcasted_iota(jnp.int32, (tq, tk), 0)) // CHUNK
    kc = (k0 + lax.broadcasted_iota(jnp.int32, (tq, tk), 1)) // CHUNK
    return kc <= qc


def _att_tiles(s, tiles):
    return _tile(s, tiles[0]), _tile(s, tiles[1])


def _lanes(x, n):
    return x if n == 1 else jnp.concatenate([x] * n, axis=1)


def _grid_ends(grid):
    i, j = pl.program_id(0), pl.program_id(1)
    return (i == 0) & (j == 0), (i == grid[0] - 1) & (j == grid[1] - 1)


def _attn_fwd(q2, k2, v2, bsz, slab):
    t = q2.shape[0]
    s = t // bsz
    tq, tk = _att_tiles(s, ATT_FWD_TILES)
    nq, groups, n_diag = s // tq, tk // HP, max(tq // tk, 1)
    sub_rows = tq // n_diag
    scale = MQK ** -0.5
    c2 = scale * LOG2E
    heads = range(ATT_HEADS)

    def body(q_ref, k_ref, v_ref, slab_ref, o_ref, lse_ref, gath_ref, send_sems, recv_sems, local_sem):
        gather = _core_row_gather_copies(slab_ref, gath_ref, send_sems, recv_sems, local_sem)
        first, last = _grid_ends((bsz, MH // ATT_HEADS))
        pl.when(first)(gather.start)

        def q_loop(qi, carry):
            q0 = pl.multiple_of(qi * tq, tq)
            rows = pl.ds(q0, tq)
            n_full = q0 // tk
            qs = [q_ref[rows, h * HP:(h + 1) * HP] for h in heads]

            def scores(h, kj, sub=None, masked=False):
                k0 = pl.multiple_of(kj * tk, tk)
                qv = qs[h] if sub is None else qs[h][sub * sub_rows:(sub + 1) * sub_rows]
                sc = _dot(qv, k_ref[pl.ds(k0, tk), h * HP:(h + 1) * HP], NT)
                return jnp.where(_att_mask(q0 + sub * sub_rows, k0, sub_rows, tk), sc, NEG) if masked else sc

            def fold(mx, sc):
                for j in range(groups):
                    mx = jnp.maximum(mx, sc[:, j * HP:(j + 1) * HP])
                return mx

            def over_diagonal(vals, step):
                out = []
                for h in heads:
                    blocks = []
                    for r in range(n_diag):
                        v = vals[h][r * sub_rows:(r + 1) * sub_rows]
                        for u in range(r + 1):
                            v = step(v, h, n_full + u, r, u == r)
                        blocks.append(v)
                    out.append(blocks[0] if n_diag == 1 else jnp.concatenate(blocks, axis=0))
                return tuple(out)

            mx = lax.fori_loop(0, n_full, lambda kj, mx: tuple(fold(mx[h], scores(h, kj)) for h in heads),
                               tuple(jnp.full((tq, HP), NEG, F32) for _ in heads))
            mx = over_diagonal(mx, lambda v, h, kj, r, masked: fold(v, scores(h, kj, r, masked)))
            mb = [jnp.broadcast_to(jnp.max(mx[h], axis=1, keepdims=True), (tq, HP)) for h in heads]

            def weighted(h, kj, sub=None, masked=False):
                m = mb[h] if sub is None else mb[h][sub * sub_rows:(sub + 1) * sub_rows]
                p = jnp.exp2((scores(h, kj, sub, masked) - _lanes(m, groups)) * c2)
                k0 = pl.multiple_of(kj * tk, tk)
                return _dot(p.astype(BF16), v_ref[pl.ds(k0, tk), h * HP:(h + 1) * HP], NN)

            acc = lax.fori_loop(0, n_full, lambda kj, acc: tuple(acc[h] + weighted(h, kj) for h in heads),
                                tuple(jnp.zeros((tq, HP), F32) for _ in heads))
            acc = over_diagonal(acc, lambda v, h, kj, r, masked: v + weighted(h, kj, r, masked))
            lane = lax.broadcasted_iota(jnp.int32, (tq, HP), 1)
            for h in heads:
                a = acc[h]
                l = jnp.sum(jnp.where(lane == MV, a, 0.0), axis=1, keepdims=True)
                o_ref[rows, h * HP:(h + 1) * HP] = (a / l).astype(BF16)
                lse_ref[rows, h * HP:(h + 1) * HP] = mb[h] * scale + jnp.log(l)
            return carry

        lax.fori_loop(0, nq, q_loop, 0)
        pl.when(last)(gather.finish)

    spec = pl.BlockSpec((s, ATT_HEADS * HP), lambda b, h: (b, h))
    return pl.pallas_call(
        body, name="attn_fwd", grid=(bsz, MH // ATT_HEADS), in_specs=[spec] * 3 + [HBM_SPEC],
        out_specs=[spec, spec, HBM_SPEC],
        out_shape=[jax.ShapeDtypeStruct((t, MH * HP), BF16), jax.ShapeDtypeStruct((t, MH * HP), F32),
                   jax.ShapeDtypeStruct((N_DEV,) + slab.shape, slab.dtype)],
        scratch_shapes=EXCHANGE_SEMS, compiler_params=_cparams(("arbitrary", "arbitrary")),
    )(q2, k2, v2, slab)


def _attn_bwd(q2, k2, v2, do2, o2, lse2, bsz, tsum):
    t = q2.shape[0]
    s = t // bsz
    tq, tk = _att_tiles(s, ATT_BWD_TILES)
    nq, nk, per, groups = s // tq, s // tk, max(tk // tq, 1), tk // HP
    scale = MQK ** -0.5
    c2 = scale * LOG2E
    heads = range(ATT_HEADS)

    def body(q_ref, k_ref, v_ref, do_ref, o_ref, lse_ref, t_ref, dq_ref, dk_ref, dv_ref, parts_ref, dq_acc, delta, lse_b2,
             send_sems, recv_sems, local_sem):
        exchange = _all_to_all_copies(t_ref, parts_ref, send_sems, recv_sems, local_sem)
        first, last = _grid_ends((bsz, MH // ATT_HEADS))
        pl.when(first)(exchange.start)
        dq_acc[...] = jnp.zeros_like(dq_acc)

        def d_loop(i, carry):
            rows = pl.ds(pl.multiple_of(i * tq, tq), tq)
            for h in heads:
                hs = slice(h * HP, (h + 1) * HP)
                dl = jnp.sum(do_ref[rows, hs].astype(F32) * o_ref[rows, hs].astype(F32), axis=1, keepdims=True)
                delta[rows, hs] = jnp.broadcast_to(dl, (tq, HP))
            lse_b2[rows, :] = lse_ref[rows, :] * LOG2E
            return carry

        lax.fori_loop(0, nq, d_loop, 0)

        def k_loop(kj, carry):
            k0 = pl.multiple_of(kj * tk, tk)
            kk = [k_ref[pl.ds(k0, tk), h * HP:(h + 1) * HP] for h in heads]
            vv = [v_ref[pl.ds(k0, tk), h * HP:(h + 1) * HP] for h in heads]

            def tile(qi, c, masked):
                q0 = pl.multiple_of(qi * tq, tq)
                rows = pl.ds(q0, tq)
                out = []
                for h in heads:
                    hs = slice(h * HP, (h + 1) * HP)
                    dk, dv = c[h]
                    q = q_ref[rows, hs]
                    do = do_ref[rows, hs]
                    e = _dot(q, kk[h], NT) * c2 - _lanes(lse_b2[rows, hs], groups)
                    if masked:
                        e = jnp.where(_att_mask(q0, k0, tq, tk), e, NEG)
                    p = jnp.exp2(e)
                    dv = dv + _dot(p.astype(BF16), do, TN)
                    ds = (p * (_dot(do, vv[h], NT) - _lanes(delta[rows, hs], groups))).astype(BF16)
                    dq_acc[rows, hs] += _dot(ds, kk[h], NN)
                    dk = dk + _dot(ds, q, TN)
                    out.append((dk, dv))
                return tuple(out)

            zero = jnp.zeros((tk, HP), F32)
            c = tuple((zero, zero) for _ in heads)
            first = k0 // tq
            for u in range(per):
                c = tile(first + u, c, True)
            c = lax.fori_loop(first + per, nq, lambda qi, c: tile(qi, c, False), c)
            for h in heads:
                dk_ref[pl.ds(k0, tk), h * HP:(h + 1) * HP] = (c[h][0] * scale).astype(BF16)
                dv_ref[pl.ds(k0, tk), h * HP:(h + 1) * HP] = c[h][1].astype(BF16)
            return carry

        lax.fori_loop(0, nk, k_loop, 0)
        dq_ref[...] = (dq_acc[...] * scale).astype(BF16)
        pl.when(last)(exchange.finish)

    spec = pl.BlockSpec((s, ATT_HEADS * HP), lambda b, h: (b, h))
    return pl.pallas_call(
        body, name="attn_bwd", grid=(bsz, MH // ATT_HEADS), in_specs=[spec] * 6 + [HBM_SPEC],
        out_specs=[spec] * 3 + [HBM_SPEC],
        out_shape=[jax.ShapeDtypeStruct((t, MH * HP), BF16)] * 3 + [jax.ShapeDtypeStruct(tsum.shape, tsum.dtype)],
        scratch_shapes=[pltpu.VMEM((s, ATT_HEADS * HP), F32)] * 3 + EXCHANGE_SEMS,
        compiler_params=_cparams(("arbitrary", "arbitrary")),
    )(q2, k2, v2, do2, o2, lse2, tsum)


def _perm_w_in_t(w):
    z = lambda n: jnp.zeros((n, w.shape[1]), w.dtype)
    return jnp.concatenate([w[:3072], w[3504:5552], w[3088:3344], w[3344:3472], w[3072:3088], z(48), w[3472:3504], z(32)],
                           axis=0)


def _unperm_w_in_t(g):
    return jnp.concatenate([g[:3072], g[5504:5520], g[5120:5376], g[5376:5504], g[5568:5600], g[3072:5120]], axis=0)


def _pad_wa(w):
    return jnp.pad(w, ((0, LANES - GLR), (0, 0)))


def _pad_wuq(w):
    return jnp.pad(w.reshape(MQR, MH, MQK), ((0, 0), (0, 0), (0, HP - MQK))).reshape(MQR, MH * HP)


def _unpad_wuq(g):
    return g.reshape(MQR, MH, HP)[:, :, :MQK].reshape(MQR, MH * MQK)


def _pad_wukv(w):
    w3 = w.reshape(MKVR, MH, NOPE + MV)
    kp = jnp.pad(w3[:, :, :NOPE], ((0, 0), (0, 0), (0, HP - NOPE))).reshape(MKVR, MH * HP)
    vp = jnp.pad(w3[:, :, NOPE:], ((0, 0), (0, 0), (0, HP - MV))).reshape(MKVR, MH * HP)
    return jnp.concatenate([kp, vp], axis=1)


def _unpad_wukv(g):
    kp = g[:, :MH * HP].reshape(MKVR, MH, HP)[:, :, :NOPE]
    vp = g[:, MH * HP:].reshape(MKVR, MH, HP)[:, :, :MV]
    return jnp.concatenate([kp, vp], axis=2).reshape(MKVR, MH * (NOPE + MV))


def _pad_wo(w):
    return jnp.pad(w.reshape(MH, MV, D), ((0, 0), (0, HP - MV), (0, 0))).reshape(MH * HP, D)


def _unpad_wo(g):
    return g.reshape(MH, HP, D)[:, :MV, :].reshape(MH * MV, D)


def _repeat_half(a):
    a3 = a.reshape(a.shape[0], -1, HP)
    a3 = jnp.concatenate([a3[:, :, :MQK], a3[:, :, NOPE:NOPE + ROPE // 2], a3[:, :, MQK + ROPE // 2:]], axis=2)
    return a3.reshape(a.shape)


def _pad_lanes(v, n=HP):
    return jnp.pad(v, ((0, 0), (0, n - v.shape[1])))


def _local_step(x, positions, tgt, wt, slab_b, sp, mod3):
    bsz, s, _ = x.shape
    t = bsz * s
    x2 = x.reshape(t, D)
    tgt2 = tgt.reshape(t, D)
    pos = positions.reshape(t, 1)
    fr16 = ROPE_THETA ** (-jnp.arange(0, ROPE, 2, dtype=F32) / ROPE)
    zero = lambda n: jnp.zeros((n,), F32)
    fr = jnp.concatenate([zero(NOPE), fr16, fr16, zero(HP - MQK)]).reshape(1, HP)
    sg = jnp.concatenate([zero(NOPE), -jnp.ones((ROPE // 2,), F32), jnp.ones((ROPE // 2,), F32), zero(HP - MQK)]).reshape(1, HP)

    w_in_t = _perm_w_in_t(wt["w_in"])
    wa_pad = _pad_wa(wt["gla_w_alpha"].T)
    wuq = _pad_wuq(wt["mla_w_uq"].T)
    wukv = _pad_wukv(wt["mla_w_ukv"].T)
    qn_g, kn_g = _pad_lanes(sp["mla_qn_g"]), _pad_lanes(sp["mla_kn_g"])

    _, h = _norm_mod_fwd(x2, sp["norm1_g"], mod3, 0, 1, "norm1_fwd")
    proj = _mm(h, w_in_t, "nt", (BF16,), "proj_fwd", tn=PW // 4)
    o_gla, og, states = _gla_fwd(proj, wa_pad, sp["gla_b_alpha"], sp["gla_out_norm_g"], bsz)
    q2, k2, v2 = _mla_pre_fwd(proj, pos, fr, sg, sp["mla_q_lat_g"], sp["mla_kv_lat_g"], _repeat_half(qn_g), _repeat_half(kn_g),
                              _repeat_half(wuq), wukv)
    o2, lse2, core_row = _attn_fwd(q2, k2, v2, bsz, slab_b)
    wt = dict(wt, **_unpack_gathered(_cross_core_fill(core_row), SLAB_B))
    wo_pad = _pad_wo(wt["mla_w_o"])
    y_a = _mm(og, wt["gla_w_o"], "nn", (BF16,), "gla_out_fwd")
    y_b = _mm(o2, wo_pad, "nn", (BF16,), "mla_out_fwd")
    mix = _merge_fwd(proj, sp["b_merge"], y_a, y_b)
    mixed = _mm(mix, wt["w_out"], "nn", (F32,), "w_out_fwd")

    x1, h2 = _norm_mod_fwd(x2, sp["norm2_g"], mod3, 3, 4, "norm2_fwd", mixed=mixed, i_gate=2)
    a, f = _mm(h2, wt["mlp_w1"], "nt", (BF16, BF16), "mlp1_fwd",
               epi=lambda acc: (acc, jnp.square(jnp.maximum(acc, 0.0))))
    tm = _tile(s, 1024)
    dy, dff, acc_g2, acc_loss = _mm(f, wt["mlp_w2"], "nn", (), "mlp2_fwd", tm=tm, tail=_loss_tail(x1, tgt2, mod3, tm))

    gw = {}
    gw["mlp_w2"] = _mm(f, dff, "tn", (BF16,), "mlp2_dw")
    da = _mm(dff, wt["mlp_w2"], "nt", (BF16,), "mlp2_dx", extras=(a,),
             epi=lambda acc, av: (acc * (2.0 * jnp.maximum(av.astype(F32), 0.0)),))
    gw["mlp_w1"] = _mm(da, h2, "tn", (BF16,), "mlp1_dw")
    dx1, dmixed, accb2, accg2 = _mm(da, wt["mlp_w1"], "nn", (), "mlp1_dx", tm=tm,
                                    tail=_norm_bwd_tail(x1, dy, sp["norm2_g"], mod3, 4, tm, mixed=mixed, i_gate=2))

    gw["w_out"] = _mm(mix, dmixed, "tn", (BF16,), "w_out_dw")
    dmix = _mm(dmixed, wt["w_out"], "nt", (BF16,), "w_out_dx")
    dy_a, dy_b, dlogits, acc_bm = _merge_bwd(dmix, proj, sp["b_merge"], y_a, y_b)
    gw["gla_w_o"] = _mm(og, dy_a, "tn", (BF16,), "gla_out_dw")
    dog = _mm(dy_a, wt["gla_w_o"], "nt", (BF16,), "gla_out_dx")
    gw["mla_w_o"] = _unpad_wo(_mm(o2, dy_b, "tn", (BF16,), "mla_out_dw"))
    do2 = _mm(dy_b, wo_pad, "nt", (BF16,), "mla_out_dx")
    dq2, dk2, dv2, parts_b = _attn_bwd(q2, k2, v2, do2, o2, lse2, bsz, _pack_per_device(gw, SLAB_B))
    dq_g, dk_g, dv_g, dgg, dz, acc_ba, acc_go = _gla_bwd(dog, o_gla, proj, wa_pad, sp["gla_b_alpha"],
                                                         sp["gla_out_norm_g"], states, bsz)
    gw["gla_w_alpha"] = _mm(proj, dz, "tn", (F32,), "gla_alpha_dw", a_off=43, m=LANES)[:GLR].T.astype(BF16)
    dmisc_gla = _mm(dz, wa_pad, "nt", (F32,), "gla_alpha_dx")
    dcq, dckv, dmisc, gwuq, gwukv, acc_mla = _mla_pre_bwd(dq2, dk2, dv2, dmisc_gla, proj, pos, fr, sg, sp["mla_q_lat_g"],
                                                         sp["mla_kv_lat_g"], qn_g, kn_g, wuq, wukv)
    gw["mla_w_uq"] = _unpad_wuq(gwuq).T.astype(BF16)
    gw["mla_w_ukv"] = _unpad_wukv(gwukv).T.astype(BF16)
    dproj = jnp.concatenate([dq_g, dk_g, dv_g, dgg, dlogits, dcq, dckv, dmisc], axis=1)
    gw["w_in"] = _unperm_w_in_t(_mm(dproj, h, "tn", (BF16,), "proj_dw", tm=PW // 4))
    parts_a, grad_x, accb1, accg1 = _mm(dproj, w_in_t, "nn", (), "proj_dx", tm=tm, tk=PW // 4,
                                        cargo=_sum_over_cores(_pack_per_device(gw, SLAB_A), "a"),
                                        tail=_norm_bwd_tail(x2, dx1, sp["norm1_g"], mod3, 1, tm))

    dmod = jnp.stack([accb1[:, 0], accb1[:, 1], accb2[:, 2], accb2[:, 0], accb2[:, 1], acc_g2[:, 0]], axis=1)

    rows = {
        "dmod": dmod.reshape(bsz * 6, D),
        "norm1_g": accg1[0:1], "norm2_g": accg2[0:1],
        "b_merge": acc_bm[0:1].reshape(2, D),
        "gla_b_alpha": _pad_lanes(acc_ba[0:1], D),
        "gla_out_norm_g": _pad_lanes(acc_go.reshape(GH, 8, GDV)[:, 0, :], D),
        "mla_q_lat_g": _pad_lanes(acc_mla[0:1], D), "mla_kv_lat_g": _pad_lanes(acc_mla[1:2], D),
        "mla_qn_g": _pad_lanes(acc_mla[2:3], D), "mla_kn_g": _pad_lanes(acc_mla[3:4], D),
        "loss": acc_loss[0:1],
    }
    return grad_x.reshape(bsz, s, D), parts_a, parts_b, rows


HBM_SPEC = pl.BlockSpec(memory_space=pltpu.HBM)


def _all_gather(ps, name):
    n = len(ps)

    def body(*refs):
        p_refs, out_refs, (send_sems, recv_sems, local_sems) = refs[:n], refs[n:2 * n], refs[2 * n:]
        x, y, c = lax.axis_index("x"), lax.axis_index("y"), lax.axis_index("c")
        me, sibling = (x, y, c), (x, y, 1 - c)
        chips = [(1 - x, y), (x, 1 - y), (1 - x, 1 - y)]

        def copy(a, k, block, to, own=False):
            slot = out_refs[a].at[4 * block[0] + 2 * block[1] + block[2]]
            return pltpu.make_async_remote_copy(
                src_ref=p_refs[a] if own else slot, dst_ref=slot, send_sem=send_sems.at[7 * a + k],
                recv_sem=recv_sems.at[7 * a + k], device_id=to, device_id_type=MESH)

        mine = [pltpu.make_async_copy(p_refs[a], out_refs[a].at[4 * x + 2 * y + c], local_sems.at[a]) for a in range(n)]
        first = [copy(a, 0, me, sibling, own=True) for a in range(n)]
        first += [copy(a, 1 + j, me, (*chip, c), own=True) for a in range(n) for j, chip in enumerate(chips)]
        for cp in mine + first:
            cp.start()
        passed = []
        for j, chip in enumerate(chips):
            for a in range(n):
                copy(a, 1 + j, (*chip, c), me).wait_recv()
                passed.append(copy(a, 4 + j, (*chip, c), sibling))
                passed[-1].start()
        for a in range(n):
            copy(a, 0, sibling, me).wait_recv()
            for j, chip in enumerate(chips):
                copy(a, 4 + j, (*chip, 1 - c), me).wait_recv()
        for cp in first + passed:
            cp.wait_send()
        for cp in mine:
            cp.wait()

    return pl.pallas_call(
        body, name=name, out_shape=[jax.ShapeDtypeStruct((N_DEV,) + p.shape, p.dtype) for p in ps],
        in_specs=[HBM_SPEC] * n, out_specs=[HBM_SPEC] * n,
        scratch_shapes=[pltpu.SemaphoreType.DMA((7 * n,)), pltpu.SemaphoreType.DMA((7 * n,)), pltpu.SemaphoreType.DMA((n,))],
    )(*ps)


def _sibling_exchange(g, name):
    def body(g_ref, out_ref, send_sem, recv_sem):
        x, y, c = lax.axis_index("x"), lax.axis_index("y"), lax.axis_index("c")
        cp = pltpu.make_async_remote_copy(src_ref=g_ref, dst_ref=out_ref, send_sem=send_sem, recv_sem=recv_sem,
                                          device_id=(x, y, 1 - c), device_id_type=MESH)
        cp.start()
        cp.wait()

    return pl.pallas_call(
        body, name=name, out_shape=jax.ShapeDtypeStruct(g.shape, g.dtype),
        in_specs=[HBM_SPEC], out_specs=HBM_SPEC,
        scratch_shapes=[pltpu.SemaphoreType.DMA(()), pltpu.SemaphoreType.DMA(())],
    )(g)


class _Exchange:
    def __init__(self, local, sends, arrivals):
        self.local, self.sends, self.arrivals = local, sends, arrivals

    def start(self):
        self.local.start()
        for cp in self.sends:
            cp.start()

    def finish(self):
        for cp in self.arrivals:
            cp.wait_recv()
        for cp in self.sends:
            cp.wait_send()
        self.local.wait()


EXCHANGE_SEMS = [pltpu.SemaphoreType.DMA((N_DEV,)), pltpu.SemaphoreType.DMA((N_DEV,)), pltpu.SemaphoreType.DMA(())]


def _all_to_all_copies(t_ref, out_ref, send_sems, recv_sems, local_sem):
    x, y, c = lax.axis_index("x"), lax.axis_index("y"), lax.axis_index("c")
    me = 4 * x + 2 * y + c

    def copy(k, src, dst):
        px, py, pc = (1 - x if k & 4 else x), (1 - y if k & 2 else y), (1 - c if k & 1 else c)
        peer = 4 * px + 2 * py + pc
        return pltpu.make_async_remote_copy(src_ref=t_ref.at[peer if src is None else src],
                                            dst_ref=out_ref.at[peer if dst is None else dst], send_sem=send_sems.at[k],
                                            recv_sem=recv_sems.at[k], device_id=(px, py, pc), device_id_type=MESH)

    return _Exchange(pltpu.make_async_copy(t_ref.at[me], out_ref.at[me], local_sem),
                     [copy(k, None, me) for k in range(1, N_DEV)], [copy(k, me, None) for k in range(1, N_DEV)])


def _chip_exchange_copies(t_ref, out_ref, send_sems, recv_sems, local_sem):
    x, y, c = lax.axis_index("x"), lax.axis_index("y"), lax.axis_index("c")
    my_chip = 2 * x + y
    chips = [(1 - x, y), (x, 1 - y), (1 - x, 1 - y)]

    def copy(j, src, dst, px, py):
        return pltpu.make_async_remote_copy(src_ref=t_ref.at[src], dst_ref=out_ref.at[dst], send_sem=send_sems.at[j],
                                            recv_sem=recv_sems.at[j], device_id=(px, py, c), device_id_type=MESH)

    return _Exchange(pltpu.make_async_copy(t_ref.at[my_chip], out_ref.at[my_chip], local_sem),
                     [copy(j, 2 * px + py, my_chip, px, py) for j, (px, py) in enumerate(chips)],
                     [copy(j, my_chip, 2 * px + py, px, py) for j, (px, py) in enumerate(chips)])


def _core_row_gather_copies(p_ref, out_ref, send_sems, recv_sems, local_sem):
    x, y, c = lax.axis_index("x"), lax.axis_index("y"), lax.axis_index("c")
    peers = [(x, y, 1 - c), (1 - x, y, c), (x, 1 - y, c), (1 - x, 1 - y, c)]

    def slot(px, py, pc):
        return out_ref.at[4 * px + 2 * py + pc]

    def copy(j, block, to):
        return pltpu.make_async_remote_copy(src_ref=p_ref, dst_ref=slot(*block), send_sem=send_sems.at[j],
                                            recv_sem=recv_sems.at[j], device_id=to, device_id_type=MESH)

    return _Exchange(pltpu.make_async_copy(p_ref, slot(x, y, c), local_sem),
                     [copy(j, (x, y, c), peer) for j, peer in enumerate(peers)],
                     [copy(j, peer, peer) for j, peer in enumerate(peers)])


def _cross_core_fill(gathered):
    def body(g_ref, out_ref, send_sems, recv_sems):
        x, y, c = lax.axis_index("x"), lax.axis_index("y"), lax.axis_index("c")
        chips = [(1 - x, y), (x, 1 - y), (1 - x, 1 - y)]

        def copy(j, pc):
            px, py = chips[j]
            slot = 4 * px + 2 * py + pc
            return pltpu.make_async_remote_copy(src_ref=g_ref.at[slot], dst_ref=out_ref.at[slot], send_sem=send_sems.at[j],
                                                recv_sem=recv_sems.at[j], device_id=(x, y, 1 - c), device_id_type=MESH)

        sends = [copy(j, c) for j in range(3)]
        for cp in sends:
            cp.start()
        for j in range(3):
            copy(j, 1 - c).wait_recv()
        for cp in sends:
            cp.wait_send()

    return pl.pallas_call(
        body, name="weights_cross_core_fill", out_shape=jax.ShapeDtypeStruct(gathered.shape, gathered.dtype),
        in_specs=[HBM_SPEC], out_specs=HBM_SPEC, input_output_aliases={0: 0},
        scratch_shapes=[pltpu.SemaphoreType.DMA((3,)), pltpu.SemaphoreType.DMA((3,))],
    )(gathered)


def _slab_block(r):
    return max(b for b in range(16, SLAB_BLOCK_MAX + 1, 16) if r % b == 0)


def _pair_sum(a, b, name):
    n, r, cdim = a.shape
    rb = _slab_block(r)
    blk = pl.BlockSpec((1, rb, cdim), lambda j, i: (j, i, 0))

    def body(a_ref, b_ref, o_ref):
        o_ref[...] = (a_ref[...].astype(F32) + b_ref[...].astype(F32)).astype(BF16)

    return pl.pallas_call(
        body, name=name, grid=(n, r // rb), in_specs=[blk, blk], out_specs=blk,
        out_shape=jax.ShapeDtypeStruct(a.shape, BF16), compiler_params=_cparams(("arbitrary", "arbitrary")),
    )(a, b)


def _adamw_math(w, g, m, v):
    m = ADAM_B1 * m + (1.0 - ADAM_B1) * g
    v = ADAM_B2 * v + (1.0 - ADAM_B2) * jnp.square(g)
    m_hat = m / (1.0 - ADAM_B1 ** ADAM_STEP)
    v_hat = v / (1.0 - ADAM_B2 ** ADAM_STEP)
    delta = -ADAM_LR * (m_hat / (jnp.sqrt(v_hat) + ADAM_EPS) + ADAM_WD * w)
    return delta, m, v


def _slab_sum(parts, name):
    n, r, cdim = parts.shape
    rb = _slab_block(r)
    blk = pl.BlockSpec((rb, cdim), lambda i: (i, 0))

    def body(p_ref, g_out):
        g = p_ref[0].astype(F32)
        for j in range(1, n):
            g = g + p_ref[j].astype(F32)
        g_out[...] = g

    return pl.pallas_call(
        body, name=name, grid=(r // rb,),
        in_specs=[pl.BlockSpec((n, rb, cdim), lambda i: (0, i, 0))], out_specs=blk,
        out_shape=jax.ShapeDtypeStruct((r, cdim), F32), compiler_params=_cparams(("arbitrary",)),
    )(parts)


def _adamw(g, w, m, v, name):
    r, cdim = w.shape
    rb = _tile(r, 256)
    blk = pl.BlockSpec((rb, cdim), lambda i: (i, 0))

    def body(g_ref, w_ref, m_ref, v_ref, d_out, m_out, v_out):
        d_out[...], m_out[...], v_out[...] = _adamw_math(w_ref[...], g_ref[...], m_ref[...], v_ref[...])

    return pl.pallas_call(
        body, name=name, grid=(r // rb,), in_specs=[blk] * 4, out_specs=[blk] * 3,
        out_shape=[jax.ShapeDtypeStruct((r, cdim), F32)] * 3, compiler_params=_cparams(("arbitrary",)),
    )(g, w, m, v)


def _adamw_small(parts, w, m, v):
    def body(p_ref, w_ref, m_ref, v_ref, g_out, d_out, m_out, v_out, loss_out):
        def total(srcs):
            acc = None
            for r in srcs:
                for j in range(N_DEV):
                    term = p_ref[j, r:r + 1, :]
                    acc = term if acc is None else acc + term
            return acc

        for prow, srcs in enumerate(SMALL_SOURCES):
            one = slice(prow, prow + 1)
            g = total(srcs)
            g_out[one, :] = g
            d_out[one, :], m_out[one, :], v_out[one, :] = _adamw_math(w_ref[one, :], g, m_ref[one, :], v_ref[one, :])
        loss_out[...] = jnp.broadcast_to(jnp.sum(total(LOSS_SOURCE), axis=1, keepdims=True), (8, LANES))

    full = lambda shp: pl.BlockSpec(shp, lambda i: (0,) * len(shp))
    return pl.pallas_call(
        body, name="adamw_small", grid=(1,),
        in_specs=[full((N_DEV, SMALL_ROWS, D)), full((16, D)), full((16, D)), full((16, D))],
        out_specs=[full((16, D))] * 4 + [full((8, LANES))],
        out_shape=[jax.ShapeDtypeStruct((16, D), F32)] * 4 + [jax.ShapeDtypeStruct((8, LANES), F32)],
        compiler_params=_cparams(("arbitrary",)),
    )(parts, w, m, v)


def _to_slab(shard, form):
    if form == "N":
        return shard
    return shard.T if form == "T" else shard.T.reshape(-1, D)


def _from_slab(block, form, shard_shape):
    if form == "N":
        return block
    return block.T if form == "T" else block.reshape(shard_shape[1], shard_shape[0]).T


def _gathered_full(g, form, shard_shape):
    if form == "TR":
        return g.reshape(N_DEV * shard_shape[1], shard_shape[0])
    return g.reshape(N_DEV * g.shape[1], D)


def _pack_slab(shards, layout):
    return jnp.concatenate([jnp.zeros((r, D), BF16) if n is None else _to_slab(shards[n], form).astype(BF16)
                            for n, r, form in layout], axis=0)


def _unpack_gathered(gathered, layout):
    out, off = {}, 0
    for n, r, form in layout:
        if n is not None:
            out[n] = _gathered_full(gathered[:, off:off + r], form, SHARD_SHAPES[n])
        off += r
    return out


def _pack_per_device(gw, layout):
    return jnp.concatenate([jnp.zeros((N_DEV, r, D), BF16) if n is None else gw[n].reshape(N_DEV, r, D)
                            for n, r, _ in layout], axis=1)


def _adamw_shards(gslab, layout, wts, mom, var):
    out, off = {}, 0
    for n, r, form in layout:
        if n is not None:
            g = _from_slab(gslab[off:off + r], form, SHARD_SHAPES[n])
            out[n] = (g,) + tuple(_adamw(g, wts[n], mom[n], var[n], "adamw_" + n))
        off += r
    return out


def _sum_over_cores(per_dev, tag):
    my_c = lax.axis_index("c")
    pairs = per_dev.reshape(4, 2, per_dev.shape[1], D)
    keep = lax.dynamic_index_in_dim(pairs, my_c, axis=1, keepdims=False)
    give = lax.dynamic_index_in_dim(pairs, 1 - my_c, axis=1, keepdims=False)
    return _pair_sum(keep, _sibling_exchange(give, "rs_sibling_exchange_" + tag), "rs_pair_sum_" + tag)


def _small_pack(vals):
    rows = []
    for n in SMALL:
        v = vals[n].reshape(-1)
        k = -(-v.shape[0] // D)
        rows.append(jnp.pad(v, (0, k * D - v.shape[0])).reshape(k, D))
    return jnp.concatenate(rows, axis=0)


def _small_unpack(packed, shapes):
    out = {}
    for n in SMALL:
        k = shapes[n][-1]
        r0 = SMALL_ROW[n]
        out[n] = packed[r0:r0 + -(-k // D)].reshape(-1)[:k].reshape(shapes[n])
    return out


def kernel(x, c, positions, w_ada, b_ada, norm1_g, w_in, b_merge, gla_w_alpha, gla_b_alpha, gla_out_norm_g, gla_w_o, mla_q_lat_g, mla_w_uq, mla_kv_lat_g, mla_w_ukv, mla_qn_g, mla_kn_g, mla_w_o, w_out, norm2_g, mlp_w1, mlp_w2, loss_target, m_w_ada, m_b_ada, m_norm1_g, m_w_in, m_b_merge, m_gla_w_alpha, m_gla_b_alpha, m_gla_out_norm_g, m_gla_w_o, m_mla_q_lat_g, m_mla_w_uq, m_mla_kv_lat_g, m_mla_w_ukv, m_mla_qn_g, m_mla_kn_g, m_mla_w_o, m_w_out, m_norm2_g, m_mlp_w1, m_mlp_w2, v_w_ada, v_b_ada, v_norm1_g, v_w_in, v_b_merge, v_gla_w_alpha, v_gla_b_alpha, v_gla_out_norm_g, v_gla_w_o, v_mla_q_lat_g, v_mla_w_uq, v_mla_kv_lat_g, v_mla_w_ukv, v_mla_qn_g, v_mla_kn_g, v_mla_w_o, v_w_out, v_norm2_g, v_mlp_w1, v_mlp_w2):
    args = dict(locals())
    wts = {n: args[n][0] for n in WEIGHTS}
    mom = {n: args["m_" + n][0] for n in WEIGHTS}
    var = {n: args["v_" + n][0] for n in WEIGHTS}
    my_c = lax.axis_index("c")
    my_dev = 4 * lax.axis_index("x") + 2 * lax.axis_index("y") + my_c
    bsz = x.shape[0]
    sp = {n: wts[n].reshape(1, -1) for n in SMALL}

    gathered_a, c_gathered = _all_gather([_pack_slab(wts, SLAB_A), jnp.pad(c, ((0, 8 - bsz), (0, 0)))], "weights_all_gather")
    wt = _unpack_gathered(gathered_a, SLAB_A)

    c_all = c_gathered[:, :bsz].reshape(N_DEV * bsz, D)
    bias = lax.dynamic_slice_in_dim(sp["b_ada"], my_dev * ADA_COLS, ADA_COLS, axis=1)
    mod_cols = _mm(c_all, wts["w_ada"], "nn", (F32,), "ada_fwd", pro=_silu, epi=lambda acc, b: (acc + b,),
                   extras=(jnp.broadcast_to(bias, (N_DEV * bsz, ADA_COLS)),))
    mod_all, = _all_gather([mod_cols], "mod_all_gather")
    mod_mine = lax.dynamic_slice_in_dim(mod_all, my_dev * bsz, bsz, axis=1)
    mod3 = jnp.transpose(mod_mine, (1, 0, 2)).reshape(bsz, 6, D)

    grad_x, parts_a, parts_b, rows = _local_step(x, positions, loss_target, wt, _pack_slab(wts, SLAB_B), sp, mod3)

    big = dict(_adamw_shards(_slab_sum(parts_a, "rs_slab_sum_a"), SLAB_A, wts, mom, var),
               **_adamw_shards(_slab_sum(parts_b, "rs_slab_sum_b"), SLAB_B, wts, mom, var))

    order = ["dmod", "norm1_g", "norm2_g", "b_merge", "gla_b_alpha", "gla_out_norm_g", "mla_q_lat_g", "mla_kv_lat_g",
             "mla_qn_g", "mla_kn_g", "loss"]
    part_rows = jnp.concatenate([rows[n] for n in order], axis=0)
    part_rows = jnp.pad(part_rows, ((0, SMALL_ROWS - part_rows.shape[0]), (0, 0)))
    all_rows, = _all_gather([part_rows], "partials_all_gather")

    dmod_all = all_rows[:, :6 * bsz].reshape(N_DEV * bsz, 6 * D)
    dmod_cols = lax.dynamic_slice_in_dim(dmod_all, my_dev * ADA_COLS, ADA_COLS, axis=1)
    g_ada = _mm(c_all, dmod_cols, "tn", (F32,), "ada_dw", pro=_silu)
    big["w_ada"] = (g_ada,) + tuple(_adamw(g_ada, wts["w_ada"], mom["w_ada"], var["w_ada"], "adamw_w_ada"))

    small = _adamw_small(all_rows, _small_pack({n: wts[n] for n in SMALL}), _small_pack({n: mom[n] for n in SMALL}),
                         _small_pack({n: var[n] for n in SMALL}))
    loss = small[4][0, 0]
    small_shapes = {n: wts[n].shape for n in SMALL}
    small = [_small_unpack(o, small_shapes) for o in small[:4]]

    outs = [loss, grad_x]
    for k in range(4):
        for n in WEIGHTS:
            val = big[n][k] if n in BIG else small[k][n]
            outs.append(val.reshape((1,) + tuple(wts[n].shape)))
    return tuple(outs)
```

```python
import jax
import jax.numpy as jnp
from jax import lax
from jax.experimental import pallas as pl
from jax.experimental.pallas import tpu as pltpu

F32 = jnp.float32
BF16 = jnp.bfloat16
MESH = pl.DeviceIdType.MESH

D = 1024
EPS = 1e-6
CHUNK = 64
GH, GDK, GDV, GLR, GTAU = 4, 128, 256, 16, 16.0
MH, MQR, MKVR, NOPE, ROPE, MV = 16, 256, 128, 64, 32, 64
MQK = NOPE + ROPE
HP = 128
FF = 4 * D
ROPE_THETA = 10000.0
IN_WIDTH = 5552
PW = 5632
N_DEV = 8
LANES = 128
SLAB_BLOCK_MAX = 400
ADA_COLS = 6 * D // N_DEV
SMALL_ROWS = 32
SMALL_SOURCES = tuple([(r, 6 + r) for r in range(6)] + [(12,), (13,), (14,), (15,), (16,), (17, 18, 19, 20),
                                                         (21,), (22,), (23,), (24,)])
LOSS_SOURCE = (25,)
VMEM_LIMIT = 56 * 1024 * 1024

ADAM_LR, ADAM_B1, ADAM_B2, ADAM_EPS, ADAM_WD, ADAM_STEP = 0.001, 0.9, 0.999, 1e-08, 0.01, 10

SLAB_A = (("w_in", 694, "T"), ("gla_w_alpha", 1, "TR"), (None, 9, None), ("mla_w_uq", 48, "TR"), ("mla_w_ukv", 32, "TR"))
SLAB_B = (("mlp_w1", 512, "T"), ("gla_w_o", 128, "N"), ("mla_w_o", 128, "N"), ("w_out", 128, "N"), ("mlp_w2", 512, "N"))
BIG = ("w_ada",) + tuple(n for n, _, _ in SLAB_A + SLAB_B if n is not None)
SHARD_SHAPES = {"w_ada": (D, 6 * D // N_DEV), "w_in": (D, IN_WIDTH // N_DEV), "gla_w_alpha": (GLR, GH * GDK // N_DEV),
                "gla_w_o": (GH * GDV // N_DEV, D), "mla_w_uq": (MQR, MH * MQK // N_DEV),
                "mla_w_ukv": (MKVR, MH * (NOPE + MV) // N_DEV), "mla_w_o": (MH * MV // N_DEV, D), "w_out": (D // N_DEV, D),
                "mlp_w1": (D, FF // N_DEV), "mlp_w2": (FF // N_DEV, D)}
SMALL = ("b_ada", "norm1_g", "norm2_g", "b_merge", "gla_b_alpha", "gla_out_norm_g", "mla_q_lat_g", "mla_kv_lat_g",
         "mla_qn_g", "mla_kn_g")
SMALL_ROW = {"b_ada": 0, "norm1_g": 6, "norm2_g": 7, "b_merge": 8, "gla_b_alpha": 10, "gla_out_norm_g": 11,
             "mla_q_lat_g": 12, "mla_kv_lat_g": 13, "mla_qn_g": 14, "mla_kn_g": 15}
WEIGHTS = ("w_ada", "b_ada", "norm1_g", "w_in", "b_merge", "gla_w_alpha", "gla_b_alpha", "gla_out_norm_g", "gla_w_o",
           "mla_q_lat_g", "mla_w_uq", "mla_kv_lat_g", "mla_w_ukv", "mla_qn_g", "mla_kn_g", "mla_w_o", "w_out",
           "norm2_g", "mlp_w1", "mlp_w2")


def _cparams(sem=None):
    return pltpu.CompilerParams(dimension_semantics=sem, vmem_limit_bytes=VMEM_LIMIT)


def _tile(n, pref):
    for t in (2048, PW // 4, 1024, 512, 256, 128):
        if t <= pref and n % t == 0:
            return t
    return n


def _dot(a, b, dims, precision=None):
    return lax.dot_general(a, b, (dims, ((), ())), preferred_element_type=F32, precision=precision)


NN = ((1,), (0,))
NT = ((1,), (1,))
TN = ((0,), (0,))


def _sigmoid(x):
    return 1.0 / (1.0 + jnp.exp(-x))


def _silu(x):
    return x * _sigmoid(x)


def _mm(a, b, mode, out_dtypes, name, *, pro=None, pro_b=None, epi=None, extras=(), a_off=0, m=None, tm=2048, tn=1024,
        tk=1024, cargo=None, tail=None):
    if mode == "tn":
        kc, n = b.shape
        m = a.shape[1] if m is None else m
    elif mode == "nn":
        m, kc = a.shape
        n = b.shape[1]
    else:
        m, kc = a.shape
        n = b.shape[0]
    tm, tn, tk = _tile(m, tm), _tile(n, tn), _tile(kc, tk)
    nk = kc // tk
    dims = {"nn": NN, "nt": NT, "tn": TN}[mode]
    if mode == "tn":
        a_spec = pl.BlockSpec((tk, tm), lambda i, j, k: (k, i + a_off))
    else:
        a_spec = pl.BlockSpec((tm, tk), lambda i, j, k: (i + a_off, k))
    if mode == "nt":
        b_spec = pl.BlockSpec((tn, tk), lambda i, j, k: (j, k))
    else:
        b_spec = pl.BlockSpec((tk, tn), lambda i, j, k: (k, j))
    o_spec = pl.BlockSpec((tm, tn), lambda i, j, k: (i, j))
    n_ex, n_out = len(extras), len(out_dtypes)
    grid = (m // tm, n // tn, nk)
    has_cargo = cargo is not None
    t_ins, t_outs = (tail["ins"], tail["outs"]) if tail else ([], [])
    assert not tail or grid[1] == 1

    def body(a_ref, b_ref, *rest):
        rest = list(rest)
        take = lambda count: [rest.pop(0) for _ in range(count)]
        ex, cargo_ref, tail_in = take(n_ex), take(has_cargo), take(len(t_ins))
        outs, parts_ref, tail_out = take(n_out), take(has_cargo), take(len(t_outs))
        acc = rest.pop(0)
        steps = [pl.program_id(axis) for axis in range(3)]
        if has_cargo:
            exchange = _chip_exchange_copies(cargo_ref[0], parts_ref[0], *rest)
            first = (steps[0] == 0) & (steps[1] == 0) & (steps[2] == 0)
            last = (steps[0] == grid[0] - 1) & (steps[1] == grid[1] - 1) & (steps[2] == grid[2] - 1)
            pl.when(first)(exchange.start)
        k = steps[2]

        @pl.when(k == 0)
        def _():
            acc[...] = jnp.zeros_like(acc)

        av = a_ref[...]
        if pro is not None:
            av = pro(av)
        bv = b_ref[...]
        if pro_b is not None:
            bv = pro_b(bv)
        acc[...] += _dot(av.astype(BF16), bv.astype(BF16), dims)

        @pl.when(k == nk - 1)
        def _():
            if tail:
                tail["fn"](acc[...], steps[0], *tail_in, *tail_out)
            res = (acc[...],) if epi is None else epi(acc[...], *[e[...] for e in ex])
            for o_ref, r in zip(outs, res):
                o_ref[...] = r.astype(o_ref.dtype)

        if has_cargo:
            pl.when(last)(exchange.finish)

    cargo_in = [cargo] if has_cargo else []
    cargo_spec = [HBM_SPEC] * len(cargo_in)
    sequential = has_cargo or bool(tail)
    out = pl.pallas_call(
        body, name=name, grid=grid,
        in_specs=[a_spec, b_spec] + [o_spec] * n_ex + cargo_spec + (tail["in_specs"] if tail else []),
        out_specs=[o_spec] * n_out + cargo_spec + (tail["out_specs"] if tail else []),
        out_shape=[jax.ShapeDtypeStruct((m, n), dt) for dt in out_dtypes]
        + [jax.ShapeDtypeStruct(c.shape, c.dtype) for c in cargo_in] + t_outs,
        scratch_shapes=[pltpu.VMEM((tm, tn), F32)] + (EXCHANGE_SEMS if has_cargo else []),
        compiler_params=_cparams(("arbitrary",) * 3 if sequential else ("parallel", "parallel", "arbitrary")),
    )(a, b, *extras, *cargo_in, *t_ins)
    return out[0] if len(out) == 1 else out


def _rows(s):
    return _tile(s, 512)


def _mod_spec():
    return pl.BlockSpec((1, 6, D), lambda b, i: (b, 0, 0))


def _tok_spec(tr, nb, width=D, col=0):
    return pl.BlockSpec((tr, width), lambda b, i: (b * nb + i, col))


def _modulated_norm(xv, gv, mod_ref, i_shift, i_scale):
    r = lax.rsqrt(jnp.mean(xv * xv, axis=1, keepdims=True) + EPS)
    return ((xv * r) * gv) * (1.0 + mod_ref[0, i_scale:i_scale + 1, :]) + mod_ref[0, i_shift:i_shift + 1, :]


def _norm_mod_fwd(x, g, mod3, i_shift, i_scale, name):
    bsz, _, _ = mod3.shape
    t = x.shape[0]
    s = t // bsz
    tr = _rows(s)
    nb = s // tr

    def body(x_ref, g_ref, mod_ref, h_ref):
        h_ref[...] = _modulated_norm(x_ref[...], g_ref[...], mod_ref, i_shift, i_scale).astype(BF16)

    tok = _tok_spec(tr, nb)
    return pl.pallas_call(
        body, name=name, grid=(bsz, nb), in_specs=[tok, pl.BlockSpec((1, D), lambda b, i: (0, 0)), _mod_spec()],
        out_specs=tok, out_shape=jax.ShapeDtypeStruct((t, D), BF16), compiler_params=_cparams(("arbitrary", "arbitrary")),
    )(x, g, mod3)


def _norm_fwd_tail(x, g, mod3, i_shift, i_scale, i_gate, tm):
    bsz, t = mod3.shape[0], x.shape[0]
    per_b = t // bsz // tm

    def fn(mixedv, i, x_ref, g_ref, mod_ref, mixed_ref, x1_ref, h_ref):
        mixed_ref[...] = mixedv
        xv = x_ref[...] + mod_ref[0, i_gate:i_gate + 1, :] * mixedv
        x1_ref[...] = xv
        h_ref[...] = _modulated_norm(xv, g_ref[...], mod_ref, i_shift, i_scale).astype(BF16)

    tok = pl.BlockSpec((tm, D), lambda i, j, k: (i, 0))
    return dict(
        fn=fn, ins=[x, g, mod3], in_specs=[tok, pl.BlockSpec((1, D), lambda i, j, k: (0, 0)), _tail_mod_spec(per_b)],
        outs=[jax.ShapeDtypeStruct((t, D), F32), jax.ShapeDtypeStruct((t, D), F32), jax.ShapeDtypeStruct((t, D), BF16)],
        out_specs=[tok, tok, tok])


def _norm_bwd_rows(xv, dhv, dresv, gv, mod_ref, i_scale, accb, accg):
    r = lax.rsqrt(jnp.mean(xv * xv, axis=1, keepdims=True) + EPS)
    xn = xv * r
    accb[0, 0:1, :] += jnp.sum(dhv, axis=0, keepdims=True)
    accb[0, 1:2, :] += jnp.sum(dhv * (xn * gv), axis=0, keepdims=True)
    tt = dhv * (1.0 + mod_ref[0, i_scale:i_scale + 1, :])
    accg[0:1, :] += jnp.sum(tt * xn, axis=0, keepdims=True)
    dxn = tt * gv
    return dresv + r * (dxn - xn * jnp.mean(dxn * xn, axis=1, keepdims=True))


def _norm_bwd_tail(x, dres, g, mod3, i_scale, tm, mixed=None, i_gate=None):
    bsz, t = mod3.shape[0], x.shape[0]
    per_b = t // bsz // tm
    has_res = mixed is not None

    def fn(dhv, i, *refs):
        if has_res:
            x_ref, dres_ref, mx_ref, g_ref, mod_ref, dx_ref, dmx_ref, accb, accg = refs
        else:
            x_ref, dres_ref, g_ref, mod_ref, dx_ref, accb, accg = refs

        @pl.when(i % per_b == 0)
        def _():
            accb[...] = jnp.zeros_like(accb)

        @pl.when(i == 0)
        def _():
            accg[...] = jnp.zeros_like(accg)

        dx = _norm_bwd_rows(x_ref[...], dhv, dres_ref[...], g_ref[...], mod_ref, i_scale, accb, accg)
        dx_ref[...] = dx
        if has_res:
            accb[0, 2:3, :] += jnp.sum(dx * mx_ref[...], axis=0, keepdims=True)
            dmx_ref[...] = (dx * mod_ref[0, i_gate:i_gate + 1, :]).astype(BF16)

    tok = pl.BlockSpec((tm, D), lambda i, j, k: (i, 0))
    res = [mixed] if has_res else []
    return dict(
        fn=fn, ins=[x, dres] + res + [g, mod3],
        in_specs=[tok, tok] + [tok] * len(res) + [pl.BlockSpec((1, D), lambda i, j, k: (0, 0)), _tail_mod_spec(per_b)],
        outs=[jax.ShapeDtypeStruct((t, D), F32)] + [jax.ShapeDtypeStruct((t, D), BF16)] * len(res)
        + [jax.ShapeDtypeStruct((bsz, 8, D), F32), jax.ShapeDtypeStruct((8, D), F32)],
        out_specs=[tok] * (1 + len(res)) + [_tail_batch_spec(per_b), pl.BlockSpec((8, D), lambda i, j, k: (0, 0))])


def _tail_mod_spec(per_b):
    return pl.BlockSpec((1, 6, D), lambda i, j, k: (i // per_b, 0, 0))


def _tail_batch_spec(per_b):
    return pl.BlockSpec((1, 8, D), lambda i, j, k: (i // per_b, 0, 0))


def _loss_tail(x1, tgt, mod3, tm):
    bsz, t = mod3.shape[0], x1.shape[0]
    per_b = t // bsz // tm

    def fn(ffv, i, x1_ref, tg_ref, mod_ref, dy_ref, dff_ref, accb, accl):
        @pl.when(i % per_b == 0)
        def _():
            accb[...] = jnp.zeros_like(accb)

        @pl.when(i == 0)
        def _():
            accl[...] = jnp.zeros_like(accl)

        gate = mod_ref[0, 5:6, :]
        err = x1_ref[...] + gate * ffv - tg_ref[...]
        accl[0:1, :] += jnp.sum(err * err, axis=0, keepdims=True) * (0.5 / D)
        dy = err * (1.0 / D)
        dy_ref[...] = dy
        dff_ref[...] = (dy * gate).astype(BF16)
        accb[0, 0:1, :] += jnp.sum(dy * ffv, axis=0, keepdims=True)

    tok = pl.BlockSpec((tm, D), lambda i, j, k: (i, 0))
    return dict(
        fn=fn, ins=[x1, tgt, mod3], in_specs=[tok, tok, _tail_mod_spec(per_b)],
        outs=[jax.ShapeDtypeStruct((t, D), F32), jax.ShapeDtypeStruct((t, D), BF16),
              jax.ShapeDtypeStruct((bsz, 8, D), F32), jax.ShapeDtypeStruct((8, D), F32)],
        out_specs=[tok, tok, _tail_batch_spec(per_b), pl.BlockSpec((8, D), lambda i, j, k: (0, 0))])


def _merge_fwd(proj, b_merge, y_a, y_b):
    t = proj.shape[0]
    tr = _tile(t, 512)

    def body(la_ref, lb_ref, bm_ref, ya_ref, yb_ref, mix_ref):
        ga = _sigmoid(la_ref[...] + bm_ref[:, 0:D])
        gb = _sigmoid(lb_ref[...] + bm_ref[:, D:2 * D])
        mix_ref[...] = (ga * ya_ref[...].astype(F32) + gb * yb_ref[...].astype(F32)).astype(BF16)

    tok = pl.BlockSpec((tr, D), lambda i: (i, 0))
    return pl.pallas_call(
        body, name="merge_fwd", grid=(t // tr,),
        in_specs=[pl.BlockSpec((tr, D), lambda i: (i, 3)), pl.BlockSpec((tr, D), lambda i: (i, 4)),
                  pl.BlockSpec((1, 2 * D), lambda i: (0, 0)), tok, tok],
        out_specs=tok, out_shape=jax.ShapeDtypeStruct((t, D), BF16),
        compiler_params=_cparams(("arbitrary",)),
    )(proj, proj, b_merge, y_a, y_b)


def _merge_bwd(dmix, proj, b_merge, y_a, y_b):
    t = proj.shape[0]
    tr = _tile(t, 512)

    def body(dm_ref, la_ref, lb_ref, bm_ref, ya_ref, yb_ref, dya_ref, dyb_ref, dl_ref, acc):
        @pl.when(pl.program_id(0) == 0)
        def _():
            acc[...] = jnp.zeros_like(acc)

        dm = dm_ref[...].astype(F32)
        ga = _sigmoid(la_ref[...] + bm_ref[:, 0:D])
        gb = _sigmoid(lb_ref[...] + bm_ref[:, D:2 * D])
        dya_ref[...] = (dm * ga).astype(BF16)
        dyb_ref[...] = (dm * gb).astype(BF16)
        dla = dm * ya_ref[...].astype(F32) * ga * (1.0 - ga)
        dlb = dm * yb_ref[...].astype(F32) * gb * (1.0 - gb)
        dl_ref[:, 0:D] = dla.astype(BF16)
        dl_ref[:, D:2 * D] = dlb.astype(BF16)
        acc[0:1, 0:D] += jnp.sum(dla, axis=0, keepdims=True)
        acc[0:1, D:2 * D] += jnp.sum(dlb, axis=0, keepdims=True)

    tok = pl.BlockSpec((tr, D), lambda i: (i, 0))
    return pl.pallas_call(
        body, name="merge_bwd", grid=(t // tr,),
        in_specs=[tok, pl.BlockSpec((tr, D), lambda i: (i, 3)), pl.BlockSpec((tr, D), lambda i: (i, 4)),
                  pl.BlockSpec((1, 2 * D), lambda i: (0, 0)), tok, tok],
        out_specs=[tok, tok, pl.BlockSpec((tr, 2 * D), lambda i: (i, 0)), pl.BlockSpec((8, 2 * D), lambda i: (0, 0))],
        out_shape=[jax.ShapeDtypeStruct((t, D), BF16), jax.ShapeDtypeStruct((t, D), BF16),
                   jax.ShapeDtypeStruct((t, 2 * D), BF16), jax.ShapeDtypeStruct((8, 2 * D), F32)],
        compiler_params=_cparams(("arbitrary",)),
    )(dmix, proj, proj, b_merge, y_a, y_b)


GLA_HEADS = 2
GLA_UNROLL = 8


def _log_sigmoid(z):
    return jnp.minimum(z, 0.0) - jnp.log(1.0 + jnp.exp(-jnp.abs(z)))


def _tri(lower):
    r = lax.broadcasted_iota(jnp.int32, (CHUNK, CHUNK), 0)
    c = lax.broadcasted_iota(jnp.int32, (CHUNK, CHUNK), 1)
    return jnp.where(r >= c if lower else r <= c, 1.0, 0.0).astype(F32)


def _gla_fwd(proj, wa_pad, b_alpha, g_out, bsz):
    t = proj.shape[0]
    s = t // bsz
    nc = s // CHUNK
    p, kw, vw = GLA_HEADS, GLA_HEADS * GDK, GLA_HEADS * GDV

    def body(q_ref, k_ref, v_ref, gg_ref, ms_ref, wa_ref, ba_ref, go_ref, o_ref, og_ref, st_ref, la, state):
        z = _dot(ms_ref[...].astype(BF16), wa_ref[...], NN) + ba_ref[...]
        la[...] = _log_sigmoid(z) * (1.0 / GTAU)
        state[...] = jnp.zeros_like(state)
        low = _tri(True)
        gout = go_ref[...]

        def chunk(n, carry):
            rows = pl.ds(pl.multiple_of(n * CHUNK, CHUNK), CHUNK)
            for hh in range(p):
                kc, vc = slice(hh * GDK, (hh + 1) * GDK), slice(hh * GDV, (hh + 1) * GDV)
                lac = la[rows, kc]
                cum = _dot(low, lac, NN, lax.Precision.HIGHEST)
                ce = jnp.sum(lac, axis=0, keepdims=True)
                kd = (k_ref[rows, kc].astype(F32) * jnp.exp(ce - cum)).astype(BF16)
                new = state[vc, :] * jnp.exp(ce) + _dot(v_ref[rows, vc].astype(BF16), kd, TN)
                state[vc, :] = new
                st_ref[pl.ds(pl.multiple_of((hh * nc + n) * GDV, GDV), GDV), :] = new.astype(BF16)
                qs = (q_ref[rows, kc].astype(F32) * (GDK ** -0.5)).astype(BF16)
                o = _dot(qs, new.astype(BF16), NT)
                o_ref[rows, vc] = o
                ro = lax.rsqrt(jnp.mean(o * o, axis=1, keepdims=True) + EPS)
                og_ref[rows, vc] = (((o * ro) * gout) * _silu(gg_ref[rows, vc].astype(F32))).astype(BF16)
            return carry

        lax.fori_loop(0, nc, chunk, 0, unroll=GLA_UNROLL)

    return pl.pallas_call(
        body, name="gla_fwd", grid=(bsz, GH // p),
        in_specs=[pl.BlockSpec((s, kw), lambda b, h: (b, h)), pl.BlockSpec((s, kw), lambda b, h: (b, GH // p + h)),
                  pl.BlockSpec((s, vw), lambda b, h: (b, GH // p + h)), pl.BlockSpec((s, vw), lambda b, h: (b, 2 * GH // p + h)),
                  pl.BlockSpec((s, LANES), lambda b, h: (b, 43)),
                  pl.BlockSpec((LANES, kw), lambda b, h: (0, h)), pl.BlockSpec((1, kw), lambda b, h: (0, h)),
                  pl.BlockSpec((1, GDV), lambda b, h: (0, 0))],
        out_specs=[pl.BlockSpec((s, vw), lambda b, h: (b, h)), pl.BlockSpec((s, vw), lambda b, h: (b, h)),
                   pl.BlockSpec((p * nc * GDV, GDK), lambda b, h: (b * (GH // p) + h, 0))],
        out_shape=[jax.ShapeDtypeStruct((t, GH * GDV), F32), jax.ShapeDtypeStruct((t, GH * GDV), BF16),
                   jax.ShapeDtypeStruct((bsz * GH * nc * GDV, GDK), BF16)],
        scratch_shapes=[pltpu.VMEM((s, kw), F32), pltpu.VMEM((vw, GDK), F32)],
        compiler_params=_cparams(("arbitrary", "arbitrary")),
    )(proj, proj, proj, proj, proj, wa_pad, b_alpha, g_out)


def _gla_bwd(dog, o, proj, wa_pad, b_alpha, g_out, states, bsz):
    t = proj.shape[0]
    s = t // bsz
    nc = s // CHUNK
    p, kw, vw = GLA_HEADS, GLA_HEADS * GDK, GLA_HEADS * GDV

    def body(dog_ref, o_ref, q_ref, k_ref, v_ref, gg_ref, ms_ref, wa_ref, ba_ref, go_ref, st_ref,
             dq_ref, dk_ref, dv_ref, dgg_ref, dz_ref, dba, dgo, zs, la, carry_g):
        @pl.when(pl.program_id(1) == 0)
        def _():
            dba[...] = jnp.zeros_like(dba)
            dgo[...] = jnp.zeros_like(dgo)

        z = _dot(ms_ref[...].astype(BF16), wa_ref[...], NN) + ba_ref[...]
        zs[...] = z
        la[...] = _log_sigmoid(z) * (1.0 / GTAU)
        carry_g[...] = jnp.zeros_like(carry_g)
        low, upp = _tri(True), _tri(False)
        gout = go_ref[...]
        last_row = lax.broadcasted_iota(jnp.int32, (CHUNK, GDK), 0) == CHUNK - 1

        def chunk(step, carry):
            n = nc - 1 - step
            rows = pl.ds(pl.multiple_of(n * CHUNK, CHUNK), CHUNK)
            for hh in range(p):
                kc, vc = slice(hh * GDK, (hh + 1) * GDK), slice(hh * GDV, (hh + 1) * GDV)
                lac = la[rows, kc]
                cum = _dot(low, lac, NN, lax.Precision.HIGHEST)
                ce = jnp.sum(lac, axis=0, keepdims=True)
                e = jnp.exp(ce - cum)
                dec = jnp.exp(ce)
                kf = k_ref[rows, kc].astype(F32)
                kd = (kf * e).astype(BF16)
                vv = v_ref[rows, vc].astype(BF16)
                qs = (q_ref[rows, kc].astype(F32) * (GDK ** -0.5)).astype(BF16)
                ov = o_ref[rows, vc]
                ro = lax.rsqrt(jnp.mean(ov * ov, axis=1, keepdims=True) + EPS)
                on = ov * ro
                gg = gg_ref[rows, vc].astype(F32)
                sg = _sigmoid(gg)
                dogv = dog_ref[rows, vc].astype(F32)
                dgg_ref[rows, vc] = (dogv * (on * gout) * (sg * (1.0 + gg * (1.0 - sg)))).astype(BF16)
                t1 = dogv * (gg * sg)
                dgo[8 * hh:8 * hh + 1, :] += jnp.sum(t1 * on, axis=0, keepdims=True)
                don = t1 * gout
                do = ro * (don - on * jnp.mean(don * on, axis=1, keepdims=True))
                dob = do.astype(BF16)
                st_n = st_ref[pl.ds(pl.multiple_of((hh * nc + n) * GDV, GDV), GDV), :]
                dq_ref[rows, kc] = (_dot(dob, st_n, NN) * (GDK ** -0.5)).astype(BF16)
                dn = carry_g[vc, :] + _dot(dob, qs, TN)
                prev = hh * nc + jnp.maximum(n - 1, 0)
                st_p = st_ref[pl.ds(pl.multiple_of(prev * GDV, GDV), GDV), :].astype(F32) * jnp.where(n > 0, 1.0, 0.0)
                ddec = jnp.sum(dn * st_p, axis=0, keepdims=True)
                dnb = dn.astype(BF16)
                dkd = _dot(vv, dnb, NN)
                dv_ref[rows, vc] = _dot(kd, dnb, NT).astype(BF16)
                dk_ref[rows, kc] = (dkd * e).astype(BF16)
                w = dkd * kf * e
                dce = jnp.sum(w, axis=0, keepdims=True) + ddec * dec
                dcum = jnp.where(last_row, dce - w, -w)
                dla = _dot(upp, dcum, NN, lax.Precision.HIGHEST)
                dz = dla * (1.0 / GTAU) * _sigmoid(-zs[rows, kc])
                dba[0:1, kc] += jnp.sum(dz, axis=0, keepdims=True)
                dz_ref[rows, kc] = dz.astype(BF16)
                carry_g[vc, :] = dn * dec
            return carry

        lax.fori_loop(0, nc, chunk, 0, unroll=GLA_UNROLL)

    hv = pl.BlockSpec((s, vw), lambda h, b: (b, h))
    hk = pl.BlockSpec((s, kw), lambda h, b: (b, h))
    return pl.pallas_call(
        body, name="gla_bwd", grid=(GH // p, bsz),
        in_specs=[hv, hv, hk, pl.BlockSpec((s, kw), lambda h, b: (b, GH // p + h)),
                  pl.BlockSpec((s, vw), lambda h, b: (b, GH // p + h)), pl.BlockSpec((s, vw), lambda h, b: (b, 2 * GH // p + h)),
                  pl.BlockSpec((s, LANES), lambda h, b: (b, 43)), pl.BlockSpec((LANES, kw), lambda h, b: (0, h)),
                  pl.BlockSpec((1, kw), lambda h, b: (0, h)), pl.BlockSpec((1, GDV), lambda h, b: (0, 0)),
                  pl.BlockSpec((p * nc * GDV, GDK), lambda h, b: (b * (GH // p) + h, 0))],
        out_specs=[hk, hk, hv, hv, hk, pl.BlockSpec((8, kw), lambda h, b: (0, h)),
                   pl.BlockSpec((8 * p, GDV), lambda h, b: (h, 0))],
        out_shape=[jax.ShapeDtypeStruct((t, GH * GDK), BF16), jax.ShapeDtypeStruct((t, GH * GDK), BF16),
                   jax.ShapeDtypeStruct((t, GH * GDV), BF16), jax.ShapeDtypeStruct((t, GH * GDV), BF16),
                   jax.ShapeDtypeStruct((t, GH * GDK), BF16), jax.ShapeDtypeStruct((8, GH * GDK), F32),
                   jax.ShapeDtypeStruct((8 * GH, GDV), F32)],
        scratch_shapes=[pltpu.VMEM((s, kw), F32), pltpu.VMEM((s, kw), F32), pltpu.VMEM((vw, GDK), F32)],
        compiler_params=_cparams(("arbitrary", "arbitrary")),
    )(dog, o, proj, proj, proj, proj, proj, wa_pad, b_alpha, g_out, states)


def _rope_tables(pos_ref, fr_ref, sg_ref):
    ang = pos_ref[...].astype(F32) * fr_ref[...]
    return jnp.cos(ang), jnp.sin(ang) * sg_ref[...]


def _partner(x):
    lane = lax.broadcasted_iota(jnp.int32, x.shape, 1)
    return jnp.where(lane < NOPE + ROPE // 2, pltpu.roll(x, LANES - ROPE // 2, 1), pltpu.roll(x, ROPE // 2, 1))


def _mla_rows(t):
    return _tile(t, 512)


def _mla_pre_fwd(proj, pos, fr, sg, q_lat_g, kv_lat_g, qn_g, kn_g, wuq, wukv):
    t = proj.shape[0]
    tr = _mla_rows(t)

    def body(cq_ref, ckv_ref, ms_ref, pos_ref, fr_ref, sg_ref, qlg, kvlg, qng, kng, wuq_ref, wukv_ref, q_out, k_out, v_out):
        lane = lax.broadcasted_iota(jnp.int32, (tr, HP), 1)
        real = jnp.where(lane < MQK, 1.0, 0.0)
        cos, sin = _rope_tables(pos_ref, fr_ref, sg_ref)
        cos = cos * real
        cq = cq_ref[...].astype(F32)
        cqn = (cq * lax.rsqrt(jnp.mean(cq * cq, axis=1, keepdims=True) + EPS) * qlg[...]).astype(BF16)
        ckv = ckv_ref[...].astype(F32)
        ckvn = (ckv * lax.rsqrt(jnp.mean(ckv * ckv, axis=1, keepdims=True) + EPS) * kvlg[...]).astype(BF16)
        kpe = jnp.where((lane >= NOPE) & (lane < MQK), ms_ref[...].astype(F32), 0.0)
        kpe = kpe + jnp.where(lane < MQK + ROPE // 2, pltpu.roll(kpe, ROPE, 1), 0.0)
        lane_all = lax.broadcasted_iota(jnp.int32, (tr, MH * HP), 1)
        v_out[...] = jnp.where(lane_all % HP == MV, 1.0, _dot(ckvn, wukv_ref[:, MH * HP:], NN)).astype(BF16)

        def norm_rope(x, gain):
            xn = x * lax.rsqrt(jnp.sum(x * x * real, axis=1, keepdims=True) * (1.0 / MQK) + EPS) * gain
            return (xn * cos + pltpu.roll(xn, LANES - ROPE // 2, 1) * sin).astype(BF16)

        for h in range(MH):
            cols = slice(h * HP, (h + 1) * HP)
            q_out[:, cols] = norm_rope(_dot(cqn, wuq_ref[:, cols], NN), qng[...])
            k_out[:, cols] = norm_rope(_dot(ckvn, wukv_ref[:, cols], NN) + kpe, kng[...])

    def full(a):
        return pl.BlockSpec(a.shape, lambda i: (0, 0))

    wide = pl.BlockSpec((tr, MH * HP), lambda i: (i, 0))
    return pl.pallas_call(
        body, name="mla_pre_fwd", grid=(t // tr,),
        in_specs=[pl.BlockSpec((tr, MQR), lambda i: (i, 20)), pl.BlockSpec((tr, MKVR), lambda i: (i, 42)),
                  pl.BlockSpec((tr, LANES), lambda i: (i, 43)), pl.BlockSpec((tr, 1), lambda i: (i, 0)),
                  full(fr), full(sg), full(q_lat_g), full(kv_lat_g), full(qn_g), full(kn_g), full(wuq), full(wukv)],
        out_specs=[wide, wide, wide],
        out_shape=[jax.ShapeDtypeStruct((t, MH * HP), BF16)] * 3,
        compiler_params=_cparams(("arbitrary",)),
    )(proj, proj, proj, pos, fr, sg, q_lat_g, kv_lat_g, qn_g, kn_g, wuq, wukv)


def _mla_pre_bwd(dq2, dk2, dv2, dmisc_gla, proj, pos, fr, sg, q_lat_g, kv_lat_g, qn_g, kn_g, wuq, wukv):
    t = proj.shape[0]
    tr = _mla_rows(t)

    def body(dq_ref, dk_ref, dv_ref, dmg_ref, cq_ref, ckv_ref, ms_ref, pos_ref, fr_ref, sg_ref, qlg, kvlg, qng, kng,
             wuq_ref, wukv_ref, dcq_ref, dckv_ref, dms_ref, dwuq, dwukv, acc, dqf, dkvf):
        @pl.when(pl.program_id(0) == 0)
        def _():
            dwuq[...] = jnp.zeros_like(dwuq)
            dwukv[...] = jnp.zeros_like(dwukv)
            acc[...] = jnp.zeros_like(acc)

        cos, sin = _rope_tables(pos_ref, fr_ref, sg_ref)
        cq = cq_ref[...].astype(F32)
        rc = lax.rsqrt(jnp.mean(cq * cq, axis=1, keepdims=True) + EPS)
        xc = cq * rc
        cqn = (xc * qlg[...]).astype(BF16)
        ckv = ckv_ref[...].astype(F32)
        rkv = lax.rsqrt(jnp.mean(ckv * ckv, axis=1, keepdims=True) + EPS)
        xkv = ckv * rkv
        ckvn = (xkv * kvlg[...]).astype(BF16)
        lane = lax.broadcasted_iota(jnp.int32, (tr, HP), 1)
        is_rope = (lane >= NOPE) & (lane < MQK)
        kpe = jnp.where(is_rope, ms_ref[...].astype(F32), 0.0)
        dkpe = jnp.zeros((tr, HP), F32)
        dqng = jnp.zeros((1, HP), F32)
        dkng = jnp.zeros((1, HP), F32)
        for h in range(MH):
            cols = slice(h * HP, (h + 1) * HP)
            qh = _dot(cqn, wuq_ref[:, cols], NN)
            rq = lax.rsqrt(jnp.sum(qh * qh, axis=1, keepdims=True) * (1.0 / MQK) + EPS)
            xq = qh * rq
            dy = dq_ref[:, cols].astype(F32)
            dqn = dy * cos - _partner(dy) * sin
            dqng += jnp.sum(dqn * xq, axis=0, keepdims=True)
            tq = dqn * qng[...]
            dqf[:, cols] = (rq * (tq - xq * (jnp.sum(tq * xq, axis=1, keepdims=True) * (1.0 / MQK)))).astype(BF16)
            kh = _dot(ckvn, wukv_ref[:, cols], NN) + kpe
            rk = lax.rsqrt(jnp.sum(kh * kh, axis=1, keepdims=True) * (1.0 / MQK) + EPS)
            xk = kh * rk
            dy = dk_ref[:, cols].astype(F32)
            dkn = dy * cos - _partner(dy) * sin
            dkng += jnp.sum(dkn * xk, axis=0, keepdims=True)
            tk = dkn * kng[...]
            dkh = rk * (tk - xk * (jnp.sum(tk * xk, axis=1, keepdims=True) * (1.0 / MQK)))
            dkvf[:, cols] = jnp.where(lane < NOPE, dkh, 0.0).astype(BF16)
            dkpe += jnp.where(is_rope, dkh, 0.0)
        dkvf[:, MH * HP:] = dv_ref[...]
        acc[2:3, 0:HP] += dqng
        acc[3:4, 0:HP] += dkng
        dms_ref[...] = (dmg_ref[...] + dkpe).astype(BF16)

        dqfv = dqf[...]
        dwuq[...] += _dot(cqn, dqfv, TN)
        dcqn = _dot(dqfv, wuq_ref[...], NT)
        acc[0:1, :] += jnp.sum(dcqn * xc, axis=0, keepdims=True)
        tc = dcqn * qlg[...]
        dcq_ref[...] = (rc * (tc - xc * jnp.mean(tc * xc, axis=1, keepdims=True))).astype(BF16)

        dkvfv = dkvf[...]
        dwukv[...] += _dot(ckvn, dkvfv, TN)
        dckvn = _dot(dkvfv, wukv_ref[...], NT)
        acc[1:2, 0:MKVR] += jnp.sum(dckvn * xkv, axis=0, keepdims=True)
        tkv = dckvn * kvlg[...]
        dckv_ref[...] = (rkv * (tkv - xkv * jnp.mean(tkv * xkv, axis=1, keepdims=True))).astype(BF16)

    def full(a):
        return pl.BlockSpec(a.shape, lambda i: (0, 0))

    wide = pl.BlockSpec((tr, MH * HP), lambda i: (i, 0))
    narrow = pl.BlockSpec((tr, LANES), lambda i: (i, 0))
    return pl.pallas_call(
        body, name="mla_pre_bwd", grid=(t // tr,),
        in_specs=[wide, wide, wide, narrow,
                  pl.BlockSpec((tr, MQR), lambda i: (i, 20)), pl.BlockSpec((tr, MKVR), lambda i: (i, 42)),
                  pl.BlockSpec((tr, LANES), lambda i: (i, 43)), pl.BlockSpec((tr, 1), lambda i: (i, 0)),
                  full(fr), full(sg), full(q_lat_g), full(kv_lat_g), full(qn_g), full(kn_g), full(wuq), full(wukv)],
        out_specs=[pl.BlockSpec((tr, MQR), lambda i: (i, 0)), narrow, narrow,
                   pl.BlockSpec((MQR, MH * HP), lambda i: (0, 0)), pl.BlockSpec((MKVR, 2 * MH * HP), lambda i: (0, 0)),
                   pl.BlockSpec((8, MQR), lambda i: (0, 0))],
        out_shape=[jax.ShapeDtypeStruct((t, MQR), BF16), jax.ShapeDtypeStruct((t, MKVR), BF16),
                   jax.ShapeDtypeStruct((t, LANES), BF16), jax.ShapeDtypeStruct((MQR, MH * HP), F32),
                   jax.ShapeDtypeStruct((MKVR, 2 * MH * HP), F32), jax.ShapeDtypeStruct((8, MQR), F32)],
        scratch_shapes=[pltpu.VMEM((tr, MH * HP), BF16), pltpu.VMEM((tr, 2 * MH * HP), BF16)],
        compiler_params=_cparams(("arbitrary",)),
    )(dq2, dk2, dv2, dmisc_gla, proj, proj, proj, pos, fr, sg, q_lat_g, kv_lat_g, qn_g, kn_g, wuq, wukv)


ATT_FWD_TILES = (1024, 512)
ATT_BWD_TILES = (512, 512)
ATT_HEADS = 2
NEG = -1e30
LOG2E = 1.4426950408889634


def _att_mask(q0, k0, tq, tk):
    qc = (q0 + lax.broadcasted_iota(jnp.int32, (tq, tk), 0)) // CHUNK
    kc = (k0 + lax.broadcasted_iota(jnp.int32, (tq, tk), 1)) // CHUNK
    return kc <= qc


def _att_tiles(s, tiles):
    return _tile(s, tiles[0]), _tile(s, tiles[1])


def _lanes(x, n):
    return x if n == 1 else jnp.concatenate([x] * n, axis=1)


def _grid_ends(grid):
    i, j = pl.program_id(0), pl.program_id(1)
    return (i == 0) & (j == 0), (i == grid[0] - 1) & (j == grid[1] - 1)


def _attn_fwd(q2, k2, v2, bsz, slab):
    t = q2.shape[0]
    s = t // bsz
    tq, tk = _att_tiles(s, ATT_FWD_TILES)
    nq, groups, n_diag = s // tq, tk // HP, max(tq // tk, 1)
    sub_rows = tq // n_diag
    scale = MQK ** -0.5
    c2 = scale * LOG2E
    heads = range(ATT_HEADS)

    def body(q_ref, k_ref, v_ref, slab_ref, o_ref, lse_ref, gath_ref, send_sems, recv_sems, local_sem):
        gather = _core_row_gather_copies(slab_ref, gath_ref, send_sems, recv_sems, local_sem)
        first, last = _grid_ends((bsz, MH // ATT_HEADS))
        pl.when(first)(gather.start)

        def q_loop(qi, carry):
            q0 = pl.multiple_of(qi * tq, tq)
            rows = pl.ds(q0, tq)
            n_full = q0 // tk
            qs = [q_ref[rows, h * HP:(h + 1) * HP] for h in heads]

            def scores(h, kj, sub=None, masked=False):
                k0 = pl.multiple_of(kj * tk, tk)
                qv = qs[h] if sub is None else qs[h][sub * sub_rows:(sub + 1) * sub_rows]
                sc = _dot(qv, k_ref[pl.ds(k0, tk), h * HP:(h + 1) * HP], NT)
                return jnp.where(_att_mask(q0 + sub * sub_rows, k0, sub_rows, tk), sc, NEG) if masked else sc

            def fold(mx, sc):
                for j in range(groups):
                    mx = jnp.maximum(mx, sc[:, j * HP:(j + 1) * HP])
                return mx

            def over_diagonal(vals, step):
                out = []
                for h in heads:
                    blocks = []
                    for r in range(n_diag):
                        v = vals[h][r * sub_rows:(r + 1) * sub_rows]
                        for u in range(r + 1):
                            v = step(v, h, n_full + u, r, u == r)
                        blocks.append(v)
                    out.append(blocks[0] if n_diag == 1 else jnp.concatenate(blocks, axis=0))
                return tuple(out)

            mx = lax.fori_loop(0, n_full, lambda kj, mx: tuple(fold(mx[h], scores(h, kj)) for h in heads),
                               tuple(jnp.full((tq, HP), NEG, F32) for _ in heads))
            mx = over_diagonal(mx, lambda v, h, kj, r, masked: fold(v, scores(h, kj, r, masked)))
            mb = [jnp.broadcast_to(jnp.max(mx[h], axis=1, keepdims=True), (tq, HP)) for h in heads]

            def weighted(h, kj, sub=None, masked=False):
                m = mb[h] if sub is None else mb[h][sub * sub_rows:(sub + 1) * sub_rows]
                p = jnp.exp2((scores(h, kj, sub, masked) - _lanes(m, groups)) * c2)
                k0 = pl.multiple_of(kj * tk, tk)
                return _dot(p.astype(BF16), v_ref[pl.ds(k0, tk), h * HP:(h + 1) * HP], NN)

            acc = lax.fori_loop(0, n_full, lambda kj, acc: tuple(acc[h] + weighted(h, kj) for h in heads),
                                tuple(jnp.zeros((tq, HP), F32) for _ in heads))
            acc = over_diagonal(acc, lambda v, h, kj, r, masked: v + weighted(h, kj, r, masked))
            lane = lax.broadcasted_iota(jnp.int32, (tq, HP), 1)
            for h in heads:
                a = acc[h]
                l = jnp.sum(jnp.where(lane == MV, a, 0.0), axis=1, keepdims=True)
                o_ref[rows, h * HP:(h + 1) * HP] = (a / l).astype(BF16)
                lse_ref[rows, h * HP:(h + 1) * HP] = mb[h] * scale + jnp.log(l)
            return carry

        lax.fori_loop(0, nq, q_loop, 0)
        pl.when(last)(gather.finish)

    spec = pl.BlockSpec((s, ATT_HEADS * HP), lambda b, h: (b, h))
    return pl.pallas_call(
        body, name="attn_fwd", grid=(bsz, MH // ATT_HEADS), in_specs=[spec] * 3 + [HBM_SPEC],
        out_specs=[spec, spec, HBM_SPEC],
        out_shape=[jax.ShapeDtypeStruct((t, MH * HP), BF16), jax.ShapeDtypeStruct((t, MH * HP), F32),
                   jax.ShapeDtypeStruct((N_DEV,) + slab.shape, slab.dtype)],
        scratch_shapes=EXCHANGE_SEMS, compiler_params=_cparams(("arbitrary", "arbitrary")),
    )(q2, k2, v2, slab)


def _attn_bwd(q2, k2, v2, do2, o2, lse2, bsz, tsum):
    t = q2.shape[0]
    s = t // bsz
    tq, tk = _att_tiles(s, ATT_BWD_TILES)
    nq, nk, per, groups = s // tq, s // tk, max(tk // tq, 1), tk // HP
    scale = MQK ** -0.5
    c2 = scale * LOG2E
    heads = range(ATT_HEADS)

    def body(q_ref, k_ref, v_ref, do_ref, o_ref, lse_ref, t_ref, dq_ref, dk_ref, dv_ref, parts_ref, dq_acc, delta, lse_b2,
             send_sems, recv_sems, local_sem):
        exchange = _all_to_all_copies(t_ref, parts_ref, send_sems, recv_sems, local_sem)
        first, last = _grid_ends((bsz, MH // ATT_HEADS))
        pl.when(first)(exchange.start)
        dq_acc[...] = jnp.zeros_like(dq_acc)

        def d_loop(i, carry):
            rows = pl.ds(pl.multiple_of(i * tq, tq), tq)
            for h in heads:
                hs = slice(h * HP, (h + 1) * HP)
                dl = jnp.sum(do_ref[rows, hs].astype(F32) * o_ref[rows, hs].astype(F32), axis=1, keepdims=True)
                delta[rows, hs] = jnp.broadcast_to(dl, (tq, HP))
            lse_b2[rows, :] = lse_ref[rows, :] * LOG2E
            return carry

        lax.fori_loop(0, nq, d_loop, 0)

        def k_loop(kj, carry):
            k0 = pl.multiple_of(kj * tk, tk)
            kk = [k_ref[pl.ds(k0, tk), h * HP:(h + 1) * HP] for h in heads]
            vv = [v_ref[pl.ds(k0, tk), h * HP:(h + 1) * HP] for h in heads]

            def tile(qi, c, masked):
                q0 = pl.multiple_of(qi * tq, tq)
                rows = pl.ds(q0, tq)
                out = []
                for h in heads:
                    hs = slice(h * HP, (h + 1) * HP)
                    dk, dv = c[h]
                    q = q_ref[rows, hs]
                    do = do_ref[rows, hs]
                    e = _dot(q, kk[h], NT) * c2 - _lanes(lse_b2[rows, hs], groups)
                    if masked:
                        e = jnp.where(_att_mask(q0, k0, tq, tk), e, NEG)
                    p = jnp.exp2(e)
                    dv = dv + _dot(p.astype(BF16), do, TN)
                    ds = (p * (_dot(do, vv[h], NT) - _lanes(delta[rows, hs], groups))).astype(BF16)
                    dq_acc[rows, hs] += _dot(ds, kk[h], NN)
                    dk = dk + _dot(ds, q, TN)
                    out.append((dk, dv))
                return tuple(out)

            zero = jnp.zeros((tk, HP), F32)
            c = tuple((zero, zero) for _ in heads)
            first = k0 // tq
            for u in range(per):
                c = tile(first + u, c, True)
            c = lax.fori_loop(first + per, nq, lambda qi, c: tile(qi, c, False), c)
            for h in heads:
                dk_ref[pl.ds(k0, tk), h * HP:(h + 1) * HP] = (c[h][0] * scale).astype(BF16)
                dv_ref[pl.ds(k0, tk), h * HP:(h + 1) * HP] = c[h][1].astype(BF16)
            return carry

        lax.fori_loop(0, nk, k_loop, 0)
        dq_ref[...] = (dq_acc[...] * scale).astype(BF16)
        pl.when(last)(exchange.finish)

    spec = pl.BlockSpec((s, ATT_HEADS * HP), lambda b, h: (b, h))
    return pl.pallas_call(
        body, name="attn_bwd", grid=(bsz, MH // ATT_HEADS), in_specs=[spec] * 6 + [HBM_SPEC],
        out_specs=[spec] * 3 + [HBM_SPEC],
        out_shape=[jax.ShapeDtypeStruct((t, MH * HP), BF16)] * 3 + [jax.ShapeDtypeStruct(tsum.shape, tsum.dtype)],
        scratch_shapes=[pltpu.VMEM((s, ATT_HEADS * HP), F32)] * 3 + EXCHANGE_SEMS,
        compiler_params=_cparams(("arbitrary", "arbitrary")),
    )(q2, k2, v2, do2, o2, lse2, tsum)


def _perm_w_in_t(w):
    z = lambda n: jnp.zeros((n, w.shape[1]), w.dtype)
    return jnp.concatenate([w[:3072], w[3504:5552], w[3088:3344], w[3344:3472], w[3072:3088], z(48), w[3472:3504], z(32)],
                           axis=0)


def _unperm_w_in_t(g):
    return jnp.concatenate([g[:3072], g[5504:5520], g[5120:5376], g[5376:5504], g[5568:5600], g[3072:5120]], axis=0)


def _pad_wa(w):
    return jnp.pad(w, ((0, LANES - GLR), (0, 0)))


def _pad_wuq(w):
    return jnp.pad(w.reshape(MQR, MH, MQK), ((0, 0), (0, 0), (0, HP - MQK))).reshape(MQR, MH * HP)


def _unpad_wuq(g):
    return g.reshape(MQR, MH, HP)[:, :, :MQK].reshape(MQR, MH * MQK)


def _pad_wukv(w):
    w3 = w.reshape(MKVR, MH, NOPE + MV)
    kp = jnp.pad(w3[:, :, :NOPE], ((0, 0), (0, 0), (0, HP - NOPE))).reshape(MKVR, MH * HP)
    vp = jnp.pad(w3[:, :, NOPE:], ((0, 0), (0, 0), (0, HP - MV))).reshape(MKVR, MH * HP)
    return jnp.concatenate([kp, vp], axis=1)


def _unpad_wukv(g):
    kp = g[:, :MH * HP].reshape(MKVR, MH, HP)[:, :, :NOPE]
    vp = g[:, MH * HP:].reshape(MKVR, MH, HP)[:, :, :MV]
    return jnp.concatenate([kp, vp], axis=2).reshape(MKVR, MH * (NOPE + MV))


def _pad_wo(w):
    return jnp.pad(w.reshape(MH, MV, D), ((0, 0), (0, HP - MV), (0, 0))).reshape(MH * HP, D)


def _unpad_wo(g):
    return g.reshape(MH, HP, D)[:, :MV, :].reshape(MH * MV, D)


def _repeat_half(a):
    a3 = a.reshape(a.shape[0], -1, HP)
    a3 = jnp.concatenate([a3[:, :, :MQK], a3[:, :, NOPE:NOPE + ROPE // 2], a3[:, :, MQK + ROPE // 2:]], axis=2)
    return a3.reshape(a.shape)


def _pad_lanes(v, n=HP):
    return jnp.pad(v, ((0, 0), (0, n - v.shape[1])))


def _local_step(x, positions, tgt, wt, slab_b, sp, mod3):
    bsz, s, _ = x.shape
    t = bsz * s
    x2 = x.reshape(t, D)
    tgt2 = tgt.reshape(t, D)
    pos = positions.reshape(t, 1)
    fr16 = ROPE_THETA ** (-jnp.arange(0, ROPE, 2, dtype=F32) / ROPE)
    zero = lambda n: jnp.zeros((n,), F32)
    fr = jnp.concatenate([zero(NOPE), fr16, fr16, zero(HP - MQK)]).reshape(1, HP)
    sg = jnp.concatenate([zero(NOPE), -jnp.ones((ROPE // 2,), F32), jnp.ones((ROPE // 2,), F32), zero(HP - MQK)]).reshape(1, HP)

    w_in_t = _perm_w_in_t(wt["w_in"])
    wa_pad = _pad_wa(wt["gla_w_alpha"].T)
    wuq = _pad_wuq(wt["mla_w_uq"].T)
    wukv = _pad_wukv(wt["mla_w_ukv"].T)
    qn_g, kn_g = _pad_lanes(sp["mla_qn_g"]), _pad_lanes(sp["mla_kn_g"])

    tm = _tile(s, 1024)
    h = _norm_mod_fwd(x2, sp["norm1_g"], mod3, 0, 1, "norm1_fwd")
    proj = _mm(h, w_in_t, "nt", (BF16,), "proj_fwd", tn=PW // 4)
    o_gla, og, states = _gla_fwd(proj, wa_pad, sp["gla_b_alpha"], sp["gla_out_norm_g"], bsz)
    q2, k2, v2 = _mla_pre_fwd(proj, pos, fr, sg, sp["mla_q_lat_g"], sp["mla_kv_lat_g"], _repeat_half(qn_g), _repeat_half(kn_g),
                              _repeat_half(wuq), wukv)
    o2, lse2, core_row = _attn_fwd(q2, k2, v2, bsz, slab_b)
    wt = dict(wt, **_unpack_gathered(_cross_core_fill(core_row), SLAB_B))
    wo_pad = _pad_wo(wt["mla_w_o"])
    y_a = _mm(og, wt["gla_w_o"], "nn", (BF16,), "gla_out_fwd")
    y_b = _mm(o2, wo_pad, "nn", (BF16,), "mla_out_fwd")
    mix = _merge_fwd(proj, sp["b_merge"], y_a, y_b)
    mixed, x1, h2 = _mm(mix, wt["w_out"], "nn", (), "w_out_fwd", tm=tm,
                        tail=_norm_fwd_tail(x2, sp["norm2_g"], mod3, 3, 4, 2, tm))
    a, f = _mm(h2, wt["mlp_w1"], "nt", (BF16, BF16), "mlp1_fwd",
               epi=lambda acc: (acc, jnp.square(jnp.maximum(acc, 0.0))))
    dy, dff, acc_g2, acc_loss = _mm(f, wt["mlp_w2"], "nn", (), "mlp2_fwd", tm=tm, tail=_loss_tail(x1, tgt2, mod3, tm))

    gw = {}
    gw["mlp_w2"] = _mm(f, dff, "tn", (BF16,), "mlp2_dw")
    da = _mm(dff, wt["mlp_w2"], "nt", (BF16,), "mlp2_dx", extras=(a,),
             epi=lambda acc, av: (acc * (2.0 * jnp.maximum(av.astype(F32), 0.0)),))
    gw["mlp_w1"] = _mm(da, h2, "tn", (BF16,), "mlp1_dw")
    dx1, dmixed, accb2, accg2 = _mm(da, wt["mlp_w1"], "nn", (), "mlp1_dx", tm=tm,
                                    tail=_norm_bwd_tail(x1, dy, sp["norm2_g"], mod3, 4, tm, mixed=mixed, i_gate=2))

    gw["w_out"] = _mm(mix, dmixed, "tn", (BF16,), "w_out_dw")
    dmix = _mm(dmixed, wt["w_out"], "nt", (BF16,), "w_out_dx")
    dy_a, dy_b, dlogits, acc_bm = _merge_bwd(dmix, proj, sp["b_merge"], y_a, y_b)
    gw["gla_w_o"] = _mm(og, dy_a, "tn", (BF16,), "gla_out_dw")
    dog = _mm(dy_a, wt["gla_w_o"], "nt", (BF16,), "gla_out_dx")
    gw["mla_w_o"] = _unpad_wo(_mm(o2, dy_b, "tn", (BF16,), "mla_out_dw"))
    do2 = _mm(dy_b, wo_pad, "nt", (BF16,), "mla_out_dx")
    dq2, dk2, dv2, parts_b = _attn_bwd(q2, k2, v2, do2, o2, lse2, bsz, _pack_per_device(gw, SLAB_B))
    dq_g, dk_g, dv_g, dgg, dz, acc_ba, acc_go = _gla_bwd(dog, o_gla, proj, wa_pad, sp["gla_b_alpha"],
                                                         sp["gla_out_norm_g"], states, bsz)
    gw["gla_w_alpha"] = _mm(proj, dz, "tn", (F32,), "gla_alpha_dw", a_off=43, m=LANES)[:GLR].T.astype(BF16)
    dmisc_gla = _mm(dz, wa_pad, "nt", (F32,), "gla_alpha_dx")
    dcq, dckv, dmisc, gwuq, gwukv, acc_mla = _mla_pre_bwd(dq2, dk2, dv2, dmisc_gla, proj, pos, fr, sg, sp["mla_q_lat_g"],
                                                         sp["mla_kv_lat_g"], qn_g, kn_g, wuq, wukv)
    gw["mla_w_uq"] = _unpad_wuq(gwuq).T.astype(BF16)
    gw["mla_w_ukv"] = _unpad_wukv(gwukv).T.astype(BF16)
    dproj = jnp.concatenate([dq_g, dk_g, dv_g, dgg, dlogits, dcq, dckv, dmisc], axis=1)
    gw["w_in"] = _unperm_w_in_t(_mm(dproj, h, "tn", (BF16,), "proj_dw", tm=PW // 4))
    parts_a, grad_x, accb1, accg1 = _mm(dproj, w_in_t, "nn", (), "proj_dx", tm=tm, tk=PW // 4,
                                        cargo=_sum_over_cores(_pack_per_device(gw, SLAB_A), "a"),
                                        tail=_norm_bwd_tail(x2, dx1, sp["norm1_g"], mod3, 1, tm))

    dmod = jnp.stack([accb1[:, 0], accb1[:, 1], accb2[:, 2], accb2[:, 0], accb2[:, 1], acc_g2[:, 0]], axis=1)

    rows = {
        "dmod": dmod.reshape(bsz * 6, D),
        "norm1_g": accg1[0:1], "norm2_g": accg2[0:1],
        "b_merge": acc_bm[0:1].reshape(2, D),
        "gla_b_alpha": _pad_lanes(acc_ba[0:1], D),
        "gla_out_norm_g": _pad_lanes(acc_go.reshape(GH, 8, GDV)[:, 0, :], D),
        "mla_q_lat_g": _pad_lanes(acc_mla[0:1], D), "mla_kv_lat_g": _pad_lanes(acc_mla[1:2], D),
        "mla_qn_g": _pad_lanes(acc_mla[2:3], D), "mla_kn_g": _pad_lanes(acc_mla[3:4], D),
        "loss": acc_loss[0:1],
    }
    return grad_x.reshape(bsz, s, D), parts_a, parts_b, rows


HBM_SPEC = pl.BlockSpec(memory_space=pltpu.HBM)


def _all_gather(ps, name):
    n = len(ps)

    def body(*refs):
        p_refs, out_refs, (send_sems, recv_sems, local_sems) = refs[:n], refs[n:2 * n], refs[2 * n:]
        x, y, c = lax.axis_index("x"), lax.axis_index("y"), lax.axis_index("c")
        me, sibling = (x, y, c), (x, y, 1 - c)
        chips = [(1 - x, y), (x, 1 - y), (1 - x, 1 - y)]

        def copy(a, k, block, to, own=False):
            slot = out_refs[a].at[4 * block[0] + 2 * block[1] + block[2]]
            return pltpu.make_async_remote_copy(
                src_ref=p_refs[a] if own else slot, dst_ref=slot, send_sem=send_sems.at[7 * a + k],
                recv_sem=recv_sems.at[7 * a + k], device_id=to, device_id_type=MESH)

        mine = [pltpu.make_async_copy(p_refs[a], out_refs[a].at[4 * x + 2 * y + c], local_sems.at[a]) for a in range(n)]
        first = [copy(a, 0, me, sibling, own=True) for a in range(n)]
        first += [copy(a, 1 + j, me, (*chip, c), own=True) for a in range(n) for j, chip in enumerate(chips)]
        for cp in mine + first:
            cp.start()
        passed = []
        for j, chip in enumerate(chips):
            for a in range(n):
                copy(a, 1 + j, (*chip, c), me).wait_recv()
                passed.append(copy(a, 4 + j, (*chip, c), sibling))
                passed[-1].start()
        for a in range(n):
            copy(a, 0, sibling, me).wait_recv()
            for j, chip in enumerate(chips):
                copy(a, 4 + j, (*chip, 1 - c), me).wait_recv()
        for cp in first + passed:
            cp.wait_send()
        for cp in mine:
            cp.wait()

    return pl.pallas_call(
        body, name=name, out_shape=[jax.ShapeDtypeStruct((N_DEV,) + p.shape, p.dtype) for p in ps],
        in_specs=[HBM_SPEC] * n, out_specs=[HBM_SPEC] * n,
        scratch_shapes=[pltpu.SemaphoreType.DMA((7 * n,)), pltpu.SemaphoreType.DMA((7 * n,)), pltpu.SemaphoreType.DMA((n,))],
    )(*ps)


def _sibling_exchange(g, name):
    def body(g_ref, out_ref, send_sem, recv_sem):
        x, y, c = lax.axis_index("x"), lax.axis_index("y"), lax.axis_index("c")
        cp = pltpu.make_async_remote_copy(src_ref=g_ref, dst_ref=out_ref, send_sem=send_sem, recv_sem=recv_sem,
                                          device_id=(x, y, 1 - c), device_id_type=MESH)
        cp.start()
        cp.wait()

    return pl.pallas_call(
        body, name=name, out_shape=jax.ShapeDtypeStruct(g.shape, g.dtype),
        in_specs=[HBM_SPEC], out_specs=HBM_SPEC,
        scratch_shapes=[pltpu.SemaphoreType.DMA(()), pltpu.SemaphoreType.DMA(())],
    )(g)


class _Exchange:
    def __init__(self, local, sends, arrivals):
        self.local, self.sends, self.arrivals = local, sends, arrivals

    def start(self):
        self.local.start()
        for cp in self.sends:
            cp.start()

    def finish(self):
        for cp in self.arrivals:
            cp.wait_recv()
        for cp in self.sends:
            cp.wait_send()
        self.local.wait()


EXCHANGE_SEMS = [pltpu.SemaphoreType.DMA((N_DEV,)), pltpu.SemaphoreType.DMA((N_DEV,)), pltpu.SemaphoreType.DMA(())]


def _all_to_all_copies(t_ref, out_ref, send_sems, recv_sems, local_sem):
    x, y, c = lax.axis_index("x"), lax.axis_index("y"), lax.axis_index("c")
    me = 4 * x + 2 * y + c

    def copy(k, src, dst):
        px, py, pc = (1 - x if k & 4 else x), (1 - y if k & 2 else y), (1 - c if k & 1 else c)
        peer = 4 * px + 2 * py + pc
        return pltpu.make_async_remote_copy(src_ref=t_ref.at[peer if src is None else src],
                                            dst_ref=out_ref.at[peer if dst is None else dst], send_sem=send_sems.at[k],
                                            recv_sem=recv_sems.at[k], device_id=(px, py, pc), device_id_type=MESH)

    return _Exchange(pltpu.make_async_copy(t_ref.at[me], out_ref.at[me], local_sem),
                     [copy(k, None, me) for k in range(1, N_DEV)], [copy(k, me, None) for k in range(1, N_DEV)])


def _chip_exchange_copies(t_ref, out_ref, send_sems, recv_sems, local_sem):
    x, y, c = lax.axis_index("x"), lax.axis_index("y"), lax.axis_index("c")
    my_chip = 2 * x + y
    chips = [(1 - x, y), (x, 1 - y), (1 - x, 1 - y)]

    def copy(j, src, dst, px, py):
        return pltpu.make_async_remote_copy(src_ref=t_ref.at[src], dst_ref=out_ref.at[dst], send_sem=send_sems.at[j],
                                            recv_sem=recv_sems.at[j], device_id=(px, py, c), device_id_type=MESH)

    return _Exchange(pltpu.make_async_copy(t_ref.at[my_chip], out_ref.at[my_chip], local_sem),
                     [copy(j, 2 * px + py, my_chip, px, py) for j, (px, py) in enumerate(chips)],
                     [copy(j, my_chip, 2 * px + py, px, py) for j, (px, py) in enumerate(chips)])


def _core_row_gather_copies(p_ref, out_ref, send_sems, recv_sems, local_sem):
    x, y, c = lax.axis_index("x"), lax.axis_index("y"), lax.axis_index("c")
    peers = [(x, y, 1 - c), (1 - x, y, c), (x, 1 - y, c), (1 - x, 1 - y, c)]

    def slot(px, py, pc):
        return out_ref.at[4 * px + 2 * py + pc]

    def copy(j, block, to):
        return pltpu.make_async_remote_copy(src_ref=p_ref, dst_ref=slot(*block), send_sem=send_sems.at[j],
                                            recv_sem=recv_sems.at[j], device_id=to, device_id_type=MESH)

    return _Exchange(pltpu.make_async_copy(p_ref, slot(x, y, c), local_sem),
                     [copy(j, (x, y, c), peer) for j, peer in enumerate(peers)],
                     [copy(j, peer, peer) for j, peer in enumerate(peers)])


def _cross_core_fill(gathered):
    def body(g_ref, out_ref, send_sems, recv_sems):
        x, y, c = lax.axis_index("x"), lax.axis_index("y"), lax.axis_index("c")
        chips = [(1 - x, y), (x, 1 - y), (1 - x, 1 - y)]

        def copy(j, pc):
            px, py = chips[j]
            slot = 4 * px + 2 * py + pc
            return pltpu.make_async_remote_copy(src_ref=g_ref.at[slot], dst_ref=out_ref.at[slot], send_sem=send_sems.at[j],
                                                recv_sem=recv_sems.at[j], device_id=(x, y, 1 - c), device_id_type=MESH)

        sends = [copy(j, c) for j in range(3)]
        for cp in sends:
            cp.start()
        for j in range(3):
            copy(j, 1 - c).wait_recv()
        for cp in sends:
            cp.wait_send()

    return pl.pallas_call(
        body, name="weights_cross_core_fill", out_shape=jax.ShapeDtypeStruct(gathered.shape, gathered.dtype),
        in_specs=[HBM_SPEC], out_specs=HBM_SPEC, input_output_aliases={0: 0},
        scratch_shapes=[pltpu.SemaphoreType.DMA((3,)), pltpu.SemaphoreType.DMA((3,))],
    )(gathered)


def _slab_block(r):
    return max(b for b in range(16, SLAB_BLOCK_MAX + 1, 16) if r % b == 0)


def _pair_sum(a, b, name):
    n, r, cdim = a.shape
    rb = _slab_block(r)
    blk = pl.BlockSpec((1, rb, cdim), lambda j, i: (j, i, 0))

    def body(a_ref, b_ref, o_ref):
        o_ref[...] = (a_ref[...].astype(F32) + b_ref[...].astype(F32)).astype(BF16)

    return pl.pallas_call(
        body, name=name, grid=(n, r // rb), in_specs=[blk, blk], out_specs=blk,
        out_shape=jax.ShapeDtypeStruct(a.shape, BF16), compiler_params=_cparams(("arbitrary", "arbitrary")),
    )(a, b)


def _adamw_math(w, g, m, v):
    m = ADAM_B1 * m + (1.0 - ADAM_B1) * g
    v = ADAM_B2 * v + (1.0 - ADAM_B2) * jnp.square(g)
    m_hat = m / (1.0 - ADAM_B1 ** ADAM_STEP)
    v_hat = v / (1.0 - ADAM_B2 ** ADAM_STEP)
    delta = -ADAM_LR * (m_hat / (jnp.sqrt(v_hat) + ADAM_EPS) + ADAM_WD * w)
    return delta, m, v


def _slab_sum(parts, name):
    n, r, cdim = parts.shape
    rb = _slab_block(r)
    blk = pl.BlockSpec((rb, cdim), lambda i: (i, 0))

    def body(p_ref, g_out):
        g = p_ref[0].astype(F32)
        for j in range(1, n):
            g = g + p_ref[j].astype(F32)
        g_out[...] = g

    return pl.pallas_call(
        body, name=name, grid=(r // rb,),
        in_specs=[pl.BlockSpec((n, rb, cdim), lambda i: (0, i, 0))], out_specs=blk,
        out_shape=jax.ShapeDtypeStruct((r, cdim), F32), compiler_params=_cparams(("arbitrary",)),
    )(parts)


def _adamw(g, w, m, v, name):
    r, cdim = w.shape
    rb = _tile(r, 256)
    blk = pl.BlockSpec((rb, cdim), lambda i: (i, 0))

    def body(g_ref, w_ref, m_ref, v_ref, d_out, m_out, v_out):
        d_out[...], m_out[...], v_out[...] = _adamw_math(w_ref[...], g_ref[...], m_ref[...], v_ref[...])

    return pl.pallas_call(
        body, name=name, grid=(r // rb,), in_specs=[blk] * 4, out_specs=[blk] * 3,
        out_shape=[jax.ShapeDtypeStruct((r, cdim), F32)] * 3, compiler_params=_cparams(("arbitrary",)),
    )(g, w, m, v)


def _adamw_small(parts, w, m, v):
    def body(p_ref, w_ref, m_ref, v_ref, g_out, d_out, m_out, v_out, loss_out):
        def total(srcs):
            acc = None
            for r in srcs:
                for j in range(N_DEV):
                    term = p_ref[j, r:r + 1, :]
                    acc = term if acc is None else acc + term
            return acc

        for prow, srcs in enumerate(SMALL_SOURCES):
            one = slice(prow, prow + 1)
            g = total(srcs)
            g_out[one, :] = g
            d_out[one, :], m_out[one, :], v_out[one, :] = _adamw_math(w_ref[one, :], g, m_ref[one, :], v_ref[one, :])
        loss_out[...] = jnp.broadcast_to(jnp.sum(total(LOSS_SOURCE), axis=1, keepdims=True), (8, LANES))

    full = lambda shp: pl.BlockSpec(shp, lambda i: (0,) * len(shp))
    return pl.pallas_call(
        body, name="adamw_small", grid=(1,),
        in_specs=[full((N_DEV, SMALL_ROWS, D)), full((16, D)), full((16, D)), full((16, D))],
        out_specs=[full((16, D))] * 4 + [full((8, LANES))],
        out_shape=[jax.ShapeDtypeStruct((16, D), F32)] * 4 + [jax.ShapeDtypeStruct((8, LANES), F32)],
        compiler_params=_cparams(("arbitrary",)),
    )(parts, w, m, v)


def _to_slab(shard, form):
    if form == "N":
        return shard
    return shard.T if form == "T" else shard.T.reshape(-1, D)


def _from_slab(block, form, shard_shape):
    if form == "N":
        return block
    return block.T if form == "T" else block.reshape(shard_shape[1], shard_shape[0]).T


def _gathered_full(g, form, shard_shape):
    if form == "TR":
        return g.reshape(N_DEV * shard_shape[1], shard_shape[0])
    return g.reshape(N_DEV * g.shape[1], D)


def _pack_slab(shards, layout):
    return jnp.concatenate([jnp.zeros((r, D), BF16) if n is None else _to_slab(shards[n], form).astype(BF16)
                            for n, r, form in layout], axis=0)


def _unpack_gathered(gathered, layout):
    out, off = {}, 0
    for n, r, form in layout:
        if n is not None:
            out[n] = _gathered_full(gathered[:, off:off + r], form, SHARD_SHAPES[n])
        off += r
    return out


def _pack_per_device(gw, layout):
    return jnp.concatenate([jnp.zeros((N_DEV, r, D), BF16) if n is None else gw[n].reshape(N_DEV, r, D)
                            for n, r, _ in layout], axis=1)


def _adamw_shards(gslab, layout, wts, mom, var):
    out, off = {}, 0
    for n, r, form in layout:
        if n is not None:
            g = _from_slab(gslab[off:off + r], form, SHARD_SHAPES[n])
            out[n] = (g,) + tuple(_adamw(g, wts[n], mom[n], var[n], "adamw_" + n))
        off += r
    return out


def _sum_over_cores(per_dev, tag):
    my_c = lax.axis_index("c")
    pairs = per_dev.reshape(4, 2, per_dev.shape[1], D)
    keep = lax.dynamic_index_in_dim(pairs, my_c, axis=1, keepdims=False)
    give = lax.dynamic_index_in_dim(pairs, 1 - my_c, axis=1, keepdims=False)
    return _pair_sum(keep, _sibling_exchange(give, "rs_sibling_exchange_" + tag), "rs_pair_sum_" + tag)


def _small_pack(vals):
    rows = []
    for n in SMALL:
        v = vals[n].reshape(-1)
        k = -(-v.shape[0] // D)
        rows.append(jnp.pad(v, (0, k * D - v.shape[0])).reshape(k, D))
    return jnp.concatenate(rows, axis=0)


def _small_unpack(packed, shapes):
    out = {}
    for n in SMALL:
        k = shapes[n][-1]
        r0 = SMALL_ROW[n]
        out[n] = packed[r0:r0 + -(-k // D)].reshape(-1)[:k].reshape(shapes[n])
    return out


def kernel(x, c, positions, w_ada, b_ada, norm1_g, w_in, b_merge, gla_w_alpha, gla_b_alpha, gla_out_norm_g, gla_w_o, mla_q_lat_g, mla_w_uq, mla_kv_lat_g, mla_w_ukv, mla_qn_g, mla_kn_g, mla_w_o, w_out, norm2_g, mlp_w1, mlp_w2, loss_target, m_w_ada, m_b_ada, m_norm1_g, m_w_in, m_b_merge, m_gla_w_alpha, m_gla_b_alpha, m_gla_out_norm_g, m_gla_w_o, m_mla_q_lat_g, m_mla_w_uq, m_mla_kv_lat_g, m_mla_w_ukv, m_mla_qn_g, m_mla_kn_g, m_mla_w_o, m_w_out, m_norm2_g, m_mlp_w1, m_mlp_w2, v_w_ada, v_b_ada, v_norm1_g, v_w_in, v_b_merge, v_gla_w_alpha, v_gla_b_alpha, v_gla_out_norm_g, v_gla_w_o, v_mla_q_lat_g, v_mla_w_uq, v_mla_kv_lat_g, v_mla_w_ukv, v_mla_qn_g, v_mla_kn_g, v_mla_w_o, v_w_out, v_norm2_g, v_mlp_w1, v_mlp_w2):
    args = dict(locals())
    wts = {n: args[n][0] for n in WEIGHTS}
    mom = {n: args["m_" + n][0] for n in WEIGHTS}
    var = {n: args["v_" + n][0] for n in WEIGHTS}
    my_c = lax.axis_index("c")
    my_dev = 4 * lax.axis_index("x") + 2 * lax.axis_index("y") + my_c
    bsz = x.shape[0]
    sp = {n: wts[n].reshape(1, -1) for n in SMALL}

    gathered_a, c_gathered = _all_gather([_pack_slab(wts, SLAB_A), jnp.pad(c, ((0, 8 - bsz), (0, 0)))], "weights_all_gather")
    wt = _unpack_gathered(gathered_a, SLAB_A)

    c_all = c_gathered[:, :bsz].reshape(N_DEV * bsz, D)
    bias = lax.dynamic_slice_in_dim(sp["b_ada"], my_dev * ADA_COLS, ADA_COLS, axis=1)
    mod_cols = _mm(c_all, wts["w_ada"], "nn", (F32,), "ada_fwd", pro=_silu, epi=lambda acc, b: (acc + b,),
                   extras=(jnp.broadcast_to(bias, (N_DEV * bsz, ADA_COLS)),))
    mod_all, = _all_gather([mod_cols], "mod_all_gather")
    mod_mine = lax.dynamic_slice_in_dim(mod_all, my_dev * bsz, bsz, axis=1)
    mod3 = jnp.transpose(mod_mine, (1, 0, 2)).reshape(bsz, 6, D)

    grad_x, parts_a, parts_b, rows = _local_step(x, positions, loss_target, wt, _pack_slab(wts, SLAB_B), sp, mod3)

    big = dict(_adamw_shards(_slab_sum(parts_a, "rs_slab_sum_a"), SLAB_A, wts, mom, var),
               **_adamw_shards(_slab_sum(parts_b, "rs_slab_sum_b"), SLAB_B, wts, mom, var))

    order = ["dmod", "norm1_g", "norm2_g", "b_merge", "gla_b_alpha", "gla_out_norm_g", "mla_q_lat_g", "mla_kv_lat_g",
             "mla_qn_g", "mla_kn_g", "loss"]
    part_rows = jnp.concatenate([rows[n] for n in order], axis=0)
    part_rows = jnp.pad(part_rows, ((0, SMALL_ROWS - part_rows.shape[0]), (0, 0)))
    all_rows, = _all_gather([part_rows], "partials_all_gather")

    dmod_all = all_rows[:, :6 * bsz].reshape(N_DEV * bsz, 6 * D)
    dmod_cols = lax.dynamic_slice_in_dim(dmod_all, my_dev * ADA_COLS, ADA_COLS, axis=1)
    g_ada = _mm(c_all, dmod_cols, "tn", (F32,), "ada_dw", pro=_silu)
    big["w_ada"] = (g_ada,) + tuple(_adamw(g_ada, wts["w_ada"], mom["w_ada"], var["w_ada"], "adamw_w_ada"))

    small = _adamw_small(all_rows, _small_pack({n: wts[n] for n in SMALL}), _small_pack({n: mom[n] for n in SMALL}),
                         _small_pack({n: var[n] for n in SMALL}))
    loss = small[4][0, 0]
    small_shapes = {n: wts[n].shape for n in SMALL}
    small = [_small_unpack(o, small_shapes) for o in small[:4]]

    outs = [loss, grad_x]
    for k in range(4):
        for n in WEIGHTS:
            val = big[n][k] if n in BIG else small[k][n]
            outs.append(val.reshape((1,) + tuple(wts[n].shape)))
    return tuple(outs)
```

```python
import jax
import jax.numpy as jnp
from jax import lax
from jax.experimental import pallas as pl
from jax.experimental.pallas import tpu as pltpu

F32 = jnp.float32
BF16 = jnp.bfloat16
MESH = pl.DeviceIdType.MESH

D = 1024
EPS = 1e-6
CHUNK = 64
GH, GDK, GDV, GLR, GTAU = 4, 128, 256, 16, 16.0
MH, MQR, MKVR, NOPE, ROPE, MV = 16, 256, 128, 64, 32, 64
MQK = NOPE + ROPE
HP = 128
FF = 4 * D
ROPE_THETA = 10000.0
IN_WIDTH = 5552
PW = 5632
N_DEV = 8
LANES = 128
SLAB_BLOCK_MAX = 400
ADA_COLS = 6 * D // N_DEV
SMALL_ROWS = 32
SMALL_SOURCES = tuple([(r, 6 + r) for r in range(6)] + [(12,), (13,), (14,), (15,), (16,), (17, 18, 19, 20),
                                                         (21,), (22,), (23,), (24,)])
LOSS_SOURCE = (25,)
VMEM_LIMIT = 56 * 1024 * 1024

ADAM_LR, ADAM_B1, ADAM_B2, ADAM_EPS, ADAM_WD, ADAM_STEP = 0.001, 0.9, 0.999, 1e-08, 0.01, 10

SLAB_A = (("w_in", 694, "T"), ("gla_w_alpha", 1, "TR"), (None, 9, None), ("mla_w_uq", 48, "TR"), ("mla_w_ukv", 32, "TR"))
SLAB_B = (("mlp_w1", 512, "T"), ("gla_w_o", 128, "N"), ("mla_w_o", 128, "N"), ("w_out", 128, "N"), ("mlp_w2", 512, "N"))
BIG = ("w_ada",) + tuple(n for n, _, _ in SLAB_A + SLAB_B if n is not None)
SHARD_SHAPES = {"w_ada": (D, 6 * D // N_DEV), "w_in": (D, IN_WIDTH // N_DEV), "gla_w_alpha": (GLR, GH * GDK // N_DEV),
                "gla_w_o": (GH * GDV // N_DEV, D), "mla_w_uq": (MQR, MH * MQK // N_DEV),
                "mla_w_ukv": (MKVR, MH * (NOPE + MV) // N_DEV), "mla_w_o": (MH * MV // N_DEV, D), "w_out": (D // N_DEV, D),
                "mlp_w1": (D, FF // N_DEV), "mlp_w2": (FF // N_DEV, D)}
SMALL = ("b_ada", "norm1_g", "norm2_g", "b_merge", "gla_b_alpha", "gla_out_norm_g", "mla_q_lat_g", "mla_kv_lat_g",
         "mla_qn_g", "mla_kn_g")
SMALL_ROW = {"b_ada": 0, "norm1_g": 6, "norm2_g": 7, "b_merge": 8, "gla_b_alpha": 10, "gla_out_norm_g": 11,
             "mla_q_lat_g": 12, "mla_kv_lat_g": 13, "mla_qn_g": 14, "mla_kn_g": 15}
WEIGHTS = ("w_ada", "b_ada", "norm1_g", "w_in", "b_merge", "gla_w_alpha", "gla_b_alpha", "gla_out_norm_g", "gla_w_o",
           "mla_q_lat_g", "mla_w_uq", "mla_kv_lat_g", "mla_w_ukv", "mla_qn_g", "mla_kn_g", "mla_w_o", "w_out",
           "norm2_g", "mlp_w1", "mlp_w2")


def _cparams(sem=None):
    return pltpu.CompilerParams(dimension_semantics=sem, vmem_limit_bytes=VMEM_LIMIT)


def _tile(n, pref):
    for t in (2048, PW // 4, 1024, 512, 256, 128):
        if t <= pref and n % t == 0:
            return t
    return n


def _dot(a, b, dims, precision=None):
    return lax.dot_general(a, b, (dims, ((), ())), preferred_element_type=F32, precision=precision)


NN = ((1,), (0,))
NT = ((1,), (1,))
TN = ((0,), (0,))


def _sigmoid(x):
    return 1.0 / (1.0 + jnp.exp(-x))


def _silu(x):
    return x * _sigmoid(x)


def _mm(a, b, mode, out_dtypes, name, *, pro=None, pro_b=None, epi=None, extras=(), a_off=0, m=None, tm=2048, tn=1024,
        tk=1024, cargo=None, tail=None):
    if mode == "tn":
        kc, n = b.shape
        m = a.shape[1] if m is None else m
    elif mode == "nn":
        m, kc = a.shape
        n = b.shape[1]
    else:
        m, kc = a.shape
        n = b.shape[0]
    tm, tn, tk = _tile(m, tm), _tile(n, tn), _tile(kc, tk)
    nk = kc // tk
    dims = {"nn": NN, "nt": NT, "tn": TN}[mode]
    if mode == "tn":
        a_spec = pl.BlockSpec((tk, tm), lambda i, j, k: (k, i + a_off))
    else:
        a_spec = pl.BlockSpec((tm, tk), lambda i, j, k: (i + a_off, k))
    if mode == "nt":
        b_spec = pl.BlockSpec((tn, tk), lambda i, j, k: (j, k))
    else:
        b_spec = pl.BlockSpec((tk, tn), lambda i, j, k: (k, j))
    o_spec = pl.BlockSpec((tm, tn), lambda i, j, k: (i, j))
    n_ex, n_out = len(extras), len(out_dtypes)
    grid = (m // tm, n // tn, nk)
    has_cargo = cargo is not None
    t_ins, t_outs = (tail["ins"], tail["outs"]) if tail else ([], [])
    assert not tail or grid[1] == 1

    def body(a_ref, b_ref, *rest):
        rest = list(rest)
        take = lambda count: [rest.pop(0) for _ in range(count)]
        ex, cargo_ref, tail_in = take(n_ex), take(has_cargo), take(len(t_ins))
        outs, parts_ref, tail_out = take(n_out), take(has_cargo), take(len(t_outs))
        acc = rest.pop(0)
        steps = [pl.program_id(axis) for axis in range(3)]
        if has_cargo:
            exchange = _chip_exchange_copies(cargo_ref[0], parts_ref[0], *rest)
            first = (steps[0] == 0) & (steps[1] == 0) & (steps[2] == 0)
            last = (steps[0] == grid[0] - 1) & (steps[1] == grid[1] - 1) & (steps[2] == grid[2] - 1)
            pl.when(first)(exchange.start)
        k = steps[2]

        @pl.when(k == 0)
        def _():
            acc[...] = jnp.zeros_like(acc)

        av = a_ref[...]
        if pro is not None:
            av = pro(av)
        bv = b_ref[...]
        if pro_b is not None:
            bv = pro_b(bv)
        acc[...] += _dot(av.astype(BF16), bv.astype(BF16), dims)

        @pl.when(k == nk - 1)
        def _():
            if tail:
                tail["fn"](acc[...], steps[0], *tail_in, *tail_out)
            res = (acc[...],) if epi is None else epi(acc[...], *[e[...] for e in ex])
            for o_ref, r in zip(outs, res):
                o_ref[...] = r.astype(o_ref.dtype)

        if has_cargo:
            pl.when(last)(exchange.finish)

    cargo_in = [cargo] if has_cargo else []
    cargo_spec = [HBM_SPEC] * len(cargo_in)
    sequential = has_cargo or bool(tail)
    out = pl.pallas_call(
        body, name=name, grid=grid,
        in_specs=[a_spec, b_spec] + [o_spec] * n_ex + cargo_spec + (tail["in_specs"] if tail else []),
        out_specs=[o_spec] * n_out + cargo_spec + (tail["out_specs"] if tail else []),
        out_shape=[jax.ShapeDtypeStruct((m, n), dt) for dt in out_dtypes]
        + [jax.ShapeDtypeStruct(c.shape, c.dtype) for c in cargo_in] + t_outs,
        scratch_shapes=[pltpu.VMEM((tm, tn), F32)] + (EXCHANGE_SEMS if has_cargo else []),
        compiler_params=_cparams(("arbitrary",) * 3 if sequential else ("parallel", "parallel", "arbitrary")),
    )(a, b, *extras, *cargo_in, *t_ins)
    return out[0] if len(out) == 1 else out


def _rows(s):
    return _tile(s, 512)


def _mod_spec():
    return pl.BlockSpec((1, 6, D), lambda b, i: (b, 0, 0))


def _tok_spec(tr, nb, width=D, col=0):
    return pl.BlockSpec((tr, width), lambda b, i: (b * nb + i, col))


def _modulated_norm(xv, gv, mod_ref, i_shift, i_scale):
    r = lax.rsqrt(jnp.mean(xv * xv, axis=1, keepdims=True) + EPS)
    return ((xv * r) * gv) * (1.0 + mod_ref[0, i_scale:i_scale + 1, :]) + mod_ref[0, i_shift:i_shift + 1, :]


def _norm_mod_fwd(x, g, mod3, i_shift, i_scale, name):
    bsz, _, _ = mod3.shape
    t = x.shape[0]
    s = t // bsz
    tr = _rows(s)
    nb = s // tr

    def body(x_ref, g_ref, mod_ref, h_ref):
        h_ref[...] = _modulated_norm(x_ref[...], g_ref[...], mod_ref, i_shift, i_scale).astype(BF16)

    tok = _tok_spec(tr, nb)
    return pl.pallas_call(
        body, name=name, grid=(bsz, nb), in_specs=[tok, pl.BlockSpec((1, D), lambda b, i: (0, 0)), _mod_spec()],
        out_specs=tok, out_shape=jax.ShapeDtypeStruct((t, D), BF16), compiler_params=_cparams(("arbitrary", "arbitrary")),
    )(x, g, mod3)


def _norm_fwd_tail(x, g, mod3, i_shift, i_scale, i_gate, tm):
    bsz, t = mod3.shape[0], x.shape[0]
    per_b = t // bsz // tm

    def fn(mixedv, i, x_ref, g_ref, mod_ref, mixed_ref, x1_ref, h_ref):
        mixed_ref[...] = mixedv
        xv = x_ref[...] + mod_ref[0, i_gate:i_gate + 1, :] * mixedv
        x1_ref[...] = xv
        h_ref[...] = _modulated_norm(xv, g_ref[...], mod_ref, i_shift, i_scale).astype(BF16)

    tok = pl.BlockSpec((tm, D), lambda i, j, k: (i, 0))
    return dict(
        fn=fn, ins=[x, g, mod3], in_specs=[tok, pl.BlockSpec((1, D), lambda i, j, k: (0, 0)), _tail_mod_spec(per_b)],
        outs=[jax.ShapeDtypeStruct((t, D), F32), jax.ShapeDtypeStruct((t, D), F32), jax.ShapeDtypeStruct((t, D), BF16)],
        out_specs=[tok, tok, tok])


def _norm_bwd_rows(xv, dhv, dresv, gv, mod_ref, i_scale, accb, accg):
    r = lax.rsqrt(jnp.mean(xv * xv, axis=1, keepdims=True) + EPS)
    xn = xv * r
    accb[0, 0:1, :] += jnp.sum(dhv, axis=0, keepdims=True)
    accb[0, 1:2, :] += jnp.sum(dhv * (xn * gv), axis=0, keepdims=True)
    tt = dhv * (1.0 + mod_ref[0, i_scale:i_scale + 1, :])
    accg[0:1, :] += jnp.sum(tt * xn, axis=0, keepdims=True)
    dxn = tt * gv
    return dresv + r * (dxn - xn * jnp.mean(dxn * xn, axis=1, keepdims=True))


def _norm_bwd_tail(x, dres, g, mod3, i_scale, tm, mixed=None, i_gate=None):
    bsz, t = mod3.shape[0], x.shape[0]
    per_b = t // bsz // tm
    has_res = mixed is not None

    def fn(dhv, i, *refs):
        if has_res:
            x_ref, dres_ref, mx_ref, g_ref, mod_ref, dx_ref, dmx_ref, accb, accg = refs
        else:
            x_ref, dres_ref, g_ref, mod_ref, dx_ref, accb, accg = refs

        @pl.when(i % per_b == 0)
        def _():
            accb[...] = jnp.zeros_like(accb)

        @pl.when(i == 0)
        def _():
            accg[...] = jnp.zeros_like(accg)

        dx = _norm_bwd_rows(x_ref[...], dhv, dres_ref[...], g_ref[...], mod_ref, i_scale, accb, accg)
        dx_ref[...] = dx
        if has_res:
            accb[0, 2:3, :] += jnp.sum(dx * mx_ref[...], axis=0, keepdims=True)
            dmx_ref[...] = (dx * mod_ref[0, i_gate:i_gate + 1, :]).astype(BF16)

    tok = pl.BlockSpec((tm, D), lambda i, j, k: (i, 0))
    res = [mixed] if has_res else []
    return dict(
        fn=fn, ins=[x, dres] + res + [g, mod3],
        in_specs=[tok, tok] + [tok] * len(res) + [pl.BlockSpec((1, D), lambda i, j, k: (0, 0)), _tail_mod_spec(per_b)],
        outs=[jax.ShapeDtypeStruct((t, D), F32)] + [jax.ShapeDtypeStruct((t, D), BF16)] * len(res)
        + [jax.ShapeDtypeStruct((bsz, 8, D), F32), jax.ShapeDtypeStruct((8, D), F32)],
        out_specs=[tok] * (1 + len(res)) + [_tail_batch_spec(per_b), pl.BlockSpec((8, D), lambda i, j, k: (0, 0))])


def _tail_mod_spec(per_b):
    return pl.BlockSpec((1, 6, D), lambda i, j, k: (i // per_b, 0, 0))


def _tail_batch_spec(per_b):
    return pl.BlockSpec((1, 8, D), lambda i, j, k: (i // per_b, 0, 0))


def _loss_tail(x1, tgt, mod3, tm):
    bsz, t = mod3.shape[0], x1.shape[0]
    per_b = t // bsz // tm

    def fn(ffv, i, x1_ref, tg_ref, mod_ref, dy_ref, dff_ref, accb, accl):
        @pl.when(i % per_b == 0)
        def _():
            accb[...] = jnp.zeros_like(accb)

        @pl.when(i == 0)
        def _():
            accl[...] = jnp.zeros_like(accl)

        gate = mod_ref[0, 5:6, :]
        err = x1_ref[...] + gate * ffv - tg_ref[...]
        accl[0:1, :] += jnp.sum(err * err, axis=0, keepdims=True) * (0.5 / D)
        dy = err * (1.0 / D)
        dy_ref[...] = dy
        dff_ref[...] = (dy * gate).astype(BF16)
        accb[0, 0:1, :] += jnp.sum(dy * ffv, axis=0, keepdims=True)

    tok = pl.BlockSpec((tm, D), lambda i, j, k: (i, 0))
    return dict(
        fn=fn, ins=[x1, tgt, mod3], in_specs=[tok, tok, _tail_mod_spec(per_b)],
        outs=[jax.ShapeDtypeStruct((t, D), F32), jax.ShapeDtypeStruct((t, D), BF16),
              jax.ShapeDtypeStruct((bsz, 8, D), F32), jax.ShapeDtypeStruct((8, D), F32)],
        out_specs=[tok, tok, _tail_batch_spec(per_b), pl.BlockSpec((8, D), lambda i, j, k: (0, 0))])


def _merge_fwd(proj, b_merge, y_a, y_b):
    t = proj.shape[0]
    tr = _tile(t, 512)

    def body(la_ref, lb_ref, bm_ref, ya_ref, yb_ref, mix_ref):
        ga = _sigmoid(la_ref[...] + bm_ref[:, 0:D])
        gb = _sigmoid(lb_ref[...] + bm_ref[:, D:2 * D])
        mix_ref[...] = (ga * ya_ref[...].astype(F32) + gb * yb_ref[...].astype(F32)).astype(BF16)

    tok = pl.BlockSpec((tr, D), lambda i: (i, 0))
    return pl.pallas_call(
        body, name="merge_fwd", grid=(t // tr,),
        in_specs=[pl.BlockSpec((tr, D), lambda i: (i, 3)), pl.BlockSpec((tr, D), lambda i: (i, 4)),
                  pl.BlockSpec((1, 2 * D), lambda i: (0, 0)), tok, tok],
        out_specs=tok, out_shape=jax.ShapeDtypeStruct((t, D), BF16),
        compiler_params=_cparams(("arbitrary",)),
    )(proj, proj, b_merge, y_a, y_b)


def _merge_bwd_tail(proj, b_merge, y_a, y_b, tm):
    t = proj.shape[0]

    def fn(dm, i, la_ref, lb_ref, bm_ref, ya_ref, yb_ref, dya_ref, dyb_ref, dl_ref, acc):
        @pl.when(i == 0)
        def _():
            acc[...] = jnp.zeros_like(acc)

        ga = _sigmoid(la_ref[...] + bm_ref[:, 0:D])
        gb = _sigmoid(lb_ref[...] + bm_ref[:, D:2 * D])
        dya_ref[...] = (dm * ga).astype(BF16)
        dyb_ref[...] = (dm * gb).astype(BF16)
        dla = dm * ya_ref[...].astype(F32) * ga * (1.0 - ga)
        dlb = dm * yb_ref[...].astype(F32) * gb * (1.0 - gb)
        dl_ref[:, 0:D] = dla.astype(BF16)
        dl_ref[:, D:2 * D] = dlb.astype(BF16)
        acc[0:1, 0:D] += jnp.sum(dla, axis=0, keepdims=True)
        acc[0:1, D:2 * D] += jnp.sum(dlb, axis=0, keepdims=True)

    tok = pl.BlockSpec((tm, D), lambda i, j, k: (i, 0))
    return dict(
        fn=fn, ins=[proj, proj, b_merge, y_a, y_b],
        in_specs=[pl.BlockSpec((tm, D), lambda i, j, k: (i, 3)), pl.BlockSpec((tm, D), lambda i, j, k: (i, 4)),
                  pl.BlockSpec((1, 2 * D), lambda i, j, k: (0, 0)), tok, tok],
        outs=[jax.ShapeDtypeStruct((t, D), BF16), jax.ShapeDtypeStruct((t, D), BF16),
              jax.ShapeDtypeStruct((t, 2 * D), BF16), jax.ShapeDtypeStruct((8, 2 * D), F32)],
        out_specs=[tok, tok, pl.BlockSpec((tm, 2 * D), lambda i, j, k: (i, 0)), pl.BlockSpec((8, 2 * D), lambda i, j, k: (0, 0))])


GLA_HEADS = 2
GLA_UNROLL = 8


def _log_sigmoid(z):
    return jnp.minimum(z, 0.0) - jnp.log(1.0 + jnp.exp(-jnp.abs(z)))


def _tri(lower):
    r = lax.broadcasted_iota(jnp.int32, (CHUNK, CHUNK), 0)
    c = lax.broadcasted_iota(jnp.int32, (CHUNK, CHUNK), 1)
    return jnp.where(r >= c if lower else r <= c, 1.0, 0.0).astype(F32)


def _gla_fwd(proj, wa_pad, b_alpha, g_out, bsz):
    t = proj.shape[0]
    s = t // bsz
    nc = s // CHUNK
    p, kw, vw = GLA_HEADS, GLA_HEADS * GDK, GLA_HEADS * GDV

    def body(q_ref, k_ref, v_ref, gg_ref, ms_ref, wa_ref, ba_ref, go_ref, o_ref, og_ref, st_ref, la, state):
        z = _dot(ms_ref[...].astype(BF16), wa_ref[...], NN) + ba_ref[...]
        la[...] = _log_sigmoid(z) * (1.0 / GTAU)
        state[...] = jnp.zeros_like(state)
        low = _tri(True)
        gout = go_ref[...]

        def chunk(n, carry):
            rows = pl.ds(pl.multiple_of(n * CHUNK, CHUNK), CHUNK)
            for hh in range(p):
                kc, vc = slice(hh * GDK, (hh + 1) * GDK), slice(hh * GDV, (hh + 1) * GDV)
                lac = la[rows, kc]
                cum = _dot(low, lac, NN, lax.Precision.HIGHEST)
                ce = jnp.sum(lac, axis=0, keepdims=True)
                kd = (k_ref[rows, kc].astype(F32) * jnp.exp(ce - cum)).astype(BF16)
                new = state[vc, :] * jnp.exp(ce) + _dot(v_ref[rows, vc].astype(BF16), kd, TN)
                state[vc, :] = new
                st_ref[pl.ds(pl.multiple_of((hh * nc + n) * GDV, GDV), GDV), :] = new.astype(BF16)
                qs = (q_ref[rows, kc].astype(F32) * (GDK ** -0.5)).astype(BF16)
                o = _dot(qs, new.astype(BF16), NT)
                o_ref[rows, vc] = o
                ro = lax.rsqrt(jnp.mean(o * o, axis=1, keepdims=True) + EPS)
                og_ref[rows, vc] = (((o * ro) * gout) * _silu(gg_ref[rows, vc].astype(F32))).astype(BF16)
            return carry

        lax.fori_loop(0, nc, chunk, 0, unroll=GLA_UNROLL)

    return pl.pallas_call(
        body, name="gla_fwd", grid=(bsz, GH // p),
        in_specs=[pl.BlockSpec((s, kw), lambda b, h: (b, h)), pl.BlockSpec((s, kw), lambda b, h: (b, GH // p + h)),
                  pl.BlockSpec((s, vw), lambda b, h: (b, GH // p + h)), pl.BlockSpec((s, vw), lambda b, h: (b, 2 * GH // p + h)),
                  pl.BlockSpec((s, LANES), lambda b, h: (b, 43)),
                  pl.BlockSpec((LANES, kw), lambda b, h: (0, h)), pl.BlockSpec((1, kw), lambda b, h: (0, h)),
                  pl.BlockSpec((1, GDV), lambda b, h: (0, 0))],
        out_specs=[pl.BlockSpec((s, vw), lambda b, h: (b, h)), pl.BlockSpec((s, vw), lambda b, h: (b, h)),
                   pl.BlockSpec((p * nc * GDV, GDK), lambda b, h: (b * (GH // p) + h, 0))],
        out_shape=[jax.ShapeDtypeStruct((t, GH * GDV), F32), jax.ShapeDtypeStruct((t, GH * GDV), BF16),
                   jax.ShapeDtypeStruct((bsz * GH * nc * GDV, GDK), BF16)],
        scratch_shapes=[pltpu.VMEM((s, kw), F32), pltpu.VMEM((vw, GDK), F32)],
        compiler_params=_cparams(("arbitrary", "arbitrary")),
    )(proj, proj, proj, proj, proj, wa_pad, b_alpha, g_out)


def _gla_bwd(dog, o, proj, wa_pad, b_alpha, g_out, states, bsz):
    t = proj.shape[0]
    s = t // bsz
    nc = s // CHUNK
    p, kw, vw = GLA_HEADS, GLA_HEADS * GDK, GLA_HEADS * GDV

    def body(dog_ref, o_ref, q_ref, k_ref, v_ref, gg_ref, ms_ref, wa_ref, ba_ref, go_ref, st_ref,
             dq_ref, dk_ref, dv_ref, dgg_ref, dz_ref, dba, dgo, zs, la, carry_g):
        @pl.when(pl.program_id(1) == 0)
        def _():
            dba[...] = jnp.zeros_like(dba)
            dgo[...] = jnp.zeros_like(dgo)

        z = _dot(ms_ref[...].astype(BF16), wa_ref[...], NN) + ba_ref[...]
        zs[...] = z
        la[...] = _log_sigmoid(z) * (1.0 / GTAU)
        carry_g[...] = jnp.zeros_like(carry_g)
        low, upp = _tri(True), _tri(False)
        gout = go_ref[...]
        last_row = lax.broadcasted_iota(jnp.int32, (CHUNK, GDK), 0) == CHUNK - 1

        def chunk(step, carry):
            n = nc - 1 - step
            rows = pl.ds(pl.multiple_of(n * CHUNK, CHUNK), CHUNK)
            for hh in range(p):
                kc, vc = slice(hh * GDK, (hh + 1) * GDK), slice(hh * GDV, (hh + 1) * GDV)
                lac = la[rows, kc]
                cum = _dot(low, lac, NN, lax.Precision.HIGHEST)
                ce = jnp.sum(lac, axis=0, keepdims=True)
                e = jnp.exp(ce - cum)
                dec = jnp.exp(ce)
                kf = k_ref[rows, kc].astype(F32)
                kd = (kf * e).astype(BF16)
                vv = v_ref[rows, vc].astype(BF16)
                qs = (q_ref[rows, kc].astype(F32) * (GDK ** -0.5)).astype(BF16)
                ov = o_ref[rows, vc]
                ro = lax.rsqrt(jnp.mean(ov * ov, axis=1, keepdims=True) + EPS)
                on = ov * ro
                gg = gg_ref[rows, vc].astype(F32)
                sg = _sigmoid(gg)
                dogv = dog_ref[rows, vc].astype(F32)
                dgg_ref[rows, vc] = (dogv * (on * gout) * (sg * (1.0 + gg * (1.0 - sg)))).astype(BF16)
                t1 = dogv * (gg * sg)
                dgo[8 * hh:8 * hh + 1, :] += jnp.sum(t1 * on, axis=0, keepdims=True)
                don = t1 * gout
                do = ro * (don - on * jnp.mean(don * on, axis=1, keepdims=True))
                dob = do.astype(BF16)
                st_n = st_ref[pl.ds(pl.multiple_of((hh * nc + n) * GDV, GDV), GDV), :]
                dq_ref[rows, kc] = (_dot(dob, st_n, NN) * (GDK ** -0.5)).astype(BF16)
                dn = carry_g[vc, :] + _dot(dob, qs, TN)
                prev = hh * nc + jnp.maximum(n - 1, 0)
                st_p = st_ref[pl.ds(pl.multiple_of(prev * GDV, GDV), GDV), :].astype(F32) * jnp.where(n > 0, 1.0, 0.0)
                ddec = jnp.sum(dn * st_p, axis=0, keepdims=True)
                dnb = dn.astype(BF16)
                dkd = _dot(vv, dnb, NN)
                dv_ref[rows, vc] = _dot(kd, dnb, NT).astype(BF16)
                dk_ref[rows, kc] = (dkd * e).astype(BF16)
                w = dkd * kf * e
                dce = jnp.sum(w, axis=0, keepdims=True) + ddec * dec
                dcum = jnp.where(last_row, dce - w, -w)
                dla = _dot(upp, dcum, NN, lax.Precision.HIGHEST)
                dz = dla * (1.0 / GTAU) * _sigmoid(-zs[rows, kc])
                dba[0:1, kc] += jnp.sum(dz, axis=0, keepdims=True)
                dz_ref[rows, kc] = dz.astype(BF16)
                carry_g[vc, :] = dn * dec
            return carry

        lax.fori_loop(0, nc, chunk, 0, unroll=GLA_UNROLL)

    hv = pl.BlockSpec((s, vw), lambda h, b: (b, h))
    hk = pl.BlockSpec((s, kw), lambda h, b: (b, h))
    return pl.pallas_call(
        body, name="gla_bwd", grid=(GH // p, bsz),
        in_specs=[hv, hv, hk, pl.BlockSpec((s, kw), lambda h, b: (b, GH // p + h)),
                  pl.BlockSpec((s, vw), lambda h, b: (b, GH // p + h)), pl.BlockSpec((s, vw), lambda h, b: (b, 2 * GH // p + h)),
                  pl.BlockSpec((s, LANES), lambda h, b: (b, 43)), pl.BlockSpec((LANES, kw), lambda h, b: (0, h)),
                  pl.BlockSpec((1, kw), lambda h, b: (0, h)), pl.BlockSpec((1, GDV), lambda h, b: (0, 0)),
                  pl.BlockSpec((p * nc * GDV, GDK), lambda h, b: (b * (GH // p) + h, 0))],
        out_specs=[hk, hk, hv, hv, hk, pl.BlockSpec((8, kw), lambda h, b: (0, h)),
                   pl.BlockSpec((8 * p, GDV), lambda h, b: (h, 0))],
        out_shape=[jax.ShapeDtypeStruct((t, GH * GDK), BF16), jax.ShapeDtypeStruct((t, GH * GDK), BF16),
                   jax.ShapeDtypeStruct((t, GH * GDV), BF16), jax.ShapeDtypeStruct((t, GH * GDV), BF16),
                   jax.ShapeDtypeStruct((t, GH * GDK), BF16), jax.ShapeDtypeStruct((8, GH * GDK), F32),
                   jax.ShapeDtypeStruct((8 * GH, GDV), F32)],
        scratch_shapes=[pltpu.VMEM((s, kw), F32), pltpu.VMEM((s, kw), F32), pltpu.VMEM((vw, GDK), F32)],
        compiler_params=_cparams(("arbitrary", "arbitrary")),
    )(dog, o, proj, proj, proj, proj, proj, wa_pad, b_alpha, g_out, states)


def _rope_tables(pos_ref, fr_ref, sg_ref):
    ang = pos_ref[...].astype(F32) * fr_ref[...]
    return jnp.cos(ang), jnp.sin(ang) * sg_ref[...]


def _partner(x):
    lane = lax.broadcasted_iota(jnp.int32, x.shape, 1)
    return jnp.where(lane < NOPE + ROPE // 2, pltpu.roll(x, LANES - ROPE // 2, 1), pltpu.roll(x, ROPE // 2, 1))


def _mla_rows(t):
    return _tile(t, 512)


def _mla_pre_fwd(proj, pos, fr, sg, q_lat_g, kv_lat_g, qn_g, kn_g, wuq, wukv):
    t = proj.shape[0]
    tr = _mla_rows(t)

    def body(cq_ref, ckv_ref, ms_ref, pos_ref, fr_ref, sg_ref, qlg, kvlg, qng, kng, wuq_ref, wukv_ref, q_out, k_out, v_out):
        lane = lax.broadcasted_iota(jnp.int32, (tr, HP), 1)
        real = jnp.where(lane < MQK, 1.0, 0.0)
        cos, sin = _rope_tables(pos_ref, fr_ref, sg_ref)
        cos = cos * real
        cq = cq_ref[...].astype(F32)
        cqn = (cq * lax.rsqrt(jnp.mean(cq * cq, axis=1, keepdims=True) + EPS) * qlg[...]).astype(BF16)
        ckv = ckv_ref[...].astype(F32)
        ckvn = (ckv * lax.rsqrt(jnp.mean(ckv * ckv, axis=1, keepdims=True) + EPS) * kvlg[...]).astype(BF16)
        kpe = jnp.where((lane >= NOPE) & (lane < MQK), ms_ref[...].astype(F32), 0.0)
        kpe = kpe + jnp.where(lane < MQK + ROPE // 2, pltpu.roll(kpe, ROPE, 1), 0.0)
        lane_all = lax.broadcasted_iota(jnp.int32, (tr, MH * HP), 1)
        v_out[...] = jnp.where(lane_all % HP == MV, 1.0, _dot(ckvn, wukv_ref[:, MH * HP:], NN)).astype(BF16)

        def norm_rope(x, gain):
            xn = x * lax.rsqrt(jnp.sum(x * x * real, axis=1, keepdims=True) * (1.0 / MQK) + EPS) * gain
            return (xn * cos + pltpu.roll(xn, LANES - ROPE // 2, 1) * sin).astype(BF16)

        for h in range(MH):
            cols = slice(h * HP, (h + 1) * HP)
            q_out[:, cols] = norm_rope(_dot(cqn, wuq_ref[:, cols], NN), qng[...])
            k_out[:, cols] = norm_rope(_dot(ckvn, wukv_ref[:, cols], NN) + kpe, kng[...])

    def full(a):
        return pl.BlockSpec(a.shape, lambda i: (0, 0))

    wide = pl.BlockSpec((tr, MH * HP), lambda i: (i, 0))
    return pl.pallas_call(
        body, name="mla_pre_fwd", grid=(t // tr,),
        in_specs=[pl.BlockSpec((tr, MQR), lambda i: (i, 20)), pl.BlockSpec((tr, MKVR), lambda i: (i, 42)),
                  pl.BlockSpec((tr, LANES), lambda i: (i, 43)), pl.BlockSpec((tr, 1), lambda i: (i, 0)),
                  full(fr), full(sg), full(q_lat_g), full(kv_lat_g), full(qn_g), full(kn_g), full(wuq), full(wukv)],
        out_specs=[wide, wide, wide],
        out_shape=[jax.ShapeDtypeStruct((t, MH * HP), BF16)] * 3,
        compiler_params=_cparams(("arbitrary",)),
    )(proj, proj, proj, pos, fr, sg, q_lat_g, kv_lat_g, qn_g, kn_g, wuq, wukv)


def _mla_pre_bwd(dq2, dk2, dv2, dmisc_gla, proj, pos, fr, sg, q_lat_g, kv_lat_g, qn_g, kn_g, wuq, wukv):
    t = proj.shape[0]
    tr = _mla_rows(t)

    def body(dq_ref, dk_ref, dv_ref, dmg_ref, cq_ref, ckv_ref, ms_ref, pos_ref, fr_ref, sg_ref, qlg, kvlg, qng, kng,
             wuq_ref, wukv_ref, dcq_ref, dckv_ref, dms_ref, dwuq, dwukv, acc, dqf, dkvf):
        @pl.when(pl.program_id(0) == 0)
        def _():
            dwuq[...] = jnp.zeros_like(dwuq)
            dwukv[...] = jnp.zeros_like(dwukv)
            acc[...] = jnp.zeros_like(acc)

        cos, sin = _rope_tables(pos_ref, fr_ref, sg_ref)
        cq = cq_ref[...].astype(F32)
        rc = lax.rsqrt(jnp.mean(cq * cq, axis=1, keepdims=True) + EPS)
        xc = cq * rc
        cqn = (xc * qlg[...]).astype(BF16)
        ckv = ckv_ref[...].astype(F32)
        rkv = lax.rsqrt(jnp.mean(ckv * ckv, axis=1, keepdims=True) + EPS)
        xkv = ckv * rkv
        ckvn = (xkv * kvlg[...]).astype(BF16)
        lane = lax.broadcasted_iota(jnp.int32, (tr, HP), 1)
        is_rope = (lane >= NOPE) & (lane < MQK)
        kpe = jnp.where(is_rope, ms_ref[...].astype(F32), 0.0)
        dkpe = jnp.zeros((tr, HP), F32)
        dqng = jnp.zeros((1, HP), F32)
        dkng = jnp.zeros((1, HP), F32)
        for h in range(MH):
            cols = slice(h * HP, (h + 1) * HP)
            qh = _dot(cqn, wuq_ref[:, cols], NN)
            rq = lax.rsqrt(jnp.sum(qh * qh, axis=1, keepdims=True) * (1.0 / MQK) + EPS)
            xq = qh * rq
            dy = dq_ref[:, cols].astype(F32)
            dqn = dy * cos - _partner(dy) * sin
            dqng += jnp.sum(dqn * xq, axis=0, keepdims=True)
            tq = dqn * qng[...]
            dqf[:, cols] = (rq * (tq - xq * (jnp.sum(tq * xq, axis=1, keepdims=True) * (1.0 / MQK)))).astype(BF16)
            kh = _dot(ckvn, wukv_ref[:, cols], NN) + kpe
            rk = lax.rsqrt(jnp.sum(kh * kh, axis=1, keepdims=True) * (1.0 / MQK) + EPS)
            xk = kh * rk
            dy = dk_ref[:, cols].astype(F32)
            dkn = dy * cos - _partner(dy) * sin
            dkng += jnp.sum(dkn * xk, axis=0, keepdims=True)
            tk = dkn * kng[...]
            dkh = rk * (tk - xk * (jnp.sum(tk * xk, axis=1, keepdims=True) * (1.0 / MQK)))
            dkvf[:, cols] = jnp.where(lane < NOPE, dkh, 0.0).astype(BF16)
            dkpe += jnp.where(is_rope, dkh, 0.0)
        dkvf[:, MH * HP:] = dv_ref[...]
        acc[2:3, 0:HP] += dqng
        acc[3:4, 0:HP] += dkng
        dms_ref[...] = (dmg_ref[...] + dkpe).astype(BF16)

        dqfv = dqf[...]
        dwuq[...] += _dot(cqn, dqfv, TN)
        dcqn = _dot(dqfv, wuq_ref[...], NT)
        acc[0:1, :] += jnp.sum(dcqn * xc, axis=0, keepdims=True)
        tc = dcqn * qlg[...]
        dcq_ref[...] = (rc * (tc - xc * jnp.mean(tc * xc, axis=1, keepdims=True))).astype(BF16)

        dkvfv = dkvf[...]
        dwukv[...] += _dot(ckvn, dkvfv, TN)
        dckvn = _dot(dkvfv, wukv_ref[...], NT)
        acc[1:2, 0:MKVR] += jnp.sum(dckvn * xkv, axis=0, keepdims=True)
        tkv = dckvn * kvlg[...]
        dckv_ref[...] = (rkv * (tkv - xkv * jnp.mean(tkv * xkv, axis=1, keepdims=True))).astype(BF16)

    def full(a):
        return pl.BlockSpec(a.shape, lambda i: (0, 0))

    wide = pl.BlockSpec((tr, MH * HP), lambda i: (i, 0))
    narrow = pl.BlockSpec((tr, LANES), lambda i: (i, 0))
    return pl.pallas_call(
        body, name="mla_pre_bwd", grid=(t // tr,),
        in_specs=[wide, wide, wide, narrow,
                  pl.BlockSpec((tr, MQR), lambda i: (i, 20)), pl.BlockSpec((tr, MKVR), lambda i: (i, 42)),
                  pl.BlockSpec((tr, LANES), lambda i: (i, 43)), pl.BlockSpec((tr, 1), lambda i: (i, 0)),
                  full(fr), full(sg), full(q_lat_g), full(kv_lat_g), full(qn_g), full(kn_g), full(wuq), full(wukv)],
        out_specs=[pl.BlockSpec((tr, MQR), lambda i: (i, 0)), narrow, narrow,
                   pl.BlockSpec((MQR, MH * HP), lambda i: (0, 0)), pl.BlockSpec((MKVR, 2 * MH * HP), lambda i: (0, 0)),
                   pl.BlockSpec((8, MQR), lambda i: (0, 0))],
        out_shape=[jax.ShapeDtypeStruct((t, MQR), BF16), jax.ShapeDtypeStruct((t, MKVR), BF16),
                   jax.ShapeDtypeStruct((t, LANES), BF16), jax.ShapeDtypeStruct((MQR, MH * HP), F32),
                   jax.ShapeDtypeStruct((MKVR, 2 * MH * HP), F32), jax.ShapeDtypeStruct((8, MQR), F32)],
        scratch_shapes=[pltpu.VMEM((tr, MH * HP), BF16), pltpu.VMEM((tr, 2 * MH * HP), BF16)],
        compiler_params=_cparams(("arbitrary",)),
    )(dq2, dk2, dv2, dmisc_gla, proj, proj, proj, pos, fr, sg, q_lat_g, kv_lat_g, qn_g, kn_g, wuq, wukv)


ATT_FWD_TILES = (1024, 512)
ATT_BWD_TILES = (512, 512)
ATT_HEADS = 2
NEG = -1e30
LOG2E = 1.4426950408889634


def _att_mask(q0, k0, tq, tk):
    qc = (q0 + lax.broadcasted_iota(jnp.int32, (tq, tk), 0)) // CHUNK
    kc = (k0 + lax.broadcasted_iota(jnp.int32, (tq, tk), 1)) // CHUNK
    return kc <= qc


def _att_tiles(s, tiles):
    return _tile(s, tiles[0]), _tile(s, tiles[1])


def _lanes(x, n):
    return x if n == 1 else jnp.concatenate([x] * n, axis=1)


def _grid_ends(grid):
    i, j = pl.program_id(0), pl.program_id(1)
    return (i == 0) & (j == 0), (i == grid[0] - 1) & (j == grid[1] - 1)


def _attn_fwd(q2, k2, v2, bsz, slab):
    t = q2.shape[0]
    s = t // bsz
    tq, tk = _att_tiles(s, ATT_FWD_TILES)
    nq, groups, n_diag = s // tq, tk // HP, max(tq // tk, 1)
    sub_rows = tq // n_diag
    scale = MQK ** -0.5
    c2 = scale * LOG2E
    heads = range(ATT_HEADS)

    def body(q_ref, k_ref, v_ref, slab_ref, o_ref, lse_ref, gath_ref, send_sems, recv_sems, local_sem):
        gather = _core_row_gather_copies(slab_ref, gath_ref, send_sems, recv_sems, local_sem)
        first, last = _grid_ends((bsz, MH // ATT_HEADS))
        pl.when(first)(gather.start)

        def q_loop(qi, carry):
            q0 = pl.multiple_of(qi * tq, tq)
            rows = pl.ds(q0, tq)
            n_full = q0 // tk
            qs = [q_ref[rows, h * HP:(h + 1) * HP] for h in heads]

            def scores(h, kj, sub=None, masked=False):
                k0 = pl.multiple_of(kj * tk, tk)
                qv = qs[h] if sub is None else qs[h][sub * sub_rows:(sub + 1) * sub_rows]
                sc = _dot(qv, k_ref[pl.ds(k0, tk), h * HP:(h + 1) * HP], NT)
                return jnp.where(_att_mask(q0 + sub * sub_rows, k0, sub_rows, tk), sc, NEG) if masked else sc

            def fold(mx, sc):
                for j in range(groups):
                    mx = jnp.maximum(mx, sc[:, j * HP:(j + 1) * HP])
                return mx

            def over_diagonal(vals, step):
                out = []
                for h in heads:
                    blocks = []
                    for r in range(n_diag):
                        v = vals[h][r * sub_rows:(r + 1) * sub_rows]
                        for u in range(r + 1):
                            v = step(v, h, n_full + u, r, u == r)
                        blocks.append(v)
                    out.append(blocks[0] if n_diag == 1 else jnp.concatenate(blocks, axis=0))
                return tuple(out)

            mx = lax.fori_loop(0, n_full, lambda kj, mx: tuple(fold(mx[h], scores(h, kj)) for h in heads),
                               tuple(jnp.full((tq, HP), NEG, F32) for _ in heads))
            mx = over_diagonal(mx, lambda v, h, kj, r, masked: fold(v, scores(h, kj, r, masked)))
            mb = [jnp.broadcast_to(jnp.max(mx[h], axis=1, keepdims=True), (tq, HP)) for h in heads]

            def weighted(h, kj, sub=None, masked=False):
                m = mb[h] if sub is None else mb[h][sub * sub_rows:(sub + 1) * sub_rows]
                p = jnp.exp2((scores(h, kj, sub, masked) - _lanes(m, groups)) * c2)
                k0 = pl.multiple_of(kj * tk, tk)
                return _dot(p.astype(BF16), v_ref[pl.ds(k0, tk), h * HP:(h + 1) * HP], NN)

            acc = lax.fori_loop(0, n_full, lambda kj, acc: tuple(acc[h] + weighted(h, kj) for h in heads),
                                tuple(jnp.zeros((tq, HP), F32) for _ in heads))
            acc = over_diagonal(acc, lambda v, h, kj, r, masked: v + weighted(h, kj, r, masked))
            lane = lax.broadcasted_iota(jnp.int32, (tq, HP), 1)
            for h in heads:
                a = acc[h]
                l = jnp.sum(jnp.where(lane == MV, a, 0.0), axis=1, keepdims=True)
                o_ref[rows, h * HP:(h + 1) * HP] = (a / l).astype(BF16)
                lse_ref[rows, h * HP:(h + 1) * HP] = mb[h] * scale + jnp.log(l)
            return carry

        lax.fori_loop(0, nq, q_loop, 0)
        pl.when(last)(gather.finish)

    spec = pl.BlockSpec((s, ATT_HEADS * HP), lambda b, h: (b, h))
    return pl.pallas_call(
        body, name="attn_fwd", grid=(bsz, MH // ATT_HEADS), in_specs=[spec] * 3 + [HBM_SPEC],
        out_specs=[spec, spec, HBM_SPEC],
        out_shape=[jax.ShapeDtypeStruct((t, MH * HP), BF16), jax.ShapeDtypeStruct((t, MH * HP), F32),
                   jax.ShapeDtypeStruct((N_DEV,) + slab.shape, slab.dtype)],
        scratch_shapes=EXCHANGE_SEMS, compiler_params=_cparams(("arbitrary", "arbitrary")),
    )(q2, k2, v2, slab)


def _attn_bwd(q2, k2, v2, do2, o2, lse2, bsz, tsum):
    t = q2.shape[0]
    s = t // bsz
    tq, tk = _att_tiles(s, ATT_BWD_TILES)
    nq, nk, per, groups = s // tq, s // tk, max(tk // tq, 1), tk // HP
    scale = MQK ** -0.5
    c2 = scale * LOG2E
    heads = range(ATT_HEADS)

    def body(q_ref, k_ref, v_ref, do_ref, o_ref, lse_ref, t_ref, dq_ref, dk_ref, dv_ref, parts_ref, dq_acc, delta, lse_b2,
             send_sems, recv_sems, local_sem):
        exchange = _all_to_all_copies(t_ref, parts_ref, send_sems, recv_sems, local_sem)
        first, last = _grid_ends((bsz, MH // ATT_HEADS))
        pl.when(first)(exchange.start)
        dq_acc[...] = jnp.zeros_like(dq_acc)

        def d_loop(i, carry):
            rows = pl.ds(pl.multiple_of(i * tq, tq), tq)
            for h in heads:
                hs = slice(h * HP, (h + 1) * HP)
                dl = jnp.sum(do_ref[rows, hs].astype(F32) * o_ref[rows, hs].astype(F32), axis=1, keepdims=True)
                delta[rows, hs] = jnp.broadcast_to(dl, (tq, HP))
            lse_b2[rows, :] = lse_ref[rows, :] * LOG2E
            return carry

        lax.fori_loop(0, nq, d_loop, 0)

        def k_loop(kj, carry):
            k0 = pl.multiple_of(kj * tk, tk)
            kk = [k_ref[pl.ds(k0, tk), h * HP:(h + 1) * HP] for h in heads]
            vv = [v_ref[pl.ds(k0, tk), h * HP:(h + 1) * HP] for h in heads]

            def tile(qi, c, masked):
                q0 = pl.multiple_of(qi * tq, tq)
                rows = pl.ds(q0, tq)
                out = []
                for h in heads:
                    hs = slice(h * HP, (h + 1) * HP)
                    dk, dv = c[h]
                    q = q_ref[rows, hs]
                    do = do_ref[rows, hs]
                    e = _dot(q, kk[h], NT) * c2 - _lanes(lse_b2[rows, hs], groups)
                    if masked:
                        e = jnp.where(_att_mask(q0, k0, tq, tk), e, NEG)
                    p = jnp.exp2(e)
                    dv = dv + _dot(p.astype(BF16), do, TN)
                    ds = (p * (_dot(do, vv[h], NT) - _lanes(delta[rows, hs], groups))).astype(BF16)
                    dq_acc[rows, hs] += _dot(ds, kk[h], NN)
                    dk = dk + _dot(ds, q, TN)
                    out.append((dk, dv))
                return tuple(out)

            zero = jnp.zeros((tk, HP), F32)
            c = tuple((zero, zero) for _ in heads)
            first = k0 // tq
            for u in range(per):
                c = tile(first + u, c, True)
            c = lax.fori_loop(first + per, nq, lambda qi, c: tile(qi, c, False), c)
            for h in heads:
                dk_ref[pl.ds(k0, tk), h * HP:(h + 1) * HP] = (c[h][0] * scale).astype(BF16)
                dv_ref[pl.ds(k0, tk), h * HP:(h + 1) * HP] = c[h][1].astype(BF16)
            return carry

        lax.fori_loop(0, nk, k_loop, 0)
        dq_ref[...] = (dq_acc[...] * scale).astype(BF16)
        pl.when(last)(exchange.finish)

    spec = pl.BlockSpec((s, ATT_HEADS * HP), lambda b, h: (b, h))
    return pl.pallas_call(
        body, name="attn_bwd", grid=(bsz, MH // ATT_HEADS), in_specs=[spec] * 6 + [HBM_SPEC],
        out_specs=[spec] * 3 + [HBM_SPEC],
        out_shape=[jax.ShapeDtypeStruct((t, MH * HP), BF16)] * 3 + [jax.ShapeDtypeStruct(tsum.shape, tsum.dtype)],
        scratch_shapes=[pltpu.VMEM((s, ATT_HEADS * HP), F32)] * 3 + EXCHANGE_SEMS,
        compiler_params=_cparams(("arbitrary", "arbitrary")),
    )(q2, k2, v2, do2, o2, lse2, tsum)


def _perm_w_in_t(w):
    z = lambda n: jnp.zeros((n, w.shape[1]), w.dtype)
    return jnp.concatenate([w[:3072], w[3504:5552], w[3088:3344], w[3344:3472], w[3072:3088], z(48), w[3472:3504], z(32)],
                           axis=0)


def _unperm_w_in_t(g):
    return jnp.concatenate([g[:3072], g[5504:5520], g[5120:5376], g[5376:5504], g[5568:5600], g[3072:5120]], axis=0)


def _pad_wa(w):
    return jnp.pad(w, ((0, LANES - GLR), (0, 0)))


def _pad_wuq(w):
    return jnp.pad(w.reshape(MQR, MH, MQK), ((0, 0), (0, 0), (0, HP - MQK))).reshape(MQR, MH * HP)


def _unpad_wuq(g):
    return g.reshape(MQR, MH, HP)[:, :, :MQK].reshape(MQR, MH * MQK)


def _pad_wukv(w):
    w3 = w.reshape(MKVR, MH, NOPE + MV)
    kp = jnp.pad(w3[:, :, :NOPE], ((0, 0), (0, 0), (0, HP - NOPE))).reshape(MKVR, MH * HP)
    vp = jnp.pad(w3[:, :, NOPE:], ((0, 0), (0, 0), (0, HP - MV))).reshape(MKVR, MH * HP)
    return jnp.concatenate([kp, vp], axis=1)


def _unpad_wukv(g):
    kp = g[:, :MH * HP].reshape(MKVR, MH, HP)[:, :, :NOPE]
    vp = g[:, MH * HP:].reshape(MKVR, MH, HP)[:, :, :MV]
    return jnp.concatenate([kp, vp], axis=2).reshape(MKVR, MH * (NOPE + MV))


def _pad_wo(w):
    return jnp.pad(w.reshape(MH, MV, D), ((0, 0), (0, HP - MV), (0, 0))).reshape(MH * HP, D)


def _unpad_wo(g):
    return g.reshape(MH, HP, D)[:, :MV, :].reshape(MH * MV, D)


def _repeat_half(a):
    a3 = a.reshape(a.shape[0], -1, HP)
    a3 = jnp.concatenate([a3[:, :, :MQK], a3[:, :, NOPE:NOPE + ROPE // 2], a3[:, :, MQK + ROPE // 2:]], axis=2)
    return a3.reshape(a.shape)


def _pad_lanes(v, n=HP):
    return jnp.pad(v, ((0, 0), (0, n - v.shape[1])))


def _local_step(x, positions, tgt, wt, slab_b, sp, mod3):
    bsz, s, _ = x.shape
    t = bsz * s
    x2 = x.reshape(t, D)
    tgt2 = tgt.reshape(t, D)
    pos = positions.reshape(t, 1)
    fr16 = ROPE_THETA ** (-jnp.arange(0, ROPE, 2, dtype=F32) / ROPE)
    zero = lambda n: jnp.zeros((n,), F32)
    fr = jnp.concatenate([zero(NOPE), fr16, fr16, zero(HP - MQK)]).reshape(1, HP)
    sg = jnp.concatenate([zero(NOPE), -jnp.ones((ROPE // 2,), F32), jnp.ones((ROPE // 2,), F32), zero(HP - MQK)]).reshape(1, HP)

    w_in_t = _perm_w_in_t(wt["w_in"])
    wa_pad = _pad_wa(wt["gla_w_alpha"].T)
    wuq = _pad_wuq(wt["mla_w_uq"].T)
    wukv = _pad_wukv(wt["mla_w_ukv"].T)
    qn_g, kn_g = _pad_lanes(sp["mla_qn_g"]), _pad_lanes(sp["mla_kn_g"])

    tm = _tile(s, 1024)
    h = _norm_mod_fwd(x2, sp["norm1_g"], mod3, 0, 1, "norm1_fwd")
    proj = _mm(h, w_in_t, "nt", (BF16,), "proj_fwd", tn=PW // 4)
    o_gla, og, states = _gla_fwd(proj, wa_pad, sp["gla_b_alpha"], sp["gla_out_norm_g"], bsz)
    q2, k2, v2 = _mla_pre_fwd(proj, pos, fr, sg, sp["mla_q_lat_g"], sp["mla_kv_lat_g"], _repeat_half(qn_g), _repeat_half(kn_g),
                              _repeat_half(wuq), wukv)
    o2, lse2, core_row = _attn_fwd(q2, k2, v2, bsz, slab_b)
    wt = dict(wt, **_unpack_gathered(_cross_core_fill(core_row), SLAB_B))
    wo_pad = _pad_wo(wt["mla_w_o"])
    y_a = _mm(og, wt["gla_w_o"], "nn", (BF16,), "gla_out_fwd")
    y_b = _mm(o2, wo_pad, "nn", (BF16,), "mla_out_fwd")
    mix = _merge_fwd(proj, sp["b_merge"], y_a, y_b)
    mixed, x1, h2 = _mm(mix, wt["w_out"], "nn", (), "w_out_fwd", tm=tm,
                        tail=_norm_fwd_tail(x2, sp["norm2_g"], mod3, 3, 4, 2, tm))
    a, f = _mm(h2, wt["mlp_w1"], "nt", (BF16, BF16), "mlp1_fwd",
               epi=lambda acc: (acc, jnp.square(jnp.maximum(acc, 0.0))))
    dy, dff, acc_g2, acc_loss = _mm(f, wt["mlp_w2"], "nn", (), "mlp2_fwd", tm=tm, tail=_loss_tail(x1, tgt2, mod3, tm))

    gw = {}
    gw["mlp_w2"] = _mm(f, dff, "tn", (BF16,), "mlp2_dw")
    da = _mm(dff, wt["mlp_w2"], "nt", (BF16,), "mlp2_dx", extras=(a,),
             epi=lambda acc, av: (acc * (2.0 * jnp.maximum(av.astype(F32), 0.0)),))
    gw["mlp_w1"] = _mm(da, h2, "tn", (BF16,), "mlp1_dw")
    dx1, dmixed, accb2, accg2 = _mm(da, wt["mlp_w1"], "nn", (), "mlp1_dx", tm=tm,
                                    tail=_norm_bwd_tail(x1, dy, sp["norm2_g"], mod3, 4, tm, mixed=mixed, i_gate=2))

    gw["w_out"] = _mm(mix, dmixed, "tn", (BF16,), "w_out_dw")
    dy_a, dy_b, dlogits, acc_bm = _mm(dmixed, wt["w_out"], "nt", (), "w_out_dx", tm=tm,
                                      tail=_merge_bwd_tail(proj, sp["b_merge"], y_a, y_b, tm))
    gw["gla_w_o"] = _mm(og, dy_a, "tn", (BF16,), "gla_out_dw")
    dog = _mm(dy_a, wt["gla_w_o"], "nt", (BF16,), "gla_out_dx")
    gw["mla_w_o"] = _unpad_wo(_mm(o2, dy_b, "tn", (BF16,), "mla_out_dw"))
    do2 = _mm(dy_b, wo_pad, "nt", (BF16,), "mla_out_dx")
    dq2, dk2, dv2, parts_b = _attn_bwd(q2, k2, v2, do2, o2, lse2, bsz, _pack_per_device(gw, SLAB_B))
    dq_g, dk_g, dv_g, dgg, dz, acc_ba, acc_go = _gla_bwd(dog, o_gla, proj, wa_pad, sp["gla_b_alpha"],
                                                         sp["gla_out_norm_g"], states, bsz)
    gw["gla_w_alpha"] = _mm(proj, dz, "tn", (F32,), "gla_alpha_dw", a_off=43, m=LANES)[:GLR].T.astype(BF16)
    dmisc_gla = _mm(dz, wa_pad, "nt", (F32,), "gla_alpha_dx")
    dcq, dckv, dmisc, gwuq, gwukv, acc_mla = _mla_pre_bwd(dq2, dk2, dv2, dmisc_gla, proj, pos, fr, sg, sp["mla_q_lat_g"],
                                                         sp["mla_kv_lat_g"], qn_g, kn_g, wuq, wukv)
    gw["mla_w_uq"] = _unpad_wuq(gwuq).T.astype(BF16)
    gw["mla_w_ukv"] = _unpad_wukv(gwukv).T.astype(BF16)
    dproj = jnp.concatenate([dq_g, dk_g, dv_g, dgg, dlogits, dcq, dckv, dmisc], axis=1)
    gw["w_in"] = _unperm_w_in_t(_mm(dproj, h, "tn", (BF16,), "proj_dw", tm=PW // 4))
    parts_a, grad_x, accb1, accg1 = _mm(dproj, w_in_t, "nn", (), "proj_dx", tm=tm, tk=PW // 4,
                                        cargo=_sum_over_cores(_pack_per_device(gw, SLAB_A), "a"),
                                        tail=_norm_bwd_tail(x2, dx1, sp["norm1_g"], mod3, 1, tm))

    dmod = jnp.stack([accb1[:, 0], accb1[:, 1], accb2[:, 2], accb2[:, 0], accb2[:, 1], acc_g2[:, 0]], axis=1)

    rows = {
        "dmod": dmod.reshape(bsz * 6, D),
        "norm1_g": accg1[0:1], "norm2_g": accg2[0:1],
        "b_merge": acc_bm[0:1].reshape(2, D),
        "gla_b_alpha": _pad_lanes(acc_ba[0:1], D),
        "gla_out_norm_g": _pad_lanes(acc_go.reshape(GH, 8, GDV)[:, 0, :], D),
        "mla_q_lat_g": _pad_lanes(acc_mla[0:1], D), "mla_kv_lat_g": _pad_lanes(acc_mla[1:2], D),
        "mla_qn_g": _pad_lanes(acc_mla[2:3], D), "mla_kn_g": _pad_lanes(acc_mla[3:4], D),
        "loss": acc_loss[0:1],
    }
    return grad_x.reshape(bsz, s, D), parts_a, parts_b, rows


HBM_SPEC = pl.BlockSpec(memory_space=pltpu.HBM)


def _all_gather(ps, name):
    n = len(ps)

    def body(*refs):
        p_refs, out_refs, (send_sems, recv_sems, local_sems) = refs[:n], refs[n:2 * n], refs[2 * n:]
        x, y, c = lax.axis_index("x"), lax.axis_index("y"), lax.axis_index("c")
        me, sibling = (x, y, c), (x, y, 1 - c)
        chips = [(1 - x, y), (x, 1 - y), (1 - x, 1 - y)]

        def copy(a, k, block, to, own=False):
            slot = out_refs[a].at[4 * block[0] + 2 * block[1] + block[2]]
            return pltpu.make_async_remote_copy(
                src_ref=p_refs[a] if own else slot, dst_ref=slot, send_sem=send_sems.at[7 * a + k],
                recv_sem=recv_sems.at[7 * a + k], device_id=to, device_id_type=MESH)

        mine = [pltpu.make_async_copy(p_refs[a], out_refs[a].at[4 * x + 2 * y + c], local_sems.at[a]) for a in range(n)]
        first = [copy(a, 0, me, sibling, own=True) for a in range(n)]
        first += [copy(a, 1 + j, me, (*chip, c), own=True) for a in range(n) for j, chip in enumerate(chips)]
        for cp in mine + first:
            cp.start()
        passed = []
        for j, chip in enumerate(chips):
            for a in range(n):
                copy(a, 1 + j, (*chip, c), me).wait_recv()
                passed.append(copy(a, 4 + j, (*chip, c), sibling))
                passed[-1].start()
        for a in range(n):
            copy(a, 0, sibling, me).wait_recv()
            for j, chip in enumerate(chips):
                copy(a, 4 + j, (*chip, 1 - c), me).wait_recv()
        for cp in first + passed:
            cp.wait_send()
        for cp in mine:
            cp.wait()

    return pl.pallas_call(
        body, name=name, out_shape=[jax.ShapeDtypeStruct((N_DEV,) + p.shape, p.dtype) for p in ps],
        in_specs=[HBM_SPEC] * n, out_specs=[HBM_SPEC] * n,
        scratch_shapes=[pltpu.SemaphoreType.DMA((7 * n,)), pltpu.SemaphoreType.DMA((7 * n,)), pltpu.SemaphoreType.DMA((n,))],
    )(*ps)


def _sibling_exchange(g, name):
    def body(g_ref, out_ref, send_sem, recv_sem):
        x, y, c = lax.axis_index("x"), lax.axis_index("y"), lax.axis_index("c")
        cp = pltpu.make_async_remote_copy(src_ref=g_ref, dst_ref=out_ref, send_sem=send_sem, recv_sem=recv_sem,
                                          device_id=(x, y, 1 - c), device_id_type=MESH)
        cp.start()
        cp.wait()

    return pl.pallas_call(
        body, name=name, out_shape=jax.ShapeDtypeStruct(g.shape, g.dtype),
        in_specs=[HBM_SPEC], out_specs=HBM_SPEC,
        scratch_shapes=[pltpu.SemaphoreType.DMA(()), pltpu.SemaphoreType.DMA(())],
    )(g)


class _Exchange:
    def __init__(self, local, sends, arrivals):
        self.local, self.sends, self.arrivals = local, sends, arrivals

    def start(self):
        self.local.start()
        for cp in self.sends:
            cp.start()

    def finish(self):
        for cp in self.arrivals:
            cp.wait_recv()
        for cp in self.sends:
            cp.wait_send()
        self.local.wait()


EXCHANGE_SEMS = [pltpu.SemaphoreType.DMA((N_DEV,)), pltpu.SemaphoreType.DMA((N_DEV,)), pltpu.SemaphoreType.DMA(())]


def _all_to_all_copies(t_ref, out_ref, send_sems, recv_sems, local_sem):
    x, y, c = lax.axis_index("x"), lax.axis_index("y"), lax.axis_index("c")
    me = 4 * x + 2 * y + c

    def copy(k, src, dst):
        px, py, pc = (1 - x if k & 4 else x), (1 - y if k & 2 else y), (1 - c if k & 1 else c)
        peer = 4 * px + 2 * py + pc
        return pltpu.make_async_remote_copy(src_ref=t_ref.at[peer if src is None else src],
                                            dst_ref=out_ref.at[peer if dst is None else dst], send_sem=send_sems.at[k],
                                            recv_sem=recv_sems.at[k], device_id=(px, py, pc), device_id_type=MESH)

    return _Exchange(pltpu.make_async_copy(t_ref.at[me], out_ref.at[me], local_sem),
                     [copy(k, None, me) for k in range(1, N_DEV)], [copy(k, me, None) for k in range(1, N_DEV)])


def _chip_exchange_copies(t_ref, out_ref, send_sems, recv_sems, local_sem):
    x, y, c = lax.axis_index("x"), lax.axis_index("y"), lax.axis_index("c")
    my_chip = 2 * x + y
    chips = [(1 - x, y), (x, 1 - y), (1 - x, 1 - y)]

    def copy(j, src, dst, px, py):
        return pltpu.make_async_remote_copy(src_ref=t_ref.at[src], dst_ref=out_ref.at[dst], send_sem=send_sems.at[j],
                                            recv_sem=recv_sems.at[j], device_id=(px, py, c), device_id_type=MESH)

    return _Exchange(pltpu.make_async_copy(t_ref.at[my_chip], out_ref.at[my_chip], local_sem),
                     [copy(j, 2 * px + py, my_chip, px, py) for j, (px, py) in enumerate(chips)],
                     [copy(j, my_chip, 2 * px + py, px, py) for j, (px, py) in enumerate(chips)])


def _core_row_gather_copies(p_ref, out_ref, send_sems, recv_sems, local_sem):
    x, y, c = lax.axis_index("x"), lax.axis_index("y"), lax.axis_index("c")
    peers = [(x, y, 1 - c), (1 - x, y, c), (x, 1 - y, c), (1 - x, 1 - y, c)]

    def slot(px, py, pc):
        return out_ref.at[4 * px + 2 * py + pc]

    def copy(j, block, to):
        return pltpu.make_async_remote_copy(src_ref=p_ref, dst_ref=slot(*block), send_sem=send_sems.at[j],
                                            recv_sem=recv_sems.at[j], device_id=to, device_id_type=MESH)

    return _Exchange(pltpu.make_async_copy(p_ref, slot(x, y, c), local_sem),
                     [copy(j, (x, y, c), peer) for j, peer in enumerate(peers)],
                     [copy(j, peer, peer) for j, peer in enumerate(peers)])


def _cross_core_fill(gathered):
    def body(g_ref, out_ref, send_sems, recv_sems):
        x, y, c = lax.axis_index("x"), lax.axis_index("y"), lax.axis_index("c")
        chips = [(1 - x, y), (x, 1 - y), (1 - x, 1 - y)]

        def copy(j, pc):
            px, py = chips[j]
            slot = 4 * px + 2 * py + pc
            return pltpu.make_async_remote_copy(src_ref=g_ref.at[slot], dst_ref=out_ref.at[slot], send_sem=send_sems.at[j],
                                                recv_sem=recv_sems.at[j], device_id=(x, y, 1 - c), device_id_type=MESH)

        sends = [copy(j, c) for j in range(3)]
        for cp in sends:
            cp.start()
        for j in range(3):
            copy(j, 1 - c).wait_recv()
        for cp in sends:
            cp.wait_send()

    return pl.pallas_call(
        body, name="weights_cross_core_fill", out_shape=jax.ShapeDtypeStruct(gathered.shape, gathered.dtype),
        in_specs=[HBM_SPEC], out_specs=HBM_SPEC, input_output_aliases={0: 0},
        scratch_shapes=[pltpu.SemaphoreType.DMA((3,)), pltpu.SemaphoreType.DMA((3,))],
    )(gathered)


def _slab_block(r):
    return max(b for b in range(16, SLAB_BLOCK_MAX + 1, 16) if r % b == 0)


def _pair_sum(a, b, name):
    n, r, cdim = a.shape
    rb = _slab_block(r)
    blk = pl.BlockSpec((1, rb, cdim), lambda j, i: (j, i, 0))

    def body(a_ref, b_ref, o_ref):
        o_ref[...] = (a_ref[...].astype(F32) + b_ref[...].astype(F32)).astype(BF16)

    return pl.pallas_call(
        body, name=name, grid=(n, r // rb), in_specs=[blk, blk], out_specs=blk,
        out_shape=jax.ShapeDtypeStruct(a.shape, BF16), compiler_params=_cparams(("arbitrary", "arbitrary")),
    )(a, b)


def _adamw_math(w, g, m, v):
    m = ADAM_B1 * m + (1.0 - ADAM_B1) * g
    v = ADAM_B2 * v + (1.0 - ADAM_B2) * jnp.square(g)
    m_hat = m / (1.0 - ADAM_B1 ** ADAM_STEP)
    v_hat = v / (1.0 - ADAM_B2 ** ADAM_STEP)
    delta = -ADAM_LR * (m_hat / (jnp.sqrt(v_hat) + ADAM_EPS) + ADAM_WD * w)
    return delta, m, v


def _slab_sum(parts, name):
    n, r, cdim = parts.shape
    rb = _slab_block(r)
    blk = pl.BlockSpec((rb, cdim), lambda i: (i, 0))

    def body(p_ref, g_out):
        g = p_ref[0].astype(F32)
        for j in range(1, n):
            g = g + p_ref[j].astype(F32)
        g_out[...] = g

    return pl.pallas_call(
        body, name=name, grid=(r // rb,),
        in_specs=[pl.BlockSpec((n, rb, cdim), lambda i: (0, i, 0))], out_specs=blk,
        out_shape=jax.ShapeDtypeStruct((r, cdim), F32), compiler_params=_cparams(("arbitrary",)),
    )(parts)


def _adamw(g, w, m, v, name):
    r, cdim = w.shape
    rb = _tile(r, 256)
    blk = pl.BlockSpec((rb, cdim), lambda i: (i, 0))

    def body(g_ref, w_ref, m_ref, v_ref, d_out, m_out, v_out):
        d_out[...], m_out[...], v_out[...] = _adamw_math(w_ref[...], g_ref[...], m_ref[...], v_ref[...])

    return pl.pallas_call(
        body, name=name, grid=(r // rb,), in_specs=[blk] * 4, out_specs=[blk] * 3,
        out_shape=[jax.ShapeDtypeStruct((r, cdim), F32)] * 3, compiler_params=_cparams(("arbitrary",)),
    )(g, w, m, v)


def _adamw_small(parts, w, m, v):
    def body(p_ref, w_ref, m_ref, v_ref, g_out, d_out, m_out, v_out, loss_out):
        def total(srcs):
            acc = None
            for r in srcs:
                for j in range(N_DEV):
                    term = p_ref[j, r:r + 1, :]
                    acc = term if acc is None else acc + term
            return acc

        for prow, srcs in enumerate(SMALL_SOURCES):
            one = slice(prow, prow + 1)
            g = total(srcs)
            g_out[one, :] = g
            d_out[one, :], m_out[one, :], v_out[one, :] = _adamw_math(w_ref[one, :], g, m_ref[one, :], v_ref[one, :])
        loss_out[...] = jnp.broadcast_to(jnp.sum(total(LOSS_SOURCE), axis=1, keepdims=True), (8, LANES))

    full = lambda shp: pl.BlockSpec(shp, lambda i: (0,) * len(shp))
    return pl.pallas_call(
        body, name="adamw_small", grid=(1,),
        in_specs=[full((N_DEV, SMALL_ROWS, D)), full((16, D)), full((16, D)), full((16, D))],
        out_specs=[full((16, D))] * 4 + [full((8, LANES))],
        out_shape=[jax.ShapeDtypeStruct((16, D), F32)] * 4 + [jax.ShapeDtypeStruct((8, LANES), F32)],
        compiler_params=_cparams(("arbitrary",)),
    )(parts, w, m, v)


def _to_slab(shard, form):
    if form == "N":
        return shard
    return shard.T if form == "T" else shard.T.reshape(-1, D)


def _from_slab(block, form, shard_shape):
    if form == "N":
        return block
    return block.T if form == "T" else block.reshape(shard_shape[1], shard_shape[0]).T


def _gathered_full(g, form, shard_shape):
    if form == "TR":
        return g.reshape(N_DEV * shard_shape[1], shard_shape[0])
    return g.reshape(N_DEV * g.shape[1], D)


def _pack_slab(shards, layout):
    return jnp.concatenate([jnp.zeros((r, D), BF16) if n is None else _to_slab(shards[n], form).astype(BF16)
                            for n, r, form in layout], axis=0)


def _unpack_gathered(gathered, layout):
    out, off = {}, 0
    for n, r, form in layout:
        if n is not None:
            out[n] = _gathered_full(gathered[:, off:off + r], form, SHARD_SHAPES[n])
        off += r
    return out


def _pack_per_device(gw, layout):
    return jnp.concatenate([jnp.zeros((N_DEV, r, D), BF16) if n is None else gw[n].reshape(N_DEV, r, D)
                            for n, r, _ in layout], axis=1)


def _adamw_shards(gslab, layout, wts, mom, var):
    out, off = {}, 0
    for n, r, form in layout:
        if n is not None:
            g = _from_slab(gslab[off:off + r], form, SHARD_SHAPES[n])
            out[n] = (g,) + tuple(_adamw(g, wts[n], mom[n], var[n], "adamw_" + n))
        off += r
    return out


def _sum_over_cores(per_dev, tag):
    my_c = lax.axis_index("c")
    pairs = per_dev.reshape(4, 2, per_dev.shape[1], D)
    keep = lax.dynamic_index_in_dim(pairs, my_c, axis=1, keepdims=False)
    give = lax.dynamic_index_in_dim(pairs, 1 - my_c, axis=1, keepdims=False)
    return _pair_sum(keep, _sibling_exchange(give, "rs_sibling_exchange_" + tag), "rs_pair_sum_" + tag)


def _small_pack(vals):
    rows = []
    for n in SMALL:
        v = vals[n].reshape(-1)
        k = -(-v.shape[0] // D)
        rows.append(jnp.pad(v, (0, k * D - v.shape[0])).reshape(k, D))
    return jnp.concatenate(rows, axis=0)


def _small_unpack(packed, shapes):
    out = {}
    for n in SMALL:
        k = shapes[n][-1]
        r0 = SMALL_ROW[n]
        out[n] = packed[r0:r0 + -(-k // D)].reshape(-1)[:k].reshape(shapes[n])
    return out


def kernel(x, c, positions, w_ada, b_ada, norm1_g, w_in, b_merge, gla_w_alpha, gla_b_alpha, gla_out_norm_g, gla_w_o, mla_q_lat_g, mla_w_uq, mla_kv_lat_g, mla_w_ukv, mla_qn_g, mla_kn_g, mla_w_o, w_out, norm2_g, mlp_w1, mlp_w2, loss_target, m_w_ada, m_b_ada, m_norm1_g, m_w_in, m_b_merge, m_gla_w_alpha, m_gla_b_alpha, m_gla_out_norm_g, m_gla_w_o, m_mla_q_lat_g, m_mla_w_uq, m_mla_kv_lat_g, m_mla_w_ukv, m_mla_qn_g, m_mla_kn_g, m_mla_w_o, m_w_out, m_norm2_g, m_mlp_w1, m_mlp_w2, v_w_ada, v_b_ada, v_norm1_g, v_w_in, v_b_merge, v_gla_w_alpha, v_gla_b_alpha, v_gla_out_norm_g, v_gla_w_o, v_mla_q_lat_g, v_mla_w_uq, v_mla_kv_lat_g, v_mla_w_ukv, v_mla_qn_g, v_mla_kn_g, v_mla_w_o, v_w_out, v_norm2_g, v_mlp_w1, v_mlp_w2):
    args = dict(locals())
    wts = {n: args[n][0] for n in WEIGHTS}
    mom = {n: args["m_" + n][0] for n in WEIGHTS}
    var = {n: args["v_" + n][0] for n in WEIGHTS}
    my_c = lax.axis_index("c")
    my_dev = 4 * lax.axis_index("x") + 2 * lax.axis_index("y") + my_c
    bsz = x.shape[0]
    sp = {n: wts[n].reshape(1, -1) for n in SMALL}

    gathered_a, c_gathered = _all_gather([_pack_slab(wts, SLAB_A), jnp.pad(c, ((0, 8 - bsz), (0, 0)))], "weights_all_gather")
    wt = _unpack_gathered(gathered_a, SLAB_A)

    c_all = c_gathered[:, :bsz].reshape(N_DEV * bsz, D)
    bias = lax.dynamic_slice_in_dim(sp["b_ada"], my_dev * ADA_COLS, ADA_COLS, axis=1)
    mod_cols = _mm(c_all, wts["w_ada"], "nn", (F32,), "ada_fwd", pro=_silu, epi=lambda acc, b: (acc + b,),
                   extras=(jnp.broadcast_to(bias, (N_DEV * bsz, ADA_COLS)),))
    mod_all, = _all_gather([mod_cols], "mod_all_gather")
    mod_mine = lax.dynamic_slice_in_dim(mod_all, my_dev * bsz, bsz, axis=1)
    mod3 = jnp.transpose(mod_mine, (1, 0, 2)).reshape(bsz, 6, D)

    grad_x, parts_a, parts_b, rows = _local_step(x, positions, loss_target, wt, _pack_slab(wts, SLAB_B), sp, mod3)

    big = dict(_adamw_shards(_slab_sum(parts_a, "rs_slab_sum_a"), SLAB_A, wts, mom, var),
               **_adamw_shards(_slab_sum(parts_b, "rs_slab_sum_b"), SLAB_B, wts, mom, var))

    order = ["dmod", "norm1_g", "norm2_g", "b_merge", "gla_b_alpha", "gla_out_norm_g", "mla_q_lat_g", "mla_kv_lat_g",
             "mla_qn_g", "mla_kn_g", "loss"]
    part_rows = jnp.concatenate([rows[n] for n in order], axis=0)
    part_rows = jnp.pad(part_rows, ((0, SMALL_ROWS - part_rows.shape[0]), (0, 0)))
    all_rows, = _all_gather([part_rows], "partials_all_gather")

    dmod_all = all_rows[:, :6 * bsz].reshape(N_DEV * bsz, 6 * D)
    dmod_cols = lax.dynamic_slice_in_dim(dmod_all, my_dev * ADA_COLS, ADA_COLS, axis=1)
    g_ada = _mm(c_all, dmod_cols, "tn", (F32,), "ada_dw", pro=_silu)
    big["w_ada"] = (g_ada,) + tuple(_adamw(g_ada, wts["w_ada"], mom["w_ada"], var["w_ada"], "adamw_w_ada"))

    small = _adamw_small(all_rows, _small_pack({n: wts[n] for n in SMALL}), _small_pack({n: mom[n] for n in SMALL}),
                         _small_pack({n: var[n] for n in SMALL}))
    loss = small[4][0, 0]
    small_shapes = {n: wts[n].shape for n in SMALL}
    small = [_small_unpack(o, small_shapes) for o in small[:4]]

    outs = [loss, grad_x]
    for k in range(4):
        for n in WEIGHTS:
            val = big[n][k] if n in BIG else small[k][n]
            outs.append(val.reshape((1,) + tuple(wts[n].shape)))
    return tuple(outs)
```

```python
import jax
import jax.numpy as jnp
from jax import lax
from jax.experimental import pallas as pl
from jax.experimental.pallas import tpu as pltpu

F32 = jnp.float32
BF16 = jnp.bfloat16
MESH = pl.DeviceIdType.MESH

D = 1024
EPS = 1e-6
CHUNK = 64
GH, GDK, GDV, GLR, GTAU = 4, 128, 256, 16, 16.0
MH, MQR, MKVR, NOPE, ROPE, MV = 16, 256, 128, 64, 32, 64
MQK = NOPE + ROPE
HP = 128
FF = 4 * D
ROPE_THETA = 10000.0
IN_WIDTH = 5552
PW = 5632
N_DEV = 8
LANES = 128
SLAB_BLOCK_MAX = 400
ADA_COLS = 6 * D // N_DEV
SMALL_ROWS = 32
SMALL_SOURCES = tuple([(r, 6 + r) for r in range(6)] + [(12,), (13,), (14,), (15,), (16,), (17, 18, 19, 20),
                                                         (21,), (22,), (23,), (24,)])
LOSS_SOURCE = (25,)
VMEM_LIMIT = 56 * 1024 * 1024

ADAM_LR, ADAM_B1, ADAM_B2, ADAM_EPS, ADAM_WD, ADAM_STEP = 0.001, 0.9, 0.999, 1e-08, 0.01, 10

SLAB_A = (("w_in", 694, "T"), ("gla_w_alpha", 1, "TR"), (None, 9, None), ("mla_w_uq", 48, "TR"), ("mla_w_ukv", 32, "TR"))
SLAB_B = (("mlp_w1", 512, "T"), ("gla_w_o", 128, "N"), ("mla_w_o", 128, "N"), ("w_out", 128, "N"), ("mlp_w2", 512, "N"))
BIG = ("w_ada",) + tuple(n for n, _, _ in SLAB_A + SLAB_B if n is not None)
SHARD_SHAPES = {"w_ada": (D, 6 * D // N_DEV), "w_in": (D, IN_WIDTH // N_DEV), "gla_w_alpha": (GLR, GH * GDK // N_DEV),
                "gla_w_o": (GH * GDV // N_DEV, D), "mla_w_uq": (MQR, MH * MQK // N_DEV),
                "mla_w_ukv": (MKVR, MH * (NOPE + MV) // N_DEV), "mla_w_o": (MH * MV // N_DEV, D), "w_out": (D // N_DEV, D),
                "mlp_w1": (D, FF // N_DEV), "mlp_w2": (FF // N_DEV, D)}
SMALL = ("b_ada", "norm1_g", "norm2_g", "b_merge", "gla_b_alpha", "gla_out_norm_g", "mla_q_lat_g", "mla_kv_lat_g",
         "mla_qn_g", "mla_kn_g")
SMALL_ROW = {"b_ada": 0, "norm1_g": 6, "norm2_g": 7, "b_merge": 8, "gla_b_alpha": 10, "gla_out_norm_g": 11,
             "mla_q_lat_g": 12, "mla_kv_lat_g": 13, "mla_qn_g": 14, "mla_kn_g": 15}
WEIGHTS = ("w_ada", "b_ada", "norm1_g", "w_in", "b_merge", "gla_w_alpha", "gla_b_alpha", "gla_out_norm_g", "gla_w_o",
           "mla_q_lat_g", "mla_w_uq", "mla_kv_lat_g", "mla_w_ukv", "mla_qn_g", "mla_kn_g", "mla_w_o", "w_out",
           "norm2_g", "mlp_w1", "mlp_w2")


def _cparams(sem=None):
    return pltpu.CompilerParams(dimension_semantics=sem, vmem_limit_bytes=VMEM_LIMIT)


def _tile(n, pref):
    for t in (2048, PW // 4, 1024, 512, 256, 128):
        if t <= pref and n % t == 0:
            return t
    return n


def _dot(a, b, dims, precision=None):
    return lax.dot_general(a, b, (dims, ((), ())), preferred_element_type=F32, precision=precision)


NN = ((1,), (0,))
NT = ((1,), (1,))
TN = ((0,), (0,))


def _sigmoid(x):
    return 1.0 / (1.0 + jnp.exp(-x))


def _silu(x):
    return x * _sigmoid(x)


def _mm(a, b, mode, out_dtypes, name, *, pro=None, pro_b=None, epi=None, extras=(), a_off=0, m=None, tm=2048, tn=1024,
        tk=1024, cargo=None, tail=None):
    if mode == "tn":
        kc, n = b.shape
        m = a.shape[1] if m is None else m
    elif mode == "nn":
        m, kc = a.shape
        n = b.shape[1]
    else:
        m, kc = a.shape
        n = b.shape[0]
    tm, tn, tk = _tile(m, tm), _tile(n, tn), _tile(kc, tk)
    nk = kc // tk
    dims = {"nn": NN, "nt": NT, "tn": TN}[mode]
    if mode == "tn":
        a_spec = pl.BlockSpec((tk, tm), lambda i, j, k: (k, i + a_off))
    else:
        a_spec = pl.BlockSpec((tm, tk), lambda i, j, k: (i + a_off, k))
    if mode == "nt":
        b_spec = pl.BlockSpec((tn, tk), lambda i, j, k: (j, k))
    else:
        b_spec = pl.BlockSpec((tk, tn), lambda i, j, k: (k, j))
    o_spec = pl.BlockSpec((tm, tn), lambda i, j, k: (i, j))
    n_ex, n_out = len(extras), len(out_dtypes)
    grid = (m // tm, n // tn, nk)
    has_cargo = cargo is not None
    t_ins, t_outs = (tail["ins"], tail["outs"]) if tail else ([], [])
    assert not tail or grid[1] == 1

    def body(a_ref, b_ref, *rest):
        rest = list(rest)
        take = lambda count: [rest.pop(0) for _ in range(count)]
        ex, cargo_ref, tail_in = take(n_ex), take(has_cargo), take(len(t_ins))
        outs, parts_ref, tail_out = take(n_out), take(has_cargo), take(len(t_outs))
        acc = rest.pop(0)
        steps = [pl.program_id(axis) for axis in range(3)]
        if has_cargo:
            exchange = _chip_exchange_copies(cargo_ref[0], parts_ref[0], *rest)
            first = (steps[0] == 0) & (steps[1] == 0) & (steps[2] == 0)
            last = (steps[0] == grid[0] - 1) & (steps[1] == grid[1] - 1) & (steps[2] == grid[2] - 1)
            pl.when(first)(exchange.start)
        k = steps[2]

        @pl.when(k == 0)
        def _():
            acc[...] = jnp.zeros_like(acc)

        av = a_ref[...]
        if pro is not None:
            av = pro(av)
        bv = b_ref[...]
        if pro_b is not None:
            bv = pro_b(bv)
        acc[...] += _dot(av.astype(BF16), bv.astype(BF16), dims)

        @pl.when(k == nk - 1)
        def _():
            if tail:
                tail["fn"](acc[...], steps[0], *tail_in, *tail_out)
            res = (acc[...],) if epi is None else epi(acc[...], *[e[...] for e in ex])
            for o_ref, r in zip(outs, res):
                o_ref[...] = r.astype(o_ref.dtype)

        if has_cargo:
            pl.when(last)(exchange.finish)

    cargo_in = [cargo] if has_cargo else []
    cargo_spec = [HBM_SPEC] * len(cargo_in)
    sequential = has_cargo or bool(tail)
    out = pl.pallas_call(
        body, name=name, grid=grid,
        in_specs=[a_spec, b_spec] + [o_spec] * n_ex + cargo_spec + (tail["in_specs"] if tail else []),
        out_specs=[o_spec] * n_out + cargo_spec + (tail["out_specs"] if tail else []),
        out_shape=[jax.ShapeDtypeStruct((m, n), dt) for dt in out_dtypes]
        + [jax.ShapeDtypeStruct(c.shape, c.dtype) for c in cargo_in] + t_outs,
        scratch_shapes=[pltpu.VMEM((tm, tn), F32)] + (EXCHANGE_SEMS if has_cargo else []),
        compiler_params=_cparams(("arbitrary",) * 3 if sequential else ("parallel", "parallel", "arbitrary")),
    )(a, b, *extras, *cargo_in, *t_ins)
    return out[0] if len(out) == 1 else out


def _rows(s):
    return _tile(s, 512)


def _mod_spec():
    return pl.BlockSpec((1, 6, D), lambda b, i: (b, 0, 0))


def _tok_spec(tr, nb, width=D, col=0):
    return pl.BlockSpec((tr, width), lambda b, i: (b * nb + i, col))


def _modulated_norm(xv, gv, mod_ref, i_shift, i_scale):
    r = lax.rsqrt(jnp.mean(xv * xv, axis=1, keepdims=True) + EPS)
    return ((xv * r) * gv) * (1.0 + mod_ref[0, i_scale:i_scale + 1, :]) + mod_ref[0, i_shift:i_shift + 1, :]


def _norm_mod_fwd(x, g, mod3, i_shift, i_scale, name):
    bsz, _, _ = mod3.shape
    t = x.shape[0]
    s = t // bsz
    tr = _rows(s)
    nb = s // tr

    def body(x_ref, g_ref, mod_ref, h_ref):
        h_ref[...] = _modulated_norm(x_ref[...], g_ref[...], mod_ref, i_shift, i_scale).astype(BF16)

    tok = _tok_spec(tr, nb)
    return pl.pallas_call(
        body, name=name, grid=(bsz, nb), in_specs=[tok, pl.BlockSpec((1, D), lambda b, i: (0, 0)), _mod_spec()],
        out_specs=tok, out_shape=jax.ShapeDtypeStruct((t, D), BF16), compiler_params=_cparams(("arbitrary", "arbitrary")),
    )(x, g, mod3)


def _norm_fwd_tail(x, g, mod3, i_shift, i_scale, i_gate, tm):
    bsz, t = mod3.shape[0], x.shape[0]
    per_b = t // bsz // tm

    def fn(mixedv, i, x_ref, g_ref, mod_ref, mixed_ref, x1_ref, h_ref):
        mixed_ref[...] = mixedv.astype(BF16)
        xv = x_ref[...] + mod_ref[0, i_gate:i_gate + 1, :] * mixedv
        x1_ref[...] = xv
        h_ref[...] = _modulated_norm(xv, g_ref[...], mod_ref, i_shift, i_scale).astype(BF16)

    tok = pl.BlockSpec((tm, D), lambda i, j, k: (i, 0))
    return dict(
        fn=fn, ins=[x, g, mod3], in_specs=[tok, pl.BlockSpec((1, D), lambda i, j, k: (0, 0)), _tail_mod_spec(per_b)],
        outs=[jax.ShapeDtypeStruct((t, D), BF16), jax.ShapeDtypeStruct((t, D), F32), jax.ShapeDtypeStruct((t, D), BF16)],
        out_specs=[tok, tok, tok])


def _norm_bwd_rows(xv, dhv, dresv, gv, mod_ref, i_scale, accb, accg):
    r = lax.rsqrt(jnp.mean(xv * xv, axis=1, keepdims=True) + EPS)
    xn = xv * r
    accb[0, 0:1, :] += jnp.sum(dhv, axis=0, keepdims=True)
    accb[0, 1:2, :] += jnp.sum(dhv * (xn * gv), axis=0, keepdims=True)
    tt = dhv * (1.0 + mod_ref[0, i_scale:i_scale + 1, :])
    accg[0:1, :] += jnp.sum(tt * xn, axis=0, keepdims=True)
    dxn = tt * gv
    return dresv + r * (dxn - xn * jnp.mean(dxn * xn, axis=1, keepdims=True))


def _norm_bwd_tail(x, dres, g, mod3, i_scale, tm, mixed=None, i_gate=None):
    bsz, t = mod3.shape[0], x.shape[0]
    per_b = t // bsz // tm
    has_res = mixed is not None

    def fn(dhv, i, *refs):
        if has_res:
            x_ref, dres_ref, mx_ref, g_ref, mod_ref, dx_ref, dmx_ref, accb, accg = refs
        else:
            x_ref, dres_ref, g_ref, mod_ref, dx_ref, accb, accg = refs

        @pl.when(i % per_b == 0)
        def _():
            accb[...] = jnp.zeros_like(accb)

        @pl.when(i == 0)
        def _():
            accg[...] = jnp.zeros_like(accg)

        dx = _norm_bwd_rows(x_ref[...], dhv, dres_ref[...], g_ref[...], mod_ref, i_scale, accb, accg)
        dx_ref[...] = dx
        if has_res:
            accb[0, 2:3, :] += jnp.sum(dx * mx_ref[...].astype(F32), axis=0, keepdims=True)
            dmx_ref[...] = (dx * mod_ref[0, i_gate:i_gate + 1, :]).astype(BF16)

    tok = pl.BlockSpec((tm, D), lambda i, j, k: (i, 0))
    res = [mixed] if has_res else []
    return dict(
        fn=fn, ins=[x, dres] + res + [g, mod3],
        in_specs=[tok, tok] + [tok] * len(res) + [pl.BlockSpec((1, D), lambda i, j, k: (0, 0)), _tail_mod_spec(per_b)],
        outs=[jax.ShapeDtypeStruct((t, D), F32)] + [jax.ShapeDtypeStruct((t, D), BF16)] * len(res)
        + [jax.ShapeDtypeStruct((bsz, 8, D), F32), jax.ShapeDtypeStruct((8, D), F32)],
        out_specs=[tok] * (1 + len(res)) + [_tail_batch_spec(per_b), pl.BlockSpec((8, D), lambda i, j, k: (0, 0))])


def _tail_mod_spec(per_b):
    return pl.BlockSpec((1, 6, D), lambda i, j, k: (i // per_b, 0, 0))


def _tail_batch_spec(per_b):
    return pl.BlockSpec((1, 8, D), lambda i, j, k: (i // per_b, 0, 0))


def _loss_tail(x1, tgt, mod3, tm):
    bsz, t = mod3.shape[0], x1.shape[0]
    per_b = t // bsz // tm

    def fn(ffv, i, x1_ref, tg_ref, mod_ref, dy_ref, dff_ref, accb, accl):
        @pl.when(i % per_b == 0)
        def _():
            accb[...] = jnp.zeros_like(accb)

        @pl.when(i == 0)
        def _():
            accl[...] = jnp.zeros_like(accl)

        gate = mod_ref[0, 5:6, :]
        err = x1_ref[...] + gate * ffv - tg_ref[...]
        accl[0:1, :] += jnp.sum(err * err, axis=0, keepdims=True) * (0.5 / D)
        dy = err * (1.0 / D)
        dy_ref[...] = dy
        dff_ref[...] = (dy * gate).astype(BF16)
        accb[0, 0:1, :] += jnp.sum(dy * ffv, axis=0, keepdims=True)

    tok = pl.BlockSpec((tm, D), lambda i, j, k: (i, 0))
    return dict(
        fn=fn, ins=[x1, tgt, mod3], in_specs=[tok, tok, _tail_mod_spec(per_b)],
        outs=[jax.ShapeDtypeStruct((t, D), F32), jax.ShapeDtypeStruct((t, D), BF16),
              jax.ShapeDtypeStruct((bsz, 8, D), F32), jax.ShapeDtypeStruct((8, D), F32)],
        out_specs=[tok, tok, _tail_batch_spec(per_b), pl.BlockSpec((8, D), lambda i, j, k: (0, 0))])


def _merge_fwd(proj, b_merge, y_a, y_b):
    t = proj.shape[0]
    tr = _tile(t, 512)

    def body(la_ref, lb_ref, bm_ref, ya_ref, yb_ref, mix_ref):
        ga = _sigmoid(la_ref[...] + bm_ref[:, 0:D])
        gb = _sigmoid(lb_ref[...] + bm_ref[:, D:2 * D])
        mix_ref[...] = (ga * ya_ref[...].astype(F32) + gb * yb_ref[...].astype(F32)).astype(BF16)

    tok = pl.BlockSpec((tr, D), lambda i: (i, 0))
    return pl.pallas_call(
        body, name="merge_fwd", grid=(t // tr,),
        in_specs=[pl.BlockSpec((tr, D), lambda i: (i, 3)), pl.BlockSpec((tr, D), lambda i: (i, 4)),
                  pl.BlockSpec((1, 2 * D), lambda i: (0, 0)), tok, tok],
        out_specs=tok, out_shape=jax.ShapeDtypeStruct((t, D), BF16),
        compiler_params=_cparams(("arbitrary",)),
    )(proj, proj, b_merge, y_a, y_b)


def _merge_bwd_tail(proj, b_merge, y_a, y_b, tm):
    t = proj.shape[0]

    def fn(dm, i, la_ref, lb_ref, bm_ref, ya_ref, yb_ref, dya_ref, dyb_ref, dl_ref, acc):
        @pl.when(i == 0)
        def _():
            acc[...] = jnp.zeros_like(acc)

        ga = _sigmoid(la_ref[...] + bm_ref[:, 0:D])
        gb = _sigmoid(lb_ref[...] + bm_ref[:, D:2 * D])
        dya_ref[...] = (dm * ga).astype(BF16)
        dyb_ref[...] = (dm * gb).astype(BF16)
        dla = dm * ya_ref[...].astype(F32) * ga * (1.0 - ga)
        dlb = dm * yb_ref[...].astype(F32) * gb * (1.0 - gb)
        dl_ref[:, 0:D] = dla.astype(BF16)
        dl_ref[:, D:2 * D] = dlb.astype(BF16)
        acc[0:1, 0:D] += jnp.sum(dla, axis=0, keepdims=True)
        acc[0:1, D:2 * D] += jnp.sum(dlb, axis=0, keepdims=True)

    tok = pl.BlockSpec((tm, D), lambda i, j, k: (i, 0))
    return dict(
        fn=fn, ins=[proj, proj, b_merge, y_a, y_b],
        in_specs=[pl.BlockSpec((tm, D), lambda i, j, k: (i, 3)), pl.BlockSpec((tm, D), lambda i, j, k: (i, 4)),
                  pl.BlockSpec((1, 2 * D), lambda i, j, k: (0, 0)), tok, tok],
        outs=[jax.ShapeDtypeStruct((t, D), BF16), jax.ShapeDtypeStruct((t, D), BF16),
              jax.ShapeDtypeStruct((t, 2 * D), BF16), jax.ShapeDtypeStruct((8, 2 * D), F32)],
        out_specs=[tok, tok, pl.BlockSpec((tm, 2 * D), lambda i, j, k: (i, 0)), pl.BlockSpec((8, 2 * D), lambda i, j, k: (0, 0))])


GLA_HEADS = 2
GLA_UNROLL = 16


def _log_sigmoid(z):
    return jnp.minimum(z, 0.0) - jnp.log(1.0 + jnp.exp(-jnp.abs(z)))


def _tri(lower):
    r = lax.broadcasted_iota(jnp.int32, (CHUNK, CHUNK), 0)
    c = lax.broadcasted_iota(jnp.int32, (CHUNK, CHUNK), 1)
    return jnp.where(r >= c if lower else r <= c, 1.0, 0.0).astype(F32)


def _gla_fwd(proj, wa_pad, b_alpha, g_out, bsz):
    t = proj.shape[0]
    s = t // bsz
    nc = s // CHUNK
    p, kw, vw = GLA_HEADS, GLA_HEADS * GDK, GLA_HEADS * GDV

    def body(q_ref, k_ref, v_ref, gg_ref, ms_ref, wa_ref, ba_ref, go_ref, o_ref, og_ref, st_ref, la, state):
        z = _dot(ms_ref[...].astype(BF16), wa_ref[...], NN) + ba_ref[...]
        la[...] = _log_sigmoid(z) * (1.0 / GTAU)
        state[...] = jnp.zeros_like(state)
        low = _tri(True)
        gout = go_ref[...]

        def chunk(n, carry):
            rows = pl.ds(pl.multiple_of(n * CHUNK, CHUNK), CHUNK)
            for hh in range(p):
                kc, vc = slice(hh * GDK, (hh + 1) * GDK), slice(hh * GDV, (hh + 1) * GDV)
                lac = la[rows, kc]
                cum = _dot(low, lac, NN, lax.Precision.HIGHEST)
                ce = jnp.sum(lac, axis=0, keepdims=True)
                kd = (k_ref[rows, kc].astype(F32) * jnp.exp(ce - cum)).astype(BF16)
                new = state[vc, :] * jnp.exp(ce) + _dot(v_ref[rows, vc].astype(BF16), kd, TN)
                state[vc, :] = new
                st_ref[pl.ds(pl.multiple_of((hh * nc + n) * GDV, GDV), GDV), :] = new.astype(BF16)
                qs = (q_ref[rows, kc].astype(F32) * (GDK ** -0.5)).astype(BF16)
                o = _dot(qs, new.astype(BF16), NT)
                o_ref[rows, vc] = o
                ro = lax.rsqrt(jnp.mean(o * o, axis=1, keepdims=True) + EPS)
                og_ref[rows, vc] = (((o * ro) * gout) * _silu(gg_ref[rows, vc].astype(F32))).astype(BF16)
            return carry

        lax.fori_loop(0, nc, chunk, 0, unroll=GLA_UNROLL)

    return pl.pallas_call(
        body, name="gla_fwd", grid=(bsz, GH // p),
        in_specs=[pl.BlockSpec((s, kw), lambda b, h: (b, h)), pl.BlockSpec((s, kw), lambda b, h: (b, GH // p + h)),
                  pl.BlockSpec((s, vw), lambda b, h: (b, GH // p + h)), pl.BlockSpec((s, vw), lambda b, h: (b, 2 * GH // p + h)),
                  pl.BlockSpec((s, LANES), lambda b, h: (b, 43)),
                  pl.BlockSpec((LANES, kw), lambda b, h: (0, h)), pl.BlockSpec((1, kw), lambda b, h: (0, h)),
                  pl.BlockSpec((1, GDV), lambda b, h: (0, 0))],
        out_specs=[pl.BlockSpec((s, vw), lambda b, h: (b, h)), pl.BlockSpec((s, vw), lambda b, h: (b, h)),
                   pl.BlockSpec((p * nc * GDV, GDK), lambda b, h: (b * (GH // p) + h, 0))],
        out_shape=[jax.ShapeDtypeStruct((t, GH * GDV), F32), jax.ShapeDtypeStruct((t, GH * GDV), BF16),
                   jax.ShapeDtypeStruct((bsz * GH * nc * GDV, GDK), BF16)],
        scratch_shapes=[pltpu.VMEM((s, kw), F32), pltpu.VMEM((vw, GDK), F32)],
        compiler_params=_cparams(("arbitrary", "arbitrary")),
    )(proj, proj, proj, proj, proj, wa_pad, b_alpha, g_out)


def _gla_bwd(dog, o, proj, wa_pad, b_alpha, g_out, states, bsz):
    t = proj.shape[0]
    s = t // bsz
    nc = s // CHUNK
    p, kw, vw = GLA_HEADS, GLA_HEADS * GDK, GLA_HEADS * GDV

    def body(dog_ref, o_ref, q_ref, k_ref, v_ref, gg_ref, ms_ref, wa_ref, ba_ref, go_ref, st_ref,
             dq_ref, dk_ref, dv_ref, dgg_ref, dz_ref, dba, dgo, zs, la, carry_g):
        @pl.when(pl.program_id(1) == 0)
        def _():
            dba[...] = jnp.zeros_like(dba)
            dgo[...] = jnp.zeros_like(dgo)

        z = _dot(ms_ref[...].astype(BF16), wa_ref[...], NN) + ba_ref[...]
        zs[...] = z
        la[...] = _log_sigmoid(z) * (1.0 / GTAU)
        carry_g[...] = jnp.zeros_like(carry_g)
        low, upp = _tri(True), _tri(False)
        gout = go_ref[...]
        last_row = lax.broadcasted_iota(jnp.int32, (CHUNK, GDK), 0) == CHUNK - 1

        def chunk(step, carry):
            n = nc - 1 - step
            rows = pl.ds(pl.multiple_of(n * CHUNK, CHUNK), CHUNK)
            for hh in range(p):
                kc, vc = slice(hh * GDK, (hh + 1) * GDK), slice(hh * GDV, (hh + 1) * GDV)
                lac = la[rows, kc]
                cum = _dot(low, lac, NN, lax.Precision.HIGHEST)
                ce = jnp.sum(lac, axis=0, keepdims=True)
                e = jnp.exp(ce - cum)
                dec = jnp.exp(ce)
                kf = k_ref[rows, kc].astype(F32)
                kd = (kf * e).astype(BF16)
                vv = v_ref[rows, vc].astype(BF16)
                qs = (q_ref[rows, kc].astype(F32) * (GDK ** -0.5)).astype(BF16)
                ov = o_ref[rows, vc]
                ro = lax.rsqrt(jnp.mean(ov * ov, axis=1, keepdims=True) + EPS)
                on = ov * ro
                gg = gg_ref[rows, vc].astype(F32)
                sg = _sigmoid(gg)
                dogv = dog_ref[rows, vc].astype(F32)
                dgg_ref[rows, vc] = (dogv * (on * gout) * (sg * (1.0 + gg * (1.0 - sg)))).astype(BF16)
                t1 = dogv * (gg * sg)
                dgo[8 * hh:8 * hh + 1, :] += jnp.sum(t1 * on, axis=0, keepdims=True)
                don = t1 * gout
                do = ro * (don - on * jnp.mean(don * on, axis=1, keepdims=True))
                dob = do.astype(BF16)
                st_n = st_ref[pl.ds(pl.multiple_of((hh * nc + n) * GDV, GDV), GDV), :]
                dq_ref[rows, kc] = (_dot(dob, st_n, NN) * (GDK ** -0.5)).astype(BF16)
                dn = carry_g[vc, :] + _dot(dob, qs, TN)
                prev = hh * nc + jnp.maximum(n - 1, 0)
                st_p = st_ref[pl.ds(pl.multiple_of(prev * GDV, GDV), GDV), :].astype(F32) * jnp.where(n > 0, 1.0, 0.0)
                ddec = jnp.sum(dn * st_p, axis=0, keepdims=True)
                dnb = dn.astype(BF16)
                dkd = _dot(vv, dnb, NN)
                dv_ref[rows, vc] = _dot(kd, dnb, NT).astype(BF16)
                dk_ref[rows, kc] = (dkd * e).astype(BF16)
                w = dkd * kf * e
                dce = jnp.sum(w, axis=0, keepdims=True) + ddec * dec
                dcum = jnp.where(last_row, dce - w, -w)
                dla = _dot(upp, dcum, NN, lax.Precision.HIGHEST)
                dz = dla * (1.0 / GTAU) * _sigmoid(-zs[rows, kc])
                dba[0:1, kc] += jnp.sum(dz, axis=0, keepdims=True)
                dz_ref[rows, kc] = dz.astype(BF16)
                carry_g[vc, :] = dn * dec
            return carry

        lax.fori_loop(0, nc, chunk, 0, unroll=GLA_UNROLL)

    hv = pl.BlockSpec((s, vw), lambda h, b: (b, h))
    hk = pl.BlockSpec((s, kw), lambda h, b: (b, h))
    return pl.pallas_call(
        body, name="gla_bwd", grid=(GH // p, bsz),
        in_specs=[hv, hv, hk, pl.BlockSpec((s, kw), lambda h, b: (b, GH // p + h)),
                  pl.BlockSpec((s, vw), lambda h, b: (b, GH // p + h)), pl.BlockSpec((s, vw), lambda h, b: (b, 2 * GH // p + h)),
                  pl.BlockSpec((s, LANES), lambda h, b: (b, 43)), pl.BlockSpec((LANES, kw), lambda h, b: (0, h)),
                  pl.BlockSpec((1, kw), lambda h, b: (0, h)), pl.BlockSpec((1, GDV), lambda h, b: (0, 0)),
                  pl.BlockSpec((p * nc * GDV, GDK), lambda h, b: (b * (GH // p) + h, 0))],
        out_specs=[hk, hk, hv, hv, hk, pl.BlockSpec((8, kw), lambda h, b: (0, h)),
                   pl.BlockSpec((8 * p, GDV), lambda h, b: (h, 0))],
        out_shape=[jax.ShapeDtypeStruct((t, GH * GDK), BF16), jax.ShapeDtypeStruct((t, GH * GDK), BF16),
                   jax.ShapeDtypeStruct((t, GH * GDV), BF16), jax.ShapeDtypeStruct((t, GH * GDV), BF16),
                   jax.ShapeDtypeStruct((t, GH * GDK), BF16), jax.ShapeDtypeStruct((8, GH * GDK), F32),
                   jax.ShapeDtypeStruct((8 * GH, GDV), F32)],
        scratch_shapes=[pltpu.VMEM((s, kw), F32), pltpu.VMEM((s, kw), F32), pltpu.VMEM((vw, GDK), F32)],
        compiler_params=_cparams(("arbitrary", "arbitrary")),
    )(dog, o, proj, proj, proj, proj, proj, wa_pad, b_alpha, g_out, states)


def _rope_tables(pos_ref, fr_ref, sg_ref):
    ang = pos_ref[...].astype(F32) * fr_ref[...]
    return jnp.cos(ang), jnp.sin(ang) * sg_ref[...]


def _partner(x):
    lane = lax.broadcasted_iota(jnp.int32, x.shape, 1)
    return jnp.where(lane < NOPE + ROPE // 2, pltpu.roll(x, LANES - ROPE // 2, 1), pltpu.roll(x, ROPE // 2, 1))


def _mla_rows(t):
    return _tile(t, 512)


def _mla_pre_fwd(proj, pos, fr, sg, q_lat_g, kv_lat_g, qn_g, kn_g, wuq, wukv):
    t = proj.shape[0]
    tr = _mla_rows(t)

    def body(cq_ref, ckv_ref, ms_ref, pos_ref, fr_ref, sg_ref, qlg, kvlg, qng, kng, wuq_ref, wukv_ref, q_out, k_out, v_out):
        lane = lax.broadcasted_iota(jnp.int32, (tr, HP), 1)
        real = jnp.where(lane < MQK, 1.0, 0.0)
        cos, sin = _rope_tables(pos_ref, fr_ref, sg_ref)
        cos = cos * real
        cq = cq_ref[...].astype(F32)
        cqn = (cq * lax.rsqrt(jnp.mean(cq * cq, axis=1, keepdims=True) + EPS) * qlg[...]).astype(BF16)
        ckv = ckv_ref[...].astype(F32)
        ckvn = (ckv * lax.rsqrt(jnp.mean(ckv * ckv, axis=1, keepdims=True) + EPS) * kvlg[...]).astype(BF16)
        kpe = jnp.where((lane >= NOPE) & (lane < MQK), ms_ref[...].astype(F32), 0.0)
        kpe = kpe + jnp.where(lane < MQK + ROPE // 2, pltpu.roll(kpe, ROPE, 1), 0.0)
        lane_all = lax.broadcasted_iota(jnp.int32, (tr, MH * HP), 1)
        v_out[...] = jnp.where(lane_all % HP == MV, 1.0, _dot(ckvn, wukv_ref[:, MH * HP:], NN)).astype(BF16)

        def norm_rope(x, gain):
            xn = x * lax.rsqrt(jnp.sum(x * x * real, axis=1, keepdims=True) * (1.0 / MQK) + EPS) * gain
            return (xn * cos + pltpu.roll(xn, LANES - ROPE // 2, 1) * sin).astype(BF16)

        for h in range(MH):
            cols = slice(h * HP, (h + 1) * HP)
            q_out[:, cols] = norm_rope(_dot(cqn, wuq_ref[:, cols], NN), qng[...])
            k_out[:, cols] = norm_rope(_dot(ckvn, wukv_ref[:, cols], NN) + kpe, kng[...])

    def full(a):
        return pl.BlockSpec(a.shape, lambda i: (0, 0))

    wide = pl.BlockSpec((tr, MH * HP), lambda i: (i, 0))
    return pl.pallas_call(
        body, name="mla_pre_fwd", grid=(t // tr,),
        in_specs=[pl.BlockSpec((tr, MQR), lambda i: (i, 20)), pl.BlockSpec((tr, MKVR), lambda i: (i, 42)),
                  pl.BlockSpec((tr, LANES), lambda i: (i, 43)), pl.BlockSpec((tr, 1), lambda i: (i, 0)),
                  full(fr), full(sg), full(q_lat_g), full(kv_lat_g), full(qn_g), full(kn_g), full(wuq), full(wukv)],
        out_specs=[wide, wide, wide],
        out_shape=[jax.ShapeDtypeStruct((t, MH * HP), BF16)] * 3,
        compiler_params=_cparams(("arbitrary",)),
    )(proj, proj, proj, pos, fr, sg, q_lat_g, kv_lat_g, qn_g, kn_g, wuq, wukv)


def _mla_pre_bwd(dq2, dk2, dv2, dmisc_gla, proj, pos, fr, sg, q_lat_g, kv_lat_g, qn_g, kn_g, wuq, wukv):
    t = proj.shape[0]
    tr = _mla_rows(t)

    def body(dq_ref, dk_ref, dv_ref, dmg_ref, cq_ref, ckv_ref, ms_ref, pos_ref, fr_ref, sg_ref, qlg, kvlg, qng, kng,
             wuq_ref, wukv_ref, dcq_ref, dckv_ref, dms_ref, dwuq, dwukv, acc, dqf, dkvf):
        @pl.when(pl.program_id(0) == 0)
        def _():
            dwuq[...] = jnp.zeros_like(dwuq)
            dwukv[...] = jnp.zeros_like(dwukv)
            acc[...] = jnp.zeros_like(acc)

        cos, sin = _rope_tables(pos_ref, fr_ref, sg_ref)
        cq = cq_ref[...].astype(F32)
        rc = lax.rsqrt(jnp.mean(cq * cq, axis=1, keepdims=True) + EPS)
        xc = cq * rc
        cqn = (xc * qlg[...]).astype(BF16)
        ckv = ckv_ref[...].astype(F32)
        rkv = lax.rsqrt(jnp.mean(ckv * ckv, axis=1, keepdims=True) + EPS)
        xkv = ckv * rkv
        ckvn = (xkv * kvlg[...]).astype(BF16)
        lane = lax.broadcasted_iota(jnp.int32, (tr, HP), 1)
        is_rope = (lane >= NOPE) & (lane < MQK)
        kpe = jnp.where(is_rope, ms_ref[...].astype(F32), 0.0)
        dkpe = jnp.zeros((tr, HP), F32)
        dqng = jnp.zeros((1, HP), F32)
        dkng = jnp.zeros((1, HP), F32)
        for h in range(MH):
            cols = slice(h * HP, (h + 1) * HP)
            qh = _dot(cqn, wuq_ref[:, cols], NN)
            rq = lax.rsqrt(jnp.sum(qh * qh, axis=1, keepdims=True) * (1.0 / MQK) + EPS)
            xq = qh * rq
            dy = dq_ref[:, cols].astype(F32)
            dqn = dy * cos - _partner(dy) * sin
            dqng += jnp.sum(dqn * xq, axis=0, keepdims=True)
            tq = dqn * qng[...]
            dqf[:, cols] = (rq * (tq - xq * (jnp.sum(tq * xq, axis=1, keepdims=True) * (1.0 / MQK)))).astype(BF16)
            kh = _dot(ckvn, wukv_ref[:, cols], NN) + kpe
            rk = lax.rsqrt(jnp.sum(kh * kh, axis=1, keepdims=True) * (1.0 / MQK) + EPS)
            xk = kh * rk
            dy = dk_ref[:, cols].astype(F32)
            dkn = dy * cos - _partner(dy) * sin
            dkng += jnp.sum(dkn * xk, axis=0, keepdims=True)
            tk = dkn * kng[...]
            dkh = rk * (tk - xk * (jnp.sum(tk * xk, axis=1, keepdims=True) * (1.0 / MQK)))
            dkvf[:, cols] = jnp.where(lane < NOPE, dkh, 0.0).astype(BF16)
            dkpe += jnp.where(is_rope, dkh, 0.0)
        dkvf[:, MH * HP:] = dv_ref[...]
        acc[2:3, 0:HP] += dqng
        acc[3:4, 0:HP] += dkng
        dms_ref[...] = (dmg_ref[...] + dkpe).astype(BF16)

        dqfv = dqf[...]
        dwuq[...] += _dot(cqn, dqfv, TN)
        dcqn = _dot(dqfv, wuq_ref[...], NT)
        acc[0:1, :] += jnp.sum(dcqn * xc, axis=0, keepdims=True)
        tc = dcqn * qlg[...]
        dcq_ref[...] = (rc * (tc - xc * jnp.mean(tc * xc, axis=1, keepdims=True))).astype(BF16)

        dkvfv = dkvf[...]
        dwukv[...] += _dot(ckvn, dkvfv, TN)
        dckvn = _dot(dkvfv, wukv_ref[...], NT)
        acc[1:2, 0:MKVR] += jnp.sum(dckvn * xkv, axis=0, keepdims=True)
        tkv = dckvn * kvlg[...]
        dckv_ref[...] = (rkv * (tkv - xkv * jnp.mean(tkv * xkv, axis=1, keepdims=True))).astype(BF16)

    def full(a):
        return pl.BlockSpec(a.shape, lambda i: (0, 0))

    wide = pl.BlockSpec((tr, MH * HP), lambda i: (i, 0))
    narrow = pl.BlockSpec((tr, LANES), lambda i: (i, 0))
    return pl.pallas_call(
        body, name="mla_pre_bwd", grid=(t // tr,),
        in_specs=[wide, wide, wide, narrow,
                  pl.BlockSpec((tr, MQR), lambda i: (i, 20)), pl.BlockSpec((tr, MKVR), lambda i: (i, 42)),
                  pl.BlockSpec((tr, LANES), lambda i: (i, 43)), pl.BlockSpec((tr, 1), lambda i: (i, 0)),
                  full(fr), full(sg), full(q_lat_g), full(kv_lat_g), full(qn_g), full(kn_g), full(wuq), full(wukv)],
        out_specs=[pl.BlockSpec((tr, MQR), lambda i: (i, 0)), narrow, narrow,
                   pl.BlockSpec((MQR, MH * HP), lambda i: (0, 0)), pl.BlockSpec((MKVR, 2 * MH * HP), lambda i: (0, 0)),
                   pl.BlockSpec((8, MQR), lambda i: (0, 0))],
        out_shape=[jax.ShapeDtypeStruct((t, MQR), BF16), jax.ShapeDtypeStruct((t, MKVR), BF16),
                   jax.ShapeDtypeStruct((t, LANES), BF16), jax.ShapeDtypeStruct((MQR, MH * HP), F32),
                   jax.ShapeDtypeStruct((MKVR, 2 * MH * HP), F32), jax.ShapeDtypeStruct((8, MQR), F32)],
        scratch_shapes=[pltpu.VMEM((tr, MH * HP), BF16), pltpu.VMEM((tr, 2 * MH * HP), BF16)],
        compiler_params=_cparams(("arbitrary",)),
    )(dq2, dk2, dv2, dmisc_gla, proj, proj, proj, pos, fr, sg, q_lat_g, kv_lat_g, qn_g, kn_g, wuq, wukv)


ATT_FWD_TILES = (1024, 512)
ATT_BWD_TILES = (512, 512)
ATT_HEADS = 2
NEG = -1e30
LOG2E = 1.4426950408889634


def _att_mask(q0, k0, tq, tk):
    qc = (q0 + lax.broadcasted_iota(jnp.int32, (tq, tk), 0)) // CHUNK
    kc = (k0 + lax.broadcasted_iota(jnp.int32, (tq, tk), 1)) // CHUNK
    return kc <= qc


def _att_tiles(s, tiles):
    return _tile(s, tiles[0]), _tile(s, tiles[1])


def _lanes(x, n):
    return x if n == 1 else jnp.concatenate([x] * n, axis=1)


def _grid_ends(grid):
    i, j = pl.program_id(0), pl.program_id(1)
    return (i == 0) & (j == 0), (i == grid[0] - 1) & (j == grid[1] - 1)


def _attn_fwd(q2, k2, v2, bsz, slab):
    t = q2.shape[0]
    s = t // bsz
    tq, tk = _att_tiles(s, ATT_FWD_TILES)
    nq, groups, n_diag = s // tq, tk // HP, max(tq // tk, 1)
    sub_rows = tq // n_diag
    scale = MQK ** -0.5
    c2 = scale * LOG2E
    heads = range(ATT_HEADS)

    def body(q_ref, k_ref, v_ref, slab_ref, o_ref, lse_ref, gath_ref, send_sems, recv_sems, local_sem):
        gather = _core_row_gather_copies(slab_ref, gath_ref, send_sems, recv_sems, local_sem)
        first, last = _grid_ends((bsz, MH // ATT_HEADS))
        pl.when(first)(gather.start)

        def q_loop(qi, carry):
            q0 = pl.multiple_of(qi * tq, tq)
            rows = pl.ds(q0, tq)
            n_full = q0 // tk
            qs = [q_ref[rows, h * HP:(h + 1) * HP] for h in heads]

            def scores(h, kj, sub=None, masked=False):
                k0 = pl.multiple_of(kj * tk, tk)
                qv = qs[h] if sub is None else qs[h][sub * sub_rows:(sub + 1) * sub_rows]
                sc = _dot(qv, k_ref[pl.ds(k0, tk), h * HP:(h + 1) * HP], NT)
                return jnp.where(_att_mask(q0 + sub * sub_rows, k0, sub_rows, tk), sc, NEG) if masked else sc

            def fold(mx, sc):
                for j in range(groups):
                    mx = jnp.maximum(mx, sc[:, j * HP:(j + 1) * HP])
                return mx

            def over_diagonal(vals, step):
                out = []
                for h in heads:
                    blocks = []
                    for r in range(n_diag):
                        v = vals[h][r * sub_rows:(r + 1) * sub_rows]
                        for u in range(r + 1):
                            v = step(v, h, n_full + u, r, u == r)
                        blocks.append(v)
                    out.append(blocks[0] if n_diag == 1 else jnp.concatenate(blocks, axis=0))
                return tuple(out)

            mx = lax.fori_loop(0, n_full, lambda kj, mx: tuple(fold(mx[h], scores(h, kj)) for h in heads),
                               tuple(jnp.full((tq, HP), NEG, F32) for _ in heads))
            mx = over_diagonal(mx, lambda v, h, kj, r, masked: fold(v, scores(h, kj, r, masked)))
            mb = [jnp.broadcast_to(jnp.max(mx[h], axis=1, keepdims=True), (tq, HP)) for h in heads]

            def weighted(h, kj, sub=None, masked=False):
                m = mb[h] if sub is None else mb[h][sub * sub_rows:(sub + 1) * sub_rows]
                p = jnp.exp2((scores(h, kj, sub, masked) - _lanes(m, groups)) * c2)
                k0 = pl.multiple_of(kj * tk, tk)
                return _dot(p.astype(BF16), v_ref[pl.ds(k0, tk), h * HP:(h + 1) * HP], NN)

            acc = lax.fori_loop(0, n_full, lambda kj, acc: tuple(acc[h] + weighted(h, kj) for h in heads),
                                tuple(jnp.zeros((tq, HP), F32) for _ in heads))
            acc = over_diagonal(acc, lambda v, h, kj, r, masked: v + weighted(h, kj, r, masked))
            lane = lax.broadcasted_iota(jnp.int32, (tq, HP), 1)
            for h in heads:
                a = acc[h]
                l = jnp.sum(jnp.where(lane == MV, a, 0.0), axis=1, keepdims=True)
                o_ref[rows, h * HP:(h + 1) * HP] = (a / l).astype(BF16)
                lse_ref[rows, h * HP:(h + 1) * HP] = mb[h] * scale + jnp.log(l)
            return carry

        lax.fori_loop(0, nq, q_loop, 0)
        pl.when(last)(gather.finish)

    spec = pl.BlockSpec((s, ATT_HEADS * HP), lambda b, h: (b, h))
    return pl.pallas_call(
        body, name="attn_fwd", grid=(bsz, MH // ATT_HEADS), in_specs=[spec] * 3 + [HBM_SPEC],
        out_specs=[spec, spec, HBM_SPEC],
        out_shape=[jax.ShapeDtypeStruct((t, MH * HP), BF16), jax.ShapeDtypeStruct((t, MH * HP), F32),
                   jax.ShapeDtypeStruct((N_DEV,) + slab.shape, slab.dtype)],
        scratch_shapes=EXCHANGE_SEMS, compiler_params=_cparams(("arbitrary", "arbitrary")),
    )(q2, k2, v2, slab)


def _attn_bwd(q2, k2, v2, do2, o2, lse2, bsz, tsum):
    t = q2.shape[0]
    s = t // bsz
    tq, tk = _att_tiles(s, ATT_BWD_TILES)
    nq, nk, per, groups = s // tq, s // tk, max(tk // tq, 1), tk // HP
    scale = MQK ** -0.5
    c2 = scale * LOG2E
    heads = range(ATT_HEADS)

    def body(q_ref, k_ref, v_ref, do_ref, o_ref, lse_ref, t_ref, dq_ref, dk_ref, dv_ref, parts_ref, dq_acc, delta, lse_b2,
             send_sems, recv_sems, local_sem):
        exchange = _all_to_all_copies(t_ref, parts_ref, send_sems, recv_sems, local_sem)
        first, last = _grid_ends((bsz, MH // ATT_HEADS))
        pl.when(first)(exchange.start)
        dq_acc[...] = jnp.zeros_like(dq_acc)

        def d_loop(i, carry):
            rows = pl.ds(pl.multiple_of(i * tq, tq), tq)
            for h in heads:
                hs = slice(h * HP, (h + 1) * HP)
                dl = jnp.sum(do_ref[rows, hs].astype(F32) * o_ref[rows, hs].astype(F32), axis=1, keepdims=True)
                delta[rows, hs] = jnp.broadcast_to(dl, (tq, HP))
            lse_b2[rows, :] = lse_ref[rows, :] * LOG2E
            return carry

        lax.fori_loop(0, nq, d_loop, 0)

        def k_loop(kj, carry):
            k0 = pl.multiple_of(kj * tk, tk)
            kk = [k_ref[pl.ds(k0, tk), h * HP:(h + 1) * HP] for h in heads]
            vv = [v_ref[pl.ds(k0, tk), h * HP:(h + 1) * HP] for h in heads]

            def tile(qi, c, masked):
                q0 = pl.multiple_of(qi * tq, tq)
                rows = pl.ds(q0, tq)
                out = []
                for h in heads:
                    hs = slice(h * HP, (h + 1) * HP)
                    dk, dv = c[h]
                    q = q_ref[rows, hs]
                    do = do_ref[rows, hs]
                    e = _dot(q, kk[h], NT) * c2 - _lanes(lse_b2[rows, hs], groups)
                    if masked:
                        e = jnp.where(_att_mask(q0, k0, tq, tk), e, NEG)
                    p = jnp.exp2(e)
                    dv = dv + _dot(p.astype(BF16), do, TN)
                    ds = (p * (_dot(do, vv[h], NT) - _lanes(delta[rows, hs], groups))).astype(BF16)
                    dq_acc[rows, hs] += _dot(ds, kk[h], NN)
                    dk = dk + _dot(ds, q, TN)
                    out.append((dk, dv))
                return tuple(out)

            zero = jnp.zeros((tk, HP), F32)
            c = tuple((zero, zero) for _ in heads)
            first = k0 // tq
            for u in range(per):
                c = tile(first + u, c, True)
            c = lax.fori_loop(first + per, nq, lambda qi, c: tile(qi, c, False), c)
            for h in heads:
                dk_ref[pl.ds(k0, tk), h * HP:(h + 1) * HP] = (c[h][0] * scale).astype(BF16)
                dv_ref[pl.ds(k0, tk), h * HP:(h + 1) * HP] = c[h][1].astype(BF16)
            return carry

        lax.fori_loop(0, nk, k_loop, 0)
        dq_ref[...] = (dq_acc[...] * scale).astype(BF16)
        pl.when(last)(exchange.finish)

    spec = pl.BlockSpec((s, ATT_HEADS * HP), lambda b, h: (b, h))
    return pl.pallas_call(
        body, name="attn_bwd", grid=(bsz, MH // ATT_HEADS), in_specs=[spec] * 6 + [HBM_SPEC],
        out_specs=[spec] * 3 + [HBM_SPEC],
        out_shape=[jax.ShapeDtypeStruct((t, MH * HP), BF16)] * 3 + [jax.ShapeDtypeStruct(tsum.shape, tsum.dtype)],
        scratch_shapes=[pltpu.VMEM((s, ATT_HEADS * HP), F32)] * 3 + EXCHANGE_SEMS,
        compiler_params=_cparams(("arbitrary", "arbitrary")),
    )(q2, k2, v2, do2, o2, lse2, tsum)


def _perm_w_in_t(w):
    z = lambda n: jnp.zeros((n, w.shape[1]), w.dtype)
    return jnp.concatenate([w[:3072], w[3504:5552], w[3088:3344], w[3344:3472], w[3072:3088], z(48), w[3472:3504], z(32)],
                           axis=0)


def _unperm_w_in_t(g):
    return jnp.concatenate([g[:3072], g[5504:5520], g[5120:5376], g[5376:5504], g[5568:5600], g[3072:5120]], axis=0)


def _pad_wa(w):
    return jnp.pad(w, ((0, LANES - GLR), (0, 0)))


def _pad_wuq(w):
    return jnp.pad(w.reshape(MQR, MH, MQK), ((0, 0), (0, 0), (0, HP - MQK))).reshape(MQR, MH * HP)


def _unpad_wuq(g):
    return g.reshape(MQR, MH, HP)[:, :, :MQK].reshape(MQR, MH * MQK)


def _pad_wukv(w):
    w3 = w.reshape(MKVR, MH, NOPE + MV)
    kp = jnp.pad(w3[:, :, :NOPE], ((0, 0), (0, 0), (0, HP - NOPE))).reshape(MKVR, MH * HP)
    vp = jnp.pad(w3[:, :, NOPE:], ((0, 0), (0, 0), (0, HP - MV))).reshape(MKVR, MH * HP)
    return jnp.concatenate([kp, vp], axis=1)


def _unpad_wukv(g):
    kp = g[:, :MH * HP].reshape(MKVR, MH, HP)[:, :, :NOPE]
    vp = g[:, MH * HP:].reshape(MKVR, MH, HP)[:, :, :MV]
    return jnp.concatenate([kp, vp], axis=2).reshape(MKVR, MH * (NOPE + MV))


def _pad_wo(w):
    return jnp.pad(w.reshape(MH, MV, D), ((0, 0), (0, HP - MV), (0, 0))).reshape(MH * HP, D)


def _unpad_wo(g):
    return g.reshape(MH, HP, D)[:, :MV, :].reshape(MH * MV, D)


def _repeat_half(a):
    a3 = a.reshape(a.shape[0], -1, HP)
    a3 = jnp.concatenate([a3[:, :, :MQK], a3[:, :, NOPE:NOPE + ROPE // 2], a3[:, :, MQK + ROPE // 2:]], axis=2)
    return a3.reshape(a.shape)


def _pad_lanes(v, n=HP):
    return jnp.pad(v, ((0, 0), (0, n - v.shape[1])))


def _local_step(x, positions, tgt, wt, slab_b, sp, mod3):
    bsz, s, _ = x.shape
    t = bsz * s
    x2 = x.reshape(t, D)
    tgt2 = tgt.reshape(t, D)
    pos = positions.reshape(t, 1)
    fr16 = ROPE_THETA ** (-jnp.arange(0, ROPE, 2, dtype=F32) / ROPE)
    zero = lambda n: jnp.zeros((n,), F32)
    fr = jnp.concatenate([zero(NOPE), fr16, fr16, zero(HP - MQK)]).reshape(1, HP)
    sg = jnp.concatenate([zero(NOPE), -jnp.ones((ROPE // 2,), F32), jnp.ones((ROPE // 2,), F32), zero(HP - MQK)]).reshape(1, HP)

    w_in_t = _perm_w_in_t(wt["w_in"])
    wa_pad = _pad_wa(wt["gla_w_alpha"].T)
    wuq = _pad_wuq(wt["mla_w_uq"].T)
    wukv = _pad_wukv(wt["mla_w_ukv"].T)
    qn_g, kn_g = _pad_lanes(sp["mla_qn_g"]), _pad_lanes(sp["mla_kn_g"])

    tm = _tile(s, 1024)
    h = _norm_mod_fwd(x2, sp["norm1_g"], mod3, 0, 1, "norm1_fwd")
    proj = _mm(h, w_in_t, "nt", (BF16,), "proj_fwd", tn=PW // 4)
    o_gla, og, states = _gla_fwd(proj, wa_pad, sp["gla_b_alpha"], sp["gla_out_norm_g"], bsz)
    q2, k2, v2 = _mla_pre_fwd(proj, pos, fr, sg, sp["mla_q_lat_g"], sp["mla_kv_lat_g"], _repeat_half(qn_g), _repeat_half(kn_g),
                              _repeat_half(wuq), wukv)
    o2, lse2, core_row = _attn_fwd(q2, k2, v2, bsz, slab_b)
    wt = dict(wt, **_unpack_gathered(_cross_core_fill(core_row), SLAB_B))
    wo_pad = _pad_wo(wt["mla_w_o"])
    y_a = _mm(og, wt["gla_w_o"], "nn", (BF16,), "gla_out_fwd")
    y_b = _mm(o2, wo_pad, "nn", (BF16,), "mla_out_fwd")
    mix = _merge_fwd(proj, sp["b_merge"], y_a, y_b)
    mixed, x1, h2 = _mm(mix, wt["w_out"], "nn", (), "w_out_fwd", tm=tm,
                        tail=_norm_fwd_tail(x2, sp["norm2_g"], mod3, 3, 4, 2, tm))
    a, f = _mm(h2, wt["mlp_w1"], "nt", (BF16, BF16), "mlp1_fwd",
               epi=lambda acc: (acc, jnp.square(jnp.maximum(acc, 0.0))))
    dy, dff, acc_g2, acc_loss = _mm(f, wt["mlp_w2"], "nn", (), "mlp2_fwd", tm=tm, tail=_loss_tail(x1, tgt2, mod3, tm))

    gw = {}
    gw["mlp_w2"] = _mm(f, dff, "tn", (BF16,), "mlp2_dw")
    da = _mm(dff, wt["mlp_w2"], "nt", (BF16,), "mlp2_dx", extras=(a,),
             epi=lambda acc, av: (acc * (2.0 * jnp.maximum(av.astype(F32), 0.0)),))
    gw["mlp_w1"] = _mm(da, h2, "tn", (BF16,), "mlp1_dw")
    dx1, dmixed, accb2, accg2 = _mm(da, wt["mlp_w1"], "nn", (), "mlp1_dx", tm=tm,
                                    tail=_norm_bwd_tail(x1, dy, sp["norm2_g"], mod3, 4, tm, mixed=mixed, i_gate=2))

    gw["w_out"] = _mm(mix, dmixed, "tn", (BF16,), "w_out_dw")
    dy_a, dy_b, dlogits, acc_bm = _mm(dmixed, wt["w_out"], "nt", (), "w_out_dx", tm=tm,
                                      tail=_merge_bwd_tail(proj, sp["b_merge"], y_a, y_b, tm))
    gw["gla_w_o"] = _mm(og, dy_a, "tn", (BF16,), "gla_out_dw")
    dog = _mm(dy_a, wt["gla_w_o"], "nt", (BF16,), "gla_out_dx")
    gw["mla_w_o"] = _unpad_wo(_mm(o2, dy_b, "tn", (BF16,), "mla_out_dw"))
    do2 = _mm(dy_b, wo_pad, "nt", (BF16,), "mla_out_dx")
    dq2, dk2, dv2, parts_b = _attn_bwd(q2, k2, v2, do2, o2, lse2, bsz, _pack_per_device(gw, SLAB_B))
    dq_g, dk_g, dv_g, dgg, dz, acc_ba, acc_go = _gla_bwd(dog, o_gla, proj, wa_pad, sp["gla_b_alpha"],
                                                         sp["gla_out_norm_g"], states, bsz)
    gw["gla_w_alpha"] = _mm(proj, dz, "tn", (F32,), "gla_alpha_dw", a_off=43, m=LANES)[:GLR].T.astype(BF16)
    dmisc_gla = _mm(dz, wa_pad, "nt", (F32,), "gla_alpha_dx")
    dcq, dckv, dmisc, gwuq, gwukv, acc_mla = _mla_pre_bwd(dq2, dk2, dv2, dmisc_gla, proj, pos, fr, sg, sp["mla_q_lat_g"],
                                                         sp["mla_kv_lat_g"], qn_g, kn_g, wuq, wukv)
    gw["mla_w_uq"] = _unpad_wuq(gwuq).T.astype(BF16)
    gw["mla_w_ukv"] = _unpad_wukv(gwukv).T.astype(BF16)
    dproj = jnp.concatenate([dq_g, dk_g, dv_g, dgg, dlogits, dcq, dckv, dmisc], axis=1)
    gw["w_in"] = _unperm_w_in_t(_mm(dproj, h, "tn", (BF16,), "proj_dw", tm=PW // 4))
    parts_a, grad_x, accb1, accg1 = _mm(dproj, w_in_t, "nn", (), "proj_dx", tm=tm, tk=PW // 4,
                                        cargo=_sum_over_cores(_pack_per_device(gw, SLAB_A), "a"),
                                        tail=_norm_bwd_tail(x2, dx1, sp["norm1_g"], mod3, 1, tm))

    dmod = jnp.stack([accb1[:, 0], accb1[:, 1], accb2[:, 2], accb2[:, 0], accb2[:, 1], acc_g2[:, 0]], axis=1)

    rows = {
        "dmod": dmod.reshape(bsz * 6, D),
        "norm1_g": accg1[0:1], "norm2_g": accg2[0:1],
        "b_merge": acc_bm[0:1].reshape(2, D),
        "gla_b_alpha": _pad_lanes(acc_ba[0:1], D),
        "gla_out_norm_g": _pad_lanes(acc_go.reshape(GH, 8, GDV)[:, 0, :], D),
        "mla_q_lat_g": _pad_lanes(acc_mla[0:1], D), "mla_kv_lat_g": _pad_lanes(acc_mla[1:2], D),
        "mla_qn_g": _pad_lanes(acc_mla[2:3], D), "mla_kn_g": _pad_lanes(acc_mla[3:4], D),
        "loss": acc_loss[0:1],
    }
    return grad_x.reshape(bsz, s, D), parts_a, parts_b, rows


HBM_SPEC = pl.BlockSpec(memory_space=pltpu.HBM)


def _all_gather(ps, name):
    n = len(ps)

    def body(*refs):
        p_refs, out_refs, (send_sems, recv_sems, local_sems) = refs[:n], refs[n:2 * n], refs[2 * n:]
        x, y, c = lax.axis_index("x"), lax.axis_index("y"), lax.axis_index("c")
        me, sibling = (x, y, c), (x, y, 1 - c)
        chips = [(1 - x, y), (x, 1 - y), (1 - x, 1 - y)]

        def copy(a, k, block, to, own=False):
            slot = out_refs[a].at[4 * block[0] + 2 * block[1] + block[2]]
            return pltpu.make_async_remote_copy(
                src_ref=p_refs[a] if own else slot, dst_ref=slot, send_sem=send_sems.at[7 * a + k],
                recv_sem=recv_sems.at[7 * a + k], device_id=to, device_id_type=MESH)

        mine = [pltpu.make_async_copy(p_refs[a], out_refs[a].at[4 * x + 2 * y + c], local_sems.at[a]) for a in range(n)]
        first = [copy(a, 0, me, sibling, own=True) for a in range(n)]
        first += [copy(a, 1 + j, me, (*chip, c), own=True) for a in range(n) for j, chip in enumerate(chips)]
        for cp in mine + first:
            cp.start()
        passed = []
        for j, chip in enumerate(chips):
            for a in range(n):
                copy(a, 1 + j, (*chip, c), me).wait_recv()
                passed.append(copy(a, 4 + j, (*chip, c), sibling))
                passed[-1].start()
        for a in range(n):
            copy(a, 0, sibling, me).wait_recv()
            for j, chip in enumerate(chips):
                copy(a, 4 + j, (*chip, 1 - c), me).wait_recv()
        for cp in first + passed:
            cp.wait_send()
        for cp in mine:
            cp.wait()

    return pl.pallas_call(
        body, name=name, out_shape=[jax.ShapeDtypeStruct((N_DEV,) + p.shape, p.dtype) for p in ps],
        in_specs=[HBM_SPEC] * n, out_specs=[HBM_SPEC] * n,
        scratch_shapes=[pltpu.SemaphoreType.DMA((7 * n,)), pltpu.SemaphoreType.DMA((7 * n,)), pltpu.SemaphoreType.DMA((n,))],
    )(*ps)


def _sibling_exchange(g, name):
    def body(g_ref, out_ref, send_sem, recv_sem):
        x, y, c = lax.axis_index("x"), lax.axis_index("y"), lax.axis_index("c")
        cp = pltpu.make_async_remote_copy(src_ref=g_ref, dst_ref=out_ref, send_sem=send_sem, recv_sem=recv_sem,
                                          device_id=(x, y, 1 - c), device_id_type=MESH)
        cp.start()
        cp.wait()

    return pl.pallas_call(
        body, name=name, out_shape=jax.ShapeDtypeStruct(g.shape, g.dtype),
        in_specs=[HBM_SPEC], out_specs=HBM_SPEC,
        scratch_shapes=[pltpu.SemaphoreType.DMA(()), pltpu.SemaphoreType.DMA(())],
    )(g)


class _Exchange:
    def __init__(self, local, sends, arrivals):
        self.local, self.sends, self.arrivals = local, sends, arrivals

    def start(self):
        self.local.start()
        for cp in self.sends:
            cp.start()

    def finish(self):
        for cp in self.arrivals:
            cp.wait_recv()
        for cp in self.sends:
            cp.wait_send()
        self.local.wait()


EXCHANGE_SEMS = [pltpu.SemaphoreType.DMA((N_DEV,)), pltpu.SemaphoreType.DMA((N_DEV,)), pltpu.SemaphoreType.DMA(())]


def _all_to_all_copies(t_ref, out_ref, send_sems, recv_sems, local_sem):
    x, y, c = lax.axis_index("x"), lax.axis_index("y"), lax.axis_index("c")
    me = 4 * x + 2 * y + c

    def copy(k, src, dst):
        px, py, pc = (1 - x if k & 4 else x), (1 - y if k & 2 else y), (1 - c if k & 1 else c)
        peer = 4 * px + 2 * py + pc
        return pltpu.make_async_remote_copy(src_ref=t_ref.at[peer if src is None else src],
                                            dst_ref=out_ref.at[peer if dst is None else dst], send_sem=send_sems.at[k],
                                            recv_sem=recv_sems.at[k], device_id=(px, py, pc), device_id_type=MESH)

    return _Exchange(pltpu.make_async_copy(t_ref.at[me], out_ref.at[me], local_sem),
                     [copy(k, None, me) for k in range(1, N_DEV)], [copy(k, me, None) for k in range(1, N_DEV)])


def _chip_exchange_copies(t_ref, out_ref, send_sems, recv_sems, local_sem):
    x, y, c = lax.axis_index("x"), lax.axis_index("y"), lax.axis_index("c")
    my_chip = 2 * x + y
    chips = [(1 - x, y), (x, 1 - y), (1 - x, 1 - y)]

    def copy(j, src, dst, px, py):
        return pltpu.make_async_remote_copy(src_ref=t_ref.at[src], dst_ref=out_ref.at[dst], send_sem=send_sems.at[j],
                                            recv_sem=recv_sems.at[j], device_id=(px, py, c), device_id_type=MESH)

    return _Exchange(pltpu.make_async_copy(t_ref.at[my_chip], out_ref.at[my_chip], local_sem),
                     [copy(j, 2 * px + py, my_chip, px, py) for j, (px, py) in enumerate(chips)],
                     [copy(j, my_chip, 2 * px + py, px, py) for j, (px, py) in enumerate(chips)])


def _core_row_gather_copies(p_ref, out_ref, send_sems, recv_sems, local_sem):
    x, y, c = lax.axis_index("x"), lax.axis_index("y"), lax.axis_index("c")
    peers = [(x, y, 1 - c), (1 - x, y, c), (x, 1 - y, c), (1 - x, 1 - y, c)]

    def slot(px, py, pc):
        return out_ref.at[4 * px + 2 * py + pc]

    def copy(j, block, to):
        return pltpu.make_async_remote_copy(src_ref=p_ref, dst_ref=slot(*block), send_sem=send_sems.at[j],
                                            recv_sem=recv_sems.at[j], device_id=to, device_id_type=MESH)

    return _Exchange(pltpu.make_async_copy(p_ref, slot(x, y, c), local_sem),
                     [copy(j, (x, y, c), peer) for j, peer in enumerate(peers)],
                     [copy(j, peer, peer) for j, peer in enumerate(peers)])


def _cross_core_fill(gathered):
    def body(g_ref, out_ref, send_sems, recv_sems):
        x, y, c = lax.axis_index("x"), lax.axis_index("y"), lax.axis_index("c")
        chips = [(1 - x, y), (x, 1 - y), (1 - x, 1 - y)]

        def copy(j, pc):
            px, py = chips[j]
            slot = 4 * px + 2 * py + pc
            return pltpu.make_async_remote_copy(src_ref=g_ref.at[slot], dst_ref=out_ref.at[slot], send_sem=send_sems.at[j],
                                                recv_sem=recv_sems.at[j], device_id=(x, y, 1 - c), device_id_type=MESH)

        sends = [copy(j, c) for j in range(3)]
        for cp in sends:
            cp.start()
        for j in range(3):
            copy(j, 1 - c).wait_recv()
        for cp in sends:
            cp.wait_send()

    return pl.pallas_call(
        body, name="weights_cross_core_fill", out_shape=jax.ShapeDtypeStruct(gathered.shape, gathered.dtype),
        in_specs=[HBM_SPEC], out_specs=HBM_SPEC, input_output_aliases={0: 0},
        scratch_shapes=[pltpu.SemaphoreType.DMA((3,)), pltpu.SemaphoreType.DMA((3,))],
    )(gathered)


def _slab_block(r):
    return max(b for b in range(16, SLAB_BLOCK_MAX + 1, 16) if r % b == 0)


def _pair_sum(a, b, name):
    n, r, cdim = a.shape
    rb = _slab_block(r)
    blk = pl.BlockSpec((1, rb, cdim), lambda j, i: (j, i, 0))

    def body(a_ref, b_ref, o_ref):
        o_ref[...] = (a_ref[...].astype(F32) + b_ref[...].astype(F32)).astype(BF16)

    return pl.pallas_call(
        body, name=name, grid=(n, r // rb), in_specs=[blk, blk], out_specs=blk,
        out_shape=jax.ShapeDtypeStruct(a.shape, BF16), compiler_params=_cparams(("arbitrary", "arbitrary")),
    )(a, b)


def _adamw_math(w, g, m, v):
    m = ADAM_B1 * m + (1.0 - ADAM_B1) * g
    v = ADAM_B2 * v + (1.0 - ADAM_B2) * jnp.square(g)
    m_hat = m / (1.0 - ADAM_B1 ** ADAM_STEP)
    v_hat = v / (1.0 - ADAM_B2 ** ADAM_STEP)
    delta = -ADAM_LR * (m_hat / (jnp.sqrt(v_hat) + ADAM_EPS) + ADAM_WD * w)
    return delta, m, v


def _slab_sum(parts, name):
    n, r, cdim = parts.shape
    rb = _slab_block(r)
    blk = pl.BlockSpec((rb, cdim), lambda i: (i, 0))

    def body(p_ref, g_out):
        g = p_ref[0].astype(F32)
        for j in range(1, n):
            g = g + p_ref[j].astype(F32)
        g_out[...] = g

    return pl.pallas_call(
        body, name=name, grid=(r // rb,),
        in_specs=[pl.BlockSpec((n, rb, cdim), lambda i: (0, i, 0))], out_specs=blk,
        out_shape=jax.ShapeDtypeStruct((r, cdim), F32), compiler_params=_cparams(("arbitrary",)),
    )(parts)


def _adamw(g, w, m, v, name):
    r, cdim = w.shape
    rb = _tile(r, 256)
    blk = pl.BlockSpec((rb, cdim), lambda i: (i, 0))

    def body(g_ref, w_ref, m_ref, v_ref, d_out, m_out, v_out):
        d_out[...], m_out[...], v_out[...] = _adamw_math(w_ref[...], g_ref[...], m_ref[...], v_ref[...])

    return pl.pallas_call(
        body, name=name, grid=(r // rb,), in_specs=[blk] * 4, out_specs=[blk] * 3,
        out_shape=[jax.ShapeDtypeStruct((r, cdim), F32)] * 3, compiler_params=_cparams(("arbitrary",)),
    )(g, w, m, v)


def _adamw_small(parts, w, m, v):
    def body(p_ref, w_ref, m_ref, v_ref, g_out, d_out, m_out, v_out, loss_out):
        def total(srcs):
            acc = None
            for r in srcs:
                for j in range(N_DEV):
                    term = p_ref[j, r:r + 1, :]
                    acc = term if acc is None else acc + term
            return acc

        for prow, srcs in enumerate(SMALL_SOURCES):
            one = slice(prow, prow + 1)
            g = total(srcs)
            g_out[one, :] = g
            d_out[one, :], m_out[one, :], v_out[one, :] = _adamw_math(w_ref[one, :], g, m_ref[one, :], v_ref[one, :])
        loss_out[...] = jnp.broadcast_to(jnp.sum(total(LOSS_SOURCE), axis=1, keepdims=True), (8, LANES))

    full = lambda shp: pl.BlockSpec(shp, lambda i: (0,) * len(shp))
    return pl.pallas_call(
        body, name="adamw_small", grid=(1,),
        in_specs=[full((N_DEV, SMALL_ROWS, D)), full((16, D)), full((16, D)), full((16, D))],
        out_specs=[full((16, D))] * 4 + [full((8, LANES))],
        out_shape=[jax.ShapeDtypeStruct((16, D), F32)] * 4 + [jax.ShapeDtypeStruct((8, LANES), F32)],
        compiler_params=_cparams(("arbitrary",)),
    )(parts, w, m, v)


def _to_slab(shard, form):
    if form == "N":
        return shard
    return shard.T if form == "T" else shard.T.reshape(-1, D)


def _from_slab(block, form, shard_shape):
    if form == "N":
        return block
    return block.T if form == "T" else block.reshape(shard_shape[1], shard_shape[0]).T


def _gathered_full(g, form, shard_shape):
    if form == "TR":
        return g.reshape(N_DEV * shard_shape[1], shard_shape[0])
    return g.reshape(N_DEV * g.shape[1], D)


def _pack_slab(shards, layout):
    return jnp.concatenate([jnp.zeros((r, D), BF16) if n is None else _to_slab(shards[n], form).astype(BF16)
                            for n, r, form in layout], axis=0)


def _unpack_gathered(gathered, layout):
    out, off = {}, 0
    for n, r, form in layout:
        if n is not None:
            out[n] = _gathered_full(gathered[:, off:off + r], form, SHARD_SHAPES[n])
        off += r
    return out


def _pack_per_device(gw, layout):
    return jnp.concatenate([jnp.zeros((N_DEV, r, D), BF16) if n is None else gw[n].reshape(N_DEV, r, D)
                            for n, r, _ in layout], axis=1)


def _adamw_shards(gslab, layout, wts, mom, var):
    out, off = {}, 0
    for n, r, form in layout:
        if n is not None:
            g = _from_slab(gslab[off:off + r], form, SHARD_SHAPES[n])
            out[n] = (g,) + tuple(_adamw(g, wts[n], mom[n], var[n], "adamw_" + n))
        off += r
    return out


def _sum_over_cores(per_dev, tag):
    my_c = lax.axis_index("c")
    pairs = per_dev.reshape(4, 2, per_dev.shape[1], D)
    keep = lax.dynamic_index_in_dim(pairs, my_c, axis=1, keepdims=False)
    give = lax.dynamic_index_in_dim(pairs, 1 - my_c, axis=1, keepdims=False)
    return _pair_sum(keep, _sibling_exchange(give, "rs_sibling_exchange_" + tag), "rs_pair_sum_" + tag)


def _small_pack(vals):
    rows = []
    for n in SMALL:
        v = vals[n].reshape(-1)
        k = -(-v.shape[0] // D)
        rows.append(jnp.pad(v, (0, k * D - v.shape[0])).reshape(k, D))
    return jnp.concatenate(rows, axis=0)


def _small_unpack(packed, shapes):
    out = {}
    for n in SMALL:
        k = shapes[n][-1]
        r0 = SMALL_ROW[n]
        out[n] = packed[r0:r0 + -(-k // D)].reshape(-1)[:k].reshape(shapes[n])
    return out


def kernel(x, c, positions, w_ada, b_ada, norm1_g, w_in, b_merge, gla_w_alpha, gla_b_alpha, gla_out_norm_g, gla_w_o, mla_q_lat_g, mla_w_uq, mla_kv_lat_g, mla_w_ukv, mla_qn_g, mla_kn_g, mla_w_o, w_out, norm2_g, mlp_w1, mlp_w2, loss_target, m_w_ada, m_b_ada, m_norm1_g, m_w_in, m_b_merge, m_gla_w_alpha, m_gla_b_alpha, m_gla_out_norm_g, m_gla_w_o, m_mla_q_lat_g, m_mla_w_uq, m_mla_kv_lat_g, m_mla_w_ukv, m_mla_qn_g, m_mla_kn_g, m_mla_w_o, m_w_out, m_norm2_g, m_mlp_w1, m_mlp_w2, v_w_ada, v_b_ada, v_norm1_g, v_w_in, v_b_merge, v_gla_w_alpha, v_gla_b_alpha, v_gla_out_norm_g, v_gla_w_o, v_mla_q_lat_g, v_mla_w_uq, v_mla_kv_lat_g, v_mla_w_ukv, v_mla_qn_g, v_mla_kn_g, v_mla_w_o, v_w_out, v_norm2_g, v_mlp_w1, v_mlp_w2):
    args = dict(locals())
    wts = {n: args[n][0] for n in WEIGHTS}
    mom = {n: args["m_" + n][0] for n in WEIGHTS}
    var = {n: args["v_" + n][0] for n in WEIGHTS}
    my_c = lax.axis_index("c")
    my_dev = 4 * lax.axis_index("x") + 2 * lax.axis_index("y") + my_c
    bsz = x.shape[0]
    sp = {n: wts[n].reshape(1, -1) for n in SMALL}

    gathered_a, c_gathered = _all_gather([_pack_slab(wts, SLAB_A), jnp.pad(c, ((0, 8 - bsz), (0, 0)))], "weights_all_gather")
    wt = _unpack_gathered(gathered_a, SLAB_A)

    c_all = c_gathered[:, :bsz].reshape(N_DEV * bsz, D)
    bias = lax.dynamic_slice_in_dim(sp["b_ada"], my_dev * ADA_COLS, ADA_COLS, axis=1)
    mod_cols = _mm(c_all, wts["w_ada"], "nn", (F32,), "ada_fwd", pro=_silu, epi=lambda acc, b: (acc + b,),
                   extras=(jnp.broadcast_to(bias, (N_DEV * bsz, ADA_COLS)),))
    mod_all, = _all_gather([mod_cols], "mod_all_gather")
    mod_mine = lax.dynamic_slice_in_dim(mod_all, my_dev * bsz, bsz, axis=1)
    mod3 = jnp.transpose(mod_mine, (1, 0, 2)).reshape(bsz, 6, D)

    grad_x, parts_a, parts_b, rows = _local_step(x, positions, loss_target, wt, _pack_slab(wts, SLAB_B), sp, mod3)

    big = dict(_adamw_shards(_slab_sum(parts_a, "rs_slab_sum_a"), SLAB_A, wts, mom, var),
               **_adamw_shards(_slab_sum(parts_b, "rs_slab_sum_b"), SLAB_B, wts, mom, var))

    order = ["dmod", "norm1_g", "norm2_g", "b_merge", "gla_b_alpha", "gla_out_norm_g", "mla_q_lat_g", "mla_kv_lat_g",
             "mla_qn_g", "mla_kn_g", "loss"]
    part_rows = jnp.concatenate([rows[n] for n in order], axis=0)
    part_rows = jnp.pad(part_rows, ((0, SMALL_ROWS - part_rows.shape[0]), (0, 0)))
    all_rows, = _all_gather([part_rows], "partials_all_gather")

    dmod_all = all_rows[:, :6 * bsz].reshape(N_DEV * bsz, 6 * D)
    dmod_cols = lax.dynamic_slice_in_dim(dmod_all, my_dev * ADA_COLS, ADA_COLS, axis=1)
    g_ada = _mm(c_all, dmod_cols, "tn", (F32,), "ada_dw", pro=_silu)
    big["w_ada"] = (g_ada,) + tuple(_adamw(g_ada, wts["w_ada"], mom["w_ada"], var["w_ada"], "adamw_w_ada"))

    small = _adamw_small(all_rows, _small_pack({n: wts[n] for n in SMALL}), _small_pack({n: mom[n] for n in SMALL}),
                         _small_pack({n: var[n] for n in SMALL}))
    loss = small[4][0, 0]
    small_shapes = {n: wts[n].shape for n in SMALL}
    small = [_small_unpack(o, small_shapes) for o in small[:4]]

    outs = [loss, grad_x]
    for k in range(4):
        for n in WEIGHTS:
            val = big[n][k] if n in BIG else small[k][n]
            outs.append(val.reshape((1,) + tuple(wts[n].shape)))
    return tuple(outs)
```

```python
import jax
import jax.numpy as jnp
from jax import lax
from jax.experimental import pallas as pl
from jax.experimental.pallas import tpu as pltpu

F32 = jnp.float32
BF16 = jnp.bfloat16
MESH = pl.DeviceIdType.MESH

D = 1024
EPS = 1e-6
CHUNK = 64
GH, GDK, GDV, GLR, GTAU = 4, 128, 256, 16, 16.0
MH, MQR, MKVR, NOPE, ROPE, MV = 16, 256, 128, 64, 32, 64
MQK = NOPE + ROPE
HP = 128
FF = 4 * D
ROPE_THETA = 10000.0
IN_WIDTH = 5552
PW = 5632
N_DEV = 8
LANES = 128
SLAB_BLOCK_MAX = 400
ADA_COLS = 6 * D // N_DEV
SMALL_ROWS = 32
SMALL_SOURCES = tuple([(r, 6 + r) for r in range(6)] + [(12,), (13,), (14,), (15,), (16,), (17, 18, 19, 20),
                                                         (21,), (22,), (23,), (24,)])
LOSS_SOURCE = (25,)
VMEM_LIMIT = 56 * 1024 * 1024

ADAM_LR, ADAM_B1, ADAM_B2, ADAM_EPS, ADAM_WD, ADAM_STEP = 0.001, 0.9, 0.999, 1e-08, 0.01, 10

SLAB_A = (("w_in", 694, "T"), ("gla_w_alpha", 1, "TR"), (None, 9, None), ("mla_w_uq", 48, "TR"), ("mla_w_ukv", 32, "TR"))
SLAB_B = (("mlp_w1", 512, "T"), ("gla_w_o", 128, "N"), ("mla_w_o", 128, "N"), ("w_out", 128, "N"), ("mlp_w2", 512, "N"))
BIG = ("w_ada",) + tuple(n for n, _, _ in SLAB_A + SLAB_B if n is not None)
SHARD_SHAPES = {"w_ada": (D, 6 * D // N_DEV), "w_in": (D, IN_WIDTH // N_DEV), "gla_w_alpha": (GLR, GH * GDK // N_DEV),
                "gla_w_o": (GH * GDV // N_DEV, D), "mla_w_uq": (MQR, MH * MQK // N_DEV),
                "mla_w_ukv": (MKVR, MH * (NOPE + MV) // N_DEV), "mla_w_o": (MH * MV // N_DEV, D), "w_out": (D // N_DEV, D),
                "mlp_w1": (D, FF // N_DEV), "mlp_w2": (FF // N_DEV, D)}
SMALL = ("b_ada", "norm1_g", "norm2_g", "b_merge", "gla_b_alpha", "gla_out_norm_g", "mla_q_lat_g", "mla_kv_lat_g",
         "mla_qn_g", "mla_kn_g")
SMALL_ROW = {"b_ada": 0, "norm1_g": 6, "norm2_g": 7, "b_merge": 8, "gla_b_alpha": 10, "gla_out_norm_g": 11,
             "mla_q_lat_g": 12, "mla_kv_lat_g": 13, "mla_qn_g": 14, "mla_kn_g": 15}
WEIGHTS = ("w_ada", "b_ada", "norm1_g", "w_in", "b_merge", "gla_w_alpha", "gla_b_alpha", "gla_out_norm_g", "gla_w_o",
           "mla_q_lat_g", "mla_w_uq", "mla_kv_lat_g", "mla_w_ukv", "mla_qn_g", "mla_kn_g", "mla_w_o", "w_out",
           "norm2_g", "mlp_w1", "mlp_w2")


def _cparams(sem=None):
    return pltpu.CompilerParams(dimension_semantics=sem, vmem_limit_bytes=VMEM_LIMIT)


def _tile(n, pref):
    for t in (2048, PW // 4, 1024, 512, 256, 128):
        if t <= pref and n % t == 0:
            return t
    return n


def _dot(a, b, dims, precision=None):
    return lax.dot_general(a, b, (dims, ((), ())), preferred_element_type=F32, precision=precision)


NN = ((1,), (0,))
NT = ((1,), (1,))
TN = ((0,), (0,))


def _sigmoid(x):
    return 1.0 / (1.0 + jnp.exp(-x))


def _silu(x):
    return x * _sigmoid(x)


def _mm(a, b, mode, out_dtypes, name, *, pro=None, pro_b=None, epi=None, extras=(), a_off=0, m=None, tm=2048, tn=1024,
        tk=1024, cargo=None, tail=None):
    if mode == "tn":
        kc, n = b.shape
        m = a.shape[1] if m is None else m
    elif mode == "nn":
        m, kc = a.shape
        n = b.shape[1]
    else:
        m, kc = a.shape
        n = b.shape[0]
    tm, tn, tk = _tile(m, tm), _tile(n, tn), _tile(kc, tk)
    nk = kc // tk
    dims = {"nn": NN, "nt": NT, "tn": TN}[mode]
    if mode == "tn":
        a_spec = pl.BlockSpec((tk, tm), lambda i, j, k: (k, i + a_off))
    else:
        a_spec = pl.BlockSpec((tm, tk), lambda i, j, k: (i + a_off, k))
    if mode == "nt":
        b_spec = pl.BlockSpec((tn, tk), lambda i, j, k: (j, k))
    else:
        b_spec = pl.BlockSpec((tk, tn), lambda i, j, k: (k, j))
    o_spec = pl.BlockSpec((tm, tn), lambda i, j, k: (i, j))
    n_ex, n_out = len(extras), len(out_dtypes)
    grid = (m // tm, n // tn, nk)
    has_cargo = cargo is not None
    t_ins, t_outs = (tail["ins"], tail["outs"]) if tail else ([], [])
    assert not tail or grid[1] == 1

    def body(a_ref, b_ref, *rest):
        rest = list(rest)
        take = lambda count: [rest.pop(0) for _ in range(count)]
        ex, cargo_ref, tail_in = take(n_ex), take(has_cargo), take(len(t_ins))
        outs, parts_ref, tail_out = take(n_out), take(has_cargo), take(len(t_outs))
        acc = rest.pop(0)
        steps = [pl.program_id(axis) for axis in range(3)]
        if has_cargo:
            exchange = _chip_exchange_copies(cargo_ref[0], parts_ref[0], *rest)
            first = (steps[0] == 0) & (steps[1] == 0) & (steps[2] == 0)
            last = (steps[0] == grid[0] - 1) & (steps[1] == grid[1] - 1) & (steps[2] == grid[2] - 1)
            pl.when(first)(exchange.start)
        k = steps[2]

        @pl.when(k == 0)
        def _():
            acc[...] = jnp.zeros_like(acc)

        av = a_ref[...]
        if pro is not None:
            av = pro(av)
        bv = b_ref[...]
        if pro_b is not None:
            bv = pro_b(bv)
        acc[...] += _dot(av.astype(BF16), bv.astype(BF16), dims)

        @pl.when(k == nk - 1)
        def _():
            if tail:
                tail["fn"](acc[...], steps[0], *tail_in, *tail_out)
            res = (acc[...],) if epi is None else epi(acc[...], *[e[...] for e in ex])
            for o_ref, r in zip(outs, res):
                o_ref[...] = r.astype(o_ref.dtype)

        if has_cargo:
            pl.when(last)(exchange.finish)

    cargo_in = [cargo] if has_cargo else []
    cargo_spec = [HBM_SPEC] * len(cargo_in)
    sequential = has_cargo or bool(tail)
    out = pl.pallas_call(
        body, name=name, grid=grid,
        in_specs=[a_spec, b_spec] + [o_spec] * n_ex + cargo_spec + (tail["in_specs"] if tail else []),
        out_specs=[o_spec] * n_out + cargo_spec + (tail["out_specs"] if tail else []),
        out_shape=[jax.ShapeDtypeStruct((m, n), dt) for dt in out_dtypes]
        + [jax.ShapeDtypeStruct(c.shape, c.dtype) for c in cargo_in] + t_outs,
        scratch_shapes=[pltpu.VMEM((tm, tn), F32)] + (EXCHANGE_SEMS if has_cargo else []),
        compiler_params=_cparams(("arbitrary",) * 3 if sequential else ("parallel", "parallel", "arbitrary")),
    )(a, b, *extras, *cargo_in, *t_ins)
    return out[0] if len(out) == 1 else out


def _rows(s):
    return _tile(s, 512)


def _mod_spec():
    return pl.BlockSpec((1, 6, D), lambda b, i: (b, 0, 0))


def _tok_spec(tr, nb, width=D, col=0):
    return pl.BlockSpec((tr, width), lambda b, i: (b * nb + i, col))


def _modulated_norm(xv, gv, mod_ref, i_shift, i_scale):
    r = lax.rsqrt(jnp.mean(xv * xv, axis=1, keepdims=True) + EPS)
    return ((xv * r) * gv) * (1.0 + mod_ref[0, i_scale:i_scale + 1, :]) + mod_ref[0, i_shift:i_shift + 1, :]


def _norm_mod_fwd(x, g, mod3, i_shift, i_scale, name):
    bsz, _, _ = mod3.shape
    t = x.shape[0]
    s = t // bsz
    tr = _rows(s)
    nb = s // tr

    def body(x_ref, g_ref, mod_ref, h_ref):
        h_ref[...] = _modulated_norm(x_ref[...], g_ref[...], mod_ref, i_shift, i_scale).astype(BF16)

    tok = _tok_spec(tr, nb)
    return pl.pallas_call(
        body, name=name, grid=(bsz, nb), in_specs=[tok, pl.BlockSpec((1, D), lambda b, i: (0, 0)), _mod_spec()],
        out_specs=tok, out_shape=jax.ShapeDtypeStruct((t, D), BF16), compiler_params=_cparams(("arbitrary", "arbitrary")),
    )(x, g, mod3)


def _norm_fwd_tail(x, g, mod3, i_shift, i_scale, i_gate, tm):
    bsz, t = mod3.shape[0], x.shape[0]
    per_b = t // bsz // tm

    def fn(mixedv, i, x_ref, g_ref, mod_ref, mixed_ref, x1_ref, h_ref):
        mixed_ref[...] = mixedv.astype(BF16)
        xv = x_ref[...] + mod_ref[0, i_gate:i_gate + 1, :] * mixedv
        x1_ref[...] = xv
        h_ref[...] = _modulated_norm(xv, g_ref[...], mod_ref, i_shift, i_scale).astype(BF16)

    tok = pl.BlockSpec((tm, D), lambda i, j, k: (i, 0))
    return dict(
        fn=fn, ins=[x, g, mod3], in_specs=[tok, pl.BlockSpec((1, D), lambda i, j, k: (0, 0)), _tail_mod_spec(per_b)],
        outs=[jax.ShapeDtypeStruct((t, D), BF16), jax.ShapeDtypeStruct((t, D), F32), jax.ShapeDtypeStruct((t, D), BF16)],
        out_specs=[tok, tok, tok])


def _norm_bwd_rows(xv, dhv, dresv, gv, mod_ref, i_scale, accb, accg):
    r = lax.rsqrt(jnp.mean(xv * xv, axis=1, keepdims=True) + EPS)
    xn = xv * r
    accb[0, 0:1, :] += jnp.sum(dhv, axis=0, keepdims=True)
    accb[0, 1:2, :] += jnp.sum(dhv * (xn * gv), axis=0, keepdims=True)
    tt = dhv * (1.0 + mod_ref[0, i_scale:i_scale + 1, :])
    accg[0:1, :] += jnp.sum(tt * xn, axis=0, keepdims=True)
    dxn = tt * gv
    return dresv + r * (dxn - xn * jnp.mean(dxn * xn, axis=1, keepdims=True))


def _norm_bwd_tail(x, dres, g, mod3, i_scale, tm, mixed=None, i_gate=None):
    bsz, t = mod3.shape[0], x.shape[0]
    per_b = t // bsz // tm
    has_res = mixed is not None

    def fn(dhv, i, *refs):
        if has_res:
            x_ref, dres_ref, mx_ref, g_ref, mod_ref, dx_ref, dmx_ref, accb, accg = refs
        else:
            x_ref, dres_ref, g_ref, mod_ref, dx_ref, accb, accg = refs

        @pl.when(i % per_b == 0)
        def _():
            accb[...] = jnp.zeros_like(accb)

        @pl.when(i == 0)
        def _():
            accg[...] = jnp.zeros_like(accg)

        dx = _norm_bwd_rows(x_ref[...], dhv, dres_ref[...], g_ref[...], mod_ref, i_scale, accb, accg)
        dx_ref[...] = dx
        if has_res:
            accb[0, 2:3, :] += jnp.sum(dx * mx_ref[...].astype(F32), axis=0, keepdims=True)
            dmx_ref[...] = (dx * mod_ref[0, i_gate:i_gate + 1, :]).astype(BF16)

    tok = pl.BlockSpec((tm, D), lambda i, j, k: (i, 0))
    res = [mixed] if has_res else []
    return dict(
        fn=fn, ins=[x, dres] + res + [g, mod3],
        in_specs=[tok, tok] + [tok] * len(res) + [pl.BlockSpec((1, D), lambda i, j, k: (0, 0)), _tail_mod_spec(per_b)],
        outs=[jax.ShapeDtypeStruct((t, D), F32)] + [jax.ShapeDtypeStruct((t, D), BF16)] * len(res)
        + [jax.ShapeDtypeStruct((bsz, 8, D), F32), jax.ShapeDtypeStruct((8, D), F32)],
        out_specs=[tok] * (1 + len(res)) + [_tail_batch_spec(per_b), pl.BlockSpec((8, D), lambda i, j, k: (0, 0))])


def _tail_mod_spec(per_b):
    return pl.BlockSpec((1, 6, D), lambda i, j, k: (i // per_b, 0, 0))


def _tail_batch_spec(per_b):
    return pl.BlockSpec((1, 8, D), lambda i, j, k: (i // per_b, 0, 0))


def _loss_tail(x1, tgt, mod3, tm):
    bsz, t = mod3.shape[0], x1.shape[0]
    per_b = t // bsz // tm

    def fn(ffv, i, x1_ref, tg_ref, mod_ref, dy_ref, dff_ref, accb, accl):
        @pl.when(i % per_b == 0)
        def _():
            accb[...] = jnp.zeros_like(accb)

        @pl.when(i == 0)
        def _():
            accl[...] = jnp.zeros_like(accl)

        gate = mod_ref[0, 5:6, :]
        err = x1_ref[...] + gate * ffv - tg_ref[...]
        accl[0:1, :] += jnp.sum(err * err, axis=0, keepdims=True) * (0.5 / D)
        dy = err * (1.0 / D)
        dy_ref[...] = dy
        dff_ref[...] = (dy * gate).astype(BF16)
        accb[0, 0:1, :] += jnp.sum(dy * ffv, axis=0, keepdims=True)

    tok = pl.BlockSpec((tm, D), lambda i, j, k: (i, 0))
    return dict(
        fn=fn, ins=[x1, tgt, mod3], in_specs=[tok, tok, _tail_mod_spec(per_b)],
        outs=[jax.ShapeDtypeStruct((t, D), F32), jax.ShapeDtypeStruct((t, D), BF16),
              jax.ShapeDtypeStruct((bsz, 8, D), F32), jax.ShapeDtypeStruct((8, D), F32)],
        out_specs=[tok, tok, _tail_batch_spec(per_b), pl.BlockSpec((8, D), lambda i, j, k: (0, 0))])


def _merge_fwd(proj, b_merge, y_a, y_b):
    t = proj.shape[0]
    tr = _tile(t, 512)

    def body(la_ref, lb_ref, bm_ref, ya_ref, yb_ref, mix_ref):
        ga = _sigmoid(la_ref[...] + bm_ref[:, 0:D])
        gb = _sigmoid(lb_ref[...] + bm_ref[:, D:2 * D])
        mix_ref[...] = (ga * ya_ref[...].astype(F32) + gb * yb_ref[...].astype(F32)).astype(BF16)

    tok = pl.BlockSpec((tr, D), lambda i: (i, 0))
    return pl.pallas_call(
        body, name="merge_fwd", grid=(t // tr,),
        in_specs=[pl.BlockSpec((tr, D), lambda i: (i, 3)), pl.BlockSpec((tr, D), lambda i: (i, 4)),
                  pl.BlockSpec((1, 2 * D), lambda i: (0, 0)), tok, tok],
        out_specs=tok, out_shape=jax.ShapeDtypeStruct((t, D), BF16),
        compiler_params=_cparams(("arbitrary",)),
    )(proj, proj, b_merge, y_a, y_b)


def _merge_bwd_tail(proj, b_merge, y_a, y_b, tm):
    t = proj.shape[0]

    def fn(dm, i, la_ref, lb_ref, bm_ref, ya_ref, yb_ref, dya_ref, dyb_ref, dl_ref, acc):
        @pl.when(i == 0)
        def _():
            acc[...] = jnp.zeros_like(acc)

        ga = _sigmoid(la_ref[...] + bm_ref[:, 0:D])
        gb = _sigmoid(lb_ref[...] + bm_ref[:, D:2 * D])
        dya_ref[...] = (dm * ga).astype(BF16)
        dyb_ref[...] = (dm * gb).astype(BF16)
        dla = dm * ya_ref[...].astype(F32) * ga * (1.0 - ga)
        dlb = dm * yb_ref[...].astype(F32) * gb * (1.0 - gb)
        dl_ref[:, 0:D] = dla.astype(BF16)
        dl_ref[:, D:2 * D] = dlb.astype(BF16)
        acc[0:1, 0:D] += jnp.sum(dla, axis=0, keepdims=True)
        acc[0:1, D:2 * D] += jnp.sum(dlb, axis=0, keepdims=True)

    tok = pl.BlockSpec((tm, D), lambda i, j, k: (i, 0))
    return dict(
        fn=fn, ins=[proj, proj, b_merge, y_a, y_b],
        in_specs=[pl.BlockSpec((tm, D), lambda i, j, k: (i, 3)), pl.BlockSpec((tm, D), lambda i, j, k: (i, 4)),
                  pl.BlockSpec((1, 2 * D), lambda i, j, k: (0, 0)), tok, tok],
        outs=[jax.ShapeDtypeStruct((t, D), BF16), jax.ShapeDtypeStruct((t, D), BF16),
              jax.ShapeDtypeStruct((t, 2 * D), BF16), jax.ShapeDtypeStruct((8, 2 * D), F32)],
        out_specs=[tok, tok, pl.BlockSpec((tm, 2 * D), lambda i, j, k: (i, 0)), pl.BlockSpec((8, 2 * D), lambda i, j, k: (0, 0))])


GLA_HEADS = 2
GLA_UNROLL = 16


def _log_sigmoid(z):
    return jnp.minimum(z, 0.0) - jnp.log(1.0 + jnp.exp(-jnp.abs(z)))


def _tri(lower):
    r = lax.broadcasted_iota(jnp.int32, (CHUNK, CHUNK), 0)
    c = lax.broadcasted_iota(jnp.int32, (CHUNK, CHUNK), 1)
    return jnp.where(r >= c if lower else r <= c, 1.0, 0.0).astype(F32)


def _gla_fwd(proj, wa_pad, b_alpha, g_out, bsz):
    t = proj.shape[0]
    s = t // bsz
    nc = s // CHUNK
    p, kw, vw = GLA_HEADS, GLA_HEADS * GDK, GLA_HEADS * GDV

    def body(q_ref, k_ref, v_ref, gg_ref, ms_ref, wa_ref, ba_ref, go_ref, o_ref, og_ref, st_ref, la, state):
        z = _dot(ms_ref[...].astype(BF16), wa_ref[...], NN) + ba_ref[...]
        la[...] = _log_sigmoid(z) * (1.0 / GTAU)
        state[...] = jnp.zeros_like(state)
        low = _tri(True)
        gout = go_ref[...]

        def chunk(n, carry):
            rows = pl.ds(pl.multiple_of(n * CHUNK, CHUNK), CHUNK)
            for hh in range(p):
                kc, vc = slice(hh * GDK, (hh + 1) * GDK), slice(hh * GDV, (hh + 1) * GDV)
                lac = la[rows, kc]
                cum = _dot(low, lac, NN, lax.Precision.HIGHEST)
                ce = jnp.sum(lac, axis=0, keepdims=True)
                kd = (k_ref[rows, kc].astype(F32) * jnp.exp(ce - cum)).astype(BF16)
                new = state[vc, :] * jnp.exp(ce) + _dot(v_ref[rows, vc].astype(BF16), kd, TN)
                state[vc, :] = new
                st_ref[pl.ds(pl.multiple_of((hh * nc + n) * GDV, GDV), GDV), :] = new.astype(BF16)
                qs = (q_ref[rows, kc].astype(F32) * (GDK ** -0.5)).astype(BF16)
                o = _dot(qs, new.astype(BF16), NT)
                o_ref[rows, vc] = o
                ro = lax.rsqrt(jnp.mean(o * o, axis=1, keepdims=True) + EPS)
                og_ref[rows, vc] = (((o * ro) * gout) * _silu(gg_ref[rows, vc].astype(F32))).astype(BF16)
            return carry

        lax.fori_loop(0, nc, chunk, 0, unroll=GLA_UNROLL)

    return pl.pallas_call(
        body, name="gla_fwd", grid=(bsz, GH // p),
        in_specs=[pl.BlockSpec((s, kw), lambda b, h: (b, h)), pl.BlockSpec((s, kw), lambda b, h: (b, GH // p + h)),
                  pl.BlockSpec((s, vw), lambda b, h: (b, GH // p + h)), pl.BlockSpec((s, vw), lambda b, h: (b, 2 * GH // p + h)),
                  pl.BlockSpec((s, LANES), lambda b, h: (b, 43)),
                  pl.BlockSpec((LANES, kw), lambda b, h: (0, h)), pl.BlockSpec((1, kw), lambda b, h: (0, h)),
                  pl.BlockSpec((1, GDV), lambda b, h: (0, 0))],
        out_specs=[pl.BlockSpec((s, vw), lambda b, h: (b, h)), pl.BlockSpec((s, vw), lambda b, h: (b, h)),
                   pl.BlockSpec((p * nc * GDV, GDK), lambda b, h: (b * (GH // p) + h, 0))],
        out_shape=[jax.ShapeDtypeStruct((t, GH * GDV), F32), jax.ShapeDtypeStruct((t, GH * GDV), BF16),
                   jax.ShapeDtypeStruct((bsz * GH * nc * GDV, GDK), BF16)],
        scratch_shapes=[pltpu.VMEM((s, kw), F32), pltpu.VMEM((vw, GDK), F32)],
        compiler_params=_cparams(("arbitrary", "arbitrary")),
    )(proj, proj, proj, proj, proj, wa_pad, b_alpha, g_out)


def _gla_bwd(dog, o, proj, wa_pad, b_alpha, g_out, states, bsz):
    t = proj.shape[0]
    s = t // bsz
    nc = s // CHUNK
    p, kw, vw = GLA_HEADS, GLA_HEADS * GDK, GLA_HEADS * GDV

    def body(dog_ref, o_ref, q_ref, k_ref, v_ref, gg_ref, ms_ref, wa_ref, ba_ref, go_ref, st_ref,
             dq_ref, dk_ref, dv_ref, dgg_ref, dz_ref, dba, dgo, zs, la, carry_g):
        @pl.when(pl.program_id(1) == 0)
        def _():
            dba[...] = jnp.zeros_like(dba)
            dgo[...] = jnp.zeros_like(dgo)

        z = _dot(ms_ref[...].astype(BF16), wa_ref[...], NN) + ba_ref[...]
        zs[...] = z
        la[...] = _log_sigmoid(z) * (1.0 / GTAU)
        carry_g[...] = jnp.zeros_like(carry_g)
        low, upp = _tri(True), _tri(False)
        gout = go_ref[...]
        last_row = lax.broadcasted_iota(jnp.int32, (CHUNK, GDK), 0) == CHUNK - 1

        def chunk(step, carry):
            n = nc - 1 - step
            rows = pl.ds(pl.multiple_of(n * CHUNK, CHUNK), CHUNK)
            for hh in range(p):
                kc, vc = slice(hh * GDK, (hh + 1) * GDK), slice(hh * GDV, (hh + 1) * GDV)
                lac = la[rows, kc]
                cum = _dot(low, lac, NN, lax.Precision.HIGHEST)
                ce = jnp.sum(lac, axis=0, keepdims=True)
                e = jnp.exp(ce - cum)
                dec = jnp.exp(ce)
                kf = k_ref[rows, kc].astype(F32)
                kd = (kf * e).astype(BF16)
                vv = v_ref[rows, vc].astype(BF16)
                qs = (q_ref[rows, kc].astype(F32) * (GDK ** -0.5)).astype(BF16)
                ov = o_ref[rows, vc]
                ro = lax.rsqrt(jnp.mean(ov * ov, axis=1, keepdims=True) + EPS)
                on = ov * ro
                gg = gg_ref[rows, vc].astype(F32)
                sg = _sigmoid(gg)
                dogv = dog_ref[rows, vc].astype(F32)
                dgg_ref[rows, vc] = (dogv * (on * gout) * (sg * (1.0 + gg * (1.0 - sg)))).astype(BF16)
                t1 = dogv * (gg * sg)
                dgo[8 * hh:8 * hh + 1, :] += jnp.sum(t1 * on, axis=0, keepdims=True)
                don = t1 * gout
                do = ro * (don - on * jnp.mean(don * on, axis=1, keepdims=True))
                dob = do.astype(BF16)
                st_n = st_ref[pl.ds(pl.multiple_of((hh * nc + n) * GDV, GDV), GDV), :]
                dq_ref[rows, kc] = (_dot(dob, st_n, NN) * (GDK ** -0.5)).astype(BF16)
                dn = carry_g[vc, :] + _dot(dob, qs, TN)
                prev = hh * nc + jnp.maximum(n - 1, 0)
                st_p = st_ref[pl.ds(pl.multiple_of(prev * GDV, GDV), GDV), :].astype(F32) * jnp.where(n > 0, 1.0, 0.0)
                ddec = jnp.sum(dn * st_p, axis=0, keepdims=True)
                dnb = dn.astype(BF16)
                dkd = _dot(vv, dnb, NN)
                dv_ref[rows, vc] = _dot(kd, dnb, NT).astype(BF16)
                dk_ref[rows, kc] = (dkd * e).astype(BF16)
                w = dkd * kf * e
                dce = jnp.sum(w, axis=0, keepdims=True) + ddec * dec
                dcum = jnp.where(last_row, dce - w, -w)
                dla = _dot(upp, dcum, NN, lax.Precision.HIGHEST)
                dz = dla * (1.0 / GTAU) * _sigmoid(-zs[rows, kc])
                dba[0:1, kc] += jnp.sum(dz, axis=0, keepdims=True)
                dz_ref[rows, kc] = dz.astype(BF16)
                carry_g[vc, :] = dn * dec
            return carry

        lax.fori_loop(0, nc, chunk, 0, unroll=GLA_UNROLL)

    hv = pl.BlockSpec((s, vw), lambda h, b: (b, h))
    hk = pl.BlockSpec((s, kw), lambda h, b: (b, h))
    return pl.pallas_call(
        body, name="gla_bwd", grid=(GH // p, bsz),
        in_specs=[hv, hv, hk, pl.BlockSpec((s, kw), lambda h, b: (b, GH // p + h)),
                  pl.BlockSpec((s, vw), lambda h, b: (b, GH // p + h)), pl.BlockSpec((s, vw), lambda h, b: (b, 2 * GH // p + h)),
                  pl.BlockSpec((s, LANES), lambda h, b: (b, 43)), pl.BlockSpec((LANES, kw), lambda h, b: (0, h)),
                  pl.BlockSpec((1, kw), lambda h, b: (0, h)), pl.BlockSpec((1, GDV), lambda h, b: (0, 0)),
                  pl.BlockSpec((p * nc * GDV, GDK), lambda h, b: (b * (GH // p) + h, 0))],
        out_specs=[hk, hk, hv, hv, hk, pl.BlockSpec((8, kw), lambda h, b: (0, h)),
                   pl.BlockSpec((8 * p, GDV), lambda h, b: (h, 0))],
        out_shape=[jax.ShapeDtypeStruct((t, GH * GDK), BF16), jax.ShapeDtypeStruct((t, GH * GDK), BF16),
                   jax.ShapeDtypeStruct((t, GH * GDV), BF16), jax.ShapeDtypeStruct((t, GH * GDV), BF16),
                   jax.ShapeDtypeStruct((t, GH * GDK), BF16), jax.ShapeDtypeStruct((8, GH * GDK), F32),
                   jax.ShapeDtypeStruct((8 * GH, GDV), F32)],
        scratch_shapes=[pltpu.VMEM((s, kw), F32), pltpu.VMEM((s, kw), F32), pltpu.VMEM((vw, GDK), F32)],
        compiler_params=_cparams(("arbitrary", "arbitrary")),
    )(dog, o, proj, proj, proj, proj, proj, wa_pad, b_alpha, g_out, states)


def _rope_tables(pos_ref, fr_ref, sg_ref):
    ang = pos_ref[...].astype(F32) * fr_ref[...]
    return jnp.cos(ang), jnp.sin(ang) * sg_ref[...]


def _partner(x):
    lane = lax.broadcasted_iota(jnp.int32, x.shape, 1)
    return jnp.where(lane < NOPE + ROPE // 2, pltpu.roll(x, LANES - ROPE // 2, 1), pltpu.roll(x, ROPE // 2, 1))


def _mla_rows(t):
    return _tile(t, 512)


def _mla_pre_fwd(proj, pos, fr, sg, q_lat_g, kv_lat_g, qn_g, kn_g, wuq, wukv):
    t = proj.shape[0]
    tr = _mla_rows(t)

    def body(cq_ref, ckv_ref, ms_ref, pos_ref, fr_ref, sg_ref, qlg, kvlg, qng, kng, wuq_ref, wukv_ref, q_out, k_out, v_out):
        lane = lax.broadcasted_iota(jnp.int32, (tr, HP), 1)
        real = jnp.where(lane < MQK, 1.0, 0.0)
        cos, sin = _rope_tables(pos_ref, fr_ref, sg_ref)
        cos = cos * real
        cq = cq_ref[...].astype(F32)
        cqn = (cq * lax.rsqrt(jnp.mean(cq * cq, axis=1, keepdims=True) + EPS) * qlg[...]).astype(BF16)
        ckv = ckv_ref[...].astype(F32)
        ckvn = (ckv * lax.rsqrt(jnp.mean(ckv * ckv, axis=1, keepdims=True) + EPS) * kvlg[...]).astype(BF16)
        kpe = jnp.where((lane >= NOPE) & (lane < MQK), ms_ref[...].astype(F32), 0.0)
        kpe = kpe + jnp.where(lane < MQK + ROPE // 2, pltpu.roll(kpe, ROPE, 1), 0.0)
        lane_all = lax.broadcasted_iota(jnp.int32, (tr, MH * HP), 1)
        v_out[...] = jnp.where(lane_all % HP == MV, 1.0, _dot(ckvn, wukv_ref[:, MH * HP:], NN)).astype(BF16)

        def norm_rope(x, gain):
            xn = x * lax.rsqrt(jnp.sum(x * x * real, axis=1, keepdims=True) * (1.0 / MQK) + EPS) * gain
            return (xn * cos + pltpu.roll(xn, LANES - ROPE // 2, 1) * sin).astype(BF16)

        for h in range(MH):
            cols = slice(h * HP, (h + 1) * HP)
            q_out[:, cols] = norm_rope(_dot(cqn, wuq_ref[:, cols], NN), qng[...])
            k_out[:, cols] = norm_rope(_dot(ckvn, wukv_ref[:, cols], NN) + kpe, kng[...])

    def full(a):
        return pl.BlockSpec(a.shape, lambda i: (0, 0))

    wide = pl.BlockSpec((tr, MH * HP), lambda i: (i, 0))
    return pl.pallas_call(
        body, name="mla_pre_fwd", grid=(t // tr,),
        in_specs=[pl.BlockSpec((tr, MQR), lambda i: (i, 20)), pl.BlockSpec((tr, MKVR), lambda i: (i, 42)),
                  pl.BlockSpec((tr, LANES), lambda i: (i, 43)), pl.BlockSpec((tr, 1), lambda i: (i, 0)),
                  full(fr), full(sg), full(q_lat_g), full(kv_lat_g), full(qn_g), full(kn_g), full(wuq), full(wukv)],
        out_specs=[wide, wide, wide],
        out_shape=[jax.ShapeDtypeStruct((t, MH * HP), BF16)] * 3,
        compiler_params=_cparams(("arbitrary",)),
    )(proj, proj, proj, pos, fr, sg, q_lat_g, kv_lat_g, qn_g, kn_g, wuq, wukv)


def _mla_pre_bwd(dq2, dk2, dv2, dmisc_gla, proj, pos, fr, sg, q_lat_g, kv_lat_g, qn_g, kn_g, wuq, wukv):
    t = proj.shape[0]
    tr = _mla_rows(t)

    def body(dq_ref, dk_ref, dv_ref, dmg_ref, cq_ref, ckv_ref, ms_ref, pos_ref, fr_ref, sg_ref, qlg, kvlg, qng, kng,
             wuq_ref, wukv_ref, dcq_ref, dckv_ref, dms_ref, dwuq, dwukv, acc, dqf, dkvf):
        @pl.when(pl.program_id(0) == 0)
        def _():
            dwuq[...] = jnp.zeros_like(dwuq)
            dwukv[...] = jnp.zeros_like(dwukv)
            acc[...] = jnp.zeros_like(acc)

        cos, sin = _rope_tables(pos_ref, fr_ref, sg_ref)
        cq = cq_ref[...].astype(F32)
        rc = lax.rsqrt(jnp.mean(cq * cq, axis=1, keepdims=True) + EPS)
        xc = cq * rc
        cqn = (xc * qlg[...]).astype(BF16)
        ckv = ckv_ref[...].astype(F32)
        rkv = lax.rsqrt(jnp.mean(ckv * ckv, axis=1, keepdims=True) + EPS)
        xkv = ckv * rkv
        ckvn = (xkv * kvlg[...]).astype(BF16)
        lane = lax.broadcasted_iota(jnp.int32, (tr, HP), 1)
        is_rope = (lane >= NOPE) & (lane < MQK)
        kpe = jnp.where(is_rope, ms_ref[...].astype(F32), 0.0)
        dkpe = jnp.zeros((tr, HP), F32)
        dqng = jnp.zeros((1, HP), F32)
        dkng = jnp.zeros((1, HP), F32)
        for h in range(MH):
            cols = slice(h * HP, (h + 1) * HP)
            qh = _dot(cqn, wuq_ref[:, cols], NN)
            rq = lax.rsqrt(jnp.sum(qh * qh, axis=1, keepdims=True) * (1.0 / MQK) + EPS)
            xq = qh * rq
            dy = dq_ref[:, cols].astype(F32)
            dqn = dy * cos - _partner(dy) * sin
            dqng += jnp.sum(dqn * xq, axis=0, keepdims=True)
            tq = dqn * qng[...]
            dqf[:, cols] = (rq * (tq - xq * (jnp.sum(tq * xq, axis=1, keepdims=True) * (1.0 / MQK)))).astype(BF16)
            kh = _dot(ckvn, wukv_ref[:, cols], NN) + kpe
            rk = lax.rsqrt(jnp.sum(kh * kh, axis=1, keepdims=True) * (1.0 / MQK) + EPS)
            xk = kh * rk
            dy = dk_ref[:, cols].astype(F32)
            dkn = dy * cos - _partner(dy) * sin
            dkng += jnp.sum(dkn * xk, axis=0, keepdims=True)
            tk = dkn * kng[...]
            dkh = rk * (tk - xk * (jnp.sum(tk * xk, axis=1, keepdims=True) * (1.0 / MQK)))
            dkvf[:, cols] = jnp.where(lane < NOPE, dkh, 0.0).astype(BF16)
            dkpe += jnp.where(is_rope, dkh, 0.0)
        dkvf[:, MH * HP:] = dv_ref[...]
        acc[2:3, 0:HP] += dqng
        acc[3:4, 0:HP] += dkng
        dms_ref[...] = (dmg_ref[...] + dkpe).astype(BF16)

        dqfv = dqf[...]
        dwuq[...] += _dot(cqn, dqfv, TN)
        dcqn = _dot(dqfv, wuq_ref[...], NT)
        acc[0:1, :] += jnp.sum(dcqn * xc, axis=0, keepdims=True)
        tc = dcqn * qlg[...]
        dcq_ref[...] = (rc * (tc - xc * jnp.mean(tc * xc, axis=1, keepdims=True))).astype(BF16)

        dkvfv = dkvf[...]
        dwukv[...] += _dot(ckvn, dkvfv, TN)
        dckvn = _dot(dkvfv, wukv_ref[...], NT)
        acc[1:2, 0:MKVR] += jnp.sum(dckvn * xkv, axis=0, keepdims=True)
        tkv = dckvn * kvlg[...]
        dckv_ref[...] = (rkv * (tkv - xkv * jnp.mean(tkv * xkv, axis=1, keepdims=True))).astype(BF16)

    def full(a):
        return pl.BlockSpec(a.shape, lambda i: (0, 0))

    wide = pl.BlockSpec((tr, MH * HP), lambda i: (i, 0))
    narrow = pl.BlockSpec((tr, LANES), lambda i: (i, 0))
    return pl.pallas_call(
        body, name="mla_pre_bwd", grid=(t // tr,),
        in_specs=[wide, wide, wide, narrow,
                  pl.BlockSpec((tr, MQR), lambda i: (i, 20)), pl.BlockSpec((tr, MKVR), lambda i: (i, 42)),
                  pl.BlockSpec((tr, LANES), lambda i: (i, 43)), pl.BlockSpec((tr, 1), lambda i: (i, 0)),
                  full(fr), full(sg), full(q_lat_g), full(kv_lat_g), full(qn_g), full(kn_g), full(wuq), full(wukv)],
        out_specs=[pl.BlockSpec((tr, MQR), lambda i: (i, 0)), narrow, narrow,
                   pl.BlockSpec((MQR, MH * HP), lambda i: (0, 0)), pl.BlockSpec((MKVR, 2 * MH * HP), lambda i: (0, 0)),
                   pl.BlockSpec((8, MQR), lambda i: (0, 0))],
        out_shape=[jax.ShapeDtypeStruct((t, MQR), BF16), jax.ShapeDtypeStruct((t, MKVR), BF16),
                   jax.ShapeDtypeStruct((t, LANES), BF16), jax.ShapeDtypeStruct((MQR, MH * HP), F32),
                   jax.ShapeDtypeStruct((MKVR, 2 * MH * HP), F32), jax.ShapeDtypeStruct((8, MQR), F32)],
        scratch_shapes=[pltpu.VMEM((tr, MH * HP), BF16), pltpu.VMEM((tr, 2 * MH * HP), BF16)],
        compiler_params=_cparams(("arbitrary",)),
    )(dq2, dk2, dv2, dmisc_gla, proj, proj, proj, pos, fr, sg, q_lat_g, kv_lat_g, qn_g, kn_g, wuq, wukv)


ATT_FWD_TILES = (1024, 512)
ATT_BWD_TILES = (512, 512)
ATT_HEADS = 2
ATT_FWD_HEADS = 4
NEG = -1e30
LOG2E = 1.4426950408889634


def _att_mask(q0, k0, tq, tk):
    qc = (q0 + lax.broadcasted_iota(jnp.int32, (tq, tk), 0)) // CHUNK
    kc = (k0 + lax.broadcasted_iota(jnp.int32, (tq, tk), 1)) // CHUNK
    return kc <= qc


def _att_tiles(s, tiles):
    return _tile(s, tiles[0]), _tile(s, tiles[1])


def _lanes(x, n):
    return x if n == 1 else jnp.concatenate([x] * n, axis=1)


def _grid_ends(grid):
    i, j = pl.program_id(0), pl.program_id(1)
    return (i == 0) & (j == 0), (i == grid[0] - 1) & (j == grid[1] - 1)


def _attn_fwd(q2, k2, v2, bsz, slab):
    t = q2.shape[0]
    s = t // bsz
    tq, tk = _att_tiles(s, ATT_FWD_TILES)
    nq, groups, n_diag = s // tq, tk // HP, max(tq // tk, 1)
    sub_rows = tq // n_diag
    scale = MQK ** -0.5
    c2 = scale * LOG2E
    heads = range(ATT_FWD_HEADS)

    def body(q_ref, k_ref, v_ref, slab_ref, o_ref, lse_ref, gath_ref, send_sems, recv_sems, local_sem):
        gather = _core_row_gather_copies(slab_ref, gath_ref, send_sems, recv_sems, local_sem)
        first, last = _grid_ends((bsz, MH // ATT_FWD_HEADS))
        pl.when(first)(gather.start)

        def q_loop(qi, carry):
            q0 = pl.multiple_of(qi * tq, tq)
            rows = pl.ds(q0, tq)
            n_full = q0 // tk
            qs = [q_ref[rows, h * HP:(h + 1) * HP] for h in heads]

            def scores(h, kj, sub=None, masked=False):
                k0 = pl.multiple_of(kj * tk, tk)
                qv = qs[h] if sub is None else qs[h][sub * sub_rows:(sub + 1) * sub_rows]
                sc = _dot(qv, k_ref[pl.ds(k0, tk), h * HP:(h + 1) * HP], NT)
                return jnp.where(_att_mask(q0 + sub * sub_rows, k0, sub_rows, tk), sc, NEG) if masked else sc

            def fold(mx, sc):
                for j in range(groups):
                    mx = jnp.maximum(mx, sc[:, j * HP:(j + 1) * HP])
                return mx

            def over_diagonal(vals, step):
                out = []
                for h in heads:
                    blocks = []
                    for r in range(n_diag):
                        v = vals[h][r * sub_rows:(r + 1) * sub_rows]
                        for u in range(r + 1):
                            v = step(v, h, n_full + u, r, u == r)
                        blocks.append(v)
                    out.append(blocks[0] if n_diag == 1 else jnp.concatenate(blocks, axis=0))
                return tuple(out)

            mx = lax.fori_loop(0, n_full, lambda kj, mx: tuple(fold(mx[h], scores(h, kj)) for h in heads),
                               tuple(jnp.full((tq, HP), NEG, F32) for _ in heads))
            mx = over_diagonal(mx, lambda v, h, kj, r, masked: fold(v, scores(h, kj, r, masked)))
            mb = [jnp.broadcast_to(jnp.max(mx[h], axis=1, keepdims=True), (tq, HP)) for h in heads]

            def weighted(h, kj, sub=None, masked=False):
                m = mb[h] if sub is None else mb[h][sub * sub_rows:(sub + 1) * sub_rows]
                p = jnp.exp2((scores(h, kj, sub, masked) - _lanes(m, groups)) * c2)
                k0 = pl.multiple_of(kj * tk, tk)
                return _dot(p.astype(BF16), v_ref[pl.ds(k0, tk), h * HP:(h + 1) * HP], NN)

            acc = lax.fori_loop(0, n_full, lambda kj, acc: tuple(acc[h] + weighted(h, kj) for h in heads),
                                tuple(jnp.zeros((tq, HP), F32) for _ in heads))
            acc = over_diagonal(acc, lambda v, h, kj, r, masked: v + weighted(h, kj, r, masked))
            lane = lax.broadcasted_iota(jnp.int32, (tq, HP), 1)
            for h in heads:
                a = acc[h]
                l = jnp.sum(jnp.where(lane == MV, a, 0.0), axis=1, keepdims=True)
                o_ref[rows, h * HP:(h + 1) * HP] = (a / l).astype(BF16)
                lse_ref[rows, h * HP:(h + 1) * HP] = mb[h] * scale + jnp.log(l)
            return carry

        lax.fori_loop(0, nq, q_loop, 0)
        pl.when(last)(gather.finish)

    spec = pl.BlockSpec((s, ATT_FWD_HEADS * HP), lambda b, h: (b, h))
    return pl.pallas_call(
        body, name="attn_fwd", grid=(bsz, MH // ATT_FWD_HEADS), in_specs=[spec] * 3 + [HBM_SPEC],
        out_specs=[spec, spec, HBM_SPEC],
        out_shape=[jax.ShapeDtypeStruct((t, MH * HP), BF16), jax.ShapeDtypeStruct((t, MH * HP), F32),
                   jax.ShapeDtypeStruct((N_DEV,) + slab.shape, slab.dtype)],
        scratch_shapes=EXCHANGE_SEMS, compiler_params=_cparams(("arbitrary", "arbitrary")),
    )(q2, k2, v2, slab)


def _attn_bwd(q2, k2, v2, do2, o2, lse2, bsz, tsum):
    t = q2.shape[0]
    s = t // bsz
    tq, tk = _att_tiles(s, ATT_BWD_TILES)
    nq, nk, per, groups = s // tq, s // tk, max(tk // tq, 1), tk // HP
    scale = MQK ** -0.5
    c2 = scale * LOG2E
    heads = range(ATT_HEADS)

    def body(q_ref, k_ref, v_ref, do_ref, o_ref, lse_ref, t_ref, dq_ref, dk_ref, dv_ref, parts_ref, dq_acc, delta, lse_b2,
             send_sems, recv_sems, local_sem):
        exchange = _all_to_all_copies(t_ref, parts_ref, send_sems, recv_sems, local_sem)
        first, last = _grid_ends((bsz, MH // ATT_HEADS))
        pl.when(first)(exchange.start)
        dq_acc[...] = jnp.zeros_like(dq_acc)

        def d_loop(i, carry):
            rows = pl.ds(pl.multiple_of(i * tq, tq), tq)
            for h in heads:
                hs = slice(h * HP, (h + 1) * HP)
                dl = jnp.sum(do_ref[rows, hs].astype(F32) * o_ref[rows, hs].astype(F32), axis=1, keepdims=True)
                delta[rows, hs] = jnp.broadcast_to(dl, (tq, HP))
            lse_b2[rows, :] = lse_ref[rows, :] * LOG2E
            return carry

        lax.fori_loop(0, nq, d_loop, 0)

        def k_loop(kj, carry):
            k0 = pl.multiple_of(kj * tk, tk)
            kk = [k_ref[pl.ds(k0, tk), h * HP:(h + 1) * HP] for h in heads]
            vv = [v_ref[pl.ds(k0, tk), h * HP:(h + 1) * HP] for h in heads]

            def tile(qi, c, masked):
                q0 = pl.multiple_of(qi * tq, tq)
                rows = pl.ds(q0, tq)
                out = []
                for h in heads:
                    hs = slice(h * HP, (h + 1) * HP)
                    dk, dv = c[h]
                    q = q_ref[rows, hs]
                    do = do_ref[rows, hs]
                    e = _dot(q, kk[h], NT) * c2 - _lanes(lse_b2[rows, hs], groups)
                    if masked:
                        e = jnp.where(_att_mask(q0, k0, tq, tk), e, NEG)
                    p = jnp.exp2(e)
                    dv = dv + _dot(p.astype(BF16), do, TN)
                    ds = (p * (_dot(do, vv[h], NT) - _lanes(delta[rows, hs], groups))).astype(BF16)
                    dq_acc[rows, hs] += _dot(ds, kk[h], NN)
                    dk = dk + _dot(ds, q, TN)
                    out.append((dk, dv))
                return tuple(out)

            zero = jnp.zeros((tk, HP), F32)
            c = tuple((zero, zero) for _ in heads)
            first = k0 // tq
            for u in range(per):
                c = tile(first + u, c, True)
            c = lax.fori_loop(first + per, nq, lambda qi, c: tile(qi, c, False), c)
            for h in heads:
                dk_ref[pl.ds(k0, tk), h * HP:(h + 1) * HP] = (c[h][0] * scale).astype(BF16)
                dv_ref[pl.ds(k0, tk), h * HP:(h + 1) * HP] = c[h][1].astype(BF16)
            return carry

        lax.fori_loop(0, nk, k_loop, 0)
        dq_ref[...] = (dq_acc[...] * scale).astype(BF16)
        pl.when(last)(exchange.finish)

    spec = pl.BlockSpec((s, ATT_HEADS * HP), lambda b, h: (b, h))
    return pl.pallas_call(
        body, name="attn_bwd", grid=(bsz, MH // ATT_HEADS), in_specs=[spec] * 6 + [HBM_SPEC],
        out_specs=[spec] * 3 + [HBM_SPEC],
        out_shape=[jax.ShapeDtypeStruct((t, MH * HP), BF16)] * 3 + [jax.ShapeDtypeStruct(tsum.shape, tsum.dtype)],
        scratch_shapes=[pltpu.VMEM((s, ATT_HEADS * HP), F32)] * 3 + EXCHANGE_SEMS,
        compiler_params=_cparams(("arbitrary", "arbitrary")),
    )(q2, k2, v2, do2, o2, lse2, tsum)


def _perm_w_in_t(w):
    z = lambda n: jnp.zeros((n, w.shape[1]), w.dtype)
    return jnp.concatenate([w[:3072], w[3504:5552], w[3088:3344], w[3344:3472], w[3072:3088], z(48), w[3472:3504], z(32)],
                           axis=0)


def _unperm_w_in_t(g):
    return jnp.concatenate([g[:3072], g[5504:5520], g[5120:5376], g[5376:5504], g[5568:5600], g[3072:5120]], axis=0)


def _pad_wa(w):
    return jnp.pad(w, ((0, LANES - GLR), (0, 0)))


def _pad_wuq(w):
    return jnp.pad(w.reshape(MQR, MH, MQK), ((0, 0), (0, 0), (0, HP - MQK))).reshape(MQR, MH * HP)


def _unpad_wuq(g):
    return g.reshape(MQR, MH, HP)[:, :, :MQK].reshape(MQR, MH * MQK)


def _pad_wukv(w):
    w3 = w.reshape(MKVR, MH, NOPE + MV)
    kp = jnp.pad(w3[:, :, :NOPE], ((0, 0), (0, 0), (0, HP - NOPE))).reshape(MKVR, MH * HP)
    vp = jnp.pad(w3[:, :, NOPE:], ((0, 0), (0, 0), (0, HP - MV))).reshape(MKVR, MH * HP)
    return jnp.concatenate([kp, vp], axis=1)


def _unpad_wukv(g):
    kp = g[:, :MH * HP].reshape(MKVR, MH, HP)[:, :, :NOPE]
    vp = g[:, MH * HP:].reshape(MKVR, MH, HP)[:, :, :MV]
    return jnp.concatenate([kp, vp], axis=2).reshape(MKVR, MH * (NOPE + MV))


def _pad_wo(w):
    return jnp.pad(w.reshape(MH, MV, D), ((0, 0), (0, HP - MV), (0, 0))).reshape(MH * HP, D)


def _unpad_wo(g):
    return g.reshape(MH, HP, D)[:, :MV, :].reshape(MH * MV, D)


def _repeat_half(a):
    a3 = a.reshape(a.shape[0], -1, HP)
    a3 = jnp.concatenate([a3[:, :, :MQK], a3[:, :, NOPE:NOPE + ROPE // 2], a3[:, :, MQK + ROPE // 2:]], axis=2)
    return a3.reshape(a.shape)


def _pad_lanes(v, n=HP):
    return jnp.pad(v, ((0, 0), (0, n - v.shape[1])))


def _local_step(x, positions, tgt, wt, slab_b, sp, mod3):
    bsz, s, _ = x.shape
    t = bsz * s
    x2 = x.reshape(t, D)
    tgt2 = tgt.reshape(t, D)
    pos = positions.reshape(t, 1)
    fr16 = ROPE_THETA ** (-jnp.arange(0, ROPE, 2, dtype=F32) / ROPE)
    zero = lambda n: jnp.zeros((n,), F32)
    fr = jnp.concatenate([zero(NOPE), fr16, fr16, zero(HP - MQK)]).reshape(1, HP)
    sg = jnp.concatenate([zero(NOPE), -jnp.ones((ROPE // 2,), F32), jnp.ones((ROPE // 2,), F32), zero(HP - MQK)]).reshape(1, HP)

    w_in_t = _perm_w_in_t(wt["w_in"])
    wa_pad = _pad_wa(wt["gla_w_alpha"].T)
    wuq = _pad_wuq(wt["mla_w_uq"].T)
    wukv = _pad_wukv(wt["mla_w_ukv"].T)
    qn_g, kn_g = _pad_lanes(sp["mla_qn_g"]), _pad_lanes(sp["mla_kn_g"])

    tm = _tile(s, 1024)
    h = _norm_mod_fwd(x2, sp["norm1_g"], mod3, 0, 1, "norm1_fwd")
    proj = _mm(h, w_in_t, "nt", (BF16,), "proj_fwd", tn=PW // 4)
    o_gla, og, states = _gla_fwd(proj, wa_pad, sp["gla_b_alpha"], sp["gla_out_norm_g"], bsz)
    q2, k2, v2 = _mla_pre_fwd(proj, pos, fr, sg, sp["mla_q_lat_g"], sp["mla_kv_lat_g"], _repeat_half(qn_g), _repeat_half(kn_g),
                              _repeat_half(wuq), wukv)
    o2, lse2, core_row = _attn_fwd(q2, k2, v2, bsz, slab_b)
    wt = dict(wt, **_unpack_gathered(_cross_core_fill(core_row), SLAB_B))
    wo_pad = _pad_wo(wt["mla_w_o"])
    y_a = _mm(og, wt["gla_w_o"], "nn", (BF16,), "gla_out_fwd")
    y_b = _mm(o2, wo_pad, "nn", (BF16,), "mla_out_fwd")
    mix = _merge_fwd(proj, sp["b_merge"], y_a, y_b)
    mixed, x1, h2 = _mm(mix, wt["w_out"], "nn", (), "w_out_fwd", tm=tm,
                        tail=_norm_fwd_tail(x2, sp["norm2_g"], mod3, 3, 4, 2, tm))
    a, f = _mm(h2, wt["mlp_w1"], "nt", (BF16, BF16), "mlp1_fwd",
               epi=lambda acc: (acc, jnp.square(jnp.maximum(acc, 0.0))))
    dy, dff, acc_g2, acc_loss = _mm(f, wt["mlp_w2"], "nn", (), "mlp2_fwd", tm=tm, tail=_loss_tail(x1, tgt2, mod3, tm))

    gw = {}
    gw["mlp_w2"] = _mm(f, dff, "tn", (BF16,), "mlp2_dw")
    da = _mm(dff, wt["mlp_w2"], "nt", (BF16,), "mlp2_dx", extras=(a,),
             epi=lambda acc, av: (acc * (2.0 * jnp.maximum(av.astype(F32), 0.0)),))
    gw["mlp_w1"] = _mm(da, h2, "tn", (BF16,), "mlp1_dw")
    dx1, dmixed, accb2, accg2 = _mm(da, wt["mlp_w1"], "nn", (), "mlp1_dx", tm=tm,
                                    tail=_norm_bwd_tail(x1, dy, sp["norm2_g"], mod3, 4, tm, mixed=mixed, i_gate=2))

    gw["w_out"] = _mm(mix, dmixed, "tn", (BF16,), "w_out_dw")
    dy_a, dy_b, dlogits, acc_bm = _mm(dmixed, wt["w_out"], "nt", (), "w_out_dx", tm=tm,
                                      tail=_merge_bwd_tail(proj, sp["b_merge"], y_a, y_b, tm))
    gw["gla_w_o"] = _mm(og, dy_a, "tn", (BF16,), "gla_out_dw")
    dog = _mm(dy_a, wt["gla_w_o"], "nt", (BF16,), "gla_out_dx")
    gw["mla_w_o"] = _unpad_wo(_mm(o2, dy_b, "tn", (BF16,), "mla_out_dw"))
    do2 = _mm(dy_b, wo_pad, "nt", (BF16,), "mla_out_dx")
    dq2, dk2, dv2, parts_b = _attn_bwd(q2, k2, v2, do2, o2, lse2, bsz, _pack_per_device(gw, SLAB_B))
    dq_g, dk_g, dv_g, dgg, dz, acc_ba, acc_go = _gla_bwd(dog, o_gla, proj, wa_pad, sp["gla_b_alpha"],
                                                         sp["gla_out_norm_g"], states, bsz)
    gw["gla_w_alpha"] = _mm(proj, dz, "tn", (F32,), "gla_alpha_dw", a_off=43, m=LANES)[:GLR].T.astype(BF16)
    dmisc_gla = _mm(dz, wa_pad, "nt", (F32,), "gla_alpha_dx")
    dcq, dckv, dmisc, gwuq, gwukv, acc_mla = _mla_pre_bwd(dq2, dk2, dv2, dmisc_gla, proj, pos, fr, sg, sp["mla_q_lat_g"],
                                                         sp["mla_kv_lat_g"], qn_g, kn_g, wuq, wukv)
    gw["mla_w_uq"] = _unpad_wuq(gwuq).T.astype(BF16)
    gw["mla_w_ukv"] = _unpad_wukv(gwukv).T.astype(BF16)
    dproj = jnp.concatenate([dq_g, dk_g, dv_g, dgg, dlogits, dcq, dckv, dmisc], axis=1)
    gw["w_in"] = _unperm_w_in_t(_mm(dproj, h, "tn", (BF16,), "proj_dw", tm=PW // 4))
    parts_a, grad_x, accb1, accg1 = _mm(dproj, w_in_t, "nn", (), "proj_dx", tm=tm, tk=PW // 4,
                                        cargo=_sum_over_cores(_pack_per_device(gw, SLAB_A), "a"),
                                        tail=_norm_bwd_tail(x2, dx1, sp["norm1_g"], mod3, 1, tm))

    dmod = jnp.stack([accb1[:, 0], accb1[:, 1], accb2[:, 2], accb2[:, 0], accb2[:, 1], acc_g2[:, 0]], axis=1)

    rows = {
        "dmod": dmod.reshape(bsz * 6, D),
        "norm1_g": accg1[0:1], "norm2_g": accg2[0:1],
        "b_merge": acc_bm[0:1].reshape(2, D),
        "gla_b_alpha": _pad_lanes(acc_ba[0:1], D),
        "gla_out_norm_g": _pad_lanes(acc_go.reshape(GH, 8, GDV)[:, 0, :], D),
        "mla_q_lat_g": _pad_lanes(acc_mla[0:1], D), "mla_kv_lat_g": _pad_lanes(acc_mla[1:2], D),
        "mla_qn_g": _pad_lanes(acc_mla[2:3], D), "mla_kn_g": _pad_lanes(acc_mla[3:4], D),
        "loss": acc_loss[0:1],
    }
    return grad_x.reshape(bsz, s, D), parts_a, parts_b, rows


HBM_SPEC = pl.BlockSpec(memory_space=pltpu.HBM)


def _all_gather(ps, name):
    n = len(ps)

    def body(*refs):
        p_refs, out_refs, (send_sems, recv_sems, local_sems) = refs[:n], refs[n:2 * n], refs[2 * n:]
        x, y, c = lax.axis_index("x"), lax.axis_index("y"), lax.axis_index("c")
        me, sibling = (x, y, c), (x, y, 1 - c)
        chips = [(1 - x, y), (x, 1 - y), (1 - x, 1 - y)]

        def copy(a, k, block, to, own=False):
            slot = out_refs[a].at[4 * block[0] + 2 * block[1] + block[2]]
            return pltpu.make_async_remote_copy(
                src_ref=p_refs[a] if own else slot, dst_ref=slot, send_sem=send_sems.at[7 * a + k],
                recv_sem=recv_sems.at[7 * a + k], device_id=to, device_id_type=MESH)

        mine = [pltpu.make_async_copy(p_refs[a], out_refs[a].at[4 * x + 2 * y + c], local_sems.at[a]) for a in range(n)]
        first = [copy(a, 0, me, sibling, own=True) for a in range(n)]
        first += [copy(a, 1 + j, me, (*chip, c), own=True) for a in range(n) for j, chip in enumerate(chips)]
        for cp in mine + first:
            cp.start()
        passed = []
        for j, chip in enumerate(chips):
            for a in range(n):
                copy(a, 1 + j, (*chip, c), me).wait_recv()
                passed.append(copy(a, 4 + j, (*chip, c), sibling))
                passed[-1].start()
        for a in range(n):
            copy(a, 0, sibling, me).wait_recv()
            for j, chip in enumerate(chips):
                copy(a, 4 + j, (*chip, 1 - c), me).wait_recv()
        for cp in first + passed:
            cp.wait_send()
        for cp in mine:
            cp.wait()

    return pl.pallas_call(
        body, name=name, out_shape=[jax.ShapeDtypeStruct((N_DEV,) + p.shape, p.dtype) for p in ps],
        in_specs=[HBM_SPEC] * n, out_specs=[HBM_SPEC] * n,
        scratch_shapes=[pltpu.SemaphoreType.DMA((7 * n,)), pltpu.SemaphoreType.DMA((7 * n,)), pltpu.SemaphoreType.DMA((n,))],
    )(*ps)


def _sibling_exchange(g, name):
    def body(g_ref, out_ref, send_sem, recv_sem):
        x, y, c = lax.axis_index("x"), lax.axis_index("y"), lax.axis_index("c")
        cp = pltpu.make_async_remote_copy(src_ref=g_ref, dst_ref=out_ref, send_sem=send_sem, recv_sem=recv_sem,
                                          device_id=(x, y, 1 - c), device_id_type=MESH)
        cp.start()
        cp.wait()

    return pl.pallas_call(
        body, name=name, out_shape=jax.ShapeDtypeStruct(g.shape, g.dtype),
        in_specs=[HBM_SPEC], out_specs=HBM_SPEC,
        scratch_shapes=[pltpu.SemaphoreType.DMA(()), pltpu.SemaphoreType.DMA(())],
    )(g)


class _Exchange:
    def __init__(self, local, sends, arrivals):
        self.local, self.sends, self.arrivals = local, sends, arrivals

    def start(self):
        self.local.start()
        for cp in self.sends:
            cp.start()

    def finish(self):
        for cp in self.arrivals:
            cp.wait_recv()
        for cp in self.sends:
            cp.wait_send()
        self.local.wait()


EXCHANGE_SEMS = [pltpu.SemaphoreType.DMA((N_DEV,)), pltpu.SemaphoreType.DMA((N_DEV,)), pltpu.SemaphoreType.DMA(())]


def _all_to_all_copies(t_ref, out_ref, send_sems, recv_sems, local_sem):
    x, y, c = lax.axis_index("x"), lax.axis_index("y"), lax.axis_index("c")
    me = 4 * x + 2 * y + c

    def copy(k, src, dst):
        px, py, pc = (1 - x if k & 4 else x), (1 - y if k & 2 else y), (1 - c if k & 1 else c)
        peer = 4 * px + 2 * py + pc
        return pltpu.make_async_remote_copy(src_ref=t_ref.at[peer if src is None else src],
                                            dst_ref=out_ref.at[peer if dst is None else dst], send_sem=send_sems.at[k],
                                            recv_sem=recv_sems.at[k], device_id=(px, py, pc), device_id_type=MESH)

    return _Exchange(pltpu.make_async_copy(t_ref.at[me], out_ref.at[me], local_sem),
                     [copy(k, None, me) for k in range(1, N_DEV)], [copy(k, me, None) for k in range(1, N_DEV)])


def _chip_exchange_copies(t_ref, out_ref, send_sems, recv_sems, local_sem):
    x, y, c = lax.axis_index("x"), lax.axis_index("y"), lax.axis_index("c")
    my_chip = 2 * x + y
    chips = [(1 - x, y), (x, 1 - y), (1 - x, 1 - y)]

    def copy(j, src, dst, px, py):
        return pltpu.make_async_remote_copy(src_ref=t_ref.at[src], dst_ref=out_ref.at[dst], send_sem=send_sems.at[j],
                                            recv_sem=recv_sems.at[j], device_id=(px, py, c), device_id_type=MESH)

    return _Exchange(pltpu.make_async_copy(t_ref.at[my_chip], out_ref.at[my_chip], local_sem),
                     [copy(j, 2 * px + py, my_chip, px, py) for j, (px, py) in enumerate(chips)],
                     [copy(j, my_chip, 2 * px + py, px, py) for j, (px, py) in enumerate(chips)])


def _core_row_gather_copies(p_ref, out_ref, send_sems, recv_sems, local_sem):
    x, y, c = lax.axis_index("x"), lax.axis_index("y"), lax.axis_index("c")
    peers = [(x, y, 1 - c), (1 - x, y, c), (x, 1 - y, c), (1 - x, 1 - y, c)]

    def slot(px, py, pc):
        return out_ref.at[4 * px + 2 * py + pc]

    def copy(j, block, to):
        return pltpu.make_async_remote_copy(src_ref=p_ref, dst_ref=slot(*block), send_sem=send_sems.at[j],
                                            recv_sem=recv_sems.at[j], device_id=to, device_id_type=MESH)

    return _Exchange(pltpu.make_async_copy(p_ref, slot(x, y, c), local_sem),
                     [copy(j, (x, y, c), peer) for j, peer in enumerate(peers)],
                     [copy(j, peer, peer) for j, peer in enumerate(peers)])


def _cross_core_fill(gathered):
    def body(g_ref, out_ref, send_sems, recv_sems):
        x, y, c = lax.axis_index("x"), lax.axis_index("y"), lax.axis_index("c")
        chips = [(1 - x, y), (x, 1 - y), (1 - x, 1 - y)]

        def copy(j, pc):
            px, py = chips[j]
            slot = 4 * px + 2 * py + pc
            return pltpu.make_async_remote_copy(src_ref=g_ref.at[slot], dst_ref=out_ref.at[slot], send_sem=send_sems.at[j],
                                                recv_sem=recv_sems.at[j], device_id=(x, y, 1 - c), device_id_type=MESH)

        sends = [copy(j, c) for j in range(3)]
        for cp in sends:
            cp.start()
        for j in range(3):
            copy(j, 1 - c).wait_recv()
        for cp in sends:
            cp.wait_send()

    return pl.pallas_call(
        body, name="weights_cross_core_fill", out_shape=jax.ShapeDtypeStruct(gathered.shape, gathered.dtype),
        in_specs=[HBM_SPEC], out_specs=HBM_SPEC, input_output_aliases={0: 0},
        scratch_shapes=[pltpu.SemaphoreType.DMA((3,)), pltpu.SemaphoreType.DMA((3,))],
    )(gathered)


def _slab_block(r):
    return max(b for b in range(16, SLAB_BLOCK_MAX + 1, 16) if r % b == 0)


def _pair_sum(a, b, name):
    n, r, cdim = a.shape
    rb = _slab_block(r)
    blk = pl.BlockSpec((1, rb, cdim), lambda j, i: (j, i, 0))

    def body(a_ref, b_ref, o_ref):
        o_ref[...] = (a_ref[...].astype(F32) + b_ref[...].astype(F32)).astype(BF16)

    return pl.pallas_call(
        body, name=name, grid=(n, r // rb), in_specs=[blk, blk], out_specs=blk,
        out_shape=jax.ShapeDtypeStruct(a.shape, BF16), compiler_params=_cparams(("arbitrary", "arbitrary")),
    )(a, b)


def _adamw_math(w, g, m, v):
    m = ADAM_B1 * m + (1.0 - ADAM_B1) * g
    v = ADAM_B2 * v + (1.0 - ADAM_B2) * jnp.square(g)
    m_hat = m / (1.0 - ADAM_B1 ** ADAM_STEP)
    v_hat = v / (1.0 - ADAM_B2 ** ADAM_STEP)
    delta = -ADAM_LR * (m_hat / (jnp.sqrt(v_hat) + ADAM_EPS) + ADAM_WD * w)
    return delta, m, v


def _slab_sum(parts, name):
    n, r, cdim = parts.shape
    rb = _slab_block(r)
    blk = pl.BlockSpec((rb, cdim), lambda i: (i, 0))

    def body(p_ref, g_out):
        g = p_ref[0].astype(F32)
        for j in range(1, n):
            g = g + p_ref[j].astype(F32)
        g_out[...] = g

    return pl.pallas_call(
        body, name=name, grid=(r // rb,),
        in_specs=[pl.BlockSpec((n, rb, cdim), lambda i: (0, i, 0))], out_specs=blk,
        out_shape=jax.ShapeDtypeStruct((r, cdim), F32), compiler_params=_cparams(("arbitrary",)),
    )(parts)


def _adamw(g, w, m, v, name):
    r, cdim = w.shape
    rb = _tile(r, 256)
    blk = pl.BlockSpec((rb, cdim), lambda i: (i, 0))

    def body(g_ref, w_ref, m_ref, v_ref, d_out, m_out, v_out):
        d_out[...], m_out[...], v_out[...] = _adamw_math(w_ref[...], g_ref[...], m_ref[...], v_ref[...])

    return pl.pallas_call(
        body, name=name, grid=(r // rb,), in_specs=[blk] * 4, out_specs=[blk] * 3,
        out_shape=[jax.ShapeDtypeStruct((r, cdim), F32)] * 3, compiler_params=_cparams(("arbitrary",)),
    )(g, w, m, v)


def _adamw_small(parts, w, m, v):
    def body(p_ref, w_ref, m_ref, v_ref, g_out, d_out, m_out, v_out, loss_out):
        def total(srcs):
            acc = None
            for r in srcs:
                for j in range(N_DEV):
                    term = p_ref[j, r:r + 1, :]
                    acc = term if acc is None else acc + term
            return acc

        for prow, srcs in enumerate(SMALL_SOURCES):
            one = slice(prow, prow + 1)
            g = total(srcs)
            g_out[one, :] = g
            d_out[one, :], m_out[one, :], v_out[one, :] = _adamw_math(w_ref[one, :], g, m_ref[one, :], v_ref[one, :])
        loss_out[...] = jnp.broadcast_to(jnp.sum(total(LOSS_SOURCE), axis=1, keepdims=True), (8, LANES))

    full = lambda shp: pl.BlockSpec(shp, lambda i: (0,) * len(shp))
    return pl.pallas_call(
        body, name="adamw_small", grid=(1,),
        in_specs=[full((N_DEV, SMALL_ROWS, D)), full((16, D)), full((16, D)), full((16, D))],
        out_specs=[full((16, D))] * 4 + [full((8, LANES))],
        out_shape=[jax.ShapeDtypeStruct((16, D), F32)] * 4 + [jax.ShapeDtypeStruct((8, LANES), F32)],
        compiler_params=_cparams(("arbitrary",)),
    )(parts, w, m, v)


def _to_slab(shard, form):
    if form == "N":
        return shard
    return shard.T if form == "T" else shard.T.reshape(-1, D)


def _from_slab(block, form, shard_shape):
    if form == "N":
        return block
    return block.T if form == "T" else block.reshape(shard_shape[1], shard_shape[0]).T


def _gathered_full(g, form, shard_shape):
    if form == "TR":
        return g.reshape(N_DEV * shard_shape[1], shard_shape[0])
    return g.reshape(N_DEV * g.shape[1], D)


def _pack_slab(shards, layout):
    return jnp.concatenate([jnp.zeros((r, D), BF16) if n is None else _to_slab(shards[n], form).astype(BF16)
                            for n, r, form in layout], axis=0)


def _unpack_gathered(gathered, layout):
    out, off = {}, 0
    for n, r, form in layout:
        if n is not None:
            out[n] = _gathered_full(gathered[:, off:off + r], form, SHARD_SHAPES[n])
        off += r
    return out


def _pack_per_device(gw, layout):
    return jnp.concatenate([jnp.zeros((N_DEV, r, D), BF16) if n is None else gw[n].reshape(N_DEV, r, D)
                            for n, r, _ in layout], axis=1)


def _adamw_shards(gslab, layout, wts, mom, var):
    out, off = {}, 0
    for n, r, form in layout:
        if n is not None:
            g = _from_slab(gslab[off:off + r], form, SHARD_SHAPES[n])
            out[n] = (g,) + tuple(_adamw(g, wts[n], mom[n], var[n], "adamw_" + n))
        off += r
    return out


def _sum_over_cores(per_dev, tag):
    my_c = lax.axis_index("c")
    pairs = per_dev.reshape(4, 2, per_dev.shape[1], D)
    keep = lax.dynamic_index_in_dim(pairs, my_c, axis=1, keepdims=False)
    give = lax.dynamic_index_in_dim(pairs, 1 - my_c, axis=1, keepdims=False)
    return _pair_sum(keep, _sibling_exchange(give, "rs_sibling_exchange_" + tag), "rs_pair_sum_" + tag)


def _small_pack(vals):
    rows = []
    for n in SMALL:
        v = vals[n].reshape(-1)
        k = -(-v.shape[0] // D)
        rows.append(jnp.pad(v, (0, k * D - v.shape[0])).reshape(k, D))
    return jnp.concatenate(rows, axis=0)


def _small_unpack(packed, shapes):
    out = {}
    for n in SMALL:
        k = shapes[n][-1]
        r0 = SMALL_ROW[n]
        out[n] = packed[r0:r0 + -(-k // D)].reshape(-1)[:k].reshape(shapes[n])
    return out


def kernel(x, c, positions, w_ada, b_ada, norm1_g, w_in, b_merge, gla_w_alpha, gla_b_alpha, gla_out_norm_g, gla_w_o, mla_q_lat_g, mla_w_uq, mla_kv_lat_g, mla_w_ukv, mla_qn_g, mla_kn_g, mla_w_o, w_out, norm2_g, mlp_w1, mlp_w2, loss_target, m_w_ada, m_b_ada, m_norm1_g, m_w_in, m_b_merge, m_gla_w_alpha, m_gla_b_alpha, m_gla_out_norm_g, m_gla_w_o, m_mla_q_lat_g, m_mla_w_uq, m_mla_kv_lat_g, m_mla_w_ukv, m_mla_qn_g, m_mla_kn_g, m_mla_w_o, m_w_out, m_norm2_g, m_mlp_w1, m_mlp_w2, v_w_ada, v_b_ada, v_norm1_g, v_w_in, v_b_merge, v_gla_w_alpha, v_gla_b_alpha, v_gla_out_norm_g, v_gla_w_o, v_mla_q_lat_g, v_mla_w_uq, v_mla_kv_lat_g, v_mla_w_ukv, v_mla_qn_g, v_mla_kn_g, v_mla_w_o, v_w_out, v_norm2_g, v_mlp_w1, v_mlp_w2):
    args = dict(locals())
    wts = {n: args[n][0] for n in WEIGHTS}
    mom = {n: args["m_" + n][0] for n in WEIGHTS}
    var = {n: args["v_" + n][0] for n in WEIGHTS}
    my_c = lax.axis_index("c")
    my_dev = 4 * lax.axis_index("x") + 2 * lax.axis_index("y") + my_c
    bsz = x.shape[0]
    sp = {n: wts[n].reshape(1, -1) for n in SMALL}

    gathered_a, c_gathered = _all_gather([_pack_slab(wts, SLAB_A), jnp.pad(c, ((0, 8 - bsz), (0, 0)))], "weights_all_gather")
    wt = _unpack_gathered(gathered_a, SLAB_A)

    c_all = c_gathered[:, :bsz].reshape(N_DEV * bsz, D)
    bias = lax.dynamic_slice_in_dim(sp["b_ada"], my_dev * ADA_COLS, ADA_COLS, axis=1)
    mod_cols = _mm(c_all, wts["w_ada"], "nn", (F32,), "ada_fwd", pro=_silu, epi=lambda acc, b: (acc + b,),
                   extras=(jnp.broadcast_to(bias, (N_DEV * bsz, ADA_COLS)),))
    mod_all, = _all_gather([mod_cols], "mod_all_gather")
    mod_mine = lax.dynamic_slice_in_dim(mod_all, my_dev * bsz, bsz, axis=1)
    mod3 = jnp.transpose(mod_mine, (1, 0, 2)).reshape(bsz, 6, D)

    grad_x, parts_a, parts_b, rows = _local_step(x, positions, loss_target, wt, _pack_slab(wts, SLAB_B), sp, mod3)

    big = dict(_adamw_shards(_slab_sum(parts_a, "rs_slab_sum_a"), SLAB_A, wts, mom, var),
               **_adamw_shards(_slab_sum(parts_b, "rs_slab_sum_b"), SLAB_B, wts, mom, var))

    order = ["dmod", "norm1_g", "norm2_g", "b_merge", "gla_b_alpha", "gla_out_norm_g", "mla_q_lat_g", "mla_kv_lat_g",
             "mla_qn_g", "mla_kn_g", "loss"]
    part_rows = jnp.concatenate([rows[n] for n in order], axis=0)
    part_rows = jnp.pad(part_rows, ((0, SMALL_ROWS - part_rows.shape[0]), (0, 0)))
    all_rows, = _all_gather([part_rows], "partials_all_gather")

    dmod_all = all_rows[:, :6 * bsz].reshape(N_DEV * bsz, 6 * D)
    dmod_cols = lax.dynamic_slice_in_dim(dmod_all, my_dev * ADA_COLS, ADA_COLS, axis=1)
    g_ada = _mm(c_all, dmod_cols, "tn", (F32,), "ada_dw", pro=_silu)
    big["w_ada"] = (g_ada,) + tuple(_adamw(g_ada, wts["w_ada"], mom["w_ada"], var["w_ada"], "adamw_w_ada"))

    small = _adamw_small(all_rows, _small_pack({n: wts[n] for n in SMALL}), _small_pack({n: mom[n] for n in SMALL}),
                         _small_pack({n: var[n] for n in SMALL}))
    loss = small[4][0, 0]
    small_shapes = {n: wts[n].shape for n in SMALL}
    small = [_small_unpack(o, small_shapes) for o in small[:4]]

    outs = [loss, grad_x]
    for k in range(4):
        for n in WEIGHTS:
            val = big[n][k] if n in BIG else small[k][n]
            outs.append(val.reshape((1,) + tuple(wts[n].shape)))
    return tuple(outs)
```

```python
import jax
import jax.numpy as jnp
from jax import lax
from jax.experimental import pallas as pl
from jax.experimental.pallas import tpu as pltpu

F32 = jnp.float32
BF16 = jnp.bfloat16
MESH = pl.DeviceIdType.MESH

D = 1024
EPS = 1e-6
CHUNK = 64
GH, GDK, GDV, GLR, GTAU = 4, 128, 256, 16, 16.0
MH, MQR, MKVR, NOPE, ROPE, MV = 16, 256, 128, 64, 32, 64
MQK = NOPE + ROPE
HP = 128
FF = 4 * D
ROPE_THETA = 10000.0
IN_WIDTH = 5552
PW = 5632
N_DEV = 8
LANES = 128
SLAB_BLOCK_MAX = 400
ADA_COLS = 6 * D // N_DEV
SMALL_ROWS = 32
SMALL_SOURCES = tuple([(r, 6 + r) for r in range(6)] + [(12,), (13,), (14,), (15,), (16,), (17, 18, 19, 20),
                                                         (21,), (22,), (23,), (24,)])
LOSS_SOURCE = (25,)
VMEM_LIMIT = 60 * 1024 * 1024

ADAM_LR, ADAM_B1, ADAM_B2, ADAM_EPS, ADAM_WD, ADAM_STEP = 0.001, 0.9, 0.999, 1e-08, 0.01, 10

SLAB_A = (("w_in", 694, "T"), ("gla_w_alpha", 1, "TR"), (None, 9, None), ("mla_w_uq", 48, "TR"), ("mla_w_ukv", 32, "TR"))
SLAB_B = (("mlp_w1", 512, "T"), ("gla_w_o", 128, "N"), ("mla_w_o", 128, "N"), ("w_out", 128, "N"), ("mlp_w2", 512, "N"))
BIG = ("w_ada",) + tuple(n for n, _, _ in SLAB_A + SLAB_B if n is not None)
SHARD_SHAPES = {"w_ada": (D, 6 * D // N_DEV), "w_in": (D, IN_WIDTH // N_DEV), "gla_w_alpha": (GLR, GH * GDK // N_DEV),
                "gla_w_o": (GH * GDV // N_DEV, D), "mla_w_uq": (MQR, MH * MQK // N_DEV),
                "mla_w_ukv": (MKVR, MH * (NOPE + MV) // N_DEV), "mla_w_o": (MH * MV // N_DEV, D), "w_out": (D // N_DEV, D),
                "mlp_w1": (D, FF // N_DEV), "mlp_w2": (FF // N_DEV, D)}
SMALL = ("b_ada", "norm1_g", "norm2_g", "b_merge", "gla_b_alpha", "gla_out_norm_g", "mla_q_lat_g", "mla_kv_lat_g",
         "mla_qn_g", "mla_kn_g")
SMALL_ROW = {"b_ada": 0, "norm1_g": 6, "norm2_g": 7, "b_merge": 8, "gla_b_alpha": 10, "gla_out_norm_g": 11,
             "mla_q_lat_g": 12, "mla_kv_lat_g": 13, "mla_qn_g": 14, "mla_kn_g": 15}
WEIGHTS = ("w_ada", "b_ada", "norm1_g", "w_in", "b_merge", "gla_w_alpha", "gla_b_alpha", "gla_out_norm_g", "gla_w_o",
           "mla_q_lat_g", "mla_w_uq", "mla_kv_lat_g", "mla_w_ukv", "mla_qn_g", "mla_kn_g", "mla_w_o", "w_out",
           "norm2_g", "mlp_w1", "mlp_w2")


def _cparams(sem=None):
    return pltpu.CompilerParams(dimension_semantics=sem, vmem_limit_bytes=VMEM_LIMIT)


def _tile(n, pref):
    for t in (2048, PW // 4, 1024, 512, 256, 128):
        if t <= pref and n % t == 0:
            return t
    return n


def _dot(a, b, dims, precision=None):
    return lax.dot_general(a, b, (dims, ((), ())), preferred_element_type=F32, precision=precision)


NN = ((1,), (0,))
NT = ((1,), (1,))
TN = ((0,), (0,))


def _sigmoid(x):
    return 1.0 / (1.0 + jnp.exp(-x))


def _silu(x):
    return x * _sigmoid(x)


def _mm(a, b, mode, out_dtypes, name, *, pro=None, pro_b=None, epi=None, extras=(), a_off=0, m=None, tm=2048, tn=1024,
        tk=1024, cargo=None, tail=None):
    if mode == "tn":
        kc, n = b.shape
        m = a.shape[1] if m is None else m
    elif mode == "nn":
        m, kc = a.shape
        n = b.shape[1]
    else:
        m, kc = a.shape
        n = b.shape[0]
    tm, tn, tk = _tile(m, tm), _tile(n, tn), _tile(kc, tk)
    nk = kc // tk
    dims = {"nn": NN, "nt": NT, "tn": TN}[mode]
    if mode == "tn":
        a_spec = pl.BlockSpec((tk, tm), lambda i, j, k: (k, i + a_off))
    else:
        a_spec = pl.BlockSpec((tm, tk), lambda i, j, k: (i + a_off, k))
    if mode == "nt":
        b_spec = pl.BlockSpec((tn, tk), lambda i, j, k: (j, k))
    else:
        b_spec = pl.BlockSpec((tk, tn), lambda i, j, k: (k, j))
    o_spec = pl.BlockSpec((tm, tn), lambda i, j, k: (i, j))
    n_ex, n_out = len(extras), len(out_dtypes)
    grid = (m // tm, n // tn, nk)
    has_cargo = cargo is not None
    t_ins, t_outs = (tail["ins"], tail["outs"]) if tail else ([], [])
    assert not tail or grid[1] == 1

    def body(a_ref, b_ref, *rest):
        rest = list(rest)
        take = lambda count: [rest.pop(0) for _ in range(count)]
        ex, cargo_ref, tail_in = take(n_ex), take(has_cargo), take(len(t_ins))
        outs, parts_ref, tail_out = take(n_out), take(has_cargo), take(len(t_outs))
        acc = rest.pop(0)
        steps = [pl.program_id(axis) for axis in range(3)]
        if has_cargo:
            exchange = _chip_exchange_copies(cargo_ref[0], parts_ref[0], *rest)
            first = (steps[0] == 0) & (steps[1] == 0) & (steps[2] == 0)
            last = (steps[0] == grid[0] - 1) & (steps[1] == grid[1] - 1) & (steps[2] == grid[2] - 1)
            pl.when(first)(exchange.start)
        k = steps[2]

        @pl.when(k == 0)
        def _():
            acc[...] = jnp.zeros_like(acc)

        av = a_ref[...]
        if pro is not None:
            av = pro(av)
        bv = b_ref[...]
        if pro_b is not None:
            bv = pro_b(bv)
        acc[...] += _dot(av.astype(BF16), bv.astype(BF16), dims)

        @pl.when(k == nk - 1)
        def _():
            if tail:
                tail["fn"](acc[...], steps[0], *tail_in, *tail_out)
            res = (acc[...],) if epi is None else epi(acc[...], *[e[...] for e in ex])
            for o_ref, r in zip(outs, res):
                o_ref[...] = r.astype(o_ref.dtype)

        if has_cargo:
            pl.when(last)(exchange.finish)

    cargo_in = [cargo] if has_cargo else []
    cargo_spec = [HBM_SPEC] * len(cargo_in)
    sequential = has_cargo or bool(tail)
    out = pl.pallas_call(
        body, name=name, grid=grid,
        in_specs=[a_spec, b_spec] + [o_spec] * n_ex + cargo_spec + (tail["in_specs"] if tail else []),
        out_specs=[o_spec] * n_out + cargo_spec + (tail["out_specs"] if tail else []),
        out_shape=[jax.ShapeDtypeStruct((m, n), dt) for dt in out_dtypes]
        + [jax.ShapeDtypeStruct(c.shape, c.dtype) for c in cargo_in] + t_outs,
        scratch_shapes=[pltpu.VMEM((tm, tn), F32)] + (EXCHANGE_SEMS if has_cargo else []),
        compiler_params=_cparams(("arbitrary",) * 3 if sequential else ("parallel", "parallel", "arbitrary")),
    )(a, b, *extras, *cargo_in, *t_ins)
    return out[0] if len(out) == 1 else out


def _rows(s):
    return _tile(s, 512)


def _mod_spec():
    return pl.BlockSpec((1, 6, D), lambda b, i: (b, 0, 0))


def _tok_spec(tr, nb, width=D, col=0):
    return pl.BlockSpec((tr, width), lambda b, i: (b * nb + i, col))


def _modulated_norm(xv, gv, mod_ref, i_shift, i_scale):
    r = lax.rsqrt(jnp.mean(xv * xv, axis=1, keepdims=True) + EPS)
    return ((xv * r) * gv) * (1.0 + mod_ref[0, i_scale:i_scale + 1, :]) + mod_ref[0, i_shift:i_shift + 1, :]


def _norm_mod_fwd(x, g, mod3, i_shift, i_scale, name):
    bsz, _, _ = mod3.shape
    t = x.shape[0]
    s = t // bsz
    tr = _rows(s)
    nb = s // tr

    def body(x_ref, g_ref, mod_ref, h_ref):
        h_ref[...] = _modulated_norm(x_ref[...], g_ref[...], mod_ref, i_shift, i_scale).astype(BF16)

    tok = _tok_spec(tr, nb)
    return pl.pallas_call(
        body, name=name, grid=(bsz, nb), in_specs=[tok, pl.BlockSpec((1, D), lambda b, i: (0, 0)), _mod_spec()],
        out_specs=tok, out_shape=jax.ShapeDtypeStruct((t, D), BF16), compiler_params=_cparams(("arbitrary", "arbitrary")),
    )(x, g, mod3)


def _norm_fwd_tail(x, g, mod3, i_shift, i_scale, i_gate, tm):
    bsz, t = mod3.shape[0], x.shape[0]
    per_b = t // bsz // tm

    def fn(mixedv, i, x_ref, g_ref, mod_ref, mixed_ref, x1_ref, h_ref):
        mixed_ref[...] = mixedv.astype(BF16)
        xv = x_ref[...] + mod_ref[0, i_gate:i_gate + 1, :] * mixedv
        x1_ref[...] = xv
        h_ref[...] = _modulated_norm(xv, g_ref[...], mod_ref, i_shift, i_scale).astype(BF16)

    tok = pl.BlockSpec((tm, D), lambda i, j, k: (i, 0))
    return dict(
        fn=fn, ins=[x, g, mod3], in_specs=[tok, pl.BlockSpec((1, D), lambda i, j, k: (0, 0)), _tail_mod_spec(per_b)],
        outs=[jax.ShapeDtypeStruct((t, D), BF16), jax.ShapeDtypeStruct((t, D), F32), jax.ShapeDtypeStruct((t, D), BF16)],
        out_specs=[tok, tok, tok])


def _norm_bwd_rows(xv, dhv, dresv, gv, mod_ref, i_scale, accb, accg):
    r = lax.rsqrt(jnp.mean(xv * xv, axis=1, keepdims=True) + EPS)
    xn = xv * r
    accb[0, 0:1, :] += jnp.sum(dhv, axis=0, keepdims=True)
    accb[0, 1:2, :] += jnp.sum(dhv * (xn * gv), axis=0, keepdims=True)
    tt = dhv * (1.0 + mod_ref[0, i_scale:i_scale + 1, :])
    accg[0:1, :] += jnp.sum(tt * xn, axis=0, keepdims=True)
    dxn = tt * gv
    return dresv + r * (dxn - xn * jnp.mean(dxn * xn, axis=1, keepdims=True))


def _norm_bwd_tail(x, dres, g, mod3, i_scale, tm, mixed=None, i_gate=None):
    bsz, t = mod3.shape[0], x.shape[0]
    per_b = t // bsz // tm
    has_res = mixed is not None

    def fn(dhv, i, *refs):
        if has_res:
            x_ref, dres_ref, mx_ref, g_ref, mod_ref, dx_ref, dmx_ref, accb, accg = refs
        else:
            x_ref, dres_ref, g_ref, mod_ref, dx_ref, accb, accg = refs

        @pl.when(i % per_b == 0)
        def _():
            accb[...] = jnp.zeros_like(accb)

        @pl.when(i == 0)
        def _():
            accg[...] = jnp.zeros_like(accg)

        dx = _norm_bwd_rows(x_ref[...], dhv, dres_ref[...], g_ref[...], mod_ref, i_scale, accb, accg)
        dx_ref[...] = dx
        if has_res:
            accb[0, 2:3, :] += jnp.sum(dx * mx_ref[...].astype(F32), axis=0, keepdims=True)
            dmx_ref[...] = (dx * mod_ref[0, i_gate:i_gate + 1, :]).astype(BF16)

    tok = pl.BlockSpec((tm, D), lambda i, j, k: (i, 0))
    res = [mixed] if has_res else []
    return dict(
        fn=fn, ins=[x, dres] + res + [g, mod3],
        in_specs=[tok, tok] + [tok] * len(res) + [pl.BlockSpec((1, D), lambda i, j, k: (0, 0)), _tail_mod_spec(per_b)],
        outs=[jax.ShapeDtypeStruct((t, D), F32)] + [jax.ShapeDtypeStruct((t, D), BF16)] * len(res)
        + [jax.ShapeDtypeStruct((bsz, 8, D), F32), jax.ShapeDtypeStruct((8, D), F32)],
        out_specs=[tok] * (1 + len(res)) + [_tail_batch_spec(per_b), pl.BlockSpec((8, D), lambda i, j, k: (0, 0))])


def _tail_mod_spec(per_b):
    return pl.BlockSpec((1, 6, D), lambda i, j, k: (i // per_b, 0, 0))


def _tail_batch_spec(per_b):
    return pl.BlockSpec((1, 8, D), lambda i, j, k: (i // per_b, 0, 0))


def _loss_tail(x1, tgt, mod3, tm):
    bsz, t = mod3.shape[0], x1.shape[0]
    per_b = t // bsz // tm

    def fn(ffv, i, x1_ref, tg_ref, mod_ref, dy_ref, dff_ref, accb, accl):
        @pl.when(i % per_b == 0)
        def _():
            accb[...] = jnp.zeros_like(accb)

        @pl.when(i == 0)
        def _():
            accl[...] = jnp.zeros_like(accl)

        gate = mod_ref[0, 5:6, :]
        err = x1_ref[...] + gate * ffv - tg_ref[...]
        accl[0:1, :] += jnp.sum(err * err, axis=0, keepdims=True) * (0.5 / D)
        dy = err * (1.0 / D)
        dy_ref[...] = dy
        dff_ref[...] = (dy * gate).astype(BF16)
        accb[0, 0:1, :] += jnp.sum(dy * ffv, axis=0, keepdims=True)

    tok = pl.BlockSpec((tm, D), lambda i, j, k: (i, 0))
    return dict(
        fn=fn, ins=[x1, tgt, mod3], in_specs=[tok, tok, _tail_mod_spec(per_b)],
        outs=[jax.ShapeDtypeStruct((t, D), F32), jax.ShapeDtypeStruct((t, D), BF16),
              jax.ShapeDtypeStruct((bsz, 8, D), F32), jax.ShapeDtypeStruct((8, D), F32)],
        out_specs=[tok, tok, _tail_batch_spec(per_b), pl.BlockSpec((8, D), lambda i, j, k: (0, 0))])


def _merge_fwd(proj, b_merge, y_a, y_b):
    t = proj.shape[0]
    tr = _tile(t, 512)

    def body(la_ref, lb_ref, bm_ref, ya_ref, yb_ref, mix_ref):
        ga = _sigmoid(la_ref[...] + bm_ref[:, 0:D])
        gb = _sigmoid(lb_ref[...] + bm_ref[:, D:2 * D])
        mix_ref[...] = (ga * ya_ref[...].astype(F32) + gb * yb_ref[...].astype(F32)).astype(BF16)

    tok = pl.BlockSpec((tr, D), lambda i: (i, 0))
    return pl.pallas_call(
        body, name="merge_fwd", grid=(t // tr,),
        in_specs=[pl.BlockSpec((tr, D), lambda i: (i, 3)), pl.BlockSpec((tr, D), lambda i: (i, 4)),
                  pl.BlockSpec((1, 2 * D), lambda i: (0, 0)), tok, tok],
        out_specs=tok, out_shape=jax.ShapeDtypeStruct((t, D), BF16),
        compiler_params=_cparams(("arbitrary",)),
    )(proj, proj, b_merge, y_a, y_b)


def _merge_bwd_tail(proj, b_merge, y_a, y_b, tm):
    t = proj.shape[0]

    def fn(dm, i, la_ref, lb_ref, bm_ref, ya_ref, yb_ref, dya_ref, dyb_ref, dl_ref, acc):
        @pl.when(i == 0)
        def _():
            acc[...] = jnp.zeros_like(acc)

        ga = _sigmoid(la_ref[...] + bm_ref[:, 0:D])
        gb = _sigmoid(lb_ref[...] + bm_ref[:, D:2 * D])
        dya_ref[...] = (dm * ga).astype(BF16)
        dyb_ref[...] = (dm * gb).astype(BF16)
        dla = dm * ya_ref[...].astype(F32) * ga * (1.0 - ga)
        dlb = dm * yb_ref[...].astype(F32) * gb * (1.0 - gb)
        dl_ref[:, 0:D] = dla.astype(BF16)
        dl_ref[:, D:2 * D] = dlb.astype(BF16)
        acc[0:1, 0:D] += jnp.sum(dla, axis=0, keepdims=True)
        acc[0:1, D:2 * D] += jnp.sum(dlb, axis=0, keepdims=True)

    tok = pl.BlockSpec((tm, D), lambda i, j, k: (i, 0))
    return dict(
        fn=fn, ins=[proj, proj, b_merge, y_a, y_b],
        in_specs=[pl.BlockSpec((tm, D), lambda i, j, k: (i, 3)), pl.BlockSpec((tm, D), lambda i, j, k: (i, 4)),
                  pl.BlockSpec((1, 2 * D), lambda i, j, k: (0, 0)), tok, tok],
        outs=[jax.ShapeDtypeStruct((t, D), BF16), jax.ShapeDtypeStruct((t, D), BF16),
              jax.ShapeDtypeStruct((t, 2 * D), BF16), jax.ShapeDtypeStruct((8, 2 * D), F32)],
        out_specs=[tok, tok, pl.BlockSpec((tm, 2 * D), lambda i, j, k: (i, 0)), pl.BlockSpec((8, 2 * D), lambda i, j, k: (0, 0))])


GLA_HEADS = 2
GLA_UNROLL = 16


def _log_sigmoid(z):
    return jnp.minimum(z, 0.0) - jnp.log(1.0 + jnp.exp(-jnp.abs(z)))


def _tri(lower):
    r = lax.broadcasted_iota(jnp.int32, (CHUNK, CHUNK), 0)
    c = lax.broadcasted_iota(jnp.int32, (CHUNK, CHUNK), 1)
    return jnp.where(r >= c if lower else r <= c, 1.0, 0.0).astype(F32)


def _gla_fwd(proj, wa_pad, b_alpha, g_out, bsz):
    t = proj.shape[0]
    s = t // bsz
    nc = s // CHUNK
    p, kw, vw = GLA_HEADS, GLA_HEADS * GDK, GLA_HEADS * GDV

    def body(q_ref, k_ref, v_ref, gg_ref, ms_ref, wa_ref, ba_ref, go_ref, o_ref, og_ref, st_ref, la, state):
        z = _dot(ms_ref[...].astype(BF16), wa_ref[...], NN) + ba_ref[...]
        la[...] = _log_sigmoid(z) * (1.0 / GTAU)
        state[...] = jnp.zeros_like(state)
        low = _tri(True)
        gout = go_ref[...]

        def chunk(n, carry):
            rows = pl.ds(pl.multiple_of(n * CHUNK, CHUNK), CHUNK)
            for hh in range(p):
                kc, vc = slice(hh * GDK, (hh + 1) * GDK), slice(hh * GDV, (hh + 1) * GDV)
                lac = la[rows, kc]
                cum = _dot(low, lac, NN, lax.Precision.HIGHEST)
                ce = jnp.sum(lac, axis=0, keepdims=True)
                kd = (k_ref[rows, kc].astype(F32) * jnp.exp(ce - cum)).astype(BF16)
                new = state[vc, :] * jnp.exp(ce) + _dot(v_ref[rows, vc].astype(BF16), kd, TN)
                state[vc, :] = new
                st_ref[pl.ds(pl.multiple_of((hh * nc + n) * GDV, GDV), GDV), :] = new.astype(BF16)
                qs = (q_ref[rows, kc].astype(F32) * (GDK ** -0.5)).astype(BF16)
                o = _dot(qs, new.astype(BF16), NT)
                o_ref[rows, vc] = o
                ro = lax.rsqrt(jnp.mean(o * o, axis=1, keepdims=True) + EPS)
                og_ref[rows, vc] = (((o * ro) * gout) * _silu(gg_ref[rows, vc].astype(F32))).astype(BF16)
            return carry

        lax.fori_loop(0, nc, chunk, 0, unroll=GLA_UNROLL)

    return pl.pallas_call(
        body, name="gla_fwd", grid=(bsz, GH // p),
        in_specs=[pl.BlockSpec((s, kw), lambda b, h: (b, h)), pl.BlockSpec((s, kw), lambda b, h: (b, GH // p + h)),
                  pl.BlockSpec((s, vw), lambda b, h: (b, GH // p + h)), pl.BlockSpec((s, vw), lambda b, h: (b, 2 * GH // p + h)),
                  pl.BlockSpec((s, LANES), lambda b, h: (b, 43)),
                  pl.BlockSpec((LANES, kw), lambda b, h: (0, h)), pl.BlockSpec((1, kw), lambda b, h: (0, h)),
                  pl.BlockSpec((1, GDV), lambda b, h: (0, 0))],
        out_specs=[pl.BlockSpec((s, vw), lambda b, h: (b, h)), pl.BlockSpec((s, vw), lambda b, h: (b, h)),
                   pl.BlockSpec((p * nc * GDV, GDK), lambda b, h: (b * (GH // p) + h, 0))],
        out_shape=[jax.ShapeDtypeStruct((t, GH * GDV), F32), jax.ShapeDtypeStruct((t, GH * GDV), BF16),
                   jax.ShapeDtypeStruct((bsz * GH * nc * GDV, GDK), BF16)],
        scratch_shapes=[pltpu.VMEM((s, kw), F32), pltpu.VMEM((vw, GDK), F32)],
        compiler_params=_cparams(("arbitrary", "arbitrary")),
    )(proj, proj, proj, proj, proj, wa_pad, b_alpha, g_out)


def _gla_bwd(dog, o, proj, wa_pad, b_alpha, g_out, states, bsz):
    t = proj.shape[0]
    s = t // bsz
    nc = s // CHUNK
    p, kw, vw = GLA_HEADS, GLA_HEADS * GDK, GLA_HEADS * GDV

    def body(dog_ref, o_ref, q_ref, k_ref, v_ref, gg_ref, ms_ref, wa_ref, ba_ref, go_ref, st_ref,
             dq_ref, dk_ref, dv_ref, dgg_ref, dz_ref, dba, dgo, zs, la, carry_g):
        @pl.when(pl.program_id(1) == 0)
        def _():
            dba[...] = jnp.zeros_like(dba)
            dgo[...] = jnp.zeros_like(dgo)

        z = _dot(ms_ref[...].astype(BF16), wa_ref[...], NN) + ba_ref[...]
        zs[...] = z
        la[...] = _log_sigmoid(z) * (1.0 / GTAU)
        carry_g[...] = jnp.zeros_like(carry_g)
        low, upp = _tri(True), _tri(False)
        gout = go_ref[...]
        last_row = lax.broadcasted_iota(jnp.int32, (CHUNK, GDK), 0) == CHUNK - 1

        def chunk(step, carry):
            n = nc - 1 - step
            rows = pl.ds(pl.multiple_of(n * CHUNK, CHUNK), CHUNK)
            for hh in range(p):
                kc, vc = slice(hh * GDK, (hh + 1) * GDK), slice(hh * GDV, (hh + 1) * GDV)
                lac = la[rows, kc]
                cum = _dot(low, lac, NN, lax.Precision.HIGHEST)
                ce = jnp.sum(lac, axis=0, keepdims=True)
                e = jnp.exp(ce - cum)
                dec = jnp.exp(ce)
                kf = k_ref[rows, kc].astype(F32)
                kd = (kf * e).astype(BF16)
                vv = v_ref[rows, vc].astype(BF16)
                qs = (q_ref[rows, kc].astype(F32) * (GDK ** -0.5)).astype(BF16)
                ov = o_ref[rows, vc]
                ro = lax.rsqrt(jnp.mean(ov * ov, axis=1, keepdims=True) + EPS)
                on = ov * ro
                gg = gg_ref[rows, vc].astype(F32)
                sg = _sigmoid(gg)
                dogv = dog_ref[rows, vc].astype(F32)
                dgg_ref[rows, vc] = (dogv * (on * gout) * (sg * (1.0 + gg * (1.0 - sg)))).astype(BF16)
                t1 = dogv * (gg * sg)
                dgo[8 * hh:8 * hh + 1, :] += jnp.sum(t1 * on, axis=0, keepdims=True)
                don = t1 * gout
                do = ro * (don - on * jnp.mean(don * on, axis=1, keepdims=True))
                dob = do.astype(BF16)
                st_n = st_ref[pl.ds(pl.multiple_of((hh * nc + n) * GDV, GDV), GDV), :]
                dq_ref[rows, kc] = (_dot(dob, st_n, NN) * (GDK ** -0.5)).astype(BF16)
                dn = carry_g[vc, :] + _dot(dob, qs, TN)
                prev = hh * nc + jnp.maximum(n - 1, 0)
                st_p = st_ref[pl.ds(pl.multiple_of(prev * GDV, GDV), GDV), :].astype(F32) * jnp.where(n > 0, 1.0, 0.0)
                ddec = jnp.sum(dn * st_p, axis=0, keepdims=True)
                dnb = dn.astype(BF16)
                dkd = _dot(vv, dnb, NN)
                dv_ref[rows, vc] = _dot(kd, dnb, NT).astype(BF16)
                dk_ref[rows, kc] = (dkd * e).astype(BF16)
                w = dkd * kf * e
                dce = jnp.sum(w, axis=0, keepdims=True) + ddec * dec
                dcum = jnp.where(last_row, dce - w, -w)
                dla = _dot(upp, dcum, NN, lax.Precision.HIGHEST)
                dz = dla * (1.0 / GTAU) * _sigmoid(-zs[rows, kc])
                dba[0:1, kc] += jnp.sum(dz, axis=0, keepdims=True)
                dz_ref[rows, kc] = dz.astype(BF16)
                carry_g[vc, :] = dn * dec
            return carry

        lax.fori_loop(0, nc, chunk, 0, unroll=GLA_UNROLL)

    hv = pl.BlockSpec((s, vw), lambda h, b: (b, h))
    hk = pl.BlockSpec((s, kw), lambda h, b: (b, h))
    return pl.pallas_call(
        body, name="gla_bwd", grid=(GH // p, bsz),
        in_specs=[hv, hv, hk, pl.BlockSpec((s, kw), lambda h, b: (b, GH // p + h)),
                  pl.BlockSpec((s, vw), lambda h, b: (b, GH // p + h)), pl.BlockSpec((s, vw), lambda h, b: (b, 2 * GH // p + h)),
                  pl.BlockSpec((s, LANES), lambda h, b: (b, 43)), pl.BlockSpec((LANES, kw), lambda h, b: (0, h)),
                  pl.BlockSpec((1, kw), lambda h, b: (0, h)), pl.BlockSpec((1, GDV), lambda h, b: (0, 0)),
                  pl.BlockSpec((p * nc * GDV, GDK), lambda h, b: (b * (GH // p) + h, 0))],
        out_specs=[hk, hk, hv, hv, hk, pl.BlockSpec((8, kw), lambda h, b: (0, h)),
                   pl.BlockSpec((8 * p, GDV), lambda h, b: (h, 0))],
        out_shape=[jax.ShapeDtypeStruct((t, GH * GDK), BF16), jax.ShapeDtypeStruct((t, GH * GDK), BF16),
                   jax.ShapeDtypeStruct((t, GH * GDV), BF16), jax.ShapeDtypeStruct((t, GH * GDV), BF16),
                   jax.ShapeDtypeStruct((t, GH * GDK), BF16), jax.ShapeDtypeStruct((8, GH * GDK), F32),
                   jax.ShapeDtypeStruct((8 * GH, GDV), F32)],
        scratch_shapes=[pltpu.VMEM((s, kw), F32), pltpu.VMEM((s, kw), F32), pltpu.VMEM((vw, GDK), F32)],
        compiler_params=_cparams(("arbitrary", "arbitrary")),
    )(dog, o, proj, proj, proj, proj, proj, wa_pad, b_alpha, g_out, states)


def _rope_tables(pos_ref, fr_ref, sg_ref):
    ang = pos_ref[...].astype(F32) * fr_ref[...]
    return jnp.cos(ang), jnp.sin(ang) * sg_ref[...]


def _partner(x):
    lane = lax.broadcasted_iota(jnp.int32, x.shape, 1)
    return jnp.where(lane < NOPE + ROPE // 2, pltpu.roll(x, LANES - ROPE // 2, 1), pltpu.roll(x, ROPE // 2, 1))


def _mla_rows(t):
    return _tile(t, 512)


def _mla_pre_fwd(proj, pos, fr, sg, q_lat_g, kv_lat_g, qn_g, kn_g, wuq, wukv):
    t = proj.shape[0]
    tr = _mla_rows(t)

    def body(cq_ref, ckv_ref, ms_ref, pos_ref, fr_ref, sg_ref, qlg, kvlg, qng, kng, wuq_ref, wukv_ref, q_out, k_out, v_out):
        lane = lax.broadcasted_iota(jnp.int32, (tr, HP), 1)
        real = jnp.where(lane < MQK, 1.0, 0.0)
        cos, sin = _rope_tables(pos_ref, fr_ref, sg_ref)
        cos = cos * real
        cq = cq_ref[...].astype(F32)
        cqn = (cq * lax.rsqrt(jnp.mean(cq * cq, axis=1, keepdims=True) + EPS) * qlg[...]).astype(BF16)
        ckv = ckv_ref[...].astype(F32)
        ckvn = (ckv * lax.rsqrt(jnp.mean(ckv * ckv, axis=1, keepdims=True) + EPS) * kvlg[...]).astype(BF16)
        kpe = jnp.where((lane >= NOPE) & (lane < MQK), ms_ref[...].astype(F32), 0.0)
        kpe = kpe + jnp.where(lane < MQK + ROPE // 2, pltpu.roll(kpe, ROPE, 1), 0.0)
        lane_all = lax.broadcasted_iota(jnp.int32, (tr, MH * HP), 1)
        v_out[...] = jnp.where(lane_all % HP == MV, 1.0, _dot(ckvn, wukv_ref[:, MH * HP:], NN)).astype(BF16)

        def norm_rope(x, gain):
            xn = x * lax.rsqrt(jnp.sum(x * x * real, axis=1, keepdims=True) * (1.0 / MQK) + EPS) * gain
            return (xn * cos + pltpu.roll(xn, LANES - ROPE // 2, 1) * sin).astype(BF16)

        for h in range(MH):
            cols = slice(h * HP, (h + 1) * HP)
            q_out[:, cols] = norm_rope(_dot(cqn, wuq_ref[:, cols], NN), qng[...])
            k_out[:, cols] = norm_rope(_dot(ckvn, wukv_ref[:, cols], NN) + kpe, kng[...])

    def full(a):
        return pl.BlockSpec(a.shape, lambda i: (0, 0))

    wide = pl.BlockSpec((tr, MH * HP), lambda i: (i, 0))
    return pl.pallas_call(
        body, name="mla_pre_fwd", grid=(t // tr,),
        in_specs=[pl.BlockSpec((tr, MQR), lambda i: (i, 20)), pl.BlockSpec((tr, MKVR), lambda i: (i, 42)),
                  pl.BlockSpec((tr, LANES), lambda i: (i, 43)), pl.BlockSpec((tr, 1), lambda i: (i, 0)),
                  full(fr), full(sg), full(q_lat_g), full(kv_lat_g), full(qn_g), full(kn_g), full(wuq), full(wukv)],
        out_specs=[wide, wide, wide],
        out_shape=[jax.ShapeDtypeStruct((t, MH * HP), BF16)] * 3,
        compiler_params=_cparams(("arbitrary",)),
    )(proj, proj, proj, pos, fr, sg, q_lat_g, kv_lat_g, qn_g, kn_g, wuq, wukv)


def _mla_pre_bwd(dq2, dk2, dv2, dmisc_gla, proj, pos, fr, sg, q_lat_g, kv_lat_g, qn_g, kn_g, wuq, wukv):
    t = proj.shape[0]
    tr = _mla_rows(t)

    def body(dq_ref, dk_ref, dv_ref, dmg_ref, cq_ref, ckv_ref, ms_ref, pos_ref, fr_ref, sg_ref, qlg, kvlg, qng, kng,
             wuq_ref, wukv_ref, dcq_ref, dckv_ref, dms_ref, dwuq, dwukv, acc, dqf, dkvf):
        @pl.when(pl.program_id(0) == 0)
        def _():
            dwuq[...] = jnp.zeros_like(dwuq)
            dwukv[...] = jnp.zeros_like(dwukv)
            acc[...] = jnp.zeros_like(acc)

        cos, sin = _rope_tables(pos_ref, fr_ref, sg_ref)
        cq = cq_ref[...].astype(F32)
        rc = lax.rsqrt(jnp.mean(cq * cq, axis=1, keepdims=True) + EPS)
        xc = cq * rc
        cqn = (xc * qlg[...]).astype(BF16)
        ckv = ckv_ref[...].astype(F32)
        rkv = lax.rsqrt(jnp.mean(ckv * ckv, axis=1, keepdims=True) + EPS)
        xkv = ckv * rkv
        ckvn = (xkv * kvlg[...]).astype(BF16)
        lane = lax.broadcasted_iota(jnp.int32, (tr, HP), 1)
        is_rope = (lane >= NOPE) & (lane < MQK)
        kpe = jnp.where(is_rope, ms_ref[...].astype(F32), 0.0)
        dkpe = jnp.zeros((tr, HP), F32)
        dqng = jnp.zeros((1, HP), F32)
        dkng = jnp.zeros((1, HP), F32)
        for h in range(MH):
            cols = slice(h * HP, (h + 1) * HP)
            qh = _dot(cqn, wuq_ref[:, cols], NN)
            rq = lax.rsqrt(jnp.sum(qh * qh, axis=1, keepdims=True) * (1.0 / MQK) + EPS)
            xq = qh * rq
            dy = dq_ref[:, cols].astype(F32)
            dqn = dy * cos - _partner(dy) * sin
            dqng += jnp.sum(dqn * xq, axis=0, keepdims=True)
            tq = dqn * qng[...]
            dqf[:, cols] = (rq * (tq - xq * (jnp.sum(tq * xq, axis=1, keepdims=True) * (1.0 / MQK)))).astype(BF16)
            kh = _dot(ckvn, wukv_ref[:, cols], NN) + kpe
            rk = lax.rsqrt(jnp.sum(kh * kh, axis=1, keepdims=True) * (1.0 / MQK) + EPS)
            xk = kh * rk
            dy = dk_ref[:, cols].astype(F32)
            dkn = dy * cos - _partner(dy) * sin
            dkng += jnp.sum(dkn * xk, axis=0, keepdims=True)
            tk = dkn * kng[...]
            dkh = rk * (tk - xk * (jnp.sum(tk * xk, axis=1, keepdims=True) * (1.0 / MQK)))
            dkvf[:, cols] = jnp.where(lane < NOPE, dkh, 0.0).astype(BF16)
            dkpe += jnp.where(is_rope, dkh, 0.0)
        dkvf[:, MH * HP:] = dv_ref[...]
        acc[2:3, 0:HP] += dqng
        acc[3:4, 0:HP] += dkng
        dms_ref[...] = (dmg_ref[...] + dkpe).astype(BF16)

        dqfv = dqf[...]
        dwuq[...] += _dot(cqn, dqfv, TN)
        dcqn = _dot(dqfv, wuq_ref[...], NT)
        acc[0:1, :] += jnp.sum(dcqn * xc, axis=0, keepdims=True)
        tc = dcqn * qlg[...]
        dcq_ref[...] = (rc * (tc - xc * jnp.mean(tc * xc, axis=1, keepdims=True))).astype(BF16)

        dkvfv = dkvf[...]
        dwukv[...] += _dot(ckvn, dkvfv, TN)
        dckvn = _dot(dkvfv, wukv_ref[...], NT)
        acc[1:2, 0:MKVR] += jnp.sum(dckvn * xkv, axis=0, keepdims=True)
        tkv = dckvn * kvlg[...]
        dckv_ref[...] = (rkv * (tkv - xkv * jnp.mean(tkv * xkv, axis=1, keepdims=True))).astype(BF16)

    def full(a):
        return pl.BlockSpec(a.shape, lambda i: (0, 0))

    wide = pl.BlockSpec((tr, MH * HP), lambda i: (i, 0))
    narrow = pl.BlockSpec((tr, LANES), lambda i: (i, 0))
    return pl.pallas_call(
        body, name="mla_pre_bwd", grid=(t // tr,),
        in_specs=[wide, wide, wide, narrow,
                  pl.BlockSpec((tr, MQR), lambda i: (i, 20)), pl.BlockSpec((tr, MKVR), lambda i: (i, 42)),
                  pl.BlockSpec((tr, LANES), lambda i: (i, 43)), pl.BlockSpec((tr, 1), lambda i: (i, 0)),
                  full(fr), full(sg), full(q_lat_g), full(kv_lat_g), full(qn_g), full(kn_g), full(wuq), full(wukv)],
        out_specs=[pl.BlockSpec((tr, MQR), lambda i: (i, 0)), narrow, narrow,
                   pl.BlockSpec((MQR, MH * HP), lambda i: (0, 0)), pl.BlockSpec((MKVR, 2 * MH * HP), lambda i: (0, 0)),
                   pl.BlockSpec((8, MQR), lambda i: (0, 0))],
        out_shape=[jax.ShapeDtypeStruct((t, MQR), BF16), jax.ShapeDtypeStruct((t, MKVR), BF16),
                   jax.ShapeDtypeStruct((t, LANES), BF16), jax.ShapeDtypeStruct((MQR, MH * HP), F32),
                   jax.ShapeDtypeStruct((MKVR, 2 * MH * HP), F32), jax.ShapeDtypeStruct((8, MQR), F32)],
        scratch_shapes=[pltpu.VMEM((tr, MH * HP), BF16), pltpu.VMEM((tr, 2 * MH * HP), BF16)],
        compiler_params=_cparams(("arbitrary",)),
    )(dq2, dk2, dv2, dmisc_gla, proj, proj, proj, pos, fr, sg, q_lat_g, kv_lat_g, qn_g, kn_g, wuq, wukv)


ATT_FWD_TILES = (1024, 512)
ATT_BWD_TILES = (512, 512)
ATT_HEADS = 4
ATT_FWD_HEADS = 4
NEG = -1e30
LOG2E = 1.4426950408889634


def _att_mask(q0, k0, tq, tk):
    qc = (q0 + lax.broadcasted_iota(jnp.int32, (tq, tk), 0)) // CHUNK
    kc = (k0 + lax.broadcasted_iota(jnp.int32, (tq, tk), 1)) // CHUNK
    return kc <= qc


def _att_tiles(s, tiles):
    return _tile(s, tiles[0]), _tile(s, tiles[1])


def _lanes(x, n):
    return x if n == 1 else jnp.concatenate([x] * n, axis=1)


def _grid_ends(grid):
    i, j = pl.program_id(0), pl.program_id(1)
    return (i == 0) & (j == 0), (i == grid[0] - 1) & (j == grid[1] - 1)


def _attn_fwd(q2, k2, v2, bsz, slab):
    t = q2.shape[0]
    s = t // bsz
    tq, tk = _att_tiles(s, ATT_FWD_TILES)
    nq, groups, n_diag = s // tq, tk // HP, max(tq // tk, 1)
    sub_rows = tq // n_diag
    scale = MQK ** -0.5
    c2 = scale * LOG2E
    heads = range(ATT_FWD_HEADS)

    def body(q_ref, k_ref, v_ref, slab_ref, o_ref, lse_ref, gath_ref, send_sems, recv_sems, local_sem):
        gather = _core_row_gather_copies(slab_ref, gath_ref, send_sems, recv_sems, local_sem)
        first, last = _grid_ends((bsz, MH // ATT_FWD_HEADS))
        pl.when(first)(gather.start)

        def q_loop(qi, carry):
            q0 = pl.multiple_of(qi * tq, tq)
            rows = pl.ds(q0, tq)
            n_full = q0 // tk
            qs = [q_ref[rows, h * HP:(h + 1) * HP] for h in heads]

            def scores(h, kj, sub=None, masked=False):
                k0 = pl.multiple_of(kj * tk, tk)
                qv = qs[h] if sub is None else qs[h][sub * sub_rows:(sub + 1) * sub_rows]
                sc = _dot(qv, k_ref[pl.ds(k0, tk), h * HP:(h + 1) * HP], NT)
                return jnp.where(_att_mask(q0 + sub * sub_rows, k0, sub_rows, tk), sc, NEG) if masked else sc

            def fold(mx, sc):
                for j in range(groups):
                    mx = jnp.maximum(mx, sc[:, j * HP:(j + 1) * HP])
                return mx

            def over_diagonal(vals, step):
                out = []
                for h in heads:
                    blocks = []
                    for r in range(n_diag):
                        v = vals[h][r * sub_rows:(r + 1) * sub_rows]
                        for u in range(r + 1):
                            v = step(v, h, n_full + u, r, u == r)
                        blocks.append(v)
                    out.append(blocks[0] if n_diag == 1 else jnp.concatenate(blocks, axis=0))
                return tuple(out)

            mx = lax.fori_loop(0, n_full, lambda kj, mx: tuple(fold(mx[h], scores(h, kj)) for h in heads),
                               tuple(jnp.full((tq, HP), NEG, F32) for _ in heads))
            mx = over_diagonal(mx, lambda v, h, kj, r, masked: fold(v, scores(h, kj, r, masked)))
            mb = [jnp.broadcast_to(jnp.max(mx[h], axis=1, keepdims=True), (tq, HP)) for h in heads]

            def weighted(h, kj, sub=None, masked=False):
                m = mb[h] if sub is None else mb[h][sub * sub_rows:(sub + 1) * sub_rows]
                p = jnp.exp2((scores(h, kj, sub, masked) - _lanes(m, groups)) * c2)
                k0 = pl.multiple_of(kj * tk, tk)
                return _dot(p.astype(BF16), v_ref[pl.ds(k0, tk), h * HP:(h + 1) * HP], NN)

            acc = lax.fori_loop(0, n_full, lambda kj, acc: tuple(acc[h] + weighted(h, kj) for h in heads),
                                tuple(jnp.zeros((tq, HP), F32) for _ in heads))
            acc = over_diagonal(acc, lambda v, h, kj, r, masked: v + weighted(h, kj, r, masked))
            lane = lax.broadcasted_iota(jnp.int32, (tq, HP), 1)
            for h in heads:
                a = acc[h]
                l = jnp.sum(jnp.where(lane == MV, a, 0.0), axis=1, keepdims=True)
                o_ref[rows, h * HP:(h + 1) * HP] = (a / l).astype(BF16)
                lse_ref[rows, h * HP:(h + 1) * HP] = mb[h] * scale + jnp.log(l)
            return carry

        lax.fori_loop(0, nq, q_loop, 0)
        pl.when(last)(gather.finish)

    spec = pl.BlockSpec((s, ATT_FWD_HEADS * HP), lambda b, h: (b, h))
    return pl.pallas_call(
        body, name="attn_fwd", grid=(bsz, MH // ATT_FWD_HEADS), in_specs=[spec] * 3 + [HBM_SPEC],
        out_specs=[spec, spec, HBM_SPEC],
        out_shape=[jax.ShapeDtypeStruct((t, MH * HP), BF16), jax.ShapeDtypeStruct((t, MH * HP), F32),
                   jax.ShapeDtypeStruct((N_DEV,) + slab.shape, slab.dtype)],
        scratch_shapes=EXCHANGE_SEMS, compiler_params=_cparams(("arbitrary", "arbitrary")),
    )(q2, k2, v2, slab)


def _attn_bwd(q2, k2, v2, do2, o2, lse2, bsz, tsum):
    t = q2.shape[0]
    s = t // bsz
    tq, tk = _att_tiles(s, ATT_BWD_TILES)
    nq, nk, per, groups = s // tq, s // tk, max(tk // tq, 1), tk // HP
    scale = MQK ** -0.5
    c2 = scale * LOG2E
    heads = range(ATT_HEADS)

    def body(q_ref, k_ref, v_ref, do_ref, o_ref, lse_ref, t_ref, dq_ref, dk_ref, dv_ref, parts_ref, dq_acc, delta, lse_b2,
             send_sems, recv_sems, local_sem):
        exchange = _all_to_all_copies(t_ref, parts_ref, send_sems, recv_sems, local_sem)
        first, last = _grid_ends((bsz, MH // ATT_HEADS))
        pl.when(first)(exchange.start)
        dq_acc[...] = jnp.zeros_like(dq_acc)

        def d_loop(i, carry):
            rows = pl.ds(pl.multiple_of(i * tq, tq), tq)
            for h in heads:
                hs = slice(h * HP, (h + 1) * HP)
                dl = jnp.sum(do_ref[rows, hs].astype(F32) * o_ref[rows, hs].astype(F32), axis=1, keepdims=True)
                delta[rows, hs] = jnp.broadcast_to(dl, (tq, HP))
            lse_b2[rows, :] = lse_ref[rows, :] * LOG2E
            return carry

        lax.fori_loop(0, nq, d_loop, 0)

        def k_loop(kj, carry):
            k0 = pl.multiple_of(kj * tk, tk)
            kk = [k_ref[pl.ds(k0, tk), h * HP:(h + 1) * HP] for h in heads]
            vv = [v_ref[pl.ds(k0, tk), h * HP:(h + 1) * HP] for h in heads]

            def tile(qi, c, masked):
                q0 = pl.multiple_of(qi * tq, tq)
                rows = pl.ds(q0, tq)
                out = []
                for h in heads:
                    hs = slice(h * HP, (h + 1) * HP)
                    dk, dv = c[h]
                    q = q_ref[rows, hs]
                    do = do_ref[rows, hs]
                    e = _dot(q, kk[h], NT) * c2 - _lanes(lse_b2[rows, hs], groups)
                    if masked:
                        e = jnp.where(_att_mask(q0, k0, tq, tk), e, NEG)
                    p = jnp.exp2(e)
                    dv = dv + _dot(p.astype(BF16), do, TN)
                    ds = (p * (_dot(do, vv[h], NT) - _lanes(delta[rows, hs], groups))).astype(BF16)
                    dq_acc[rows, hs] += _dot(ds, kk[h], NN)
                    dk = dk + _dot(ds, q, TN)
                    out.append((dk, dv))
                return tuple(out)

            zero = jnp.zeros((tk, HP), F32)
            c = tuple((zero, zero) for _ in heads)
            first = k0 // tq
            for u in range(per):
                c = tile(first + u, c, True)
            c = lax.fori_loop(first + per, nq, lambda qi, c: tile(qi, c, False), c)
            for h in heads:
                dk_ref[pl.ds(k0, tk), h * HP:(h + 1) * HP] = (c[h][0] * scale).astype(BF16)
                dv_ref[pl.ds(k0, tk), h * HP:(h + 1) * HP] = c[h][1].astype(BF16)
            return carry

        lax.fori_loop(0, nk, k_loop, 0)
        dq_ref[...] = (dq_acc[...] * scale).astype(BF16)
        pl.when(last)(exchange.finish)

    spec = pl.BlockSpec((s, ATT_HEADS * HP), lambda b, h: (b, h))
    return pl.pallas_call(
        body, name="attn_bwd", grid=(bsz, MH // ATT_HEADS), in_specs=[spec] * 6 + [HBM_SPEC],
        out_specs=[spec] * 3 + [HBM_SPEC],
        out_shape=[jax.ShapeDtypeStruct((t, MH * HP), BF16)] * 3 + [jax.ShapeDtypeStruct(tsum.shape, tsum.dtype)],
        scratch_shapes=[pltpu.VMEM((s, ATT_HEADS * HP), F32)] * 3 + EXCHANGE_SEMS,
        compiler_params=_cparams(("arbitrary", "arbitrary")),
    )(q2, k2, v2, do2, o2, lse2, tsum)


def _perm_w_in_t(w):
    z = lambda n: jnp.zeros((n, w.shape[1]), w.dtype)
    return jnp.concatenate([w[:3072], w[3504:5552], w[3088:3344], w[3344:3472], w[3072:3088], z(48), w[3472:3504], z(32)],
                           axis=0)


def _unperm_w_in_t(g):
    return jnp.concatenate([g[:3072], g[5504:5520], g[5120:5376], g[5376:5504], g[5568:5600], g[3072:5120]], axis=0)


def _pad_wa(w):
    return jnp.pad(w, ((0, LANES - GLR), (0, 0)))


def _pad_wuq(w):
    return jnp.pad(w.reshape(MQR, MH, MQK), ((0, 0), (0, 0), (0, HP - MQK))).reshape(MQR, MH * HP)


def _unpad_wuq(g):
    return g.reshape(MQR, MH, HP)[:, :, :MQK].reshape(MQR, MH * MQK)


def _pad_wukv(w):
    w3 = w.reshape(MKVR, MH, NOPE + MV)
    kp = jnp.pad(w3[:, :, :NOPE], ((0, 0), (0, 0), (0, HP - NOPE))).reshape(MKVR, MH * HP)
    vp = jnp.pad(w3[:, :, NOPE:], ((0, 0), (0, 0), (0, HP - MV))).reshape(MKVR, MH * HP)
    return jnp.concatenate([kp, vp], axis=1)


def _unpad_wukv(g):
    kp = g[:, :MH * HP].reshape(MKVR, MH, HP)[:, :, :NOPE]
    vp = g[:, MH * HP:].reshape(MKVR, MH, HP)[:, :, :MV]
    return jnp.concatenate([kp, vp], axis=2).reshape(MKVR, MH * (NOPE + MV))


def _pad_wo(w):
    return jnp.pad(w.reshape(MH, MV, D), ((0, 0), (0, HP - MV), (0, 0))).reshape(MH * HP, D)


def _unpad_wo(g):
    return g.reshape(MH, HP, D)[:, :MV, :].reshape(MH * MV, D)


def _repeat_half(a):
    a3 = a.reshape(a.shape[0], -1, HP)
    a3 = jnp.concatenate([a3[:, :, :MQK], a3[:, :, NOPE:NOPE + ROPE // 2], a3[:, :, MQK + ROPE // 2:]], axis=2)
    return a3.reshape(a.shape)


def _pad_lanes(v, n=HP):
    return jnp.pad(v, ((0, 0), (0, n - v.shape[1])))


def _local_step(x, positions, tgt, wt, slab_b, sp, mod3):
    bsz, s, _ = x.shape
    t = bsz * s
    x2 = x.reshape(t, D)
    tgt2 = tgt.reshape(t, D)
    pos = positions.reshape(t, 1)
    fr16 = ROPE_THETA ** (-jnp.arange(0, ROPE, 2, dtype=F32) / ROPE)
    zero = lambda n: jnp.zeros((n,), F32)
    fr = jnp.concatenate([zero(NOPE), fr16, fr16, zero(HP - MQK)]).reshape(1, HP)
    sg = jnp.concatenate([zero(NOPE), -jnp.ones((ROPE // 2,), F32), jnp.ones((ROPE // 2,), F32), zero(HP - MQK)]).reshape(1, HP)

    w_in_t = _perm_w_in_t(wt["w_in"])
    wa_pad = _pad_wa(wt["gla_w_alpha"].T)
    wuq = _pad_wuq(wt["mla_w_uq"].T)
    wukv = _pad_wukv(wt["mla_w_ukv"].T)
    qn_g, kn_g = _pad_lanes(sp["mla_qn_g"]), _pad_lanes(sp["mla_kn_g"])

    tm = _tile(s, 1024)
    h = _norm_mod_fwd(x2, sp["norm1_g"], mod3, 0, 1, "norm1_fwd")
    proj = _mm(h, w_in_t, "nt", (BF16,), "proj_fwd", tn=PW // 4)
    o_gla, og, states = _gla_fwd(proj, wa_pad, sp["gla_b_alpha"], sp["gla_out_norm_g"], bsz)
    q2, k2, v2 = _mla_pre_fwd(proj, pos, fr, sg, sp["mla_q_lat_g"], sp["mla_kv_lat_g"], _repeat_half(qn_g), _repeat_half(kn_g),
                              _repeat_half(wuq), wukv)
    o2, lse2, core_row = _attn_fwd(q2, k2, v2, bsz, slab_b)
    wt = dict(wt, **_unpack_gathered(_cross_core_fill(core_row), SLAB_B))
    wo_pad = _pad_wo(wt["mla_w_o"])
    y_a = _mm(og, wt["gla_w_o"], "nn", (BF16,), "gla_out_fwd")
    y_b = _mm(o2, wo_pad, "nn", (BF16,), "mla_out_fwd")
    mix = _merge_fwd(proj, sp["b_merge"], y_a, y_b)
    mixed, x1, h2 = _mm(mix, wt["w_out"], "nn", (), "w_out_fwd", tm=tm,
                        tail=_norm_fwd_tail(x2, sp["norm2_g"], mod3, 3, 4, 2, tm))
    a, f = _mm(h2, wt["mlp_w1"], "nt", (BF16, BF16), "mlp1_fwd",
               epi=lambda acc: (acc, jnp.square(jnp.maximum(acc, 0.0))))
    dy, dff, acc_g2, acc_loss = _mm(f, wt["mlp_w2"], "nn", (), "mlp2_fwd", tm=tm, tail=_loss_tail(x1, tgt2, mod3, tm))

    gw = {}
    gw["mlp_w2"] = _mm(f, dff, "tn", (BF16,), "mlp2_dw")
    da = _mm(dff, wt["mlp_w2"], "nt", (BF16,), "mlp2_dx", extras=(a,),
             epi=lambda acc, av: (acc * (2.0 * jnp.maximum(av.astype(F32), 0.0)),))
    gw["mlp_w1"] = _mm(da, h2, "tn", (BF16,), "mlp1_dw")
    dx1, dmixed, accb2, accg2 = _mm(da, wt["mlp_w1"], "nn", (), "mlp1_dx", tm=tm,
                                    tail=_norm_bwd_tail(x1, dy, sp["norm2_g"], mod3, 4, tm, mixed=mixed, i_gate=2))

    gw["w_out"] = _mm(mix, dmixed, "tn", (BF16,), "w_out_dw")
    dy_a, dy_b, dlogits, acc_bm = _mm(dmixed, wt["w_out"], "nt", (), "w_out_dx", tm=tm,
                                      tail=_merge_bwd_tail(proj, sp["b_merge"], y_a, y_b, tm))
    gw["gla_w_o"] = _mm(og, dy_a, "tn", (BF16,), "gla_out_dw")
    dog = _mm(dy_a, wt["gla_w_o"], "nt", (BF16,), "gla_out_dx")
    gw["mla_w_o"] = _unpad_wo(_mm(o2, dy_b, "tn", (BF16,), "mla_out_dw"))
    do2 = _mm(dy_b, wo_pad, "nt", (BF16,), "mla_out_dx")
    dq2, dk2, dv2, parts_b = _attn_bwd(q2, k2, v2, do2, o2, lse2, bsz, _pack_per_device(gw, SLAB_B))
    dq_g, dk_g, dv_g, dgg, dz, acc_ba, acc_go = _gla_bwd(dog, o_gla, proj, wa_pad, sp["gla_b_alpha"],
                                                         sp["gla_out_norm_g"], states, bsz)
    gw["gla_w_alpha"] = _mm(proj, dz, "tn", (F32,), "gla_alpha_dw", a_off=43, m=LANES)[:GLR].T.astype(BF16)
    dmisc_gla = _mm(dz, wa_pad, "nt", (F32,), "gla_alpha_dx")
    dcq, dckv, dmisc, gwuq, gwukv, acc_mla = _mla_pre_bwd(dq2, dk2, dv2, dmisc_gla, proj, pos, fr, sg, sp["mla_q_lat_g"],
                                                         sp["mla_kv_lat_g"], qn_g, kn_g, wuq, wukv)
    gw["mla_w_uq"] = _unpad_wuq(gwuq).T.astype(BF16)
    gw["mla_w_ukv"] = _unpad_wukv(gwukv).T.astype(BF16)
    dproj = jnp.concatenate([dq_g, dk_g, dv_g, dgg, dlogits, dcq, dckv, dmisc], axis=1)
    gw["w_in"] = _unperm_w_in_t(_mm(dproj, h, "tn", (BF16,), "proj_dw", tm=PW // 4))
    parts_a, grad_x, accb1, accg1 = _mm(dproj, w_in_t, "nn", (), "proj_dx", tm=tm, tk=PW // 4,
                                        cargo=_sum_over_cores(_pack_per_device(gw, SLAB_A), "a"),
                                        tail=_norm_bwd_tail(x2, dx1, sp["norm1_g"], mod3, 1, tm))

    dmod = jnp.stack([accb1[:, 0], accb1[:, 1], accb2[:, 2], accb2[:, 0], accb2[:, 1], acc_g2[:, 0]], axis=1)

    rows = {
        "dmod": dmod.reshape(bsz * 6, D),
        "norm1_g": accg1[0:1], "norm2_g": accg2[0:1],
        "b_merge": acc_bm[0:1].reshape(2, D),
        "gla_b_alpha": _pad_lanes(acc_ba[0:1], D),
        "gla_out_norm_g": _pad_lanes(acc_go.reshape(GH, 8, GDV)[:, 0, :], D),
        "mla_q_lat_g": _pad_lanes(acc_mla[0:1], D), "mla_kv_lat_g": _pad_lanes(acc_mla[1:2], D),
        "mla_qn_g": _pad_lanes(acc_mla[2:3], D), "mla_kn_g": _pad_lanes(acc_mla[3:4], D),
        "loss": acc_loss[0:1],
    }
    return grad_x.reshape(bsz, s, D), parts_a, parts_b, rows


HBM_SPEC = pl.BlockSpec(memory_space=pltpu.HBM)


def _all_gather(ps, name):
    n = len(ps)

    def body(*refs):
        p_refs, out_refs, (send_sems, recv_sems, local_sems) = refs[:n], refs[n:2 * n], refs[2 * n:]
        x, y, c = lax.axis_index("x"), lax.axis_index("y"), lax.axis_index("c")
        me, sibling = (x, y, c), (x, y, 1 - c)
        chips = [(1 - x, y), (x, 1 - y), (1 - x, 1 - y)]

        def copy(a, k, block, to, own=False):
            slot = out_refs[a].at[4 * block[0] + 2 * block[1] + block[2]]
            return pltpu.make_async_remote_copy(
                src_ref=p_refs[a] if own else slot, dst_ref=slot, send_sem=send_sems.at[7 * a + k],
                recv_sem=recv_sems.at[7 * a + k], device_id=to, device_id_type=MESH)

        mine = [pltpu.make_async_copy(p_refs[a], out_refs[a].at[4 * x + 2 * y + c], local_sems.at[a]) for a in range(n)]
        first = [copy(a, 0, me, sibling, own=True) for a in range(n)]
        first += [copy(a, 1 + j, me, (*chip, c), own=True) for a in range(n) for j, chip in enumerate(chips)]
        for cp in mine + first:
            cp.start()
        passed = []
        for j, chip in enumerate(chips):
            for a in range(n):
                copy(a, 1 + j, (*chip, c), me).wait_recv()
                passed.append(copy(a, 4 + j, (*chip, c), sibling))
                passed[-1].start()
        for a in range(n):
            copy(a, 0, sibling, me).wait_recv()
            for j, chip in enumerate(chips):
                copy(a, 4 + j, (*chip, 1 - c), me).wait_recv()
        for cp in first + passed:
            cp.wait_send()
        for cp in mine:
            cp.wait()

    return pl.pallas_call(
        body, name=name, out_shape=[jax.ShapeDtypeStruct((N_DEV,) + p.shape, p.dtype) for p in ps],
        in_specs=[HBM_SPEC] * n, out_specs=[HBM_SPEC] * n,
        scratch_shapes=[pltpu.SemaphoreType.DMA((7 * n,)), pltpu.SemaphoreType.DMA((7 * n,)), pltpu.SemaphoreType.DMA((n,))],
    )(*ps)


def _sibling_exchange(g, name):
    def body(g_ref, out_ref, send_sem, recv_sem):
        x, y, c = lax.axis_index("x"), lax.axis_index("y"), lax.axis_index("c")
        cp = pltpu.make_async_remote_copy(src_ref=g_ref, dst_ref=out_ref, send_sem=send_sem, recv_sem=recv_sem,
                                          device_id=(x, y, 1 - c), device_id_type=MESH)
        cp.start()
        cp.wait()

    return pl.pallas_call(
        body, name=name, out_shape=jax.ShapeDtypeStruct(g.shape, g.dtype),
        in_specs=[HBM_SPEC], out_specs=HBM_SPEC,
        scratch_shapes=[pltpu.SemaphoreType.DMA(()), pltpu.SemaphoreType.DMA(())],
    )(g)


class _Exchange:
    def __init__(self, local, sends, arrivals):
        self.local, self.sends, self.arrivals = local, sends, arrivals

    def start(self):
        self.local.start()
        for cp in self.sends:
            cp.start()

    def finish(self):
        for cp in self.arrivals:
            cp.wait_recv()
        for cp in self.sends:
            cp.wait_send()
        self.local.wait()


EXCHANGE_SEMS = [pltpu.SemaphoreType.DMA((N_DEV,)), pltpu.SemaphoreType.DMA((N_DEV,)), pltpu.SemaphoreType.DMA(())]


def _all_to_all_copies(t_ref, out_ref, send_sems, recv_sems, local_sem):
    x, y, c = lax.axis_index("x"), lax.axis_index("y"), lax.axis_index("c")
    me = 4 * x + 2 * y + c

    def copy(k, src, dst):
        px, py, pc = (1 - x if k & 4 else x), (1 - y if k & 2 else y), (1 - c if k & 1 else c)
        peer = 4 * px + 2 * py + pc
        return pltpu.make_async_remote_copy(src_ref=t_ref.at[peer if src is None else src],
                                            dst_ref=out_ref.at[peer if dst is None else dst], send_sem=send_sems.at[k],
                                            recv_sem=recv_sems.at[k], device_id=(px, py, pc), device_id_type=MESH)

    return _Exchange(pltpu.make_async_copy(t_ref.at[me], out_ref.at[me], local_sem),
                     [copy(k, None, me) for k in range(1, N_DEV)], [copy(k, me, None) for k in range(1, N_DEV)])


def _chip_exchange_copies(t_ref, out_ref, send_sems, recv_sems, local_sem):
    x, y, c = lax.axis_index("x"), lax.axis_index("y"), lax.axis_index("c")
    my_chip = 2 * x + y
    chips = [(1 - x, y), (x, 1 - y), (1 - x, 1 - y)]

    def copy(j, src, dst, px, py):
        return pltpu.make_async_remote_copy(src_ref=t_ref.at[src], dst_ref=out_ref.at[dst], send_sem=send_sems.at[j],
                                            recv_sem=recv_sems.at[j], device_id=(px, py, c), device_id_type=MESH)

    return _Exchange(pltpu.make_async_copy(t_ref.at[my_chip], out_ref.at[my_chip], local_sem),
                     [copy(j, 2 * px + py, my_chip, px, py) for j, (px, py) in enumerate(chips)],
                     [copy(j, my_chip, 2 * px + py, px, py) for j, (px, py) in enumerate(chips)])


def _core_row_gather_copies(p_ref, out_ref, send_sems, recv_sems, local_sem):
    x, y, c = lax.axis_index("x"), lax.axis_index("y"), lax.axis_index("c")
    peers = [(x, y, 1 - c), (1 - x, y, c), (x, 1 - y, c), (1 - x, 1 - y, c)]

    def slot(px, py, pc):
        return out_ref.at[4 * px + 2 * py + pc]

    def copy(j, block, to):
        return pltpu.make_async_remote_copy(src_ref=p_ref, dst_ref=slot(*block), send_sem=send_sems.at[j],
                                            recv_sem=recv_sems.at[j], device_id=to, device_id_type=MESH)

    return _Exchange(pltpu.make_async_copy(p_ref, slot(x, y, c), local_sem),
                     [copy(j, (x, y, c), peer) for j, peer in enumerate(peers)],
                     [copy(j, peer, peer) for j, peer in enumerate(peers)])


def _cross_core_fill(gathered):
    def body(g_ref, out_ref, send_sems, recv_sems):
        x, y, c = lax.axis_index("x"), lax.axis_index("y"), lax.axis_index("c")
        chips = [(1 - x, y), (x, 1 - y), (1 - x, 1 - y)]

        def copy(j, pc):
            px, py = chips[j]
            slot = 4 * px + 2 * py + pc
            return pltpu.make_async_remote_copy(src_ref=g_ref.at[slot], dst_ref=out_ref.at[slot], send_sem=send_sems.at[j],
                                                recv_sem=recv_sems.at[j], device_id=(x, y, 1 - c), device_id_type=MESH)

        sends = [copy(j, c) for j in range(3)]
        for cp in sends:
            cp.start()
        for j in range(3):
            copy(j, 1 - c).wait_recv()
        for cp in sends:
            cp.wait_send()

    return pl.pallas_call(
        body, name="weights_cross_core_fill", out_shape=jax.ShapeDtypeStruct(gathered.shape, gathered.dtype),
        in_specs=[HBM_SPEC], out_specs=HBM_SPEC, input_output_aliases={0: 0},
        scratch_shapes=[pltpu.SemaphoreType.DMA((3,)), pltpu.SemaphoreType.DMA((3,))],
    )(gathered)


def _slab_block(r):
    return max(b for b in range(16, SLAB_BLOCK_MAX + 1, 16) if r % b == 0)


def _pair_sum(a, b, name):
    n, r, cdim = a.shape
    rb = _slab_block(r)
    blk = pl.BlockSpec((1, rb, cdim), lambda j, i: (j, i, 0))

    def body(a_ref, b_ref, o_ref):
        o_ref[...] = (a_ref[...].astype(F32) + b_ref[...].astype(F32)).astype(BF16)

    return pl.pallas_call(
        body, name=name, grid=(n, r // rb), in_specs=[blk, blk], out_specs=blk,
        out_shape=jax.ShapeDtypeStruct(a.shape, BF16), compiler_params=_cparams(("arbitrary", "arbitrary")),
    )(a, b)


def _adamw_math(w, g, m, v):
    m = ADAM_B1 * m + (1.0 - ADAM_B1) * g
    v = ADAM_B2 * v + (1.0 - ADAM_B2) * jnp.square(g)
    m_hat = m / (1.0 - ADAM_B1 ** ADAM_STEP)
    v_hat = v / (1.0 - ADAM_B2 ** ADAM_STEP)
    delta = -ADAM_LR * (m_hat / (jnp.sqrt(v_hat) + ADAM_EPS) + ADAM_WD * w)
    return delta, m, v


def _slab_sum(parts, name):
    n, r, cdim = parts.shape
    rb = _slab_block(r)
    blk = pl.BlockSpec((rb, cdim), lambda i: (i, 0))

    def body(p_ref, g_out):
        g = p_ref[0].astype(F32)
        for j in range(1, n):
            g = g + p_ref[j].astype(F32)
        g_out[...] = g

    return pl.pallas_call(
        body, name=name, grid=(r // rb,),
        in_specs=[pl.BlockSpec((n, rb, cdim), lambda i: (0, i, 0))], out_specs=blk,
        out_shape=jax.ShapeDtypeStruct((r, cdim), F32), compiler_params=_cparams(("arbitrary",)),
    )(parts)


def _adamw(g, w, m, v, name):
    r, cdim = w.shape
    rb = _tile(r, 256)
    blk = pl.BlockSpec((rb, cdim), lambda i: (i, 0))

    def body(g_ref, w_ref, m_ref, v_ref, d_out, m_out, v_out):
        d_out[...], m_out[...], v_out[...] = _adamw_math(w_ref[...], g_ref[...], m_ref[...], v_ref[...])

    return pl.pallas_call(
        body, name=name, grid=(r // rb,), in_specs=[blk] * 4, out_specs=[blk] * 3,
        out_shape=[jax.ShapeDtypeStruct((r, cdim), F32)] * 3, compiler_params=_cparams(("arbitrary",)),
    )(g, w, m, v)


def _adamw_small(parts, w, m, v):
    def body(p_ref, w_ref, m_ref, v_ref, g_out, d_out, m_out, v_out, loss_out):
        def total(srcs):
            acc = None
            for r in srcs:
                for j in range(N_DEV):
                    term = p_ref[j, r:r + 1, :]
                    acc = term if acc is None else acc + term
            return acc

        for prow, srcs in enumerate(SMALL_SOURCES):
            one = slice(prow, prow + 1)
            g = total(srcs)
            g_out[one, :] = g
            d_out[one, :], m_out[one, :], v_out[one, :] = _adamw_math(w_ref[one, :], g, m_ref[one, :], v_ref[one, :])
        loss_out[...] = jnp.broadcast_to(jnp.sum(total(LOSS_SOURCE), axis=1, keepdims=True), (8, LANES))

    full = lambda shp: pl.BlockSpec(shp, lambda i: (0,) * len(shp))
    return pl.pallas_call(
        body, name="adamw_small", grid=(1,),
        in_specs=[full((N_DEV, SMALL_ROWS, D)), full((16, D)), full((16, D)), full((16, D))],
        out_specs=[full((16, D))] * 4 + [full((8, LANES))],
        out_shape=[jax.ShapeDtypeStruct((16, D), F32)] * 4 + [jax.ShapeDtypeStruct((8, LANES), F32)],
        compiler_params=_cparams(("arbitrary",)),
    )(parts, w, m, v)


def _to_slab(shard, form):
    if form == "N":
        return shard
    return shard.T if form == "T" else shard.T.reshape(-1, D)


def _from_slab(block, form, shard_shape):
    if form == "N":
        return block
    return block.T if form == "T" else block.reshape(shard_shape[1], shard_shape[0]).T


def _gathered_full(g, form, shard_shape):
    if form == "TR":
        return g.reshape(N_DEV * shard_shape[1], shard_shape[0])
    return g.reshape(N_DEV * g.shape[1], D)


def _pack_slab(shards, layout):
    return jnp.concatenate([jnp.zeros((r, D), BF16) if n is None else _to_slab(shards[n], form).astype(BF16)
                            for n, r, form in layout], axis=0)


def _unpack_gathered(gathered, layout):
    out, off = {}, 0
    for n, r, form in layout:
        if n is not None:
            out[n] = _gathered_full(gathered[:, off:off + r], form, SHARD_SHAPES[n])
        off += r
    return out


def _pack_per_device(gw, layout):
    return jnp.concatenate([jnp.zeros((N_DEV, r, D), BF16) if n is None else gw[n].reshape(N_DEV, r, D)
                            for n, r, _ in layout], axis=1)


def _adamw_shards(gslab, layout, wts, mom, var):
    out, off = {}, 0
    for n, r, form in layout:
        if n is not None:
            g = _from_slab(gslab[off:off + r], form, SHARD_SHAPES[n])
            out[n] = (g,) + tuple(_adamw(g, wts[n], mom[n], var[n], "adamw_" + n))
        off += r
    return out


def _sum_over_cores(per_dev, tag):
    my_c = lax.axis_index("c")
    pairs = per_dev.reshape(4, 2, per_dev.shape[1], D)
    keep = lax.dynamic_index_in_dim(pairs, my_c, axis=1, keepdims=False)
    give = lax.dynamic_index_in_dim(pairs, 1 - my_c, axis=1, keepdims=False)
    return _pair_sum(keep, _sibling_exchange(give, "rs_sibling_exchange_" + tag), "rs_pair_sum_" + tag)


def _small_pack(vals):
    rows = []
    for n in SMALL:
        v = vals[n].reshape(-1)
        k = -(-v.shape[0] // D)
        rows.append(jnp.pad(v, (0, k * D - v.shape[0])).reshape(k, D))
    return jnp.concatenate(rows, axis=0)


def _small_unpack(packed, shapes):
    out = {}
    for n in SMALL:
        k = shapes[n][-1]
        r0 = SMALL_ROW[n]
        out[n] = packed[r0:r0 + -(-k // D)].reshape(-1)[:k].reshape(shapes[n])
    return out


def kernel(x, c, positions, w_ada, b_ada, norm1_g, w_in, b_merge, gla_w_alpha, gla_b_alpha, gla_out_norm_g, gla_w_o, mla_q_lat_g, mla_w_uq, mla_kv_lat_g, mla_w_ukv, mla_qn_g, mla_kn_g, mla_w_o, w_out, norm2_g, mlp_w1, mlp_w2, loss_target, m_w_ada, m_b_ada, m_norm1_g, m_w_in, m_b_merge, m_gla_w_alpha, m_gla_b_alpha, m_gla_out_norm_g, m_gla_w_o, m_mla_q_lat_g, m_mla_w_uq, m_mla_kv_lat_g, m_mla_w_ukv, m_mla_qn_g, m_mla_kn_g, m_mla_w_o, m_w_out, m_norm2_g, m_mlp_w1, m_mlp_w2, v_w_ada, v_b_ada, v_norm1_g, v_w_in, v_b_merge, v_gla_w_alpha, v_gla_b_alpha, v_gla_out_norm_g, v_gla_w_o, v_mla_q_lat_g, v_mla_w_uq, v_mla_kv_lat_g, v_mla_w_ukv, v_mla_qn_g, v_mla_kn_g, v_mla_w_o, v_w_out, v_norm2_g, v_mlp_w1, v_mlp_w2):
    args = dict(locals())
    wts = {n: args[n][0] for n in WEIGHTS}
    mom = {n: args["m_" + n][0] for n in WEIGHTS}
    var = {n: args["v_" + n][0] for n in WEIGHTS}
    my_c = lax.axis_index("c")
    my_dev = 4 * lax.axis_index("x") + 2 * lax.axis_index("y") + my_c
    bsz = x.shape[0]
    sp = {n: wts[n].reshape(1, -1) for n in SMALL}

    gathered_a, c_gathered = _all_gather([_pack_slab(wts, SLAB_A), jnp.pad(c, ((0, 8 - bsz), (0, 0)))], "weights_all_gather")
    wt = _unpack_gathered(gathered_a, SLAB_A)

    c_all = c_gathered[:, :bsz].reshape(N_DEV * bsz, D)
    bias = lax.dynamic_slice_in_dim(sp["b_ada"], my_dev * ADA_COLS, ADA_COLS, axis=1)
    mod_cols = _mm(c_all, wts["w_ada"], "nn", (F32,), "ada_fwd", pro=_silu, epi=lambda acc, b: (acc + b,),
                   extras=(jnp.broadcast_to(bias, (N_DEV * bsz, ADA_COLS)),))
    mod_all, = _all_gather([mod_cols], "mod_all_gather")
    mod_mine = lax.dynamic_slice_in_dim(mod_all, my_dev * bsz, bsz, axis=1)
    mod3 = jnp.transpose(mod_mine, (1, 0, 2)).reshape(bsz, 6, D)

    grad_x, parts_a, parts_b, rows = _local_step(x, positions, loss_target, wt, _pack_slab(wts, SLAB_B), sp, mod3)

    big = dict(_adamw_shards(_slab_sum(parts_a, "rs_slab_sum_a"), SLAB_A, wts, mom, var),
               **_adamw_shards(_slab_sum(parts_b, "rs_slab_sum_b"), SLAB_B, wts, mom, var))

    order = ["dmod", "norm1_g", "norm2_g", "b_merge", "gla_b_alpha", "gla_out_norm_g", "mla_q_lat_g", "mla_kv_lat_g",
             "mla_qn_g", "mla_kn_g", "loss"]
    part_rows = jnp.concatenate([rows[n] for n in order], axis=0)
    part_rows = jnp.pad(part_rows, ((0, SMALL_ROWS - part_rows.shape[0]), (0, 0)))
    all_rows, = _all_gather([part_rows], "partials_all_gather")

    dmod_all = all_rows[:, :6 * bsz].reshape(N_DEV * bsz, 6 * D)
    dmod_cols = lax.dynamic_slice_in_dim(dmod_all, my_dev * ADA_COLS, ADA_COLS, axis=1)
    g_ada = _mm(c_all, dmod_cols, "tn", (F32,), "ada_dw", pro=_silu)
    big["w_ada"] = (g_ada,) + tuple(_adamw(g_ada, wts["w_ada"], mom["w_ada"], var["w_ada"], "adamw_w_ada"))

    small = _adamw_small(all_rows, _small_pack({n: wts[n] for n in SMALL}), _small_pack({n: mom[n] for n in SMALL}),
                         _small_pack({n: var[n] for n in SMALL}))
    loss = small[4][0, 0]
    small_shapes = {n: wts[n].shape for n in SMALL}
    small = [_small_unpack(o, small_shapes) for o in small[:4]]

    outs = [loss, grad_x]
    for k in range(4):
        for n in WEIGHTS:
            val = big[n][k] if n in BIG else small[k][n]
            outs.append(val.reshape((1,) + tuple(wts[n].shape)))
    return tuple(outs)
```

```python
import jax
import jax.numpy as jnp
from jax import lax
from jax.experimental import pallas as pl
from jax.experimental.pallas import tpu as pltpu

F32 = jnp.float32
BF16 = jnp.bfloat16
MESH = pl.DeviceIdType.MESH

D = 1024
EPS = 1e-6
CHUNK = 64
GH, GDK, GDV, GLR, GTAU = 4, 128, 256, 16, 16.0
MH, MQR, MKVR, NOPE, ROPE, MV = 16, 256, 128, 64, 32, 64
MQK = NOPE + ROPE
HP = 128
FF = 4 * D
ROPE_THETA = 10000.0
IN_WIDTH = 5552
PW = 5632
N_DEV = 8
LANES = 128
SLAB_BLOCK_MAX = 400
ADA_COLS = 6 * D // N_DEV
SMALL_ROWS = 32
SMALL_SOURCES = tuple([(r, 6 + r) for r in range(6)] + [(12,), (13,), (14,), (15,), (16,), (17, 18, 19, 20),
                                                         (21,), (22,), (23,), (24,)])
LOSS_SOURCE = (25,)
VMEM_LIMIT = 60 * 1024 * 1024

ADAM_LR, ADAM_B1, ADAM_B2, ADAM_EPS, ADAM_WD, ADAM_STEP = 0.001, 0.9, 0.999, 1e-08, 0.01, 10

SLAB_A = (("w_in", 694, "T"), ("gla_w_alpha", 1, "TR"), (None, 9, None), ("mla_w_uq", 48, "TR"), ("mla_w_ukv", 32, "TR"))
SLAB_B = (("mlp_w1", 512, "T"), ("gla_w_o", 128, "N"), ("mla_w_o", 128, "N"), ("w_out", 128, "N"), ("mlp_w2", 512, "N"))
BIG = ("w_ada",) + tuple(n for n, _, _ in SLAB_A + SLAB_B if n is not None)
SHARD_SHAPES = {"w_ada": (D, 6 * D // N_DEV), "w_in": (D, IN_WIDTH // N_DEV), "gla_w_alpha": (GLR, GH * GDK // N_DEV),
                "gla_w_o": (GH * GDV // N_DEV, D), "mla_w_uq": (MQR, MH * MQK // N_DEV),
                "mla_w_ukv": (MKVR, MH * (NOPE + MV) // N_DEV), "mla_w_o": (MH * MV // N_DEV, D), "w_out": (D // N_DEV, D),
                "mlp_w1": (D, FF // N_DEV), "mlp_w2": (FF // N_DEV, D)}
SMALL = ("b_ada", "norm1_g", "norm2_g", "b_merge", "gla_b_alpha", "gla_out_norm_g", "mla_q_lat_g", "mla_kv_lat_g",
         "mla_qn_g", "mla_kn_g")
SMALL_ROW = {"b_ada": 0, "norm1_g": 6, "norm2_g": 7, "b_merge": 8, "gla_b_alpha": 10, "gla_out_norm_g": 11,
             "mla_q_lat_g": 12, "mla_kv_lat_g": 13, "mla_qn_g": 14, "mla_kn_g": 15}
WEIGHTS = ("w_ada", "b_ada", "norm1_g", "w_in", "b_merge", "gla_w_alpha", "gla_b_alpha", "gla_out_norm_g", "gla_w_o",
           "mla_q_lat_g", "mla_w_uq", "mla_kv_lat_g", "mla_w_ukv", "mla_qn_g", "mla_kn_g", "mla_w_o", "w_out",
           "norm2_g", "mlp_w1", "mlp_w2")


def _cparams(sem=None):
    return pltpu.CompilerParams(dimension_semantics=sem, vmem_limit_bytes=VMEM_LIMIT)


def _tile(n, pref):
    for t in (2048, PW // 4, 1024, 512, 256, 128):
        if t <= pref and n % t == 0:
            return t
    return n


def _dot(a, b, dims, precision=None):
    return lax.dot_general(a, b, (dims, ((), ())), preferred_element_type=F32, precision=precision)


NN = ((1,), (0,))
NT = ((1,), (1,))
TN = ((0,), (0,))


def _sigmoid(x):
    return 1.0 / (1.0 + jnp.exp(-x))


def _silu(x):
    return x * _sigmoid(x)


def _mm(a, b, mode, out_dtypes, name, *, pro=None, pro_b=None, epi=None, extras=(), a_off=0, m=None, tm=2048, tn=1024,
        tk=1024, cargo=None, tail=None):
    if mode == "tn":
        kc, n = b.shape
        m = a.shape[1] if m is None else m
    elif mode == "nn":
        m, kc = a.shape
        n = b.shape[1]
    else:
        m, kc = a.shape
        n = b.shape[0]
    tm, tn, tk = _tile(m, tm), _tile(n, tn), _tile(kc, tk)
    nk = kc // tk
    dims = {"nn": NN, "nt": NT, "tn": TN}[mode]
    if mode == "tn":
        a_spec = pl.BlockSpec((tk, tm), lambda i, j, k: (k, i + a_off))
    else:
        a_spec = pl.BlockSpec((tm, tk), lambda i, j, k: (i + a_off, k))
    if mode == "nt":
        b_spec = pl.BlockSpec((tn, tk), lambda i, j, k: (j, k))
    else:
        b_spec = pl.BlockSpec((tk, tn), lambda i, j, k: (k, j))
    o_spec = pl.BlockSpec((tm, tn), lambda i, j, k: (i, j))
    n_ex, n_out = len(extras), len(out_dtypes)
    grid = (m // tm, n // tn, nk)
    has_cargo = cargo is not None
    t_ins, t_outs = (tail["ins"], tail["outs"]) if tail else ([], [])
    assert not tail or grid[1] == 1

    def body(a_ref, b_ref, *rest):
        rest = list(rest)
        take = lambda count: [rest.pop(0) for _ in range(count)]
        ex, cargo_ref, tail_in = take(n_ex), take(has_cargo), take(len(t_ins))
        outs, parts_ref, tail_out = take(n_out), take(has_cargo), take(len(t_outs))
        acc = rest.pop(0)
        steps = [pl.program_id(axis) for axis in range(3)]
        if has_cargo:
            exchange = _chip_exchange_copies(cargo_ref[0], parts_ref[0], *rest)
            first = (steps[0] == 0) & (steps[1] == 0) & (steps[2] == 0)
            last = (steps[0] == grid[0] - 1) & (steps[1] == grid[1] - 1) & (steps[2] == grid[2] - 1)
            pl.when(first)(exchange.start)
        k = steps[2]

        @pl.when(k == 0)
        def _():
            acc[...] = jnp.zeros_like(acc)

        av = a_ref[...]
        if pro is not None:
            av = pro(av)
        bv = b_ref[...]
        if pro_b is not None:
            bv = pro_b(bv)
        acc[...] += _dot(av.astype(BF16), bv.astype(BF16), dims)

        @pl.when(k == nk - 1)
        def _():
            if tail:
                tail["fn"](acc[...], steps[0], *tail_in, *tail_out)
            res = (acc[...],) if epi is None else epi(acc[...], *[e[...] for e in ex])
            for o_ref, r in zip(outs, res):
                o_ref[...] = r.astype(o_ref.dtype)

        if has_cargo:
            pl.when(last)(exchange.finish)

    cargo_in = [cargo] if has_cargo else []
    cargo_spec = [HBM_SPEC] * len(cargo_in)
    sequential = has_cargo or bool(tail)
    out = pl.pallas_call(
        body, name=name, grid=grid,
        in_specs=[a_spec, b_spec] + [o_spec] * n_ex + cargo_spec + (tail["in_specs"] if tail else []),
        out_specs=[o_spec] * n_out + cargo_spec + (tail["out_specs"] if tail else []),
        out_shape=[jax.ShapeDtypeStruct((m, n), dt) for dt in out_dtypes]
        + [jax.ShapeDtypeStruct(c.shape, c.dtype) for c in cargo_in] + t_outs,
        scratch_shapes=[pltpu.VMEM((tm, tn), F32)] + (EXCHANGE_SEMS if has_cargo else []),
        compiler_params=_cparams(("arbitrary",) * 3 if sequential else ("parallel", "parallel", "arbitrary")),
    )(a, b, *extras, *cargo_in, *t_ins)
    return out[0] if len(out) == 1 else out


def _rows(s):
    return _tile(s, 512)


def _mod_spec():
    return pl.BlockSpec((1, 6, D), lambda b, i: (b, 0, 0))


def _tok_spec(tr, nb, width=D, col=0):
    return pl.BlockSpec((tr, width), lambda b, i: (b * nb + i, col))


def _modulated_norm(xv, gv, mod_ref, i_shift, i_scale):
    r = lax.rsqrt(jnp.mean(xv * xv, axis=1, keepdims=True) + EPS)
    return ((xv * r) * gv) * (1.0 + mod_ref[0, i_scale:i_scale + 1, :]) + mod_ref[0, i_shift:i_shift + 1, :]


def _norm_mod_fwd(x, g, mod3, i_shift, i_scale, name):
    bsz, _, _ = mod3.shape
    t = x.shape[0]
    s = t // bsz
    tr = _rows(s)
    nb = s // tr

    def body(x_ref, g_ref, mod_ref, h_ref):
        h_ref[...] = _modulated_norm(x_ref[...], g_ref[...], mod_ref, i_shift, i_scale).astype(BF16)

    tok = _tok_spec(tr, nb)
    return pl.pallas_call(
        body, name=name, grid=(bsz, nb), in_specs=[tok, pl.BlockSpec((1, D), lambda b, i: (0, 0)), _mod_spec()],
        out_specs=tok, out_shape=jax.ShapeDtypeStruct((t, D), BF16), compiler_params=_cparams(("arbitrary", "arbitrary")),
    )(x, g, mod3)


def _norm_fwd_tail(x, g, mod3, i_shift, i_scale, i_gate, tm):
    bsz, t = mod3.shape[0], x.shape[0]
    per_b = t // bsz // tm

    def fn(mixedv, i, x_ref, g_ref, mod_ref, mixed_ref, x1_ref, h_ref):
        mixed_ref[...] = mixedv.astype(BF16)
        xv = x_ref[...] + mod_ref[0, i_gate:i_gate + 1, :] * mixedv
        x1_ref[...] = xv
        h_ref[...] = _modulated_norm(xv, g_ref[...], mod_ref, i_shift, i_scale).astype(BF16)

    tok = pl.BlockSpec((tm, D), lambda i, j, k: (i, 0))
    return dict(
        fn=fn, ins=[x, g, mod3], in_specs=[tok, pl.BlockSpec((1, D), lambda i, j, k: (0, 0)), _tail_mod_spec(per_b)],
        outs=[jax.ShapeDtypeStruct((t, D), BF16), jax.ShapeDtypeStruct((t, D), F32), jax.ShapeDtypeStruct((t, D), BF16)],
        out_specs=[tok, tok, tok])


def _norm_bwd_rows(xv, dhv, dresv, gv, mod_ref, i_scale, accb, accg):
    r = lax.rsqrt(jnp.mean(xv * xv, axis=1, keepdims=True) + EPS)
    xn = xv * r
    accb[0, 0:1, :] += jnp.sum(dhv, axis=0, keepdims=True)
    accb[0, 1:2, :] += jnp.sum(dhv * (xn * gv), axis=0, keepdims=True)
    tt = dhv * (1.0 + mod_ref[0, i_scale:i_scale + 1, :])
    accg[0:1, :] += jnp.sum(tt * xn, axis=0, keepdims=True)
    dxn = tt * gv
    return dresv + r * (dxn - xn * jnp.mean(dxn * xn, axis=1, keepdims=True))


def _norm_bwd_tail(x, dres, g, mod3, i_scale, tm, mixed=None, i_gate=None):
    bsz, t = mod3.shape[0], x.shape[0]
    per_b = t // bsz // tm
    has_res = mixed is not None

    def fn(dhv, i, *refs):
        if has_res:
            x_ref, dres_ref, mx_ref, g_ref, mod_ref, dx_ref, dmx_ref, accb, accg = refs
        else:
            x_ref, dres_ref, g_ref, mod_ref, dx_ref, accb, accg = refs

        @pl.when(i % per_b == 0)
        def _():
            accb[...] = jnp.zeros_like(accb)

        @pl.when(i == 0)
        def _():
            accg[...] = jnp.zeros_like(accg)

        dx = _norm_bwd_rows(x_ref[...], dhv, dres_ref[...], g_ref[...], mod_ref, i_scale, accb, accg)
        dx_ref[...] = dx
        if has_res:
            accb[0, 2:3, :] += jnp.sum(dx * mx_ref[...].astype(F32), axis=0, keepdims=True)
            dmx_ref[...] = (dx * mod_ref[0, i_gate:i_gate + 1, :]).astype(BF16)

    tok = pl.BlockSpec((tm, D), lambda i, j, k: (i, 0))
    res = [mixed] if has_res else []
    return dict(
        fn=fn, ins=[x, dres] + res + [g, mod3],
        in_specs=[tok, tok] + [tok] * len(res) + [pl.BlockSpec((1, D), lambda i, j, k: (0, 0)), _tail_mod_spec(per_b)],
        outs=[jax.ShapeDtypeStruct((t, D), F32)] + [jax.ShapeDtypeStruct((t, D), BF16)] * len(res)
        + [jax.ShapeDtypeStruct((bsz, 8, D), F32), jax.ShapeDtypeStruct((8, D), F32)],
        out_specs=[tok] * (1 + len(res)) + [_tail_batch_spec(per_b), pl.BlockSpec((8, D), lambda i, j, k: (0, 0))])


def _tail_mod_spec(per_b):
    return pl.BlockSpec((1, 6, D), lambda i, j, k: (i // per_b, 0, 0))


def _tail_batch_spec(per_b):
    return pl.BlockSpec((1, 8, D), lambda i, j, k: (i // per_b, 0, 0))


def _loss_tail(x1, tgt, mod3, tm):
    bsz, t = mod3.shape[0], x1.shape[0]
    per_b = t // bsz // tm

    def fn(ffv, i, x1_ref, tg_ref, mod_ref, dy_ref, dff_ref, accb, accl):
        @pl.when(i % per_b == 0)
        def _():
            accb[...] = jnp.zeros_like(accb)

        @pl.when(i == 0)
        def _():
            accl[...] = jnp.zeros_like(accl)

        gate = mod_ref[0, 5:6, :]
        err = x1_ref[...] + gate * ffv - tg_ref[...]
        accl[0:1, :] += jnp.sum(err * err, axis=0, keepdims=True) * (0.5 / D)
        dy = err * (1.0 / D)
        dy_ref[...] = dy
        dff_ref[...] = (dy * gate).astype(BF16)
        accb[0, 0:1, :] += jnp.sum(dy * ffv, axis=0, keepdims=True)

    tok = pl.BlockSpec((tm, D), lambda i, j, k: (i, 0))
    return dict(
        fn=fn, ins=[x1, tgt, mod3], in_specs=[tok, tok, _tail_mod_spec(per_b)],
        outs=[jax.ShapeDtypeStruct((t, D), F32), jax.ShapeDtypeStruct((t, D), BF16),
              jax.ShapeDtypeStruct((bsz, 8, D), F32), jax.ShapeDtypeStruct((8, D), F32)],
        out_specs=[tok, tok, _tail_batch_spec(per_b), pl.BlockSpec((8, D), lambda i, j, k: (0, 0))])


def _merge_fwd(proj, b_merge, y_a, y_b):
    t = proj.shape[0]
    tr = _tile(t, 512)

    def body(la_ref, lb_ref, bm_ref, ya_ref, yb_ref, mix_ref):
        ga = _sigmoid(la_ref[...] + bm_ref[:, 0:D])
        gb = _sigmoid(lb_ref[...] + bm_ref[:, D:2 * D])
        mix_ref[...] = (ga * ya_ref[...].astype(F32) + gb * yb_ref[...].astype(F32)).astype(BF16)

    tok = pl.BlockSpec((tr, D), lambda i: (i, 0))
    return pl.pallas_call(
        body, name="merge_fwd", grid=(t // tr,),
        in_specs=[pl.BlockSpec((tr, D), lambda i: (i, 3)), pl.BlockSpec((tr, D), lambda i: (i, 4)),
                  pl.BlockSpec((1, 2 * D), lambda i: (0, 0)), tok, tok],
        out_specs=tok, out_shape=jax.ShapeDtypeStruct((t, D), BF16),
        compiler_params=_cparams(("arbitrary",)),
    )(proj, proj, b_merge, y_a, y_b)


def _merge_bwd_tail(proj, b_merge, y_a, y_b, tm):
    t = proj.shape[0]

    def fn(dm, i, la_ref, lb_ref, bm_ref, ya_ref, yb_ref, dya_ref, dyb_ref, dl_ref, acc):
        @pl.when(i == 0)
        def _():
            acc[...] = jnp.zeros_like(acc)

        ga = _sigmoid(la_ref[...] + bm_ref[:, 0:D])
        gb = _sigmoid(lb_ref[...] + bm_ref[:, D:2 * D])
        dya_ref[...] = (dm * ga).astype(BF16)
        dyb_ref[...] = (dm * gb).astype(BF16)
        dla = dm * ya_ref[...].astype(F32) * ga * (1.0 - ga)
        dlb = dm * yb_ref[...].astype(F32) * gb * (1.0 - gb)
        dl_ref[:, 0:D] = dla.astype(BF16)
        dl_ref[:, D:2 * D] = dlb.astype(BF16)
        acc[0:1, 0:D] += jnp.sum(dla, axis=0, keepdims=True)
        acc[0:1, D:2 * D] += jnp.sum(dlb, axis=0, keepdims=True)

    tok = pl.BlockSpec((tm, D), lambda i, j, k: (i, 0))
    return dict(
        fn=fn, ins=[proj, proj, b_merge, y_a, y_b],
        in_specs=[pl.BlockSpec((tm, D), lambda i, j, k: (i, 3)), pl.BlockSpec((tm, D), lambda i, j, k: (i, 4)),
                  pl.BlockSpec((1, 2 * D), lambda i, j, k: (0, 0)), tok, tok],
        outs=[jax.ShapeDtypeStruct((t, D), BF16), jax.ShapeDtypeStruct((t, D), BF16),
              jax.ShapeDtypeStruct((t, 2 * D), BF16), jax.ShapeDtypeStruct((8, 2 * D), F32)],
        out_specs=[tok, tok, pl.BlockSpec((tm, 2 * D), lambda i, j, k: (i, 0)), pl.BlockSpec((8, 2 * D), lambda i, j, k: (0, 0))])


GLA_HEADS = 2
GLA_UNROLL = 16


def _log_sigmoid(z):
    return jnp.minimum(z, 0.0) - jnp.log(1.0 + jnp.exp(-jnp.abs(z)))


def _tri(lower):
    r = lax.broadcasted_iota(jnp.int32, (CHUNK, CHUNK), 0)
    c = lax.broadcasted_iota(jnp.int32, (CHUNK, CHUNK), 1)
    return jnp.where(r >= c if lower else r <= c, 1.0, 0.0).astype(F32)


def _gla_fwd(proj, wa_pad, b_alpha, g_out, bsz):
    t = proj.shape[0]
    s = t // bsz
    nc = s // CHUNK
    p, kw, vw = GLA_HEADS, GLA_HEADS * GDK, GLA_HEADS * GDV

    def body(q_ref, k_ref, v_ref, gg_ref, ms_ref, wa_ref, ba_ref, go_ref, o_ref, og_ref, st_ref, la, state):
        z = _dot(ms_ref[...].astype(BF16), wa_ref[...], NN) + ba_ref[...]
        la[...] = _log_sigmoid(z) * (1.0 / GTAU)
        state[...] = jnp.zeros_like(state)
        low = _tri(True)
        gout = go_ref[...]

        def chunk(n, carry):
            rows = pl.ds(pl.multiple_of(n * CHUNK, CHUNK), CHUNK)
            for hh in range(p):
                kc, vc = slice(hh * GDK, (hh + 1) * GDK), slice(hh * GDV, (hh + 1) * GDV)
                lac = la[rows, kc]
                cum = _dot(low, lac, NN, lax.Precision.HIGHEST)
                ce = jnp.sum(lac, axis=0, keepdims=True)
                kd = (k_ref[rows, kc].astype(F32) * jnp.exp(ce - cum)).astype(BF16)
                new = state[vc, :] * jnp.exp(ce) + _dot(v_ref[rows, vc].astype(BF16), kd, TN)
                state[vc, :] = new
                st_ref[pl.ds(pl.multiple_of((hh * nc + n) * GDV, GDV), GDV), :] = new.astype(BF16)
                qs = (q_ref[rows, kc].astype(F32) * (GDK ** -0.5)).astype(BF16)
                o = _dot(qs, new.astype(BF16), NT)
                o_ref[rows, vc] = o
                ro = lax.rsqrt(jnp.mean(o * o, axis=1, keepdims=True) + EPS)
                og_ref[rows, vc] = (((o * ro) * gout) * _silu(gg_ref[rows, vc].astype(F32))).astype(BF16)
            return carry

        lax.fori_loop(0, nc, chunk, 0, unroll=GLA_UNROLL)

    return pl.pallas_call(
        body, name="gla_fwd", grid=(bsz, GH // p),
        in_specs=[pl.BlockSpec((s, kw), lambda b, h: (b, h)), pl.BlockSpec((s, kw), lambda b, h: (b, GH // p + h)),
                  pl.BlockSpec((s, vw), lambda b, h: (b, GH // p + h)), pl.BlockSpec((s, vw), lambda b, h: (b, 2 * GH // p + h)),
                  pl.BlockSpec((s, LANES), lambda b, h: (b, 43)),
                  pl.BlockSpec((LANES, kw), lambda b, h: (0, h)), pl.BlockSpec((1, kw), lambda b, h: (0, h)),
                  pl.BlockSpec((1, GDV), lambda b, h: (0, 0))],
        out_specs=[pl.BlockSpec((s, vw), lambda b, h: (b, h)), pl.BlockSpec((s, vw), lambda b, h: (b, h)),
                   pl.BlockSpec((p * nc * GDV, GDK), lambda b, h: (b * (GH // p) + h, 0))],
        out_shape=[jax.ShapeDtypeStruct((t, GH * GDV), F32), jax.ShapeDtypeStruct((t, GH * GDV), BF16),
                   jax.ShapeDtypeStruct((bsz * GH * nc * GDV, GDK), BF16)],
        scratch_shapes=[pltpu.VMEM((s, kw), F32), pltpu.VMEM((vw, GDK), F32)],
        compiler_params=_cparams(("arbitrary", "arbitrary")),
    )(proj, proj, proj, proj, proj, wa_pad, b_alpha, g_out)


def _gla_bwd(dog, o, proj, wa_pad, b_alpha, g_out, states, bsz):
    t = proj.shape[0]
    s = t // bsz
    nc = s // CHUNK
    p, kw, vw = GLA_HEADS, GLA_HEADS * GDK, GLA_HEADS * GDV

    def body(dog_ref, o_ref, q_ref, k_ref, v_ref, gg_ref, ms_ref, wa_ref, ba_ref, go_ref, st_ref,
             dq_ref, dk_ref, dv_ref, dgg_ref, dz_ref, dba, dgo, zs, la, carry_g):
        @pl.when(pl.program_id(1) == 0)
        def _():
            dba[...] = jnp.zeros_like(dba)
            dgo[...] = jnp.zeros_like(dgo)

        z = _dot(ms_ref[...].astype(BF16), wa_ref[...], NN) + ba_ref[...]
        zs[...] = z
        la[...] = _log_sigmoid(z) * (1.0 / GTAU)
        carry_g[...] = jnp.zeros_like(carry_g)
        low, upp = _tri(True), _tri(False)
        gout = go_ref[...]
        last_row = lax.broadcasted_iota(jnp.int32, (CHUNK, GDK), 0) == CHUNK - 1

        def chunk(step, carry):
            n = nc - 1 - step
            rows = pl.ds(pl.multiple_of(n * CHUNK, CHUNK), CHUNK)
            for hh in range(p):
                kc, vc = slice(hh * GDK, (hh + 1) * GDK), slice(hh * GDV, (hh + 1) * GDV)
                lac = la[rows, kc]
                cum = _dot(low, lac, NN, lax.Precision.HIGHEST)
                ce = jnp.sum(lac, axis=0, keepdims=True)
                e = jnp.exp(ce - cum)
                dec = jnp.exp(ce)
                kf = k_ref[rows, kc].astype(F32)
                kd = (kf * e).astype(BF16)
                vv = v_ref[rows, vc].astype(BF16)
                qs = (q_ref[rows, kc].astype(F32) * (GDK ** -0.5)).astype(BF16)
                ov = o_ref[rows, vc]
                ro = lax.rsqrt(jnp.mean(ov * ov, axis=1, keepdims=True) + EPS)
                on = ov * ro
                gg = gg_ref[rows, vc].astype(F32)
                sg = _sigmoid(gg)
                dogv = dog_ref[rows, vc].astype(F32)
                dgg_ref[rows, vc] = (dogv * (on * gout) * (sg * (1.0 + gg * (1.0 - sg)))).astype(BF16)
                t1 = dogv * (gg * sg)
                dgo[8 * hh:8 * hh + 1, :] += jnp.sum(t1 * on, axis=0, keepdims=True)
                don = t1 * gout
                do = ro * (don - on * jnp.mean(don * on, axis=1, keepdims=True))
                dob = do.astype(BF16)
                st_n = st_ref[pl.ds(pl.multiple_of((hh * nc + n) * GDV, GDV), GDV), :]
                dq_ref[rows, kc] = (_dot(dob, st_n, NN) * (GDK ** -0.5)).astype(BF16)
                dn = carry_g[vc, :] + _dot(dob, qs, TN)
                prev = hh * nc + jnp.maximum(n - 1, 0)
                st_p = st_ref[pl.ds(pl.multiple_of(prev * GDV, GDV), GDV), :].astype(F32) * jnp.where(n > 0, 1.0, 0.0)
                ddec = jnp.sum(dn * st_p, axis=0, keepdims=True)
                dnb = dn.astype(BF16)
                dkd = _dot(vv, dnb, NN)
                dv_ref[rows, vc] = _dot(kd, dnb, NT).astype(BF16)
                dk_ref[rows, kc] = (dkd * e).astype(BF16)
                w = dkd * kf * e
                dce = jnp.sum(w, axis=0, keepdims=True) + ddec * dec
                dcum = jnp.where(last_row, dce - w, -w)
                dla = _dot(upp, dcum, NN, lax.Precision.HIGHEST)
                dz = dla * (1.0 / GTAU) * _sigmoid(-zs[rows, kc])
                dba[0:1, kc] += jnp.sum(dz, axis=0, keepdims=True)
                dz_ref[rows, kc] = dz.astype(BF16)
                carry_g[vc, :] = dn * dec
            return carry

        lax.fori_loop(0, nc, chunk, 0, unroll=GLA_UNROLL)

    hv = pl.BlockSpec((s, vw), lambda h, b: (b, h))
    hk = pl.BlockSpec((s, kw), lambda h, b: (b, h))
    return pl.pallas_call(
        body, name="gla_bwd", grid=(GH // p, bsz),
        in_specs=[hv, hv, hk, pl.BlockSpec((s, kw), lambda h, b: (b, GH // p + h)),
                  pl.BlockSpec((s, vw), lambda h, b: (b, GH // p + h)), pl.BlockSpec((s, vw), lambda h, b: (b, 2 * GH // p + h)),
                  pl.BlockSpec((s, LANES), lambda h, b: (b, 43)), pl.BlockSpec((LANES, kw), lambda h, b: (0, h)),
                  pl.BlockSpec((1, kw), lambda h, b: (0, h)), pl.BlockSpec((1, GDV), lambda h, b: (0, 0)),
                  pl.BlockSpec((p * nc * GDV, GDK), lambda h, b: (b * (GH // p) + h, 0))],
        out_specs=[hk, hk, hv, hv, hk, pl.BlockSpec((8, kw), lambda h, b: (0, h)),
                   pl.BlockSpec((8 * p, GDV), lambda h, b: (h, 0))],
        out_shape=[jax.ShapeDtypeStruct((t, GH * GDK), BF16), jax.ShapeDtypeStruct((t, GH * GDK), BF16),
                   jax.ShapeDtypeStruct((t, GH * GDV), BF16), jax.ShapeDtypeStruct((t, GH * GDV), BF16),
                   jax.ShapeDtypeStruct((t, GH * GDK), BF16), jax.ShapeDtypeStruct((8, GH * GDK), F32),
                   jax.ShapeDtypeStruct((8 * GH, GDV), F32)],
        scratch_shapes=[pltpu.VMEM((s, kw), F32), pltpu.VMEM((s, kw), F32), pltpu.VMEM((vw, GDK), F32)],
        compiler_params=_cparams(("arbitrary", "arbitrary")),
    )(dog, o, proj, proj, proj, proj, proj, wa_pad, b_alpha, g_out, states)


def _rope_tables(pos_ref, fr_ref, sg_ref):
    ang = pos_ref[...].astype(F32) * fr_ref[...]
    return jnp.cos(ang), jnp.sin(ang) * sg_ref[...]


def _partner(x):
    lane = lax.broadcasted_iota(jnp.int32, x.shape, 1)
    return jnp.where(lane < NOPE + ROPE // 2, pltpu.roll(x, LANES - ROPE // 2, 1), pltpu.roll(x, ROPE // 2, 1))


def _mla_rows(t):
    return _tile(t, 512)


def _mla_pre_fwd(proj, pos, fr, sg, q_lat_g, kv_lat_g, qn_g, kn_g, wuq, wukv):
    t = proj.shape[0]
    tr = _mla_rows(t)

    def body(cq_ref, ckv_ref, ms_ref, pos_ref, fr_ref, sg_ref, qlg, kvlg, qng, kng, wuq_ref, wukv_ref, q_out, k_out, v_out):
        lane = lax.broadcasted_iota(jnp.int32, (tr, HP), 1)
        real = jnp.where(lane < MQK, 1.0, 0.0)
        cos, sin = _rope_tables(pos_ref, fr_ref, sg_ref)
        cos = cos * real
        cq = cq_ref[...].astype(F32)
        cqn = (cq * lax.rsqrt(jnp.mean(cq * cq, axis=1, keepdims=True) + EPS) * qlg[...]).astype(BF16)
        ckv = ckv_ref[...].astype(F32)
        ckvn = (ckv * lax.rsqrt(jnp.mean(ckv * ckv, axis=1, keepdims=True) + EPS) * kvlg[...]).astype(BF16)
        kpe = jnp.where((lane >= NOPE) & (lane < MQK), ms_ref[...].astype(F32), 0.0)
        kpe = kpe + jnp.where(lane < MQK + ROPE // 2, pltpu.roll(kpe, ROPE, 1), 0.0)
        lane_all = lax.broadcasted_iota(jnp.int32, (tr, MH * HP), 1)
        v_out[...] = jnp.where(lane_all % HP == MV, 1.0, _dot(ckvn, wukv_ref[:, MH * HP:], NN)).astype(BF16)

        def norm_rope(x, gain):
            xn = x * lax.rsqrt(jnp.sum(x * x * real, axis=1, keepdims=True) * (1.0 / MQK) + EPS) * gain
            return (xn * cos + pltpu.roll(xn, LANES - ROPE // 2, 1) * sin).astype(BF16)

        for h in range(MH):
            cols = slice(h * HP, (h + 1) * HP)
            q_out[:, cols] = norm_rope(_dot(cqn, wuq_ref[:, cols], NN), qng[...])
            k_out[:, cols] = norm_rope(_dot(ckvn, wukv_ref[:, cols], NN) + kpe, kng[...])

    def full(a):
        return pl.BlockSpec(a.shape, lambda i: (0, 0))

    wide = pl.BlockSpec((tr, MH * HP), lambda i: (i, 0))
    return pl.pallas_call(
        body, name="mla_pre_fwd", grid=(t // tr,),
        in_specs=[pl.BlockSpec((tr, MQR), lambda i: (i, 20)), pl.BlockSpec((tr, MKVR), lambda i: (i, 42)),
                  pl.BlockSpec((tr, LANES), lambda i: (i, 43)), pl.BlockSpec((tr, 1), lambda i: (i, 0)),
                  full(fr), full(sg), full(q_lat_g), full(kv_lat_g), full(qn_g), full(kn_g), full(wuq), full(wukv)],
        out_specs=[wide, wide, wide],
        out_shape=[jax.ShapeDtypeStruct((t, MH * HP), BF16)] * 3,
        compiler_params=_cparams(("arbitrary",)),
    )(proj, proj, proj, pos, fr, sg, q_lat_g, kv_lat_g, qn_g, kn_g, wuq, wukv)


def _mla_pre_bwd(dq2, dk2, dv2, dmisc_gla, proj, pos, fr, sg, q_lat_g, kv_lat_g, qn_g, kn_g, wuq, wukv):
    t = proj.shape[0]
    tr = _mla_rows(t)

    def body(dq_ref, dk_ref, dv_ref, dmg_ref, cq_ref, ckv_ref, ms_ref, pos_ref, fr_ref, sg_ref, qlg, kvlg, qng, kng,
             wuq_ref, wukv_ref, dcq_ref, dckv_ref, dms_ref, dwuq, dwukv, acc, dqf, dkvf):
        @pl.when(pl.program_id(0) == 0)
        def _():
            dwuq[...] = jnp.zeros_like(dwuq)
            dwukv[...] = jnp.zeros_like(dwukv)
            acc[...] = jnp.zeros_like(acc)

        cos, sin = _rope_tables(pos_ref, fr_ref, sg_ref)
        cq = cq_ref[...].astype(F32)
        rc = lax.rsqrt(jnp.mean(cq * cq, axis=1, keepdims=True) + EPS)
        xc = cq * rc
        cqn = (xc * qlg[...]).astype(BF16)
        ckv = ckv_ref[...].astype(F32)
        rkv = lax.rsqrt(jnp.mean(ckv * ckv, axis=1, keepdims=True) + EPS)
        xkv = ckv * rkv
        ckvn = (xkv * kvlg[...]).astype(BF16)
        lane = lax.broadcasted_iota(jnp.int32, (tr, HP), 1)
        is_rope = (lane >= NOPE) & (lane < MQK)
        kpe = jnp.where(is_rope, ms_ref[...].astype(F32), 0.0)
        dkpe = jnp.zeros((tr, HP), F32)
        dqng = jnp.zeros((1, HP), F32)
        dkng = jnp.zeros((1, HP), F32)
        for h in range(MH):
            cols = slice(h * HP, (h + 1) * HP)
            qh = _dot(cqn, wuq_ref[:, cols], NN)
            rq = lax.rsqrt(jnp.sum(qh * qh, axis=1, keepdims=True) * (1.0 / MQK) + EPS)
            xq = qh * rq
            dy = dq_ref[:, cols].astype(F32)
            dqn = dy * cos - _partner(dy) * sin
            dqng += jnp.sum(dqn * xq, axis=0, keepdims=True)
            tq = dqn * qng[...]
            dqf[:, cols] = (rq * (tq - xq * (jnp.sum(tq * xq, axis=1, keepdims=True) * (1.0 / MQK)))).astype(BF16)
            kh = _dot(ckvn, wukv_ref[:, cols], NN) + kpe
            rk = lax.rsqrt(jnp.sum(kh * kh, axis=1, keepdims=True) * (1.0 / MQK) + EPS)
            xk = kh * rk
            dy = dk_ref[:, cols].astype(F32)
            dkn = dy * cos - _partner(dy) * sin
            dkng += jnp.sum(dkn * xk, axis=0, keepdims=True)
            tk = dkn * kng[...]
            dkh = rk * (tk - xk * (jnp.sum(tk * xk, axis=1, keepdims=True) * (1.0 / MQK)))
            dkvf[:, cols] = jnp.where(lane < NOPE, dkh, 0.0).astype(BF16)
            dkpe += jnp.where(is_rope, dkh, 0.0)
        dkvf[:, MH * HP:] = dv_ref[...]
        acc[2:3, 0:HP] += dqng
        acc[3:4, 0:HP] += dkng
        dms_ref[...] = (dmg_ref[...] + dkpe).astype(BF16)

        dqfv = dqf[...]
        dwuq[...] += _dot(cqn, dqfv, TN)
        dcqn = _dot(dqfv, wuq_ref[...], NT)
        acc[0:1, :] += jnp.sum(dcqn * xc, axis=0, keepdims=True)
        tc = dcqn * qlg[...]
        dcq_ref[...] = (rc * (tc - xc * jnp.mean(tc * xc, axis=1, keepdims=True))).astype(BF16)

        dkvfv = dkvf[...]
        dwukv[...] += _dot(ckvn, dkvfv, TN)
        dckvn = _dot(dkvfv, wukv_ref[...], NT)
        acc[1:2, 0:MKVR] += jnp.sum(dckvn * xkv, axis=0, keepdims=True)
        tkv = dckvn * kvlg[...]
        dckv_ref[...] = (rkv * (tkv - xkv * jnp.mean(tkv * xkv, axis=1, keepdims=True))).astype(BF16)

    def full(a):
        return pl.BlockSpec(a.shape, lambda i: (0, 0))

    wide = pl.BlockSpec((tr, MH * HP), lambda i: (i, 0))
    narrow = pl.BlockSpec((tr, LANES), lambda i: (i, 0))
    return pl.pallas_call(
        body, name="mla_pre_bwd", grid=(t // tr,),
        in_specs=[wide, wide, wide, narrow,
                  pl.BlockSpec((tr, MQR), lambda i: (i, 20)), pl.BlockSpec((tr, MKVR), lambda i: (i, 42)),
                  pl.BlockSpec((tr, LANES), lambda i: (i, 43)), pl.BlockSpec((tr, 1), lambda i: (i, 0)),
                  full(fr), full(sg), full(q_lat_g), full(kv_lat_g), full(qn_g), full(kn_g), full(wuq), full(wukv)],
        out_specs=[pl.BlockSpec((tr, MQR), lambda i: (i, 0)), narrow, narrow,
                   pl.BlockSpec((MQR, MH * HP), lambda i: (0, 0)), pl.BlockSpec((MKVR, 2 * MH * HP), lambda i: (0, 0)),
                   pl.BlockSpec((8, MQR), lambda i: (0, 0))],
        out_shape=[jax.ShapeDtypeStruct((t, MQR), BF16), jax.ShapeDtypeStruct((t, MKVR), BF16),
                   jax.ShapeDtypeStruct((t, LANES), BF16), jax.ShapeDtypeStruct((MQR, MH * HP), F32),
                   jax.ShapeDtypeStruct((MKVR, 2 * MH * HP), F32), jax.ShapeDtypeStruct((8, MQR), F32)],
        scratch_shapes=[pltpu.VMEM((tr, MH * HP), BF16), pltpu.VMEM((tr, 2 * MH * HP), BF16)],
        compiler_params=_cparams(("arbitrary",)),
    )(dq2, dk2, dv2, dmisc_gla, proj, proj, proj, pos, fr, sg, q_lat_g, kv_lat_g, qn_g, kn_g, wuq, wukv)


ATT_FWD_TILES = (1024, 512)
ATT_BWD_TILES = (512, 512)
ATT_HEADS = 4
ATT_FWD_HEADS = 4
NEG = -1e30
LOG2E = 1.4426950408889634


def _att_mask(q0, k0, tq, tk):
    qc = (q0 + lax.broadcasted_iota(jnp.int32, (tq, tk), 0)) // CHUNK
    kc = (k0 + lax.broadcasted_iota(jnp.int32, (tq, tk), 1)) // CHUNK
    return kc <= qc


def _att_tiles(s, tiles):
    return _tile(s, tiles[0]), _tile(s, tiles[1])


def _lanes(x, n):
    return x if n == 1 else jnp.concatenate([x] * n, axis=1)


def _grid_ends(grid):
    i, j = pl.program_id(0), pl.program_id(1)
    return (i == 0) & (j == 0), (i == grid[0] - 1) & (j == grid[1] - 1)


def _attn_fwd(q2, k2, v2, bsz, slab):
    t = q2.shape[0]
    s = t // bsz
    tq, tk = _att_tiles(s, ATT_FWD_TILES)
    nq, groups, n_diag = s // tq, tk // HP, max(tq // tk, 1)
    sub_rows = tq // n_diag
    scale = MQK ** -0.5
    c2 = scale * LOG2E
    heads = range(ATT_FWD_HEADS)

    def body(q_ref, k_ref, v_ref, slab_ref, o_ref, lse_ref, gath_ref, send_sems, recv_sems, local_sem):
        gather = _core_row_gather_copies(slab_ref, gath_ref, send_sems, recv_sems, local_sem)
        first, last = _grid_ends((bsz, MH // ATT_FWD_HEADS))
        pl.when(first)(gather.start)

        def q_loop(qi, carry):
            q0 = pl.multiple_of(qi * tq, tq)
            rows = pl.ds(q0, tq)
            n_full = q0 // tk
            qs = [q_ref[rows, h * HP:(h + 1) * HP] for h in heads]

            def scores(h, kj, sub=None, masked=False):
                k0 = pl.multiple_of(kj * tk, tk)
                qv = qs[h] if sub is None else qs[h][sub * sub_rows:(sub + 1) * sub_rows]
                sc = _dot(qv, k_ref[pl.ds(k0, tk), h * HP:(h + 1) * HP], NT)
                return jnp.where(_att_mask(q0 + sub * sub_rows, k0, sub_rows, tk), sc, NEG) if masked else sc

            def fold(mx, sc):
                for j in range(groups):
                    mx = jnp.maximum(mx, sc[:, j * HP:(j + 1) * HP])
                return mx

            def over_diagonal(vals, step):
                out = []
                for h in heads:
                    blocks = []
                    for r in range(n_diag):
                        v = vals[h][r * sub_rows:(r + 1) * sub_rows]
                        for u in range(r + 1):
                            v = step(v, h, n_full + u, r, u == r)
                        blocks.append(v)
                    out.append(blocks[0] if n_diag == 1 else jnp.concatenate(blocks, axis=0))
                return tuple(out)

            mx = lax.fori_loop(0, n_full, lambda kj, mx: tuple(fold(mx[h], scores(h, kj)) for h in heads),
                               tuple(jnp.full((tq, HP), NEG, F32) for _ in heads))
            mx = over_diagonal(mx, lambda v, h, kj, r, masked: fold(v, scores(h, kj, r, masked)))
            mb = [jnp.broadcast_to(jnp.max(mx[h], axis=1, keepdims=True), (tq, HP)) for h in heads]

            def weighted(h, kj, sub=None, masked=False):
                m = mb[h] if sub is None else mb[h][sub * sub_rows:(sub + 1) * sub_rows]
                p = jnp.exp2((scores(h, kj, sub, masked) - _lanes(m, groups)) * c2)
                k0 = pl.multiple_of(kj * tk, tk)
                return _dot(p.astype(BF16), v_ref[pl.ds(k0, tk), h * HP:(h + 1) * HP], NN)

            acc = lax.fori_loop(0, n_full, lambda kj, acc: tuple(acc[h] + weighted(h, kj) for h in heads),
                                tuple(jnp.zeros((tq, HP), F32) for _ in heads))
            acc = over_diagonal(acc, lambda v, h, kj, r, masked: v + weighted(h, kj, r, masked))
            lane = lax.broadcasted_iota(jnp.int32, (tq, HP), 1)
            for h in heads:
                a = acc[h]
                l = jnp.sum(jnp.where(lane == MV, a, 0.0), axis=1, keepdims=True)
                o_ref[rows, h * HP:(h + 1) * HP] = (a / l).astype(BF16)
                lse_ref[rows, h * HP:(h + 1) * HP] = mb[h] * scale + jnp.log(l)
            return carry

        lax.fori_loop(0, nq, q_loop, 0)

        @pl.when(last)
        def _():
            gather.finish()
            _cross_core_fill(gath_ref, send_sems, recv_sems, 4)

    spec = pl.BlockSpec((s, ATT_FWD_HEADS * HP), lambda b, h: (b, h))
    return pl.pallas_call(
        body, name="attn_fwd", grid=(bsz, MH // ATT_FWD_HEADS), in_specs=[spec] * 3 + [HBM_SPEC],
        out_specs=[spec, spec, HBM_SPEC],
        out_shape=[jax.ShapeDtypeStruct((t, MH * HP), BF16), jax.ShapeDtypeStruct((t, MH * HP), F32),
                   jax.ShapeDtypeStruct((N_DEV,) + slab.shape, slab.dtype)],
        scratch_shapes=EXCHANGE_SEMS, compiler_params=_cparams(("arbitrary", "arbitrary")),
    )(q2, k2, v2, slab)


def _attn_bwd(q2, k2, v2, do2, o2, lse2, bsz, tsum):
    t = q2.shape[0]
    s = t // bsz
    tq, tk = _att_tiles(s, ATT_BWD_TILES)
    nq, nk, per, groups = s // tq, s // tk, max(tk // tq, 1), tk // HP
    scale = MQK ** -0.5
    c2 = scale * LOG2E
    heads = range(ATT_HEADS)

    def body(q_ref, k_ref, v_ref, do_ref, o_ref, lse_ref, t_ref, dq_ref, dk_ref, dv_ref, parts_ref, dq_acc, delta, lse_b2,
             send_sems, recv_sems, local_sem):
        exchange = _all_to_all_copies(t_ref, parts_ref, send_sems, recv_sems, local_sem)
        first, last = _grid_ends((bsz, MH // ATT_HEADS))
        pl.when(first)(exchange.start)
        dq_acc[...] = jnp.zeros_like(dq_acc)

        def d_loop(i, carry):
            rows = pl.ds(pl.multiple_of(i * tq, tq), tq)
            for h in heads:
                hs = slice(h * HP, (h + 1) * HP)
                dl = jnp.sum(do_ref[rows, hs].astype(F32) * o_ref[rows, hs].astype(F32), axis=1, keepdims=True)
                delta[rows, hs] = jnp.broadcast_to(dl, (tq, HP))
            lse_b2[rows, :] = lse_ref[rows, :] * LOG2E
            return carry

        lax.fori_loop(0, nq, d_loop, 0)

        def k_loop(kj, carry):
            k0 = pl.multiple_of(kj * tk, tk)
            kk = [k_ref[pl.ds(k0, tk), h * HP:(h + 1) * HP] for h in heads]
            vv = [v_ref[pl.ds(k0, tk), h * HP:(h + 1) * HP] for h in heads]

            def tile(qi, c, masked):
                q0 = pl.multiple_of(qi * tq, tq)
                rows = pl.ds(q0, tq)
                out = []
                for h in heads:
                    hs = slice(h * HP, (h + 1) * HP)
                    dk, dv = c[h]
                    q = q_ref[rows, hs]
                    do = do_ref[rows, hs]
                    e = _dot(q, kk[h], NT) * c2 - _lanes(lse_b2[rows, hs], groups)
                    if masked:
                        e = jnp.where(_att_mask(q0, k0, tq, tk), e, NEG)
                    p = jnp.exp2(e)
                    dv = dv + _dot(p.astype(BF16), do, TN)
                    ds = (p * (_dot(do, vv[h], NT) - _lanes(delta[rows, hs], groups))).astype(BF16)
                    dq_acc[rows, hs] += _dot(ds, kk[h], NN)
                    dk = dk + _dot(ds, q, TN)
                    out.append((dk, dv))
                return tuple(out)

            zero = jnp.zeros((tk, HP), F32)
            c = tuple((zero, zero) for _ in heads)
            first = k0 // tq
            for u in range(per):
                c = tile(first + u, c, True)
            c = lax.fori_loop(first + per, nq, lambda qi, c: tile(qi, c, False), c)
            for h in heads:
                dk_ref[pl.ds(k0, tk), h * HP:(h + 1) * HP] = (c[h][0] * scale).astype(BF16)
                dv_ref[pl.ds(k0, tk), h * HP:(h + 1) * HP] = c[h][1].astype(BF16)
            return carry

        lax.fori_loop(0, nk, k_loop, 0)
        dq_ref[...] = (dq_acc[...] * scale).astype(BF16)
        pl.when(last)(exchange.finish)

    spec = pl.BlockSpec((s, ATT_HEADS * HP), lambda b, h: (b, h))
    return pl.pallas_call(
        body, name="attn_bwd", grid=(bsz, MH // ATT_HEADS), in_specs=[spec] * 6 + [HBM_SPEC],
        out_specs=[spec] * 3 + [HBM_SPEC],
        out_shape=[jax.ShapeDtypeStruct((t, MH * HP), BF16)] * 3 + [jax.ShapeDtypeStruct(tsum.shape, tsum.dtype)],
        scratch_shapes=[pltpu.VMEM((s, ATT_HEADS * HP), F32)] * 3 + EXCHANGE_SEMS,
        compiler_params=_cparams(("arbitrary", "arbitrary")),
    )(q2, k2, v2, do2, o2, lse2, tsum)


def _perm_w_in_t(w):
    z = lambda n: jnp.zeros((n, w.shape[1]), w.dtype)
    return jnp.concatenate([w[:3072], w[3504:5552], w[3088:3344], w[3344:3472], w[3072:3088], z(48), w[3472:3504], z(32)],
                           axis=0)


def _unperm_w_in_t(g):
    return jnp.concatenate([g[:3072], g[5504:5520], g[5120:5376], g[5376:5504], g[5568:5600], g[3072:5120]], axis=0)


def _pad_wa(w):
    return jnp.pad(w, ((0, LANES - GLR), (0, 0)))


def _pad_wuq(w):
    return jnp.pad(w.reshape(MQR, MH, MQK), ((0, 0), (0, 0), (0, HP - MQK))).reshape(MQR, MH * HP)


def _unpad_wuq(g):
    return g.reshape(MQR, MH, HP)[:, :, :MQK].reshape(MQR, MH * MQK)


def _pad_wukv(w):
    w3 = w.reshape(MKVR, MH, NOPE + MV)
    kp = jnp.pad(w3[:, :, :NOPE], ((0, 0), (0, 0), (0, HP - NOPE))).reshape(MKVR, MH * HP)
    vp = jnp.pad(w3[:, :, NOPE:], ((0, 0), (0, 0), (0, HP - MV))).reshape(MKVR, MH * HP)
    return jnp.concatenate([kp, vp], axis=1)


def _unpad_wukv(g):
    kp = g[:, :MH * HP].reshape(MKVR, MH, HP)[:, :, :NOPE]
    vp = g[:, MH * HP:].reshape(MKVR, MH, HP)[:, :, :MV]
    return jnp.concatenate([kp, vp], axis=2).reshape(MKVR, MH * (NOPE + MV))


def _pad_wo(w):
    return jnp.pad(w.reshape(MH, MV, D), ((0, 0), (0, HP - MV), (0, 0))).reshape(MH * HP, D)


def _unpad_wo(g):
    return g.reshape(MH, HP, D)[:, :MV, :].reshape(MH * MV, D)


def _repeat_half(a):
    a3 = a.reshape(a.shape[0], -1, HP)
    a3 = jnp.concatenate([a3[:, :, :MQK], a3[:, :, NOPE:NOPE + ROPE // 2], a3[:, :, MQK + ROPE // 2:]], axis=2)
    return a3.reshape(a.shape)


def _pad_lanes(v, n=HP):
    return jnp.pad(v, ((0, 0), (0, n - v.shape[1])))


def _local_step(x, positions, tgt, wt, slab_b, sp, mod3):
    bsz, s, _ = x.shape
    t = bsz * s
    x2 = x.reshape(t, D)
    tgt2 = tgt.reshape(t, D)
    pos = positions.reshape(t, 1)
    fr16 = ROPE_THETA ** (-jnp.arange(0, ROPE, 2, dtype=F32) / ROPE)
    zero = lambda n: jnp.zeros((n,), F32)
    fr = jnp.concatenate([zero(NOPE), fr16, fr16, zero(HP - MQK)]).reshape(1, HP)
    sg = jnp.concatenate([zero(NOPE), -jnp.ones((ROPE // 2,), F32), jnp.ones((ROPE // 2,), F32), zero(HP - MQK)]).reshape(1, HP)

    w_in_t = _perm_w_in_t(wt["w_in"])
    wa_pad = _pad_wa(wt["gla_w_alpha"].T)
    wuq = _pad_wuq(wt["mla_w_uq"].T)
    wukv = _pad_wukv(wt["mla_w_ukv"].T)
    qn_g, kn_g = _pad_lanes(sp["mla_qn_g"]), _pad_lanes(sp["mla_kn_g"])

    tm = _tile(s, 1024)
    h = _norm_mod_fwd(x2, sp["norm1_g"], mod3, 0, 1, "norm1_fwd")
    proj = _mm(h, w_in_t, "nt", (BF16,), "proj_fwd", tn=PW // 4)
    o_gla, og, states = _gla_fwd(proj, wa_pad, sp["gla_b_alpha"], sp["gla_out_norm_g"], bsz)
    q2, k2, v2 = _mla_pre_fwd(proj, pos, fr, sg, sp["mla_q_lat_g"], sp["mla_kv_lat_g"], _repeat_half(qn_g), _repeat_half(kn_g),
                              _repeat_half(wuq), wukv)
    o2, lse2, core_row = _attn_fwd(q2, k2, v2, bsz, slab_b)
    wt = dict(wt, **_unpack_gathered(core_row, SLAB_B))
    wo_pad = _pad_wo(wt["mla_w_o"])
    y_a = _mm(og, wt["gla_w_o"], "nn", (BF16,), "gla_out_fwd")
    y_b = _mm(o2, wo_pad, "nn", (BF16,), "mla_out_fwd")
    mix = _merge_fwd(proj, sp["b_merge"], y_a, y_b)
    mixed, x1, h2 = _mm(mix, wt["w_out"], "nn", (), "w_out_fwd", tm=tm,
                        tail=_norm_fwd_tail(x2, sp["norm2_g"], mod3, 3, 4, 2, tm))
    a, f = _mm(h2, wt["mlp_w1"], "nt", (BF16, BF16), "mlp1_fwd",
               epi=lambda acc: (acc, jnp.square(jnp.maximum(acc, 0.0))))
    dy, dff, acc_g2, acc_loss = _mm(f, wt["mlp_w2"], "nn", (), "mlp2_fwd", tm=tm, tail=_loss_tail(x1, tgt2, mod3, tm))

    gw = {}
    gw["mlp_w2"] = _mm(f, dff, "tn", (BF16,), "mlp2_dw")
    da = _mm(dff, wt["mlp_w2"], "nt", (BF16,), "mlp2_dx", extras=(a,),
             epi=lambda acc, av: (acc * (2.0 * jnp.maximum(av.astype(F32), 0.0)),))
    gw["mlp_w1"] = _mm(da, h2, "tn", (BF16,), "mlp1_dw")
    dx1, dmixed, accb2, accg2 = _mm(da, wt["mlp_w1"], "nn", (), "mlp1_dx", tm=tm,
                                    tail=_norm_bwd_tail(x1, dy, sp["norm2_g"], mod3, 4, tm, mixed=mixed, i_gate=2))

    gw["w_out"] = _mm(mix, dmixed, "tn", (BF16,), "w_out_dw")
    dy_a, dy_b, dlogits, acc_bm = _mm(dmixed, wt["w_out"], "nt", (), "w_out_dx", tm=tm,
                                      tail=_merge_bwd_tail(proj, sp["b_merge"], y_a, y_b, tm))
    gw["gla_w_o"] = _mm(og, dy_a, "tn", (BF16,), "gla_out_dw")
    dog = _mm(dy_a, wt["gla_w_o"], "nt", (BF16,), "gla_out_dx")
    gw["mla_w_o"] = _unpad_wo(_mm(o2, dy_b, "tn", (BF16,), "mla_out_dw"))
    do2 = _mm(dy_b, wo_pad, "nt", (BF16,), "mla_out_dx")
    dq2, dk2, dv2, parts_b = _attn_bwd(q2, k2, v2, do2, o2, lse2, bsz, _pack_per_device(gw, SLAB_B))
    dq_g, dk_g, dv_g, dgg, dz, acc_ba, acc_go = _gla_bwd(dog, o_gla, proj, wa_pad, sp["gla_b_alpha"],
                                                         sp["gla_out_norm_g"], states, bsz)
    gw["gla_w_alpha"] = _mm(proj, dz, "tn", (F32,), "gla_alpha_dw", a_off=43, m=LANES)[:GLR].T.astype(BF16)
    dmisc_gla = _mm(dz, wa_pad, "nt", (F32,), "gla_alpha_dx")
    dcq, dckv, dmisc, gwuq, gwukv, acc_mla = _mla_pre_bwd(dq2, dk2, dv2, dmisc_gla, proj, pos, fr, sg, sp["mla_q_lat_g"],
                                                         sp["mla_kv_lat_g"], qn_g, kn_g, wuq, wukv)
    gw["mla_w_uq"] = _unpad_wuq(gwuq).T.astype(BF16)
    gw["mla_w_ukv"] = _unpad_wukv(gwukv).T.astype(BF16)
    dproj = jnp.concatenate([dq_g, dk_g, dv_g, dgg, dlogits, dcq, dckv, dmisc], axis=1)
    gw["w_in"] = _unperm_w_in_t(_mm(dproj, h, "tn", (BF16,), "proj_dw", tm=PW // 4))
    parts_a, grad_x, accb1, accg1 = _mm(dproj, w_in_t, "nn", (), "proj_dx", tm=tm, tk=PW // 4,
                                        cargo=_sum_over_cores(_pack_per_device(gw, SLAB_A), "a"),
                                        tail=_norm_bwd_tail(x2, dx1, sp["norm1_g"], mod3, 1, tm))

    dmod = jnp.stack([accb1[:, 0], accb1[:, 1], accb2[:, 2], accb2[:, 0], accb2[:, 1], acc_g2[:, 0]], axis=1)

    rows = {
        "dmod": dmod.reshape(bsz * 6, D),
        "norm1_g": accg1[0:1], "norm2_g": accg2[0:1],
        "b_merge": acc_bm[0:1].reshape(2, D),
        "gla_b_alpha": _pad_lanes(acc_ba[0:1], D),
        "gla_out_norm_g": _pad_lanes(acc_go.reshape(GH, 8, GDV)[:, 0, :], D),
        "mla_q_lat_g": _pad_lanes(acc_mla[0:1], D), "mla_kv_lat_g": _pad_lanes(acc_mla[1:2], D),
        "mla_qn_g": _pad_lanes(acc_mla[2:3], D), "mla_kn_g": _pad_lanes(acc_mla[3:4], D),
        "loss": acc_loss[0:1],
    }
    return grad_x.reshape(bsz, s, D), parts_a, parts_b, rows


HBM_SPEC = pl.BlockSpec(memory_space=pltpu.HBM)


def _all_gather(ps, name):
    n = len(ps)

    def body(*refs):
        p_refs, out_refs, (send_sems, recv_sems, local_sems) = refs[:n], refs[n:2 * n], refs[2 * n:]
        x, y, c = lax.axis_index("x"), lax.axis_index("y"), lax.axis_index("c")
        me, sibling = (x, y, c), (x, y, 1 - c)
        chips = [(1 - x, y), (x, 1 - y), (1 - x, 1 - y)]

        def copy(a, k, block, to, own=False):
            slot = out_refs[a].at[4 * block[0] + 2 * block[1] + block[2]]
            return pltpu.make_async_remote_copy(
                src_ref=p_refs[a] if own else slot, dst_ref=slot, send_sem=send_sems.at[7 * a + k],
                recv_sem=recv_sems.at[7 * a + k], device_id=to, device_id_type=MESH)

        mine = [pltpu.make_async_copy(p_refs[a], out_refs[a].at[4 * x + 2 * y + c], local_sems.at[a]) for a in range(n)]
        first = [copy(a, 0, me, sibling, own=True) for a in range(n)]
        first += [copy(a, 1 + j, me, (*chip, c), own=True) for a in range(n) for j, chip in enumerate(chips)]
        for cp in mine + first:
            cp.start()
        passed = []
        for j, chip in enumerate(chips):
            for a in range(n):
                copy(a, 1 + j, (*chip, c), me).wait_recv()
                passed.append(copy(a, 4 + j, (*chip, c), sibling))
                passed[-1].start()
        for a in range(n):
            copy(a, 0, sibling, me).wait_recv()
            for j, chip in enumerate(chips):
                copy(a, 4 + j, (*chip, 1 - c), me).wait_recv()
        for cp in first + passed:
            cp.wait_send()
        for cp in mine:
            cp.wait()

    return pl.pallas_call(
        body, name=name, out_shape=[jax.ShapeDtypeStruct((N_DEV,) + p.shape, p.dtype) for p in ps],
        in_specs=[HBM_SPEC] * n, out_specs=[HBM_SPEC] * n,
        scratch_shapes=[pltpu.SemaphoreType.DMA((7 * n,)), pltpu.SemaphoreType.DMA((7 * n,)), pltpu.SemaphoreType.DMA((n,))],
    )(*ps)


def _sibling_exchange(g, name):
    def body(g_ref, out_ref, send_sem, recv_sem):
        x, y, c = lax.axis_index("x"), lax.axis_index("y"), lax.axis_index("c")
        cp = pltpu.make_async_remote_copy(src_ref=g_ref, dst_ref=out_ref, send_sem=send_sem, recv_sem=recv_sem,
                                          device_id=(x, y, 1 - c), device_id_type=MESH)
        cp.start()
        cp.wait()

    return pl.pallas_call(
        body, name=name, out_shape=jax.ShapeDtypeStruct(g.shape, g.dtype),
        in_specs=[HBM_SPEC], out_specs=HBM_SPEC,
        scratch_shapes=[pltpu.SemaphoreType.DMA(()), pltpu.SemaphoreType.DMA(())],
    )(g)


class _Exchange:
    def __init__(self, local, sends, arrivals):
        self.local, self.sends, self.arrivals = local, sends, arrivals

    def start(self):
        self.local.start()
        for cp in self.sends:
            cp.start()

    def finish(self):
        for cp in self.arrivals:
            cp.wait_recv()
        for cp in self.sends:
            cp.wait_send()
        self.local.wait()


EXCHANGE_SEMS = [pltpu.SemaphoreType.DMA((N_DEV,)), pltpu.SemaphoreType.DMA((N_DEV,)), pltpu.SemaphoreType.DMA(())]


def _all_to_all_copies(t_ref, out_ref, send_sems, recv_sems, local_sem):
    x, y, c = lax.axis_index("x"), lax.axis_index("y"), lax.axis_index("c")
    me = 4 * x + 2 * y + c

    def copy(k, src, dst):
        px, py, pc = (1 - x if k & 4 else x), (1 - y if k & 2 else y), (1 - c if k & 1 else c)
        peer = 4 * px + 2 * py + pc
        return pltpu.make_async_remote_copy(src_ref=t_ref.at[peer if src is None else src],
                                            dst_ref=out_ref.at[peer if dst is None else dst], send_sem=send_sems.at[k],
                                            recv_sem=recv_sems.at[k], device_id=(px, py, pc), device_id_type=MESH)

    return _Exchange(pltpu.make_async_copy(t_ref.at[me], out_ref.at[me], local_sem),
                     [copy(k, None, me) for k in range(1, N_DEV)], [copy(k, me, None) for k in range(1, N_DEV)])


def _chip_exchange_copies(t_ref, out_ref, send_sems, recv_sems, local_sem):
    x, y, c = lax.axis_index("x"), lax.axis_index("y"), lax.axis_index("c")
    my_chip = 2 * x + y
    chips = [(1 - x, y), (x, 1 - y), (1 - x, 1 - y)]

    def copy(j, src, dst, px, py):
        return pltpu.make_async_remote_copy(src_ref=t_ref.at[src], dst_ref=out_ref.at[dst], send_sem=send_sems.at[j],
                                            recv_sem=recv_sems.at[j], device_id=(px, py, c), device_id_type=MESH)

    return _Exchange(pltpu.make_async_copy(t_ref.at[my_chip], out_ref.at[my_chip], local_sem),
                     [copy(j, 2 * px + py, my_chip, px, py) for j, (px, py) in enumerate(chips)],
                     [copy(j, my_chip, 2 * px + py, px, py) for j, (px, py) in enumerate(chips)])


def _core_row_gather_copies(p_ref, out_ref, send_sems, recv_sems, local_sem):
    x, y, c = lax.axis_index("x"), lax.axis_index("y"), lax.axis_index("c")
    peers = [(x, y, 1 - c), (1 - x, y, c), (x, 1 - y, c), (1 - x, 1 - y, c)]

    def slot(px, py, pc):
        return out_ref.at[4 * px + 2 * py + pc]

    def copy(j, block, to):
        return pltpu.make_async_remote_copy(src_ref=p_ref, dst_ref=slot(*block), send_sem=send_sems.at[j],
                                            recv_sem=recv_sems.at[j], device_id=to, device_id_type=MESH)

    return _Exchange(pltpu.make_async_copy(p_ref, slot(x, y, c), local_sem),
                     [copy(j, (x, y, c), peer) for j, peer in enumerate(peers)],
                     [copy(j, peer, peer) for j, peer in enumerate(peers)])


def _cross_core_fill(g_ref, send_sems, recv_sems, base):
    x, y, c = lax.axis_index("x"), lax.axis_index("y"), lax.axis_index("c")
    chips = [(1 - x, y), (x, 1 - y), (1 - x, 1 - y)]

    def copy(j, pc):
        px, py = chips[j]
        slot = g_ref.at[4 * px + 2 * py + pc]
        return pltpu.make_async_remote_copy(src_ref=slot, dst_ref=slot, send_sem=send_sems.at[base + j],
                                            recv_sem=recv_sems.at[base + j], device_id=(x, y, 1 - c), device_id_type=MESH)

    sends = [copy(j, c) for j in range(3)]
    for cp in sends:
        cp.start()
    for j in range(3):
        copy(j, 1 - c).wait_recv()
    for cp in sends:
        cp.wait_send()


def _slab_block(r):
    return max(b for b in range(16, SLAB_BLOCK_MAX + 1, 16) if r % b == 0)


def _pair_sum(a, b, name):
    n, r, cdim = a.shape
    rb = _slab_block(r)
    blk = pl.BlockSpec((1, rb, cdim), lambda j, i: (j, i, 0))

    def body(a_ref, b_ref, o_ref):
        o_ref[...] = (a_ref[...].astype(F32) + b_ref[...].astype(F32)).astype(BF16)

    return pl.pallas_call(
        body, name=name, grid=(n, r // rb), in_specs=[blk, blk], out_specs=blk,
        out_shape=jax.ShapeDtypeStruct(a.shape, BF16), compiler_params=_cparams(("arbitrary", "arbitrary")),
    )(a, b)


def _adamw_math(w, g, m, v):
    m = ADAM_B1 * m + (1.0 - ADAM_B1) * g
    v = ADAM_B2 * v + (1.0 - ADAM_B2) * jnp.square(g)
    m_hat = m / (1.0 - ADAM_B1 ** ADAM_STEP)
    v_hat = v / (1.0 - ADAM_B2 ** ADAM_STEP)
    delta = -ADAM_LR * (m_hat / (jnp.sqrt(v_hat) + ADAM_EPS) + ADAM_WD * w)
    return delta, m, v


def _slab_sum(parts, name):
    n, r, cdim = parts.shape
    rb = _slab_block(r)
    blk = pl.BlockSpec((rb, cdim), lambda i: (i, 0))

    def body(p_ref, g_out):
        g = p_ref[0].astype(F32)
        for j in range(1, n):
            g = g + p_ref[j].astype(F32)
        g_out[...] = g

    return pl.pallas_call(
        body, name=name, grid=(r // rb,),
        in_specs=[pl.BlockSpec((n, rb, cdim), lambda i: (0, i, 0))], out_specs=blk,
        out_shape=jax.ShapeDtypeStruct((r, cdim), F32), compiler_params=_cparams(("arbitrary",)),
    )(parts)


def _adamw(g, w, m, v, name):
    r, cdim = w.shape
    rb = _tile(r, 256)
    blk = pl.BlockSpec((rb, cdim), lambda i: (i, 0))

    def body(g_ref, w_ref, m_ref, v_ref, d_out, m_out, v_out):
        d_out[...], m_out[...], v_out[...] = _adamw_math(w_ref[...], g_ref[...], m_ref[...], v_ref[...])

    return pl.pallas_call(
        body, name=name, grid=(r // rb,), in_specs=[blk] * 4, out_specs=[blk] * 3,
        out_shape=[jax.ShapeDtypeStruct((r, cdim), F32)] * 3, compiler_params=_cparams(("arbitrary",)),
    )(g, w, m, v)


def _adamw_small(parts, w, m, v):
    def body(p_ref, w_ref, m_ref, v_ref, g_out, d_out, m_out, v_out, loss_out):
        def total(srcs):
            acc = None
            for r in srcs:
                for j in range(N_DEV):
                    term = p_ref[j, r:r + 1, :]
                    acc = term if acc is None else acc + term
            return acc

        for prow, srcs in enumerate(SMALL_SOURCES):
            one = slice(prow, prow + 1)
            g = total(srcs)
            g_out[one, :] = g
            d_out[one, :], m_out[one, :], v_out[one, :] = _adamw_math(w_ref[one, :], g, m_ref[one, :], v_ref[one, :])
        loss_out[...] = jnp.broadcast_to(jnp.sum(total(LOSS_SOURCE), axis=1, keepdims=True), (8, LANES))

    full = lambda shp: pl.BlockSpec(shp, lambda i: (0,) * len(shp))
    return pl.pallas_call(
        body, name="adamw_small", grid=(1,),
        in_specs=[full((N_DEV, SMALL_ROWS, D)), full((16, D)), full((16, D)), full((16, D))],
        out_specs=[full((16, D))] * 4 + [full((8, LANES))],
        out_shape=[jax.ShapeDtypeStruct((16, D), F32)] * 4 + [jax.ShapeDtypeStruct((8, LANES), F32)],
        compiler_params=_cparams(("arbitrary",)),
    )(parts, w, m, v)


def _to_slab(shard, form):
    if form == "N":
        return shard
    return shard.T if form == "T" else shard.T.reshape(-1, D)


def _from_slab(block, form, shard_shape):
    if form == "N":
        return block
    return block.T if form == "T" else block.reshape(shard_shape[1], shard_shape[0]).T


def _gathered_full(g, form, shard_shape):
    if form == "TR":
        return g.reshape(N_DEV * shard_shape[1], shard_shape[0])
    return g.reshape(N_DEV * g.shape[1], D)


def _pack_slab(shards, layout):
    return jnp.concatenate([jnp.zeros((r, D), BF16) if n is None else _to_slab(shards[n], form).astype(BF16)
                            for n, r, form in layout], axis=0)


def _unpack_gathered(gathered, layout):
    out, off = {}, 0
    for n, r, form in layout:
        if n is not None:
            out[n] = _gathered_full(gathered[:, off:off + r], form, SHARD_SHAPES[n])
        off += r
    return out


def _pack_per_device(gw, layout):
    return jnp.concatenate([jnp.zeros((N_DEV, r, D), BF16) if n is None else gw[n].reshape(N_DEV, r, D)
                            for n, r, _ in layout], axis=1)


def _adamw_shards(gslab, layout, wts, mom, var):
    out, off = {}, 0
    for n, r, form in layout:
        if n is not None:
            g = _from_slab(gslab[off:off + r], form, SHARD_SHAPES[n])
            out[n] = (g,) + tuple(_adamw(g, wts[n], mom[n], var[n], "adamw_" + n))
        off += r
    return out


def _sum_over_cores(per_dev, tag):
    my_c = lax.axis_index("c")
    pairs = per_dev.reshape(4, 2, per_dev.shape[1], D)
    keep = lax.dynamic_index_in_dim(pairs, my_c, axis=1, keepdims=False)
    give = lax.dynamic_index_in_dim(pairs, 1 - my_c, axis=1, keepdims=False)
    return _pair_sum(keep, _sibling_exchange(give, "rs_sibling_exchange_" + tag), "rs_pair_sum_" + tag)


def _small_pack(vals):
    rows = []
    for n in SMALL:
        v = vals[n].reshape(-1)
        k = -(-v.shape[0] // D)
        rows.append(jnp.pad(v, (0, k * D - v.shape[0])).reshape(k, D))
    return jnp.concatenate(rows, axis=0)


def _small_unpack(packed, shapes):
    out = {}
    for n in SMALL:
        k = shapes[n][-1]
        r0 = SMALL_ROW[n]
        out[n] = packed[r0:r0 + -(-k // D)].reshape(-1)[:k].reshape(shapes[n])
    return out


def kernel(x, c, positions, w_ada, b_ada, norm1_g, w_in, b_merge, gla_w_alpha, gla_b_alpha, gla_out_norm_g, gla_w_o, mla_q_lat_g, mla_w_uq, mla_kv_lat_g, mla_w_ukv, mla_qn_g, mla_kn_g, mla_w_o, w_out, norm2_g, mlp_w1, mlp_w2, loss_target, m_w_ada, m_b_ada, m_norm1_g, m_w_in, m_b_merge, m_gla_w_alpha, m_gla_b_alpha, m_gla_out_norm_g, m_gla_w_o, m_mla_q_lat_g, m_mla_w_uq, m_mla_kv_lat_g, m_mla_w_ukv, m_mla_qn_g, m_mla_kn_g, m_mla_w_o, m_w_out, m_norm2_g, m_mlp_w1, m_mlp_w2, v_w_ada, v_b_ada, v_norm1_g, v_w_in, v_b_merge, v_gla_w_alpha, v_gla_b_alpha, v_gla_out_norm_g, v_gla_w_o, v_mla_q_lat_g, v_mla_w_uq, v_mla_kv_lat_g, v_mla_w_ukv, v_mla_qn_g, v_mla_kn_g, v_mla_w_o, v_w_out, v_norm2_g, v_mlp_w1, v_mlp_w2):
    args = dict(locals())
    wts = {n: args[n][0] for n in WEIGHTS}
    mom = {n: args["m_" + n][0] for n in WEIGHTS}
    var = {n: args["v_" + n][0] for n in WEIGHTS}
    my_c = lax.axis_index("c")
    my_dev = 4 * lax.axis_index("x") + 2 * lax.axis_index("y") + my_c
    bsz = x.shape[0]
    sp = {n: wts[n].reshape(1, -1) for n in SMALL}

    gathered_a, c_gathered = _all_gather([_pack_slab(wts, SLAB_A), jnp.pad(c, ((0, 8 - bsz), (0, 0)))], "weights_all_gather")
    wt = _unpack_gathered(gathered_a, SLAB_A)

    c_all = c_gathered[:, :bsz].reshape(N_DEV * bsz, D)
    bias = lax.dynamic_slice_in_dim(sp["b_ada"], my_dev * ADA_COLS, ADA_COLS, axis=1)
    mod_cols = _mm(c_all, wts["w_ada"], "nn", (F32,), "ada_fwd", pro=_silu, epi=lambda acc, b: (acc + b,),
                   extras=(jnp.broadcast_to(bias, (N_DEV * bsz, ADA_COLS)),))
    mod_all, = _all_gather([mod_cols], "mod_all_gather")
    mod_mine = lax.dynamic_slice_in_dim(mod_all, my_dev * bsz, bsz, axis=1)
    mod3 = jnp.transpose(mod_mine, (1, 0, 2)).reshape(bsz, 6, D)

    grad_x, parts_a, parts_b, rows = _local_step(x, positions, loss_target, wt, _pack_slab(wts, SLAB_B), sp, mod3)

    big = dict(_adamw_shards(_slab_sum(parts_a, "rs_slab_sum_a"), SLAB_A, wts, mom, var),
               **_adamw_shards(_slab_sum(parts_b, "rs_slab_sum_b"), SLAB_B, wts, mom, var))

    order = ["dmod", "norm1_g", "norm2_g", "b_merge", "gla_b_alpha", "gla_out_norm_g", "mla_q_lat_g", "mla_kv_lat_g",
             "mla_qn_g", "mla_kn_g", "loss"]
    part_rows = jnp.concatenate([rows[n] for n in order], axis=0)
    part_rows = jnp.pad(part_rows, ((0, SMALL_ROWS - part_rows.shape[0]), (0, 0)))
    all_rows, = _all_gather([part_rows], "partials_all_gather")

    dmod_all = all_rows[:, :6 * bsz].reshape(N_DEV * bsz, 6 * D)
    dmod_cols = lax.dynamic_slice_in_dim(dmod_all, my_dev * ADA_COLS, ADA_COLS, axis=1)
    g_ada = _mm(c_all, dmod_cols, "tn", (F32,), "ada_dw", pro=_silu)
    big["w_ada"] = (g_ada,) + tuple(_adamw(g_ada, wts["w_ada"], mom["w_ada"], var["w_ada"], "adamw_w_ada"))

    small = _adamw_small(all_rows, _small_pack({n: wts[n] for n in SMALL}), _small_pack({n: mom[n] for n in SMALL}),
                         _small_pack({n: var[n] for n in SMALL}))
    loss = small[4][0, 0]
    small_shapes = {n: wts[n].shape for n in SMALL}
    small = [_small_unpack(o, small_shapes) for o in small[:4]]

    outs = [loss, grad_x]
    for k in range(4):
        for n in WEIGHTS:
            val = big[n][k] if n in BIG else small[k][n]
            outs.append(val.reshape((1,) + tuple(wts[n].shape)))
    return tuple(outs)
```
